```python
import math
import jax, jax.numpy as jnp
from jax import lax
import numpy as np

D_MODEL = 2048
BATCH = 8
SEQ = 4096
DEPTH = 2

D_MIX = D_MODEL
N_MIXERS = 4
MIX_W = D_MIX // N_MIXERS
IN_COLS = 6 * MIX_W
S5_GROUP_CH = 16
S5_GROUPS = MIX_W // S5_GROUP_CH
S5_STATE = 64
CONV_WIDTH = 31
LRU_HEADS = 8
LRU_HEAD_DIM = MIX_W // LRU_HEADS
LRU_CONV_WIDTH = 4
LRU_C = 8.0
POOL_WINDOWS = (2, 4, 8, 16)
POOL_GROUP_W = MIX_W // len(POOL_WINDOWS)
FFN_DIM = 5504
FFN_CONV_WIDTH = 3
EPS = 1e-6

kernel_name = "hybrid_parallel_s5_conformer_rglru_pool"


def rmsnorm(x, g):
    xf = x.astype(jnp.float32)
    y = xf * lax.rsqrt(jnp.mean(xf * xf, axis=-1, keepdims=True) + EPS)
    return (y * g.astype(jnp.float32)).astype(x.dtype)


def layernorm(x, g, b):
    xf = x.astype(jnp.float32)
    mu = jnp.mean(xf, axis=-1, keepdims=True)
    var = jnp.mean(jnp.square(xf - mu), axis=-1, keepdims=True)
    y = (xf - mu) * lax.rsqrt(var + EPS)
    return (y * g.astype(jnp.float32) + b.astype(jnp.float32)).astype(x.dtype)


def causal_dwconv(x, w, b):
    k, c = w.shape
    y = lax.conv_general_dilated(
        x, w[:, None, :].astype(x.dtype), window_strides=(1,), padding=((k - 1, 0),),
        dimension_numbers=('NWC', 'WIO', 'NWC'), feature_group_count=c)
    return y + b


def _linear_combine(left, right):
    a_l, b_l = left
    a_r, b_r = right
    return (a_l * a_r, a_r * b_l + b_r)


def s5_mixer(u, lam_re, lam_im, log_step, b_re, b_im, c_re, c_im, d, w_glu, b_glu):
    dt = u.dtype
    bsz, seqlen, _ = u.shape
    uf = u.astype(jnp.float32).reshape(bsz, seqlen, S5_GROUPS, S5_GROUP_CH)
    lam = lax.complex(lam_re.astype(jnp.float32), lam_im.astype(jnp.float32))
    step = jnp.exp(log_step.astype(jnp.float32))[:, None]
    lam_bar = jnp.exp(lam * step)
    bmat = lax.complex(b_re.astype(jnp.float32), b_im.astype(jnp.float32))
    b_bar = ((lam_bar - 1.0) / lam)[..., None] * bmat
    bu = jnp.einsum('blgh,gph->blgp', uf.astype(jnp.complex64), b_bar)
    a = jnp.broadcast_to(lam_bar, bu.shape)
    _, states = lax.associative_scan(_linear_combine, (a, bu), axis=1)
    cmat = lax.complex(c_re.astype(jnp.float32), c_im.astype(jnp.float32))
    y = jnp.einsum('blgp,ghp->blgh', states, cmat).real
    y = y + d.astype(jnp.float32).reshape(S5_GROUPS, S5_GROUP_CH) * uf
    y = y.reshape(bsz, seqlen, MIX_W)
    g = jax.nn.gelu(y, approximate=True)
    out = g * jax.nn.sigmoid(g @ w_glu.astype(jnp.float32) + b_glu.astype(jnp.float32))
    return out.astype(dt)


def conformer_conv_mixer(v, g, w_dw, b_dw, ln_g, ln_b, w_pw, b_pw):
    h = v * jax.nn.sigmoid(g)
    h = causal_dwconv(h, w_dw, b_dw)
    h = layernorm(h, ln_g, ln_b)
    h = jax.nn.silu(h)
    return h @ w_pw + b_pw


def rglru_mixer(xb, gb, w_conv, b_conv, w_r, b_r, w_i, b_i, lam):
    dt = xb.dtype
    bsz, seqlen, _ = xb.shape
    xc = causal_dwconv(xb, w_conv, b_conv)
    xh = xc.reshape(bsz, seqlen, LRU_HEADS, LRU_HEAD_DIM)
    r = jax.nn.sigmoid(jnp.einsum('blhd,hde->blhe', xh, w_r).reshape(bsz, seqlen, MIX_W) + b_r)
    i = jax.nn.sigmoid(jnp.einsum('blhd,hde->blhe', xh, w_i).reshape(bsz, seqlen, MIX_W) + b_i)
    log_a = -LRU_C * r.astype(jnp.float32) * jax.nn.softplus(-lam.astype(jnp.float32))
    a = jnp.exp(log_a)
    mult = jnp.sqrt(-jnp.expm1(2.0 * log_a))
    bt = mult * (i * xc).astype(jnp.float32)
    _, h = lax.associative_scan(_linear_combine, (a, bt), axis=1)
    return h.astype(dt) * jax.nn.gelu(gb, approximate=True)


def pool_mixer(xp, w, scale):
    dt = xp.dtype
    bsz, seqlen, _ = xp.shape
    xf = xp.astype(jnp.float32)
    cs = jnp.cumsum(xf, axis=1)
    cs_pad = jnp.concatenate([jnp.zeros((bsz, 1, MIX_W), jnp.float32), cs], axis=1)
    pos = jnp.arange(seqlen, dtype=jnp.float32) + 1.0
    diffs = []
    for gi, win in enumerate(POOL_WINDOWS):
        sl = slice(gi * POOL_GROUP_W, (gi + 1) * POOL_GROUP_W)
        upper = cs_pad[:, 1:, sl]
        lower = jnp.concatenate(
            [jnp.zeros((bsz, win - 1, POOL_GROUP_W), jnp.float32), cs_pad[:, :seqlen - win + 1, sl]], axis=1)
        count = jnp.minimum(pos, float(win))[None, :, None]
        diffs.append((upper - lower) / count - xf[:, :, sl])
    dg = jnp.stack(diffs, axis=2)
    y = jnp.einsum('blgc,gce->blge', dg, w.astype(jnp.float32)).reshape(bsz, seqlen, MIX_W)
    return (y * scale.astype(jnp.float32)).astype(dt)


def conv_gated_mlp(h, w_up, w_dw, b_dw, w_down):
    up = h @ w_up
    gate, val = jnp.split(up, 2, axis=-1)
    gate = causal_dwconv(gate, w_dw, b_dw)
    return (jax.nn.gelu(gate, approximate=True) * val) @ w_down


def _fwd_setup_inputs(seed: int = 0) -> dict:
    key = jax.random.key(seed)
    ks = iter(jax.random.split(key, 40))
    nrm = lambda shape, std: std * jax.random.normal(next(ks), shape, jnp.float32)
    L_, G, P, H = DEPTH, S5_GROUPS, S5_STATE, S5_GROUP_CH
    x = jax.random.normal(next(ks), (BATCH, SEQ, D_MODEL), jnp.float32)
    lam_im_base = jnp.pi * jnp.arange(P, dtype=jnp.float32)
    a0 = jax.random.uniform(next(ks), (L_, MIX_W), jnp.float32, 0.9, 0.999)
    a_base = a0 ** (1.0 / LRU_C)
    return {
        "x": x,
        "norm_mix_g": 1.0 + nrm((L_, D_MODEL), 0.02),
        "w_in": nrm((L_, D_MODEL, IN_COLS), D_MODEL ** -0.5),
        "s5_lam_re": -0.5 + nrm((L_, G, P), 0.01),
        "s5_lam_im": lam_im_base + nrm((L_, G, P), 0.01),
        "s5_log_step": jax.random.uniform(next(ks), (L_, G), jnp.float32, math.log(0.001), math.log(0.1)),
        "s5_b_re": nrm((L_, G, P, H), (2.0 * H) ** -0.5),
        "s5_b_im": nrm((L_, G, P, H), (2.0 * H) ** -0.5),
        "s5_c_re": nrm((L_, G, H, P), (2.0 * P) ** -0.5),
        "s5_c_im": nrm((L_, G, H, P), (2.0 * P) ** -0.5),
        "s5_d": nrm((L_, MIX_W), 1.0),
        "s5_w_glu": nrm((L_, MIX_W, MIX_W), MIX_W ** -0.5),
        "s5_b_glu": nrm((L_, MIX_W), 0.01),
        "cv_w_dw": nrm((L_, CONV_WIDTH, MIX_W), CONV_WIDTH ** -0.5),
        "cv_b_dw": nrm((L_, MIX_W), 0.01),
        "cv_ln_g": 1.0 + nrm((L_, MIX_W), 0.02),
        "cv_ln_b": nrm((L_, MIX_W), 0.01),
        "cv_w_pw": nrm((L_, MIX_W, MIX_W), MIX_W ** -0.5),
        "cv_b_pw": nrm((L_, MIX_W), 0.01),
        "lru_w_conv": nrm((L_, LRU_CONV_WIDTH, MIX_W), LRU_CONV_WIDTH ** -0.5),
        "lru_b_conv": nrm((L_, MIX_W), 0.01),
        "lru_w_r": nrm((L_, LRU_HEADS, LRU_HEAD_DIM, LRU_HEAD_DIM), LRU_HEAD_DIM ** -0.5),
        "lru_b_r": nrm((L_, MIX_W), 0.01),
        "lru_w_i": nrm((L_, LRU_HEADS, LRU_HEAD_DIM, LRU_HEAD_DIM), LRU_HEAD_DIM ** -0.5),
        "lru_b_i": nrm((L_, MIX_W), 0.01),
        "lru_lam": jnp.log(a_base) - jnp.log1p(-a_base),
        "pool_w": nrm((L_, len(POOL_WINDOWS), POOL_GROUP_W, POOL_GROUP_W), POOL_GROUP_W ** -0.5),
        "pool_scale": 1.0 + nrm((L_, MIX_W), 0.02),
        "w_out": nrm((L_, D_MIX, D_MODEL), D_MIX ** -0.5),
        "norm_ffn_g": 1.0 + nrm((L_, D_MODEL), 0.02),
        "ffn_w_up": nrm((L_, D_MODEL, 2 * FFN_DIM), D_MODEL ** -0.5),
        "ffn_w_dw": nrm((L_, FFN_CONV_WIDTH, FFN_DIM), FFN_CONV_WIDTH ** -0.5),
        "ffn_b_dw": nrm((L_, FFN_DIM), 0.01),
        "ffn_w_down": nrm((L_, FFN_DIM, D_MODEL), FFN_DIM ** -0.5),
        "norm_final_g": 1.0 + nrm((D_MODEL,), 0.02),
    }


def _fwd_reference(x, norm_mix_g, w_in, s5_lam_re, s5_lam_im, s5_log_step, s5_b_re, s5_b_im,
              s5_c_re, s5_c_im, s5_d, s5_w_glu, s5_b_glu, cv_w_dw, cv_b_dw, cv_ln_g, cv_ln_b,
              cv_w_pw, cv_b_pw, lru_w_conv, lru_b_conv, lru_w_r, lru_b_r, lru_w_i, lru_b_i,
              lru_lam, pool_w, pool_scale, w_out, norm_ffn_g, ffn_w_up, ffn_w_dw, ffn_b_dw,
              ffn_w_down, norm_final_g):
    split_idx = [MIX_W * k for k in range(1, 6)]
    for l in range(DEPTH):
        h = rmsnorm(x, norm_mix_g[l])
        proj = h @ w_in[l]
        s5_u, cv_v, cv_g, lru_x, lru_g, pool_x = jnp.split(proj, split_idx, axis=-1)
        y_s5 = s5_mixer(s5_u, s5_lam_re[l], s5_lam_im[l], s5_log_step[l], s5_b_re[l], s5_b_im[l],
                        s5_c_re[l], s5_c_im[l], s5_d[l], s5_w_glu[l], s5_b_glu[l])
        y_cv = conformer_conv_mixer(cv_v, cv_g, cv_w_dw[l], cv_b_dw[l], cv_ln_g[l], cv_ln_b[l],
                                    cv_w_pw[l], cv_b_pw[l])
        y_lru = rglru_mixer(lru_x, lru_g, lru_w_conv[l], lru_b_conv[l], lru_w_r[l], lru_b_r[l],
                            lru_w_i[l], lru_b_i[l], lru_lam[l])
        y_pool = pool_mixer(pool_x, pool_w[l], pool_scale[l])
        mixed = jnp.concatenate([y_s5, y_cv, y_lru, y_pool], axis=-1)
        x = x + mixed @ w_out[l]
        h = rmsnorm(x, norm_ffn_g[l])
        x = x + conv_gated_mlp(h, ffn_w_up[l], ffn_w_dw[l], ffn_b_dw[l], ffn_w_down[l])
    return rmsnorm(x, norm_final_g)


import jax as _jax
import jax.numpy as _jnp

TWIN_FORMAT = 'train_step'
FWD_PARAMS = ['x', 'norm_mix_g', 'w_in', 's5_lam_re', 's5_lam_im', 's5_log_step', 's5_b_re', 's5_b_im', 's5_c_re', 's5_c_im', 's5_d', 's5_w_glu', 's5_b_glu', 'cv_w_dw', 'cv_b_dw', 'cv_ln_g', 'cv_ln_b', 'cv_w_pw', 'cv_b_pw', 'lru_w_conv', 'lru_b_conv', 'lru_w_r', 'lru_b_r', 'lru_w_i', 'lru_b_i', 'lru_lam', 'pool_w', 'pool_scale', 'w_out', 'norm_ffn_g', 'ffn_w_up', 'ffn_w_dw', 'ffn_b_dw', 'ffn_w_down', 'norm_final_g']
TWIN_WEIGHTS = ['norm_mix_g', 'w_in', 's5_lam_re', 's5_lam_im', 's5_log_step', 's5_b_re', 's5_b_im', 's5_c_re', 's5_c_im', 's5_d', 's5_w_glu', 's5_b_glu', 'cv_w_dw', 'cv_b_dw', 'cv_ln_g', 'cv_ln_b', 'cv_w_pw', 'cv_b_pw', 'lru_w_conv', 'lru_b_conv', 'lru_w_r', 'lru_b_r', 'lru_w_i', 'lru_b_i', 'lru_lam', 'pool_w', 'pool_scale', 'w_out', 'norm_ffn_g', 'ffn_w_up', 'ffn_w_dw', 'ffn_b_dw', 'ffn_w_down', 'norm_final_g']
TWIN_DIFF_INPUT = 'x'
TWIN_INPUTS = ['x', 'norm_mix_g', 'w_in', 's5_lam_re', 's5_lam_im', 's5_log_step', 's5_b_re', 's5_b_im', 's5_c_re', 's5_c_im', 's5_d', 's5_w_glu', 's5_b_glu', 'cv_w_dw', 'cv_b_dw', 'cv_ln_g', 'cv_ln_b', 'cv_w_pw', 'cv_b_pw', 'lru_w_conv', 'lru_b_conv', 'lru_w_r', 'lru_b_r', 'lru_w_i', 'lru_b_i', 'lru_lam', 'pool_w', 'pool_scale', 'w_out', 'norm_ffn_g', 'ffn_w_up', 'ffn_w_dw', 'ffn_b_dw', 'ffn_w_down', 'norm_final_g', 'loss_target', 'm_norm_mix_g', 'm_w_in', 'm_s5_lam_re', 'm_s5_lam_im', 'm_s5_log_step', 'm_s5_b_re', 'm_s5_b_im', 'm_s5_c_re', 'm_s5_c_im', 'm_s5_d', 'm_s5_w_glu', 'm_s5_b_glu', 'm_cv_w_dw', 'm_cv_b_dw', 'm_cv_ln_g', 'm_cv_ln_b', 'm_cv_w_pw', 'm_cv_b_pw', 'm_lru_w_conv', 'm_lru_b_conv', 'm_lru_w_r', 'm_lru_b_r', 'm_lru_w_i', 'm_lru_b_i', 'm_lru_lam', 'm_pool_w', 'm_pool_scale', 'm_w_out', 'm_norm_ffn_g', 'm_ffn_w_up', 'm_ffn_w_dw', 'm_ffn_b_dw', 'm_ffn_w_down', 'm_norm_final_g', 'v_norm_mix_g', 'v_w_in', 'v_s5_lam_re', 'v_s5_lam_im', 'v_s5_log_step', 'v_s5_b_re', 'v_s5_b_im', 'v_s5_c_re', 'v_s5_c_im', 'v_s5_d', 'v_s5_w_glu', 'v_s5_b_glu', 'v_cv_w_dw', 'v_cv_b_dw', 'v_cv_ln_g', 'v_cv_ln_b', 'v_cv_w_pw', 'v_cv_b_pw', 'v_lru_w_conv', 'v_lru_b_conv', 'v_lru_w_r', 'v_lru_b_r', 'v_lru_w_i', 'v_lru_b_i', 'v_lru_lam', 'v_pool_w', 'v_pool_scale', 'v_w_out', 'v_norm_ffn_g', 'v_ffn_w_up', 'v_ffn_w_dw', 'v_ffn_b_dw', 'v_ffn_w_down', 'v_norm_final_g']
TWIN_OUTPUTS = ['loss', 'grad_x', 'grad_norm_mix_g', 'grad_w_in', 'grad_s5_lam_re', 'grad_s5_lam_im', 'grad_s5_log_step', 'grad_s5_b_re', 'grad_s5_b_im', 'grad_s5_c_re', 'grad_s5_c_im', 'grad_s5_d', 'grad_s5_w_glu', 'grad_s5_b_glu', 'grad_cv_w_dw', 'grad_cv_b_dw', 'grad_cv_ln_g', 'grad_cv_ln_b', 'grad_cv_w_pw', 'grad_cv_b_pw', 'grad_lru_w_conv', 'grad_lru_b_conv', 'grad_lru_w_r', 'grad_lru_b_r', 'grad_lru_w_i', 'grad_lru_b_i', 'grad_lru_lam', 'grad_pool_w', 'grad_pool_scale', 'grad_w_out', 'grad_norm_ffn_g', 'grad_ffn_w_up', 'grad_ffn_w_dw', 'grad_ffn_b_dw', 'grad_ffn_w_down', 'grad_norm_final_g', 'delta_norm_mix_g', 'delta_w_in', 'delta_s5_lam_re', 'delta_s5_lam_im', 'delta_s5_log_step', 'delta_s5_b_re', 'delta_s5_b_im', 'delta_s5_c_re', 'delta_s5_c_im', 'delta_s5_d', 'delta_s5_w_glu', 'delta_s5_b_glu', 'delta_cv_w_dw', 'delta_cv_b_dw', 'delta_cv_ln_g', 'delta_cv_ln_b', 'delta_cv_w_pw', 'delta_cv_b_pw', 'delta_lru_w_conv', 'delta_lru_b_conv', 'delta_lru_w_r', 'delta_lru_b_r', 'delta_lru_w_i', 'delta_lru_b_i', 'delta_lru_lam', 'delta_pool_w', 'delta_pool_scale', 'delta_w_out', 'delta_norm_ffn_g', 'delta_ffn_w_up', 'delta_ffn_w_dw', 'delta_ffn_b_dw', 'delta_ffn_w_down', 'delta_norm_final_g', 'new_m_norm_mix_g', 'new_m_w_in', 'new_m_s5_lam_re', 'new_m_s5_lam_im', 'new_m_s5_log_step', 'new_m_s5_b_re', 'new_m_s5_b_im', 'new_m_s5_c_re', 'new_m_s5_c_im', 'new_m_s5_d', 'new_m_s5_w_glu', 'new_m_s5_b_glu', 'new_m_cv_w_dw', 'new_m_cv_b_dw', 'new_m_cv_ln_g', 'new_m_cv_ln_b', 'new_m_cv_w_pw', 'new_m_cv_b_pw', 'new_m_lru_w_conv', 'new_m_lru_b_conv', 'new_m_lru_w_r', 'new_m_lru_b_r', 'new_m_lru_w_i', 'new_m_lru_b_i', 'new_m_lru_lam', 'new_m_pool_w', 'new_m_pool_scale', 'new_m_w_out', 'new_m_norm_ffn_g', 'new_m_ffn_w_up', 'new_m_ffn_w_dw', 'new_m_ffn_b_dw', 'new_m_ffn_w_down', 'new_m_norm_final_g', 'new_v_norm_mix_g', 'new_v_w_in', 'new_v_s5_lam_re', 'new_v_s5_lam_im', 'new_v_s5_log_step', 'new_v_s5_b_re', 'new_v_s5_b_im', 'new_v_s5_c_re', 'new_v_s5_c_im', 'new_v_s5_d', 'new_v_s5_w_glu', 'new_v_s5_b_glu', 'new_v_cv_w_dw', 'new_v_cv_b_dw', 'new_v_cv_ln_g', 'new_v_cv_ln_b', 'new_v_cv_w_pw', 'new_v_cv_b_pw', 'new_v_lru_w_conv', 'new_v_lru_b_conv', 'new_v_lru_w_r', 'new_v_lru_b_r', 'new_v_lru_w_i', 'new_v_lru_b_i', 'new_v_lru_lam', 'new_v_pool_w', 'new_v_pool_scale', 'new_v_w_out', 'new_v_norm_ffn_g', 'new_v_ffn_w_up', 'new_v_ffn_w_dw', 'new_v_ffn_b_dw', 'new_v_ffn_w_down', 'new_v_norm_final_g']
TWIN_LEAF_KINDS = {'loss': 'loss', 'grad_x': 'grad_x', 'grad_norm_mix_g': 'grad_w', 'grad_w_in': 'grad_w', 'grad_s5_lam_re': 'grad_w', 'grad_s5_lam_im': 'grad_w', 'grad_s5_log_step': 'grad_w', 'grad_s5_b_re': 'grad_w', 'grad_s5_b_im': 'grad_w', 'grad_s5_c_re': 'grad_w', 'grad_s5_c_im': 'grad_w', 'grad_s5_d': 'grad_w', 'grad_s5_w_glu': 'grad_w', 'grad_s5_b_glu': 'grad_w', 'grad_cv_w_dw': 'grad_w', 'grad_cv_b_dw': 'grad_w', 'grad_cv_ln_g': 'grad_w', 'grad_cv_ln_b': 'grad_w', 'grad_cv_w_pw': 'grad_w', 'grad_cv_b_pw': 'grad_w', 'grad_lru_w_conv': 'grad_w', 'grad_lru_b_conv': 'grad_w', 'grad_lru_w_r': 'grad_w', 'grad_lru_b_r': 'grad_w', 'grad_lru_w_i': 'grad_w', 'grad_lru_b_i': 'grad_w', 'grad_lru_lam': 'grad_w', 'grad_pool_w': 'grad_w', 'grad_pool_scale': 'grad_w', 'grad_w_out': 'grad_w', 'grad_norm_ffn_g': 'grad_w', 'grad_ffn_w_up': 'grad_w', 'grad_ffn_w_dw': 'grad_w', 'grad_ffn_b_dw': 'grad_w', 'grad_ffn_w_down': 'grad_w', 'grad_norm_final_g': 'grad_w', 'delta_norm_mix_g': 'delta_w', 'delta_w_in': 'delta_w', 'delta_s5_lam_re': 'delta_w', 'delta_s5_lam_im': 'delta_w', 'delta_s5_log_step': 'delta_w', 'delta_s5_b_re': 'delta_w', 'delta_s5_b_im': 'delta_w', 'delta_s5_c_re': 'delta_w', 'delta_s5_c_im': 'delta_w', 'delta_s5_d': 'delta_w', 'delta_s5_w_glu': 'delta_w', 'delta_s5_b_glu': 'delta_w', 'delta_cv_w_dw': 'delta_w', 'delta_cv_b_dw': 'delta_w', 'delta_cv_ln_g': 'delta_w', 'delta_cv_ln_b': 'delta_w', 'delta_cv_w_pw': 'delta_w', 'delta_cv_b_pw': 'delta_w', 'delta_lru_w_conv': 'delta_w', 'delta_lru_b_conv': 'delta_w', 'delta_lru_w_r': 'delta_w', 'delta_lru_b_r': 'delta_w', 'delta_lru_w_i': 'delta_w', 'delta_lru_b_i': 'delta_w', 'delta_lru_lam': 'delta_w', 'delta_pool_w': 'delta_w', 'delta_pool_scale': 'delta_w', 'delta_w_out': 'delta_w', 'delta_norm_ffn_g': 'delta_w', 'delta_ffn_w_up': 'delta_w', 'delta_ffn_w_dw': 'delta_w', 'delta_ffn_b_dw': 'delta_w', 'delta_ffn_w_down': 'delta_w', 'delta_norm_final_g': 'delta_w', 'new_m_norm_mix_g': 'new_m', 'new_m_w_in': 'new_m', 'new_m_s5_lam_re': 'new_m', 'new_m_s5_lam_im': 'new_m', 'new_m_s5_log_step': 'new_m', 'new_m_s5_b_re': 'new_m', 'new_m_s5_b_im': 'new_m', 'new_m_s5_c_re': 'new_m', 'new_m_s5_c_im': 'new_m', 'new_m_s5_d': 'new_m', 'new_m_s5_w_glu': 'new_m', 'new_m_s5_b_glu': 'new_m', 'new_m_cv_w_dw': 'new_m', 'new_m_cv_b_dw': 'new_m', 'new_m_cv_ln_g': 'new_m', 'new_m_cv_ln_b': 'new_m', 'new_m_cv_w_pw': 'new_m', 'new_m_cv_b_pw': 'new_m', 'new_m_lru_w_conv': 'new_m', 'new_m_lru_b_conv': 'new_m', 'new_m_lru_w_r': 'new_m', 'new_m_lru_b_r': 'new_m', 'new_m_lru_w_i': 'new_m', 'new_m_lru_b_i': 'new_m', 'new_m_lru_lam': 'new_m', 'new_m_pool_w': 'new_m', 'new_m_pool_scale': 'new_m', 'new_m_w_out': 'new_m', 'new_m_norm_ffn_g': 'new_m', 'new_m_ffn_w_up': 'new_m', 'new_m_ffn_w_dw': 'new_m', 'new_m_ffn_b_dw': 'new_m', 'new_m_ffn_w_down': 'new_m', 'new_m_norm_final_g': 'new_m', 'new_v_norm_mix_g': 'new_v', 'new_v_w_in': 'new_v', 'new_v_s5_lam_re': 'new_v', 'new_v_s5_lam_im': 'new_v', 'new_v_s5_log_step': 'new_v', 'new_v_s5_b_re': 'new_v', 'new_v_s5_b_im': 'new_v', 'new_v_s5_c_re': 'new_v', 'new_v_s5_c_im': 'new_v', 'new_v_s5_d': 'new_v', 'new_v_s5_w_glu': 'new_v', 'new_v_s5_b_glu': 'new_v', 'new_v_cv_w_dw': 'new_v', 'new_v_cv_b_dw': 'new_v', 'new_v_cv_ln_g': 'new_v', 'new_v_cv_ln_b': 'new_v', 'new_v_cv_w_pw': 'new_v', 'new_v_cv_b_pw': 'new_v', 'new_v_lru_w_conv': 'new_v', 'new_v_lru_b_conv': 'new_v', 'new_v_lru_w_r': 'new_v', 'new_v_lru_b_r': 'new_v', 'new_v_lru_w_i': 'new_v', 'new_v_lru_b_i': 'new_v', 'new_v_lru_lam': 'new_v', 'new_v_pool_w': 'new_v', 'new_v_pool_scale': 'new_v', 'new_v_w_out': 'new_v', 'new_v_norm_ffn_g': 'new_v', 'new_v_ffn_w_up': 'new_v', 'new_v_ffn_w_dw': 'new_v', 'new_v_ffn_b_dw': 'new_v', 'new_v_ffn_w_down': 'new_v', 'new_v_norm_final_g': 'new_v'}


def _forward(args):
    return _fwd_reference(*[args[k] for k in FWD_PARAMS])


def _output_shape():
    def fwd():
        inp = _fwd_setup_inputs(0)
        return _fwd_reference(*[inp[k] for k in FWD_PARAMS])
    out = _jax.eval_shape(fwd)
    return out.shape, out.dtype

N_MICROBATCH = 1
ADAM_LR = 0.001
ADAM_B1 = 0.9
ADAM_B2 = 0.999
ADAM_EPS = 1e-08
ADAM_WD = 0.01
ADAM_STEP = 10
PER_EXAMPLE_BATCH_AXIS = {'x': 0, 'loss_target': 0}
SHARED_INPUTS = []
_WEIGHT_DTYPES = {'norm_mix_g': _jnp.float32, 'w_in': _jnp.float32, 's5_lam_re': _jnp.float32, 's5_lam_im': _jnp.float32, 's5_log_step': _jnp.float32, 's5_b_re': _jnp.float32, 's5_b_im': _jnp.float32, 's5_c_re': _jnp.float32, 's5_c_im': _jnp.float32, 's5_d': _jnp.float32, 's5_w_glu': _jnp.float32, 's5_b_glu': _jnp.float32, 'cv_w_dw': _jnp.float32, 'cv_b_dw': _jnp.float32, 'cv_ln_g': _jnp.float32, 'cv_ln_b': _jnp.float32, 'cv_w_pw': _jnp.float32, 'cv_b_pw': _jnp.float32, 'lru_w_conv': _jnp.float32, 'lru_b_conv': _jnp.float32, 'lru_w_r': _jnp.float32, 'lru_b_r': _jnp.float32, 'lru_w_i': _jnp.float32, 'lru_b_i': _jnp.float32, 'lru_lam': _jnp.float32, 'pool_w': _jnp.float32, 'pool_scale': _jnp.float32, 'w_out': _jnp.float32, 'norm_ffn_g': _jnp.float32, 'ffn_w_up': _jnp.float32, 'ffn_w_dw': _jnp.float32, 'ffn_b_dw': _jnp.float32, 'ffn_w_down': _jnp.float32, 'norm_final_g': _jnp.float32}
MOMENT_SCALE = {'norm_mix_g': 5.524406e-02, 'w_in': 4.491174e-02, 's5_lam_re': 1.464161e-03, 's5_lam_im': 1.553525e-03, 's5_log_step': 1.170819e+00, 's5_b_re': 9.754285e-04, 's5_b_im': 9.890522e-04, 's5_c_re': 1.982450e-03, 's5_c_im': 1.978814e-03, 's5_d': 3.487052e-02, 's5_w_glu': 8.634902e-03, 's5_b_glu': 1.465971e-02, 'cv_w_dw': 5.074947e-02, 'cv_b_dw': 1.146475e-01, 'cv_ln_g': 6.659649e-02, 'cv_ln_b': 6.143284e-02, 'cv_w_pw': 5.097047e-02, 'cv_b_pw': 1.141650e-01, 'lru_w_conv': 4.636019e-02, 'lru_b_conv': 3.727016e-01, 'lru_w_r': 1.266074e-02, 'lru_b_r': 1.145400e-02, 'lru_w_i': 2.311761e-02, 'lru_b_i': 1.633832e-02, 'lru_lam': 2.293824e-02, 'pool_w': 7.075014e-02, 'pool_scale': 7.622157e-02, 'w_out': 5.110615e-02, 'norm_ffn_g': 6.247269e-02, 'ffn_w_up': 2.642116e-02, 'ffn_w_dw': 2.677923e-02, 'ffn_b_dw': 2.600161e-02, 'ffn_w_down': 4.260410e-02, 'norm_final_g': 1.599115e+01}


def _to_microbatches(a, axis):
    t = _jnp.moveaxis(a, axis, 0)
    t = t.reshape((N_MICROBATCH, t.shape[0] // N_MICROBATCH) + t.shape[1:])
    return _jnp.moveaxis(t, 1, axis + 1)


def setup_inputs(seed: int = 0) -> dict:
    inp = _fwd_setup_inputs(seed)
    key = _jax.random.fold_in(_jax.random.key(seed), 7919)
    shape, _ = _output_shape()
    out = dict(inp)
    out["loss_target"] = _jax.random.normal(_jax.random.fold_in(key, 0), shape, _jnp.float32)
    for i, name in enumerate(TWIN_WEIGHTS):
        w = inp[name].astype(_jnp.float32)
        if MOMENT_SCALE is None:
            s = _jnp.sqrt(_jnp.mean(_jnp.square(w)) + 1e-30)
        else:
            s = MOMENT_SCALE[name]
        km, kv = _jax.random.split(_jax.random.fold_in(key, i + 1))
        out[name] = w
        out["m_" + name] = s * _jax.random.normal(km, w.shape, _jnp.float32)
        out["v_" + name] = (s * s) * _jax.random.uniform(kv, w.shape, _jnp.float32, 0.5, 1.5)
    if N_MICROBATCH > 1:
        for name, axis in PER_EXAMPLE_BATCH_AXIS.items():
            out[name] = _to_microbatches(out[name], axis)
    return {'x': out['x'], 'norm_mix_g': out['norm_mix_g'], 'w_in': out['w_in'], 's5_lam_re': out['s5_lam_re'], 's5_lam_im': out['s5_lam_im'], 's5_log_step': out['s5_log_step'], 's5_b_re': out['s5_b_re'], 's5_b_im': out['s5_b_im'], 's5_c_re': out['s5_c_re'], 's5_c_im': out['s5_c_im'], 's5_d': out['s5_d'], 's5_w_glu': out['s5_w_glu'], 's5_b_glu': out['s5_b_glu'], 'cv_w_dw': out['cv_w_dw'], 'cv_b_dw': out['cv_b_dw'], 'cv_ln_g': out['cv_ln_g'], 'cv_ln_b': out['cv_ln_b'], 'cv_w_pw': out['cv_w_pw'], 'cv_b_pw': out['cv_b_pw'], 'lru_w_conv': out['lru_w_conv'], 'lru_b_conv': out['lru_b_conv'], 'lru_w_r': out['lru_w_r'], 'lru_b_r': out['lru_b_r'], 'lru_w_i': out['lru_w_i'], 'lru_b_i': out['lru_b_i'], 'lru_lam': out['lru_lam'], 'pool_w': out['pool_w'], 'pool_scale': out['pool_scale'], 'w_out': out['w_out'], 'norm_ffn_g': out['norm_ffn_g'], 'ffn_w_up': out['ffn_w_up'], 'ffn_w_dw': out['ffn_w_dw'], 'ffn_b_dw': out['ffn_b_dw'], 'ffn_w_down': out['ffn_w_down'], 'norm_final_g': out['norm_final_g'], 'loss_target': out['loss_target'], 'm_norm_mix_g': out['m_norm_mix_g'], 'm_w_in': out['m_w_in'], 'm_s5_lam_re': out['m_s5_lam_re'], 'm_s5_lam_im': out['m_s5_lam_im'], 'm_s5_log_step': out['m_s5_log_step'], 'm_s5_b_re': out['m_s5_b_re'], 'm_s5_b_im': out['m_s5_b_im'], 'm_s5_c_re': out['m_s5_c_re'], 'm_s5_c_im': out['m_s5_c_im'], 'm_s5_d': out['m_s5_d'], 'm_s5_w_glu': out['m_s5_w_glu'], 'm_s5_b_glu': out['m_s5_b_glu'], 'm_cv_w_dw': out['m_cv_w_dw'], 'm_cv_b_dw': out['m_cv_b_dw'], 'm_cv_ln_g': out['m_cv_ln_g'], 'm_cv_ln_b': out['m_cv_ln_b'], 'm_cv_w_pw': out['m_cv_w_pw'], 'm_cv_b_pw': out['m_cv_b_pw'], 'm_lru_w_conv': out['m_lru_w_conv'], 'm_lru_b_conv': out['m_lru_b_conv'], 'm_lru_w_r': out['m_lru_w_r'], 'm_lru_b_r': out['m_lru_b_r'], 'm_lru_w_i': out['m_lru_w_i'], 'm_lru_b_i': out['m_lru_b_i'], 'm_lru_lam': out['m_lru_lam'], 'm_pool_w': out['m_pool_w'], 'm_pool_scale': out['m_pool_scale'], 'm_w_out': out['m_w_out'], 'm_norm_ffn_g': out['m_norm_ffn_g'], 'm_ffn_w_up': out['m_ffn_w_up'], 'm_ffn_w_dw': out['m_ffn_w_dw'], 'm_ffn_b_dw': out['m_ffn_b_dw'], 'm_ffn_w_down': out['m_ffn_w_down'], 'm_norm_final_g': out['m_norm_final_g'], 'v_norm_mix_g': out['v_norm_mix_g'], 'v_w_in': out['v_w_in'], 'v_s5_lam_re': out['v_s5_lam_re'], 'v_s5_lam_im': out['v_s5_lam_im'], 'v_s5_log_step': out['v_s5_log_step'], 'v_s5_b_re': out['v_s5_b_re'], 'v_s5_b_im': out['v_s5_b_im'], 'v_s5_c_re': out['v_s5_c_re'], 'v_s5_c_im': out['v_s5_c_im'], 'v_s5_d': out['v_s5_d'], 'v_s5_w_glu': out['v_s5_w_glu'], 'v_s5_b_glu': out['v_s5_b_glu'], 'v_cv_w_dw': out['v_cv_w_dw'], 'v_cv_b_dw': out['v_cv_b_dw'], 'v_cv_ln_g': out['v_cv_ln_g'], 'v_cv_ln_b': out['v_cv_ln_b'], 'v_cv_w_pw': out['v_cv_w_pw'], 'v_cv_b_pw': out['v_cv_b_pw'], 'v_lru_w_conv': out['v_lru_w_conv'], 'v_lru_b_conv': out['v_lru_b_conv'], 'v_lru_w_r': out['v_lru_w_r'], 'v_lru_b_r': out['v_lru_b_r'], 'v_lru_w_i': out['v_lru_w_i'], 'v_lru_b_i': out['v_lru_b_i'], 'v_lru_lam': out['v_lru_lam'], 'v_pool_w': out['v_pool_w'], 'v_pool_scale': out['v_pool_scale'], 'v_w_out': out['v_w_out'], 'v_norm_ffn_g': out['v_norm_ffn_g'], 'v_ffn_w_up': out['v_ffn_w_up'], 'v_ffn_w_dw': out['v_ffn_w_dw'], 'v_ffn_b_dw': out['v_ffn_b_dw'], 'v_ffn_w_down': out['v_ffn_w_down'], 'v_norm_final_g': out['v_norm_final_g']}


def _loss(weights, diff, rest, loss_target):
    with _jax.named_scope("forward"):
        args = {**rest, TWIN_DIFF_INPUT: diff, **{k: w.astype(_WEIGHT_DTYPES[k]) for k, w in weights.items()}}
        y = _forward(args)
    with _jax.named_scope("loss_head"):
        err = _jnp.square(y.astype(_jnp.float32) - loss_target)
        return 0.5 * _jnp.sum(_jnp.mean(err, axis=-1)) if err.ndim else 0.5 * err


def _adamw(w, g, m, v):
    m = ADAM_B1 * m + (1.0 - ADAM_B1) * g
    v = ADAM_B2 * v + (1.0 - ADAM_B2) * _jnp.square(g)
    m_hat = m / (1.0 - ADAM_B1 ** ADAM_STEP)
    v_hat = v / (1.0 - ADAM_B2 ** ADAM_STEP)
    delta = -ADAM_LR * (m_hat / (_jnp.sqrt(v_hat) + ADAM_EPS) + ADAM_WD * w)
    return delta, m, v


def reference(x, norm_mix_g, w_in, s5_lam_re, s5_lam_im, s5_log_step, s5_b_re, s5_b_im, s5_c_re, s5_c_im, s5_d, s5_w_glu, s5_b_glu, cv_w_dw, cv_b_dw, cv_ln_g, cv_ln_b, cv_w_pw, cv_b_pw, lru_w_conv, lru_b_conv, lru_w_r, lru_b_r, lru_w_i, lru_b_i, lru_lam, pool_w, pool_scale, w_out, norm_ffn_g, ffn_w_up, ffn_w_dw, ffn_b_dw, ffn_w_down, norm_final_g, loss_target, m_norm_mix_g, m_w_in, m_s5_lam_re, m_s5_lam_im, m_s5_log_step, m_s5_b_re, m_s5_b_im, m_s5_c_re, m_s5_c_im, m_s5_d, m_s5_w_glu, m_s5_b_glu, m_cv_w_dw, m_cv_b_dw, m_cv_ln_g, m_cv_ln_b, m_cv_w_pw, m_cv_b_pw, m_lru_w_conv, m_lru_b_conv, m_lru_w_r, m_lru_b_r, m_lru_w_i, m_lru_b_i, m_lru_lam, m_pool_w, m_pool_scale, m_w_out, m_norm_ffn_g, m_ffn_w_up, m_ffn_w_dw, m_ffn_b_dw, m_ffn_w_down, m_norm_final_g, v_norm_mix_g, v_w_in, v_s5_lam_re, v_s5_lam_im, v_s5_log_step, v_s5_b_re, v_s5_b_im, v_s5_c_re, v_s5_c_im, v_s5_d, v_s5_w_glu, v_s5_b_glu, v_cv_w_dw, v_cv_b_dw, v_cv_ln_g, v_cv_ln_b, v_cv_w_pw, v_cv_b_pw, v_lru_w_conv, v_lru_b_conv, v_lru_w_r, v_lru_b_r, v_lru_w_i, v_lru_b_i, v_lru_lam, v_pool_w, v_pool_scale, v_w_out, v_norm_ffn_g, v_ffn_w_up, v_ffn_w_dw, v_ffn_b_dw, v_ffn_w_down, v_norm_final_g):
    given = dict(x=x, norm_mix_g=norm_mix_g, w_in=w_in, s5_lam_re=s5_lam_re, s5_lam_im=s5_lam_im, s5_log_step=s5_log_step, s5_b_re=s5_b_re, s5_b_im=s5_b_im, s5_c_re=s5_c_re, s5_c_im=s5_c_im, s5_d=s5_d, s5_w_glu=s5_w_glu, s5_b_glu=s5_b_glu, cv_w_dw=cv_w_dw, cv_b_dw=cv_b_dw, cv_ln_g=cv_ln_g, cv_ln_b=cv_ln_b, cv_w_pw=cv_w_pw, cv_b_pw=cv_b_pw, lru_w_conv=lru_w_conv, lru_b_conv=lru_b_conv, lru_w_r=lru_w_r, lru_b_r=lru_b_r, lru_w_i=lru_w_i, lru_b_i=lru_b_i, lru_lam=lru_lam, pool_w=pool_w, pool_scale=pool_scale, w_out=w_out, norm_ffn_g=norm_ffn_g, ffn_w_up=ffn_w_up, ffn_w_dw=ffn_w_dw, ffn_b_dw=ffn_b_dw, ffn_w_down=ffn_w_down, norm_final_g=norm_final_g, loss_target=loss_target, m_norm_mix_g=m_norm_mix_g, m_w_in=m_w_in, m_s5_lam_re=m_s5_lam_re, m_s5_lam_im=m_s5_lam_im, m_s5_log_step=m_s5_log_step, m_s5_b_re=m_s5_b_re, m_s5_b_im=m_s5_b_im, m_s5_c_re=m_s5_c_re, m_s5_c_im=m_s5_c_im, m_s5_d=m_s5_d, m_s5_w_glu=m_s5_w_glu, m_s5_b_glu=m_s5_b_glu, m_cv_w_dw=m_cv_w_dw, m_cv_b_dw=m_cv_b_dw, m_cv_ln_g=m_cv_ln_g, m_cv_ln_b=m_cv_ln_b, m_cv_w_pw=m_cv_w_pw, m_cv_b_pw=m_cv_b_pw, m_lru_w_conv=m_lru_w_conv, m_lru_b_conv=m_lru_b_conv, m_lru_w_r=m_lru_w_r, m_lru_b_r=m_lru_b_r, m_lru_w_i=m_lru_w_i, m_lru_b_i=m_lru_b_i, m_lru_lam=m_lru_lam, m_pool_w=m_pool_w, m_pool_scale=m_pool_scale, m_w_out=m_w_out, m_norm_ffn_g=m_norm_ffn_g, m_ffn_w_up=m_ffn_w_up, m_ffn_w_dw=m_ffn_w_dw, m_ffn_b_dw=m_ffn_b_dw, m_ffn_w_down=m_ffn_w_down, m_norm_final_g=m_norm_final_g, v_norm_mix_g=v_norm_mix_g, v_w_in=v_w_in, v_s5_lam_re=v_s5_lam_re, v_s5_lam_im=v_s5_lam_im, v_s5_log_step=v_s5_log_step, v_s5_b_re=v_s5_b_re, v_s5_b_im=v_s5_b_im, v_s5_c_re=v_s5_c_re, v_s5_c_im=v_s5_c_im, v_s5_d=v_s5_d, v_s5_w_glu=v_s5_w_glu, v_s5_b_glu=v_s5_b_glu, v_cv_w_dw=v_cv_w_dw, v_cv_b_dw=v_cv_b_dw, v_cv_ln_g=v_cv_ln_g, v_cv_ln_b=v_cv_ln_b, v_cv_w_pw=v_cv_w_pw, v_cv_b_pw=v_cv_b_pw, v_lru_w_conv=v_lru_w_conv, v_lru_b_conv=v_lru_b_conv, v_lru_w_r=v_lru_w_r, v_lru_b_r=v_lru_b_r, v_lru_w_i=v_lru_w_i, v_lru_b_i=v_lru_b_i, v_lru_lam=v_lru_lam, v_pool_w=v_pool_w, v_pool_scale=v_pool_scale, v_w_out=v_w_out, v_norm_ffn_g=v_norm_ffn_g, v_ffn_w_up=v_ffn_w_up, v_ffn_w_dw=v_ffn_w_dw, v_ffn_b_dw=v_ffn_b_dw, v_ffn_w_down=v_ffn_w_down, v_norm_final_g=v_norm_final_g)
    weights = {n: given[n] for n in TWIN_WEIGHTS}
    shared = {n: given[n] for n in SHARED_INPUTS}
    per_example = {n: given[n] for n in ['x']}
    grad_fn = _jax.value_and_grad(_loss, argnums=(0, 1))

    def one_microbatch(ex, loss_target):
        ex = dict(ex)
        diff = ex.pop(TWIN_DIFF_INPUT)
        return grad_fn(weights, diff, {**shared, **ex}, loss_target)

    if N_MICROBATCH == 1:
        loss, (grad_w, grad_x) = one_microbatch(per_example, given["loss_target"])
    else:
        def body(carry, xs):
            loss_sum, grad_sum = carry
            l_k, (gw_k, gx_k) = one_microbatch(xs[0], xs[1])
            with _jax.named_scope("update"):
                return (loss_sum + l_k, _jax.tree.map(_jnp.add, grad_sum, gw_k)), gx_k

        init = (_jnp.zeros((), _jnp.float32), _jax.tree.map(_jnp.zeros_like, weights))
        (loss, grad_w), grad_x = _jax.lax.scan(body, init, (per_example, given["loss_target"]))
    with _jax.named_scope("update"):
        delta_w, new_m, new_v = {}, {}, {}
        for n in TWIN_WEIGHTS:
            delta_w[n], new_m[n], new_v[n] = _adamw(weights[n], grad_w[n], given["m_" + n], given["v_" + n])
    return (loss, grad_x, *[grad_w[n] for n in TWIN_WEIGHTS], *[delta_w[n] for n in TWIN_WEIGHTS],
            *[new_m[n] for n in TWIN_WEIGHTS], *[new_v[n] for n in TWIN_WEIGHTS])
```

```python
import functools

import jax
import jax.numpy as jnp
from jax import lax
from jax.experimental import pallas as pl
from jax.experimental.pallas import tpu as pltpu

F32 = jnp.float32
BF16 = jnp.bfloat16

VMEM_LIMIT_BYTES = 56 * 1024 * 1024
SUBLANES = 8

EPS = 1e-6
S5_GROUPS, S5_STATE, S5_GROUP_CH = 32, 64, 16
LRU_HEADS, LRU_C = 8, 8.0
POOL_WINDOWS = (2, 4, 8, 16)
CV_TAPS, LRU_TAPS, FFN_TAPS = 31, 4, 3
SCAN_CHUNK = 64

ADAM_LR, ADAM_B1, ADAM_B2, ADAM_EPS, ADAM_WD, ADAM_STEP = 0.001, 0.9, 0.999, 1e-08, 0.01, 10

NN = ((1,), (0,))
NT = ((1,), (1,))
TN = ((0,), (0,))

WEIGHTS = ['norm_mix_g', 'w_in', 's5_lam_re', 's5_lam_im', 's5_log_step', 's5_b_re', 's5_b_im', 's5_c_re', 's5_c_im',
           's5_d', 's5_w_glu', 's5_b_glu', 'cv_w_dw', 'cv_b_dw', 'cv_ln_g', 'cv_ln_b', 'cv_w_pw', 'cv_b_pw',
           'lru_w_conv', 'lru_b_conv', 'lru_w_r', 'lru_b_r', 'lru_w_i', 'lru_b_i', 'lru_lam', 'pool_w', 'pool_scale',
           'w_out', 'norm_ffn_g', 'ffn_w_up', 'ffn_w_dw', 'ffn_b_dw', 'ffn_w_down', 'norm_final_g']
BIG = ('w_in', 'w_out', 'ffn_w_up', 'ffn_w_down', 's5_w_glu', 'cv_w_pw')
SMALL_SHARDED = {'cv_w_dw': 2, 'lru_w_conv': 2, 'ffn_w_dw': 2}


def _params(sem=None):
    if sem is None:
        return pltpu.CompilerParams(vmem_limit_bytes=VMEM_LIMIT_BYTES)
    return pltpu.CompilerParams(dimension_semantics=sem, vmem_limit_bytes=VMEM_LIMIT_BYTES)


def _row_tile(rows, cap):
    best = SUBLANES
    for t in range(SUBLANES, min(rows, cap) + 1, SUBLANES):
        if rows % t == 0:
            best = t
    return best


def _bdot(a, b, dims=NN):
    return lax.dot_general(a.astype(BF16), b.astype(BF16), (dims, ((), ())), preferred_element_type=F32)


@jax.custom_vjp
def bdot(a, b):
    return _bdot(a, b)


def _bdot_fwd(a, b):
    return _bdot(a, b), (a, b)


def _bdot_bwd(res, g):
    a, b = res
    return _bdot(g, b, NT).astype(a.dtype), _bdot(a, g, TN).astype(b.dtype)


bdot.defvjp(_bdot_fwd, _bdot_bwd)


def _mm(name, a, b, out_sds, grid, a_spec, b_spec, o_spec, dims, k_axis=None, add=None, add_spec=None):
    nk = grid[k_axis] if k_axis is not None else 1
    has_add = add is not None
    acc_shape = tuple(d for d in o_spec.block_shape if d is not None)

    def body(*refs):
        a_ref, b_ref = refs[0], refs[1]
        add_ref = refs[2] if has_add else None
        o_ref = refs[3] if has_add else refs[2]
        prod = _bdot(a_ref[...], b_ref[...], dims)
        if k_axis is None:
            if has_add:
                prod = prod + add_ref[...]
            o_ref[...] = prod.astype(o_ref.dtype)
        else:
            acc_ref = refs[-1]
            k = pl.program_id(k_axis)

            @pl.when(k == 0)
            def _():
                acc_ref[...] = prod

            @pl.when(k > 0)
            def _():
                acc_ref[...] += prod

            @pl.when(k == nk - 1)
            def _():
                r = acc_ref[...]
                if has_add:
                    r = r + add_ref[...]
                o_ref[...] = r.astype(o_ref.dtype)

    sem = tuple("arbitrary" if d == k_axis else "parallel" for d in range(len(grid)))
    in_specs = [a_spec, b_spec] + ([add_spec] if has_add else [])
    args = (a, b) + ((add,) if has_add else ())
    scratch = [pltpu.VMEM(acc_shape, F32)] if k_axis is not None else []
    return pl.pallas_call(body, out_shape=out_sds, grid=grid, in_specs=in_specs, out_specs=o_spec,
                          scratch_shapes=scratch, compiler_params=_params(sem), name=name)(*args)


def _rms(x, g):
    return x * lax.rsqrt(jnp.mean(x * x, axis=-1, keepdims=True) + EPS) * g


def rms_fwd(name, x, g, tm):
    rows, d = x.shape

    def body(x_ref, g_ref, o_ref):
        o_ref[...] = _rms(x_ref[...], g_ref[...]).astype(BF16)

    return pl.pallas_call(
        body, out_shape=jax.ShapeDtypeStruct((rows, d), BF16), grid=(rows // tm,),
        in_specs=[pl.BlockSpec((tm, d), lambda i: (i, 0)), pl.BlockSpec((1, d), lambda i: (0, 0))],
        out_specs=pl.BlockSpec((tm, d), lambda i: (i, 0)), compiler_params=_params(("parallel",)), name=name)(x, g)


def rms_bwd(name, x, g, dh, dres, tm):
    rows, d = x.shape

    def body(x_ref, g_ref, dh_ref, dres_ref, dx_ref, dg_ref):
        _, vjp = jax.vjp(_rms, x_ref[...], g_ref[...])
        dx, dg = vjp(dh_ref[...])
        dx_ref[...] = dx + dres_ref[...]

        @pl.when(pl.program_id(0) == 0)
        def _():
            dg_ref[...] = jnp.zeros_like(dg_ref)

        dg_ref[...] += dg

    row = pl.BlockSpec((tm, d), lambda i: (i, 0))
    vec = pl.BlockSpec((1, d), lambda i: (0, 0))
    return pl.pallas_call(
        body, out_shape=(jax.ShapeDtypeStruct((rows, d), F32), jax.ShapeDtypeStruct((1, d), F32)), grid=(rows // tm,),
        in_specs=[row, vec, row, row], out_specs=(row, vec), compiler_params=_params(("arbitrary",)), name=name)(x, g, dh, dres)


def final_loss(name, x, g, target, tm):
    rows, d = x.shape

    def body(x_ref, g_ref, t_ref, l_ref, dx_ref, dg_ref):
        def f(xv, gv):
            e = _rms(xv, gv) - t_ref[...]
            return 0.5 * jnp.sum(jnp.mean(e * e, axis=-1))

        loss, (dx, dg) = jax.value_and_grad(f, argnums=(0, 1))(x_ref[...], g_ref[...])
        dx_ref[...] = dx

        @pl.when(pl.program_id(0) == 0)
        def _():
            l_ref[...] = jnp.zeros_like(l_ref)
            dg_ref[...] = jnp.zeros_like(dg_ref)

        l_ref[...] += jnp.full(l_ref.shape, loss, F32)
        dg_ref[...] += dg

    row = pl.BlockSpec((tm, d), lambda i: (i, 0))
    vec = pl.BlockSpec((1, d), lambda i: (0, 0))
    lspec = pl.BlockSpec((1, 128), lambda i: (0, 0))
    return pl.pallas_call(
        body, out_shape=(jax.ShapeDtypeStruct((1, 128), F32), jax.ShapeDtypeStruct((rows, d), F32), jax.ShapeDtypeStruct((1, d), F32)),
        grid=(rows // tm,), in_specs=[row, vec, row], out_specs=(lspec, row, vec),
        compiler_params=_params(("arbitrary",)), name=name)(x, g, target)


def _rowwise(name, fn, row_ins, par_ins, n_row_out, row_out_dtypes, tm, with_grads=False):
    rows = row_ins[0][0].shape[0]
    n_prim = len(row_ins) - (n_row_out if with_grads else 0)
    n_par = len(par_ins)

    def body(*refs):
        ins = [r[...] for r in refs[:len(row_ins) + n_par]]
        outs = refs[len(row_ins) + n_par:]
        prim, cts, pars = ins[:n_prim], ins[n_prim:len(row_ins)], ins[len(row_ins):]
        if not with_grads:
            res = fn(*prim, *pars)
            for o_ref, r in zip(outs, res):
                o_ref[...] = r.astype(o_ref.dtype)
            return
        _, vjp = jax.vjp(fn, *prim, *[p.astype(F32) for p in pars])
        grads = vjp(tuple(cts))
        for o_ref, gr in zip(outs[:n_prim], grads[:n_prim]):
            o_ref[...] = gr.astype(o_ref.dtype)

        @pl.when(pl.program_id(0) == 0)
        def _():
            for o_ref in outs[n_prim:]:
                o_ref[...] = jnp.zeros_like(o_ref)

        for o_ref, gr in zip(outs[n_prim:], grads[n_prim:]):
            o_ref[...] += gr.astype(F32)

    in_specs = [pl.BlockSpec((tm, w), (lambda i, c=c: (i, c))) for (_, c, w) in row_ins]
    in_specs += [pl.BlockSpec(p.shape, (lambda i, n=p.ndim: (0,) * n)) for p in par_ins]
    args = [a for (a, _, _) in row_ins] + list(par_ins)
    if not with_grads:
        out_shape = tuple(jax.ShapeDtypeStruct((rows, w), dt) for (w, dt) in row_out_dtypes)
        out_specs = tuple(pl.BlockSpec((tm, w), lambda i: (i, 0)) for (w, _) in row_out_dtypes)
        sem = ("parallel",)
    else:
        out_shape = tuple(jax.ShapeDtypeStruct((rows, w), dt) for (w, dt) in row_out_dtypes)
        out_shape += tuple(jax.ShapeDtypeStruct(p.shape, F32) for p in par_ins)
        out_specs = tuple(pl.BlockSpec((tm, w), lambda i: (i, 0)) for (w, _) in row_out_dtypes)
        out_specs += tuple(pl.BlockSpec(p.shape, (lambda i, n=p.ndim: (0,) * n)) for p in par_ins)
        sem = ("arbitrary",)
    return pl.pallas_call(body, out_shape=out_shape, grid=(rows // tm,), in_specs=in_specs, out_specs=out_specs,
                          compiler_params=_params(sem), name=name)(*args)


def _glu(v, g):
    return (v * jax.nn.sigmoid(g),)


def _neg_expm1(z):
    return -jnp.tanh(0.5 * z) * (jnp.exp(z) + 1.0)


def _lru_gate(xc, w_r, w_i, b_r, b_i, lam):
    r = jax.nn.sigmoid(bdot(xc, w_r) + b_r)
    i = jax.nn.sigmoid(bdot(xc, w_i) + b_i)
    log_a = -LRU_C * r * jax.nn.softplus(-lam)
    a = jnp.exp(log_a)
    mult = jnp.sqrt(_neg_expm1(2.0 * log_a))
    return a, mult * (i * xc)


def _layernorm(x, g, b):
    mu = jnp.mean(x, axis=-1, keepdims=True)
    var = jnp.mean(jnp.square(x - mu), axis=-1, keepdims=True)
    return (x - mu) * lax.rsqrt(var + EPS) * g + b


def _mix_post(y_ssm, u, h1, hseq, lru_g, dgp, s5_d, w_glu, b_glu, ln_g, ln_b, w_pw, b_pw, pool_bd, pool_scale):
    y = y_ssm + s5_d * u
    gl = jax.nn.gelu(y, approximate=True)
    out_s5 = gl * jax.nn.sigmoid(bdot(gl, w_glu) + b_glu)
    out_cv = bdot(jax.nn.silu(_layernorm(h1, ln_g, ln_b)), w_pw) + b_pw
    out_lru = hseq * jax.nn.gelu(lru_g, approximate=True)
    out_pool = bdot(dgp, pool_bd) * pool_scale
    return (jnp.concatenate([out_s5, out_cv, out_lru, out_pool], axis=-1),)


def _ffn_act(gc, val):
    return (jax.nn.gelu(gc, approximate=True) * val,)


def ffn_act_fwd(name, gc, up, tm):
    _, rows, c = gc.shape

    def body(g_ref, v_ref, o_ref):
        o_ref[...] = _ffn_act(g_ref[...], v_ref[...])[0].astype(BF16)

    return pl.pallas_call(
        body, out_shape=jax.ShapeDtypeStruct((2, rows, c), BF16), grid=(2, rows // tm),
        in_specs=[pl.BlockSpec((None, tm, c), lambda h, i: (h, i, 0)), pl.BlockSpec((None, tm, c), lambda h, i: (h + 2, i, 0))],
        out_specs=pl.BlockSpec((None, tm, c), lambda h, i: (h, i, 0)),
        compiler_params=_params(("parallel", "parallel")), name=name)(gc, up)


def ffn_act_bwd(name, gc, up, dact, tm):
    _, rows, c = gc.shape

    def body(g_ref, v_ref, d_ref, dg_ref, dv_ref):
        _, vjp = jax.vjp(_ffn_act, g_ref[...], v_ref[...])
        dg, dv = vjp((d_ref[...],))
        dg_ref[...] = dg
        dv_ref[...] = dv.astype(BF16)

    blk = pl.BlockSpec((None, tm, c), lambda h, i: (h, i, 0))
    return pl.pallas_call(
        body, out_shape=(jax.ShapeDtypeStruct((2, rows, c), F32), jax.ShapeDtypeStruct((2, rows, c), BF16)), grid=(2, rows // tm),
        in_specs=[blk, pl.BlockSpec((None, tm, c), lambda h, i: (h + 2, i, 0)), blk], out_specs=(blk, blk),
        compiler_params=_params(("parallel", "parallel")), name=name)(gc, up, dact)


def _halo_rows(taps):
    return -(-(taps - 1) // SUBLANES) * SUBLANES


def dwconv_fwd(name, x, cblk, c, w, b, taps, tm, out_dtype=F32):
    nb = w.shape[0]
    rows = x.shape[1]
    halo = _halo_rows(taps)
    per = tm // halo

    def body(x_ref, h_ref, w_ref, b_ref, o_ref):
        i = pl.program_id(1)
        prev = jnp.where(i > 0, h_ref[...], 0.0)
        ext = jnp.concatenate([prev, x_ref[...]], axis=0)
        acc = jnp.broadcast_to(b_ref[...], (tm, c))
        for k in range(taps):
            off = halo - (taps - 1) + k
            acc = acc + w_ref[k:k + 1, :] * ext[off:off + tm]
        o_ref[...] = acc.astype(o_ref.dtype)

    return pl.pallas_call(
        body, out_shape=jax.ShapeDtypeStruct((nb, rows, c), out_dtype), grid=(nb, rows // tm),
        in_specs=[pl.BlockSpec((None, tm, c), lambda n, i: (n, i, cblk)),
                  pl.BlockSpec((None, halo, c), lambda n, i: (n, jnp.maximum(i * per - 1, 0), cblk)),
                  pl.BlockSpec((None, taps, c), lambda n, i: (n, 0, 0)),
                  pl.BlockSpec((None, 1, c), lambda n, i: (n, 0, 0))],
        out_specs=pl.BlockSpec((None, tm, c), lambda n, i: (n, i, 0)),
        compiler_params=_params(("parallel", "parallel")), name=name)(x, x, w, b)


def dwconv_bwd(name, dy, x, cblk, c, w, taps, tm, dx_dtype=F32):
    nb = w.shape[0]
    rows = x.shape[1]
    halo = _halo_rows(taps)
    per = tm // halo
    n_tiles = rows // tm
    last_halo = rows // halo - 1

    def body(dy_ref, dn_ref, x_ref, xp_ref, w_ref, dx_ref, dw_ref, db_ref):
        i = pl.program_id(1)
        dyv = dy_ref[...]
        nxt = jnp.where(i < n_tiles - 1, dn_ref[...], 0.0)
        dext = jnp.concatenate([dyv, nxt], axis=0)
        prev = jnp.where(i > 0, xp_ref[...], 0.0)
        xext = jnp.concatenate([prev, x_ref[...]], axis=0)
        acc = jnp.zeros((tm, c), F32)

        @pl.when(i == 0)
        def _():
            dw_ref[...] = jnp.zeros_like(dw_ref)
            db_ref[...] = jnp.zeros_like(db_ref)

        for k in range(taps):
            acc = acc + w_ref[k:k + 1, :] * dext[taps - 1 - k:taps - 1 - k + tm]
            off = halo - (taps - 1) + k
            dw_ref[k:k + 1, :] += jnp.sum(dyv * xext[off:off + tm], axis=0, keepdims=True)
        dx_ref[...] = acc.astype(dx_ref.dtype)
        db_ref[...] += jnp.sum(dyv, axis=0, keepdims=True)

    return pl.pallas_call(
        body, out_shape=(jax.ShapeDtypeStruct((nb, rows, c), dx_dtype), jax.ShapeDtypeStruct((nb, taps, c), F32),
                         jax.ShapeDtypeStruct((nb, 1, c), F32)),
        grid=(nb, n_tiles),
        in_specs=[pl.BlockSpec((None, tm, c), lambda n, i: (n, i, 0)),
                  pl.BlockSpec((None, halo, c), lambda n, i: (n, jnp.minimum((i + 1) * per, last_halo), 0)),
                  pl.BlockSpec((None, tm, c), lambda n, i: (n, i, cblk)),
                  pl.BlockSpec((None, halo, c), lambda n, i: (n, jnp.maximum(i * per - 1, 0), cblk)),
                  pl.BlockSpec((None, taps, c), lambda n, i: (n, 0, 0))],
        out_specs=(pl.BlockSpec((None, tm, c), lambda n, i: (n, i, 0)), pl.BlockSpec((None, taps, c), lambda n, i: (n, 0, 0)),
                   pl.BlockSpec((None, 1, c), lambda n, i: (n, 0, 0))),
        compiler_params=_params(("parallel", "arbitrary")), name=name)(dy, dy, x, x, w)


POOL_HALO = 16


def pool_fwd(name, proj, cblk, tm):
    rows = proj.shape[0]
    c = 128 * len(POOL_WINDOWS)
    per = tm // POOL_HALO

    def body(x_ref, h_ref, o_ref):
        i = pl.program_id(0)
        xv = x_ref[...]
        ext = jnp.concatenate([jnp.where(i > 0, h_ref[...], 0.0), xv], axis=0)
        t1 = (lax.broadcasted_iota(jnp.int32, (tm, 128), 0) + i * tm + 1).astype(F32)
        outs = []
        for gi, win in enumerate(POOL_WINDOWS):
            seg = ext[:, gi * 128:(gi + 1) * 128]
            s = seg[POOL_HALO:POOL_HALO + tm]
            for j in range(1, win):
                s = s + seg[POOL_HALO - j:POOL_HALO - j + tm]
            outs.append(s / jnp.minimum(t1, float(win)) - xv[:, gi * 128:(gi + 1) * 128])
        o_ref[...] = jnp.concatenate(outs, axis=-1)

    return pl.pallas_call(
        body, out_shape=jax.ShapeDtypeStruct((rows, c), F32), grid=(rows // tm,),
        in_specs=[pl.BlockSpec((tm, c), lambda i: (i, cblk)),
                  pl.BlockSpec((POOL_HALO, c), lambda i: (jnp.maximum(i * per - 1, 0), cblk))],
        out_specs=pl.BlockSpec((tm, c), lambda i: (i, 0)), compiler_params=_params(("parallel",)), name=name)(proj, proj)


def pool_bwd(name, dd, tm):
    rows, c = dd.shape
    per = tm // POOL_HALO
    n_tiles = rows // tm
    last_halo = rows // POOL_HALO - 1

    def body(d_ref, n_ref, o_ref):
        i = pl.program_id(0)
        dv = d_ref[...]
        nxt = jnp.where(i < n_tiles - 1, n_ref[...], 0.0)
        t1 = (lax.broadcasted_iota(jnp.int32, (tm, 128), 0) + i * tm + 1).astype(F32)
        t1n = (lax.broadcasted_iota(jnp.int32, (POOL_HALO, 128), 0) + (i + 1) * tm + 1).astype(F32)
        outs = []
        for gi, win in enumerate(POOL_WINDOWS):
            sl = slice(gi * 128, (gi + 1) * 128)
            q = jnp.concatenate([dv[:, sl] / jnp.minimum(t1, float(win)), nxt[:, sl] / jnp.minimum(t1n, float(win))], axis=0)
            s = q[0:tm]
            for j in range(1, win):
                s = s + q[j:j + tm]
            outs.append(s - dv[:, sl])
        o_ref[...] = jnp.concatenate(outs, axis=-1)

    return pl.pallas_call(
        body, out_shape=jax.ShapeDtypeStruct((rows, c), F32), grid=(n_tiles,),
        in_specs=[pl.BlockSpec((tm, c), lambda i: (i, 0)),
                  pl.BlockSpec((POOL_HALO, c), lambda i: (jnp.minimum((i + 1) * per, last_halo), 0))],
        out_specs=pl.BlockSpec((tm, c), lambda i: (i, 0)), compiler_params=_params(("parallel",)), name=name)(dd, dd)


def _shift_down(v, s, fill):
    r = lax.broadcasted_iota(jnp.int32, v.shape, 0)
    return jnp.where(r >= s, pltpu.roll(v, s, 0), fill)


def _shift_up(v, s, fill):
    n = v.shape[0]
    r = lax.broadcasted_iota(jnp.int32, v.shape, 0)
    return jnp.where(r < n - s, pltpu.roll(v, n - s, 0), fill)


def _cscan_chunk(vr, vi, powers, reverse):
    for k, (qr, qi) in enumerate(powers):
        s = 1 << k
        if reverse:
            sr, si = _shift_up(vr, s, 0.0), _shift_up(vi, s, 0.0)
            vr, vi = vr + qr * sr + qi * si, vi + qr * si - qi * sr
        else:
            sr, si = _shift_down(vr, s, 0.0), _shift_down(vi, s, 0.0)
            vr, vi = vr + qr * sr - qi * si, vi + qr * si + qi * sr
    return vr, vi


def _powers(pr, pi, n):
    out = [(pr, pi)]
    for _ in range(n - 1):
        pr, pi = pr * pr - pi * pi, 2.0 * pr * pi
        out.append((pr, pi))
    return out


def s5_scan_fwd(name, bu, a):
    _, rows, n = bu.shape
    t = min(SCAN_CHUNK, rows)
    steps = t.bit_length() - 1

    def body(bu_ref, a_ref, z_ref):
        pr, pi = a_ref[0], a_ref[1]
        powers = _powers(pr, pi, steps)
        r = lax.broadcasted_iota(jnp.int32, (t, 128), 0)
        tr, ti = _cscan_chunk(jnp.where(r == 0, pr, 0.0), jnp.where(r == 0, pi, 0.0), powers, False)

        def chunk(ci, carry):
            base = pl.multiple_of(ci * t, t)
            vr, vi = _cscan_chunk(bu_ref[0, pl.ds(base, t), :], bu_ref[1, pl.ds(base, t), :], powers, False)
            cr, cim = carry
            zr = vr + tr * cr - ti * cim
            zi = vi + tr * cim + ti * cr
            z_ref[0, pl.ds(base, t), :] = zr
            z_ref[1, pl.ds(base, t), :] = zi
            return zr[t - 1:t, :], zi[t - 1:t, :]

        zero = jnp.zeros((1, 128), F32)
        lax.fori_loop(0, rows // t, chunk, (zero, zero))

    return pl.pallas_call(
        body, out_shape=jax.ShapeDtypeStruct((2, rows, n), F32), grid=(n // 128,),
        in_specs=[pl.BlockSpec((2, rows, 128), lambda j: (0, 0, j)), pl.BlockSpec((2, 1, 128), lambda j: (0, 0, j))],
        out_specs=pl.BlockSpec((2, rows, 128), lambda j: (0, 0, j)), compiler_params=_params(("parallel",)), name=name)(bu, a)


def s5_scan_bwd(name, dz, z, a):
    _, rows, n = dz.shape
    t = min(SCAN_CHUNK, rows)
    steps = t.bit_length() - 1
    n_chunks = rows // t

    def body(dz_ref, z_ref, a_ref, lam_ref, da_ref):
        pr, pi = a_ref[0], a_ref[1]
        powers = _powers(pr, pi, steps)
        r = lax.broadcasted_iota(jnp.int32, (t, 128), 0)
        tr, ti = _cscan_chunk(jnp.where(r == t - 1, pr, 0.0), jnp.where(r == t - 1, -pi, 0.0), powers, True)

        def chunk(k, carry):
            ci = n_chunks - 1 - k
            base = pl.multiple_of(ci * t, t)
            vr, vi = _cscan_chunk(dz_ref[0, pl.ds(base, t), :], dz_ref[1, pl.ds(base, t), :], powers, True)
            cr, cim, dar, dai = carry
            lr = vr + tr * cr - ti * cim
            li = vi + tr * cim + ti * cr
            lam_ref[0, pl.ds(base, t), :] = lr
            lam_ref[1, pl.ds(base, t), :] = li
            pbase = pl.multiple_of(jnp.maximum(base - SUBLANES, 0), SUBLANES)
            keep = (ci > 0).astype(F32)
            pzr = z_ref[0, pl.ds(pbase, SUBLANES), :][SUBLANES - 1:SUBLANES, :] * keep
            pzi = z_ref[1, pl.ds(pbase, SUBLANES), :][SUBLANES - 1:SUBLANES, :] * keep
            zpr = _shift_down(z_ref[0, pl.ds(base, t), :], 1, pzr)
            zpi = _shift_down(z_ref[1, pl.ds(base, t), :], 1, pzi)
            dar = dar + jnp.sum(lr * zpr + li * zpi, axis=0, keepdims=True)
            dai = dai + jnp.sum(li * zpr - lr * zpi, axis=0, keepdims=True)
            return lr[0:1, :], li[0:1, :], dar, dai

        zero = jnp.zeros((1, 128), F32)
        _, _, dar, dai = lax.fori_loop(0, n_chunks, chunk, (zero, zero, zero, zero))
        da_ref[0] = dar
        da_ref[1] = dai

    seq = pl.BlockSpec((2, rows, 128), lambda j: (0, 0, j))
    vec = pl.BlockSpec((2, 1, 128), lambda j: (0, 0, j))
    return pl.pallas_call(
        body, out_shape=(jax.ShapeDtypeStruct((2, rows, n), F32), jax.ShapeDtypeStruct((2, 1, n), F32)), grid=(n // 128,),
        in_specs=[seq, seq, vec], out_specs=(seq, vec), compiler_params=_params(("parallel",)), name=name)(dz, z, a)


def _rscan_chunk(a, b, steps, reverse):
    shift = _shift_up if reverse else _shift_down
    for k in range(steps):
        s = 1 << k
        b = b + a * shift(b, s, 0.0)
        a = a * shift(a, s, 1.0)
    return a, b


def lru_scan_fwd(name, a, b):
    rows, n = a.shape
    t = min(SCAN_CHUNK, rows)
    steps = t.bit_length() - 1

    def body(a_ref, b_ref, h_ref):
        def chunk(ci, carry):
            base = pl.multiple_of(ci * t, t)
            pa, hb = _rscan_chunk(a_ref[pl.ds(base, t), :], b_ref[pl.ds(base, t), :], steps, False)
            h = hb + pa * carry
            h_ref[pl.ds(base, t), :] = h
            return h[t - 1:t, :]

        lax.fori_loop(0, rows // t, chunk, jnp.zeros((1, 128), F32))

    seq = pl.BlockSpec((rows, 128), lambda j: (0, j))
    return pl.pallas_call(body, out_shape=jax.ShapeDtypeStruct((rows, n), F32), grid=(n // 128,), in_specs=[seq, seq],
                          out_specs=seq, compiler_params=_params(("parallel",)), name=name)(a, b)


def lru_scan_bwd(name, dh, a, h):
    rows, n = a.shape
    t = min(SCAN_CHUNK, rows)
    steps = t.bit_length() - 1
    n_chunks = rows // t

    def body(dh_ref, a_ref, h_ref, da_ref, db_ref):
        def chunk(k, carry):
            ci = n_chunks - 1 - k
            base = pl.multiple_of(ci * t, t)
            nbase = pl.multiple_of(jnp.minimum(base + t, rows - SUBLANES), SUBLANES)
            a_next = a_ref[pl.ds(nbase, SUBLANES), :][0:1, :]
            an = _shift_up(a_ref[pl.ds(base, t), :], 1, a_next)
            pa, mb = _rscan_chunk(an, dh_ref[pl.ds(base, t), :], steps, True)
            mu = mb + pa * carry
            pbase = pl.multiple_of(jnp.maximum(base - SUBLANES, 0), SUBLANES)
            hp_row = h_ref[pl.ds(pbase, SUBLANES), :][SUBLANES - 1:SUBLANES, :] * (ci > 0).astype(F32)
            hp = _shift_down(h_ref[pl.ds(base, t), :], 1, hp_row)
            da_ref[pl.ds(base, t), :] = mu * hp
            db_ref[pl.ds(base, t), :] = mu
            return mu[0:1, :]

        lax.fori_loop(0, n_chunks, chunk, jnp.zeros((1, 128), F32))

    seq = pl.BlockSpec((rows, 128), lambda j: (0, j))
    return pl.pallas_call(
        body, out_shape=(jax.ShapeDtypeStruct((rows, n), F32), jax.ShapeDtypeStruct((rows, n), F32)), grid=(n // 128,),
        in_specs=[seq, seq, seq], out_specs=(seq, seq), compiler_params=_params(("parallel",)), name=name)(dh, a, h)


def _s5_param(lr, li, ls, bre, bim):
    st = jnp.exp(ls)
    er = jnp.exp(lr * st)
    th = li * st
    ar, ai = er * jnp.cos(th), er * jnp.sin(th)
    nr, ni = ar - 1.0, ai
    den = lr * lr + li * li
    cr, ci = (nr * lr + ni * li) / den, (ni * lr - nr * li) / den
    return ar, ai, cr * bre - ci * bim, cr * bim + ci * bre


def s5_param_fwd(name, lr, li, ls, bre, bim):
    gh, n = bre.shape

    def body(lr_ref, li_ref, ls_ref, bre_ref, bim_ref, a_ref, bb_ref):
        ar, ai, br, bi = _s5_param(lr_ref[...], li_ref[...], ls_ref[...], bre_ref[...], bim_ref[...])
        a_ref[0] = ar
        a_ref[1] = ai
        bb_ref[0] = br.astype(BF16)
        bb_ref[1] = bi.astype(BF16)

    return pl.pallas_call(body, out_shape=(jax.ShapeDtypeStruct((2, 1, n), F32), jax.ShapeDtypeStruct((2, gh, n), BF16)),
                          compiler_params=_params(), name=name)(lr, li, ls, bre, bim)


def s5_param_bwd(name, lr, li, ls, bre, bim, da, dbb, gsum):
    gh, n = bre.shape

    def body(lr_ref, li_ref, ls_ref, bre_ref, bim_ref, da_ref, dbb_ref, gs_ref, dlr_ref, dli_ref, dls_ref, dbre_ref, dbim_ref):
        _, vjp = jax.vjp(_s5_param, lr_ref[...], li_ref[...], ls_ref[...], bre_ref[...], bim_ref[...])
        dlr, dli, dls, dbre, dbim = vjp((da_ref[0], da_ref[1], dbb_ref[0], dbb_ref[1]))
        dlr_ref[...] = dlr
        dli_ref[...] = dli
        dls_ref[...] = jnp.dot(jnp.broadcast_to(dls, (SUBLANES, n)), gs_ref[...], preferred_element_type=F32,
                               precision=lax.Precision.HIGHEST)
        dbre_ref[...] = dbre
        dbim_ref[...] = dbim

    vec = jax.ShapeDtypeStruct((1, n), F32)
    mat = jax.ShapeDtypeStruct((gh, n), F32)
    return pl.pallas_call(body, out_shape=(vec, vec, jax.ShapeDtypeStruct((SUBLANES, 128), F32), mat, mat),
                          compiler_params=_params(), name=name)(lr, li, ls, bre, bim, da, dbb, gsum)


def sum_lead(name, x, tr):
    n, rows, cols = x.shape

    def body(x_ref, o_ref):
        acc = x_ref[0]
        for j in range(1, n):
            acc = acc + x_ref[j]
        o_ref[...] = acc

    return pl.pallas_call(
        body, out_shape=jax.ShapeDtypeStruct((rows, cols), x.dtype), grid=(rows // tr,),
        in_specs=[pl.BlockSpec((n, tr, cols), lambda i: (0, i, 0))], out_specs=pl.BlockSpec((tr, cols), lambda i: (i, 0)),
        compiler_params=_params(("parallel",)), name=name)(x)


def _adamw(w, g, m, v):
    m = ADAM_B1 * m + (1.0 - ADAM_B1) * g
    v = ADAM_B2 * v + (1.0 - ADAM_B2) * jnp.square(g)
    m_hat = m / (1.0 - ADAM_B1 ** ADAM_STEP)
    v_hat = v / (1.0 - ADAM_B2 ** ADAM_STEP)
    delta = -ADAM_LR * (m_hat / (jnp.sqrt(v_hat) + ADAM_EPS) + ADAM_WD * w)
    return delta, m, v


def adamw_sharded(name, w, m, v, g0, g1, split_cols, tile):
    _, r, c = w.shape
    if split_cols:
        nt = c // tile
        per = (c // 2) // tile
        wspec = pl.BlockSpec((None, r, tile), lambda l, t: (l, 0, t))
        gspec = pl.BlockSpec((None, r, tile), lambda l, t: (t // per, 0, t % per))
    else:
        nt = r // tile
        per = (r // 2) // tile
        wspec = pl.BlockSpec((None, tile, c), lambda l, t: (l, t, 0))
        gspec = pl.BlockSpec((None, tile, c), lambda l, t: (t // per, t % per, 0))

    def body(w_ref, m_ref, v_ref, g0_ref, g1_ref, g_ref, d_ref, nm_ref, nv_ref):
        g = jnp.where(pl.program_id(0) == 0, g0_ref[...], g1_ref[...])
        d, nm, nv = _adamw(w_ref[...], g, m_ref[...], v_ref[...])
        g_ref[...] = g
        d_ref[...] = d
        nm_ref[...] = nm
        nv_ref[...] = nv

    sds = jax.ShapeDtypeStruct(w.shape, F32)
    return pl.pallas_call(body, out_shape=(sds,) * 4, grid=(2, nt), in_specs=[wspec, wspec, wspec, gspec, gspec],
                          out_specs=(wspec,) * 4, compiler_params=_params(("parallel", "parallel")), name=name)(w, m, v, g0, g1)


def adamw_flat(name, w, g, m, v, tr):
    rows, cols = w.shape

    def body(w_ref, g_ref, m_ref, v_ref, d_ref, nm_ref, nv_ref):
        d, nm, nv = _adamw(w_ref[...], g_ref[...], m_ref[...], v_ref[...])
        d_ref[...] = d
        nm_ref[...] = nm
        nv_ref[...] = nv

    blk = pl.BlockSpec((tr, cols), lambda i: (i, 0))
    sds = jax.ShapeDtypeStruct((rows, cols), F32)
    return pl.pallas_call(body, out_shape=(sds,) * 3, grid=(rows // tr,), in_specs=[blk] * 4, out_specs=(blk,) * 3,
                          compiler_params=_params(("parallel",)), name=name)(w, g, m, v)


def _flips(axes):
    out = []
    for fx in ((0, 1) if "x" in axes else (0,)):
        for fy in ((0, 1) if "y" in axes else (0,)):
            for fc in ((0, 1) if "c" in axes else (0,)):
                if fx or fy or fc:
                    out.append((fx, fy, fc))
    return out


def _slot(pos, axes):
    s = 0
    for name, p in zip(("x", "y", "c"), pos):
        if name in axes:
            s = 2 * s + p
    return s


def _exchange(name, arrs, axes, scatter):
    flips = _flips(axes)
    n = len(flips) + 1
    na = len(arrs)

    def body(*refs):
        ins, outs = refs[:na], refs[na:2 * na]
        send_sems, recv_sems, local_sems = refs[2 * na:]
        me = (lax.axis_index("x"), lax.axis_index("y"), lax.axis_index("c"))
        my = _slot(me, axes)
        peers = [tuple((1 - p) if f else p for p, f in zip(me, fl)) for fl in flips]

        def src(a, dest_slot):
            return ins[a].at[dest_slot] if scatter else ins[a]

        local = [pltpu.make_async_copy(src(a, my), outs[a].at[my], local_sems.at[a]) for a in range(na)]
        for cp in local:
            cp.start()

        def remote(a, j, landing_slot, dest_slot):
            return pltpu.make_async_remote_copy(
                src_ref=src(a, dest_slot), dst_ref=outs[a].at[landing_slot], send_sem=send_sems.at[a * len(flips) + j],
                recv_sem=recv_sems.at[a * len(flips) + j], device_id=peers[j], device_id_type=pl.DeviceIdType.MESH)

        sends = [remote(a, j, my, _slot(peers[j], axes)) for a in range(na) for j in range(len(flips))]
        for cp in sends:
            cp.start()
        for a in range(na):
            for j in range(len(flips)):
                remote(a, j, _slot(peers[j], axes), _slot(peers[j], axes)).wait_recv()
        for cp in sends:
            cp.wait_send()
        for cp in local:
            cp.wait()

    if scatter:
        out_shape = tuple(jax.ShapeDtypeStruct(a.shape, a.dtype) for a in arrs)
    else:
        out_shape = tuple(jax.ShapeDtypeStruct((n,) + a.shape, a.dtype) for a in arrs)
    anyspec = pl.BlockSpec(memory_space=pl.ANY)
    return pl.pallas_call(
        body, out_shape=out_shape, in_specs=[anyspec] * na, out_specs=(anyspec,) * na,
        scratch_shapes=[pltpu.SemaphoreType.DMA((na * len(flips),)), pltpu.SemaphoreType.DMA((na * len(flips),)),
                        pltpu.SemaphoreType.DMA((na,))],
        name=name)(*arrs)


def all_gather(name, arrs, axes):
    return _exchange(name, arrs, axes, False)


def all_to_all(name, arrs, axes):
    return _exchange(name, arrs, axes, True)


def _block_diag(blocks):
    g, r, c = blocks.shape
    eye = jnp.eye(g, dtype=blocks.dtype)
    return (blocks[:, :, None, :] * eye[:, None, :, None]).reshape(g * r, g * c)


def _diag_blocks(mat, g):
    r, c = mat.shape[0] // g, mat.shape[1] // g
    eye = jnp.eye(g, dtype=mat.dtype)
    return (mat.reshape(g, r, g, c) * eye[:, None, :, None]).sum(axis=2)


def _halves(gfull, shards):
    rows, cols = gfull.shape
    return gfull.reshape(shards, 2, rows // shards // 2, cols).transpose(1, 0, 2, 3)


def _step(inp):
    x = inp['x'][0]
    target = inp['loss_target'][0]
    rows, d = x.shape
    depth = inp['w_in'].shape[0]
    mix_w = d // 4
    n_state = S5_GROUPS * S5_STATE
    ffn_half = inp['ffn_w_up'].shape[2]
    tm = min(512, rows)
    tc = min(256, rows)
    tl = min(512, rows)
    xy = ("x", "y")

    send, keys = [], []
    for l in range(depth):
        for nme in BIG:
            send.append(inp[nme][l].astype(BF16))
            keys.append((nme, l))
    for nme in SMALL_SHARDED:
        send.append(inp[nme])
        keys.append((nme, None))
    gathered = dict(zip(keys, all_gather("gather_weights", send, xy)))

    def full_small(nme):
        g = gathered[(nme, None)]
        return g.transpose(1, 2, 0, 3).reshape(g.shape[1], g.shape[2], 4 * g.shape[3])

    cv_w_dw, lru_w_conv, ffn_w_dw = full_small('cv_w_dw'), full_small('lru_w_conv'), full_small('ffn_w_dw')

    gsum = jnp.repeat(jnp.eye(128, dtype=F32)[:S5_GROUPS], S5_STATE, axis=0)

    saved = []
    grads = {nme: [None] * depth for nme in WEIGHTS}
    xcur = x
    for l in range(depth):
        w_in = gathered[('w_in', l)]
        w_out = gathered[('w_out', l)].reshape(d, d)
        w_up = gathered[('ffn_w_up', l)]
        w_down = gathered[('ffn_w_down', l)].reshape(2, ffn_half, d)
        w_glu = gathered[('s5_w_glu', l)].reshape(mix_w, mix_w)
        w_pw = gathered[('cv_w_pw', l)].reshape(mix_w, mix_w)
        ncol = w_in.shape[2]
        vec = lambda a: a[l].reshape(1, -1)

        lam_re, lam_im = vec(inp['s5_lam_re']), vec(inp['s5_lam_im'])
        log_step = jnp.broadcast_to(inp['s5_log_step'][l][:, None], (S5_GROUPS, S5_STATE)).reshape(1, n_state)
        b_re = _block_diag(inp['s5_b_re'][l].transpose(0, 2, 1))
        b_im = _block_diag(inp['s5_b_im'][l].transpose(0, 2, 1))
        c_cat = jnp.stack([_block_diag(inp['s5_c_re'][l].transpose(0, 2, 1)),
                           -_block_diag(inp['s5_c_im'][l].transpose(0, 2, 1))]).astype(BF16)
        a_bar, b_bar = s5_param_fwd(f"s5_param_fwd{l}", lam_re, lam_im, log_step, b_re, b_im)
        w_r = _block_diag(inp['lru_w_r'][l]).astype(BF16)
        w_i = _block_diag(inp['lru_w_i'][l]).astype(BF16)
        pool_bd = _block_diag(inp['pool_w'][l]).astype(BF16)
        post_pars = [vec(inp['s5_d']), w_glu, vec(inp['s5_b_glu']), vec(inp['cv_ln_g']), vec(inp['cv_ln_b']), w_pw,
                     vec(inp['cv_b_pw']), pool_bd, vec(inp['pool_scale'])]
        gate_pars = [w_r, w_i, vec(inp['lru_b_r']), vec(inp['lru_b_i']), vec(inp['lru_lam'])]

        h = rms_fwd(f"rms_mix{l}", xcur, vec(inp['norm_mix_g']), tm)
        proj = _mm(f"proj{l}", h, w_in, jax.ShapeDtypeStruct((rows, 4 * ncol), F32), (rows // tm, 4),
                   pl.BlockSpec((tm, d), lambda i, j: (i, 0)), pl.BlockSpec((None, d, ncol), lambda i, j: (j, 0, 0)),
                   pl.BlockSpec((tm, ncol), lambda i, j: (i, j)), NN)
        proj3 = proj.reshape(1, rows, 4 * ncol)
        nh = n_state // 2
        bu = _mm(f"s5_bu{l}", proj, b_bar, jax.ShapeDtypeStruct((2, rows, n_state), F32), (rows // tm, 2, 2),
                 pl.BlockSpec((tm, mix_w), lambda i, c, n: (i, 0)), pl.BlockSpec((None, mix_w, nh), lambda i, c, n: (c, 0, n)),
                 pl.BlockSpec((None, tm, nh), lambda i, c, n: (c, i, n)), NN)
        z = s5_scan_fwd(f"s5_scan{l}", bu, a_bar)
        y_ssm = _mm(f"s5_read{l}", z, c_cat, jax.ShapeDtypeStruct((rows, mix_w), F32), (rows // tm, 4),
                    pl.BlockSpec((None, tm, nh), lambda i, k: (k // 2, i, k % 2)),
                    pl.BlockSpec((None, nh, mix_w), lambda i, k: (k // 2, k % 2, 0)),
                    pl.BlockSpec((tm, mix_w), lambda i, k: (i, 0)), NN, k_axis=1)
        (h0,) = _rowwise(f"cv_glu{l}", _glu, [(proj, 1, mix_w), (proj, 2, mix_w)], [], 1, [(mix_w, F32)], tm)
        h1 = dwconv_fwd(f"cv_conv{l}", h0.reshape(1, rows, mix_w), 0, mix_w, cv_w_dw[l][None], vec(inp['cv_b_dw'])[None],
                        CV_TAPS, tc)[0]
        xc = dwconv_fwd(f"lru_conv{l}", proj3, 3, mix_w, lru_w_conv[l][None], vec(inp['lru_b_conv'])[None], LRU_TAPS, tc)[0]
        a_t, b_t = _rowwise(f"lru_gate{l}", _lru_gate, [(xc, 0, mix_w)], gate_pars, 2, [(mix_w, F32), (mix_w, F32)], tm)
        hseq = lru_scan_fwd(f"lru_scan{l}", a_t, b_t)
        dgp = pool_fwd(f"pool{l}", proj, 5, tc)
        post_rows = [(y_ssm, 0, mix_w), (proj, 0, mix_w), (h1, 0, mix_w), (hseq, 0, mix_w), (proj, 4, mix_w), (dgp, 0, mix_w)]
        (mixed,) = _rowwise(f"mix_post{l}", _mix_post, post_rows, post_pars, 1, [(d, BF16)], tm)
        x1 = _mm(f"out_proj{l}", mixed, w_out, jax.ShapeDtypeStruct((rows, d), F32), (rows // tm, 2, 4),
                 pl.BlockSpec((tm, mix_w), lambda i, j, k: (i, k)), pl.BlockSpec((mix_w, d // 2), lambda i, j, k: (k, j)),
                 pl.BlockSpec((tm, d // 2), lambda i, j, k: (i, j)), NN, k_axis=2,
                 add=xcur, add_spec=pl.BlockSpec((tm, d // 2), lambda i, j, k: (i, j)))

        h2 = rms_fwd(f"rms_ffn{l}", x1, vec(inp['norm_ffn_g']), tm)
        tu = min(256, rows)
        up = _mm(f"ffn_up{l}", h2, w_up, jax.ShapeDtypeStruct((4, rows, ffn_half), F32), (4, rows // tu),
                 pl.BlockSpec((tu, d), lambda k, i: (i, 0)), pl.BlockSpec((None, d, ffn_half), lambda k, i: (k, 0, 0)),
                 pl.BlockSpec((None, tu, ffn_half), lambda k, i: (k, i, 0)), NN)
        w_dw = ffn_w_dw[l].reshape(FFN_TAPS, 2, ffn_half).transpose(1, 0, 2)
        b_dw = inp['ffn_b_dw'][l].reshape(2, 1, ffn_half)
        gc = dwconv_fwd(f"ffn_conv{l}", up, 0, ffn_half, w_dw, b_dw, FFN_TAPS, tc)
        act = ffn_act_fwd(f"ffn_act{l}", gc, up, tc)
        x2 = _mm(f"ffn_down{l}", act, w_down, jax.ShapeDtypeStruct((rows, d), F32), (rows // tm, 4, 2),
                 pl.BlockSpec((None, tm, ffn_half), lambda i, j, k: (k, i, 0)), pl.BlockSpec((None, ffn_half, d // 4), lambda i, j, k: (k, 0, j)),
                 pl.BlockSpec((tm, d // 4), lambda i, j, k: (i, j)), NN, k_axis=2,
                 add=x1, add_spec=pl.BlockSpec((tm, d // 4), lambda i, j, k: (i, j)))
        saved.append(dict(x=xcur, h=h, proj=proj, z=z, y_ssm=y_ssm, h0=h0, h1=h1, xc=xc, a_t=a_t, hseq=hseq, dgp=dgp,
                          mixed=mixed, x1=x1, h2=h2, up=up, gc=gc, act=act, w_in=w_in, w_out=w_out, w_up=w_up, w_down=w_down,
                          a_bar=a_bar, b_bar=b_bar, c_cat=c_cat, post_pars=post_pars, gate_pars=gate_pars, w_dw=w_dw,
                          s5=(lam_re, lam_im, log_step, b_re, b_im), cv_w=cv_w_dw[l][None], lru_w=lru_w_conv[l][None]))
        xcur = x2

    loss_row, dx, dg_final = final_loss("final_loss", xcur, inp['norm_final_g'].reshape(1, d), target, tm)
    grads['norm_final_g'] = dg_final.reshape(d)

    big_g = {nme: [None] * depth for nme in BIG}
    for l in reversed(range(depth)):
        s = saved[l]
        ncol = s['w_in'].shape[2]
        nh = n_state // 2
        tu = min(256, rows)
        dact = _mm(f"d_act{l}", dx, s['w_down'], jax.ShapeDtypeStruct((2, rows, ffn_half), F32), (2, rows // tu),
                   pl.BlockSpec((tu, d), lambda k, i: (i, 0)), pl.BlockSpec((None, ffn_half, d), lambda k, i: (k, 0, 0)),
                   pl.BlockSpec((None, tu, ffn_half), lambda k, i: (k, i, 0)), NT)
        tn = d // 4
        big_g['ffn_w_down'][l] = _mm(
            f"dw_down{l}", s['act'], dx, jax.ShapeDtypeStruct((2, 2, ffn_half, d // 2), F32), (2, 4, rows // tl),
            pl.BlockSpec((None, tl, ffn_half), lambda hh, n, k: (hh, k, 0)), pl.BlockSpec((tl, tn), lambda hh, n, k: (k, n)),
            pl.BlockSpec((None, None, ffn_half, tn), lambda hh, n, k: (n // 2, hh, 0, n % 2)), TN, k_axis=2)
        dgc, dval = ffn_act_bwd(f"ffn_act_bwd{l}", s['gc'], s['up'], dact, tc)
        dgate, dw_dw, db_dw = dwconv_bwd(f"ffn_conv_bwd{l}", dgc, s['up'], 0, ffn_half, s['w_dw'], FFN_TAPS, tc, dx_dtype=BF16)
        grads['ffn_w_dw'][l] = dw_dw.transpose(1, 0, 2).reshape(FFN_TAPS, 2 * ffn_half)
        grads['ffn_b_dw'][l] = db_dw.reshape(2 * ffn_half)
        dup = jnp.concatenate([dgate, dval], axis=0)
        dh2 = _mm(f"d_h2{l}", dup, s['w_up'], jax.ShapeDtypeStruct((rows, d), F32), (rows // tm, 4, 4),
                  pl.BlockSpec((None, tm, ffn_half), lambda i, j, k: (k, i, 0)), pl.BlockSpec((None, d // 4, ffn_half), lambda i, j, k: (k, j, 0)),
                  pl.BlockSpec((tm, d // 4), lambda i, j, k: (i, j)), NT, k_axis=2)
        tmm = d // 4
        big_g['ffn_w_up'][l] = _mm(
            f"dw_up{l}", s['h2'], dup, jax.ShapeDtypeStruct((2, 4, d // 2, ffn_half), F32), (4, 4, rows // tl),
            pl.BlockSpec((tl, tmm), lambda k4, m, k: (k, m)), pl.BlockSpec((None, tl, ffn_half), lambda k4, m, k: (k4, k, 0)),
            pl.BlockSpec((None, None, tmm, ffn_half), lambda k4, m, k: (m // 2, k4, m % 2, 0)), TN, k_axis=2)
        dx1, dg = rms_bwd(f"rms_ffn_bwd{l}", s['x1'], inp['norm_ffn_g'][l].reshape(1, d), dh2, dx, tm)
        grads['norm_ffn_g'][l] = dg.reshape(d)
        dmixed = _mm(f"d_mixed{l}", dx1, s['w_out'], jax.ShapeDtypeStruct((rows, d), F32), (rows // tm, 4),
                     pl.BlockSpec((tm, d), lambda i, j: (i, 0)), pl.BlockSpec((d // 4, d), lambda i, j: (j, 0)),
                     pl.BlockSpec((tm, d // 4), lambda i, j: (i, j)), NT)
        tq = mix_w // 2
        big_g['w_out'][l] = _mm(
            f"dw_out{l}", s['mixed'], dx1, jax.ShapeDtypeStruct((2, 4, tq, d), F32), (8, rows // tl),
            pl.BlockSpec((tl, tq), lambda t, k: (k, t)), pl.BlockSpec((tl, d), lambda t, k: (k, 0)),
            pl.BlockSpec((None, None, tq, d), lambda t, k: (t % 2, t // 2, 0, 0)), TN, k_axis=1)
        post_rows = [(s['y_ssm'], 0, mix_w), (s['proj'], 0, mix_w), (s['h1'], 0, mix_w), (s['hseq'], 0, mix_w),
                     (s['proj'], 4, mix_w), (s['dgp'], 0, mix_w), (dmixed, 0, d)]
        res = _rowwise(f"mix_post_bwd{l}", _mix_post, post_rows, s['post_pars'], 1, [(mix_w, F32)] * 6, tm, with_grads=True)
        dy_ssm, du_dir, dh1, dhseq, dlru_g, ddgp = res[:6]
        dd, dwglu, dbglu, dlng, dlnb, dwpw, dbpw, dpoolbd, dscale = res[6:]
        grads['s5_d'][l], grads['s5_b_glu'][l] = dd.reshape(mix_w), dbglu.reshape(mix_w)
        grads['cv_ln_g'][l], grads['cv_ln_b'][l], grads['cv_b_pw'][l] = dlng.reshape(mix_w), dlnb.reshape(mix_w), dbpw.reshape(mix_w)
        grads['pool_w'][l] = _diag_blocks(dpoolbd, len(POOL_WINDOWS))
        grads['pool_scale'][l] = dscale.reshape(mix_w)
        big_g['s5_w_glu'][l] = _halves(dwglu, 4)
        big_g['cv_w_pw'][l] = _halves(dwpw, 4)
        dz = _mm(f"s5_dz{l}", dy_ssm, s['c_cat'], jax.ShapeDtypeStruct((2, rows, n_state), F32), (rows // tm, 2, 2),
                 pl.BlockSpec((tm, mix_w), lambda i, c, n: (i, 0)), pl.BlockSpec((None, nh, mix_w), lambda i, c, n: (c, n, 0)),
                 pl.BlockSpec((None, tm, nh), lambda i, c, n: (c, i, n)), NT)
        dccat = _mm(f"s5_dc{l}", s['z'], dy_ssm, jax.ShapeDtypeStruct((2, n_state, mix_w), F32), (2, n_state // mix_w, rows // tl),
                    pl.BlockSpec((None, tl, mix_w), lambda c, m, k: (c, k, m)), pl.BlockSpec((tl, mix_w), lambda c, m, k: (k, 0)),
                    pl.BlockSpec((None, mix_w, mix_w), lambda c, m, k: (c, m, 0)), TN, k_axis=2)
        grads['s5_c_re'][l] = _diag_blocks(dccat[0], S5_GROUPS).transpose(0, 2, 1)
        grads['s5_c_im'][l] = -_diag_blocks(dccat[1], S5_GROUPS).transpose(0, 2, 1)
        lam, da_bar = s5_scan_bwd(f"s5_scan_bwd{l}", dz, s['z'], s['a_bar'])
        du = _mm(f"s5_du{l}", lam, s['b_bar'], jax.ShapeDtypeStruct((rows, mix_w), F32), (rows // tm, 4),
                 pl.BlockSpec((None, tm, nh), lambda i, k: (k // 2, i, k % 2)), pl.BlockSpec((None, mix_w, nh), lambda i, k: (k // 2, 0, k % 2)),
                 pl.BlockSpec((tm, mix_w), lambda i, k: (i, 0)), NT, k_axis=1,
                 add=du_dir, add_spec=pl.BlockSpec((tm, mix_w), lambda i, k: (i, 0)))
        dbbar = _mm(f"s5_db{l}", s['proj'], lam, jax.ShapeDtypeStruct((2, mix_w, n_state), F32), (2, 2, rows // tl),
                    pl.BlockSpec((tl, mix_w), lambda c, n, k: (k, 0)), pl.BlockSpec((None, tl, nh), lambda c, n, k: (c, k, n)),
                    pl.BlockSpec((None, mix_w, nh), lambda c, n, k: (c, 0, n)), TN, k_axis=2)
        dlr, dli, dls, dbre, dbim = s5_param_bwd(f"s5_param_bwd{l}", *s['s5'], da_bar, dbbar, gsum)
        grads['s5_lam_re'][l] = dlr.reshape(S5_GROUPS, S5_STATE)
        grads['s5_lam_im'][l] = dli.reshape(S5_GROUPS, S5_STATE)
        grads['s5_log_step'][l] = dls[0, :S5_GROUPS]
        grads['s5_b_re'][l] = _diag_blocks(dbre, S5_GROUPS).transpose(0, 2, 1)
        grads['s5_b_im'][l] = _diag_blocks(dbim, S5_GROUPS).transpose(0, 2, 1)
        dh0, dw_cv, db_cv = dwconv_bwd(f"cv_conv_bwd{l}", dh1.reshape(1, rows, mix_w), s['h0'].reshape(1, rows, mix_w), 0, mix_w,
                                       s['cv_w'], CV_TAPS, tc)
        grads['cv_w_dw'][l], grads['cv_b_dw'][l] = dw_cv[0], db_cv.reshape(mix_w)
        dv, dgg = _rowwise(f"cv_glu_bwd{l}", _glu, [(s['proj'], 1, mix_w), (s['proj'], 2, mix_w), (dh0[0], 0, mix_w)], [], 1,
                           [(mix_w, F32)] * 2, tm, with_grads=True)
        da_t, db_t = lru_scan_bwd(f"lru_scan_bwd{l}", dhseq, s['a_t'], s['hseq'])
        res = _rowwise(f"lru_gate_bwd{l}", _lru_gate, [(s['xc'], 0, mix_w), (da_t, 0, mix_w), (db_t, 0, mix_w)], s['gate_pars'], 2,
                       [(mix_w, F32)], tm, with_grads=True)
        dxc, dwr, dwi, dbr, dbi, dlam = res
        grads['lru_w_r'][l], grads['lru_w_i'][l] = _diag_blocks(dwr, LRU_HEADS), _diag_blocks(dwi, LRU_HEADS)
        grads['lru_b_r'][l], grads['lru_b_i'][l], grads['lru_lam'][l] = dbr.reshape(mix_w), dbi.reshape(mix_w), dlam.reshape(mix_w)
        dlx, dw_lc, db_lc = dwconv_bwd(f"lru_conv_bwd{l}", dxc.reshape(1, rows, mix_w), s['proj'].reshape(1, rows, 4 * ncol), 3, mix_w,
                                       s['lru_w'], LRU_TAPS, tc)
        grads['lru_w_conv'][l], grads['lru_b_conv'][l] = dw_lc[0], db_lc.reshape(mix_w)
        dpx = pool_bwd(f"pool_bwd{l}", ddgp, tc)
        dproj = jnp.concatenate([du, dv, dgg, dlx[0], dlru_g, dpx], axis=-1)
        dh = _mm(f"d_h{l}", dproj, s['w_in'], jax.ShapeDtypeStruct((rows, d), F32), (rows // tm, 4, 4),
                 pl.BlockSpec((tm, ncol), lambda i, j, k: (i, k)), pl.BlockSpec((None, d // 4, ncol), lambda i, j, k: (k, j, 0)),
                 pl.BlockSpec((tm, d // 4), lambda i, j, k: (i, j)), NT, k_axis=2)
        big_g['w_in'][l] = _mm(
            f"dw_in{l}", s['h'], dproj, jax.ShapeDtypeStruct((2, 4, d // 2, ncol), F32), (4, 4, rows // tl),
            pl.BlockSpec((tl, tmm), lambda k4, m, k: (k, m)), pl.BlockSpec((tl, ncol), lambda k4, m, k: (k, k4)),
            pl.BlockSpec((None, None, tmm, ncol), lambda k4, m, k: (m // 2, k4, m % 2, 0)), TN, k_axis=2)
        dx, dg = rms_bwd(f"rms_mix_bwd{l}", s['x'], inp['norm_mix_g'][l].reshape(1, d), dh, dx1, tm)
        grads['norm_mix_g'][l] = dg.reshape(d)

    big_keys = [(nme, l) for nme in BIG for l in range(depth)]
    g_list = []
    for nme, l in big_keys:
        g = big_g[nme][l]
        if nme == 'ffn_w_down':
            g = g.reshape(2, 4, ffn_half // 2, d // 2)
        g_list.append(g)
    r1 = all_to_all("reduce_cores", g_list, ("c",))
    s_list = [sum_lead(f"sum_cores_{nme}{l}", r.reshape(2, -1, r.shape[-1]), _row_tile(r.shape[1] * r.shape[2], 128)).reshape(r.shape[1:])
              for (nme, l), r in zip(big_keys, r1)]
    r2 = all_to_all("reduce_chips", s_list, xy)
    t_list = [sum_lead(f"sum_chips_{nme}{l}", r, _row_tile(r.shape[1], 128)) for (nme, l), r in zip(big_keys, r2)]
    t_full = dict(zip(big_keys, all_gather("share_cores", t_list, ("c",))))

    outs = {}
    tiles = {'w_in': 256, 'w_out': 128, 'ffn_w_up': 128, 'ffn_w_down': 256, 's5_w_glu': 64, 'cv_w_pw': 64}
    for nme in BIG:
        g0, g1 = t_full[(nme, 0)], t_full[(nme, 1)]
        outs[nme] = adamw_sharded(f"adamw_{nme}", inp[nme], inp['m_' + nme], inp['v_' + nme], g0, g1, nme == 'ffn_w_down', tiles[nme])

    small = [nme for nme in WEIGHTS if nme not in BIG]
    full_g = {nme: (grads[nme] if nme == 'norm_final_g' else jnp.stack(grads[nme])) for nme in small}
    flat = jnp.concatenate([full_g[nme].reshape(-1) for nme in small])
    n_flat = flat.shape[0]
    pad = (-n_flat) % (128 * 64)
    packed = jnp.pad(flat, (0, pad)).reshape(-1, 128)
    (g8,) = all_gather("gather_small", [packed], ("x", "y", "c"))
    gsum_small = sum_lead("sum_small", g8, 64).reshape(-1)
    my_chip = 2 * lax.axis_index("x") + lax.axis_index("y")
    red, off = {}, 0
    for nme in small:
        g = gsum_small[off:off + full_g[nme].size].reshape(full_g[nme].shape)
        off += full_g[nme].size
        if nme in SMALL_SHARDED:
            width = inp[nme].shape[2]
            g = lax.dynamic_slice_in_dim(g, my_chip * width, width, axis=2)
        red[nme] = g

    def pack(tree):
        f = jnp.concatenate([tree[nme].reshape(-1) for nme in small])
        return jnp.pad(f, (0, (-f.shape[0]) % (128 * 64))).reshape(-1, 128)

    pd, pm, pv = adamw_flat("adamw_small", pack({n_: inp[n_] for n_ in small}), pack(red), pack({n_: inp['m_' + n_] for n_ in small}),
                            pack({n_: inp['v_' + n_] for n_ in small}), 64)
    off = 0
    for nme in small:
        size, shape = inp[nme].size, inp[nme].shape
        outs[nme] = (red[nme],) + tuple(p.reshape(-1)[off:off + size].reshape(shape) for p in (pd, pm, pv))
        off += size

    loss = lax.psum(loss_row[0, 0], ("x", "y", "c"))
    result = [loss, dx[None]]
    for part in range(4):
        result += [outs[nme][part] for nme in WEIGHTS]
    return tuple(result)


def kernel(x, norm_mix_g, w_in, s5_lam_re, s5_lam_im, s5_log_step, s5_b_re, s5_b_im, s5_c_re, s5_c_im, s5_d, s5_w_glu, s5_b_glu, cv_w_dw, cv_b_dw, cv_ln_g, cv_ln_b, cv_w_pw, cv_b_pw, lru_w_conv, lru_b_conv, lru_w_r, lru_b_r, lru_w_i, lru_b_i, lru_lam, pool_w, pool_scale, w_out, norm_ffn_g, ffn_w_up, ffn_w_dw, ffn_b_dw, ffn_w_down, norm_final_g, loss_target, m_norm_mix_g, m_w_in, m_s5_lam_re, m_s5_lam_im, m_s5_log_step, m_s5_b_re, m_s5_b_im, m_s5_c_re, m_s5_c_im, m_s5_d, m_s5_w_glu, m_s5_b_glu, m_cv_w_dw, m_cv_b_dw, m_cv_ln_g, m_cv_ln_b, m_cv_w_pw, m_cv_b_pw, m_lru_w_conv, m_lru_b_conv, m_lru_w_r, m_lru_b_r, m_lru_w_i, m_lru_b_i, m_lru_lam, m_pool_w, m_pool_scale, m_w_out, m_norm_ffn_g, m_ffn_w_up, m_ffn_w_dw, m_ffn_b_dw, m_ffn_w_down, m_norm_final_g, v_norm_mix_g, v_w_in, v_s5_lam_re, v_s5_lam_im, v_s5_log_step, v_s5_b_re, v_s5_b_im, v_s5_c_re, v_s5_c_im, v_s5_d, v_s5_w_glu, v_s5_b_glu, v_cv_w_dw, v_cv_b_dw, v_cv_ln_g, v_cv_ln_b, v_cv_w_pw, v_cv_b_pw, v_lru_w_conv, v_lru_b_conv, v_lru_w_r, v_lru_b_r, v_lru_w_i, v_lru_b_i, v_lru_lam, v_pool_w, v_pool_scale, v_w_out, v_norm_ffn_g, v_ffn_w_up, v_ffn_w_dw, v_ffn_b_dw, v_ffn_w_down, v_norm_final_g):
    inp = dict(locals())
    return _step(inp)
```

```python
import functools

import jax
import jax.numpy as jnp
from jax import lax
from jax.experimental import pallas as pl
from jax.experimental.pallas import tpu as pltpu

F32 = jnp.float32
BF16 = jnp.bfloat16

VMEM_LIMIT_BYTES = 56 * 1024 * 1024
SUBLANES = 8

EPS = 1e-6
S5_GROUPS, S5_STATE, S5_GROUP_CH = 32, 64, 16
LRU_HEADS, LRU_C = 8, 8.0
POOL_WINDOWS = (2, 4, 8, 16)
CV_TAPS, LRU_TAPS, FFN_TAPS = 31, 4, 3
SCAN_CHUNK = 64

ADAM_LR, ADAM_B1, ADAM_B2, ADAM_EPS, ADAM_WD, ADAM_STEP = 0.001, 0.9, 0.999, 1e-08, 0.01, 10

NN = ((1,), (0,))
NT = ((1,), (1,))
TN = ((0,), (0,))

WEIGHTS = ['norm_mix_g', 'w_in', 's5_lam_re', 's5_lam_im', 's5_log_step', 's5_b_re', 's5_b_im', 's5_c_re', 's5_c_im',
           's5_d', 's5_w_glu', 's5_b_glu', 'cv_w_dw', 'cv_b_dw', 'cv_ln_g', 'cv_ln_b', 'cv_w_pw', 'cv_b_pw',
           'lru_w_conv', 'lru_b_conv', 'lru_w_r', 'lru_b_r', 'lru_w_i', 'lru_b_i', 'lru_lam', 'pool_w', 'pool_scale',
           'w_out', 'norm_ffn_g', 'ffn_w_up', 'ffn_w_dw', 'ffn_b_dw', 'ffn_w_down', 'norm_final_g']
BIG = ('w_in', 'w_out', 'ffn_w_up', 'ffn_w_down', 's5_w_glu', 'cv_w_pw')
SMALL_SHARDED = {'cv_w_dw': 2, 'lru_w_conv': 2, 'ffn_w_dw': 2}


def _params(sem=None):
    if sem is None:
        return pltpu.CompilerParams(vmem_limit_bytes=VMEM_LIMIT_BYTES)
    return pltpu.CompilerParams(dimension_semantics=sem, vmem_limit_bytes=VMEM_LIMIT_BYTES)


def _row_tile(rows, cap):
    best = SUBLANES
    for t in range(SUBLANES, min(rows, cap) + 1, SUBLANES):
        if rows % t == 0:
            best = t
    return best


def _bdot(a, b, dims=NN):
    return lax.dot_general(a.astype(BF16), b.astype(BF16), (dims, ((), ())), preferred_element_type=F32)


@jax.custom_vjp
def bdot(a, b):
    return _bdot(a, b)


def _bdot_fwd(a, b):
    return _bdot(a, b), (a, b)


def _bdot_bwd(res, g):
    a, b = res
    return _bdot(g, b, NT).astype(a.dtype), _bdot(a, g, TN).astype(b.dtype)


bdot.defvjp(_bdot_fwd, _bdot_bwd)


def _mm(name, a, b, out_sds, grid, a_spec, b_spec, o_spec, dims, k_axis=None, add=None, add_spec=None):
    nk = grid[k_axis] if k_axis is not None else 1
    has_add = add is not None
    acc_shape = tuple(d for d in o_spec.block_shape if d is not None)

    def body(*refs):
        a_ref, b_ref = refs[0], refs[1]
        add_ref = refs[2] if has_add else None
        o_ref = refs[3] if has_add else refs[2]
        prod = _bdot(a_ref[...], b_ref[...], dims)
        if k_axis is None:
            if has_add:
                prod = prod + add_ref[...]
            o_ref[...] = prod.astype(o_ref.dtype)
        else:
            acc_ref = refs[-1]
            k = pl.program_id(k_axis)

            @pl.when(k == 0)
            def _():
                acc_ref[...] = prod

            @pl.when(k > 0)
            def _():
                acc_ref[...] += prod

            @pl.when(k == nk - 1)
            def _():
                r = acc_ref[...]
                if has_add:
                    r = r + add_ref[...]
                o_ref[...] = r.astype(o_ref.dtype)

    sem = tuple("arbitrary" if d == k_axis else "parallel" for d in range(len(grid)))
    in_specs = [a_spec, b_spec] + ([add_spec] if has_add else [])
    args = (a, b) + ((add,) if has_add else ())
    scratch = [pltpu.VMEM(acc_shape, F32)] if k_axis is not None else []
    return pl.pallas_call(body, out_shape=out_sds, grid=grid, in_specs=in_specs, out_specs=o_spec,
                          scratch_shapes=scratch, compiler_params=_params(sem), name=name)(*args)


def _rms(x, g):
    return x * lax.rsqrt(jnp.mean(x * x, axis=-1, keepdims=True) + EPS) * g


def rms_fwd(name, x, g, tm):
    rows, d = x.shape

    def body(x_ref, g_ref, o_ref):
        o_ref[...] = _rms(x_ref[...], g_ref[...]).astype(BF16)

    return pl.pallas_call(
        body, out_shape=jax.ShapeDtypeStruct((rows, d), BF16), grid=(rows // tm,),
        in_specs=[pl.BlockSpec((tm, d), lambda i: (i, 0)), pl.BlockSpec((1, d), lambda i: (0, 0))],
        out_specs=pl.BlockSpec((tm, d), lambda i: (i, 0)), compiler_params=_params(("parallel",)), name=name)(x, g)


def rms_bwd(name, x, g, dh, dres, tm):
    rows, d = x.shape

    def body(x_ref, g_ref, dh_ref, dres_ref, dx_ref, dg_ref):
        _, vjp = jax.vjp(_rms, x_ref[...], g_ref[...])
        dx, dg = vjp(dh_ref[...])
        dx_ref[...] = dx + dres_ref[...]

        @pl.when(pl.program_id(0) == 0)
        def _():
            dg_ref[...] = jnp.zeros_like(dg_ref)

        dg_ref[...] += dg

    row = pl.BlockSpec((tm, d), lambda i: (i, 0))
    vec = pl.BlockSpec((1, d), lambda i: (0, 0))
    return pl.pallas_call(
        body, out_shape=(jax.ShapeDtypeStruct((rows, d), F32), jax.ShapeDtypeStruct((1, d), F32)), grid=(rows // tm,),
        in_specs=[row, vec, row, row], out_specs=(row, vec), compiler_params=_params(("arbitrary",)), name=name)(x, g, dh, dres)


def final_loss(name, x, g, target, tm):
    rows, d = x.shape

    def body(x_ref, g_ref, t_ref, l_ref, dx_ref, dg_ref):
        def f(xv, gv):
            e = _rms(xv, gv) - t_ref[...]
            return 0.5 * jnp.sum(jnp.mean(e * e, axis=-1))

        loss, (dx, dg) = jax.value_and_grad(f, argnums=(0, 1))(x_ref[...], g_ref[...])
        dx_ref[...] = dx

        @pl.when(pl.program_id(0) == 0)
        def _():
            l_ref[...] = jnp.zeros_like(l_ref)
            dg_ref[...] = jnp.zeros_like(dg_ref)

        l_ref[...] += jnp.full(l_ref.shape, loss, F32)
        dg_ref[...] += dg

    row = pl.BlockSpec((tm, d), lambda i: (i, 0))
    vec = pl.BlockSpec((1, d), lambda i: (0, 0))
    lspec = pl.BlockSpec((1, 128), lambda i: (0, 0))
    return pl.pallas_call(
        body, out_shape=(jax.ShapeDtypeStruct((1, 128), F32), jax.ShapeDtypeStruct((rows, d), F32), jax.ShapeDtypeStruct((1, d), F32)),
        grid=(rows // tm,), in_specs=[row, vec, row], out_specs=(lspec, row, vec),
        compiler_params=_params(("arbitrary",)), name=name)(x, g, target)


def _rowwise(name, fn, row_ins, par_ins, n_row_out, row_out_dtypes, tm, with_grads=False):
    rows = row_ins[0][0].shape[0]
    n_prim = len(row_ins) - (n_row_out if with_grads else 0)
    n_par = len(par_ins)

    def body(*refs):
        ins = [r[...] for r in refs[:len(row_ins) + n_par]]
        outs = refs[len(row_ins) + n_par:]
        prim, cts, pars = ins[:n_prim], ins[n_prim:len(row_ins)], ins[len(row_ins):]
        if not with_grads:
            res = fn(*prim, *pars)
            for o_ref, r in zip(outs, res):
                o_ref[...] = r.astype(o_ref.dtype)
            return
        _, vjp = jax.vjp(fn, *prim, *[p.astype(F32) for p in pars])
        grads = vjp(tuple(cts))
        for o_ref, gr in zip(outs[:n_prim], grads[:n_prim]):
            o_ref[...] = gr.astype(o_ref.dtype)

        @pl.when(pl.program_id(0) == 0)
        def _():
            for o_ref in outs[n_prim:]:
                o_ref[...] = jnp.zeros_like(o_ref)

        for o_ref, gr in zip(outs[n_prim:], grads[n_prim:]):
            o_ref[...] += gr.astype(F32)

    in_specs = [pl.BlockSpec((tm, w), (lambda i, c=c: (i, c))) for (_, c, w) in row_ins]
    in_specs += [pl.BlockSpec(p.shape, (lambda i, n=p.ndim: (0,) * n)) for p in par_ins]
    args = [a for (a, _, _) in row_ins] + list(par_ins)
    if not with_grads:
        out_shape = tuple(jax.ShapeDtypeStruct((rows, w), dt) for (w, dt) in row_out_dtypes)
        out_specs = tuple(pl.BlockSpec((tm, w), lambda i: (i, 0)) for (w, _) in row_out_dtypes)
        sem = ("parallel",)
    else:
        out_shape = tuple(jax.ShapeDtypeStruct((rows, w), dt) for (w, dt) in row_out_dtypes)
        out_shape += tuple(jax.ShapeDtypeStruct(p.shape, F32) for p in par_ins)
        out_specs = tuple(pl.BlockSpec((tm, w), lambda i: (i, 0)) for (w, _) in row_out_dtypes)
        out_specs += tuple(pl.BlockSpec(p.shape, (lambda i, n=p.ndim: (0,) * n)) for p in par_ins)
        sem = ("arbitrary",)
    return pl.pallas_call(body, out_shape=out_shape, grid=(rows // tm,), in_specs=in_specs, out_specs=out_specs,
                          compiler_params=_params(sem), name=name)(*args)


def _glu(v, g):
    return (v * jax.nn.sigmoid(g),)


def _neg_expm1(z):
    return -jnp.tanh(0.5 * z) * (jnp.exp(z) + 1.0)


def _lru_gate(xc, w_r, w_i, b_r, b_i, lam):
    r = jax.nn.sigmoid(bdot(xc, w_r) + b_r)
    i = jax.nn.sigmoid(bdot(xc, w_i) + b_i)
    log_a = -LRU_C * r * jax.nn.softplus(-lam)
    a = jnp.exp(log_a)
    mult = jnp.sqrt(_neg_expm1(2.0 * log_a))
    return a, mult * (i * xc)


def _layernorm(x, g, b):
    mu = jnp.mean(x, axis=-1, keepdims=True)
    var = jnp.mean(jnp.square(x - mu), axis=-1, keepdims=True)
    return (x - mu) * lax.rsqrt(var + EPS) * g + b


def _mix_post(y_ssm, u, h1, hseq, lru_g, dgp, s5_d, w_glu, b_glu, ln_g, ln_b, w_pw, b_pw, pool_bd, pool_scale):
    y = y_ssm + s5_d * u
    gl = jax.nn.gelu(y, approximate=True)
    out_s5 = gl * jax.nn.sigmoid(bdot(gl, w_glu) + b_glu)
    out_cv = bdot(jax.nn.silu(_layernorm(h1, ln_g, ln_b)), w_pw) + b_pw
    out_lru = hseq * jax.nn.gelu(lru_g, approximate=True)
    out_pool = bdot(dgp, pool_bd) * pool_scale
    return (jnp.concatenate([out_s5, out_cv, out_lru, out_pool], axis=-1),)


def _ffn_act(gc, val):
    return (jax.nn.gelu(gc, approximate=True) * val,)


def ffn_act_fwd(name, gc, up, tm):
    _, rows, c = gc.shape

    def body(g_ref, v_ref, o_ref):
        o_ref[...] = _ffn_act(g_ref[...], v_ref[...])[0].astype(BF16)

    return pl.pallas_call(
        body, out_shape=jax.ShapeDtypeStruct((2, rows, c), BF16), grid=(2, rows // tm),
        in_specs=[pl.BlockSpec((None, tm, c), lambda h, i: (h, i, 0)), pl.BlockSpec((None, tm, c), lambda h, i: (h + 2, i, 0))],
        out_specs=pl.BlockSpec((None, tm, c), lambda h, i: (h, i, 0)),
        compiler_params=_params(("parallel", "parallel")), name=name)(gc, up)


def ffn_act_bwd(name, gc, up, dact, tm):
    _, rows, c = gc.shape

    def body(g_ref, v_ref, d_ref, dg_ref, dv_ref):
        _, vjp = jax.vjp(_ffn_act, g_ref[...], v_ref[...])
        dg, dv = vjp((d_ref[...],))
        dg_ref[...] = dg
        dv_ref[...] = dv.astype(BF16)

    blk = pl.BlockSpec((None, tm, c), lambda h, i: (h, i, 0))
    return pl.pallas_call(
        body, out_shape=(jax.ShapeDtypeStruct((2, rows, c), F32), jax.ShapeDtypeStruct((2, rows, c), BF16)), grid=(2, rows // tm),
        in_specs=[blk, pl.BlockSpec((None, tm, c), lambda h, i: (h + 2, i, 0)), blk], out_specs=(blk, blk),
        compiler_params=_params(("parallel", "parallel")), name=name)(gc, up, dact)


def _halo_rows(taps):
    return -(-(taps - 1) // SUBLANES) * SUBLANES


def dwconv_fwd(name, x, cblk, c, w, b, taps, tm, out_dtype=F32):
    nb = w.shape[0]
    rows = x.shape[1]
    halo = _halo_rows(taps)
    per = tm // halo

    def body(x_ref, h_ref, w_ref, b_ref, o_ref):
        i = pl.program_id(1)
        prev = jnp.where(i > 0, h_ref[...], 0.0)
        ext = jnp.concatenate([prev, x_ref[...]], axis=0)
        acc = jnp.broadcast_to(b_ref[...], (tm, c))
        for k in range(taps):
            off = halo - (taps - 1) + k
            acc = acc + w_ref[k:k + 1, :] * ext[off:off + tm]
        o_ref[...] = acc.astype(o_ref.dtype)

    return pl.pallas_call(
        body, out_shape=jax.ShapeDtypeStruct((nb, rows, c), out_dtype), grid=(nb, rows // tm),
        in_specs=[pl.BlockSpec((None, tm, c), lambda n, i: (n, i, cblk)),
                  pl.BlockSpec((None, halo, c), lambda n, i: (n, jnp.maximum(i * per - 1, 0), cblk)),
                  pl.BlockSpec((None, taps, c), lambda n, i: (n, 0, 0)),
                  pl.BlockSpec((None, 1, c), lambda n, i: (n, 0, 0))],
        out_specs=pl.BlockSpec((None, tm, c), lambda n, i: (n, i, 0)),
        compiler_params=_params(("parallel", "parallel")), name=name)(x, x, w, b)


def dwconv_bwd(name, dy, x, cblk, c, w, taps, tm, dx_dtype=F32):
    nb = w.shape[0]
    rows = x.shape[1]
    halo = _halo_rows(taps)
    per = tm // halo
    n_tiles = rows // tm
    last_halo = rows // halo - 1

    def body(dy_ref, dn_ref, x_ref, xp_ref, w_ref, dx_ref, dw_ref, db_ref):
        i = pl.program_id(1)
        dyv = dy_ref[...]
        nxt = jnp.where(i < n_tiles - 1, dn_ref[...], 0.0)
        dext = jnp.concatenate([dyv, nxt], axis=0)
        prev = jnp.where(i > 0, xp_ref[...], 0.0)
        xext = jnp.concatenate([prev, x_ref[...]], axis=0)
        acc = jnp.zeros((tm, c), F32)

        @pl.when(i == 0)
        def _():
            dw_ref[...] = jnp.zeros_like(dw_ref)
            db_ref[...] = jnp.zeros_like(db_ref)

        for k in range(taps):
            acc = acc + w_ref[k:k + 1, :] * dext[taps - 1 - k:taps - 1 - k + tm]
            off = halo - (taps - 1) + k
            dw_ref[k:k + 1, :] += jnp.sum(dyv * xext[off:off + tm], axis=0, keepdims=True)
        dx_ref[...] = acc.astype(dx_ref.dtype)
        db_ref[...] += jnp.sum(dyv, axis=0, keepdims=True)

    return pl.pallas_call(
        body, out_shape=(jax.ShapeDtypeStruct((nb, rows, c), dx_dtype), jax.ShapeDtypeStruct((nb, taps, c), F32),
                         jax.ShapeDtypeStruct((nb, 1, c), F32)),
        grid=(nb, n_tiles),
        in_specs=[pl.BlockSpec((None, tm, c), lambda n, i: (n, i, 0)),
                  pl.BlockSpec((None, halo, c), lambda n, i: (n, jnp.minimum((i + 1) * per, last_halo), 0)),
                  pl.BlockSpec((None, tm, c), lambda n, i: (n, i, cblk)),
                  pl.BlockSpec((None, halo, c), lambda n, i: (n, jnp.maximum(i * per - 1, 0), cblk)),
                  pl.BlockSpec((None, taps, c), lambda n, i: (n, 0, 0))],
        out_specs=(pl.BlockSpec((None, tm, c), lambda n, i: (n, i, 0)), pl.BlockSpec((None, taps, c), lambda n, i: (n, 0, 0)),
                   pl.BlockSpec((None, 1, c), lambda n, i: (n, 0, 0))),
        compiler_params=_params(("parallel", "arbitrary")), name=name)(dy, dy, x, x, w)


POOL_HALO = 16


def pool_fwd(name, proj, cblk, tm):
    rows = proj.shape[0]
    c = 128 * len(POOL_WINDOWS)
    per = tm // POOL_HALO

    def body(x_ref, h_ref, o_ref):
        i = pl.program_id(0)
        xv = x_ref[...]
        ext = jnp.concatenate([jnp.where(i > 0, h_ref[...], 0.0), xv], axis=0)
        t1 = (lax.broadcasted_iota(jnp.int32, (tm, 128), 0) + i * tm + 1).astype(F32)
        outs = []
        for gi, win in enumerate(POOL_WINDOWS):
            seg = ext[:, gi * 128:(gi + 1) * 128]
            s = seg[POOL_HALO:POOL_HALO + tm]
            for j in range(1, win):
                s = s + seg[POOL_HALO - j:POOL_HALO - j + tm]
            outs.append(s / jnp.minimum(t1, float(win)) - xv[:, gi * 128:(gi + 1) * 128])
        o_ref[...] = jnp.concatenate(outs, axis=-1)

    return pl.pallas_call(
        body, out_shape=jax.ShapeDtypeStruct((rows, c), F32), grid=(rows // tm,),
        in_specs=[pl.BlockSpec((tm, c), lambda i: (i, cblk)),
                  pl.BlockSpec((POOL_HALO, c), lambda i: (jnp.maximum(i * per - 1, 0), cblk))],
        out_specs=pl.BlockSpec((tm, c), lambda i: (i, 0)), compiler_params=_params(("parallel",)), name=name)(proj, proj)


def pool_bwd(name, dd, tm):
    rows, c = dd.shape
    per = tm // POOL_HALO
    n_tiles = rows // tm
    last_halo = rows // POOL_HALO - 1

    def body(d_ref, n_ref, o_ref):
        i = pl.program_id(0)
        dv = d_ref[...]
        nxt = jnp.where(i < n_tiles - 1, n_ref[...], 0.0)
        t1 = (lax.broadcasted_iota(jnp.int32, (tm, 128), 0) + i * tm + 1).astype(F32)
        t1n = (lax.broadcasted_iota(jnp.int32, (POOL_HALO, 128), 0) + (i + 1) * tm + 1).astype(F32)
        outs = []
        for gi, win in enumerate(POOL_WINDOWS):
            sl = slice(gi * 128, (gi + 1) * 128)
            q = jnp.concatenate([dv[:, sl] / jnp.minimum(t1, float(win)), nxt[:, sl] / jnp.minimum(t1n, float(win))], axis=0)
            s = q[0:tm]
            for j in range(1, win):
                s = s + q[j:j + tm]
            outs.append(s - dv[:, sl])
        o_ref[...] = jnp.concatenate(outs, axis=-1)

    return pl.pallas_call(
        body, out_shape=jax.ShapeDtypeStruct((rows, c), F32), grid=(n_tiles,),
        in_specs=[pl.BlockSpec((tm, c), lambda i: (i, 0)),
                  pl.BlockSpec((POOL_HALO, c), lambda i: (jnp.minimum((i + 1) * per, last_halo), 0))],
        out_specs=pl.BlockSpec((tm, c), lambda i: (i, 0)), compiler_params=_params(("parallel",)), name=name)(dd, dd)


def _shift_down(v, s, fill):
    r = lax.broadcasted_iota(jnp.int32, v.shape, 0)
    return jnp.where(r >= s, pltpu.roll(v, s, 0), fill)


def _shift_up(v, s, fill):
    n = v.shape[0]
    r = lax.broadcasted_iota(jnp.int32, v.shape, 0)
    return jnp.where(r < n - s, pltpu.roll(v, n - s, 0), fill)


def _cscan_chunk(vr, vi, powers, reverse):
    for k, (qr, qi) in enumerate(powers):
        s = 1 << k
        if reverse:
            sr, si = _shift_up(vr, s, 0.0), _shift_up(vi, s, 0.0)
            vr, vi = vr + qr * sr + qi * si, vi + qr * si - qi * sr
        else:
            sr, si = _shift_down(vr, s, 0.0), _shift_down(vi, s, 0.0)
            vr, vi = vr + qr * sr - qi * si, vi + qr * si + qi * sr
    return vr, vi


def _powers(pr, pi, n):
    out = [(pr, pi)]
    for _ in range(n - 1):
        pr, pi = pr * pr - pi * pi, 2.0 * pr * pi
        out.append((pr, pi))
    return out


def s5_scan_fwd(name, bu, a):
    _, rows, n = bu.shape
    t = min(SCAN_CHUNK, rows)
    steps = t.bit_length() - 1

    def body(bu_ref, a_ref, z_ref):
        pr, pi = a_ref[0], a_ref[1]
        powers = _powers(pr, pi, steps)
        r = lax.broadcasted_iota(jnp.int32, (t, 128), 0)
        tr, ti = _cscan_chunk(jnp.where(r == 0, pr, 0.0), jnp.where(r == 0, pi, 0.0), powers, False)

        def chunk(ci, carry):
            base = pl.multiple_of(ci * t, t)
            vr, vi = _cscan_chunk(bu_ref[0, pl.ds(base, t), :], bu_ref[1, pl.ds(base, t), :], powers, False)
            cr, cim = carry
            zr = vr + tr * cr - ti * cim
            zi = vi + tr * cim + ti * cr
            z_ref[0, pl.ds(base, t), :] = zr
            z_ref[1, pl.ds(base, t), :] = zi
            return zr[t - 1:t, :], zi[t - 1:t, :]

        zero = jnp.zeros((1, 128), F32)
        lax.fori_loop(0, rows // t, chunk, (zero, zero))

    return pl.pallas_call(
        body, out_shape=jax.ShapeDtypeStruct((2, rows, n), F32), grid=(n // 128,),
        in_specs=[pl.BlockSpec((2, rows, 128), lambda j: (0, 0, j)), pl.BlockSpec((2, 1, 128), lambda j: (0, 0, j))],
        out_specs=pl.BlockSpec((2, rows, 128), lambda j: (0, 0, j)), compiler_params=_params(("parallel",)), name=name)(bu, a)


def s5_scan_bwd(name, dz, z, a):
    _, rows, n = dz.shape
    t = min(SCAN_CHUNK, rows)
    steps = t.bit_length() - 1
    n_chunks = rows // t

    def body(dz_ref, z_ref, a_ref, lam_ref, da_ref):
        pr, pi = a_ref[0], a_ref[1]
        powers = _powers(pr, pi, steps)
        r = lax.broadcasted_iota(jnp.int32, (t, 128), 0)
        tr, ti = _cscan_chunk(jnp.where(r == t - 1, pr, 0.0), jnp.where(r == t - 1, -pi, 0.0), powers, True)

        def chunk(k, carry):
            ci = n_chunks - 1 - k
            base = pl.multiple_of(ci * t, t)
            vr, vi = _cscan_chunk(dz_ref[0, pl.ds(base, t), :], dz_ref[1, pl.ds(base, t), :], powers, True)
            cr, cim, dar, dai = carry
            lr = vr + tr * cr - ti * cim
            li = vi + tr * cim + ti * cr
            lam_ref[0, pl.ds(base, t), :] = lr
            lam_ref[1, pl.ds(base, t), :] = li
            pbase = pl.multiple_of(jnp.maximum(base - SUBLANES, 0), SUBLANES)
            keep = (ci > 0).astype(F32)
            pzr = z_ref[0, pl.ds(pbase, SUBLANES), :][SUBLANES - 1:SUBLANES, :] * keep
            pzi = z_ref[1, pl.ds(pbase, SUBLANES), :][SUBLANES - 1:SUBLANES, :] * keep
            zpr = _shift_down(z_ref[0, pl.ds(base, t), :], 1, pzr)
            zpi = _shift_down(z_ref[1, pl.ds(base, t), :], 1, pzi)
            dar = dar + jnp.sum(lr * zpr + li * zpi, axis=0, keepdims=True)
            dai = dai + jnp.sum(li * zpr - lr * zpi, axis=0, keepdims=True)
            return lr[0:1, :], li[0:1, :], dar, dai

        zero = jnp.zeros((1, 128), F32)
        _, _, dar, dai = lax.fori_loop(0, n_chunks, chunk, (zero, zero, zero, zero))
        da_ref[0] = dar
        da_ref[1] = dai

    seq = pl.BlockSpec((2, rows, 128), lambda j: (0, 0, j))
    vec = pl.BlockSpec((2, 1, 128), lambda j: (0, 0, j))
    return pl.pallas_call(
        body, out_shape=(jax.ShapeDtypeStruct((2, rows, n), F32), jax.ShapeDtypeStruct((2, 1, n), F32)), grid=(n // 128,),
        in_specs=[seq, seq, vec], out_specs=(seq, vec), compiler_params=_params(("parallel",)), name=name)(dz, z, a)


def _rscan_chunk(a, b, steps, reverse):
    shift = _shift_up if reverse else _shift_down
    for k in range(steps):
        s = 1 << k
        b = b + a * shift(b, s, 0.0)
        a = a * shift(a, s, 1.0)
    return a, b


def lru_scan_fwd(name, a, b):
    rows, n = a.shape
    t = min(SCAN_CHUNK, rows)
    steps = t.bit_length() - 1

    def body(a_ref, b_ref, h_ref):
        def chunk(ci, carry):
            base = pl.multiple_of(ci * t, t)
            pa, hb = _rscan_chunk(a_ref[pl.ds(base, t), :], b_ref[pl.ds(base, t), :], steps, False)
            h = hb + pa * carry
            h_ref[pl.ds(base, t), :] = h
            return h[t - 1:t, :]

        lax.fori_loop(0, rows // t, chunk, jnp.zeros((1, 128), F32))

    seq = pl.BlockSpec((rows, 128), lambda j: (0, j))
    return pl.pallas_call(body, out_shape=jax.ShapeDtypeStruct((rows, n), F32), grid=(n // 128,), in_specs=[seq, seq],
                          out_specs=seq, compiler_params=_params(("parallel",)), name=name)(a, b)


def lru_scan_bwd(name, dh, a, h):
    rows, n = a.shape
    t = min(SCAN_CHUNK, rows)
    steps = t.bit_length() - 1
    n_chunks = rows // t

    def body(dh_ref, a_ref, h_ref, da_ref, db_ref):
        def chunk(k, carry):
            ci = n_chunks - 1 - k
            base = pl.multiple_of(ci * t, t)
            nbase = pl.multiple_of(jnp.minimum(base + t, rows - SUBLANES), SUBLANES)
            a_next = a_ref[pl.ds(nbase, SUBLANES), :][0:1, :]
            an = _shift_up(a_ref[pl.ds(base, t), :], 1, a_next)
            pa, mb = _rscan_chunk(an, dh_ref[pl.ds(base, t), :], steps, True)
            mu = mb + pa * carry
            pbase = pl.multiple_of(jnp.maximum(base - SUBLANES, 0), SUBLANES)
            hp_row = h_ref[pl.ds(pbase, SUBLANES), :][SUBLANES - 1:SUBLANES, :] * (ci > 0).astype(F32)
            hp = _shift_down(h_ref[pl.ds(base, t), :], 1, hp_row)
            da_ref[pl.ds(base, t), :] = mu * hp
            db_ref[pl.ds(base, t), :] = mu
            return mu[0:1, :]

        lax.fori_loop(0, n_chunks, chunk, jnp.zeros((1, 128), F32))

    seq = pl.BlockSpec((rows, 128), lambda j: (0, j))
    return pl.pallas_call(
        body, out_shape=(jax.ShapeDtypeStruct((rows, n), F32), jax.ShapeDtypeStruct((rows, n), F32)), grid=(n // 128,),
        in_specs=[seq, seq, seq], out_specs=(seq, seq), compiler_params=_params(("parallel",)), name=name)(dh, a, h)


def _s5_param(lr, li, ls, bre, bim):
    st = jnp.exp(ls)
    er = jnp.exp(lr * st)
    th = li * st
    ar, ai = er * jnp.cos(th), er * jnp.sin(th)
    nr, ni = ar - 1.0, ai
    den = lr * lr + li * li
    cr, ci = (nr * lr + ni * li) / den, (ni * lr - nr * li) / den
    return ar, ai, cr * bre - ci * bim, cr * bim + ci * bre


def s5_param_fwd(name, lr, li, ls, bre, bim):
    gh, n = bre.shape

    def body(lr_ref, li_ref, ls_ref, bre_ref, bim_ref, a_ref, bb_ref):
        ar, ai, br, bi = _s5_param(lr_ref[...], li_ref[...], ls_ref[...], bre_ref[...], bim_ref[...])
        a_ref[0] = ar
        a_ref[1] = ai
        bb_ref[0] = br.astype(BF16)
        bb_ref[1] = bi.astype(BF16)

    return pl.pallas_call(body, out_shape=(jax.ShapeDtypeStruct((2, 1, n), F32), jax.ShapeDtypeStruct((2, gh, n), BF16)),
                          compiler_params=_params(), name=name)(lr, li, ls, bre, bim)


def s5_param_bwd(name, lr, li, ls, bre, bim, da, dbb, gsum):
    gh, n = bre.shape

    def body(lr_ref, li_ref, ls_ref, bre_ref, bim_ref, da_ref, dbb_ref, gs_ref, dlr_ref, dli_ref, dls_ref, dbre_ref, dbim_ref):
        _, vjp = jax.vjp(_s5_param, lr_ref[...], li_ref[...], ls_ref[...], bre_ref[...], bim_ref[...])
        dlr, dli, dls, dbre, dbim = vjp((da_ref[0], da_ref[1], dbb_ref[0], dbb_ref[1]))
        dlr_ref[...] = dlr
        dli_ref[...] = dli
        dls_ref[...] = jnp.dot(jnp.broadcast_to(dls, (SUBLANES, n)), gs_ref[...], preferred_element_type=F32,
                               precision=lax.Precision.HIGHEST)
        dbre_ref[...] = dbre
        dbim_ref[...] = dbim

    vec = jax.ShapeDtypeStruct((1, n), F32)
    mat = jax.ShapeDtypeStruct((gh, n), F32)
    return pl.pallas_call(body, out_shape=(vec, vec, jax.ShapeDtypeStruct((SUBLANES, 128), F32), mat, mat),
                          compiler_params=_params(), name=name)(lr, li, ls, bre, bim, da, dbb, gsum)


def sum_lead(name, x, tr):
    n, rows, cols = x.shape

    def body(x_ref, o_ref):
        acc = x_ref[0]
        for j in range(1, n):
            acc = acc + x_ref[j]
        o_ref[...] = acc

    return pl.pallas_call(
        body, out_shape=jax.ShapeDtypeStruct((rows, cols), x.dtype), grid=(rows // tr,),
        in_specs=[pl.BlockSpec((n, tr, cols), lambda i: (0, i, 0))], out_specs=pl.BlockSpec((tr, cols), lambda i: (i, 0)),
        compiler_params=_params(("parallel",)), name=name)(x)


def _adamw(w, g, m, v):
    m = ADAM_B1 * m + (1.0 - ADAM_B1) * g
    v = ADAM_B2 * v + (1.0 - ADAM_B2) * jnp.square(g)
    m_hat = m / (1.0 - ADAM_B1 ** ADAM_STEP)
    v_hat = v / (1.0 - ADAM_B2 ** ADAM_STEP)
    delta = -ADAM_LR * (m_hat / (jnp.sqrt(v_hat) + ADAM_EPS) + ADAM_WD * w)
    return delta, m, v


def adamw_sharded(name, w, m, v, g0, g1, split_cols, tile):
    _, r, c = w.shape
    if split_cols:
        nt = c // tile
        per = (c // 2) // tile
        wspec = pl.BlockSpec((None, r, tile), lambda l, t: (l, 0, t))
        gspec = pl.BlockSpec((None, r, tile), lambda l, t: (t // per, 0, t % per))
    else:
        nt = r // tile
        per = (r // 2) // tile
        wspec = pl.BlockSpec((None, tile, c), lambda l, t: (l, t, 0))
        gspec = pl.BlockSpec((None, tile, c), lambda l, t: (t // per, t % per, 0))

    def body(w_ref, m_ref, v_ref, g0_ref, g1_ref, g_ref, d_ref, nm_ref, nv_ref):
        g = jnp.where(pl.program_id(0) == 0, g0_ref[...], g1_ref[...])
        d, nm, nv = _adamw(w_ref[...], g, m_ref[...], v_ref[...])
        g_ref[...] = g
        d_ref[...] = d
        nm_ref[...] = nm
        nv_ref[...] = nv

    sds = jax.ShapeDtypeStruct(w.shape, F32)
    return pl.pallas_call(body, out_shape=(sds,) * 4, grid=(2, nt), in_specs=[wspec, wspec, wspec, gspec, gspec],
                          out_specs=(wspec,) * 4, compiler_params=_params(("parallel", "parallel")), name=name)(w, m, v, g0, g1)


def adamw_flat(name, w, g, m, v, tr):
    rows, cols = w.shape

    def body(w_ref, g_ref, m_ref, v_ref, d_ref, nm_ref, nv_ref):
        d, nm, nv = _adamw(w_ref[...], g_ref[...], m_ref[...], v_ref[...])
        d_ref[...] = d
        nm_ref[...] = nm
        nv_ref[...] = nv

    blk = pl.BlockSpec((tr, cols), lambda i: (i, 0))
    sds = jax.ShapeDtypeStruct((rows, cols), F32)
    return pl.pallas_call(body, out_shape=(sds,) * 3, grid=(rows // tr,), in_specs=[blk] * 4, out_specs=(blk,) * 3,
                          compiler_params=_params(("parallel",)), name=name)(w, g, m, v)


def _flips(axes):
    out = []
    for fx in ((0, 1) if "x" in axes else (0,)):
        for fy in ((0, 1) if "y" in axes else (0,)):
            for fc in ((0, 1) if "c" in axes else (0,)):
                if fx or fy or fc:
                    out.append((fx, fy, fc))
    return out


def _slot(pos, axes):
    s = 0
    for name, p in zip(("x", "y", "c"), pos):
        if name in axes:
            s = 2 * s + p
    return s


def _exchange(name, arrs, axes, scatter):
    flips = _flips(axes)
    n = len(flips) + 1
    na = len(arrs)

    def body(*refs):
        ins, outs = refs[:na], refs[na:2 * na]
        send_sems, recv_sems, local_sems = refs[2 * na:]
        me = (lax.axis_index("x"), lax.axis_index("y"), lax.axis_index("c"))
        my = _slot(me, axes)
        peers = [tuple((1 - p) if f else p for p, f in zip(me, fl)) for fl in flips]

        def src(a, dest_slot):
            return ins[a].at[dest_slot] if scatter else ins[a]

        local = [pltpu.make_async_copy(src(a, my), outs[a].at[my], local_sems.at[a]) for a in range(na)]
        for cp in local:
            cp.start()

        def remote(a, j, landing_slot, dest_slot):
            return pltpu.make_async_remote_copy(
                src_ref=src(a, dest_slot), dst_ref=outs[a].at[landing_slot], send_sem=send_sems.at[a * len(flips) + j],
                recv_sem=recv_sems.at[a * len(flips) + j], device_id=peers[j], device_id_type=pl.DeviceIdType.MESH)

        sends = [remote(a, j, my, _slot(peers[j], axes)) for a in range(na) for j in range(len(flips))]
        for cp in sends:
            cp.start()
        for a in range(na):
            for j in range(len(flips)):
                remote(a, j, _slot(peers[j], axes), _slot(peers[j], axes)).wait_recv()
        for cp in sends:
            cp.wait_send()
        for cp in local:
            cp.wait()

    if scatter:
        out_shape = tuple(jax.ShapeDtypeStruct(a.shape, a.dtype) for a in arrs)
    else:
        out_shape = tuple(jax.ShapeDtypeStruct((n,) + a.shape, a.dtype) for a in arrs)
    anyspec = pl.BlockSpec(memory_space=pl.ANY)
    return pl.pallas_call(
        body, out_shape=out_shape, in_specs=[anyspec] * na, out_specs=(anyspec,) * na,
        scratch_shapes=[pltpu.SemaphoreType.DMA((na * len(flips),)), pltpu.SemaphoreType.DMA((na * len(flips),)),
                        pltpu.SemaphoreType.DMA((na,))],
        name=name)(*arrs)


def all_gather(name, arrs, axes):
    return _exchange(name, arrs, axes, False)


def all_to_all(name, arrs, axes):
    return _exchange(name, arrs, axes, True)


def _pair_exchange(name, ins, in_specs, n_steps, tile, fn_send, fn_out, out_shape, out_spec):
    n_in = len(ins)

    def body(*refs):
        in_refs, o_ref = refs[:n_in], refs[n_in]
        send_buf, recv_buf, send_sems, recv_sems, credit = refs[n_in + 1:]
        i = pl.program_id(0)
        slot = lax.rem(i, 2)
        c = lax.axis_index("c")
        sibling = (lax.axis_index("x"), lax.axis_index("y"), 1 - c)
        vals = [r[...] for r in in_refs]
        send_buf[slot] = fn_send(*vals, c)

        @pl.when(i >= 2)
        def _():
            pl.semaphore_wait(credit, 1)

        copy = pltpu.make_async_remote_copy(
            src_ref=send_buf.at[slot], dst_ref=recv_buf.at[slot], send_sem=send_sems.at[slot], recv_sem=recv_sems.at[slot],
            device_id=sibling, device_id_type=pl.DeviceIdType.MESH)
        copy.start()
        copy.wait_recv()
        o_ref[...] = fn_out(*vals, recv_buf[slot], c).astype(o_ref.dtype)
        copy.wait_send()

        @pl.when(i < n_steps - 2)
        def _():
            pl.semaphore_signal(credit, inc=1, device_id=sibling, device_id_type=pl.DeviceIdType.MESH)

    return pl.pallas_call(
        body, out_shape=out_shape, grid=(n_steps,), in_specs=in_specs, out_specs=out_spec,
        scratch_shapes=[pltpu.VMEM((2,) + tile, F32), pltpu.VMEM((2,) + tile, F32), pltpu.SemaphoreType.DMA((2,)),
                        pltpu.SemaphoreType.DMA((2,)), pltpu.SemaphoreType.REGULAR],
        compiler_params=_params(("arbitrary",)), name=name)(*ins)


def _tile_rows(rows, cols):
    return _row_tile(rows, max(SUBLANES, (3 << 19) // (4 * cols)))


def reduce_cores(name, g):
    _, m, cols = g.shape
    tr = _tile_rows(m, cols)

    def fn_send(g0, g1, c):
        return jnp.where(c == 0, g1, g0)

    def fn_out(g0, g1, got, c):
        return jnp.where(c == 0, g0, g1) + got

    return _pair_exchange(
        name, [g, g], [pl.BlockSpec((None, tr, cols), lambda i: (0, i, 0)), pl.BlockSpec((None, tr, cols), lambda i: (1, i, 0))],
        m // tr, (tr, cols), fn_send, fn_out, jax.ShapeDtypeStruct((m, cols), BF16), pl.BlockSpec((tr, cols), lambda i: (i, 0)))


def sum_and_share(name, parts):
    n, r, cols = parts.shape
    tr = _tile_rows(r, cols)

    def total(p):
        acc = p[0].astype(F32)
        for j in range(1, n):
            acc = acc + p[j].astype(F32)
        return acc

    def fn_send(p, c):
        return total(p)

    def fn_out(p, got, c):
        mine = total(p)
        return jnp.stack([jnp.where(c == 0, mine, got), jnp.where(c == 0, got, mine)])

    return _pair_exchange(
        name, [parts], [pl.BlockSpec((n, tr, cols), lambda i: (0, i, 0))], r // tr, (tr, cols), fn_send, fn_out,
        jax.ShapeDtypeStruct((2, r, cols), F32), pl.BlockSpec((2, tr, cols), lambda i: (0, i, 0)))


def _block_diag(blocks):
    g, r, c = blocks.shape
    eye = jnp.eye(g, dtype=blocks.dtype)
    return (blocks[:, :, None, :] * eye[:, None, :, None]).reshape(g * r, g * c)


def _diag_blocks(mat, g):
    r, c = mat.shape[0] // g, mat.shape[1] // g
    eye = jnp.eye(g, dtype=mat.dtype)
    return (mat.reshape(g, r, g, c) * eye[:, None, :, None]).sum(axis=2)


def _halves(gfull, shards):
    rows, cols = gfull.shape
    return gfull.reshape(shards, 2, rows // shards // 2, cols).transpose(1, 0, 2, 3)


def _step(inp):
    x = inp['x'][0]
    target = inp['loss_target'][0]
    rows, d = x.shape
    depth = inp['w_in'].shape[0]
    mix_w = d // 4
    n_state = S5_GROUPS * S5_STATE
    ffn_half = inp['ffn_w_up'].shape[2]
    tm = min(512, rows)
    tc = min(256, rows)
    tl = min(512, rows)
    xy = ("x", "y")

    send, keys = [], []
    for l in range(depth):
        for nme in BIG:
            send.append(inp[nme][l].astype(BF16))
            keys.append((nme, l))
    for nme in SMALL_SHARDED:
        send.append(inp[nme])
        keys.append((nme, None))
    gathered = dict(zip(keys, all_gather("gather_weights", send, xy)))

    def full_small(nme):
        g = gathered[(nme, None)]
        return g.transpose(1, 2, 0, 3).reshape(g.shape[1], g.shape[2], 4 * g.shape[3])

    cv_w_dw, lru_w_conv, ffn_w_dw = full_small('cv_w_dw'), full_small('lru_w_conv'), full_small('ffn_w_dw')

    gsum = jnp.repeat(jnp.eye(128, dtype=F32)[:S5_GROUPS], S5_STATE, axis=0)

    saved = []
    grads = {nme: [None] * depth for nme in WEIGHTS}
    xcur = x
    for l in range(depth):
        w_in = gathered[('w_in', l)]
        w_out = gathered[('w_out', l)].reshape(d, d)
        w_up = gathered[('ffn_w_up', l)]
        w_down = gathered[('ffn_w_down', l)].reshape(2, ffn_half, d)
        w_glu = gathered[('s5_w_glu', l)].reshape(mix_w, mix_w)
        w_pw = gathered[('cv_w_pw', l)].reshape(mix_w, mix_w)
        ncol = w_in.shape[2]
        vec = lambda a: a[l].reshape(1, -1)

        lam_re, lam_im = vec(inp['s5_lam_re']), vec(inp['s5_lam_im'])
        log_step = jnp.broadcast_to(inp['s5_log_step'][l][:, None], (S5_GROUPS, S5_STATE)).reshape(1, n_state)
        b_re = _block_diag(inp['s5_b_re'][l].transpose(0, 2, 1))
        b_im = _block_diag(inp['s5_b_im'][l].transpose(0, 2, 1))
        c_cat = jnp.stack([_block_diag(inp['s5_c_re'][l].transpose(0, 2, 1)),
                           -_block_diag(inp['s5_c_im'][l].transpose(0, 2, 1))]).astype(BF16)
        a_bar, b_bar = s5_param_fwd(f"s5_param_fwd{l}", lam_re, lam_im, log_step, b_re, b_im)
        w_r = _block_diag(inp['lru_w_r'][l]).astype(BF16)
        w_i = _block_diag(inp['lru_w_i'][l]).astype(BF16)
        pool_bd = _block_diag(inp['pool_w'][l]).astype(BF16)
        post_pars = [vec(inp['s5_d']), w_glu, vec(inp['s5_b_glu']), vec(inp['cv_ln_g']), vec(inp['cv_ln_b']), w_pw,
                     vec(inp['cv_b_pw']), pool_bd, vec(inp['pool_scale'])]
        gate_pars = [w_r, w_i, vec(inp['lru_b_r']), vec(inp['lru_b_i']), vec(inp['lru_lam'])]

        h = rms_fwd(f"rms_mix{l}", xcur, vec(inp['norm_mix_g']), tm)
        proj = _mm(f"proj{l}", h, w_in, jax.ShapeDtypeStruct((rows, 4 * ncol), F32), (rows // tm, 4),
                   pl.BlockSpec((tm, d), lambda i, j: (i, 0)), pl.BlockSpec((None, d, ncol), lambda i, j: (j, 0, 0)),
                   pl.BlockSpec((tm, ncol), lambda i, j: (i, j)), NN)
        proj3 = proj.reshape(1, rows, 4 * ncol)
        nh = n_state // 2
        bu = _mm(f"s5_bu{l}", proj, b_bar, jax.ShapeDtypeStruct((2, rows, n_state), F32), (rows // tm, 2, 2),
                 pl.BlockSpec((tm, mix_w), lambda i, c, n: (i, 0)), pl.BlockSpec((None, mix_w, nh), lambda i, c, n: (c, 0, n)),
                 pl.BlockSpec((None, tm, nh), lambda i, c, n: (c, i, n)), NN)
        z = s5_scan_fwd(f"s5_scan{l}", bu, a_bar)
        y_ssm = _mm(f"s5_read{l}", z, c_cat, jax.ShapeDtypeStruct((rows, mix_w), F32), (rows // tm, 4),
                    pl.BlockSpec((None, tm, nh), lambda i, k: (k // 2, i, k % 2)),
                    pl.BlockSpec((None, nh, mix_w), lambda i, k: (k // 2, k % 2, 0)),
                    pl.BlockSpec((tm, mix_w), lambda i, k: (i, 0)), NN, k_axis=1)
        (h0,) = _rowwise(f"cv_glu{l}", _glu, [(proj, 1, mix_w), (proj, 2, mix_w)], [], 1, [(mix_w, F32)], tm)
        h1 = dwconv_fwd(f"cv_conv{l}", h0.reshape(1, rows, mix_w), 0, mix_w, cv_w_dw[l][None], vec(inp['cv_b_dw'])[None],
                        CV_TAPS, tc)[0]
        xc = dwconv_fwd(f"lru_conv{l}", proj3, 3, mix_w, lru_w_conv[l][None], vec(inp['lru_b_conv'])[None], LRU_TAPS, tc)[0]
        a_t, b_t = _rowwise(f"lru_gate{l}", _lru_gate, [(xc, 0, mix_w)], gate_pars, 2, [(mix_w, F32), (mix_w, F32)], tm)
        hseq = lru_scan_fwd(f"lru_scan{l}", a_t, b_t)
        dgp = pool_fwd(f"pool{l}", proj, 5, tc)
        post_rows = [(y_ssm, 0, mix_w), (proj, 0, mix_w), (h1, 0, mix_w), (hseq, 0, mix_w), (proj, 4, mix_w), (dgp, 0, mix_w)]
        (mixed,) = _rowwise(f"mix_post{l}", _mix_post, post_rows, post_pars, 1, [(d, BF16)], tm)
        x1 = _mm(f"out_proj{l}", mixed, w_out, jax.ShapeDtypeStruct((rows, d), F32), (rows // tm, 2, 4),
                 pl.BlockSpec((tm, mix_w), lambda i, j, k: (i, k)), pl.BlockSpec((mix_w, d // 2), lambda i, j, k: (k, j)),
                 pl.BlockSpec((tm, d // 2), lambda i, j, k: (i, j)), NN, k_axis=2,
                 add=xcur, add_spec=pl.BlockSpec((tm, d // 2), lambda i, j, k: (i, j)))

        h2 = rms_fwd(f"rms_ffn{l}", x1, vec(inp['norm_ffn_g']), tm)
        tu = min(256, rows)
        up = _mm(f"ffn_up{l}", h2, w_up, jax.ShapeDtypeStruct((4, rows, ffn_half), F32), (4, rows // tu),
                 pl.BlockSpec((tu, d), lambda k, i: (i, 0)), pl.BlockSpec((None, d, ffn_half), lambda k, i: (k, 0, 0)),
                 pl.BlockSpec((None, tu, ffn_half), lambda k, i: (k, i, 0)), NN)
        w_dw = ffn_w_dw[l].reshape(FFN_TAPS, 2, ffn_half).transpose(1, 0, 2)
        b_dw = inp['ffn_b_dw'][l].reshape(2, 1, ffn_half)
        gc = dwconv_fwd(f"ffn_conv{l}", up, 0, ffn_half, w_dw, b_dw, FFN_TAPS, tc)
        act = ffn_act_fwd(f"ffn_act{l}", gc, up, tc)
        x2 = _mm(f"ffn_down{l}", act, w_down, jax.ShapeDtypeStruct((rows, d), F32), (rows // tm, 4, 2),
                 pl.BlockSpec((None, tm, ffn_half), lambda i, j, k: (k, i, 0)), pl.BlockSpec((None, ffn_half, d // 4), lambda i, j, k: (k, 0, j)),
                 pl.BlockSpec((tm, d // 4), lambda i, j, k: (i, j)), NN, k_axis=2,
                 add=x1, add_spec=pl.BlockSpec((tm, d // 4), lambda i, j, k: (i, j)))
        saved.append(dict(x=xcur, h=h, proj=proj, z=z, y_ssm=y_ssm, h0=h0, h1=h1, xc=xc, a_t=a_t, hseq=hseq, dgp=dgp,
                          mixed=mixed, x1=x1, h2=h2, up=up, gc=gc, act=act, w_in=w_in, w_out=w_out, w_up=w_up, w_down=w_down,
                          a_bar=a_bar, b_bar=b_bar, c_cat=c_cat, post_pars=post_pars, gate_pars=gate_pars, w_dw=w_dw,
                          s5=(lam_re, lam_im, log_step, b_re, b_im), cv_w=cv_w_dw[l][None], lru_w=lru_w_conv[l][None]))
        xcur = x2

    loss_row, dx, dg_final = final_loss("final_loss", xcur, inp['norm_final_g'].reshape(1, d), target, tm)
    grads['norm_final_g'] = dg_final.reshape(d)

    big_g = {nme: [None] * depth for nme in BIG}
    for l in reversed(range(depth)):
        s = saved[l]
        ncol = s['w_in'].shape[2]
        nh = n_state // 2
        tu = min(256, rows)
        dact = _mm(f"d_act{l}", dx, s['w_down'], jax.ShapeDtypeStruct((2, rows, ffn_half), F32), (2, rows // tu),
                   pl.BlockSpec((tu, d), lambda k, i: (i, 0)), pl.BlockSpec((None, ffn_half, d), lambda k, i: (k, 0, 0)),
                   pl.BlockSpec((None, tu, ffn_half), lambda k, i: (k, i, 0)), NT)
        tn = d // 4
        big_g['ffn_w_down'][l] = _mm(
            f"dw_down{l}", s['act'], dx, jax.ShapeDtypeStruct((2, 2, ffn_half, d // 2), F32), (2, 4, rows // tl),
            pl.BlockSpec((None, tl, ffn_half), lambda hh, n, k: (hh, k, 0)), pl.BlockSpec((tl, tn), lambda hh, n, k: (k, n)),
            pl.BlockSpec((None, None, ffn_half, tn), lambda hh, n, k: (n // 2, hh, 0, n % 2)), TN, k_axis=2)
        dgc, dval = ffn_act_bwd(f"ffn_act_bwd{l}", s['gc'], s['up'], dact, tc)
        dgate, dw_dw, db_dw = dwconv_bwd(f"ffn_conv_bwd{l}", dgc, s['up'], 0, ffn_half, s['w_dw'], FFN_TAPS, tc, dx_dtype=BF16)
        grads['ffn_w_dw'][l] = dw_dw.transpose(1, 0, 2).reshape(FFN_TAPS, 2 * ffn_half)
        grads['ffn_b_dw'][l] = db_dw.reshape(2 * ffn_half)
        dup = jnp.concatenate([dgate, dval], axis=0)
        dh2 = _mm(f"d_h2{l}", dup, s['w_up'], jax.ShapeDtypeStruct((rows, d), F32), (rows // tm, 4, 4),
                  pl.BlockSpec((None, tm, ffn_half), lambda i, j, k: (k, i, 0)), pl.BlockSpec((None, d // 4, ffn_half), lambda i, j, k: (k, j, 0)),
                  pl.BlockSpec((tm, d // 4), lambda i, j, k: (i, j)), NT, k_axis=2)
        tmm = d // 4
        big_g['ffn_w_up'][l] = _mm(
            f"dw_up{l}", s['h2'], dup, jax.ShapeDtypeStruct((2, 4, d // 2, ffn_half), F32), (4, 4, rows // tl),
            pl.BlockSpec((tl, tmm), lambda k4, m, k: (k, m)), pl.BlockSpec((None, tl, ffn_half), lambda k4, m, k: (k4, k, 0)),
            pl.BlockSpec((None, None, tmm, ffn_half), lambda k4, m, k: (m // 2, k4, m % 2, 0)), TN, k_axis=2)
        dx1, dg = rms_bwd(f"rms_ffn_bwd{l}", s['x1'], inp['norm_ffn_g'][l].reshape(1, d), dh2, dx, tm)
        grads['norm_ffn_g'][l] = dg.reshape(d)
        dmixed = _mm(f"d_mixed{l}", dx1, s['w_out'], jax.ShapeDtypeStruct((rows, d), F32), (rows // tm, 4),
                     pl.BlockSpec((tm, d), lambda i, j: (i, 0)), pl.BlockSpec((d // 4, d), lambda i, j: (j, 0)),
                     pl.BlockSpec((tm, d // 4), lambda i, j: (i, j)), NT)
        tq = mix_w // 2
        big_g['w_out'][l] = _mm(
            f"dw_out{l}", s['mixed'], dx1, jax.ShapeDtypeStruct((2, 4, tq, d), F32), (8, rows // tl),
            pl.BlockSpec((tl, tq), lambda t, k: (k, t)), pl.BlockSpec((tl, d), lambda t, k: (k, 0)),
            pl.BlockSpec((None, None, tq, d), lambda t, k: (t % 2, t // 2, 0, 0)), TN, k_axis=1)
        post_rows = [(s['y_ssm'], 0, mix_w), (s['proj'], 0, mix_w), (s['h1'], 0, mix_w), (s['hseq'], 0, mix_w),
                     (s['proj'], 4, mix_w), (s['dgp'], 0, mix_w), (dmixed, 0, d)]
        res = _rowwise(f"mix_post_bwd{l}", _mix_post, post_rows, s['post_pars'], 1, [(mix_w, F32)] * 6, tm, with_grads=True)
        dy_ssm, du_dir, dh1, dhseq, dlru_g, ddgp = res[:6]
        dd, dwglu, dbglu, dlng, dlnb, dwpw, dbpw, dpoolbd, dscale = res[6:]
        grads['s5_d'][l], grads['s5_b_glu'][l] = dd.reshape(mix_w), dbglu.reshape(mix_w)
        grads['cv_ln_g'][l], grads['cv_ln_b'][l], grads['cv_b_pw'][l] = dlng.reshape(mix_w), dlnb.reshape(mix_w), dbpw.reshape(mix_w)
        grads['pool_w'][l] = _diag_blocks(dpoolbd, len(POOL_WINDOWS))
        grads['pool_scale'][l] = dscale.reshape(mix_w)
        big_g['s5_w_glu'][l] = _halves(dwglu, 4)
        big_g['cv_w_pw'][l] = _halves(dwpw, 4)
        dz = _mm(f"s5_dz{l}", dy_ssm, s['c_cat'], jax.ShapeDtypeStruct((2, rows, n_state), F32), (rows // tm, 2, 2),
                 pl.BlockSpec((tm, mix_w), lambda i, c, n: (i, 0)), pl.BlockSpec((None, nh, mix_w), lambda i, c, n: (c, n, 0)),
                 pl.BlockSpec((None, tm, nh), lambda i, c, n: (c, i, n)), NT)
        dccat = _mm(f"s5_dc{l}", s['z'], dy_ssm, jax.ShapeDtypeStruct((2, n_state, mix_w), F32), (2, n_state // mix_w, rows // tl),
                    pl.BlockSpec((None, tl, mix_w), lambda c, m, k: (c, k, m)), pl.BlockSpec((tl, mix_w), lambda c, m, k: (k, 0)),
                    pl.BlockSpec((None, mix_w, mix_w), lambda c, m, k: (c, m, 0)), TN, k_axis=2)
        grads['s5_c_re'][l] = _diag_blocks(dccat[0], S5_GROUPS).transpose(0, 2, 1)
        grads['s5_c_im'][l] = -_diag_blocks(dccat[1], S5_GROUPS).transpose(0, 2, 1)
        lam, da_bar = s5_scan_bwd(f"s5_scan_bwd{l}", dz, s['z'], s['a_bar'])
        du = _mm(f"s5_du{l}", lam, s['b_bar'], jax.ShapeDtypeStruct((rows, mix_w), F32), (rows // tm, 4),
                 pl.BlockSpec((None, tm, nh), lambda i, k: (k // 2, i, k % 2)), pl.BlockSpec((None, mix_w, nh), lambda i, k: (k // 2, 0, k % 2)),
                 pl.BlockSpec((tm, mix_w), lambda i, k: (i, 0)), NT, k_axis=1,
                 add=du_dir, add_spec=pl.BlockSpec((tm, mix_w), lambda i, k: (i, 0)))
        dbbar = _mm(f"s5_db{l}", s['proj'], lam, jax.ShapeDtypeStruct((2, mix_w, n_state), F32), (2, 2, rows // tl),
                    pl.BlockSpec((tl, mix_w), lambda c, n, k: (k, 0)), pl.BlockSpec((None, tl, nh), lambda c, n, k: (c, k, n)),
                    pl.BlockSpec((None, mix_w, nh), lambda c, n, k: (c, 0, n)), TN, k_axis=2)
        dlr, dli, dls, dbre, dbim = s5_param_bwd(f"s5_param_bwd{l}", *s['s5'], da_bar, dbbar, gsum)
        grads['s5_lam_re'][l] = dlr.reshape(S5_GROUPS, S5_STATE)
        grads['s5_lam_im'][l] = dli.reshape(S5_GROUPS, S5_STATE)
        grads['s5_log_step'][l] = dls[0, :S5_GROUPS]
        grads['s5_b_re'][l] = _diag_blocks(dbre, S5_GROUPS).transpose(0, 2, 1)
        grads['s5_b_im'][l] = _diag_blocks(dbim, S5_GROUPS).transpose(0, 2, 1)
        dh0, dw_cv, db_cv = dwconv_bwd(f"cv_conv_bwd{l}", dh1.reshape(1, rows, mix_w), s['h0'].reshape(1, rows, mix_w), 0, mix_w,
                                       s['cv_w'], CV_TAPS, tc)
        grads['cv_w_dw'][l], grads['cv_b_dw'][l] = dw_cv[0], db_cv.reshape(mix_w)
        dv, dgg = _rowwise(f"cv_glu_bwd{l}", _glu, [(s['proj'], 1, mix_w), (s['proj'], 2, mix_w), (dh0[0], 0, mix_w)], [], 1,
                           [(mix_w, F32)] * 2, tm, with_grads=True)
        da_t, db_t = lru_scan_bwd(f"lru_scan_bwd{l}", dhseq, s['a_t'], s['hseq'])
        res = _rowwise(f"lru_gate_bwd{l}", _lru_gate, [(s['xc'], 0, mix_w), (da_t, 0, mix_w), (db_t, 0, mix_w)], s['gate_pars'], 2,
                       [(mix_w, F32)], tm, with_grads=True)
        dxc, dwr, dwi, dbr, dbi, dlam = res
        grads['lru_w_r'][l], grads['lru_w_i'][l] = _diag_blocks(dwr, LRU_HEADS), _diag_blocks(dwi, LRU_HEADS)
        grads['lru_b_r'][l], grads['lru_b_i'][l], grads['lru_lam'][l] = dbr.reshape(mix_w), dbi.reshape(mix_w), dlam.reshape(mix_w)
        dlx, dw_lc, db_lc = dwconv_bwd(f"lru_conv_bwd{l}", dxc.reshape(1, rows, mix_w), s['proj'].reshape(1, rows, 4 * ncol), 3, mix_w,
                                       s['lru_w'], LRU_TAPS, tc)
        grads['lru_w_conv'][l], grads['lru_b_conv'][l] = dw_lc[0], db_lc.reshape(mix_w)
        dpx = pool_bwd(f"pool_bwd{l}", ddgp, tc)
        dproj = jnp.concatenate([du, dv, dgg, dlx[0], dlru_g, dpx], axis=-1)
        dh = _mm(f"d_h{l}", dproj, s['w_in'], jax.ShapeDtypeStruct((rows, d), F32), (rows // tm, 4, 4),
                 pl.BlockSpec((tm, ncol), lambda i, j, k: (i, k)), pl.BlockSpec((None, d // 4, ncol), lambda i, j, k: (k, j, 0)),
                 pl.BlockSpec((tm, d // 4), lambda i, j, k: (i, j)), NT, k_axis=2)
        big_g['w_in'][l] = _mm(
            f"dw_in{l}", s['h'], dproj, jax.ShapeDtypeStruct((2, 4, d // 2, ncol), F32), (4, 4, rows // tl),
            pl.BlockSpec((tl, tmm), lambda k4, m, k: (k, m)), pl.BlockSpec((tl, ncol), lambda k4, m, k: (k, k4)),
            pl.BlockSpec((None, None, tmm, ncol), lambda k4, m, k: (m // 2, k4, m % 2, 0)), TN, k_axis=2)
        dx, dg = rms_bwd(f"rms_mix_bwd{l}", s['x'], inp['norm_mix_g'][l].reshape(1, d), dh, dx1, tm)
        grads['norm_mix_g'][l] = dg.reshape(d)

    big_keys = [(nme, l) for nme in BIG for l in range(depth)]
    g_list = []
    for nme, l in big_keys:
        g = big_g[nme][l]
        if nme == 'ffn_w_down':
            g = g.reshape(2, 4, ffn_half // 2, d // 2)
        g_list.append(g)
    s_list = [reduce_cores(f"reduce_cores_{nme}{l}", g.reshape(2, -1, g.shape[-1])).reshape(g.shape[1:])
              for (nme, l), g in zip(big_keys, g_list)]
    r2 = all_to_all("reduce_chips", s_list, xy)
    t_full = {key: sum_and_share(f"share_cores_{key[0]}{key[1]}", r) for key, r in zip(big_keys, r2)}

    outs = {}
    tiles = {'w_in': 256, 'w_out': 128, 'ffn_w_up': 128, 'ffn_w_down': 256, 's5_w_glu': 64, 'cv_w_pw': 64}
    for nme in BIG:
        g0, g1 = t_full[(nme, 0)], t_full[(nme, 1)]
        outs[nme] = adamw_sharded(f"adamw_{nme}", inp[nme], inp['m_' + nme], inp['v_' + nme], g0, g1, nme == 'ffn_w_down', tiles[nme])

    small = [nme for nme in WEIGHTS if nme not in BIG]
    full_g = {nme: (grads[nme] if nme == 'norm_final_g' else jnp.stack(grads[nme])) for nme in small}
    flat = jnp.concatenate([full_g[nme].reshape(-1) for nme in small])
    n_flat = flat.shape[0]
    pad = (-n_flat) % (128 * 64)
    packed = jnp.pad(flat, (0, pad)).reshape(-1, 128)
    (g8,) = all_gather("gather_small", [packed], ("x", "y", "c"))
    gsum_small = sum_lead("sum_small", g8, 64).reshape(-1)
    my_chip = 2 * lax.axis_index("x") + lax.axis_index("y")
    red, off = {}, 0
    for nme in small:
        g = gsum_small[off:off + full_g[nme].size].reshape(full_g[nme].shape)
        off += full_g[nme].size
        if nme in SMALL_SHARDED:
            width = inp[nme].shape[2]
            g = lax.dynamic_slice_in_dim(g, my_chip * width, width, axis=2)
        red[nme] = g

    def pack(tree):
        f = jnp.concatenate([tree[nme].reshape(-1) for nme in small])
        return jnp.pad(f, (0, (-f.shape[0]) % (128 * 64))).reshape(-1, 128)

    pd, pm, pv = adamw_flat("adamw_small", pack({n_: inp[n_] for n_ in small}), pack(red), pack({n_: inp['m_' + n_] for n_ in small}),
                            pack({n_: inp['v_' + n_] for n_ in small}), 64)
    off = 0
    for nme in small:
        size, shape = inp[nme].size, inp[nme].shape
        outs[nme] = (red[nme],) + tuple(p.reshape(-1)[off:off + size].reshape(shape) for p in (pd, pm, pv))
        off += size

    loss = lax.psum(loss_row[0, 0], ("x", "y", "c"))
    result = [loss, dx[None]]
    for part in range(4):
        result += [outs[nme][part] for nme in WEIGHTS]
    return tuple(result)


def kernel(x, norm_mix_g, w_in, s5_lam_re, s5_lam_im, s5_log_step, s5_b_re, s5_b_im, s5_c_re, s5_c_im, s5_d, s5_w_glu, s5_b_glu, cv_w_dw, cv_b_dw, cv_ln_g, cv_ln_b, cv_w_pw, cv_b_pw, lru_w_conv, lru_b_conv, lru_w_r, lru_b_r, lru_w_i, lru_b_i, lru_lam, pool_w, pool_scale, w_out, norm_ffn_g, ffn_w_up, ffn_w_dw, ffn_b_dw, ffn_w_down, norm_final_g, loss_target, m_norm_mix_g, m_w_in, m_s5_lam_re, m_s5_lam_im, m_s5_log_step, m_s5_b_re, m_s5_b_im, m_s5_c_re, m_s5_c_im, m_s5_d, m_s5_w_glu, m_s5_b_glu, m_cv_w_dw, m_cv_b_dw, m_cv_ln_g, m_cv_ln_b, m_cv_w_pw, m_cv_b_pw, m_lru_w_conv, m_lru_b_conv, m_lru_w_r, m_lru_b_r, m_lru_w_i, m_lru_b_i, m_lru_lam, m_pool_w, m_pool_scale, m_w_out, m_norm_ffn_g, m_ffn_w_up, m_ffn_w_dw, m_ffn_b_dw, m_ffn_w_down, m_norm_final_g, v_norm_mix_g, v_w_in, v_s5_lam_re, v_s5_lam_im, v_s5_log_step, v_s5_b_re, v_s5_b_im, v_s5_c_re, v_s5_c_im, v_s5_d, v_s5_w_glu, v_s5_b_glu, v_cv_w_dw, v_cv_b_dw, v_cv_ln_g, v_cv_ln_b, v_cv_w_pw, v_cv_b_pw, v_lru_w_conv, v_lru_b_conv, v_lru_w_r, v_lru_b_r, v_lru_w_i, v_lru_b_i, v_lru_lam, v_pool_w, v_pool_scale, v_w_out, v_norm_ffn_g, v_ffn_w_up, v_ffn_w_dw, v_ffn_b_dw, v_ffn_w_down, v_norm_final_g):
    inp = dict(locals())
    return _step(inp)
```

```python
import functools

import jax
import jax.numpy as jnp
from jax import lax
from jax.experimental import pallas as pl
from jax.experimental.pallas import tpu as pltpu

F32 = jnp.float32
BF16 = jnp.bfloat16

VMEM_LIMIT_BYTES = 56 * 1024 * 1024
SUBLANES = 8

EPS = 1e-6
S5_GROUPS, S5_STATE, S5_GROUP_CH = 32, 64, 16
LRU_HEADS, LRU_C = 8, 8.0
POOL_WINDOWS = (2, 4, 8, 16)
CV_TAPS, LRU_TAPS, FFN_TAPS = 31, 4, 3
SCAN_CHUNK = 64

ADAM_LR, ADAM_B1, ADAM_B2, ADAM_EPS, ADAM_WD, ADAM_STEP = 0.001, 0.9, 0.999, 1e-08, 0.01, 10

NN = ((1,), (0,))
NT = ((1,), (1,))
TN = ((0,), (0,))

WEIGHTS = ['norm_mix_g', 'w_in', 's5_lam_re', 's5_lam_im', 's5_log_step', 's5_b_re', 's5_b_im', 's5_c_re', 's5_c_im',
           's5_d', 's5_w_glu', 's5_b_glu', 'cv_w_dw', 'cv_b_dw', 'cv_ln_g', 'cv_ln_b', 'cv_w_pw', 'cv_b_pw',
           'lru_w_conv', 'lru_b_conv', 'lru_w_r', 'lru_b_r', 'lru_w_i', 'lru_b_i', 'lru_lam', 'pool_w', 'pool_scale',
           'w_out', 'norm_ffn_g', 'ffn_w_up', 'ffn_w_dw', 'ffn_b_dw', 'ffn_w_down', 'norm_final_g']
BIG = ('w_in', 'w_out', 'ffn_w_up', 'ffn_w_down', 's5_w_glu', 'cv_w_pw')
SMALL_SHARDED = {'cv_w_dw': 2, 'lru_w_conv': 2, 'ffn_w_dw': 2}


def _params(sem=None):
    if sem is None:
        return pltpu.CompilerParams(vmem_limit_bytes=VMEM_LIMIT_BYTES)
    return pltpu.CompilerParams(dimension_semantics=sem, vmem_limit_bytes=VMEM_LIMIT_BYTES)


def _row_tile(rows, cap):
    best = SUBLANES
    for t in range(SUBLANES, min(rows, cap) + 1, SUBLANES):
        if rows % t == 0:
            best = t
    return best


def _bdot(a, b, dims=NN):
    return lax.dot_general(a.astype(BF16), b.astype(BF16), (dims, ((), ())), preferred_element_type=F32)


@jax.custom_vjp
def bdot(a, b):
    return _bdot(a, b)


def _bdot_fwd(a, b):
    return _bdot(a, b), (a, b)


def _bdot_bwd(res, g):
    a, b = res
    return _bdot(g, b, NT).astype(a.dtype), _bdot(a, g, TN).astype(b.dtype)


bdot.defvjp(_bdot_fwd, _bdot_bwd)


def _mm(name, a, b, out_sds, grid, a_spec, b_spec, o_spec, dims, k_axis=None, add=None, add_spec=None, inner=None):
    nk = grid[k_axis] if k_axis is not None else 1
    has_add = add is not None
    acc_shape = tuple(d for d in o_spec.block_shape if d is not None)
    acc_in_out = out_sds.dtype == F32

    def product(a_ref, b_ref):
        if inner is None:
            return _bdot(a_ref[...], b_ref[...], dims)
        kind, n = inner
        width = a_ref.shape[-1] // n
        acc = None
        for j in range(n):
            a_j = a_ref[j] if kind == "lead" else a_ref[:, j * width:(j + 1) * width]
            p = _bdot(a_j, b_ref[j], dims)
            acc = p if acc is None else acc + p
        return acc

    def body(*refs):
        a_ref, b_ref = refs[0], refs[1]
        add_ref = refs[2] if has_add else None
        o_ref = refs[3] if has_add else refs[2]
        prod = product(a_ref, b_ref)
        if k_axis is None:
            if has_add:
                prod = prod + add_ref[...]
            o_ref[...] = prod.astype(o_ref.dtype)
        else:
            acc_ref = o_ref if acc_in_out else refs[-1]
            k = pl.program_id(k_axis)

            @pl.when(k == 0)
            def _():
                acc_ref[...] = prod

            @pl.when(k > 0)
            def _():
                acc_ref[...] += prod

            if has_add or not acc_in_out:
                @pl.when(k == nk - 1)
                def _():
                    r = acc_ref[...]
                    if has_add:
                        r = r + add_ref[...]
                    o_ref[...] = r.astype(o_ref.dtype)

    sem = tuple("arbitrary" if d == k_axis else "parallel" for d in range(len(grid)))
    in_specs = [a_spec, b_spec] + ([add_spec] if has_add else [])
    args = (a, b) + ((add,) if has_add else ())
    scratch = [pltpu.VMEM(acc_shape, F32)] if (k_axis is not None and not acc_in_out) else []
    return pl.pallas_call(body, out_shape=out_sds, grid=grid, in_specs=in_specs, out_specs=o_spec,
                          scratch_shapes=scratch, compiler_params=_params(sem), name=name)(*args)


def _rms(x, g):
    return x * lax.rsqrt(jnp.mean(x * x, axis=-1, keepdims=True) + EPS) * g


def rms_fwd(name, x, g, tm):
    rows, d = x.shape

    def body(x_ref, g_ref, o_ref):
        o_ref[...] = _rms(x_ref[...], g_ref[...]).astype(BF16)

    return pl.pallas_call(
        body, out_shape=jax.ShapeDtypeStruct((rows, d), BF16), grid=(rows // tm,),
        in_specs=[pl.BlockSpec((tm, d), lambda i: (i, 0)), pl.BlockSpec((1, d), lambda i: (0, 0))],
        out_specs=pl.BlockSpec((tm, d), lambda i: (i, 0)), compiler_params=_params(("parallel",)), name=name)(x, g)


def rms_bwd(name, x, g, dh, dres, tm):
    rows, d = x.shape

    def body(x_ref, g_ref, dh_ref, dres_ref, dx_ref, dg_ref):
        _, vjp = jax.vjp(_rms, x_ref[...], g_ref[...])
        dx, dg = vjp(dh_ref[...])
        dx_ref[...] = dx + dres_ref[...]

        @pl.when(pl.program_id(0) == 0)
        def _():
            dg_ref[...] = jnp.zeros_like(dg_ref)

        dg_ref[...] += dg

    row = pl.BlockSpec((tm, d), lambda i: (i, 0))
    vec = pl.BlockSpec((1, d), lambda i: (0, 0))
    return pl.pallas_call(
        body, out_shape=(jax.ShapeDtypeStruct((rows, d), F32), jax.ShapeDtypeStruct((1, d), F32)), grid=(rows // tm,),
        in_specs=[row, vec, row, row], out_specs=(row, vec), compiler_params=_params(("arbitrary",)), name=name)(x, g, dh, dres)


def final_loss(name, x, g, target, tm):
    rows, d = x.shape

    def body(x_ref, g_ref, t_ref, l_ref, dx_ref, dg_ref):
        def f(xv, gv):
            e = _rms(xv, gv) - t_ref[...]
            return 0.5 * jnp.sum(jnp.mean(e * e, axis=-1))

        loss, (dx, dg) = jax.value_and_grad(f, argnums=(0, 1))(x_ref[...], g_ref[...])
        dx_ref[...] = dx

        @pl.when(pl.program_id(0) == 0)
        def _():
            l_ref[...] = jnp.zeros_like(l_ref)
            dg_ref[...] = jnp.zeros_like(dg_ref)

        l_ref[...] += jnp.full(l_ref.shape, loss, F32)
        dg_ref[...] += dg

    row = pl.BlockSpec((tm, d), lambda i: (i, 0))
    vec = pl.BlockSpec((1, d), lambda i: (0, 0))
    lspec = pl.BlockSpec((1, 128), lambda i: (0, 0))
    return pl.pallas_call(
        body, out_shape=(jax.ShapeDtypeStruct((1, 128), F32), jax.ShapeDtypeStruct((rows, d), F32), jax.ShapeDtypeStruct((1, d), F32)),
        grid=(rows // tm,), in_specs=[row, vec, row], out_specs=(lspec, row, vec),
        compiler_params=_params(("arbitrary",)), name=name)(x, g, target)


def _rowwise(name, fn, row_ins, par_ins, n_row_out, row_out_dtypes, tm, with_grads=False):
    rows = row_ins[0][0].shape[0]
    n_prim = len(row_ins) - (n_row_out if with_grads else 0)
    n_par = len(par_ins)

    def body(*refs):
        ins = [r[...] for r in refs[:len(row_ins) + n_par]]
        outs = refs[len(row_ins) + n_par:]
        prim, cts, pars = ins[:n_prim], ins[n_prim:len(row_ins)], ins[len(row_ins):]
        if not with_grads:
            res = fn(*prim, *pars)
            for o_ref, r in zip(outs, res):
                o_ref[...] = r.astype(o_ref.dtype)
            return
        _, vjp = jax.vjp(fn, *prim, *[p.astype(F32) for p in pars])
        grads = vjp(tuple(cts))
        for o_ref, gr in zip(outs[:n_prim], grads[:n_prim]):
            o_ref[...] = gr.astype(o_ref.dtype)

        @pl.when(pl.program_id(0) == 0)
        def _():
            for o_ref in outs[n_prim:]:
                o_ref[...] = jnp.zeros_like(o_ref)

        for o_ref, gr in zip(outs[n_prim:], grads[n_prim:]):
            o_ref[...] += gr.astype(F32)

    in_specs = [pl.BlockSpec((tm, w), (lambda i, c=c: (i, c))) for (_, c, w) in row_ins]
    in_specs += [pl.BlockSpec(p.shape, (lambda i, n=p.ndim: (0,) * n)) for p in par_ins]
    args = [a for (a, _, _) in row_ins] + list(par_ins)
    if not with_grads:
        out_shape = tuple(jax.ShapeDtypeStruct((rows, w), dt) for (w, dt) in row_out_dtypes)
        out_specs = tuple(pl.BlockSpec((tm, w), lambda i: (i, 0)) for (w, _) in row_out_dtypes)
        sem = ("parallel",)
    else:
        out_shape = tuple(jax.ShapeDtypeStruct((rows, w), dt) for (w, dt) in row_out_dtypes)
        out_shape += tuple(jax.ShapeDtypeStruct(p.shape, F32) for p in par_ins)
        out_specs = tuple(pl.BlockSpec((tm, w), lambda i: (i, 0)) for (w, _) in row_out_dtypes)
        out_specs += tuple(pl.BlockSpec(p.shape, (lambda i, n=p.ndim: (0,) * n)) for p in par_ins)
        sem = ("arbitrary",)
    return pl.pallas_call(body, out_shape=out_shape, grid=(rows // tm,), in_specs=in_specs, out_specs=out_specs,
                          compiler_params=_params(sem), name=name)(*args)


def _glu(v, g):
    return (v * jax.nn.sigmoid(g),)


def _neg_expm1(z):
    return -jnp.tanh(0.5 * z) * (jnp.exp(z) + 1.0)


def _lru_gate(xc, w_r, w_i, b_r, b_i, lam):
    r = jax.nn.sigmoid(bdot(xc, w_r) + b_r)
    i = jax.nn.sigmoid(bdot(xc, w_i) + b_i)
    log_a = -LRU_C * r * jax.nn.softplus(-lam)
    a = jnp.exp(log_a)
    mult = jnp.sqrt(_neg_expm1(2.0 * log_a))
    return a, mult * (i * xc)


def _layernorm(x, g, b):
    mu = jnp.mean(x, axis=-1, keepdims=True)
    var = jnp.mean(jnp.square(x - mu), axis=-1, keepdims=True)
    return (x - mu) * lax.rsqrt(var + EPS) * g + b


def _mix_post(y_ssm, u, h1, hseq, lru_g, dgp, s5_d, w_glu, b_glu, ln_g, ln_b, w_pw, b_pw, pool_bd, pool_scale):
    y = y_ssm + s5_d * u
    gl = jax.nn.gelu(y, approximate=True)
    out_s5 = gl * jax.nn.sigmoid(bdot(gl, w_glu) + b_glu)
    out_cv = bdot(jax.nn.silu(_layernorm(h1, ln_g, ln_b)), w_pw) + b_pw
    out_lru = hseq * jax.nn.gelu(lru_g, approximate=True)
    out_pool = bdot(dgp, pool_bd) * pool_scale
    return (jnp.concatenate([out_s5, out_cv, out_lru, out_pool], axis=-1),)


def _ffn_act(gc, val):
    return (jax.nn.gelu(gc, approximate=True) * val,)


def ffn_act_fwd(name, gc, up, tm):
    _, rows, c = gc.shape

    def body(g_ref, v_ref, o_ref):
        o_ref[...] = _ffn_act(g_ref[...], v_ref[...])[0].astype(BF16)

    return pl.pallas_call(
        body, out_shape=jax.ShapeDtypeStruct((2, rows, c), BF16), grid=(2, rows // tm),
        in_specs=[pl.BlockSpec((None, tm, c), lambda h, i: (h, i, 0)), pl.BlockSpec((None, tm, c), lambda h, i: (h + 2, i, 0))],
        out_specs=pl.BlockSpec((None, tm, c), lambda h, i: (h, i, 0)),
        compiler_params=_params(("parallel", "parallel")), name=name)(gc, up)


def ffn_act_bwd(name, gc, up, dact, tm):
    _, rows, c = gc.shape

    def body(g_ref, v_ref, d_ref, dg_ref, dv_ref):
        _, vjp = jax.vjp(_ffn_act, g_ref[...], v_ref[...])
        dg, dv = vjp((d_ref[...],))
        dg_ref[...] = dg
        dv_ref[...] = dv.astype(BF16)

    blk = pl.BlockSpec((None, tm, c), lambda h, i: (h, i, 0))
    return pl.pallas_call(
        body, out_shape=(jax.ShapeDtypeStruct((2, rows, c), F32), jax.ShapeDtypeStruct((2, rows, c), BF16)), grid=(2, rows // tm),
        in_specs=[blk, pl.BlockSpec((None, tm, c), lambda h, i: (h + 2, i, 0)), blk], out_specs=(blk, blk),
        compiler_params=_params(("parallel", "parallel")), name=name)(gc, up, dact)


def _halo_rows(taps):
    return -(-(taps - 1) // SUBLANES) * SUBLANES


def dwconv_fwd(name, x, cblk, c, w, b, taps, tm, out_dtype=F32):
    nb = w.shape[0]
    rows = x.shape[1]
    halo = _halo_rows(taps)
    per = tm // halo

    def body(x_ref, h_ref, w_ref, b_ref, o_ref):
        i = pl.program_id(1)
        prev = jnp.where(i > 0, h_ref[...], 0.0)
        ext = jnp.concatenate([prev, x_ref[...]], axis=0)
        acc = jnp.broadcast_to(b_ref[...], (tm, c))
        for k in range(taps):
            off = halo - (taps - 1) + k
            acc = acc + w_ref[k:k + 1, :] * ext[off:off + tm]
        o_ref[...] = acc.astype(o_ref.dtype)

    return pl.pallas_call(
        body, out_shape=jax.ShapeDtypeStruct((nb, rows, c), out_dtype), grid=(nb, rows // tm),
        in_specs=[pl.BlockSpec((None, tm, c), lambda n, i: (n, i, cblk)),
                  pl.BlockSpec((None, halo, c), lambda n, i: (n, jnp.maximum(i * per - 1, 0), cblk)),
                  pl.BlockSpec((None, taps, c), lambda n, i: (n, 0, 0)),
                  pl.BlockSpec((None, 1, c), lambda n, i: (n, 0, 0))],
        out_specs=pl.BlockSpec((None, tm, c), lambda n, i: (n, i, 0)),
        compiler_params=_params(("parallel", "parallel")), name=name)(x, x, w, b)


def dwconv_bwd(name, dy, x, cblk, c, w, taps, tm, dx_dtype=F32):
    nb = w.shape[0]
    rows = x.shape[1]
    halo = _halo_rows(taps)
    per = tm // halo
    n_tiles = rows // tm
    last_halo = rows // halo - 1

    def body(dy_ref, dn_ref, x_ref, xp_ref, w_ref, dx_ref, dw_ref, db_ref):
        i = pl.program_id(1)
        dyv = dy_ref[...]
        nxt = jnp.where(i < n_tiles - 1, dn_ref[...], 0.0)
        dext = jnp.concatenate([dyv, nxt], axis=0)
        prev = jnp.where(i > 0, xp_ref[...], 0.0)
        xext = jnp.concatenate([prev, x_ref[...]], axis=0)
        acc = jnp.zeros((tm, c), F32)

        @pl.when(i == 0)
        def _():
            dw_ref[...] = jnp.zeros_like(dw_ref)
            db_ref[...] = jnp.zeros_like(db_ref)

        for k in range(taps):
            acc = acc + w_ref[k:k + 1, :] * dext[taps - 1 - k:taps - 1 - k + tm]
            off = halo - (taps - 1) + k
            dw_ref[k:k + 1, :] += jnp.sum(dyv * xext[off:off + tm], axis=0, keepdims=True)
        dx_ref[...] = acc.astype(dx_ref.dtype)
        db_ref[...] += jnp.sum(dyv, axis=0, keepdims=True)

    return pl.pallas_call(
        body, out_shape=(jax.ShapeDtypeStruct((nb, rows, c), dx_dtype), jax.ShapeDtypeStruct((nb, taps, c), F32),
                         jax.ShapeDtypeStruct((nb, 1, c), F32)),
        grid=(nb, n_tiles),
        in_specs=[pl.BlockSpec((None, tm, c), lambda n, i: (n, i, 0)),
                  pl.BlockSpec((None, halo, c), lambda n, i: (n, jnp.minimum((i + 1) * per, last_halo), 0)),
                  pl.BlockSpec((None, tm, c), lambda n, i: (n, i, cblk)),
                  pl.BlockSpec((None, halo, c), lambda n, i: (n, jnp.maximum(i * per - 1, 0), cblk)),
                  pl.BlockSpec((None, taps, c), lambda n, i: (n, 0, 0))],
        out_specs=(pl.BlockSpec((None, tm, c), lambda n, i: (n, i, 0)), pl.BlockSpec((None, taps, c), lambda n, i: (n, 0, 0)),
                   pl.BlockSpec((None, 1, c), lambda n, i: (n, 0, 0))),
        compiler_params=_params(("parallel", "arbitrary")), name=name)(dy, dy, x, x, w)


POOL_HALO = 16


def pool_fwd(name, proj, cblk, tm):
    rows = proj.shape[0]
    c = 128 * len(POOL_WINDOWS)
    per = tm // POOL_HALO

    def body(x_ref, h_ref, o_ref):
        i = pl.program_id(0)
        xv = x_ref[...]
        ext = jnp.concatenate([jnp.where(i > 0, h_ref[...], 0.0), xv], axis=0)
        t1 = (lax.broadcasted_iota(jnp.int32, (tm, 128), 0) + i * tm + 1).astype(F32)
        outs = []
        for gi, win in enumerate(POOL_WINDOWS):
            seg = ext[:, gi * 128:(gi + 1) * 128]
            s = seg[POOL_HALO:POOL_HALO + tm]
            for j in range(1, win):
                s = s + seg[POOL_HALO - j:POOL_HALO - j + tm]
            outs.append(s / jnp.minimum(t1, float(win)) - xv[:, gi * 128:(gi + 1) * 128])
        o_ref[...] = jnp.concatenate(outs, axis=-1)

    return pl.pallas_call(
        body, out_shape=jax.ShapeDtypeStruct((rows, c), F32), grid=(rows // tm,),
        in_specs=[pl.BlockSpec((tm, c), lambda i: (i, cblk)),
                  pl.BlockSpec((POOL_HALO, c), lambda i: (jnp.maximum(i * per - 1, 0), cblk))],
        out_specs=pl.BlockSpec((tm, c), lambda i: (i, 0)), compiler_params=_params(("parallel",)), name=name)(proj, proj)


def pool_bwd(name, dd, tm):
    rows, c = dd.shape
    per = tm // POOL_HALO
    n_tiles = rows // tm
    last_halo = rows // POOL_HALO - 1

    def body(d_ref, n_ref, o_ref):
        i = pl.program_id(0)
        dv = d_ref[...]
        nxt = jnp.where(i < n_tiles - 1, n_ref[...], 0.0)
        t1 = (lax.broadcasted_iota(jnp.int32, (tm, 128), 0) + i * tm + 1).astype(F32)
        t1n = (lax.broadcasted_iota(jnp.int32, (POOL_HALO, 128), 0) + (i + 1) * tm + 1).astype(F32)
        outs = []
        for gi, win in enumerate(POOL_WINDOWS):
            sl = slice(gi * 128, (gi + 1) * 128)
            q = jnp.concatenate([dv[:, sl] / jnp.minimum(t1, float(win)), nxt[:, sl] / jnp.minimum(t1n, float(win))], axis=0)
            s = q[0:tm]
            for j in range(1, win):
                s = s + q[j:j + tm]
            outs.append(s - dv[:, sl])
        o_ref[...] = jnp.concatenate(outs, axis=-1)

    return pl.pallas_call(
        body, out_shape=jax.ShapeDtypeStruct((rows, c), F32), grid=(n_tiles,),
        in_specs=[pl.BlockSpec((tm, c), lambda i: (i, 0)),
                  pl.BlockSpec((POOL_HALO, c), lambda i: (jnp.minimum((i + 1) * per, last_halo), 0))],
        out_specs=pl.BlockSpec((tm, c), lambda i: (i, 0)), compiler_params=_params(("parallel",)), name=name)(dd, dd)


def _shift_down(v, s, fill):
    r = lax.broadcasted_iota(jnp.int32, v.shape, 0)
    return jnp.where(r >= s, pltpu.roll(v, s, 0), fill)


def _shift_up(v, s, fill):
    n = v.shape[0]
    r = lax.broadcasted_iota(jnp.int32, v.shape, 0)
    return jnp.where(r < n - s, pltpu.roll(v, n - s, 0), fill)


def _cscan_chunk(vr, vi, powers, reverse):
    for k, (qr, qi) in enumerate(powers):
        s = 1 << k
        if reverse:
            sr, si = _shift_up(vr, s, 0.0), _shift_up(vi, s, 0.0)
            vr, vi = vr + qr * sr + qi * si, vi + qr * si - qi * sr
        else:
            sr, si = _shift_down(vr, s, 0.0), _shift_down(vi, s, 0.0)
            vr, vi = vr + qr * sr - qi * si, vi + qr * si + qi * sr
    return vr, vi


def _powers(pr, pi, n):
    out = [(pr, pi)]
    for _ in range(n - 1):
        pr, pi = pr * pr - pi * pi, 2.0 * pr * pi
        out.append((pr, pi))
    return out


def s5_scan_fwd(name, bu, a):
    _, rows, n = bu.shape
    t = min(SCAN_CHUNK, rows)
    steps = t.bit_length() - 1

    def body(bu_ref, a_ref, z_ref):
        pr, pi = a_ref[0], a_ref[1]
        powers = _powers(pr, pi, steps)
        r = lax.broadcasted_iota(jnp.int32, (t, 128), 0)
        tr, ti = _cscan_chunk(jnp.where(r == 0, pr, 0.0), jnp.where(r == 0, pi, 0.0), powers, False)

        def chunk(ci, carry):
            base = pl.multiple_of(ci * t, t)
            vr, vi = _cscan_chunk(bu_ref[0, pl.ds(base, t), :], bu_ref[1, pl.ds(base, t), :], powers, False)
            cr, cim = carry
            zr = vr + tr * cr - ti * cim
            zi = vi + tr * cim + ti * cr
            z_ref[0, pl.ds(base, t), :] = zr
            z_ref[1, pl.ds(base, t), :] = zi
            return zr[t - 1:t, :], zi[t - 1:t, :]

        zero = jnp.zeros((1, 128), F32)
        lax.fori_loop(0, rows // t, chunk, (zero, zero))

    return pl.pallas_call(
        body, out_shape=jax.ShapeDtypeStruct((2, rows, n), F32), grid=(n // 128,),
        in_specs=[pl.BlockSpec((2, rows, 128), lambda j: (0, 0, j)), pl.BlockSpec((2, 1, 128), lambda j: (0, 0, j))],
        out_specs=pl.BlockSpec((2, rows, 128), lambda j: (0, 0, j)), compiler_params=_params(("parallel",)), name=name)(bu, a)


def s5_scan_bwd(name, dz, z, a):
    _, rows, n = dz.shape
    t = min(SCAN_CHUNK, rows)
    steps = t.bit_length() - 1
    n_chunks = rows // t

    def body(dz_ref, z_ref, a_ref, lam_ref, da_ref):
        pr, pi = a_ref[0], a_ref[1]
        powers = _powers(pr, pi, steps)
        r = lax.broadcasted_iota(jnp.int32, (t, 128), 0)
        tr, ti = _cscan_chunk(jnp.where(r == t - 1, pr, 0.0), jnp.where(r == t - 1, -pi, 0.0), powers, True)

        def chunk(k, carry):
            ci = n_chunks - 1 - k
            base = pl.multiple_of(ci * t, t)
            vr, vi = _cscan_chunk(dz_ref[0, pl.ds(base, t), :], dz_ref[1, pl.ds(base, t), :], powers, True)
            cr, cim, dar, dai = carry
            lr = vr + tr * cr - ti * cim
            li = vi + tr * cim + ti * cr
            lam_ref[0, pl.ds(base, t), :] = lr
            lam_ref[1, pl.ds(base, t), :] = li
            pbase = pl.multiple_of(jnp.maximum(base - SUBLANES, 0), SUBLANES)
            keep = (ci > 0).astype(F32)
            pzr = z_ref[0, pl.ds(pbase, SUBLANES), :][SUBLANES - 1:SUBLANES, :] * keep
            pzi = z_ref[1, pl.ds(pbase, SUBLANES), :][SUBLANES - 1:SUBLANES, :] * keep
            zpr = _shift_down(z_ref[0, pl.ds(base, t), :], 1, pzr)
            zpi = _shift_down(z_ref[1, pl.ds(base, t), :], 1, pzi)
            dar = dar + jnp.sum(lr * zpr + li * zpi, axis=0, keepdims=True)
            dai = dai + jnp.sum(li * zpr - lr * zpi, axis=0, keepdims=True)
            return lr[0:1, :], li[0:1, :], dar, dai

        zero = jnp.zeros((1, 128), F32)
        _, _, dar, dai = lax.fori_loop(0, n_chunks, chunk, (zero, zero, zero, zero))
        da_ref[0] = dar
        da_ref[1] = dai

    seq = pl.BlockSpec((2, rows, 128), lambda j: (0, 0, j))
    vec = pl.BlockSpec((2, 1, 128), lambda j: (0, 0, j))
    return pl.pallas_call(
        body, out_shape=(jax.ShapeDtypeStruct((2, rows, n), F32), jax.ShapeDtypeStruct((2, 1, n), F32)), grid=(n // 128,),
        in_specs=[seq, seq, vec], out_specs=(seq, vec), compiler_params=_params(("parallel",)), name=name)(dz, z, a)


def _rscan_chunk(a, b, steps, reverse):
    shift = _shift_up if reverse else _shift_down
    for k in range(steps):
        s = 1 << k
        b = b + a * shift(b, s, 0.0)
        a = a * shift(a, s, 1.0)
    return a, b


def lru_scan_fwd(name, a, b):
    rows, n = a.shape
    t = min(SCAN_CHUNK, rows)
    steps = t.bit_length() - 1

    def body(a_ref, b_ref, h_ref):
        def chunk(ci, carry):
            base = pl.multiple_of(ci * t, t)
            pa, hb = _rscan_chunk(a_ref[pl.ds(base, t), :], b_ref[pl.ds(base, t), :], steps, False)
            h = hb + pa * carry
            h_ref[pl.ds(base, t), :] = h
            return h[t - 1:t, :]

        lax.fori_loop(0, rows // t, chunk, jnp.zeros((1, 128), F32))

    seq = pl.BlockSpec((rows, 128), lambda j: (0, j))
    return pl.pallas_call(body, out_shape=jax.ShapeDtypeStruct((rows, n), F32), grid=(n // 128,), in_specs=[seq, seq],
                          out_specs=seq, compiler_params=_params(("parallel",)), name=name)(a, b)


def lru_scan_bwd(name, dh, a, h):
    rows, n = a.shape
    t = min(SCAN_CHUNK, rows)
    steps = t.bit_length() - 1
    n_chunks = rows // t

    def body(dh_ref, a_ref, h_ref, da_ref, db_ref):
        def chunk(k, carry):
            ci = n_chunks - 1 - k
            base = pl.multiple_of(ci * t, t)
            nbase = pl.multiple_of(jnp.minimum(base + t, rows - SUBLANES), SUBLANES)
            a_next = a_ref[pl.ds(nbase, SUBLANES), :][0:1, :]
            an = _shift_up(a_ref[pl.ds(base, t), :], 1, a_next)
            pa, mb = _rscan_chunk(an, dh_ref[pl.ds(base, t), :], steps, True)
            mu = mb + pa * carry
            pbase = pl.multiple_of(jnp.maximum(base - SUBLANES, 0), SUBLANES)
            hp_row = h_ref[pl.ds(pbase, SUBLANES), :][SUBLANES - 1:SUBLANES, :] * (ci > 0).astype(F32)
            hp = _shift_down(h_ref[pl.ds(base, t), :], 1, hp_row)
            da_ref[pl.ds(base, t), :] = mu * hp
            db_ref[pl.ds(base, t), :] = mu
            return mu[0:1, :]

        lax.fori_loop(0, n_chunks, chunk, jnp.zeros((1, 128), F32))

    seq = pl.BlockSpec((rows, 128), lambda j: (0, j))
    return pl.pallas_call(
        body, out_shape=(jax.ShapeDtypeStruct((rows, n), F32), jax.ShapeDtypeStruct((rows, n), F32)), grid=(n // 128,),
        in_specs=[seq, seq, seq], out_specs=(seq, seq), compiler_params=_params(("parallel",)), name=name)(dh, a, h)


def _s5_param(lr, li, ls, bre, bim):
    st = jnp.exp(ls)
    er = jnp.exp(lr * st)
    th = li * st
    ar, ai = er * jnp.cos(th), er * jnp.sin(th)
    nr, ni = ar - 1.0, ai
    den = lr * lr + li * li
    cr, ci = (nr * lr + ni * li) / den, (ni * lr - nr * li) / den
    return ar, ai, cr * bre - ci * bim, cr * bim + ci * bre


def s5_param_fwd(name, lr, li, ls, bre, bim):
    gh, n = bre.shape

    def body(lr_ref, li_ref, ls_ref, bre_ref, bim_ref, a_ref, bb_ref):
        ar, ai, br, bi = _s5_param(lr_ref[...], li_ref[...], ls_ref[...], bre_ref[...], bim_ref[...])
        a_ref[0] = ar
        a_ref[1] = ai
        bb_ref[0] = br.astype(BF16)
        bb_ref[1] = bi.astype(BF16)

    return pl.pallas_call(body, out_shape=(jax.ShapeDtypeStruct((2, 1, n), F32), jax.ShapeDtypeStruct((2, gh, n), BF16)),
                          compiler_params=_params(), name=name)(lr, li, ls, bre, bim)


def s5_param_bwd(name, lr, li, ls, bre, bim, da, dbb, gsum):
    gh, n = bre.shape

    def body(lr_ref, li_ref, ls_ref, bre_ref, bim_ref, da_ref, dbb_ref, gs_ref, dlr_ref, dli_ref, dls_ref, dbre_ref, dbim_ref):
        _, vjp = jax.vjp(_s5_param, lr_ref[...], li_ref[...], ls_ref[...], bre_ref[...], bim_ref[...])
        dlr, dli, dls, dbre, dbim = vjp((da_ref[0], da_ref[1], dbb_ref[0], dbb_ref[1]))
        dlr_ref[...] = dlr
        dli_ref[...] = dli
        dls_ref[...] = jnp.dot(jnp.broadcast_to(dls, (SUBLANES, n)), gs_ref[...], preferred_element_type=F32,
                               precision=lax.Precision.HIGHEST)
        dbre_ref[...] = dbre
        dbim_ref[...] = dbim

    vec = jax.ShapeDtypeStruct((1, n), F32)
    mat = jax.ShapeDtypeStruct((gh, n), F32)
    return pl.pallas_call(body, out_shape=(vec, vec, jax.ShapeDtypeStruct((SUBLANES, 128), F32), mat, mat),
                          compiler_params=_params(), name=name)(lr, li, ls, bre, bim, da, dbb, gsum)


def sum_lead(name, x, tr):
    n, rows, cols = x.shape

    def body(x_ref, o_ref):
        acc = x_ref[0]
        for j in range(1, n):
            acc = acc + x_ref[j]
        o_ref[...] = acc

    return pl.pallas_call(
        body, out_shape=jax.ShapeDtypeStruct((rows, cols), x.dtype), grid=(rows // tr,),
        in_specs=[pl.BlockSpec((n, tr, cols), lambda i: (0, i, 0))], out_specs=pl.BlockSpec((tr, cols), lambda i: (i, 0)),
        compiler_params=_params(("parallel",)), name=name)(x)


def _adamw(w, g, m, v):
    m = ADAM_B1 * m + (1.0 - ADAM_B1) * g
    v = ADAM_B2 * v + (1.0 - ADAM_B2) * jnp.square(g)
    m_hat = m / (1.0 - ADAM_B1 ** ADAM_STEP)
    v_hat = v / (1.0 - ADAM_B2 ** ADAM_STEP)
    delta = -ADAM_LR * (m_hat / (jnp.sqrt(v_hat) + ADAM_EPS) + ADAM_WD * w)
    return delta, m, v


def adamw_sharded(name, w, m, v, g0, g1, split_cols, tile):
    _, r, c = w.shape
    if split_cols:
        nt = c // tile
        per = (c // 2) // tile
        wspec = pl.BlockSpec((None, r, tile), lambda l, t: (l, 0, t))
        gspec = pl.BlockSpec((None, r, tile), lambda l, t: (t // per, 0, t % per))
    else:
        nt = r // tile
        per = (r // 2) // tile
        wspec = pl.BlockSpec((None, tile, c), lambda l, t: (l, t, 0))
        gspec = pl.BlockSpec((None, tile, c), lambda l, t: (t // per, t % per, 0))

    def body(w_ref, m_ref, v_ref, g0_ref, g1_ref, g_ref, d_ref, nm_ref, nv_ref):
        g = jnp.where(pl.program_id(0) == 0, g0_ref[...], g1_ref[...])
        d, nm, nv = _adamw(w_ref[...], g, m_ref[...], v_ref[...])
        g_ref[...] = g
        d_ref[...] = d
        nm_ref[...] = nm
        nv_ref[...] = nv

    sds = jax.ShapeDtypeStruct(w.shape, F32)
    return pl.pallas_call(body, out_shape=(sds,) * 4, grid=(2, nt), in_specs=[wspec, wspec, wspec, gspec, gspec],
                          out_specs=(wspec,) * 4, compiler_params=_params(("parallel", "parallel")), name=name)(w, m, v, g0, g1)


def adamw_flat(name, w, g, m, v, tr):
    rows, cols = w.shape

    def body(w_ref, g_ref, m_ref, v_ref, d_ref, nm_ref, nv_ref):
        d, nm, nv = _adamw(w_ref[...], g_ref[...], m_ref[...], v_ref[...])
        d_ref[...] = d
        nm_ref[...] = nm
        nv_ref[...] = nv

    blk = pl.BlockSpec((tr, cols), lambda i: (i, 0))
    sds = jax.ShapeDtypeStruct((rows, cols), F32)
    return pl.pallas_call(body, out_shape=(sds,) * 3, grid=(rows // tr,), in_specs=[blk] * 4, out_specs=(blk,) * 3,
                          compiler_params=_params(("parallel",)), name=name)(w, g, m, v)


def _flips(axes):
    out = []
    for fx in ((0, 1) if "x" in axes else (0,)):
        for fy in ((0, 1) if "y" in axes else (0,)):
            for fc in ((0, 1) if "c" in axes else (0,)):
                if fx or fy or fc:
                    out.append((fx, fy, fc))
    return out


def _slot(pos, axes):
    s = 0
    for name, p in zip(("x", "y", "c"), pos):
        if name in axes:
            s = 2 * s + p
    return s


def _exchange(name, arrs, axes, scatter):
    flips = _flips(axes)
    n = len(flips) + 1
    na = len(arrs)

    def body(*refs):
        ins, outs = refs[:na], refs[na:2 * na]
        send_sems, recv_sems, local_sems = refs[2 * na:]
        me = (lax.axis_index("x"), lax.axis_index("y"), lax.axis_index("c"))
        my = _slot(me, axes)
        peers = [tuple((1 - p) if f else p for p, f in zip(me, fl)) for fl in flips]

        def src(a, dest_slot):
            return ins[a].at[dest_slot] if scatter else ins[a]

        local = [pltpu.make_async_copy(src(a, my), outs[a].at[my], local_sems.at[a]) for a in range(na)]
        for cp in local:
            cp.start()

        def remote(a, j, landing_slot, dest_slot):
            return pltpu.make_async_remote_copy(
                src_ref=src(a, dest_slot), dst_ref=outs[a].at[landing_slot], send_sem=send_sems.at[a * len(flips) + j],
                recv_sem=recv_sems.at[a * len(flips) + j], device_id=peers[j], device_id_type=pl.DeviceIdType.MESH)

        sends = [remote(a, j, my, _slot(peers[j], axes)) for a in range(na) for j in range(len(flips))]
        for cp in sends:
            cp.start()
        for a in range(na):
            for j in range(len(flips)):
                remote(a, j, _slot(peers[j], axes), _slot(peers[j], axes)).wait_recv()
        for cp in sends:
            cp.wait_send()
        for cp in local:
            cp.wait()

    if scatter:
        out_shape = tuple(jax.ShapeDtypeStruct(a.shape, a.dtype) for a in arrs)
    else:
        out_shape = tuple(jax.ShapeDtypeStruct((n,) + a.shape, a.dtype) for a in arrs)
    anyspec = pl.BlockSpec(memory_space=pl.ANY)
    return pl.pallas_call(
        body, out_shape=out_shape, in_specs=[anyspec] * na, out_specs=(anyspec,) * na,
        scratch_shapes=[pltpu.SemaphoreType.DMA((na * len(flips),)), pltpu.SemaphoreType.DMA((na * len(flips),)),
                        pltpu.SemaphoreType.DMA((na,))],
        name=name)(*arrs)


def all_gather(name, arrs, axes):
    return _exchange(name, arrs, axes, False)


def all_to_all(name, arrs, axes):
    return _exchange(name, arrs, axes, True)


def _pair_exchange(name, ins, in_specs, n_steps, tile, fn_send, fn_out, out_shape, out_spec):
    n_in = len(ins)

    def body(*refs):
        in_refs, o_ref = refs[:n_in], refs[n_in]
        send_buf, recv_buf, send_sems, recv_sems, credit = refs[n_in + 1:]
        i = pl.program_id(0)
        slot = lax.rem(i, 2)
        c = lax.axis_index("c")
        sibling = (lax.axis_index("x"), lax.axis_index("y"), 1 - c)
        vals = [r[...] for r in in_refs]
        send_buf[slot] = fn_send(*vals, c)

        @pl.when(i >= 2)
        def _():
            pl.semaphore_wait(credit, 1)

        copy = pltpu.make_async_remote_copy(
            src_ref=send_buf.at[slot], dst_ref=recv_buf.at[slot], send_sem=send_sems.at[slot], recv_sem=recv_sems.at[slot],
            device_id=sibling, device_id_type=pl.DeviceIdType.MESH)
        copy.start()
        copy.wait_recv()
        o_ref[...] = fn_out(*vals, recv_buf[slot], c).astype(o_ref.dtype)
        copy.wait_send()

        @pl.when(i < n_steps - 2)
        def _():
            pl.semaphore_signal(credit, inc=1, device_id=sibling, device_id_type=pl.DeviceIdType.MESH)

    return pl.pallas_call(
        body, out_shape=out_shape, grid=(n_steps,), in_specs=in_specs, out_specs=out_spec,
        scratch_shapes=[pltpu.VMEM((2,) + tile, F32), pltpu.VMEM((2,) + tile, F32), pltpu.SemaphoreType.DMA((2,)),
                        pltpu.SemaphoreType.DMA((2,)), pltpu.SemaphoreType.REGULAR],
        compiler_params=_params(("arbitrary",)), name=name)(*ins)


def _tile_rows(rows, cols):
    return _row_tile(rows, max(SUBLANES, (3 << 19) // (4 * cols)))


def reduce_cores(name, g):
    _, m, cols = g.shape
    tr = _tile_rows(m, cols)

    def fn_send(g0, g1, c):
        return jnp.where(c == 0, g1, g0)

    def fn_out(g0, g1, got, c):
        return jnp.where(c == 0, g0, g1) + got

    return _pair_exchange(
        name, [g, g], [pl.BlockSpec((None, tr, cols), lambda i: (0, i, 0)), pl.BlockSpec((None, tr, cols), lambda i: (1, i, 0))],
        m // tr, (tr, cols), fn_send, fn_out, jax.ShapeDtypeStruct((m, cols), BF16), pl.BlockSpec((tr, cols), lambda i: (i, 0)))


def sum_and_share(name, parts):
    n, r, cols = parts.shape
    tr = _tile_rows(r, cols)

    def total(p):
        acc = p[0].astype(F32)
        for j in range(1, n):
            acc = acc + p[j].astype(F32)
        return acc

    def fn_send(p, c):
        return total(p)

    def fn_out(p, got, c):
        mine = total(p)
        return jnp.stack([jnp.where(c == 0, mine, got), jnp.where(c == 0, got, mine)])

    return _pair_exchange(
        name, [parts], [pl.BlockSpec((n, tr, cols), lambda i: (0, i, 0))], r // tr, (tr, cols), fn_send, fn_out,
        jax.ShapeDtypeStruct((2, r, cols), F32), pl.BlockSpec((2, tr, cols), lambda i: (0, i, 0)))


def _block_diag(blocks):
    g, r, c = blocks.shape
    eye = jnp.eye(g, dtype=blocks.dtype)
    return (blocks[:, :, None, :] * eye[:, None, :, None]).reshape(g * r, g * c)


def _diag_blocks(mat, g):
    r, c = mat.shape[0] // g, mat.shape[1] // g
    eye = jnp.eye(g, dtype=mat.dtype)
    return (mat.reshape(g, r, g, c) * eye[:, None, :, None]).sum(axis=2)


def _halves(gfull, shards):
    rows, cols = gfull.shape
    return gfull.reshape(shards, 2, rows // shards // 2, cols).transpose(1, 0, 2, 3)


def _step(inp):
    x = inp['x'][0]
    target = inp['loss_target'][0]
    rows, d = x.shape
    depth = inp['w_in'].shape[0]
    mix_w = d // 4
    n_state = S5_GROUPS * S5_STATE
    ffn_half = inp['ffn_w_up'].shape[2]
    tm = min(512, rows)
    tc = min(256, rows)
    tl = min(512, rows)
    xy = ("x", "y")

    send, keys = [], []
    for l in range(depth):
        for nme in BIG:
            send.append(inp[nme][l].astype(BF16))
            keys.append((nme, l))
    for nme in SMALL_SHARDED:
        send.append(inp[nme])
        keys.append((nme, None))
    gathered = dict(zip(keys, all_gather("gather_weights", send, xy)))

    def full_small(nme):
        g = gathered[(nme, None)]
        return g.transpose(1, 2, 0, 3).reshape(g.shape[1], g.shape[2], 4 * g.shape[3])

    cv_w_dw, lru_w_conv, ffn_w_dw = full_small('cv_w_dw'), full_small('lru_w_conv'), full_small('ffn_w_dw')

    gsum = jnp.repeat(jnp.eye(128, dtype=F32)[:S5_GROUPS], S5_STATE, axis=0)

    saved = []
    grads = {nme: [None] * depth for nme in WEIGHTS}
    xcur = x
    for l in range(depth):
        w_in = gathered[('w_in', l)]
        w_out = gathered[('w_out', l)].reshape(d, d)
        w_up = gathered[('ffn_w_up', l)]
        w_down = gathered[('ffn_w_down', l)].reshape(2, ffn_half, d)
        w_glu = gathered[('s5_w_glu', l)].reshape(mix_w, mix_w)
        w_pw = gathered[('cv_w_pw', l)].reshape(mix_w, mix_w)
        ncol = w_in.shape[2]
        vec = lambda a: a[l].reshape(1, -1)

        lam_re, lam_im = vec(inp['s5_lam_re']), vec(inp['s5_lam_im'])
        log_step = jnp.broadcast_to(inp['s5_log_step'][l][:, None], (S5_GROUPS, S5_STATE)).reshape(1, n_state)
        b_re = _block_diag(inp['s5_b_re'][l].transpose(0, 2, 1))
        b_im = _block_diag(inp['s5_b_im'][l].transpose(0, 2, 1))
        c_cat = jnp.stack([_block_diag(inp['s5_c_re'][l].transpose(0, 2, 1)),
                           -_block_diag(inp['s5_c_im'][l].transpose(0, 2, 1))]).astype(BF16)
        a_bar, b_bar = s5_param_fwd(f"s5_param_fwd{l}", lam_re, lam_im, log_step, b_re, b_im)
        w_r = _block_diag(inp['lru_w_r'][l]).astype(BF16)
        w_i = _block_diag(inp['lru_w_i'][l]).astype(BF16)
        pool_bd = _block_diag(inp['pool_w'][l]).astype(BF16)
        post_pars = [vec(inp['s5_d']), w_glu, vec(inp['s5_b_glu']), vec(inp['cv_ln_g']), vec(inp['cv_ln_b']), w_pw,
                     vec(inp['cv_b_pw']), pool_bd, vec(inp['pool_scale'])]
        gate_pars = [w_r, w_i, vec(inp['lru_b_r']), vec(inp['lru_b_i']), vec(inp['lru_lam'])]

        h = rms_fwd(f"rms_mix{l}", xcur, vec(inp['norm_mix_g']), tm)
        proj = _mm(f"proj{l}", h, w_in, jax.ShapeDtypeStruct((rows, 4 * ncol), F32), (rows // tm, 4),
                   pl.BlockSpec((tm, d), lambda i, j: (i, 0)), pl.BlockSpec((None, d, ncol), lambda i, j: (j, 0, 0)),
                   pl.BlockSpec((tm, ncol), lambda i, j: (i, j)), NN)
        proj3 = proj.reshape(1, rows, 4 * ncol)
        nh = n_state // 2
        bu = _mm(f"s5_bu{l}", proj, b_bar, jax.ShapeDtypeStruct((2, rows, n_state), F32), (rows // tm, 2, 2),
                 pl.BlockSpec((tm, mix_w), lambda i, c, n: (i, 0)), pl.BlockSpec((None, mix_w, nh), lambda i, c, n: (c, 0, n)),
                 pl.BlockSpec((None, tm, nh), lambda i, c, n: (c, i, n)), NN)
        z = s5_scan_fwd(f"s5_scan{l}", bu, a_bar)
        y_ssm = _mm(f"s5_read{l}", z, c_cat, jax.ShapeDtypeStruct((rows, mix_w), F32), (rows // tm, 4),
                    pl.BlockSpec((None, tm, nh), lambda i, k: (k // 2, i, k % 2)),
                    pl.BlockSpec((None, nh, mix_w), lambda i, k: (k // 2, k % 2, 0)),
                    pl.BlockSpec((tm, mix_w), lambda i, k: (i, 0)), NN, k_axis=1)
        (h0,) = _rowwise(f"cv_glu{l}", _glu, [(proj, 1, mix_w), (proj, 2, mix_w)], [], 1, [(mix_w, F32)], tm)
        h1 = dwconv_fwd(f"cv_conv{l}", h0.reshape(1, rows, mix_w), 0, mix_w, cv_w_dw[l][None], vec(inp['cv_b_dw'])[None],
                        CV_TAPS, tc)[0]
        xc = dwconv_fwd(f"lru_conv{l}", proj3, 3, mix_w, lru_w_conv[l][None], vec(inp['lru_b_conv'])[None], LRU_TAPS, tc)[0]
        a_t, b_t = _rowwise(f"lru_gate{l}", _lru_gate, [(xc, 0, mix_w)], gate_pars, 2, [(mix_w, F32), (mix_w, F32)], tm)
        hseq = lru_scan_fwd(f"lru_scan{l}", a_t, b_t)
        dgp = pool_fwd(f"pool{l}", proj, 5, tc)
        post_rows = [(y_ssm, 0, mix_w), (proj, 0, mix_w), (h1, 0, mix_w), (hseq, 0, mix_w), (proj, 4, mix_w), (dgp, 0, mix_w)]
        (mixed,) = _rowwise(f"mix_post{l}", _mix_post, post_rows, post_pars, 1, [(d, BF16)], tm)
        x1 = _mm(f"out_proj{l}", mixed, w_out, jax.ShapeDtypeStruct((rows, d), F32), (rows // tm, 2),
                 pl.BlockSpec((tm, d), lambda i, j: (i, 0)), pl.BlockSpec((d, d // 2), lambda i, j: (0, j)),
                 pl.BlockSpec((tm, d // 2), lambda i, j: (i, j)), NN,
                 add=xcur, add_spec=pl.BlockSpec((tm, d // 2), lambda i, j: (i, j)))

        h2 = rms_fwd(f"rms_ffn{l}", x1, vec(inp['norm_ffn_g']), tm)
        tu = min(256, rows)
        up = _mm(f"ffn_up{l}", h2, w_up, jax.ShapeDtypeStruct((4, rows, ffn_half), F32), (4, rows // tu),
                 pl.BlockSpec((tu, d), lambda k, i: (i, 0)), pl.BlockSpec((None, d, ffn_half), lambda k, i: (k, 0, 0)),
                 pl.BlockSpec((None, tu, ffn_half), lambda k, i: (k, i, 0)), NN)
        w_dw = ffn_w_dw[l].reshape(FFN_TAPS, 2, ffn_half).transpose(1, 0, 2)
        b_dw = inp['ffn_b_dw'][l].reshape(2, 1, ffn_half)
        gc = dwconv_fwd(f"ffn_conv{l}", up, 0, ffn_half, w_dw, b_dw, FFN_TAPS, tc)
        act = ffn_act_fwd(f"ffn_act{l}", gc, up, tc)
        x2 = _mm(f"ffn_down{l}", act, w_down, jax.ShapeDtypeStruct((rows, d), F32), (rows // tm, 4),
                 pl.BlockSpec((2, tm, ffn_half), lambda i, j: (0, i, 0)), pl.BlockSpec((2, ffn_half, d // 4), lambda i, j: (0, 0, j)),
                 pl.BlockSpec((tm, d // 4), lambda i, j: (i, j)), NN, inner=("lead", 2),
                 add=x1, add_spec=pl.BlockSpec((tm, d // 4), lambda i, j: (i, j)))
        saved.append(dict(x=xcur, h=h, proj=proj, z=z, y_ssm=y_ssm, h0=h0, h1=h1, xc=xc, a_t=a_t, hseq=hseq, dgp=dgp,
                          mixed=mixed, x1=x1, h2=h2, up=up, gc=gc, act=act, w_in=w_in, w_out=w_out, w_up=w_up, w_down=w_down,
                          a_bar=a_bar, b_bar=b_bar, c_cat=c_cat, post_pars=post_pars, gate_pars=gate_pars, w_dw=w_dw,
                          s5=(lam_re, lam_im, log_step, b_re, b_im), cv_w=cv_w_dw[l][None], lru_w=lru_w_conv[l][None]))
        xcur = x2

    loss_row, dx, dg_final = final_loss("final_loss", xcur, inp['norm_final_g'].reshape(1, d), target, tm)
    grads['norm_final_g'] = dg_final.reshape(d)

    big_g = {nme: [None] * depth for nme in BIG}
    for l in reversed(range(depth)):
        s = saved[l]
        ncol = s['w_in'].shape[2]
        nh = n_state // 2
        tu = min(256, rows)
        dact = _mm(f"d_act{l}", dx, s['w_down'], jax.ShapeDtypeStruct((2, rows, ffn_half), F32), (2, rows // tu),
                   pl.BlockSpec((tu, d), lambda k, i: (i, 0)), pl.BlockSpec((None, ffn_half, d), lambda k, i: (k, 0, 0)),
                   pl.BlockSpec((None, tu, ffn_half), lambda k, i: (k, i, 0)), NT)
        tn = d // 4
        tk = min(1024, rows)
        big_g['ffn_w_down'][l] = _mm(
            f"dw_down{l}", s['act'], dx, jax.ShapeDtypeStruct((2, 2, ffn_half, d // 2), F32), (2, 4, rows // tk),
            pl.BlockSpec((None, tk, ffn_half), lambda hh, n, k: (hh, k, 0)), pl.BlockSpec((tk, tn), lambda hh, n, k: (k, n)),
            pl.BlockSpec((None, None, ffn_half, tn), lambda hh, n, k: (n // 2, hh, 0, n % 2)), TN, k_axis=2)
        dgc, dval = ffn_act_bwd(f"ffn_act_bwd{l}", s['gc'], s['up'], dact, tc)
        dgate, dw_dw, db_dw = dwconv_bwd(f"ffn_conv_bwd{l}", dgc, s['up'], 0, ffn_half, s['w_dw'], FFN_TAPS, tc, dx_dtype=BF16)
        grads['ffn_w_dw'][l] = dw_dw.transpose(1, 0, 2).reshape(FFN_TAPS, 2 * ffn_half)
        grads['ffn_b_dw'][l] = db_dw.reshape(2 * ffn_half)
        dup = jnp.concatenate([dgate, dval], axis=0)
        tm2 = min(1024, rows)
        dh2 = _mm(f"d_h2{l}", dup, s['w_up'], jax.ShapeDtypeStruct((rows, d), F32), (rows // tm2, 2, 4),
                  pl.BlockSpec((None, tm2, ffn_half), lambda i, j, k: (k, i, 0)), pl.BlockSpec((None, d // 2, ffn_half), lambda i, j, k: (k, j, 0)),
                  pl.BlockSpec((tm2, d // 2), lambda i, j, k: (i, j)), NT, k_axis=2)
        tmm = d // 4
        big_g['ffn_w_up'][l] = _mm(
            f"dw_up{l}", dup, s['h2'], jax.ShapeDtypeStruct((2, 4, ffn_half, d // 2), F32), (4, 4, rows // tk),
            pl.BlockSpec((None, tk, ffn_half), lambda k4, n, k: (k4, k, 0)), pl.BlockSpec((tk, tn), lambda k4, n, k: (k, n)),
            pl.BlockSpec((None, None, ffn_half, tn), lambda k4, n, k: (n // 2, k4, 0, n % 2)), TN, k_axis=2)
        dx1, dg = rms_bwd(f"rms_ffn_bwd{l}", s['x1'], inp['norm_ffn_g'][l].reshape(1, d), dh2, dx, tm)
        grads['norm_ffn_g'][l] = dg.reshape(d)
        dmixed = _mm(f"d_mixed{l}", dx1, s['w_out'], jax.ShapeDtypeStruct((rows, d), F32), (rows // tm, 4),
                     pl.BlockSpec((tm, d), lambda i, j: (i, 0)), pl.BlockSpec((d // 4, d), lambda i, j: (j, 0)),
                     pl.BlockSpec((tm, d // 4), lambda i, j: (i, j)), NT)
        tq = mix_w // 2
        big_g['w_out'][l] = _mm(
            f"dw_out{l}", s['mixed'], dx1, jax.ShapeDtypeStruct((2, 4, tq, d), F32), (8, rows // tk),
            pl.BlockSpec((tk, tq), lambda t, k: (k, t)), pl.BlockSpec((tk, d), lambda t, k: (k, 0)),
            pl.BlockSpec((None, None, tq, d), lambda t, k: (t % 2, t // 2, 0, 0)), TN, k_axis=1)
        post_rows = [(s['y_ssm'], 0, mix_w), (s['proj'], 0, mix_w), (s['h1'], 0, mix_w), (s['hseq'], 0, mix_w),
                     (s['proj'], 4, mix_w), (s['dgp'], 0, mix_w), (dmixed, 0, d)]
        res = _rowwise(f"mix_post_bwd{l}", _mix_post, post_rows, s['post_pars'], 1, [(mix_w, F32)] * 6, tm, with_grads=True)
        dy_ssm, du_dir, dh1, dhseq, dlru_g, ddgp = res[:6]
        dd, dwglu, dbglu, dlng, dlnb, dwpw, dbpw, dpoolbd, dscale = res[6:]
        grads['s5_d'][l], grads['s5_b_glu'][l] = dd.reshape(mix_w), dbglu.reshape(mix_w)
        grads['cv_ln_g'][l], grads['cv_ln_b'][l], grads['cv_b_pw'][l] = dlng.reshape(mix_w), dlnb.reshape(mix_w), dbpw.reshape(mix_w)
        grads['pool_w'][l] = _diag_blocks(dpoolbd, len(POOL_WINDOWS))
        grads['pool_scale'][l] = dscale.reshape(mix_w)
        big_g['s5_w_glu'][l] = _halves(dwglu, 4)
        big_g['cv_w_pw'][l] = _halves(dwpw, 4)
        dz = _mm(f"s5_dz{l}", dy_ssm, s['c_cat'], jax.ShapeDtypeStruct((2, rows, n_state), F32), (rows // tm, 2, 2),
                 pl.BlockSpec((tm, mix_w), lambda i, c, n: (i, 0)), pl.BlockSpec((None, nh, mix_w), lambda i, c, n: (c, n, 0)),
                 pl.BlockSpec((None, tm, nh), lambda i, c, n: (c, i, n)), NT)
        dccat = _mm(f"s5_dc{l}", s['z'], dy_ssm, jax.ShapeDtypeStruct((2, n_state, mix_w), F32), (2, n_state // mix_w, rows // tl),
                    pl.BlockSpec((None, tl, mix_w), lambda c, m, k: (c, k, m)), pl.BlockSpec((tl, mix_w), lambda c, m, k: (k, 0)),
                    pl.BlockSpec((None, mix_w, mix_w), lambda c, m, k: (c, m, 0)), TN, k_axis=2)
        grads['s5_c_re'][l] = _diag_blocks(dccat[0], S5_GROUPS).transpose(0, 2, 1)
        grads['s5_c_im'][l] = -_diag_blocks(dccat[1], S5_GROUPS).transpose(0, 2, 1)
        lam, da_bar = s5_scan_bwd(f"s5_scan_bwd{l}", dz, s['z'], s['a_bar'])
        du = _mm(f"s5_du{l}", lam, s['b_bar'], jax.ShapeDtypeStruct((rows, mix_w), F32), (rows // tm, 4),
                 pl.BlockSpec((None, tm, nh), lambda i, k: (k // 2, i, k % 2)), pl.BlockSpec((None, mix_w, nh), lambda i, k: (k // 2, 0, k % 2)),
                 pl.BlockSpec((tm, mix_w), lambda i, k: (i, 0)), NT, k_axis=1,
                 add=du_dir, add_spec=pl.BlockSpec((tm, mix_w), lambda i, k: (i, 0)))
        dbbar = _mm(f"s5_db{l}", s['proj'], lam, jax.ShapeDtypeStruct((2, mix_w, n_state), F32), (2, 2, rows // tl),
                    pl.BlockSpec((tl, mix_w), lambda c, n, k: (k, 0)), pl.BlockSpec((None, tl, nh), lambda c, n, k: (c, k, n)),
                    pl.BlockSpec((None, mix_w, nh), lambda c, n, k: (c, 0, n)), TN, k_axis=2)
        dlr, dli, dls, dbre, dbim = s5_param_bwd(f"s5_param_bwd{l}", *s['s5'], da_bar, dbbar, gsum)
        grads['s5_lam_re'][l] = dlr.reshape(S5_GROUPS, S5_STATE)
        grads['s5_lam_im'][l] = dli.reshape(S5_GROUPS, S5_STATE)
        grads['s5_log_step'][l] = dls[0, :S5_GROUPS]
        grads['s5_b_re'][l] = _diag_blocks(dbre, S5_GROUPS).transpose(0, 2, 1)
        grads['s5_b_im'][l] = _diag_blocks(dbim, S5_GROUPS).transpose(0, 2, 1)
        dh0, dw_cv, db_cv = dwconv_bwd(f"cv_conv_bwd{l}", dh1.reshape(1, rows, mix_w), s['h0'].reshape(1, rows, mix_w), 0, mix_w,
                                       s['cv_w'], CV_TAPS, tc)
        grads['cv_w_dw'][l], grads['cv_b_dw'][l] = dw_cv[0], db_cv.reshape(mix_w)
        dv, dgg = _rowwise(f"cv_glu_bwd{l}", _glu, [(s['proj'], 1, mix_w), (s['proj'], 2, mix_w), (dh0[0], 0, mix_w)], [], 1,
                           [(mix_w, F32)] * 2, tm, with_grads=True)
        da_t, db_t = lru_scan_bwd(f"lru_scan_bwd{l}", dhseq, s['a_t'], s['hseq'])
        res = _rowwise(f"lru_gate_bwd{l}", _lru_gate, [(s['xc'], 0, mix_w), (da_t, 0, mix_w), (db_t, 0, mix_w)], s['gate_pars'], 2,
                       [(mix_w, F32)], tm, with_grads=True)
        dxc, dwr, dwi, dbr, dbi, dlam = res
        grads['lru_w_r'][l], grads['lru_w_i'][l] = _diag_blocks(dwr, LRU_HEADS), _diag_blocks(dwi, LRU_HEADS)
        grads['lru_b_r'][l], grads['lru_b_i'][l], grads['lru_lam'][l] = dbr.reshape(mix_w), dbi.reshape(mix_w), dlam.reshape(mix_w)
        dlx, dw_lc, db_lc = dwconv_bwd(f"lru_conv_bwd{l}", dxc.reshape(1, rows, mix_w), s['proj'].reshape(1, rows, 4 * ncol), 3, mix_w,
                                       s['lru_w'], LRU_TAPS, tc)
        grads['lru_w_conv'][l], grads['lru_b_conv'][l] = dw_lc[0], db_lc.reshape(mix_w)
        dpx = pool_bwd(f"pool_bwd{l}", ddgp, tc)
        dproj = jnp.concatenate([du, dv, dgg, dlx[0], dlru_g, dpx], axis=-1)
        dh = _mm(f"d_h{l}", dproj, s['w_in'], jax.ShapeDtypeStruct((rows, d), F32), (rows // tm, 4),
                 pl.BlockSpec((tm, 4 * ncol), lambda i, j: (i, 0)), pl.BlockSpec((4, d // 4, ncol), lambda i, j: (0, j, 0)),
                 pl.BlockSpec((tm, d // 4), lambda i, j: (i, j)), NT, inner=("cols", 4))
        tk2 = min(2048, rows)
        big_g['w_in'][l] = _mm(
            f"dw_in{l}", s['h'], dproj, jax.ShapeDtypeStruct((2, 4, d // 2, ncol), F32), (4, 4, rows // tk2),
            pl.BlockSpec((tk2, tmm), lambda k4, m, k: (k, m)), pl.BlockSpec((tk2, ncol), lambda k4, m, k: (k, k4)),
            pl.BlockSpec((None, None, tmm, ncol), lambda k4, m, k: (m // 2, k4, m % 2, 0)), TN, k_axis=2)
        dx, dg = rms_bwd(f"rms_mix_bwd{l}", s['x'], inp['norm_mix_g'][l].reshape(1, d), dh, dx1, tm)
        grads['norm_mix_g'][l] = dg.reshape(d)

    big_keys = [(nme, l) for nme in BIG for l in range(depth)]
    g_list = []
    for nme, l in big_keys:
        g = big_g[nme][l]
        if nme == 'ffn_w_down':
            g = g.reshape(2, 4, ffn_half // 2, d // 2)
        g_list.append(g)
    s_list = [reduce_cores(f"reduce_cores_{nme}{l}", g.reshape(2, -1, g.shape[-1])).reshape(g.shape[1:])
              for (nme, l), g in zip(big_keys, g_list)]
    r2 = all_to_all("reduce_chips", s_list, xy)
    t_full = {key: sum_and_share(f"share_cores_{key[0]}{key[1]}", r) for key, r in zip(big_keys, r2)}

    outs = {}
    tiles = {'w_in': 256, 'w_out': 128, 'ffn_w_up': 128, 'ffn_w_down': 256, 's5_w_glu': 64, 'cv_w_pw': 64}
    for nme in BIG:
        g0, g1 = t_full[(nme, 0)], t_full[(nme, 1)]
        if nme == 'ffn_w_up':
            res = adamw_sharded(f"adamw_{nme}", *(jnp.swapaxes(inp[p + nme], 1, 2) for p in ('', 'm_', 'v_')), g0, g1, True, tiles[nme])
            outs[nme] = tuple(jnp.swapaxes(r, 1, 2) for r in res)
        else:
            outs[nme] = adamw_sharded(f"adamw_{nme}", inp[nme], inp['m_' + nme], inp['v_' + nme], g0, g1, nme == 'ffn_w_down', tiles[nme])

    small = [nme for nme in WEIGHTS if nme not in BIG]
    full_g = {nme: (grads[nme] if nme == 'norm_final_g' else jnp.stack(grads[nme])) for nme in small}
    flat = jnp.concatenate([full_g[nme].reshape(-1) for nme in small])
    n_flat = flat.shape[0]
    pad = (-n_flat) % (128 * 64)
    packed = jnp.pad(flat, (0, pad)).reshape(-1, 128)
    (g8,) = all_gather("gather_small", [packed], ("x", "y", "c"))
    gsum_small = sum_lead("sum_small", g8, 64).reshape(-1)
    my_chip = 2 * lax.axis_index("x") + lax.axis_index("y")
    red, off = {}, 0
    for nme in small:
        g = gsum_small[off:off + full_g[nme].size].reshape(full_g[nme].shape)
        off += full_g[nme].size
        if nme in SMALL_SHARDED:
            width = inp[nme].shape[2]
            g = lax.dynamic_slice_in_dim(g, my_chip * width, width, axis=2)
        red[nme] = g

    def pack(tree):
        f = jnp.concatenate([tree[nme].reshape(-1) for nme in small])
        return jnp.pad(f, (0, (-f.shape[0]) % (128 * 64))).reshape(-1, 128)

    pd, pm, pv = adamw_flat("adamw_small", pack({n_: inp[n_] for n_ in small}), pack(red), pack({n_: inp['m_' + n_] for n_ in small}),
                            pack({n_: inp['v_' + n_] for n_ in small}), 64)
    off = 0
    for nme in small:
        size, shape = inp[nme].size, inp[nme].shape
        outs[nme] = (red[nme],) + tuple(p.reshape(-1)[off:off + size].reshape(shape) for p in (pd, pm, pv))
        off += size

    loss = lax.psum(loss_row[0, 0], ("x", "y", "c"))
    result = [loss, dx[None]]
    for part in range(4):
        result += [outs[nme][part] for nme in WEIGHTS]
    return tuple(result)


def kernel(x, norm_mix_g, w_in, s5_lam_re, s5_lam_im, s5_log_step, s5_b_re, s5_b_im, s5_c_re, s5_c_im, s5_d, s5_w_glu, s5_b_glu, cv_w_dw, cv_b_dw, cv_ln_g, cv_ln_b, cv_w_pw, cv_b_pw, lru_w_conv, lru_b_conv, lru_w_r, lru_b_r, lru_w_i, lru_b_i, lru_lam, pool_w, pool_scale, w_out, norm_ffn_g, ffn_w_up, ffn_w_dw, ffn_b_dw, ffn_w_down, norm_final_g, loss_target, m_norm_mix_g, m_w_in, m_s5_lam_re, m_s5_lam_im, m_s5_log_step, m_s5_b_re, m_s5_b_im, m_s5_c_re, m_s5_c_im, m_s5_d, m_s5_w_glu, m_s5_b_glu, m_cv_w_dw, m_cv_b_dw, m_cv_ln_g, m_cv_ln_b, m_cv_w_pw, m_cv_b_pw, m_lru_w_conv, m_lru_b_conv, m_lru_w_r, m_lru_b_r, m_lru_w_i, m_lru_b_i, m_lru_lam, m_pool_w, m_pool_scale, m_w_out, m_norm_ffn_g, m_ffn_w_up, m_ffn_w_dw, m_ffn_b_dw, m_ffn_w_down, m_norm_final_g, v_norm_mix_g, v_w_in, v_s5_lam_re, v_s5_lam_im, v_s5_log_step, v_s5_b_re, v_s5_b_im, v_s5_c_re, v_s5_c_im, v_s5_d, v_s5_w_glu, v_s5_b_glu, v_cv_w_dw, v_cv_b_dw, v_cv_ln_g, v_cv_ln_b, v_cv_w_pw, v_cv_b_pw, v_lru_w_conv, v_lru_b_conv, v_lru_w_r, v_lru_b_r, v_lru_w_i, v_lru_b_i, v_lru_lam, v_pool_w, v_pool_scale, v_w_out, v_norm_ffn_g, v_ffn_w_up, v_ffn_w_dw, v_ffn_b_dw, v_ffn_w_down, v_norm_final_g):
    inp = dict(locals())
    return _step(inp)
```

```python
import functools

import jax
import jax.numpy as jnp
from jax import lax
from jax.experimental import pallas as pl
from jax.experimental.pallas import tpu as pltpu

F32 = jnp.float32
BF16 = jnp.bfloat16

VMEM_LIMIT_BYTES = 56 * 1024 * 1024
SUBLANES = 8

EPS = 1e-6
S5_GROUPS, S5_STATE, S5_GROUP_CH = 32, 64, 16
LRU_HEADS, LRU_C = 8, 8.0
POOL_WINDOWS = (2, 4, 8, 16)
CV_TAPS, LRU_TAPS, FFN_TAPS = 31, 4, 3
SCAN_CHUNK = 64

ADAM_LR, ADAM_B1, ADAM_B2, ADAM_EPS, ADAM_WD, ADAM_STEP = 0.001, 0.9, 0.999, 1e-08, 0.01, 10

NN = ((1,), (0,))
NT = ((1,), (1,))
TN = ((0,), (0,))

WEIGHTS = ['norm_mix_g', 'w_in', 's5_lam_re', 's5_lam_im', 's5_log_step', 's5_b_re', 's5_b_im', 's5_c_re', 's5_c_im',
           's5_d', 's5_w_glu', 's5_b_glu', 'cv_w_dw', 'cv_b_dw', 'cv_ln_g', 'cv_ln_b', 'cv_w_pw', 'cv_b_pw',
           'lru_w_conv', 'lru_b_conv', 'lru_w_r', 'lru_b_r', 'lru_w_i', 'lru_b_i', 'lru_lam', 'pool_w', 'pool_scale',
           'w_out', 'norm_ffn_g', 'ffn_w_up', 'ffn_w_dw', 'ffn_b_dw', 'ffn_w_down', 'norm_final_g']
BIG = ('w_in', 'w_out', 'ffn_w_up', 'ffn_w_down', 's5_w_glu', 'cv_w_pw')
SMALL_SHARDED = {'cv_w_dw': 2, 'lru_w_conv': 2, 'ffn_w_dw': 2}


def _params(sem=None):
    if sem is None:
        return pltpu.CompilerParams(vmem_limit_bytes=VMEM_LIMIT_BYTES)
    return pltpu.CompilerParams(dimension_semantics=sem, vmem_limit_bytes=VMEM_LIMIT_BYTES)


def _row_tile(rows, cap):
    best = SUBLANES
    for t in range(SUBLANES, min(rows, cap) + 1, SUBLANES):
        if rows % t == 0:
            best = t
    return best


def _bdot(a, b, dims=NN):
    return lax.dot_general(a.astype(BF16), b.astype(BF16), (dims, ((), ())), preferred_element_type=F32)


@jax.custom_vjp
def bdot(a, b):
    return _bdot(a, b)


def _bdot_fwd(a, b):
    return _bdot(a, b), (a, b)


def _bdot_bwd(res, g):
    a, b = res
    return _bdot(g, b, NT).astype(a.dtype), _bdot(a, g, TN).astype(b.dtype)


bdot.defvjp(_bdot_fwd, _bdot_bwd)


def _mm(name, a, b, out_sds, grid, a_spec, b_spec, o_spec, dims, k_axis=None, add=None, add_spec=None, inner=None):
    nk = grid[k_axis] if k_axis is not None else 1
    has_add = add is not None
    acc_shape = tuple(d for d in o_spec.block_shape if d is not None)
    acc_in_out = out_sds.dtype == F32

    def product(a_ref, b_ref):
        if inner is None:
            return _bdot(a_ref[...], b_ref[...], dims)
        kind, n = inner
        width = a_ref.shape[-1] // n
        acc = None
        for j in range(n):
            a_j = a_ref[j] if kind == "lead" else a_ref[:, j * width:(j + 1) * width]
            p = _bdot(a_j, b_ref[j], dims)
            acc = p if acc is None else acc + p
        return acc

    def body(*refs):
        a_ref, b_ref = refs[0], refs[1]
        add_ref = refs[2] if has_add else None
        o_ref = refs[3] if has_add else refs[2]
        prod = product(a_ref, b_ref)
        if k_axis is None:
            if has_add:
                prod = prod + add_ref[...]
            o_ref[...] = prod.astype(o_ref.dtype)
        else:
            acc_ref = o_ref if acc_in_out else refs[-1]
            k = pl.program_id(k_axis)

            @pl.when(k == 0)
            def _():
                acc_ref[...] = prod

            @pl.when(k > 0)
            def _():
                acc_ref[...] += prod

            if has_add or not acc_in_out:
                @pl.when(k == nk - 1)
                def _():
                    r = acc_ref[...]
                    if has_add:
                        r = r + add_ref[...]
                    o_ref[...] = r.astype(o_ref.dtype)

    sem = tuple("arbitrary" if d == k_axis else "parallel" for d in range(len(grid)))
    in_specs = [a_spec, b_spec] + ([add_spec] if has_add else [])
    args = (a, b) + ((add,) if has_add else ())
    scratch = [pltpu.VMEM(acc_shape, F32)] if (k_axis is not None and not acc_in_out) else []
    return pl.pallas_call(body, out_shape=out_sds, grid=grid, in_specs=in_specs, out_specs=o_spec,
                          scratch_shapes=scratch, compiler_params=_params(sem), name=name)(*args)


def _rms(x, g):
    return x * lax.rsqrt(jnp.mean(x * x, axis=-1, keepdims=True) + EPS) * g


def rms_fwd(name, x, g, tm):
    rows, d = x.shape

    def body(x_ref, g_ref, o_ref):
        o_ref[...] = _rms(x_ref[...], g_ref[...]).astype(BF16)

    return pl.pallas_call(
        body, out_shape=jax.ShapeDtypeStruct((rows, d), BF16), grid=(rows // tm,),
        in_specs=[pl.BlockSpec((tm, d), lambda i: (i, 0)), pl.BlockSpec((1, d), lambda i: (0, 0))],
        out_specs=pl.BlockSpec((tm, d), lambda i: (i, 0)), compiler_params=_params(("parallel",)), name=name)(x, g)


def rms_bwd(name, x, g, dh, dres, tm):
    rows, d = x.shape

    def body(x_ref, g_ref, dh_ref, dres_ref, dx_ref, dg_ref):
        _, vjp = jax.vjp(_rms, x_ref[...], g_ref[...])
        dx, dg = vjp(dh_ref[...])
        dx_ref[...] = dx + dres_ref[...]

        @pl.when(pl.program_id(0) == 0)
        def _():
            dg_ref[...] = jnp.zeros_like(dg_ref)

        dg_ref[...] += dg

    row = pl.BlockSpec((tm, d), lambda i: (i, 0))
    vec = pl.BlockSpec((1, d), lambda i: (0, 0))
    return pl.pallas_call(
        body, out_shape=(jax.ShapeDtypeStruct((rows, d), F32), jax.ShapeDtypeStruct((1, d), F32)), grid=(rows // tm,),
        in_specs=[row, vec, row, row], out_specs=(row, vec), compiler_params=_params(("arbitrary",)), name=name)(x, g, dh, dres)


def final_loss(name, x, g, target, tm):
    rows, d = x.shape

    def body(x_ref, g_ref, t_ref, l_ref, dx_ref, dg_ref):
        def f(xv, gv):
            e = _rms(xv, gv) - t_ref[...]
            return 0.5 * jnp.sum(jnp.mean(e * e, axis=-1))

        loss, (dx, dg) = jax.value_and_grad(f, argnums=(0, 1))(x_ref[...], g_ref[...])
        dx_ref[...] = dx

        @pl.when(pl.program_id(0) == 0)
        def _():
            l_ref[...] = jnp.zeros_like(l_ref)
            dg_ref[...] = jnp.zeros_like(dg_ref)

        l_ref[...] += jnp.full(l_ref.shape, loss, F32)
        dg_ref[...] += dg

    row = pl.BlockSpec((tm, d), lambda i: (i, 0))
    vec = pl.BlockSpec((1, d), lambda i: (0, 0))
    lspec = pl.BlockSpec((1, 128), lambda i: (0, 0))
    return pl.pallas_call(
        body, out_shape=(jax.ShapeDtypeStruct((1, 128), F32), jax.ShapeDtypeStruct((rows, d), F32), jax.ShapeDtypeStruct((1, d), F32)),
        grid=(rows // tm,), in_specs=[row, vec, row], out_specs=(lspec, row, vec),
        compiler_params=_params(("arbitrary",)), name=name)(x, g, target)


def _rowwise(name, fn, row_ins, par_ins, n_row_out, row_out_dtypes, tm, with_grads=False):
    rows = row_ins[0][0].shape[0]
    n_prim = len(row_ins) - (n_row_out if with_grads else 0)
    n_par = len(par_ins)

    def body(*refs):
        ins = [r[...] for r in refs[:len(row_ins) + n_par]]
        outs = refs[len(row_ins) + n_par:]
        prim, cts, pars = ins[:n_prim], ins[n_prim:len(row_ins)], ins[len(row_ins):]
        if not with_grads:
            res = fn(*prim, *pars)
            for o_ref, r in zip(outs, res):
                o_ref[...] = r.astype(o_ref.dtype)
            return
        _, vjp = jax.vjp(fn, *prim, *[p.astype(F32) for p in pars])
        grads = vjp(tuple(cts))
        for o_ref, gr in zip(outs[:n_prim], grads[:n_prim]):
            o_ref[...] = gr.astype(o_ref.dtype)

        @pl.when(pl.program_id(0) == 0)
        def _():
            for o_ref in outs[n_prim:]:
                o_ref[...] = jnp.zeros_like(o_ref)

        for o_ref, gr in zip(outs[n_prim:], grads[n_prim:]):
            o_ref[...] += gr.astype(F32)

    in_specs = [pl.BlockSpec((tm, w), (lambda i, c=c: (i, c))) for (_, c, w) in row_ins]
    in_specs += [pl.BlockSpec(p.shape, (lambda i, n=p.ndim: (0,) * n)) for p in par_ins]
    args = [a for (a, _, _) in row_ins] + list(par_ins)
    if not with_grads:
        out_shape = tuple(jax.ShapeDtypeStruct((rows, w), dt) for (w, dt) in row_out_dtypes)
        out_specs = tuple(pl.BlockSpec((tm, w), lambda i: (i, 0)) for (w, _) in row_out_dtypes)
        sem = ("parallel",)
    else:
        out_shape = tuple(jax.ShapeDtypeStruct((rows, w), dt) for (w, dt) in row_out_dtypes)
        out_shape += tuple(jax.ShapeDtypeStruct(p.shape, F32) for p in par_ins)
        out_specs = tuple(pl.BlockSpec((tm, w), lambda i: (i, 0)) for (w, _) in row_out_dtypes)
        out_specs += tuple(pl.BlockSpec(p.shape, (lambda i, n=p.ndim: (0,) * n)) for p in par_ins)
        sem = ("arbitrary",)
    return pl.pallas_call(body, out_shape=out_shape, grid=(rows // tm,), in_specs=in_specs, out_specs=out_specs,
                          compiler_params=_params(sem), name=name)(*args)


def _glu(v, g):
    return (v * jax.nn.sigmoid(g),)


def _neg_expm1(z):
    return -jnp.tanh(0.5 * z) * (jnp.exp(z) + 1.0)


def _lru_gate(xc, w_r, w_i, b_r, b_i, lam):
    r = jax.nn.sigmoid(bdot(xc, w_r) + b_r)
    i = jax.nn.sigmoid(bdot(xc, w_i) + b_i)
    log_a = -LRU_C * r * jax.nn.softplus(-lam)
    a = jnp.exp(log_a)
    mult = jnp.sqrt(_neg_expm1(2.0 * log_a))
    return a, mult * (i * xc)


def _layernorm(x, g, b):
    mu = jnp.mean(x, axis=-1, keepdims=True)
    var = jnp.mean(jnp.square(x - mu), axis=-1, keepdims=True)
    return (x - mu) * lax.rsqrt(var + EPS) * g + b


def _mix_post(y_ssm, u, h1, hseq, lru_g, dgp, s5_d, w_glu, b_glu, ln_g, ln_b, w_pw, b_pw, pool_bd, pool_scale):
    y = y_ssm + s5_d * u
    gl = jax.nn.gelu(y, approximate=True)
    out_s5 = gl * jax.nn.sigmoid(bdot(gl, w_glu) + b_glu)
    out_cv = bdot(jax.nn.silu(_layernorm(h1, ln_g, ln_b)), w_pw) + b_pw
    out_lru = hseq * jax.nn.gelu(lru_g, approximate=True)
    out_pool = bdot(dgp, pool_bd) * pool_scale
    return (jnp.concatenate([out_s5, out_cv, out_lru, out_pool], axis=-1),)


def _ffn_act(gc, val):
    return (jax.nn.gelu(gc, approximate=True) * val,)


def ffn_act_fwd(name, gc, up, tm):
    _, rows, c = gc.shape

    def body(g_ref, v_ref, o_ref):
        o_ref[...] = _ffn_act(g_ref[...], v_ref[...])[0].astype(BF16)

    return pl.pallas_call(
        body, out_shape=jax.ShapeDtypeStruct((2, rows, c), BF16), grid=(2, rows // tm),
        in_specs=[pl.BlockSpec((None, tm, c), lambda h, i: (h, i, 0)), pl.BlockSpec((None, tm, c), lambda h, i: (h + 2, i, 0))],
        out_specs=pl.BlockSpec((None, tm, c), lambda h, i: (h, i, 0)),
        compiler_params=_params(("parallel", "parallel")), name=name)(gc, up)


def ffn_act_bwd(name, gc, up, dact, tm):
    _, rows, c = gc.shape

    def body(g_ref, v_ref, d_ref, dg_ref, dv_ref):
        _, vjp = jax.vjp(_ffn_act, g_ref[...], v_ref[...])
        dg, dv = vjp((d_ref[...],))
        dg_ref[...] = dg
        dv_ref[...] = dv.astype(BF16)

    blk = pl.BlockSpec((None, tm, c), lambda h, i: (h, i, 0))
    return pl.pallas_call(
        body, out_shape=(jax.ShapeDtypeStruct((2, rows, c), F32), jax.ShapeDtypeStruct((2, rows, c), BF16)), grid=(2, rows // tm),
        in_specs=[blk, pl.BlockSpec((None, tm, c), lambda h, i: (h + 2, i, 0)), blk], out_specs=(blk, blk),
        compiler_params=_params(("parallel", "parallel")), name=name)(gc, up, dact)


def _halo_rows(taps):
    return -(-(taps - 1) // SUBLANES) * SUBLANES


def dwconv_fwd(name, x, cblk, c, w, b, taps, tm, out_dtype=F32):
    nb = w.shape[0]
    rows = x.shape[1]
    halo = _halo_rows(taps)
    per = tm // halo

    def body(x_ref, h_ref, w_ref, b_ref, o_ref):
        i = pl.program_id(1)
        prev = jnp.where(i > 0, h_ref[...], 0.0)
        ext = jnp.concatenate([prev, x_ref[...]], axis=0)
        acc = jnp.broadcast_to(b_ref[...], (tm, c))
        for k in range(taps):
            off = halo - (taps - 1) + k
            acc = acc + w_ref[k:k + 1, :] * ext[off:off + tm]
        o_ref[...] = acc.astype(o_ref.dtype)

    return pl.pallas_call(
        body, out_shape=jax.ShapeDtypeStruct((nb, rows, c), out_dtype), grid=(nb, rows // tm),
        in_specs=[pl.BlockSpec((None, tm, c), lambda n, i: (n, i, cblk)),
                  pl.BlockSpec((None, halo, c), lambda n, i: (n, jnp.maximum(i * per - 1, 0), cblk)),
                  pl.BlockSpec((None, taps, c), lambda n, i: (n, 0, 0)),
                  pl.BlockSpec((None, 1, c), lambda n, i: (n, 0, 0))],
        out_specs=pl.BlockSpec((None, tm, c), lambda n, i: (n, i, 0)),
        compiler_params=_params(("parallel", "parallel")), name=name)(x, x, w, b)


def dwconv_bwd(name, dy, x, cblk, c, w, taps, tm, dx_dtype=F32):
    nb = w.shape[0]
    rows = x.shape[1]
    halo = _halo_rows(taps)
    per = tm // halo
    n_tiles = rows // tm
    last_halo = rows // halo - 1

    def body(dy_ref, dn_ref, x_ref, xp_ref, w_ref, dx_ref, dw_ref, db_ref):
        i = pl.program_id(1)
        dyv = dy_ref[...]
        nxt = jnp.where(i < n_tiles - 1, dn_ref[...], 0.0)
        dext = jnp.concatenate([dyv, nxt], axis=0)
        prev = jnp.where(i > 0, xp_ref[...], 0.0)
        xext = jnp.concatenate([prev, x_ref[...]], axis=0)
        acc = jnp.zeros((tm, c), F32)

        @pl.when(i == 0)
        def _():
            dw_ref[...] = jnp.zeros_like(dw_ref)
            db_ref[...] = jnp.zeros_like(db_ref)

        for k in range(taps):
            acc = acc + w_ref[k:k + 1, :] * dext[taps - 1 - k:taps - 1 - k + tm]
            off = halo - (taps - 1) + k
            dw_ref[k:k + 1, :] += jnp.sum(dyv * xext[off:off + tm], axis=0, keepdims=True)
        dx_ref[...] = acc.astype(dx_ref.dtype)
        db_ref[...] += jnp.sum(dyv, axis=0, keepdims=True)

    return pl.pallas_call(
        body, out_shape=(jax.ShapeDtypeStruct((nb, rows, c), dx_dtype), jax.ShapeDtypeStruct((nb, taps, c), F32),
                         jax.ShapeDtypeStruct((nb, 1, c), F32)),
        grid=(nb, n_tiles),
        in_specs=[pl.BlockSpec((None, tm, c), lambda n, i: (n, i, 0)),
                  pl.BlockSpec((None, halo, c), lambda n, i: (n, jnp.minimum((i + 1) * per, last_halo), 0)),
                  pl.BlockSpec((None, tm, c), lambda n, i: (n, i, cblk)),
                  pl.BlockSpec((None, halo, c), lambda n, i: (n, jnp.maximum(i * per - 1, 0), cblk)),
                  pl.BlockSpec((None, taps, c), lambda n, i: (n, 0, 0))],
        out_specs=(pl.BlockSpec((None, tm, c), lambda n, i: (n, i, 0)), pl.BlockSpec((None, taps, c), lambda n, i: (n, 0, 0)),
                   pl.BlockSpec((None, 1, c), lambda n, i: (n, 0, 0))),
        compiler_params=_params(("parallel", "arbitrary")), name=name)(dy, dy, x, x, w)


POOL_HALO = 16


def pool_fwd(name, proj, cblk, tm):
    rows = proj.shape[0]
    c = 128 * len(POOL_WINDOWS)
    per = tm // POOL_HALO

    def body(x_ref, h_ref, o_ref):
        i = pl.program_id(0)
        xv = x_ref[...]
        ext = jnp.concatenate([jnp.where(i > 0, h_ref[...], 0.0), xv], axis=0)
        t1 = (lax.broadcasted_iota(jnp.int32, (tm, 128), 0) + i * tm + 1).astype(F32)
        outs = []
        for gi, win in enumerate(POOL_WINDOWS):
            seg = ext[:, gi * 128:(gi + 1) * 128]
            s = seg[POOL_HALO:POOL_HALO + tm]
            for j in range(1, win):
                s = s + seg[POOL_HALO - j:POOL_HALO - j + tm]
            outs.append(s / jnp.minimum(t1, float(win)) - xv[:, gi * 128:(gi + 1) * 128])
        o_ref[...] = jnp.concatenate(outs, axis=-1)

    return pl.pallas_call(
        body, out_shape=jax.ShapeDtypeStruct((rows, c), F32), grid=(rows // tm,),
        in_specs=[pl.BlockSpec((tm, c), lambda i: (i, cblk)),
                  pl.BlockSpec((POOL_HALO, c), lambda i: (jnp.maximum(i * per - 1, 0), cblk))],
        out_specs=pl.BlockSpec((tm, c), lambda i: (i, 0)), compiler_params=_params(("parallel",)), name=name)(proj, proj)


def pool_bwd(name, dd, tm):
    rows, c = dd.shape
    per = tm // POOL_HALO
    n_tiles = rows // tm
    last_halo = rows // POOL_HALO - 1

    def body(d_ref, n_ref, o_ref):
        i = pl.program_id(0)
        dv = d_ref[...]
        nxt = jnp.where(i < n_tiles - 1, n_ref[...], 0.0)
        t1 = (lax.broadcasted_iota(jnp.int32, (tm, 128), 0) + i * tm + 1).astype(F32)
        t1n = (lax.broadcasted_iota(jnp.int32, (POOL_HALO, 128), 0) + (i + 1) * tm + 1).astype(F32)
        outs = []
        for gi, win in enumerate(POOL_WINDOWS):
            sl = slice(gi * 128, (gi + 1) * 128)
            q = jnp.concatenate([dv[:, sl] / jnp.minimum(t1, float(win)), nxt[:, sl] / jnp.minimum(t1n, float(win))], axis=0)
            s = q[0:tm]
            for j in range(1, win):
                s = s + q[j:j + tm]
            outs.append(s - dv[:, sl])
        o_ref[...] = jnp.concatenate(outs, axis=-1)

    return pl.pallas_call(
        body, out_shape=jax.ShapeDtypeStruct((rows, c), F32), grid=(n_tiles,),
        in_specs=[pl.BlockSpec((tm, c), lambda i: (i, 0)),
                  pl.BlockSpec((POOL_HALO, c), lambda i: (jnp.minimum((i + 1) * per, last_halo), 0))],
        out_specs=pl.BlockSpec((tm, c), lambda i: (i, 0)), compiler_params=_params(("parallel",)), name=name)(dd, dd)


def _shift_down(v, s, fill):
    r = lax.broadcasted_iota(jnp.int32, v.shape, 0)
    return jnp.where(r >= s, pltpu.roll(v, s, 0), fill)


def _shift_up(v, s, fill):
    n = v.shape[0]
    r = lax.broadcasted_iota(jnp.int32, v.shape, 0)
    return jnp.where(r < n - s, pltpu.roll(v, n - s, 0), fill)


def _cscan_chunk(vr, vi, powers, reverse):
    for k, (qr, qi) in enumerate(powers):
        s = 1 << k
        if reverse:
            sr, si = _shift_up(vr, s, 0.0), _shift_up(vi, s, 0.0)
            vr, vi = vr + qr * sr + qi * si, vi + qr * si - qi * sr
        else:
            sr, si = _shift_down(vr, s, 0.0), _shift_down(vi, s, 0.0)
            vr, vi = vr + qr * sr - qi * si, vi + qr * si + qi * sr
    return vr, vi


def _powers(pr, pi, n):
    out = [(pr, pi)]
    for _ in range(n - 1):
        pr, pi = pr * pr - pi * pi, 2.0 * pr * pi
        out.append((pr, pi))
    return out


def s5_scan_fwd(name, bu, a):
    _, rows, n = bu.shape
    t = min(SCAN_CHUNK, rows)
    steps = t.bit_length() - 1

    def body(bu_ref, a_ref, z_ref):
        pr, pi = a_ref[0], a_ref[1]
        powers = _powers(pr, pi, steps)
        r = lax.broadcasted_iota(jnp.int32, (t, 128), 0)
        tr, ti = _cscan_chunk(jnp.where(r == 0, pr, 0.0), jnp.where(r == 0, pi, 0.0), powers, False)

        def chunk(ci, carry):
            base = pl.multiple_of(ci * t, t)
            vr, vi = _cscan_chunk(bu_ref[0, pl.ds(base, t), :], bu_ref[1, pl.ds(base, t), :], powers, False)
            cr, cim = carry
            zr = vr + tr * cr - ti * cim
            zi = vi + tr * cim + ti * cr
            z_ref[0, pl.ds(base, t), :] = zr
            z_ref[1, pl.ds(base, t), :] = zi
            return zr[t - 1:t, :], zi[t - 1:t, :]

        zero = jnp.zeros((1, 128), F32)
        lax.fori_loop(0, rows // t, chunk, (zero, zero))

    return pl.pallas_call(
        body, out_shape=jax.ShapeDtypeStruct((2, rows, n), F32), grid=(n // 128,),
        in_specs=[pl.BlockSpec((2, rows, 128), lambda j: (0, 0, j)), pl.BlockSpec((2, 1, 128), lambda j: (0, 0, j))],
        out_specs=pl.BlockSpec((2, rows, 128), lambda j: (0, 0, j)), compiler_params=_params(("parallel",)), name=name)(bu, a)


def s5_scan_bwd(name, dz, z, a):
    _, rows, n = dz.shape
    t = min(SCAN_CHUNK, rows)
    steps = t.bit_length() - 1
    n_chunks = rows // t

    def body(dz_ref, z_ref, a_ref, lam_ref, da_ref):
        pr, pi = a_ref[0], a_ref[1]
        powers = _powers(pr, pi, steps)
        r = lax.broadcasted_iota(jnp.int32, (t, 128), 0)
        tr, ti = _cscan_chunk(jnp.where(r == t - 1, pr, 0.0), jnp.where(r == t - 1, -pi, 0.0), powers, True)

        def chunk(k, carry):
            ci = n_chunks - 1 - k
            base = pl.multiple_of(ci * t, t)
            vr, vi = _cscan_chunk(dz_ref[0, pl.ds(base, t), :], dz_ref[1, pl.ds(base, t), :], powers, True)
            cr, cim, dar, dai = carry
            lr = vr + tr * cr - ti * cim
            li = vi + tr * cim + ti * cr
            lam_ref[0, pl.ds(base, t), :] = lr
            lam_ref[1, pl.ds(base, t), :] = li
            pbase = pl.multiple_of(jnp.maximum(base - SUBLANES, 0), SUBLANES)
            keep = (ci > 0).astype(F32)
            pzr = z_ref[0, pl.ds(pbase, SUBLANES), :][SUBLANES - 1:SUBLANES, :] * keep
            pzi = z_ref[1, pl.ds(pbase, SUBLANES), :][SUBLANES - 1:SUBLANES, :] * keep
            zpr = _shift_down(z_ref[0, pl.ds(base, t), :], 1, pzr)
            zpi = _shift_down(z_ref[1, pl.ds(base, t), :], 1, pzi)
            dar = dar + jnp.sum(lr * zpr + li * zpi, axis=0, keepdims=True)
            dai = dai + jnp.sum(li * zpr - lr * zpi, axis=0, keepdims=True)
            return lr[0:1, :], li[0:1, :], dar, dai

        zero = jnp.zeros((1, 128), F32)
        _, _, dar, dai = lax.fori_loop(0, n_chunks, chunk, (zero, zero, zero, zero))
        da_ref[0] = dar
        da_ref[1] = dai

    seq = pl.BlockSpec((2, rows, 128), lambda j: (0, 0, j))
    vec = pl.BlockSpec((2, 1, 128), lambda j: (0, 0, j))
    return pl.pallas_call(
        body, out_shape=(jax.ShapeDtypeStruct((2, rows, n), F32), jax.ShapeDtypeStruct((2, 1, n), F32)), grid=(n // 128,),
        in_specs=[seq, seq, vec], out_specs=(seq, vec), compiler_params=_params(("parallel",)), name=name)(dz, z, a)


def _rscan_chunk(a, b, steps, reverse):
    shift = _shift_up if reverse else _shift_down
    for k in range(steps):
        s = 1 << k
        b = b + a * shift(b, s, 0.0)
        a = a * shift(a, s, 1.0)
    return a, b


def lru_scan_fwd(name, a, b):
    rows, n = a.shape
    t = min(SCAN_CHUNK, rows)
    steps = t.bit_length() - 1

    def body(a_ref, b_ref, h_ref):
        def chunk(ci, carry):
            base = pl.multiple_of(ci * t, t)
            pa, hb = _rscan_chunk(a_ref[pl.ds(base, t), :], b_ref[pl.ds(base, t), :], steps, False)
            h = hb + pa * carry
            h_ref[pl.ds(base, t), :] = h
            return h[t - 1:t, :]

        lax.fori_loop(0, rows // t, chunk, jnp.zeros((1, 128), F32))

    seq = pl.BlockSpec((rows, 128), lambda j: (0, j))
    return pl.pallas_call(body, out_shape=jax.ShapeDtypeStruct((rows, n), F32), grid=(n // 128,), in_specs=[seq, seq],
                          out_specs=seq, compiler_params=_params(("parallel",)), name=name)(a, b)


def lru_scan_bwd(name, dh, a, h):
    rows, n = a.shape
    t = min(SCAN_CHUNK, rows)
    steps = t.bit_length() - 1
    n_chunks = rows // t

    def body(dh_ref, a_ref, h_ref, da_ref, db_ref):
        def chunk(k, carry):
            ci = n_chunks - 1 - k
            base = pl.multiple_of(ci * t, t)
            nbase = pl.multiple_of(jnp.minimum(base + t, rows - SUBLANES), SUBLANES)
            a_next = a_ref[pl.ds(nbase, SUBLANES), :][0:1, :]
            an = _shift_up(a_ref[pl.ds(base, t), :], 1, a_next)
            pa, mb = _rscan_chunk(an, dh_ref[pl.ds(base, t), :], steps, True)
            mu = mb + pa * carry
            pbase = pl.multiple_of(jnp.maximum(base - SUBLANES, 0), SUBLANES)
            hp_row = h_ref[pl.ds(pbase, SUBLANES), :][SUBLANES - 1:SUBLANES, :] * (ci > 0).astype(F32)
            hp = _shift_down(h_ref[pl.ds(base, t), :], 1, hp_row)
            da_ref[pl.ds(base, t), :] = mu * hp
            db_ref[pl.ds(base, t), :] = mu
            return mu[0:1, :]

        lax.fori_loop(0, n_chunks, chunk, jnp.zeros((1, 128), F32))

    seq = pl.BlockSpec((rows, 128), lambda j: (0, j))
    return pl.pallas_call(
        body, out_shape=(jax.ShapeDtypeStruct((rows, n), F32), jax.ShapeDtypeStruct((rows, n), F32)), grid=(n // 128,),
        in_specs=[seq, seq, seq], out_specs=(seq, seq), compiler_params=_params(("parallel",)), name=name)(dh, a, h)


def _s5_param(lr, li, ls, bre, bim):
    st = jnp.exp(ls)
    er = jnp.exp(lr * st)
    th = li * st
    ar, ai = er * jnp.cos(th), er * jnp.sin(th)
    nr, ni = ar - 1.0, ai
    den = lr * lr + li * li
    cr, ci = (nr * lr + ni * li) / den, (ni * lr - nr * li) / den
    return ar, ai, cr * bre - ci * bim, cr * bim + ci * bre


def s5_param_fwd(name, lr, li, ls, bre, bim):
    gh, n = bre.shape

    def body(lr_ref, li_ref, ls_ref, bre_ref, bim_ref, a_ref, bb_ref):
        ar, ai, br, bi = _s5_param(lr_ref[...], li_ref[...], ls_ref[...], bre_ref[...], bim_ref[...])
        a_ref[0] = ar
        a_ref[1] = ai
        bb_ref[0] = br.astype(BF16)
        bb_ref[1] = bi.astype(BF16)

    return pl.pallas_call(body, out_shape=(jax.ShapeDtypeStruct((2, 1, n), F32), jax.ShapeDtypeStruct((2, gh, n), BF16)),
                          compiler_params=_params(), name=name)(lr, li, ls, bre, bim)


def s5_param_bwd(name, lr, li, ls, bre, bim, da, dbb, gsum):
    gh, n = bre.shape

    def body(lr_ref, li_ref, ls_ref, bre_ref, bim_ref, da_ref, dbb_ref, gs_ref, dlr_ref, dli_ref, dls_ref, dbre_ref, dbim_ref):
        _, vjp = jax.vjp(_s5_param, lr_ref[...], li_ref[...], ls_ref[...], bre_ref[...], bim_ref[...])
        dlr, dli, dls, dbre, dbim = vjp((da_ref[0], da_ref[1], dbb_ref[0], dbb_ref[1]))
        dlr_ref[...] = dlr
        dli_ref[...] = dli
        dls_ref[...] = jnp.dot(jnp.broadcast_to(dls, (SUBLANES, n)), gs_ref[...], preferred_element_type=F32,
                               precision=lax.Precision.HIGHEST)
        dbre_ref[...] = dbre
        dbim_ref[...] = dbim

    vec = jax.ShapeDtypeStruct((1, n), F32)
    mat = jax.ShapeDtypeStruct((gh, n), F32)
    return pl.pallas_call(body, out_shape=(vec, vec, jax.ShapeDtypeStruct((SUBLANES, 128), F32), mat, mat),
                          compiler_params=_params(), name=name)(lr, li, ls, bre, bim, da, dbb, gsum)


def sum_lead(name, x, tr):
    n, rows, cols = x.shape

    def body(x_ref, o_ref):
        acc = x_ref[0]
        for j in range(1, n):
            acc = acc + x_ref[j]
        o_ref[...] = acc

    return pl.pallas_call(
        body, out_shape=jax.ShapeDtypeStruct((rows, cols), x.dtype), grid=(rows // tr,),
        in_specs=[pl.BlockSpec((n, tr, cols), lambda i: (0, i, 0))], out_specs=pl.BlockSpec((tr, cols), lambda i: (i, 0)),
        compiler_params=_params(("parallel",)), name=name)(x)


def _adamw(w, g, m, v):
    m = ADAM_B1 * m + (1.0 - ADAM_B1) * g
    v = ADAM_B2 * v + (1.0 - ADAM_B2) * jnp.square(g)
    m_hat = m / (1.0 - ADAM_B1 ** ADAM_STEP)
    v_hat = v / (1.0 - ADAM_B2 ** ADAM_STEP)
    delta = -ADAM_LR * (m_hat / (jnp.sqrt(v_hat) + ADAM_EPS) + ADAM_WD * w)
    return delta, m, v


def adamw_sharded(name, w, m, v, g0, g1, split_cols, tile):
    _, r, c = w.shape
    if split_cols:
        nt = c // tile
        per = (c // 2) // tile
        wspec = pl.BlockSpec((None, r, tile), lambda l, t: (l, 0, t))
        gspec = pl.BlockSpec((None, r, tile), lambda l, t: (t // per, 0, t % per))
    else:
        nt = r // tile
        per = (r // 2) // tile
        wspec = pl.BlockSpec((None, tile, c), lambda l, t: (l, t, 0))
        gspec = pl.BlockSpec((None, tile, c), lambda l, t: (t // per, t % per, 0))

    def body(w_ref, m_ref, v_ref, g0_ref, g1_ref, g_ref, d_ref, nm_ref, nv_ref):
        g = jnp.where(pl.program_id(0) == 0, g0_ref[...], g1_ref[...])
        d, nm, nv = _adamw(w_ref[...], g, m_ref[...], v_ref[...])
        g_ref[...] = g
        d_ref[...] = d
        nm_ref[...] = nm
        nv_ref[...] = nv

    sds = jax.ShapeDtypeStruct(w.shape, F32)
    return pl.pallas_call(body, out_shape=(sds,) * 4, grid=(2, nt), in_specs=[wspec, wspec, wspec, gspec, gspec],
                          out_specs=(wspec,) * 4, compiler_params=_params(("parallel", "parallel")), name=name)(w, m, v, g0, g1)


def adamw_flat(name, w, g, m, v, tr):
    rows, cols = w.shape

    def body(w_ref, g_ref, m_ref, v_ref, d_ref, nm_ref, nv_ref):
        d, nm, nv = _adamw(w_ref[...], g_ref[...], m_ref[...], v_ref[...])
        d_ref[...] = d
        nm_ref[...] = nm
        nv_ref[...] = nv

    blk = pl.BlockSpec((tr, cols), lambda i: (i, 0))
    sds = jax.ShapeDtypeStruct((rows, cols), F32)
    return pl.pallas_call(body, out_shape=(sds,) * 3, grid=(rows // tr,), in_specs=[blk] * 4, out_specs=(blk,) * 3,
                          compiler_params=_params(("parallel",)), name=name)(w, g, m, v)


def _flips(axes):
    out = []
    for fx in ((0, 1) if "x" in axes else (0,)):
        for fy in ((0, 1) if "y" in axes else (0,)):
            for fc in ((0, 1) if "c" in axes else (0,)):
                if fx or fy or fc:
                    out.append((fx, fy, fc))
    return out


def _slot(pos, axes):
    s = 0
    for name, p in zip(("x", "y", "c"), pos):
        if name in axes:
            s = 2 * s + p
    return s


def _exchange(name, arrs, axes, scatter):
    flips = _flips(axes)
    n = len(flips) + 1
    na = len(arrs)

    def body(*refs):
        ins, outs = refs[:na], refs[na:2 * na]
        send_sems, recv_sems, local_sems = refs[2 * na:]
        me = (lax.axis_index("x"), lax.axis_index("y"), lax.axis_index("c"))
        my = _slot(me, axes)
        peers = [tuple((1 - p) if f else p for p, f in zip(me, fl)) for fl in flips]

        def src(a, dest_slot):
            return ins[a].at[dest_slot] if scatter else ins[a]

        local = [pltpu.make_async_copy(src(a, my), outs[a].at[my], local_sems.at[a]) for a in range(na)]
        for cp in local:
            cp.start()

        def remote(a, j, landing_slot, dest_slot):
            return pltpu.make_async_remote_copy(
                src_ref=src(a, dest_slot), dst_ref=outs[a].at[landing_slot], send_sem=send_sems.at[a * len(flips) + j],
                recv_sem=recv_sems.at[a * len(flips) + j], device_id=peers[j], device_id_type=pl.DeviceIdType.MESH)

        sends = [remote(a, j, my, _slot(peers[j], axes)) for a in range(na) for j in range(len(flips))]
        for cp in sends:
            cp.start()
        for a in range(na):
            for j in range(len(flips)):
                remote(a, j, _slot(peers[j], axes), _slot(peers[j], axes)).wait_recv()
        for cp in sends:
            cp.wait_send()
        for cp in local:
            cp.wait()

    if scatter:
        out_shape = tuple(jax.ShapeDtypeStruct(a.shape, a.dtype) for a in arrs)
    else:
        out_shape = tuple(jax.ShapeDtypeStruct((n,) + a.shape, a.dtype) for a in arrs)
    anyspec = pl.BlockSpec(memory_space=pl.ANY)
    return pl.pallas_call(
        body, out_shape=out_shape, in_specs=[anyspec] * na, out_specs=(anyspec,) * na,
        scratch_shapes=[pltpu.SemaphoreType.DMA((na * len(flips),)), pltpu.SemaphoreType.DMA((na * len(flips),)),
                        pltpu.SemaphoreType.DMA((na,))],
        name=name)(*arrs)


def all_gather(name, arrs, axes):
    return _exchange(name, arrs, axes, False)


def all_to_all(name, arrs, axes):
    return _exchange(name, arrs, axes, True)


_HBM = pl.BlockSpec(memory_space=pltpu.HBM)
_SEM = pl.BlockSpec(memory_space=pltpu.SEMAPHORE)
_EFFECT = pltpu.SideEffectType.DATAFLOW_SIDE_EFFECTING


def place_own(name, arrs, axes, scatter):
    n = len(_flips(axes)) + 1
    na = len(arrs)

    def body(*refs):
        ins, outs, sems = refs[:na], refs[na:2 * na], refs[2 * na]
        my = _slot((lax.axis_index("x"), lax.axis_index("y"), lax.axis_index("c")), axes)
        copies = [pltpu.make_async_copy(ins[a].at[my] if scatter else ins[a], outs[a].at[my], sems.at[a]) for a in range(na)]
        for cp in copies:
            cp.start()
        for cp in copies:
            cp.wait()

    out_shape = tuple(jax.ShapeDtypeStruct(a.shape if scatter else (n,) + a.shape, a.dtype) for a in arrs)
    anyspec = pl.BlockSpec(memory_space=pl.ANY)
    return pl.pallas_call(body, out_shape=out_shape, in_specs=[anyspec] * na, out_specs=(anyspec,) * na,
                          scratch_shapes=[pltpu.SemaphoreType.DMA((na,))], name=name)(*arrs)


def _peers(axes):
    me = (lax.axis_index("x"), lax.axis_index("y"), lax.axis_index("c"))
    return me, [tuple((1 - p) if f else p for p, f in zip(me, fl)) for fl in _flips(axes)]


def exchange_start(name, groups, axes, scatter):
    flat = [p for grp in groups for p in grp]
    na, ng, npeer = len(flat), len(groups), len(_flips(axes))

    def body(*refs):
        srcs, lands = refs[:na], refs[na:2 * na]
        sems, token = refs[2 * na:2 * na + 2 * ng], refs[-1]
        me, peers = _peers(axes)
        my = _slot(me, axes)
        ai = 0
        for g, grp in enumerate(groups):
            for k in range(len(grp)):
                for j, peer in enumerate(peers):
                    src = srcs[ai].at[_slot(peer, axes)] if scatter else srcs[ai]
                    pltpu.make_async_remote_copy(
                        src_ref=src, dst_ref=lands[ai].at[my], send_sem=sems[2 * g].at[k * npeer + j],
                        recv_sem=sems[2 * g + 1].at[k * npeer + j], device_id=peer, device_id_type=pl.DeviceIdType.MESH).start()
                ai += 1
        token[...] = jnp.zeros_like(token)

    out_shape, out_specs = [], []
    for grp in groups:
        out_shape += [pltpu.SemaphoreType.DMA((npeer * len(grp),))] * 2
        out_specs += [_SEM, _SEM]
    for idx in (0, 1):
        out_shape += [pltpu.HBM(p[idx].shape, p[idx].dtype) for p in flat]
        out_specs += [_HBM] * na
    out_shape.append(jax.ShapeDtypeStruct((SUBLANES, 128), F32))
    out_specs.append(pl.BlockSpec(memory_space=pltpu.VMEM))
    args = [pltpu.with_memory_space_constraint(p[idx], pltpu.HBM) for idx in (0, 1) for p in flat]
    res = pl.pallas_call(body, out_shape=tuple(out_shape), in_specs=[_HBM] * (2 * na), out_specs=tuple(out_specs),
                         input_output_aliases={i: 2 * ng + i for i in range(2 * na)},
                         compiler_params=pltpu.CompilerParams(has_side_effects=_EFFECT), name=name)(*args)
    sems = [(res[2 * g], res[2 * g + 1]) for g in range(ng)]
    srcs_t, lands_t = res[2 * ng:2 * ng + na], res[2 * ng + na:2 * ng + 2 * na]
    out, ai = [], 0
    for grp in groups:
        out.append((sems[len(out)], list(srcs_t[ai:ai + len(grp)]), list(lands_t[ai:ai + len(grp)])))
        ai += len(grp)
    return out, res[-1]


def exchange_wait(name, group, after, axes, scatter):
    (send_sems, recv_sems), srcs, lands = group
    n = len(srcs)
    npeer = len(_flips(axes))

    def body(*refs):
        s_refs, l_refs = refs[:n], refs[n:2 * n]
        ssem, rsem = refs[2 * n], refs[2 * n + 1]
        _, peers = _peers(axes)
        for k in range(n):
            for j, peer in enumerate(peers):
                slot = _slot(peer, axes)
                copy = pltpu.make_async_remote_copy(
                    src_ref=s_refs[k].at[slot] if scatter else s_refs[k], dst_ref=l_refs[k].at[slot], send_sem=ssem.at[k * npeer + j],
                    recv_sem=rsem.at[k * npeer + j], device_id=peer, device_id_type=pl.DeviceIdType.MESH)
                copy.wait_send()
                copy.wait_recv()

    out_shape = tuple(pltpu.HBM(a.shape, a.dtype) for a in list(srcs) + list(lands))
    res = pl.pallas_call(body, out_shape=out_shape, in_specs=[_HBM] * (2 * n) + [_SEM, _SEM, pl.BlockSpec(memory_space=pl.ANY)],
                         out_specs=(_HBM,) * (2 * n), input_output_aliases={i: i for i in range(2 * n)},
                         compiler_params=pltpu.CompilerParams(has_side_effects=_EFFECT), name=name)(*srcs, *lands, send_sems, recv_sems, after)
    return list(res[n:])


def _pair_exchange(name, ins, in_specs, n_steps, tile, fn_send, fn_out, out_shape, out_spec):
    n_in = len(ins)

    def body(*refs):
        in_refs, o_ref = refs[:n_in], refs[n_in]
        send_buf, recv_buf, send_sems, recv_sems, credit = refs[n_in + 1:]
        i = pl.program_id(0)
        slot = lax.rem(i, 2)
        c = lax.axis_index("c")
        sibling = (lax.axis_index("x"), lax.axis_index("y"), 1 - c)
        vals = [r[...] for r in in_refs]
        send_buf[slot] = fn_send(*vals, c)

        @pl.when(i >= 2)
        def _():
            pl.semaphore_wait(credit, 1)

        copy = pltpu.make_async_remote_copy(
            src_ref=send_buf.at[slot], dst_ref=recv_buf.at[slot], send_sem=send_sems.at[slot], recv_sem=recv_sems.at[slot],
            device_id=sibling, device_id_type=pl.DeviceIdType.MESH)
        copy.start()
        copy.wait_recv()
        o_ref[...] = fn_out(*vals, recv_buf[slot], c).astype(o_ref.dtype)
        copy.wait_send()

        @pl.when(i < n_steps - 2)
        def _():
            pl.semaphore_signal(credit, inc=1, device_id=sibling, device_id_type=pl.DeviceIdType.MESH)

    return pl.pallas_call(
        body, out_shape=out_shape, grid=(n_steps,), in_specs=in_specs, out_specs=out_spec,
        scratch_shapes=[pltpu.VMEM((2,) + tile, F32), pltpu.VMEM((2,) + tile, F32), pltpu.SemaphoreType.DMA((2,)),
                        pltpu.SemaphoreType.DMA((2,)), pltpu.SemaphoreType.REGULAR],
        compiler_params=_params(("arbitrary",)), name=name)(*ins)


def _tile_rows(rows, cols):
    return _row_tile(rows, max(SUBLANES, (3 << 19) // (4 * cols)))


def reduce_cores(name, g):
    _, m, cols = g.shape
    tr = _tile_rows(m, cols)

    def fn_send(g0, g1, c):
        return jnp.where(c == 0, g1, g0)

    def fn_out(g0, g1, got, c):
        return jnp.where(c == 0, g0, g1) + got

    return _pair_exchange(
        name, [g, g], [pl.BlockSpec((None, tr, cols), lambda i: (0, i, 0)), pl.BlockSpec((None, tr, cols), lambda i: (1, i, 0))],
        m // tr, (tr, cols), fn_send, fn_out, jax.ShapeDtypeStruct((m, cols), BF16), pl.BlockSpec((tr, cols), lambda i: (i, 0)))


def sum_and_share(name, parts):
    n, r, cols = parts.shape
    tr = _tile_rows(r, cols)

    def total(p):
        acc = p[0].astype(F32)
        for j in range(1, n):
            acc = acc + p[j].astype(F32)
        return acc

    def fn_send(p, c):
        return total(p)

    def fn_out(p, got, c):
        mine = total(p)
        return jnp.stack([jnp.where(c == 0, mine, got), jnp.where(c == 0, got, mine)])

    return _pair_exchange(
        name, [parts], [pl.BlockSpec((n, tr, cols), lambda i: (0, i, 0))], r // tr, (tr, cols), fn_send, fn_out,
        jax.ShapeDtypeStruct((2, r, cols), F32), pl.BlockSpec((2, tr, cols), lambda i: (0, i, 0)))


def _block_diag(blocks):
    g, r, c = blocks.shape
    eye = jnp.eye(g, dtype=blocks.dtype)
    return (blocks[:, :, None, :] * eye[:, None, :, None]).reshape(g * r, g * c)


def _diag_blocks(mat, g):
    r, c = mat.shape[0] // g, mat.shape[1] // g
    eye = jnp.eye(g, dtype=mat.dtype)
    return (mat.reshape(g, r, g, c) * eye[:, None, :, None]).sum(axis=2)


def _halves(gfull, shards):
    rows, cols = gfull.shape
    return gfull.reshape(shards, 2, rows // shards // 2, cols).transpose(1, 0, 2, 3)


def _step(inp):
    x = inp['x'][0]
    target = inp['loss_target'][0]
    rows, d = x.shape
    depth = inp['w_in'].shape[0]
    mix_w = d // 4
    n_state = S5_GROUPS * S5_STATE
    ffn_half = inp['ffn_w_up'].shape[2]
    tm = min(512, rows)
    tc = min(256, rows)
    tl = min(512, rows)
    xy = ("x", "y")

    send = {(nme, l): inp[nme][l].astype(BF16) for l in range(depth) for nme in BIG}
    send.update({(nme, None): inp[nme] for nme in SMALL_SHARDED})
    group_keys = []
    for l in range(depth):
        group_keys += [[('w_in', l)] + ([(nme, None) for nme in SMALL_SHARDED] if l == 0 else []),
                       [('w_out', l), ('s5_w_glu', l), ('cv_w_pw', l)], [('ffn_w_up', l)], [('ffn_w_down', l)]]
    order = [key for grp in group_keys for key in grp]
    lands = dict(zip(order, place_own("gather_place", [send[key] for key in order], xy, False)))
    gather_groups, gather_token = exchange_start("gather_start", [[(send[key], lands[key]) for key in grp] for grp in group_keys], xy, False)

    def gathered(gi, after):
        return dict(zip(group_keys[gi], exchange_wait(f"gather_wait{gi}", gather_groups[gi], after, xy, False)))

    def full_small(g):
        return g.transpose(1, 2, 0, 3).reshape(g.shape[1], g.shape[2], 4 * g.shape[3])

    gsum = jnp.repeat(jnp.eye(128, dtype=F32)[:S5_GROUPS], S5_STATE, axis=0)

    saved = []
    grads = {nme: [None] * depth for nme in WEIGHTS}
    xcur = x
    for l in range(depth):
        vec = lambda a: a[l].reshape(1, -1)
        gain = vec(inp['norm_mix_g']) + (gather_token[0, 0] if l == 0 else 0.0)
        h = rms_fwd(f"rms_mix{l}", xcur, gain, tm)
        got = gathered(4 * l, h)
        w_in = got[('w_in', l)]
        if l == 0:
            cv_w_dw, lru_w_conv, ffn_w_dw = (full_small(got[(nme, None)]) for nme in ('cv_w_dw', 'lru_w_conv', 'ffn_w_dw'))
        ncol = w_in.shape[2]

        lam_re, lam_im = vec(inp['s5_lam_re']), vec(inp['s5_lam_im'])
        log_step = jnp.broadcast_to(inp['s5_log_step'][l][:, None], (S5_GROUPS, S5_STATE)).reshape(1, n_state)
        b_re = _block_diag(inp['s5_b_re'][l].transpose(0, 2, 1))
        b_im = _block_diag(inp['s5_b_im'][l].transpose(0, 2, 1))
        c_cat = jnp.stack([_block_diag(inp['s5_c_re'][l].transpose(0, 2, 1)),
                           -_block_diag(inp['s5_c_im'][l].transpose(0, 2, 1))]).astype(BF16)
        a_bar, b_bar = s5_param_fwd(f"s5_param_fwd{l}", lam_re, lam_im, log_step, b_re, b_im)
        w_r = _block_diag(inp['lru_w_r'][l]).astype(BF16)
        w_i = _block_diag(inp['lru_w_i'][l]).astype(BF16)
        pool_bd = _block_diag(inp['pool_w'][l]).astype(BF16)
        gate_pars = [w_r, w_i, vec(inp['lru_b_r']), vec(inp['lru_b_i']), vec(inp['lru_lam'])]

        proj = _mm(f"proj{l}", h, w_in, jax.ShapeDtypeStruct((rows, 4 * ncol), F32), (rows // tm, 4),
                   pl.BlockSpec((tm, d), lambda i, j: (i, 0)), pl.BlockSpec((None, d, ncol), lambda i, j: (j, 0, 0)),
                   pl.BlockSpec((tm, ncol), lambda i, j: (i, j)), NN)
        proj3 = proj.reshape(1, rows, 4 * ncol)
        nh = n_state // 2
        bu = _mm(f"s5_bu{l}", proj, b_bar, jax.ShapeDtypeStruct((2, rows, n_state), F32), (rows // tm, 2, 2),
                 pl.BlockSpec((tm, mix_w), lambda i, c, n: (i, 0)), pl.BlockSpec((None, mix_w, nh), lambda i, c, n: (c, 0, n)),
                 pl.BlockSpec((None, tm, nh), lambda i, c, n: (c, i, n)), NN)
        z = s5_scan_fwd(f"s5_scan{l}", bu, a_bar)
        y_ssm = _mm(f"s5_read{l}", z, c_cat, jax.ShapeDtypeStruct((rows, mix_w), F32), (rows // tm, 4),
                    pl.BlockSpec((None, tm, nh), lambda i, k: (k // 2, i, k % 2)),
                    pl.BlockSpec((None, nh, mix_w), lambda i, k: (k // 2, k % 2, 0)),
                    pl.BlockSpec((tm, mix_w), lambda i, k: (i, 0)), NN, k_axis=1)
        (h0,) = _rowwise(f"cv_glu{l}", _glu, [(proj, 1, mix_w), (proj, 2, mix_w)], [], 1, [(mix_w, F32)], tm)
        h1 = dwconv_fwd(f"cv_conv{l}", h0.reshape(1, rows, mix_w), 0, mix_w, cv_w_dw[l][None], vec(inp['cv_b_dw'])[None],
                        CV_TAPS, tc)[0]
        xc = dwconv_fwd(f"lru_conv{l}", proj3, 3, mix_w, lru_w_conv[l][None], vec(inp['lru_b_conv'])[None], LRU_TAPS, tc)[0]
        a_t, b_t = _rowwise(f"lru_gate{l}", _lru_gate, [(xc, 0, mix_w)], gate_pars, 2, [(mix_w, F32), (mix_w, F32)], tm)
        hseq = lru_scan_fwd(f"lru_scan{l}", a_t, b_t)
        dgp = pool_fwd(f"pool{l}", proj, 5, tc)
        got = gathered(4 * l + 1, proj)
        w_out = got[('w_out', l)].reshape(d, d)
        w_glu, w_pw = got[('s5_w_glu', l)].reshape(mix_w, mix_w), got[('cv_w_pw', l)].reshape(mix_w, mix_w)
        post_pars = [vec(inp['s5_d']), w_glu, vec(inp['s5_b_glu']), vec(inp['cv_ln_g']), vec(inp['cv_ln_b']), w_pw,
                     vec(inp['cv_b_pw']), pool_bd, vec(inp['pool_scale'])]
        post_rows = [(y_ssm, 0, mix_w), (proj, 0, mix_w), (h1, 0, mix_w), (hseq, 0, mix_w), (proj, 4, mix_w), (dgp, 0, mix_w)]
        (mixed,) = _rowwise(f"mix_post{l}", _mix_post, post_rows, post_pars, 1, [(d, BF16)], tm)
        x1 = _mm(f"out_proj{l}", mixed, w_out, jax.ShapeDtypeStruct((rows, d), F32), (rows // tm, 2),
                 pl.BlockSpec((tm, d), lambda i, j: (i, 0)), pl.BlockSpec((d, d // 2), lambda i, j: (0, j)),
                 pl.BlockSpec((tm, d // 2), lambda i, j: (i, j)), NN,
                 add=xcur, add_spec=pl.BlockSpec((tm, d // 2), lambda i, j: (i, j)))

        h2 = rms_fwd(f"rms_ffn{l}", x1, vec(inp['norm_ffn_g']), tm)
        tu = min(256, rows)
        w_up = gathered(4 * l + 2, x1)[('ffn_w_up', l)]
        up = _mm(f"ffn_up{l}", h2, w_up, jax.ShapeDtypeStruct((4, rows, ffn_half), F32), (4, rows // tu),
                 pl.BlockSpec((tu, d), lambda k, i: (i, 0)), pl.BlockSpec((None, d, ffn_half), lambda k, i: (k, 0, 0)),
                 pl.BlockSpec((None, tu, ffn_half), lambda k, i: (k, i, 0)), NN)
        w_dw = ffn_w_dw[l].reshape(FFN_TAPS, 2, ffn_half).transpose(1, 0, 2)
        b_dw = inp['ffn_b_dw'][l].reshape(2, 1, ffn_half)
        gc = dwconv_fwd(f"ffn_conv{l}", up, 0, ffn_half, w_dw, b_dw, FFN_TAPS, tc)
        act = ffn_act_fwd(f"ffn_act{l}", gc, up, tc)
        w_down = gathered(4 * l + 3, up)[('ffn_w_down', l)].reshape(2, ffn_half, d)
        x2 = _mm(f"ffn_down{l}", act, w_down, jax.ShapeDtypeStruct((rows, d), F32), (rows // tm, 4),
                 pl.BlockSpec((2, tm, ffn_half), lambda i, j: (0, i, 0)), pl.BlockSpec((2, ffn_half, d // 4), lambda i, j: (0, 0, j)),
                 pl.BlockSpec((tm, d // 4), lambda i, j: (i, j)), NN, inner=("lead", 2),
                 add=x1, add_spec=pl.BlockSpec((tm, d // 4), lambda i, j: (i, j)))
        saved.append(dict(x=xcur, h=h, proj=proj, z=z, y_ssm=y_ssm, h0=h0, h1=h1, xc=xc, a_t=a_t, hseq=hseq, dgp=dgp,
                          mixed=mixed, x1=x1, h2=h2, up=up, gc=gc, act=act, w_in=w_in, w_out=w_out, w_up=w_up, w_down=w_down,
                          a_bar=a_bar, b_bar=b_bar, c_cat=c_cat, post_pars=post_pars, gate_pars=gate_pars, w_dw=w_dw,
                          s5=(lam_re, lam_im, log_step, b_re, b_im), cv_w=cv_w_dw[l][None], lru_w=lru_w_conv[l][None]))
        xcur = x2

    loss_row, dx, dg_final = final_loss("final_loss", xcur, inp['norm_final_g'].reshape(1, d), target, tm)
    grads['norm_final_g'] = dg_final.reshape(d)

    big_g = {nme: [None] * depth for nme in BIG}
    reduce_groups = []

    def start_reduce(tag, keys):
        pieces = []
        for nme, lyr in keys:
            g = big_g[nme][lyr]
            if nme == 'ffn_w_down':
                g = g.reshape(2, 4, ffn_half // 2, d // 2)
            pieces.append(reduce_cores(f"reduce_cores_{nme}{lyr}", g.reshape(2, -1, g.shape[-1])).reshape(g.shape[1:]))
        zones = place_own(f"reduce_place_{tag}", pieces, xy, True)
        groups, token = exchange_start(f"reduce_start_{tag}", [list(zip(pieces, zones))], xy, True)
        reduce_groups.append((tag, keys, groups[0]))
        return token

    for l in reversed(range(depth)):
        s = saved[l]
        ncol = s['w_in'].shape[2]
        nh = n_state // 2
        tu = min(256, rows)
        dact = _mm(f"d_act{l}", dx, s['w_down'], jax.ShapeDtypeStruct((2, rows, ffn_half), F32), (2, rows // tu),
                   pl.BlockSpec((tu, d), lambda k, i: (i, 0)), pl.BlockSpec((None, ffn_half, d), lambda k, i: (k, 0, 0)),
                   pl.BlockSpec((None, tu, ffn_half), lambda k, i: (k, i, 0)), NT)
        tn = d // 4
        tk = min(1024, rows)
        big_g['ffn_w_down'][l] = _mm(
            f"dw_down{l}", s['act'], dx, jax.ShapeDtypeStruct((2, 2, ffn_half, d // 2), F32), (2, 4, rows // tk),
            pl.BlockSpec((None, tk, ffn_half), lambda hh, n, k: (hh, k, 0)), pl.BlockSpec((tk, tn), lambda hh, n, k: (k, n)),
            pl.BlockSpec((None, None, ffn_half, tn), lambda hh, n, k: (n // 2, hh, 0, n % 2)), TN, k_axis=2)
        dgc, dval = ffn_act_bwd(f"ffn_act_bwd{l}", s['gc'], s['up'], dact, tc)
        dgate, dw_dw, db_dw = dwconv_bwd(f"ffn_conv_bwd{l}", dgc, s['up'], 0, ffn_half, s['w_dw'], FFN_TAPS, tc, dx_dtype=BF16)
        grads['ffn_w_dw'][l] = dw_dw.transpose(1, 0, 2).reshape(FFN_TAPS, 2 * ffn_half)
        grads['ffn_b_dw'][l] = db_dw.reshape(2 * ffn_half)
        dup = jnp.concatenate([dgate, dval], axis=0)
        tm2 = min(1024, rows)
        dh2 = _mm(f"d_h2{l}", dup, s['w_up'], jax.ShapeDtypeStruct((rows, d), F32), (rows // tm2, 2, 4),
                  pl.BlockSpec((None, tm2, ffn_half), lambda i, j, k: (k, i, 0)), pl.BlockSpec((None, d // 2, ffn_half), lambda i, j, k: (k, j, 0)),
                  pl.BlockSpec((tm2, d // 2), lambda i, j, k: (i, j)), NT, k_axis=2)
        tmm = d // 4
        big_g['ffn_w_up'][l] = _mm(
            f"dw_up{l}", dup, s['h2'], jax.ShapeDtypeStruct((2, 4, ffn_half, d // 2), F32), (4, 4, rows // tk),
            pl.BlockSpec((None, tk, ffn_half), lambda k4, n, k: (k4, k, 0)), pl.BlockSpec((tk, tn), lambda k4, n, k: (k, n)),
            pl.BlockSpec((None, None, ffn_half, tn), lambda k4, n, k: (n // 2, k4, 0, n % 2)), TN, k_axis=2)
        token = start_reduce(f"ffn{l}", [('ffn_w_down', l), ('ffn_w_up', l)])
        dx1, dg = rms_bwd(f"rms_ffn_bwd{l}", s['x1'], inp['norm_ffn_g'][l].reshape(1, d) + token[0, 0], dh2, dx, tm)
        grads['norm_ffn_g'][l] = dg.reshape(d)
        dmixed = _mm(f"d_mixed{l}", dx1, s['w_out'], jax.ShapeDtypeStruct((rows, d), F32), (rows // tm, 4),
                     pl.BlockSpec((tm, d), lambda i, j: (i, 0)), pl.BlockSpec((d // 4, d), lambda i, j: (j, 0)),
                     pl.BlockSpec((tm, d // 4), lambda i, j: (i, j)), NT)
        tq = mix_w // 2
        big_g['w_out'][l] = _mm(
            f"dw_out{l}", s['mixed'], dx1, jax.ShapeDtypeStruct((2, 4, tq, d), F32), (8, rows // tk),
            pl.BlockSpec((tk, tq), lambda t, k: (k, t)), pl.BlockSpec((tk, d), lambda t, k: (k, 0)),
            pl.BlockSpec((None, None, tq, d), lambda t, k: (t % 2, t // 2, 0, 0)), TN, k_axis=1)
        post_rows = [(s['y_ssm'], 0, mix_w), (s['proj'], 0, mix_w), (s['h1'], 0, mix_w), (s['hseq'], 0, mix_w),
                     (s['proj'], 4, mix_w), (s['dgp'], 0, mix_w), (dmixed, 0, d)]
        res = _rowwise(f"mix_post_bwd{l}", _mix_post, post_rows, s['post_pars'], 1, [(mix_w, F32)] * 6, tm, with_grads=True)
        dy_ssm, du_dir, dh1, dhseq, dlru_g, ddgp = res[:6]
        dd, dwglu, dbglu, dlng, dlnb, dwpw, dbpw, dpoolbd, dscale = res[6:]
        grads['s5_d'][l], grads['s5_b_glu'][l] = dd.reshape(mix_w), dbglu.reshape(mix_w)
        grads['cv_ln_g'][l], grads['cv_ln_b'][l], grads['cv_b_pw'][l] = dlng.reshape(mix_w), dlnb.reshape(mix_w), dbpw.reshape(mix_w)
        grads['pool_w'][l] = _diag_blocks(dpoolbd, len(POOL_WINDOWS))
        grads['pool_scale'][l] = dscale.reshape(mix_w)
        big_g['s5_w_glu'][l] = _halves(dwglu, 4)
        big_g['cv_w_pw'][l] = _halves(dwpw, 4)
        dz = _mm(f"s5_dz{l}", dy_ssm, s['c_cat'], jax.ShapeDtypeStruct((2, rows, n_state), F32), (rows // tm, 2, 2),
                 pl.BlockSpec((tm, mix_w), lambda i, c, n: (i, 0)), pl.BlockSpec((None, nh, mix_w), lambda i, c, n: (c, n, 0)),
                 pl.BlockSpec((None, tm, nh), lambda i, c, n: (c, i, n)), NT)
        dccat = _mm(f"s5_dc{l}", s['z'], dy_ssm, jax.ShapeDtypeStruct((2, n_state, mix_w), F32), (2, n_state // mix_w, rows // tl),
                    pl.BlockSpec((None, tl, mix_w), lambda c, m, k: (c, k, m)), pl.BlockSpec((tl, mix_w), lambda c, m, k: (k, 0)),
                    pl.BlockSpec((None, mix_w, mix_w), lambda c, m, k: (c, m, 0)), TN, k_axis=2)
        grads['s5_c_re'][l] = _diag_blocks(dccat[0], S5_GROUPS).transpose(0, 2, 1)
        grads['s5_c_im'][l] = -_diag_blocks(dccat[1], S5_GROUPS).transpose(0, 2, 1)
        lam, da_bar = s5_scan_bwd(f"s5_scan_bwd{l}", dz, s['z'], s['a_bar'])
        du = _mm(f"s5_du{l}", lam, s['b_bar'], jax.ShapeDtypeStruct((rows, mix_w), F32), (rows // tm, 4),
                 pl.BlockSpec((None, tm, nh), lambda i, k: (k // 2, i, k % 2)), pl.BlockSpec((None, mix_w, nh), lambda i, k: (k // 2, 0, k % 2)),
                 pl.BlockSpec((tm, mix_w), lambda i, k: (i, 0)), NT, k_axis=1,
                 add=du_dir, add_spec=pl.BlockSpec((tm, mix_w), lambda i, k: (i, 0)))
        dbbar = _mm(f"s5_db{l}", s['proj'], lam, jax.ShapeDtypeStruct((2, mix_w, n_state), F32), (2, 2, rows // tl),
                    pl.BlockSpec((tl, mix_w), lambda c, n, k: (k, 0)), pl.BlockSpec((None, tl, nh), lambda c, n, k: (c, k, n)),
                    pl.BlockSpec((None, mix_w, nh), lambda c, n, k: (c, 0, n)), TN, k_axis=2)
        dlr, dli, dls, dbre, dbim = s5_param_bwd(f"s5_param_bwd{l}", *s['s5'], da_bar, dbbar, gsum)
        grads['s5_lam_re'][l] = dlr.reshape(S5_GROUPS, S5_STATE)
        grads['s5_lam_im'][l] = dli.reshape(S5_GROUPS, S5_STATE)
        grads['s5_log_step'][l] = dls[0, :S5_GROUPS]
        grads['s5_b_re'][l] = _diag_blocks(dbre, S5_GROUPS).transpose(0, 2, 1)
        grads['s5_b_im'][l] = _diag_blocks(dbim, S5_GROUPS).transpose(0, 2, 1)
        dh0, dw_cv, db_cv = dwconv_bwd(f"cv_conv_bwd{l}", dh1.reshape(1, rows, mix_w), s['h0'].reshape(1, rows, mix_w), 0, mix_w,
                                       s['cv_w'], CV_TAPS, tc)
        grads['cv_w_dw'][l], grads['cv_b_dw'][l] = dw_cv[0], db_cv.reshape(mix_w)
        dv, dgg = _rowwise(f"cv_glu_bwd{l}", _glu, [(s['proj'], 1, mix_w), (s['proj'], 2, mix_w), (dh0[0], 0, mix_w)], [], 1,
                           [(mix_w, F32)] * 2, tm, with_grads=True)
        da_t, db_t = lru_scan_bwd(f"lru_scan_bwd{l}", dhseq, s['a_t'], s['hseq'])
        res = _rowwise(f"lru_gate_bwd{l}", _lru_gate, [(s['xc'], 0, mix_w), (da_t, 0, mix_w), (db_t, 0, mix_w)], s['gate_pars'], 2,
                       [(mix_w, F32)], tm, with_grads=True)
        dxc, dwr, dwi, dbr, dbi, dlam = res
        grads['lru_w_r'][l], grads['lru_w_i'][l] = _diag_blocks(dwr, LRU_HEADS), _diag_blocks(dwi, LRU_HEADS)
        grads['lru_b_r'][l], grads['lru_b_i'][l], grads['lru_lam'][l] = dbr.reshape(mix_w), dbi.reshape(mix_w), dlam.reshape(mix_w)
        dlx, dw_lc, db_lc = dwconv_bwd(f"lru_conv_bwd{l}", dxc.reshape(1, rows, mix_w), s['proj'].reshape(1, rows, 4 * ncol), 3, mix_w,
                                       s['lru_w'], LRU_TAPS, tc)
        grads['lru_w_conv'][l], grads['lru_b_conv'][l] = dw_lc[0], db_lc.reshape(mix_w)
        dpx = pool_bwd(f"pool_bwd{l}", ddgp, tc)
        dproj = jnp.concatenate([du, dv, dgg, dlx[0], dlru_g, dpx], axis=-1)
        dh = _mm(f"d_h{l}", dproj, s['w_in'], jax.ShapeDtypeStruct((rows, d), F32), (rows // tm, 4),
                 pl.BlockSpec((tm, 4 * ncol), lambda i, j: (i, 0)), pl.BlockSpec((4, d // 4, ncol), lambda i, j: (0, j, 0)),
                 pl.BlockSpec((tm, d // 4), lambda i, j: (i, j)), NT, inner=("cols", 4))
        tk2 = min(2048, rows)
        big_g['w_in'][l] = _mm(
            f"dw_in{l}", s['h'], dproj, jax.ShapeDtypeStruct((2, 4, d // 2, ncol), F32), (4, 4, rows // tk2),
            pl.BlockSpec((tk2, tmm), lambda k4, m, k: (k, m)), pl.BlockSpec((tk2, ncol), lambda k4, m, k: (k, k4)),
            pl.BlockSpec((None, None, tmm, ncol), lambda k4, m, k: (m // 2, k4, m % 2, 0)), TN, k_axis=2)
        token = start_reduce(f"mix{l}", [('w_out', l), ('s5_w_glu', l), ('cv_w_pw', l), ('w_in', l)])
        dx, dg = rms_bwd(f"rms_mix_bwd{l}", s['x'], inp['norm_mix_g'][l].reshape(1, d) + token[0, 0], dh, dx1, tm)
        grads['norm_mix_g'][l] = dg.reshape(d)

    t_full = {}
    for tag, keys, group in reduce_groups:
        for key, r in zip(keys, exchange_wait(f"reduce_wait_{tag}", group, dx, xy, True)):
            t_full[key] = sum_and_share(f"share_cores_{key[0]}{key[1]}", r)

    outs = {}
    tiles = {'w_in': 256, 'w_out': 128, 'ffn_w_up': 128, 'ffn_w_down': 256, 's5_w_glu': 64, 'cv_w_pw': 64}
    for nme in BIG:
        g0, g1 = t_full[(nme, 0)], t_full[(nme, 1)]
        if nme == 'ffn_w_up':
            res = adamw_sharded(f"adamw_{nme}", *(jnp.swapaxes(inp[p + nme], 1, 2) for p in ('', 'm_', 'v_')), g0, g1, True, tiles[nme])
            outs[nme] = tuple(jnp.swapaxes(r, 1, 2) for r in res)
        else:
            outs[nme] = adamw_sharded(f"adamw_{nme}", inp[nme], inp['m_' + nme], inp['v_' + nme], g0, g1, nme == 'ffn_w_down', tiles[nme])

    small = [nme for nme in WEIGHTS if nme not in BIG]
    full_g = {nme: (grads[nme] if nme == 'norm_final_g' else jnp.stack(grads[nme])) for nme in small}
    flat = jnp.concatenate([full_g[nme].reshape(-1) for nme in small])
    n_flat = flat.shape[0]
    pad = (-n_flat) % (128 * 64)
    packed = jnp.pad(flat, (0, pad)).reshape(-1, 128)
    (g8,) = all_gather("gather_small", [packed], ("x", "y", "c"))
    gsum_small = sum_lead("sum_small", g8, 64).reshape(-1)
    my_chip = 2 * lax.axis_index("x") + lax.axis_index("y")
    red, off = {}, 0
    for nme in small:
        g = gsum_small[off:off + full_g[nme].size].reshape(full_g[nme].shape)
        off += full_g[nme].size
        if nme in SMALL_SHARDED:
            width = inp[nme].shape[2]
            g = lax.dynamic_slice_in_dim(g, my_chip * width, width, axis=2)
        red[nme] = g

    def pack(tree):
        f = jnp.concatenate([tree[nme].reshape(-1) for nme in small])
        return jnp.pad(f, (0, (-f.shape[0]) % (128 * 64))).reshape(-1, 128)

    pd, pm, pv = adamw_flat("adamw_small", pack({n_: inp[n_] for n_ in small}), pack(red), pack({n_: inp['m_' + n_] for n_ in small}),
                            pack({n_: inp['v_' + n_] for n_ in small}), 64)
    off = 0
    for nme in small:
        size, shape = inp[nme].size, inp[nme].shape
        outs[nme] = (red[nme],) + tuple(p.reshape(-1)[off:off + size].reshape(shape) for p in (pd, pm, pv))
        off += size

    loss = lax.psum(loss_row[0, 0], ("x", "y", "c"))
    result = [loss, dx[None]]
    for part in range(4):
        result += [outs[nme][part] for nme in WEIGHTS]
    return tuple(result)


def kernel(x, norm_mix_g, w_in, s5_lam_re, s5_lam_im, s5_log_step, s5_b_re, s5_b_im, s5_c_re, s5_c_im, s5_d, s5_w_glu, s5_b_glu, cv_w_dw, cv_b_dw, cv_ln_g, cv_ln_b, cv_w_pw, cv_b_pw, lru_w_conv, lru_b_conv, lru_w_r, lru_b_r, lru_w_i, lru_b_i, lru_lam, pool_w, pool_scale, w_out, norm_ffn_g, ffn_w_up, ffn_w_dw, ffn_b_dw, ffn_w_down, norm_final_g, loss_target, m_norm_mix_g, m_w_in, m_s5_lam_re, m_s5_lam_im, m_s5_log_step, m_s5_b_re, m_s5_b_im, m_s5_c_re, m_s5_c_im, m_s5_d, m_s5_w_glu, m_s5_b_glu, m_cv_w_dw, m_cv_b_dw, m_cv_ln_g, m_cv_ln_b, m_cv_w_pw, m_cv_b_pw, m_lru_w_conv, m_lru_b_conv, m_lru_w_r, m_lru_b_r, m_lru_w_i, m_lru_b_i, m_lru_lam, m_pool_w, m_pool_scale, m_w_out, m_norm_ffn_g, m_ffn_w_up, m_ffn_w_dw, m_ffn_b_dw, m_ffn_w_down, m_norm_final_g, v_norm_mix_g, v_w_in, v_s5_lam_re, v_s5_lam_im, v_s5_log_step, v_s5_b_re, v_s5_b_im, v_s5_c_re, v_s5_c_im, v_s5_d, v_s5_w_glu, v_s5_b_glu, v_cv_w_dw, v_cv_b_dw, v_cv_ln_g, v_cv_ln_b, v_cv_w_pw, v_cv_b_pw, v_lru_w_conv, v_lru_b_conv, v_lru_w_r, v_lru_b_r, v_lru_w_i, v_lru_b_i, v_lru_lam, v_pool_w, v_pool_scale, v_w_out, v_norm_ffn_g, v_ffn_w_up, v_ffn_w_dw, v_ffn_b_dw, v_ffn_w_down, v_norm_final_g):
    inp = dict(locals())
    return _step(inp)
```

```python
import functools

import jax
import jax.numpy as jnp
from jax import lax
from jax.experimental import pallas as pl
from jax.experimental.pallas import tpu as pltpu

F32 = jnp.float32
BF16 = jnp.bfloat16

VMEM_LIMIT_BYTES = 56 * 1024 * 1024
SUBLANES = 8

EPS = 1e-6
S5_GROUPS, S5_STATE, S5_GROUP_CH = 32, 64, 16
LRU_HEADS, LRU_C = 8, 8.0
POOL_WINDOWS = (2, 4, 8, 16)
CV_TAPS, LRU_TAPS, FFN_TAPS = 31, 4, 3
SCAN_CHUNK = 64

ADAM_LR, ADAM_B1, ADAM_B2, ADAM_EPS, ADAM_WD, ADAM_STEP = 0.001, 0.9, 0.999, 1e-08, 0.01, 10

NN = ((1,), (0,))
NT = ((1,), (1,))
TN = ((0,), (0,))

WEIGHTS = ['norm_mix_g', 'w_in', 's5_lam_re', 's5_lam_im', 's5_log_step', 's5_b_re', 's5_b_im', 's5_c_re', 's5_c_im',
           's5_d', 's5_w_glu', 's5_b_glu', 'cv_w_dw', 'cv_b_dw', 'cv_ln_g', 'cv_ln_b', 'cv_w_pw', 'cv_b_pw',
           'lru_w_conv', 'lru_b_conv', 'lru_w_r', 'lru_b_r', 'lru_w_i', 'lru_b_i', 'lru_lam', 'pool_w', 'pool_scale',
           'w_out', 'norm_ffn_g', 'ffn_w_up', 'ffn_w_dw', 'ffn_b_dw', 'ffn_w_down', 'norm_final_g']
BIG = ('w_in', 'w_out', 'ffn_w_up', 'ffn_w_down', 's5_w_glu', 'cv_w_pw')
SMALL_SHARDED = {'cv_w_dw': 2, 'lru_w_conv': 2, 'ffn_w_dw': 2}


def _params(sem=None):
    if sem is None:
        return pltpu.CompilerParams(vmem_limit_bytes=VMEM_LIMIT_BYTES)
    return pltpu.CompilerParams(dimension_semantics=sem, vmem_limit_bytes=VMEM_LIMIT_BYTES)


def _row_tile(rows, cap):
    best = SUBLANES
    for t in range(SUBLANES, min(rows, cap) + 1, SUBLANES):
        if rows % t == 0:
            best = t
    return best


def _bdot(a, b, dims=NN):
    return lax.dot_general(a.astype(BF16), b.astype(BF16), (dims, ((), ())), preferred_element_type=F32)


@jax.custom_vjp
def bdot(a, b):
    return _bdot(a, b)


def _bdot_fwd(a, b):
    return _bdot(a, b), (a, b)


def _bdot_bwd(res, g):
    a, b = res
    return _bdot(g, b, NT).astype(a.dtype), _bdot(a, g, TN).astype(b.dtype)


bdot.defvjp(_bdot_fwd, _bdot_bwd)


def _mm(name, a, b, out_sds, grid, a_spec, b_spec, o_spec, dims, k_axis=None, add=None, add_spec=None, inner=None):
    nk = grid[k_axis] if k_axis is not None else 1
    has_add = add is not None
    acc_shape = tuple(d for d in o_spec.block_shape if d is not None)
    acc_in_out = out_sds.dtype == F32

    def product(a_ref, b_ref):
        if inner is None:
            return _bdot(a_ref[...], b_ref[...], dims)
        kind, n = inner
        width = a_ref.shape[-1] // n
        acc = None
        for j in range(n):
            a_j = a_ref[j] if kind == "lead" else a_ref[:, j * width:(j + 1) * width]
            p = _bdot(a_j, b_ref[j], dims)
            acc = p if acc is None else acc + p
        return acc

    def body(*refs):
        a_ref, b_ref = refs[0], refs[1]
        add_ref = refs[2] if has_add else None
        o_ref = refs[3] if has_add else refs[2]
        prod = product(a_ref, b_ref)
        if k_axis is None:
            if has_add:
                prod = prod + add_ref[...]
            o_ref[...] = prod.astype(o_ref.dtype)
        else:
            acc_ref = o_ref if acc_in_out else refs[-1]
            k = pl.program_id(k_axis)

            @pl.when(k == 0)
            def _():
                acc_ref[...] = prod

            @pl.when(k > 0)
            def _():
                acc_ref[...] += prod

            if has_add or not acc_in_out:
                @pl.when(k == nk - 1)
                def _():
                    r = acc_ref[...]
                    if has_add:
                        r = r + add_ref[...]
                    o_ref[...] = r.astype(o_ref.dtype)

    sem = tuple("arbitrary" if d == k_axis else "parallel" for d in range(len(grid)))
    in_specs = [a_spec, b_spec] + ([add_spec] if has_add else [])
    args = (a, b) + ((add,) if has_add else ())
    scratch = [pltpu.VMEM(acc_shape, F32)] if (k_axis is not None and not acc_in_out) else []
    return pl.pallas_call(body, out_shape=out_sds, grid=grid, in_specs=in_specs, out_specs=o_spec,
                          scratch_shapes=scratch, compiler_params=_params(sem), name=name)(*args)


def _rms(x, g):
    return x * lax.rsqrt(jnp.mean(x * x, axis=-1, keepdims=True) + EPS) * g


def rms_fwd(name, x, g, tm):
    rows, d = x.shape

    def body(x_ref, g_ref, o_ref):
        o_ref[...] = _rms(x_ref[...], g_ref[...]).astype(BF16)

    return pl.pallas_call(
        body, out_shape=jax.ShapeDtypeStruct((rows, d), BF16), grid=(rows // tm,),
        in_specs=[pl.BlockSpec((tm, d), lambda i: (i, 0)), pl.BlockSpec((1, d), lambda i: (0, 0))],
        out_specs=pl.BlockSpec((tm, d), lambda i: (i, 0)), compiler_params=_params(("parallel",)), name=name)(x, g)


def rms_bwd(name, x, g, dh, dres, tm):
    rows, d = x.shape

    def body(x_ref, g_ref, dh_ref, dres_ref, dx_ref, dg_ref):
        _, vjp = jax.vjp(_rms, x_ref[...], g_ref[...])
        dx, dg = vjp(dh_ref[...])
        dx_ref[...] = dx + dres_ref[...]

        @pl.when(pl.program_id(0) == 0)
        def _():
            dg_ref[...] = jnp.zeros_like(dg_ref)

        dg_ref[...] += dg

    row = pl.BlockSpec((tm, d), lambda i: (i, 0))
    vec = pl.BlockSpec((1, d), lambda i: (0, 0))
    return pl.pallas_call(
        body, out_shape=(jax.ShapeDtypeStruct((rows, d), F32), jax.ShapeDtypeStruct((1, d), F32)), grid=(rows // tm,),
        in_specs=[row, vec, row, row], out_specs=(row, vec), compiler_params=_params(("arbitrary",)), name=name)(x, g, dh, dres)


def final_loss(name, x, g, target, tm):
    rows, d = x.shape

    def body(x_ref, g_ref, t_ref, l_ref, dx_ref, dg_ref):
        def f(xv, gv):
            e = _rms(xv, gv) - t_ref[...]
            return 0.5 * jnp.sum(jnp.mean(e * e, axis=-1))

        loss, (dx, dg) = jax.value_and_grad(f, argnums=(0, 1))(x_ref[...], g_ref[...])
        dx_ref[...] = dx

        @pl.when(pl.program_id(0) == 0)
        def _():
            l_ref[...] = jnp.zeros_like(l_ref)
            dg_ref[...] = jnp.zeros_like(dg_ref)

        l_ref[...] += jnp.full(l_ref.shape, loss, F32)
        dg_ref[...] += dg

    row = pl.BlockSpec((tm, d), lambda i: (i, 0))
    vec = pl.BlockSpec((1, d), lambda i: (0, 0))
    lspec = pl.BlockSpec((1, 128), lambda i: (0, 0))
    return pl.pallas_call(
        body, out_shape=(jax.ShapeDtypeStruct((1, 128), F32), jax.ShapeDtypeStruct((rows, d), F32), jax.ShapeDtypeStruct((1, d), F32)),
        grid=(rows // tm,), in_specs=[row, vec, row], out_specs=(lspec, row, vec),
        compiler_params=_params(("arbitrary",)), name=name)(x, g, target)


def _rowwise(name, fn, row_ins, par_ins, n_row_out, row_out_dtypes, tm, with_grads=False):
    rows = row_ins[0][0].shape[0]
    n_prim = len(row_ins) - (n_row_out if with_grads else 0)
    n_par = len(par_ins)

    def body(*refs):
        ins = [r[...] for r in refs[:len(row_ins) + n_par]]
        outs = refs[len(row_ins) + n_par:]
        prim, cts, pars = ins[:n_prim], ins[n_prim:len(row_ins)], ins[len(row_ins):]
        if not with_grads:
            res = fn(*prim, *pars)
            for o_ref, r in zip(outs, res):
                o_ref[...] = r.astype(o_ref.dtype)
            return
        _, vjp = jax.vjp(fn, *prim, *[p.astype(F32) for p in pars])
        grads = vjp(tuple(cts))
        for o_ref, gr in zip(outs[:n_prim], grads[:n_prim]):
            o_ref[...] = gr.astype(o_ref.dtype)

        @pl.when(pl.program_id(0) == 0)
        def _():
            for o_ref in outs[n_prim:]:
                o_ref[...] = jnp.zeros_like(o_ref)

        for o_ref, gr in zip(outs[n_prim:], grads[n_prim:]):
            o_ref[...] += gr.astype(F32)

    in_specs = [pl.BlockSpec((tm, w), (lambda i, c=c: (i, c))) for (_, c, w) in row_ins]
    in_specs += [pl.BlockSpec(p.shape, (lambda i, n=p.ndim: (0,) * n)) for p in par_ins]
    args = [a for (a, _, _) in row_ins] + list(par_ins)
    if not with_grads:
        out_shape = tuple(jax.ShapeDtypeStruct((rows, w), dt) for (w, dt) in row_out_dtypes)
        out_specs = tuple(pl.BlockSpec((tm, w), lambda i: (i, 0)) for (w, _) in row_out_dtypes)
        sem = ("parallel",)
    else:
        out_shape = tuple(jax.ShapeDtypeStruct((rows, w), dt) for (w, dt) in row_out_dtypes)
        out_shape += tuple(jax.ShapeDtypeStruct(p.shape, F32) for p in par_ins)
        out_specs = tuple(pl.BlockSpec((tm, w), lambda i: (i, 0)) for (w, _) in row_out_dtypes)
        out_specs += tuple(pl.BlockSpec(p.shape, (lambda i, n=p.ndim: (0,) * n)) for p in par_ins)
        sem = ("arbitrary",)
    return pl.pallas_call(body, out_shape=out_shape, grid=(rows // tm,), in_specs=in_specs, out_specs=out_specs,
                          compiler_params=_params(sem), name=name)(*args)


def _glu(v, g):
    return (v * jax.nn.sigmoid(g),)


def _neg_expm1(z):
    return -jnp.tanh(0.5 * z) * (jnp.exp(z) + 1.0)


def _lru_gate(xc, w_r, w_i, b_r, b_i, lam):
    r = jax.nn.sigmoid(bdot(xc, w_r) + b_r)
    i = jax.nn.sigmoid(bdot(xc, w_i) + b_i)
    log_a = -LRU_C * r * jax.nn.softplus(-lam)
    a = jnp.exp(log_a)
    mult = jnp.sqrt(_neg_expm1(2.0 * log_a))
    return a, mult * (i * xc)


def _layernorm(x, g, b):
    mu = jnp.mean(x, axis=-1, keepdims=True)
    var = jnp.mean(jnp.square(x - mu), axis=-1, keepdims=True)
    return (x - mu) * lax.rsqrt(var + EPS) * g + b


def _mix_post(y_ssm, u, h1, hseq, lru_g, dgp, s5_d, w_glu, b_glu, ln_g, ln_b, w_pw, b_pw, pool_bd, pool_scale):
    y = y_ssm + s5_d * u
    gl = jax.nn.gelu(y, approximate=True)
    out_s5 = gl * jax.nn.sigmoid(bdot(gl, w_glu) + b_glu)
    out_cv = bdot(jax.nn.silu(_layernorm(h1, ln_g, ln_b)), w_pw) + b_pw
    out_lru = hseq * jax.nn.gelu(lru_g, approximate=True)
    out_pool = bdot(dgp, pool_bd) * pool_scale
    return (jnp.concatenate([out_s5, out_cv, out_lru, out_pool], axis=-1),)


def _ffn_act(gc, val):
    return (jax.nn.gelu(gc, approximate=True) * val,)


def ffn_act_fwd(name, gc, up, tm):
    _, rows, c = gc.shape

    def body(g_ref, v_ref, o_ref):
        o_ref[...] = _ffn_act(g_ref[...], v_ref[...])[0].astype(BF16)

    return pl.pallas_call(
        body, out_shape=jax.ShapeDtypeStruct((2, rows, c), BF16), grid=(2, rows // tm),
        in_specs=[pl.BlockSpec((None, tm, c), lambda h, i: (h, i, 0)), pl.BlockSpec((None, tm, c), lambda h, i: (h + 2, i, 0))],
        out_specs=pl.BlockSpec((None, tm, c), lambda h, i: (h, i, 0)),
        compiler_params=_params(("parallel", "parallel")), name=name)(gc, up)


def ffn_act_bwd(name, gc, up, dact, tm):
    _, rows, c = gc.shape

    def body(g_ref, v_ref, d_ref, dg_ref, dv_ref):
        _, vjp = jax.vjp(_ffn_act, g_ref[...], v_ref[...])
        dg, dv = vjp((d_ref[...],))
        dg_ref[...] = dg
        dv_ref[...] = dv.astype(BF16)

    blk = pl.BlockSpec((None, tm, c), lambda h, i: (h, i, 0))
    return pl.pallas_call(
        body, out_shape=(jax.ShapeDtypeStruct((2, rows, c), F32), jax.ShapeDtypeStruct((2, rows, c), BF16)), grid=(2, rows // tm),
        in_specs=[blk, pl.BlockSpec((None, tm, c), lambda h, i: (h + 2, i, 0)), blk], out_specs=(blk, blk),
        compiler_params=_params(("parallel", "parallel")), name=name)(gc, up, dact)


def _halo_rows(taps):
    return -(-(taps - 1) // SUBLANES) * SUBLANES


def dwconv_fwd(name, x, cblk, c, w, b, taps, tm, out_dtype=F32):
    nb = w.shape[0]
    rows = x.shape[1]
    halo = _halo_rows(taps)
    per = tm // halo

    def body(x_ref, h_ref, w_ref, b_ref, o_ref):
        i = pl.program_id(1)
        prev = jnp.where(i > 0, h_ref[...], 0.0)
        ext = jnp.concatenate([prev, x_ref[...]], axis=0)
        acc = jnp.broadcast_to(b_ref[...], (tm, c))
        for k in range(taps):
            off = halo - (taps - 1) + k
            acc = acc + w_ref[k:k + 1, :] * ext[off:off + tm]
        o_ref[...] = acc.astype(o_ref.dtype)

    return pl.pallas_call(
        body, out_shape=jax.ShapeDtypeStruct((nb, rows, c), out_dtype), grid=(nb, rows // tm),
        in_specs=[pl.BlockSpec((None, tm, c), lambda n, i: (n, i, cblk)),
                  pl.BlockSpec((None, halo, c), lambda n, i: (n, jnp.maximum(i * per - 1, 0), cblk)),
                  pl.BlockSpec((None, taps, c), lambda n, i: (n, 0, 0)),
                  pl.BlockSpec((None, 1, c), lambda n, i: (n, 0, 0))],
        out_specs=pl.BlockSpec((None, tm, c), lambda n, i: (n, i, 0)),
        compiler_params=_params(("parallel", "parallel")), name=name)(x, x, w, b)


def dwconv_bwd(name, dy, x, cblk, c, w, taps, tm, dx_dtype=F32):
    nb = w.shape[0]
    rows = x.shape[1]
    halo = _halo_rows(taps)
    per = tm // halo
    n_tiles = rows // tm
    last_halo = rows // halo - 1

    def body(dy_ref, dn_ref, x_ref, xp_ref, w_ref, dx_ref, dw_ref, db_ref):
        i = pl.program_id(1)
        dyv = dy_ref[...]
        nxt = jnp.where(i < n_tiles - 1, dn_ref[...], 0.0)
        dext = jnp.concatenate([dyv, nxt], axis=0)
        prev = jnp.where(i > 0, xp_ref[...], 0.0)
        xext = jnp.concatenate([prev, x_ref[...]], axis=0)
        acc = jnp.zeros((tm, c), F32)

        @pl.when(i == 0)
        def _():
            dw_ref[...] = jnp.zeros_like(dw_ref)
            db_ref[...] = jnp.zeros_like(db_ref)

        for k in range(taps):
            acc = acc + w_ref[k:k + 1, :] * dext[taps - 1 - k:taps - 1 - k + tm]
            off = halo - (taps - 1) + k
            dw_ref[k:k + 1, :] += jnp.sum(dyv * xext[off:off + tm], axis=0, keepdims=True)
        dx_ref[...] = acc.astype(dx_ref.dtype)
        db_ref[...] += jnp.sum(dyv, axis=0, keepdims=True)

    return pl.pallas_call(
        body, out_shape=(jax.ShapeDtypeStruct((nb, rows, c), dx_dtype), jax.ShapeDtypeStruct((nb, taps, c), F32),
                         jax.ShapeDtypeStruct((nb, 1, c), F32)),
        grid=(nb, n_tiles),
        in_specs=[pl.BlockSpec((None, tm, c), lambda n, i: (n, i, 0)),
                  pl.BlockSpec((None, halo, c), lambda n, i: (n, jnp.minimum((i + 1) * per, last_halo), 0)),
                  pl.BlockSpec((None, tm, c), lambda n, i: (n, i, cblk)),
                  pl.BlockSpec((None, halo, c), lambda n, i: (n, jnp.maximum(i * per - 1, 0), cblk)),
                  pl.BlockSpec((None, taps, c), lambda n, i: (n, 0, 0))],
        out_specs=(pl.BlockSpec((None, tm, c), lambda n, i: (n, i, 0)), pl.BlockSpec((None, taps, c), lambda n, i: (n, 0, 0)),
                   pl.BlockSpec((None, 1, c), lambda n, i: (n, 0, 0))),
        compiler_params=_params(("parallel", "arbitrary")), name=name)(dy, dy, x, x, w)


POOL_HALO = 16


def pool_fwd(name, proj, cblk, tm):
    rows = proj.shape[0]
    c = 128 * len(POOL_WINDOWS)
    per = tm // POOL_HALO

    def body(x_ref, h_ref, o_ref):
        i = pl.program_id(0)
        xv = x_ref[...]
        ext = jnp.concatenate([jnp.where(i > 0, h_ref[...], 0.0), xv], axis=0)
        t1 = (lax.broadcasted_iota(jnp.int32, (tm, 128), 0) + i * tm + 1).astype(F32)
        outs = []
        for gi, win in enumerate(POOL_WINDOWS):
            seg = ext[:, gi * 128:(gi + 1) * 128]
            s = seg[POOL_HALO:POOL_HALO + tm]
            for j in range(1, win):
                s = s + seg[POOL_HALO - j:POOL_HALO - j + tm]
            outs.append(s / jnp.minimum(t1, float(win)) - xv[:, gi * 128:(gi + 1) * 128])
        o_ref[...] = jnp.concatenate(outs, axis=-1)

    return pl.pallas_call(
        body, out_shape=jax.ShapeDtypeStruct((rows, c), F32), grid=(rows // tm,),
        in_specs=[pl.BlockSpec((tm, c), lambda i: (i, cblk)),
                  pl.BlockSpec((POOL_HALO, c), lambda i: (jnp.maximum(i * per - 1, 0), cblk))],
        out_specs=pl.BlockSpec((tm, c), lambda i: (i, 0)), compiler_params=_params(("parallel",)), name=name)(proj, proj)


def pool_bwd(name, dd, tm):
    rows, c = dd.shape
    per = tm // POOL_HALO
    n_tiles = rows // tm
    last_halo = rows // POOL_HALO - 1

    def body(d_ref, n_ref, o_ref):
        i = pl.program_id(0)
        dv = d_ref[...]
        nxt = jnp.where(i < n_tiles - 1, n_ref[...], 0.0)
        t1 = (lax.broadcasted_iota(jnp.int32, (tm, 128), 0) + i * tm + 1).astype(F32)
        t1n = (lax.broadcasted_iota(jnp.int32, (POOL_HALO, 128), 0) + (i + 1) * tm + 1).astype(F32)
        outs = []
        for gi, win in enumerate(POOL_WINDOWS):
            sl = slice(gi * 128, (gi + 1) * 128)
            q = jnp.concatenate([dv[:, sl] / jnp.minimum(t1, float(win)), nxt[:, sl] / jnp.minimum(t1n, float(win))], axis=0)
            s = q[0:tm]
            for j in range(1, win):
                s = s + q[j:j + tm]
            outs.append(s - dv[:, sl])
        o_ref[...] = jnp.concatenate(outs, axis=-1)

    return pl.pallas_call(
        body, out_shape=jax.ShapeDtypeStruct((rows, c), F32), grid=(n_tiles,),
        in_specs=[pl.BlockSpec((tm, c), lambda i: (i, 0)),
                  pl.BlockSpec((POOL_HALO, c), lambda i: (jnp.minimum((i + 1) * per, last_halo), 0))],
        out_specs=pl.BlockSpec((tm, c), lambda i: (i, 0)), compiler_params=_params(("parallel",)), name=name)(dd, dd)


def _shift_down(v, s, fill):
    r = lax.broadcasted_iota(jnp.int32, v.shape, 0)
    return jnp.where(r >= s, pltpu.roll(v, s, 0), fill)


def _shift_up(v, s, fill):
    n = v.shape[0]
    r = lax.broadcasted_iota(jnp.int32, v.shape, 0)
    return jnp.where(r < n - s, pltpu.roll(v, n - s, 0), fill)


def _cscan_chunk(vr, vi, powers, reverse):
    for k, (qr, qi) in enumerate(powers):
        s = 1 << k
        if reverse:
            sr, si = _shift_up(vr, s, 0.0), _shift_up(vi, s, 0.0)
            vr, vi = vr + qr * sr + qi * si, vi + qr * si - qi * sr
        else:
            sr, si = _shift_down(vr, s, 0.0), _shift_down(vi, s, 0.0)
            vr, vi = vr + qr * sr - qi * si, vi + qr * si + qi * sr
    return vr, vi


def _powers(pr, pi, n):
    out = [(pr, pi)]
    for _ in range(n - 1):
        pr, pi = pr * pr - pi * pi, 2.0 * pr * pi
        out.append((pr, pi))
    return out


def s5_scan_fwd(name, bu, a):
    _, rows, n = bu.shape
    t = min(SCAN_CHUNK, rows)
    steps = t.bit_length() - 1

    def body(bu_ref, a_ref, z_ref):
        pr, pi = a_ref[0], a_ref[1]
        powers = _powers(pr, pi, steps)
        r = lax.broadcasted_iota(jnp.int32, (t, 128), 0)
        tr, ti = _cscan_chunk(jnp.where(r == 0, pr, 0.0), jnp.where(r == 0, pi, 0.0), powers, False)

        def chunk(ci, carry):
            base = pl.multiple_of(ci * t, t)
            vr, vi = _cscan_chunk(bu_ref[0, pl.ds(base, t), :], bu_ref[1, pl.ds(base, t), :], powers, False)
            cr, cim = carry
            zr = vr + tr * cr - ti * cim
            zi = vi + tr * cim + ti * cr
            z_ref[0, pl.ds(base, t), :] = zr
            z_ref[1, pl.ds(base, t), :] = zi
            return zr[t - 1:t, :], zi[t - 1:t, :]

        zero = jnp.zeros((1, 128), F32)
        lax.fori_loop(0, rows // t, chunk, (zero, zero))

    return pl.pallas_call(
        body, out_shape=jax.ShapeDtypeStruct((2, rows, n), F32), grid=(n // 128,),
        in_specs=[pl.BlockSpec((2, rows, 128), lambda j: (0, 0, j)), pl.BlockSpec((2, 1, 128), lambda j: (0, 0, j))],
        out_specs=pl.BlockSpec((2, rows, 128), lambda j: (0, 0, j)), compiler_params=_params(("parallel",)), name=name)(bu, a)


def s5_scan_bwd(name, dz, z, a):
    _, rows, n = dz.shape
    t = min(SCAN_CHUNK, rows)
    steps = t.bit_length() - 1
    n_chunks = rows // t

    def body(dz_ref, z_ref, a_ref, lam_ref, da_ref):
        pr, pi = a_ref[0], a_ref[1]
        powers = _powers(pr, pi, steps)
        r = lax.broadcasted_iota(jnp.int32, (t, 128), 0)
        tr, ti = _cscan_chunk(jnp.where(r == t - 1, pr, 0.0), jnp.where(r == t - 1, -pi, 0.0), powers, True)

        def chunk(k, carry):
            ci = n_chunks - 1 - k
            base = pl.multiple_of(ci * t, t)
            vr, vi = _cscan_chunk(dz_ref[0, pl.ds(base, t), :], dz_ref[1, pl.ds(base, t), :], powers, True)
            cr, cim, dar, dai = carry
            lr = vr + tr * cr - ti * cim
            li = vi + tr * cim + ti * cr
            lam_ref[0, pl.ds(base, t), :] = lr
            lam_ref[1, pl.ds(base, t), :] = li
            pbase = pl.multiple_of(jnp.maximum(base - SUBLANES, 0), SUBLANES)
            keep = (ci > 0).astype(F32)
            pzr = z_ref[0, pl.ds(pbase, SUBLANES), :][SUBLANES - 1:SUBLANES, :] * keep
            pzi = z_ref[1, pl.ds(pbase, SUBLANES), :][SUBLANES - 1:SUBLANES, :] * keep
            zpr = _shift_down(z_ref[0, pl.ds(base, t), :], 1, pzr)
            zpi = _shift_down(z_ref[1, pl.ds(base, t), :], 1, pzi)
            dar = dar + jnp.sum(lr * zpr + li * zpi, axis=0, keepdims=True)
            dai = dai + jnp.sum(li * zpr - lr * zpi, axis=0, keepdims=True)
            return lr[0:1, :], li[0:1, :], dar, dai

        zero = jnp.zeros((1, 128), F32)
        _, _, dar, dai = lax.fori_loop(0, n_chunks, chunk, (zero, zero, zero, zero))
        da_ref[0] = dar
        da_ref[1] = dai

    seq = pl.BlockSpec((2, rows, 128), lambda j: (0, 0, j))
    vec = pl.BlockSpec((2, 1, 128), lambda j: (0, 0, j))
    return pl.pallas_call(
        body, out_shape=(jax.ShapeDtypeStruct((2, rows, n), F32), jax.ShapeDtypeStruct((2, 1, n), F32)), grid=(n // 128,),
        in_specs=[seq, seq, vec], out_specs=(seq, vec), compiler_params=_params(("parallel",)), name=name)(dz, z, a)


def _rscan_chunk(a, b, steps, reverse):
    shift = _shift_up if reverse else _shift_down
    for k in range(steps):
        s = 1 << k
        b = b + a * shift(b, s, 0.0)
        a = a * shift(a, s, 1.0)
    return a, b


def lru_scan_fwd(name, a, b):
    rows, n = a.shape
    t = min(SCAN_CHUNK, rows)
    steps = t.bit_length() - 1

    def body(a_ref, b_ref, h_ref):
        def chunk(ci, carry):
            base = pl.multiple_of(ci * t, t)
            pa, hb = _rscan_chunk(a_ref[pl.ds(base, t), :], b_ref[pl.ds(base, t), :], steps, False)
            h = hb + pa * carry
            h_ref[pl.ds(base, t), :] = h
            return h[t - 1:t, :]

        lax.fori_loop(0, rows // t, chunk, jnp.zeros((1, 128), F32))

    seq = pl.BlockSpec((rows, 128), lambda j: (0, j))
    return pl.pallas_call(body, out_shape=jax.ShapeDtypeStruct((rows, n), F32), grid=(n // 128,), in_specs=[seq, seq],
                          out_specs=seq, compiler_params=_params(("parallel",)), name=name)(a, b)


def lru_scan_bwd(name, dh, a, h):
    rows, n = a.shape
    t = min(SCAN_CHUNK, rows)
    steps = t.bit_length() - 1
    n_chunks = rows // t

    def body(dh_ref, a_ref, h_ref, da_ref, db_ref):
        def chunk(k, carry):
            ci = n_chunks - 1 - k
            base = pl.multiple_of(ci * t, t)
            nbase = pl.multiple_of(jnp.minimum(base + t, rows - SUBLANES), SUBLANES)
            a_next = a_ref[pl.ds(nbase, SUBLANES), :][0:1, :]
            an = _shift_up(a_ref[pl.ds(base, t), :], 1, a_next)
            pa, mb = _rscan_chunk(an, dh_ref[pl.ds(base, t), :], steps, True)
            mu = mb + pa * carry
            pbase = pl.multiple_of(jnp.maximum(base - SUBLANES, 0), SUBLANES)
            hp_row = h_ref[pl.ds(pbase, SUBLANES), :][SUBLANES - 1:SUBLANES, :] * (ci > 0).astype(F32)
            hp = _shift_down(h_ref[pl.ds(base, t), :], 1, hp_row)
            da_ref[pl.ds(base, t), :] = mu * hp
            db_ref[pl.ds(base, t), :] = mu
            return mu[0:1, :]

        lax.fori_loop(0, n_chunks, chunk, jnp.zeros((1, 128), F32))

    seq = pl.BlockSpec((rows, 128), lambda j: (0, j))
    return pl.pallas_call(
        body, out_shape=(jax.ShapeDtypeStruct((rows, n), F32), jax.ShapeDtypeStruct((rows, n), F32)), grid=(n // 128,),
        in_specs=[seq, seq, seq], out_specs=(seq, seq), compiler_params=_params(("parallel",)), name=name)(dh, a, h)


def _s5_param(lr, li, ls, bre, bim):
    st = jnp.exp(ls)
    er = jnp.exp(lr * st)
    th = li * st
    ar, ai = er * jnp.cos(th), er * jnp.sin(th)
    nr, ni = ar - 1.0, ai
    den = lr * lr + li * li
    cr, ci = (nr * lr + ni * li) / den, (ni * lr - nr * li) / den
    return ar, ai, cr * bre - ci * bim, cr * bim + ci * bre


def s5_param_fwd(name, lr, li, ls, bre, bim):
    gh, n = bre.shape

    def body(lr_ref, li_ref, ls_ref, bre_ref, bim_ref, a_ref, bb_ref):
        ar, ai, br, bi = _s5_param(lr_ref[...], li_ref[...], ls_ref[...], bre_ref[...], bim_ref[...])
        a_ref[0] = ar
        a_ref[1] = ai
        bb_ref[0] = br.astype(BF16)
        bb_ref[1] = bi.astype(BF16)

    return pl.pallas_call(body, out_shape=(jax.ShapeDtypeStruct((2, 1, n), F32), jax.ShapeDtypeStruct((2, gh, n), BF16)),
                          compiler_params=_params(), name=name)(lr, li, ls, bre, bim)


def s5_param_bwd(name, lr, li, ls, bre, bim, da, dbb, gsum):
    gh, n = bre.shape

    def body(lr_ref, li_ref, ls_ref, bre_ref, bim_ref, da_ref, dbb_ref, gs_ref, dlr_ref, dli_ref, dls_ref, dbre_ref, dbim_ref):
        _, vjp = jax.vjp(_s5_param, lr_ref[...], li_ref[...], ls_ref[...], bre_ref[...], bim_ref[...])
        dlr, dli, dls, dbre, dbim = vjp((da_ref[0], da_ref[1], dbb_ref[0], dbb_ref[1]))
        dlr_ref[...] = dlr
        dli_ref[...] = dli
        dls_ref[...] = jnp.dot(jnp.broadcast_to(dls, (SUBLANES, n)), gs_ref[...], preferred_element_type=F32,
                               precision=lax.Precision.HIGHEST)
        dbre_ref[...] = dbre
        dbim_ref[...] = dbim

    vec = jax.ShapeDtypeStruct((1, n), F32)
    mat = jax.ShapeDtypeStruct((gh, n), F32)
    return pl.pallas_call(body, out_shape=(vec, vec, jax.ShapeDtypeStruct((SUBLANES, 128), F32), mat, mat),
                          compiler_params=_params(), name=name)(lr, li, ls, bre, bim, da, dbb, gsum)


def sum_lead(name, x, tr):
    n, rows, cols = x.shape

    def body(x_ref, o_ref):
        acc = x_ref[0]
        for j in range(1, n):
            acc = acc + x_ref[j]
        o_ref[...] = acc

    return pl.pallas_call(
        body, out_shape=jax.ShapeDtypeStruct((rows, cols), x.dtype), grid=(rows // tr,),
        in_specs=[pl.BlockSpec((n, tr, cols), lambda i: (0, i, 0))], out_specs=pl.BlockSpec((tr, cols), lambda i: (i, 0)),
        compiler_params=_params(("parallel",)), name=name)(x)


def _adamw(w, g, m, v):
    m = ADAM_B1 * m + (1.0 - ADAM_B1) * g
    v = ADAM_B2 * v + (1.0 - ADAM_B2) * jnp.square(g)
    m_hat = m / (1.0 - ADAM_B1 ** ADAM_STEP)
    v_hat = v / (1.0 - ADAM_B2 ** ADAM_STEP)
    delta = -ADAM_LR * (m_hat / (jnp.sqrt(v_hat) + ADAM_EPS) + ADAM_WD * w)
    return delta, m, v


def adamw_sharded(name, w, m, v, g0, g1, split_cols, tile):
    _, r, c = w.shape
    if split_cols:
        nt = c // tile
        per = (c // 2) // tile
        wspec = pl.BlockSpec((None, r, tile), lambda l, t: (l, 0, t))
        gspec = pl.BlockSpec((None, r, tile), lambda l, t: (t // per, 0, t % per))
    else:
        nt = r // tile
        per = (r // 2) // tile
        wspec = pl.BlockSpec((None, tile, c), lambda l, t: (l, t, 0))
        gspec = pl.BlockSpec((None, tile, c), lambda l, t: (t // per, t % per, 0))

    def body(w_ref, m_ref, v_ref, g0_ref, g1_ref, g_ref, d_ref, nm_ref, nv_ref):
        g = jnp.where(pl.program_id(0) == 0, g0_ref[...], g1_ref[...])
        d, nm, nv = _adamw(w_ref[...], g, m_ref[...], v_ref[...])
        g_ref[...] = g
        d_ref[...] = d
        nm_ref[...] = nm
        nv_ref[...] = nv

    sds = jax.ShapeDtypeStruct(w.shape, F32)
    return pl.pallas_call(body, out_shape=(sds,) * 4, grid=(2, nt), in_specs=[wspec, wspec, wspec, gspec, gspec],
                          out_specs=(wspec,) * 4, compiler_params=_params(("parallel", "parallel")), name=name)(w, m, v, g0, g1)


def adamw_flat(name, w, g, m, v, tr):
    rows, cols = w.shape

    def body(w_ref, g_ref, m_ref, v_ref, d_ref, nm_ref, nv_ref):
        d, nm, nv = _adamw(w_ref[...], g_ref[...], m_ref[...], v_ref[...])
        d_ref[...] = d
        nm_ref[...] = nm
        nv_ref[...] = nv

    blk = pl.BlockSpec((tr, cols), lambda i: (i, 0))
    sds = jax.ShapeDtypeStruct((rows, cols), F32)
    return pl.pallas_call(body, out_shape=(sds,) * 3, grid=(rows // tr,), in_specs=[blk] * 4, out_specs=(blk,) * 3,
                          compiler_params=_params(("parallel",)), name=name)(w, g, m, v)


def _flips(axes):
    out = []
    for fx in ((0, 1) if "x" in axes else (0,)):
        for fy in ((0, 1) if "y" in axes else (0,)):
            for fc in ((0, 1) if "c" in axes else (0,)):
                if fx or fy or fc:
                    out.append((fx, fy, fc))
    return out


def _slot(pos, axes):
    s = 0
    for name, p in zip(("x", "y", "c"), pos):
        if name in axes:
            s = 2 * s + p
    return s


def _exchange(name, arrs, axes, scatter):
    flips = _flips(axes)
    n = len(flips) + 1
    na = len(arrs)

    def body(*refs):
        ins, outs = refs[:na], refs[na:2 * na]
        send_sems, recv_sems, local_sems = refs[2 * na:]
        me = (lax.axis_index("x"), lax.axis_index("y"), lax.axis_index("c"))
        my = _slot(me, axes)
        peers = [tuple((1 - p) if f else p for p, f in zip(me, fl)) for fl in flips]

        def src(a, dest_slot):
            return ins[a].at[dest_slot] if scatter else ins[a]

        local = [pltpu.make_async_copy(src(a, my), outs[a].at[my], local_sems.at[a]) for a in range(na)]
        for cp in local:
            cp.start()

        def remote(a, j, landing_slot, dest_slot):
            return pltpu.make_async_remote_copy(
                src_ref=src(a, dest_slot), dst_ref=outs[a].at[landing_slot], send_sem=send_sems.at[a * len(flips) + j],
                recv_sem=recv_sems.at[a * len(flips) + j], device_id=peers[j], device_id_type=pl.DeviceIdType.MESH)

        sends = [remote(a, j, my, _slot(peers[j], axes)) for a in range(na) for j in range(len(flips))]
        for cp in sends:
            cp.start()
        for a in range(na):
            for j in range(len(flips)):
                remote(a, j, _slot(peers[j], axes), _slot(peers[j], axes)).wait_recv()
        for cp in sends:
            cp.wait_send()
        for cp in local:
            cp.wait()

    if scatter:
        out_shape = tuple(jax.ShapeDtypeStruct(a.shape, a.dtype) for a in arrs)
    else:
        out_shape = tuple(jax.ShapeDtypeStruct((n,) + a.shape, a.dtype) for a in arrs)
    anyspec = pl.BlockSpec(memory_space=pl.ANY)
    return pl.pallas_call(
        body, out_shape=out_shape, in_specs=[anyspec] * na, out_specs=(anyspec,) * na,
        scratch_shapes=[pltpu.SemaphoreType.DMA((na * len(flips),)), pltpu.SemaphoreType.DMA((na * len(flips),)),
                        pltpu.SemaphoreType.DMA((na,))],
        name=name)(*arrs)


def all_gather(name, arrs, axes):
    return _exchange(name, arrs, axes, False)


def all_to_all(name, arrs, axes):
    return _exchange(name, arrs, axes, True)


_HBM = pl.BlockSpec(memory_space=pltpu.HBM)
_SEM = pl.BlockSpec(memory_space=pltpu.SEMAPHORE)
_EFFECT = pltpu.SideEffectType.DATAFLOW_SIDE_EFFECTING


def place_own(name, arrs, axes, scatter):
    n = len(_flips(axes)) + 1
    na = len(arrs)

    def body(*refs):
        ins, outs, sems = refs[:na], refs[na:2 * na], refs[2 * na]
        my = _slot((lax.axis_index("x"), lax.axis_index("y"), lax.axis_index("c")), axes)
        copies = [pltpu.make_async_copy(ins[a].at[my] if scatter else ins[a], outs[a].at[my], sems.at[a]) for a in range(na)]
        for cp in copies:
            cp.start()
        for cp in copies:
            cp.wait()

    out_shape = tuple(jax.ShapeDtypeStruct(a.shape if scatter else (n,) + a.shape, a.dtype) for a in arrs)
    anyspec = pl.BlockSpec(memory_space=pl.ANY)
    return pl.pallas_call(body, out_shape=out_shape, in_specs=[anyspec] * na, out_specs=(anyspec,) * na,
                          scratch_shapes=[pltpu.SemaphoreType.DMA((na,))], name=name)(*arrs)


def _peers(axes):
    me = (lax.axis_index("x"), lax.axis_index("y"), lax.axis_index("c"))
    return me, [tuple((1 - p) if f else p for p, f in zip(me, fl)) for fl in _flips(axes)]


def place_tile(name, arr, layer, my):
    _, r, cols = arr.shape
    tr = _tile_rows(r, cols)

    def body(my_ref, x_ref, o_ref):
        o_ref[...] = x_ref[...].astype(BF16)

    grid_spec = pltpu.PrefetchScalarGridSpec(
        num_scalar_prefetch=1, grid=(r // tr,), in_specs=[pl.BlockSpec((None, tr, cols), lambda i, my: (layer, i, 0))],
        out_specs=pl.BlockSpec((None, tr, cols), lambda i, my: (my[0], i, 0)))
    return pl.pallas_call(body, out_shape=jax.ShapeDtypeStruct((4, r, cols), BF16), grid_spec=grid_spec,
                          compiler_params=_params(("parallel",)), name=name)(my, arr)


def exchange_start(name, groups, axes, scatter):
    flat = [(p if scatter else (p,)) for grp in groups for p in grp]
    per = 2 if scatter else 1
    na, ng, npeer = len(flat), len(groups), len(_flips(axes))

    def body(*refs):
        ops = refs[:per * na]
        zones = ops[(per - 1) * na:]
        sems, token = refs[per * na:per * na + 2 * ng], refs[-1]
        me, peers = _peers(axes)
        my = _slot(me, axes)
        ai = 0
        for g, grp in enumerate(groups):
            for k in range(len(grp)):
                for j, peer in enumerate(peers):
                    src = ops[ai].at[_slot(peer, axes)] if scatter else zones[ai].at[my]
                    dst = zones[ai].at[j] if scatter else zones[ai].at[my]
                    pltpu.make_async_remote_copy(
                        src_ref=src, dst_ref=dst, send_sem=sems[2 * g].at[k * npeer + j],
                        recv_sem=sems[2 * g + 1].at[k * npeer + j], device_id=peer, device_id_type=pl.DeviceIdType.MESH).start()
                ai += 1
        token[...] = jnp.zeros_like(token)

    out_shape, out_specs = [], []
    for grp in groups:
        out_shape += [pltpu.SemaphoreType.DMA((npeer * len(grp),))] * 2
        out_specs += [_SEM, _SEM]
    for idx in range(per):
        out_shape += [pltpu.HBM(p[idx].shape, p[idx].dtype) for p in flat]
        out_specs += [_HBM] * na
    out_shape.append(jax.ShapeDtypeStruct((SUBLANES, 128), F32))
    out_specs.append(pl.BlockSpec(memory_space=pltpu.VMEM))
    args = [pltpu.with_memory_space_constraint(p[idx], pltpu.HBM) for idx in range(per) for p in flat]
    res = pl.pallas_call(body, out_shape=tuple(out_shape), in_specs=[_HBM] * (per * na), out_specs=tuple(out_specs),
                         input_output_aliases={i: 2 * ng + i for i in range(per * na)},
                         compiler_params=pltpu.CompilerParams(has_side_effects=_EFFECT), name=name)(*args)
    thru = res[2 * ng:2 * ng + per * na]
    out, ai = [], 0
    for g, grp in enumerate(groups):
        srcs = list(thru[ai:ai + len(grp)]) if scatter else []
        zones = list(thru[(per - 1) * na + ai:(per - 1) * na + ai + len(grp)])
        out.append(((res[2 * g], res[2 * g + 1]), srcs, zones))
        ai += len(grp)
    return out, res[-1]


def exchange_wait(name, group, after, axes, scatter):
    (send_sems, recv_sems), srcs, zones = group
    n, ns = len(zones), len(srcs)
    npeer = len(_flips(axes))

    def body(*refs):
        z_refs = refs[ns:ns + n]
        ssem, rsem = refs[ns + n], refs[ns + n + 1]
        _, peers = _peers(axes)
        for k in range(n):
            for j, peer in enumerate(peers):
                part = z_refs[k].at[j if scatter else _slot(peer, axes)]
                copy = pltpu.make_async_remote_copy(
                    src_ref=part, dst_ref=part, send_sem=ssem.at[k * npeer + j], recv_sem=rsem.at[k * npeer + j],
                    device_id=peer, device_id_type=pl.DeviceIdType.MESH)
                copy.wait_send()
                copy.wait_recv()

    ops = list(srcs) + list(zones)
    out_shape = tuple(pltpu.HBM(a.shape, a.dtype) for a in ops)
    res = pl.pallas_call(body, out_shape=out_shape, in_specs=[_HBM] * len(ops) + [_SEM, _SEM, pl.BlockSpec(memory_space=pl.ANY)],
                         out_specs=(_HBM,) * len(ops), input_output_aliases={i: i for i in range(len(ops))},
                         compiler_params=pltpu.CompilerParams(has_side_effects=_EFFECT), name=name)(*ops, send_sems, recv_sems, after)
    return list(res[:ns]), list(res[ns:])


def _pair_exchange(name, ins, in_specs, n_steps, tile, fn_send, fn_out, out_shape, out_spec, prefetch=None):
    n_in = len(ins)

    def body(*refs):
        if prefetch is not None:
            refs = refs[1:]
        in_refs, o_ref = refs[:n_in], refs[n_in]
        send_buf, recv_buf, send_sems, recv_sems, credit = refs[n_in + 1:]
        i = pl.program_id(0)
        slot = lax.rem(i, 2)
        c = lax.axis_index("c")
        sibling = (lax.axis_index("x"), lax.axis_index("y"), 1 - c)
        vals = [r[...] for r in in_refs]
        send_buf[slot] = fn_send(*vals, c)

        @pl.when(i >= 2)
        def _():
            pl.semaphore_wait(credit, 1)

        copy = pltpu.make_async_remote_copy(
            src_ref=send_buf.at[slot], dst_ref=recv_buf.at[slot], send_sem=send_sems.at[slot], recv_sem=recv_sems.at[slot],
            device_id=sibling, device_id_type=pl.DeviceIdType.MESH)
        copy.start()
        copy.wait_recv()
        o_ref[...] = fn_out(*vals, recv_buf[slot], c).astype(o_ref.dtype)
        copy.wait_send()

        @pl.when(i < n_steps - 2)
        def _():
            pl.semaphore_signal(credit, inc=1, device_id=sibling, device_id_type=pl.DeviceIdType.MESH)

    scratch = [pltpu.VMEM((2,) + tile, F32), pltpu.VMEM((2,) + tile, F32), pltpu.SemaphoreType.DMA((2,)),
               pltpu.SemaphoreType.DMA((2,)), pltpu.SemaphoreType.REGULAR]
    if prefetch is None:
        return pl.pallas_call(body, out_shape=out_shape, grid=(n_steps,), in_specs=in_specs, out_specs=out_spec,
                              scratch_shapes=scratch, compiler_params=_params(("arbitrary",)), name=name)(*ins)
    grid_spec = pltpu.PrefetchScalarGridSpec(num_scalar_prefetch=1, grid=(n_steps,), in_specs=in_specs, out_specs=out_spec,
                                             scratch_shapes=scratch)
    return pl.pallas_call(body, out_shape=out_shape, grid_spec=grid_spec, compiler_params=_params(("arbitrary",)),
                          name=name)(prefetch, *ins)


def _tile_rows(rows, cols):
    return _row_tile(rows, max(SUBLANES, (3 << 19) // (4 * cols)))


def reduce_cores(name, g):
    _, m, cols = g.shape
    tr = _tile_rows(m, cols)

    def fn_send(g0, g1, c):
        return jnp.where(c == 0, g1, g0)

    def fn_out(g0, g1, got, c):
        return jnp.where(c == 0, g0, g1) + got

    return _pair_exchange(
        name, [g, g], [pl.BlockSpec((None, tr, cols), lambda i: (0, i, 0)), pl.BlockSpec((None, tr, cols), lambda i: (1, i, 0))],
        m // tr, (tr, cols), fn_send, fn_out, jax.ShapeDtypeStruct((m, cols), BF16), pl.BlockSpec((tr, cols), lambda i: (i, 0)))


def sum_and_share(name, own, parts, my):
    n, r, cols = parts.shape
    tr = _tile_rows(r, cols)

    def total(o, p):
        acc = o.astype(F32)
        for j in range(n):
            acc = acc + p[j].astype(F32)
        return acc

    def fn_send(o, p, c):
        return total(o, p)

    def fn_out(o, p, got, c):
        mine = total(o, p)
        return jnp.stack([jnp.where(c == 0, mine, got), jnp.where(c == 0, got, mine)])

    return _pair_exchange(
        name, [own, parts], [pl.BlockSpec((None, tr, cols), lambda i, my_ref: (my_ref[0], i, 0)), pl.BlockSpec((n, tr, cols), lambda i, my_ref: (0, i, 0))],
        r // tr, (tr, cols), fn_send, fn_out, jax.ShapeDtypeStruct((2, r, cols), F32),
        pl.BlockSpec((2, tr, cols), lambda i, my_ref: (0, i, 0)), prefetch=my)


def _block_diag(blocks):
    g, r, c = blocks.shape
    eye = jnp.eye(g, dtype=blocks.dtype)
    return (blocks[:, :, None, :] * eye[:, None, :, None]).reshape(g * r, g * c)


def _diag_blocks(mat, g):
    r, c = mat.shape[0] // g, mat.shape[1] // g
    eye = jnp.eye(g, dtype=mat.dtype)
    return (mat.reshape(g, r, g, c) * eye[:, None, :, None]).sum(axis=2)


def _halves(gfull, shards):
    rows, cols = gfull.shape
    return gfull.reshape(shards, 2, rows // shards // 2, cols).transpose(1, 0, 2, 3)


def _step(inp):
    x = inp['x'][0]
    target = inp['loss_target'][0]
    rows, d = x.shape
    depth = inp['w_in'].shape[0]
    mix_w = d // 4
    n_state = S5_GROUPS * S5_STATE
    ffn_half = inp['ffn_w_up'].shape[2]
    tm = min(512, rows)
    tc = min(256, rows)
    tl = min(512, rows)
    xy = ("x", "y")

    my_chip = (2 * lax.axis_index("x") + lax.axis_index("y")).astype(jnp.int32).reshape(1)
    zones = {}
    for l in range(depth):
        for nme in BIG:
            if nme == 'ffn_w_up':
                zones[(nme, l)] = place_tile(f"place_{nme}{l}", inp[nme][l].astype(BF16)[None], 0, my_chip)
            else:
                zones[(nme, l)] = place_tile(f"place_{nme}{l}", inp[nme], l, my_chip)
    small_keys = [(nme, None) for nme in SMALL_SHARDED]
    zones.update(zip(small_keys, place_own("place_small", [inp[nme] for nme in SMALL_SHARDED], xy, False)))
    group_keys = []
    for l in range(depth):
        group_keys += [[('w_in', l)] + (small_keys if l == 0 else []),
                       [('w_out', l), ('s5_w_glu', l), ('cv_w_pw', l)], [('ffn_w_up', l)], [('ffn_w_down', l)]]
    gather_groups, gather_token = exchange_start("gather_start", [[zones[key] for key in grp] for grp in group_keys], xy, False)

    def gathered(gi, after):
        return dict(zip(group_keys[gi], exchange_wait(f"gather_wait{gi}", gather_groups[gi], after, xy, False)[1]))

    def full_small(g):
        return g.transpose(1, 2, 0, 3).reshape(g.shape[1], g.shape[2], 4 * g.shape[3])

    gsum = jnp.repeat(jnp.eye(128, dtype=F32)[:S5_GROUPS], S5_STATE, axis=0)

    saved = []
    grads = {nme: [None] * depth for nme in WEIGHTS}
    xcur = x
    for l in range(depth):
        vec = lambda a: a[l].reshape(1, -1)
        gain = vec(inp['norm_mix_g']) + (gather_token[0, 0] if l == 0 else 0.0)
        h = rms_fwd(f"rms_mix{l}", xcur, gain, tm)
        got = gathered(4 * l, h)
        w_in = got[('w_in', l)]
        if l == 0:
            cv_w_dw, lru_w_conv, ffn_w_dw = (full_small(got[(nme, None)]) for nme in ('cv_w_dw', 'lru_w_conv', 'ffn_w_dw'))
        ncol = w_in.shape[2]

        lam_re, lam_im = vec(inp['s5_lam_re']), vec(inp['s5_lam_im'])
        log_step = jnp.broadcast_to(inp['s5_log_step'][l][:, None], (S5_GROUPS, S5_STATE)).reshape(1, n_state)
        b_re = _block_diag(inp['s5_b_re'][l].transpose(0, 2, 1))
        b_im = _block_diag(inp['s5_b_im'][l].transpose(0, 2, 1))
        c_cat = jnp.stack([_block_diag(inp['s5_c_re'][l].transpose(0, 2, 1)),
                           -_block_diag(inp['s5_c_im'][l].transpose(0, 2, 1))]).astype(BF16)
        a_bar, b_bar = s5_param_fwd(f"s5_param_fwd{l}", lam_re, lam_im, log_step, b_re, b_im)
        w_r = _block_diag(inp['lru_w_r'][l]).astype(BF16)
        w_i = _block_diag(inp['lru_w_i'][l]).astype(BF16)
        pool_bd = _block_diag(inp['pool_w'][l]).astype(BF16)
        gate_pars = [w_r, w_i, vec(inp['lru_b_r']), vec(inp['lru_b_i']), vec(inp['lru_lam'])]

        proj = _mm(f"proj{l}", h, w_in, jax.ShapeDtypeStruct((rows, 4 * ncol), F32), (rows // tm, 4),
                   pl.BlockSpec((tm, d), lambda i, j: (i, 0)), pl.BlockSpec((None, d, ncol), lambda i, j: (j, 0, 0)),
                   pl.BlockSpec((tm, ncol), lambda i, j: (i, j)), NN)
        proj3 = proj.reshape(1, rows, 4 * ncol)
        nh = n_state // 2
        bu = _mm(f"s5_bu{l}", proj, b_bar, jax.ShapeDtypeStruct((2, rows, n_state), F32), (rows // tm, 2, 2),
                 pl.BlockSpec((tm, mix_w), lambda i, c, n: (i, 0)), pl.BlockSpec((None, mix_w, nh), lambda i, c, n: (c, 0, n)),
                 pl.BlockSpec((None, tm, nh), lambda i, c, n: (c, i, n)), NN)
        z = s5_scan_fwd(f"s5_scan{l}", bu, a_bar)
        y_ssm = _mm(f"s5_read{l}", z, c_cat, jax.ShapeDtypeStruct((rows, mix_w), F32), (rows // tm, 4),
                    pl.BlockSpec((None, tm, nh), lambda i, k: (k // 2, i, k % 2)),
                    pl.BlockSpec((None, nh, mix_w), lambda i, k: (k // 2, k % 2, 0)),
                    pl.BlockSpec((tm, mix_w), lambda i, k: (i, 0)), NN, k_axis=1)
        (h0,) = _rowwise(f"cv_glu{l}", _glu, [(proj, 1, mix_w), (proj, 2, mix_w)], [], 1, [(mix_w, F32)], tm)
        h1 = dwconv_fwd(f"cv_conv{l}", h0.reshape(1, rows, mix_w), 0, mix_w, cv_w_dw[l][None], vec(inp['cv_b_dw'])[None],
                        CV_TAPS, tc)[0]
        xc = dwconv_fwd(f"lru_conv{l}", proj3, 3, mix_w, lru_w_conv[l][None], vec(inp['lru_b_conv'])[None], LRU_TAPS, tc)[0]
        a_t, b_t = _rowwise(f"lru_gate{l}", _lru_gate, [(xc, 0, mix_w)], gate_pars, 2, [(mix_w, F32), (mix_w, F32)], tm)
        hseq = lru_scan_fwd(f"lru_scan{l}", a_t, b_t)
        dgp = pool_fwd(f"pool{l}", proj, 5, tc)
        got = gathered(4 * l + 1, proj)
        w_out = got[('w_out', l)].reshape(d, d)
        w_glu, w_pw = got[('s5_w_glu', l)].reshape(mix_w, mix_w), got[('cv_w_pw', l)].reshape(mix_w, mix_w)
        post_pars = [vec(inp['s5_d']), w_glu, vec(inp['s5_b_glu']), vec(inp['cv_ln_g']), vec(inp['cv_ln_b']), w_pw,
                     vec(inp['cv_b_pw']), pool_bd, vec(inp['pool_scale'])]
        post_rows = [(y_ssm, 0, mix_w), (proj, 0, mix_w), (h1, 0, mix_w), (hseq, 0, mix_w), (proj, 4, mix_w), (dgp, 0, mix_w)]
        (mixed,) = _rowwise(f"mix_post{l}", _mix_post, post_rows, post_pars, 1, [(d, BF16)], tm)
        x1 = _mm(f"out_proj{l}", mixed, w_out, jax.ShapeDtypeStruct((rows, d), F32), (rows // tm, 2),
                 pl.BlockSpec((tm, d), lambda i, j: (i, 0)), pl.BlockSpec((d, d // 2), lambda i, j: (0, j)),
                 pl.BlockSpec((tm, d // 2), lambda i, j: (i, j)), NN,
                 add=xcur, add_spec=pl.BlockSpec((tm, d // 2), lambda i, j: (i, j)))

        h2 = rms_fwd(f"rms_ffn{l}", x1, vec(inp['norm_ffn_g']), tm)
        tu = min(256, rows)
        w_up = gathered(4 * l + 2, x1)[('ffn_w_up', l)]
        up = _mm(f"ffn_up{l}", h2, w_up, jax.ShapeDtypeStruct((4, rows, ffn_half), F32), (4, rows // tu),
                 pl.BlockSpec((tu, d), lambda k, i: (i, 0)), pl.BlockSpec((None, d, ffn_half), lambda k, i: (k, 0, 0)),
                 pl.BlockSpec((None, tu, ffn_half), lambda k, i: (k, i, 0)), NN)
        w_dw = ffn_w_dw[l].reshape(FFN_TAPS, 2, ffn_half).transpose(1, 0, 2)
        b_dw = inp['ffn_b_dw'][l].reshape(2, 1, ffn_half)
        gc = dwconv_fwd(f"ffn_conv{l}", up, 0, ffn_half, w_dw, b_dw, FFN_TAPS, tc)
        act = ffn_act_fwd(f"ffn_act{l}", gc, up, tc)
        w_down = gathered(4 * l + 3, up)[('ffn_w_down', l)].reshape(2, ffn_half, d)
        x2 = _mm(f"ffn_down{l}", act, w_down, jax.ShapeDtypeStruct((rows, d), F32), (rows // tm, 4),
                 pl.BlockSpec((2, tm, ffn_half), lambda i, j: (0, i, 0)), pl.BlockSpec((2, ffn_half, d // 4), lambda i, j: (0, 0, j)),
                 pl.BlockSpec((tm, d // 4), lambda i, j: (i, j)), NN, inner=("lead", 2),
                 add=x1, add_spec=pl.BlockSpec((tm, d // 4), lambda i, j: (i, j)))
        saved.append(dict(x=xcur, h=h, proj=proj, z=z, y_ssm=y_ssm, h0=h0, h1=h1, xc=xc, a_t=a_t, hseq=hseq, dgp=dgp,
                          mixed=mixed, x1=x1, h2=h2, up=up, gc=gc, act=act, w_in=w_in, w_out=w_out, w_up=w_up, w_down=w_down,
                          a_bar=a_bar, b_bar=b_bar, c_cat=c_cat, post_pars=post_pars, gate_pars=gate_pars, w_dw=w_dw,
                          s5=(lam_re, lam_im, log_step, b_re, b_im), cv_w=cv_w_dw[l][None], lru_w=lru_w_conv[l][None]))
        xcur = x2

    loss_row, dx, dg_final = final_loss("final_loss", xcur, inp['norm_final_g'].reshape(1, d), target, tm)
    grads['norm_final_g'] = dg_final.reshape(d)

    big_g = {nme: [None] * depth for nme in BIG}
    reduce_groups = []

    def start_reduce(tag, keys):
        pieces = []
        for nme, lyr in keys:
            g = big_g[nme][lyr]
            if nme == 'ffn_w_down':
                g = g.reshape(2, 4, ffn_half // 2, d // 2)
            pieces.append(reduce_cores(f"reduce_cores_{nme}{lyr}", g.reshape(2, -1, g.shape[-1])).reshape(g.shape[1:]))
        landing = [lax.empty((3,) + p.shape[1:], p.dtype) for p in pieces]
        groups, token = exchange_start(f"reduce_start_{tag}", [list(zip(pieces, landing))], xy, True)
        reduce_groups.append((tag, keys, groups[0]))
        return token

    for l in reversed(range(depth)):
        s = saved[l]
        ncol = s['w_in'].shape[2]
        nh = n_state // 2
        tu = min(256, rows)
        dact = _mm(f"d_act{l}", dx, s['w_down'], jax.ShapeDtypeStruct((2, rows, ffn_half), F32), (2, rows // tu),
                   pl.BlockSpec((tu, d), lambda k, i: (i, 0)), pl.BlockSpec((None, ffn_half, d), lambda k, i: (k, 0, 0)),
                   pl.BlockSpec((None, tu, ffn_half), lambda k, i: (k, i, 0)), NT)
        tn = d // 4
        tk = min(1024, rows)
        big_g['ffn_w_down'][l] = _mm(
            f"dw_down{l}", s['act'], dx, jax.ShapeDtypeStruct((2, 2, ffn_half, d // 2), F32), (2, 4, rows // tk),
            pl.BlockSpec((None, tk, ffn_half), lambda hh, n, k: (hh, k, 0)), pl.BlockSpec((tk, tn), lambda hh, n, k: (k, n)),
            pl.BlockSpec((None, None, ffn_half, tn), lambda hh, n, k: (n // 2, hh, 0, n % 2)), TN, k_axis=2)
        dgc, dval = ffn_act_bwd(f"ffn_act_bwd{l}", s['gc'], s['up'], dact, tc)
        dgate, dw_dw, db_dw = dwconv_bwd(f"ffn_conv_bwd{l}", dgc, s['up'], 0, ffn_half, s['w_dw'], FFN_TAPS, tc, dx_dtype=BF16)
        grads['ffn_w_dw'][l] = dw_dw.transpose(1, 0, 2).reshape(FFN_TAPS, 2 * ffn_half)
        grads['ffn_b_dw'][l] = db_dw.reshape(2 * ffn_half)
        dup = jnp.concatenate([dgate, dval], axis=0)
        tm2 = min(1024, rows)
        dh2 = _mm(f"d_h2{l}", dup, s['w_up'], jax.ShapeDtypeStruct((rows, d), F32), (rows // tm2, 2, 4),
                  pl.BlockSpec((None, tm2, ffn_half), lambda i, j, k: (k, i, 0)), pl.BlockSpec((None, d // 2, ffn_half), lambda i, j, k: (k, j, 0)),
                  pl.BlockSpec((tm2, d // 2), lambda i, j, k: (i, j)), NT, k_axis=2)
        tmm = d // 4
        big_g['ffn_w_up'][l] = _mm(
            f"dw_up{l}", dup, s['h2'], jax.ShapeDtypeStruct((2, 4, ffn_half, d // 2), F32), (4, 4, rows // tk),
            pl.BlockSpec((None, tk, ffn_half), lambda k4, n, k: (k4, k, 0)), pl.BlockSpec((tk, tn), lambda k4, n, k: (k, n)),
            pl.BlockSpec((None, None, ffn_half, tn), lambda k4, n, k: (n // 2, k4, 0, n % 2)), TN, k_axis=2)
        token = start_reduce(f"ffn{l}", [('ffn_w_down', l), ('ffn_w_up', l)])
        dx1, dg = rms_bwd(f"rms_ffn_bwd{l}", s['x1'], inp['norm_ffn_g'][l].reshape(1, d) + token[0, 0], dh2, dx, tm)
        grads['norm_ffn_g'][l] = dg.reshape(d)
        dmixed = _mm(f"d_mixed{l}", dx1, s['w_out'], jax.ShapeDtypeStruct((rows, d), F32), (rows // tm, 4),
                     pl.BlockSpec((tm, d), lambda i, j: (i, 0)), pl.BlockSpec((d // 4, d), lambda i, j: (j, 0)),
                     pl.BlockSpec((tm, d // 4), lambda i, j: (i, j)), NT)
        tq = mix_w // 2
        big_g['w_out'][l] = _mm(
            f"dw_out{l}", s['mixed'], dx1, jax.ShapeDtypeStruct((2, 4, tq, d), F32), (8, rows // tk),
            pl.BlockSpec((tk, tq), lambda t, k: (k, t)), pl.BlockSpec((tk, d), lambda t, k: (k, 0)),
            pl.BlockSpec((None, None, tq, d), lambda t, k: (t % 2, t // 2, 0, 0)), TN, k_axis=1)
        post_rows = [(s['y_ssm'], 0, mix_w), (s['proj'], 0, mix_w), (s['h1'], 0, mix_w), (s['hseq'], 0, mix_w),
                     (s['proj'], 4, mix_w), (s['dgp'], 0, mix_w), (dmixed, 0, d)]
        res = _rowwise(f"mix_post_bwd{l}", _mix_post, post_rows, s['post_pars'], 1, [(mix_w, F32)] * 6, tm, with_grads=True)
        dy_ssm, du_dir, dh1, dhseq, dlru_g, ddgp = res[:6]
        dd, dwglu, dbglu, dlng, dlnb, dwpw, dbpw, dpoolbd, dscale = res[6:]
        grads['s5_d'][l], grads['s5_b_glu'][l] = dd.reshape(mix_w), dbglu.reshape(mix_w)
        grads['cv_ln_g'][l], grads['cv_ln_b'][l], grads['cv_b_pw'][l] = dlng.reshape(mix_w), dlnb.reshape(mix_w), dbpw.reshape(mix_w)
        grads['pool_w'][l] = _diag_blocks(dpoolbd, len(POOL_WINDOWS))
        grads['pool_scale'][l] = dscale.reshape(mix_w)
        big_g['s5_w_glu'][l] = _halves(dwglu, 4)
        big_g['cv_w_pw'][l] = _halves(dwpw, 4)
        dz = _mm(f"s5_dz{l}", dy_ssm, s['c_cat'], jax.ShapeDtypeStruct((2, rows, n_state), F32), (rows // tm, 2, 2),
                 pl.BlockSpec((tm, mix_w), lambda i, c, n: (i, 0)), pl.BlockSpec((None, nh, mix_w), lambda i, c, n: (c, n, 0)),
                 pl.BlockSpec((None, tm, nh), lambda i, c, n: (c, i, n)), NT)
        dccat = _mm(f"s5_dc{l}", s['z'], dy_ssm, jax.ShapeDtypeStruct((2, n_state, mix_w), F32), (2, n_state // mix_w, rows // tl),
                    pl.BlockSpec((None, tl, mix_w), lambda c, m, k: (c, k, m)), pl.BlockSpec((tl, mix_w), lambda c, m, k: (k, 0)),
                    pl.BlockSpec((None, mix_w, mix_w), lambda c, m, k: (c, m, 0)), TN, k_axis=2)
        grads['s5_c_re'][l] = _diag_blocks(dccat[0], S5_GROUPS).transpose(0, 2, 1)
        grads['s5_c_im'][l] = -_diag_blocks(dccat[1], S5_GROUPS).transpose(0, 2, 1)
        lam, da_bar = s5_scan_bwd(f"s5_scan_bwd{l}", dz, s['z'], s['a_bar'])
        du = _mm(f"s5_du{l}", lam, s['b_bar'], jax.ShapeDtypeStruct((rows, mix_w), F32), (rows // tm, 4),
                 pl.BlockSpec((None, tm, nh), lambda i, k: (k // 2, i, k % 2)), pl.BlockSpec((None, mix_w, nh), lambda i, k: (k // 2, 0, k % 2)),
                 pl.BlockSpec((tm, mix_w), lambda i, k: (i, 0)), NT, k_axis=1,
                 add=du_dir, add_spec=pl.BlockSpec((tm, mix_w), lambda i, k: (i, 0)))
        dbbar = _mm(f"s5_db{l}", s['proj'], lam, jax.ShapeDtypeStruct((2, mix_w, n_state), F32), (2, 2, rows // tl),
                    pl.BlockSpec((tl, mix_w), lambda c, n, k: (k, 0)), pl.BlockSpec((None, tl, nh), lambda c, n, k: (c, k, n)),
                    pl.BlockSpec((None, mix_w, nh), lambda c, n, k: (c, 0, n)), TN, k_axis=2)
        dlr, dli, dls, dbre, dbim = s5_param_bwd(f"s5_param_bwd{l}", *s['s5'], da_bar, dbbar, gsum)
        grads['s5_lam_re'][l] = dlr.reshape(S5_GROUPS, S5_STATE)
        grads['s5_lam_im'][l] = dli.reshape(S5_GROUPS, S5_STATE)
        grads['s5_log_step'][l] = dls[0, :S5_GROUPS]
        grads['s5_b_re'][l] = _diag_blocks(dbre, S5_GROUPS).transpose(0, 2, 1)
        grads['s5_b_im'][l] = _diag_blocks(dbim, S5_GROUPS).transpose(0, 2, 1)
        dh0, dw_cv, db_cv = dwconv_bwd(f"cv_conv_bwd{l}", dh1.reshape(1, rows, mix_w), s['h0'].reshape(1, rows, mix_w), 0, mix_w,
                                       s['cv_w'], CV_TAPS, tc)
        grads['cv_w_dw'][l], grads['cv_b_dw'][l] = dw_cv[0], db_cv.reshape(mix_w)
        dv, dgg = _rowwise(f"cv_glu_bwd{l}", _glu, [(s['proj'], 1, mix_w), (s['proj'], 2, mix_w), (dh0[0], 0, mix_w)], [], 1,
                           [(mix_w, F32)] * 2, tm, with_grads=True)
        da_t, db_t = lru_scan_bwd(f"lru_scan_bwd{l}", dhseq, s['a_t'], s['hseq'])
        res = _rowwise(f"lru_gate_bwd{l}", _lru_gate, [(s['xc'], 0, mix_w), (da_t, 0, mix_w), (db_t, 0, mix_w)], s['gate_pars'], 2,
                       [(mix_w, F32)], tm, with_grads=True)
        dxc, dwr, dwi, dbr, dbi, dlam = res
        grads['lru_w_r'][l], grads['lru_w_i'][l] = _diag_blocks(dwr, LRU_HEADS), _diag_blocks(dwi, LRU_HEADS)
        grads['lru_b_r'][l], grads['lru_b_i'][l], grads['lru_lam'][l] = dbr.reshape(mix_w), dbi.reshape(mix_w), dlam.reshape(mix_w)
        dlx, dw_lc, db_lc = dwconv_bwd(f"lru_conv_bwd{l}", dxc.reshape(1, rows, mix_w), s['proj'].reshape(1, rows, 4 * ncol), 3, mix_w,
                                       s['lru_w'], LRU_TAPS, tc)
        grads['lru_w_conv'][l], grads['lru_b_conv'][l] = dw_lc[0], db_lc.reshape(mix_w)
        dpx = pool_bwd(f"pool_bwd{l}", ddgp, tc)
        dproj = jnp.concatenate([du, dv, dgg, dlx[0], dlru_g, dpx], axis=-1)
        dh = _mm(f"d_h{l}", dproj, s['w_in'], jax.ShapeDtypeStruct((rows, d), F32), (rows // tm, 4),
                 pl.BlockSpec((tm, 4 * ncol), lambda i, j: (i, 0)), pl.BlockSpec((4, d // 4, ncol), lambda i, j: (0, j, 0)),
                 pl.BlockSpec((tm, d // 4), lambda i, j: (i, j)), NT, inner=("cols", 4))
        tk2 = min(2048, rows)
        big_g['w_in'][l] = _mm(
            f"dw_in{l}", s['h'], dproj, jax.ShapeDtypeStruct((2, 4, d // 2, ncol), F32), (4, 4, rows // tk2),
            pl.BlockSpec((tk2, tmm), lambda k4, m, k: (k, m)), pl.BlockSpec((tk2, ncol), lambda k4, m, k: (k, k4)),
            pl.BlockSpec((None, None, tmm, ncol), lambda k4, m, k: (m // 2, k4, m % 2, 0)), TN, k_axis=2)
        token = start_reduce(f"mix{l}", [('w_out', l), ('s5_w_glu', l), ('cv_w_pw', l), ('w_in', l)])
        dx, dg = rms_bwd(f"rms_mix_bwd{l}", s['x'], inp['norm_mix_g'][l].reshape(1, d) + token[0, 0], dh, dx1, tm)
        grads['norm_mix_g'][l] = dg.reshape(d)

    t_full = {}
    for tag, keys, group in reduce_groups:
        pieces, parts = exchange_wait(f"reduce_wait_{tag}", group, dx, xy, True)
        for key, own, got in zip(keys, pieces, parts):
            t_full[key] = sum_and_share(f"share_cores_{key[0]}{key[1]}", own, got, my_chip)

    outs = {}
    tiles = {'w_in': 256, 'w_out': 128, 'ffn_w_up': 128, 'ffn_w_down': 256, 's5_w_glu': 64, 'cv_w_pw': 64}
    for nme in BIG:
        g0, g1 = t_full[(nme, 0)], t_full[(nme, 1)]
        if nme == 'ffn_w_up':
            res = adamw_sharded(f"adamw_{nme}", *(jnp.swapaxes(inp[p + nme], 1, 2) for p in ('', 'm_', 'v_')), g0, g1, True, tiles[nme])
            outs[nme] = tuple(jnp.swapaxes(r, 1, 2) for r in res)
        else:
            outs[nme] = adamw_sharded(f"adamw_{nme}", inp[nme], inp['m_' + nme], inp['v_' + nme], g0, g1, nme == 'ffn_w_down', tiles[nme])

    small = [nme for nme in WEIGHTS if nme not in BIG]
    full_g = {nme: (grads[nme] if nme == 'norm_final_g' else jnp.stack(grads[nme])) for nme in small}
    flat = jnp.concatenate([full_g[nme].reshape(-1) for nme in small])
    n_flat = flat.shape[0]
    pad = (-n_flat) % (128 * 64)
    packed = jnp.pad(flat, (0, pad)).reshape(-1, 128)
    (g8,) = all_gather("gather_small", [packed], ("x", "y", "c"))
    gsum_small = sum_lead("sum_small", g8, 64).reshape(-1)
    red, off = {}, 0
    for nme in small:
        g = gsum_small[off:off + full_g[nme].size].reshape(full_g[nme].shape)
        off += full_g[nme].size
        if nme in SMALL_SHARDED:
            width = inp[nme].shape[2]
            g = lax.dynamic_slice_in_dim(g, my_chip[0] * width, width, axis=2)
        red[nme] = g

    def pack(tree):
        f = jnp.concatenate([tree[nme].reshape(-1) for nme in small])
        return jnp.pad(f, (0, (-f.shape[0]) % (128 * 64))).reshape(-1, 128)

    pd, pm, pv = adamw_flat("adamw_small", pack({n_: inp[n_] for n_ in small}), pack(red), pack({n_: inp['m_' + n_] for n_ in small}),
                            pack({n_: inp['v_' + n_] for n_ in small}), 64)
    off = 0
    for nme in small:
        size, shape = inp[nme].size, inp[nme].shape
        outs[nme] = (red[nme],) + tuple(p.reshape(-1)[off:off + size].reshape(shape) for p in (pd, pm, pv))
        off += size

    loss = lax.psum(loss_row[0, 0], ("x", "y", "c"))
    result = [loss, dx[None]]
    for part in range(4):
        result += [outs[nme][part] for nme in WEIGHTS]
    return tuple(result)


def kernel(x, norm_mix_g, w_in, s5_lam_re, s5_lam_im, s5_log_step, s5_b_re, s5_b_im, s5_c_re, s5_c_im, s5_d, s5_w_glu, s5_b_glu, cv_w_dw, cv_b_dw, cv_ln_g, cv_ln_b, cv_w_pw, cv_b_pw, lru_w_conv, lru_b_conv, lru_w_r, lru_b_r, lru_w_i, lru_b_i, lru_lam, pool_w, pool_scale, w_out, norm_ffn_g, ffn_w_up, ffn_w_dw, ffn_b_dw, ffn_w_down, norm_final_g, loss_target, m_norm_mix_g, m_w_in, m_s5_lam_re, m_s5_lam_im, m_s5_log_step, m_s5_b_re, m_s5_b_im, m_s5_c_re, m_s5_c_im, m_s5_d, m_s5_w_glu, m_s5_b_glu, m_cv_w_dw, m_cv_b_dw, m_cv_ln_g, m_cv_ln_b, m_cv_w_pw, m_cv_b_pw, m_lru_w_conv, m_lru_b_conv, m_lru_w_r, m_lru_b_r, m_lru_w_i, m_lru_b_i, m_lru_lam, m_pool_w, m_pool_scale, m_w_out, m_norm_ffn_g, m_ffn_w_up, m_ffn_w_dw, m_ffn_b_dw, m_ffn_w_down, m_norm_final_g, v_norm_mix_g, v_w_in, v_s5_lam_re, v_s5_lam_im, v_s5_log_step, v_s5_b_re, v_s5_b_im, v_s5_c_re, v_s5_c_im, v_s5_d, v_s5_w_glu, v_s5_b_glu, v_cv_w_dw, v_cv_b_dw, v_cv_ln_g, v_cv_ln_b, v_cv_w_pw, v_cv_b_pw, v_lru_w_conv, v_lru_b_conv, v_lru_w_r, v_lru_b_r, v_lru_w_i, v_lru_b_i, v_lru_lam, v_pool_w, v_pool_scale, v_w_out, v_norm_ffn_g, v_ffn_w_up, v_ffn_w_dw, v_ffn_b_dw, v_ffn_w_down, v_norm_final_g):
    inp = dict(locals())
    return _step(inp)
```

```python
import functools

import jax
import jax.numpy as jnp
from jax import lax
from jax.experimental import pallas as pl
from jax.experimental.pallas import tpu as pltpu

F32 = jnp.float32
BF16 = jnp.bfloat16

VMEM_LIMIT_BYTES = 56 * 1024 * 1024
SUBLANES = 8

EPS = 1e-6
S5_GROUPS, S5_STATE, S5_GROUP_CH = 32, 64, 16
LRU_HEADS, LRU_C = 8, 8.0
POOL_WINDOWS = (2, 4, 8, 16)
CV_TAPS, LRU_TAPS, FFN_TAPS = 31, 4, 3
SCAN_CHUNK = 64

ADAM_LR, ADAM_B1, ADAM_B2, ADAM_EPS, ADAM_WD, ADAM_STEP = 0.001, 0.9, 0.999, 1e-08, 0.01, 10

NN = ((1,), (0,))
NT = ((1,), (1,))
TN = ((0,), (0,))

WEIGHTS = ['norm_mix_g', 'w_in', 's5_lam_re', 's5_lam_im', 's5_log_step', 's5_b_re', 's5_b_im', 's5_c_re', 's5_c_im',
           's5_d', 's5_w_glu', 's5_b_glu', 'cv_w_dw', 'cv_b_dw', 'cv_ln_g', 'cv_ln_b', 'cv_w_pw', 'cv_b_pw',
           'lru_w_conv', 'lru_b_conv', 'lru_w_r', 'lru_b_r', 'lru_w_i', 'lru_b_i', 'lru_lam', 'pool_w', 'pool_scale',
           'w_out', 'norm_ffn_g', 'ffn_w_up', 'ffn_w_dw', 'ffn_b_dw', 'ffn_w_down', 'norm_final_g']
BIG = ('w_in', 'w_out', 'ffn_w_up', 'ffn_w_down', 's5_w_glu', 'cv_w_pw')
SMALL_SHARDED = {'cv_w_dw': 2, 'lru_w_conv': 2, 'ffn_w_dw': 2}


def _params(sem=None):
    if sem is None:
        return pltpu.CompilerParams(vmem_limit_bytes=VMEM_LIMIT_BYTES)
    return pltpu.CompilerParams(dimension_semantics=sem, vmem_limit_bytes=VMEM_LIMIT_BYTES)


def _row_tile(rows, cap, mult=SUBLANES):
    best = mult
    for t in range(mult, min(rows, cap) + 1, mult):
        if rows % t == 0:
            best = t
    return best


def _bdot(a, b, dims=NN):
    return lax.dot_general(a.astype(BF16), b.astype(BF16), (dims, ((), ())), preferred_element_type=F32)


@jax.custom_vjp
def bdot(a, b):
    return _bdot(a, b)


def _bdot_fwd(a, b):
    return _bdot(a, b), (a, b)


def _bdot_bwd(res, g):
    a, b = res
    return _bdot(g, b, NT).astype(a.dtype), _bdot(a, g, TN).astype(b.dtype)


bdot.defvjp(_bdot_fwd, _bdot_bwd)


def _mm(name, a, b, out_sds, grid, a_spec, b_spec, o_spec, dims, k_axis=None, add=None, add_spec=None, inner=None):
    nk = grid[k_axis] if k_axis is not None else 1
    has_add = add is not None
    acc_shape = tuple(d for d in o_spec.block_shape if d is not None)
    acc_in_out = out_sds.dtype == F32

    def product(a_ref, b_ref):
        if inner is None:
            return _bdot(a_ref[...], b_ref[...], dims)
        kind, n = inner
        width = a_ref.shape[-1] // n
        acc = None
        for j in range(n):
            a_j = a_ref[j] if kind == "lead" else a_ref[:, j * width:(j + 1) * width]
            p = _bdot(a_j, b_ref[j], dims)
            acc = p if acc is None else acc + p
        return acc

    def body(*refs):
        a_ref, b_ref = refs[0], refs[1]
        add_ref = refs[2] if has_add else None
        o_ref = refs[3] if has_add else refs[2]
        prod = product(a_ref, b_ref)
        if k_axis is None:
            if has_add:
                prod = prod + add_ref[...]
            o_ref[...] = prod.astype(o_ref.dtype)
        else:
            acc_ref = o_ref if acc_in_out else refs[-1]
            k = pl.program_id(k_axis)

            @pl.when(k == 0)
            def _():
                acc_ref[...] = prod

            @pl.when(k > 0)
            def _():
                acc_ref[...] += prod

            if has_add or not acc_in_out:
                @pl.when(k == nk - 1)
                def _():
                    r = acc_ref[...]
                    if has_add:
                        r = r + add_ref[...]
                    o_ref[...] = r.astype(o_ref.dtype)

    sem = tuple("arbitrary" if d == k_axis else "parallel" for d in range(len(grid)))
    in_specs = [a_spec, b_spec] + ([add_spec] if has_add else [])
    args = (a, b) + ((add,) if has_add else ())
    scratch = [pltpu.VMEM(acc_shape, F32)] if (k_axis is not None and not acc_in_out) else []
    return pl.pallas_call(body, out_shape=out_sds, grid=grid, in_specs=in_specs, out_specs=o_spec,
                          scratch_shapes=scratch, compiler_params=_params(sem), name=name)(*args)


def _rms(x, g):
    return x * lax.rsqrt(jnp.mean(x * x, axis=-1, keepdims=True) + EPS) * g


def rms_fwd(name, x, g, tm):
    rows, d = x.shape

    def body(x_ref, g_ref, o_ref):
        o_ref[...] = _rms(x_ref[...], g_ref[...]).astype(BF16)

    return pl.pallas_call(
        body, out_shape=jax.ShapeDtypeStruct((rows, d), BF16), grid=(rows // tm,),
        in_specs=[pl.BlockSpec((tm, d), lambda i: (i, 0)), pl.BlockSpec((1, d), lambda i: (0, 0))],
        out_specs=pl.BlockSpec((tm, d), lambda i: (i, 0)), compiler_params=_params(("parallel",)), name=name)(x, g)


def rms_bwd(name, x, g, dh, dres, tm):
    rows, d = x.shape

    def body(x_ref, g_ref, dh_ref, dres_ref, dx_ref, dg_ref):
        _, vjp = jax.vjp(_rms, x_ref[...], g_ref[...])
        dx, dg = vjp(dh_ref[...])
        dx_ref[...] = dx + dres_ref[...]

        @pl.when(pl.program_id(0) == 0)
        def _():
            dg_ref[...] = jnp.zeros_like(dg_ref)

        dg_ref[...] += dg

    row = pl.BlockSpec((tm, d), lambda i: (i, 0))
    vec = pl.BlockSpec((1, d), lambda i: (0, 0))
    return pl.pallas_call(
        body, out_shape=(jax.ShapeDtypeStruct((rows, d), F32), jax.ShapeDtypeStruct((1, d), F32)), grid=(rows // tm,),
        in_specs=[row, vec, row, row], out_specs=(row, vec), compiler_params=_params(("arbitrary",)), name=name)(x, g, dh, dres)


def final_loss(name, x, g, target, tm):
    rows, d = x.shape

    def body(x_ref, g_ref, t_ref, l_ref, dx_ref, dg_ref):
        def f(xv, gv):
            e = _rms(xv, gv) - t_ref[...]
            return 0.5 * jnp.sum(jnp.mean(e * e, axis=-1))

        loss, (dx, dg) = jax.value_and_grad(f, argnums=(0, 1))(x_ref[...], g_ref[...])
        dx_ref[...] = dx

        @pl.when(pl.program_id(0) == 0)
        def _():
            l_ref[...] = jnp.zeros_like(l_ref)
            dg_ref[...] = jnp.zeros_like(dg_ref)

        l_ref[...] += jnp.full(l_ref.shape, loss, F32)
        dg_ref[...] += dg

    row = pl.BlockSpec((tm, d), lambda i: (i, 0))
    vec = pl.BlockSpec((1, d), lambda i: (0, 0))
    lspec = pl.BlockSpec((1, 128), lambda i: (0, 0))
    return pl.pallas_call(
        body, out_shape=(jax.ShapeDtypeStruct((1, 128), F32), jax.ShapeDtypeStruct((rows, d), F32), jax.ShapeDtypeStruct((1, d), F32)),
        grid=(rows // tm,), in_specs=[row, vec, row], out_specs=(lspec, row, vec),
        compiler_params=_params(("arbitrary",)), name=name)(x, g, target)


def _rowwise(name, fn, row_ins, par_ins, n_row_out, row_out_dtypes, tm, with_grads=False):
    rows = row_ins[0][0].shape[0]
    n_prim = len(row_ins) - (n_row_out if with_grads else 0)
    n_par = len(par_ins)

    def body(*refs):
        ins = [r[...] for r in refs[:len(row_ins) + n_par]]
        outs = refs[len(row_ins) + n_par:]
        prim, cts, pars = ins[:n_prim], ins[n_prim:len(row_ins)], ins[len(row_ins):]
        if not with_grads:
            res = fn(*prim, *pars)
            for o_ref, r in zip(outs, res):
                o_ref[...] = r.astype(o_ref.dtype)
            return
        _, vjp = jax.vjp(fn, *prim, *[p.astype(F32) for p in pars])
        grads = vjp(tuple(cts))
        for o_ref, gr in zip(outs[:n_prim], grads[:n_prim]):
            o_ref[...] = gr.astype(o_ref.dtype)

        @pl.when(pl.program_id(0) == 0)
        def _():
            for o_ref in outs[n_prim:]:
                o_ref[...] = jnp.zeros_like(o_ref)

        for o_ref, gr in zip(outs[n_prim:], grads[n_prim:]):
            o_ref[...] += gr.astype(F32)

    in_specs = [pl.BlockSpec((tm, w), (lambda i, c=c: (i, c))) for (_, c, w) in row_ins]
    in_specs += [pl.BlockSpec(p.shape, (lambda i, n=p.ndim: (0,) * n)) for p in par_ins]
    args = [a for (a, _, _) in row_ins] + list(par_ins)
    if not with_grads:
        out_shape = tuple(jax.ShapeDtypeStruct((rows, w), dt) for (w, dt) in row_out_dtypes)
        out_specs = tuple(pl.BlockSpec((tm, w), lambda i: (i, 0)) for (w, _) in row_out_dtypes)
        sem = ("parallel",)
    else:
        out_shape = tuple(jax.ShapeDtypeStruct((rows, w), dt) for (w, dt) in row_out_dtypes)
        out_shape += tuple(jax.ShapeDtypeStruct(p.shape, F32) for p in par_ins)
        out_specs = tuple(pl.BlockSpec((tm, w), lambda i: (i, 0)) for (w, _) in row_out_dtypes)
        out_specs += tuple(pl.BlockSpec(p.shape, (lambda i, n=p.ndim: (0,) * n)) for p in par_ins)
        sem = ("arbitrary",)
    return pl.pallas_call(body, out_shape=out_shape, grid=(rows // tm,), in_specs=in_specs, out_specs=out_specs,
                          compiler_params=_params(sem), name=name)(*args)


def _glu(v, g):
    return (v * jax.nn.sigmoid(g),)


def _neg_expm1(z):
    return -jnp.tanh(0.5 * z) * (jnp.exp(z) + 1.0)


def _lru_gate(xc, w_r, w_i, b_r, b_i, lam):
    r = jax.nn.sigmoid(bdot(xc, w_r) + b_r)
    i = jax.nn.sigmoid(bdot(xc, w_i) + b_i)
    log_a = -LRU_C * r * jax.nn.softplus(-lam)
    a = jnp.exp(log_a)
    mult = jnp.sqrt(_neg_expm1(2.0 * log_a))
    return a, mult * (i * xc)


def _layernorm(x, g, b):
    mu = jnp.mean(x, axis=-1, keepdims=True)
    var = jnp.mean(jnp.square(x - mu), axis=-1, keepdims=True)
    return (x - mu) * lax.rsqrt(var + EPS) * g + b


def _mix_post(y_ssm, u, h1, hseq, lru_g, dgp, s5_d, w_glu, b_glu, ln_g, ln_b, w_pw, b_pw, pool_bd, pool_scale):
    y = y_ssm + s5_d * u
    gl = jax.nn.gelu(y, approximate=True)
    out_s5 = gl * jax.nn.sigmoid(bdot(gl, w_glu) + b_glu)
    out_cv = bdot(jax.nn.silu(_layernorm(h1, ln_g, ln_b)), w_pw) + b_pw
    out_lru = hseq * jax.nn.gelu(lru_g, approximate=True)
    out_pool = bdot(dgp, pool_bd) * pool_scale
    return (jnp.concatenate([out_s5, out_cv, out_lru, out_pool], axis=-1),)


def _ffn_act(gc, val):
    return (jax.nn.gelu(gc, approximate=True) * val,)


def ffn_act_fwd(name, gc, up, tm):
    _, rows, c = gc.shape

    def body(g_ref, v_ref, o_ref):
        o_ref[...] = _ffn_act(g_ref[...], v_ref[...])[0].astype(BF16)

    return pl.pallas_call(
        body, out_shape=jax.ShapeDtypeStruct((2, rows, c), BF16), grid=(2, rows // tm),
        in_specs=[pl.BlockSpec((None, tm, c), lambda h, i: (h, i, 0)), pl.BlockSpec((None, tm, c), lambda h, i: (h + 2, i, 0))],
        out_specs=pl.BlockSpec((None, tm, c), lambda h, i: (h, i, 0)),
        compiler_params=_params(("parallel", "parallel")), name=name)(gc, up)


def ffn_act_bwd(name, gc, up, dact, tm):
    _, rows, c = gc.shape

    def body(g_ref, v_ref, d_ref, dg_ref, dv_ref):
        _, vjp = jax.vjp(_ffn_act, g_ref[...], v_ref[...])
        dg, dv = vjp((d_ref[...],))
        dg_ref[...] = dg
        dv_ref[...] = dv.astype(BF16)

    blk = pl.BlockSpec((None, tm, c), lambda h, i: (h, i, 0))
    return pl.pallas_call(
        body, out_shape=(jax.ShapeDtypeStruct((2, rows, c), F32), jax.ShapeDtypeStruct((2, rows, c), BF16)), grid=(2, rows // tm),
        in_specs=[blk, pl.BlockSpec((None, tm, c), lambda h, i: (h + 2, i, 0)), blk], out_specs=(blk, blk),
        compiler_params=_params(("parallel", "parallel")), name=name)(gc, up, dact)


def _halo_rows(taps):
    return -(-(taps - 1) // SUBLANES) * SUBLANES


def dwconv_fwd(name, x, cblk, c, w, b, taps, tm, out_dtype=F32):
    nb = w.shape[0]
    rows = x.shape[1]
    halo = _halo_rows(taps)
    per = tm // halo

    def body(x_ref, h_ref, w_ref, b_ref, o_ref):
        i = pl.program_id(1)
        prev = jnp.where(i > 0, h_ref[...], 0.0)
        ext = jnp.concatenate([prev, x_ref[...]], axis=0)
        acc = jnp.broadcast_to(b_ref[...], (tm, c))
        for k in range(taps):
            off = halo - (taps - 1) + k
            acc = acc + w_ref[k:k + 1, :] * ext[off:off + tm]
        o_ref[...] = acc.astype(o_ref.dtype)

    return pl.pallas_call(
        body, out_shape=jax.ShapeDtypeStruct((nb, rows, c), out_dtype), grid=(nb, rows // tm),
        in_specs=[pl.BlockSpec((None, tm, c), lambda n, i: (n, i, cblk)),
                  pl.BlockSpec((None, halo, c), lambda n, i: (n, jnp.maximum(i * per - 1, 0), cblk)),
                  pl.BlockSpec((None, taps, c), lambda n, i: (n, 0, 0)),
                  pl.BlockSpec((None, 1, c), lambda n, i: (n, 0, 0))],
        out_specs=pl.BlockSpec((None, tm, c), lambda n, i: (n, i, 0)),
        compiler_params=_params(("parallel", "parallel")), name=name)(x, x, w, b)


def dwconv_bwd(name, dy, x, cblk, c, w, taps, tm, dx_dtype=F32):
    nb = w.shape[0]
    rows = x.shape[1]
    halo = _halo_rows(taps)
    per = tm // halo
    n_tiles = rows // tm
    last_halo = rows // halo - 1

    def body(dy_ref, dn_ref, x_ref, xp_ref, w_ref, dx_ref, dw_ref, db_ref):
        i = pl.program_id(1)
        dyv = dy_ref[...]
        nxt = jnp.where(i < n_tiles - 1, dn_ref[...], 0.0)
        dext = jnp.concatenate([dyv, nxt], axis=0)
        prev = jnp.where(i > 0, xp_ref[...], 0.0)
        xext = jnp.concatenate([prev, x_ref[...]], axis=0)
        acc = jnp.zeros((tm, c), F32)

        @pl.when(i == 0)
        def _():
            dw_ref[...] = jnp.zeros_like(dw_ref)
            db_ref[...] = jnp.zeros_like(db_ref)

        for k in range(taps):
            acc = acc + w_ref[k:k + 1, :] * dext[taps - 1 - k:taps - 1 - k + tm]
            off = halo - (taps - 1) + k
            dw_ref[k:k + 1, :] += jnp.sum(dyv * xext[off:off + tm], axis=0, keepdims=True)
        dx_ref[...] = acc.astype(dx_ref.dtype)
        db_ref[...] += jnp.sum(dyv, axis=0, keepdims=True)

    return pl.pallas_call(
        body, out_shape=(jax.ShapeDtypeStruct((nb, rows, c), dx_dtype), jax.ShapeDtypeStruct((nb, taps, c), F32),
                         jax.ShapeDtypeStruct((nb, 1, c), F32)),
        grid=(nb, n_tiles),
        in_specs=[pl.BlockSpec((None, tm, c), lambda n, i: (n, i, 0)),
                  pl.BlockSpec((None, halo, c), lambda n, i: (n, jnp.minimum((i + 1) * per, last_halo), 0)),
                  pl.BlockSpec((None, tm, c), lambda n, i: (n, i, cblk)),
                  pl.BlockSpec((None, halo, c), lambda n, i: (n, jnp.maximum(i * per - 1, 0), cblk)),
                  pl.BlockSpec((None, taps, c), lambda n, i: (n, 0, 0))],
        out_specs=(pl.BlockSpec((None, tm, c), lambda n, i: (n, i, 0)), pl.BlockSpec((None, taps, c), lambda n, i: (n, 0, 0)),
                   pl.BlockSpec((None, 1, c), lambda n, i: (n, 0, 0))),
        compiler_params=_params(("parallel", "arbitrary")), name=name)(dy, dy, x, x, w)


def ffn_gate_fwd(name, up, w, b, tm):
    _, rows, c = up.shape
    halo = _halo_rows(FFN_TAPS)
    per = tm // halo

    def body(g_ref, gp_ref, v_ref, w_ref, b_ref, o_ref):
        i = pl.program_id(1)
        ext = jnp.concatenate([jnp.where(i > 0, gp_ref[...], 0.0), g_ref[...]], axis=0)
        gc = jnp.broadcast_to(b_ref[...], (tm, c))
        for k in range(FFN_TAPS):
            off = halo - (FFN_TAPS - 1) + k
            gc = gc + w_ref[k:k + 1, :] * ext[off:off + tm]
        o_ref[...] = _ffn_act(gc, v_ref[...])[0].astype(BF16)

    return pl.pallas_call(
        body, out_shape=jax.ShapeDtypeStruct((2, rows, c), BF16), grid=(2, rows // tm),
        in_specs=[pl.BlockSpec((None, tm, c), lambda h, i: (h, i, 0)),
                  pl.BlockSpec((None, halo, c), lambda h, i: (h, jnp.maximum(i * per - 1, 0), 0)),
                  pl.BlockSpec((None, tm, c), lambda h, i: (h + 2, i, 0)),
                  pl.BlockSpec((None, FFN_TAPS, c), lambda h, i: (h, 0, 0)), pl.BlockSpec((None, 1, c), lambda h, i: (h, 0, 0))],
        out_specs=pl.BlockSpec((None, tm, c), lambda h, i: (h, i, 0)),
        compiler_params=_params(("parallel", "parallel")), name=name)(up, up, up, w, b)


def ffn_gate_bwd(name, up, dact, w, b, tm):
    _, rows, c = up.shape
    halo = _halo_rows(FFN_TAPS)
    per = tm // halo
    n_tiles = rows // tm
    last_halo = rows // halo - 1
    n_ext = tm + halo

    def body(g_ref, gp_ref, gn_ref, v_ref, vn_ref, d_ref, dn_ref, w_ref, b_ref, dup_ref, dw_ref, db_ref):
        i = pl.program_id(1)
        gext = jnp.concatenate([jnp.where(i > 0, gp_ref[...], 0.0), g_ref[...], gn_ref[...]], axis=0)
        gc = jnp.broadcast_to(b_ref[...], (n_ext, c))
        for k in range(FFN_TAPS):
            off = halo - (FFN_TAPS - 1) + k
            gc = gc + w_ref[k:k + 1, :] * gext[off:off + n_ext]
        vext = jnp.concatenate([v_ref[...], vn_ref[...]], axis=0)
        dext = jnp.concatenate([d_ref[...], dn_ref[...]], axis=0)
        _, vjp = jax.vjp(_ffn_act, gc, vext)
        dgc, dval = vjp((dext,))
        r = lax.broadcasted_iota(jnp.int32, (n_ext, c), 0)
        dgc = jnp.where((r < tm) | (i < n_tiles - 1), dgc, 0.0)
        dgate = jnp.zeros((tm, c), F32)
        for k in range(FFN_TAPS):
            dgate = dgate + w_ref[k:k + 1, :] * dgc[FFN_TAPS - 1 - k:FFN_TAPS - 1 - k + tm]
        dup_ref[0] = dgate.astype(BF16)
        dup_ref[1] = dval[:tm].astype(BF16)

        @pl.when(i == 0)
        def _():
            dw_ref[...] = jnp.zeros_like(dw_ref)
            db_ref[...] = jnp.zeros_like(db_ref)

        dgc_t = dgc[:tm]
        for k in range(FFN_TAPS):
            off = halo - (FFN_TAPS - 1) + k
            dw_ref[k:k + 1, :] += jnp.sum(dgc_t * gext[off:off + tm], axis=0, keepdims=True)
        db_ref[...] += jnp.sum(dgc_t, axis=0, keepdims=True)

    def tile(shift):
        return pl.BlockSpec((None, tm, c), lambda h, i: (h + shift, i, 0))

    def after(shift):
        return pl.BlockSpec((None, halo, c), lambda h, i: (h + shift, jnp.minimum((i + 1) * per, last_halo), 0))

    return pl.pallas_call(
        body, out_shape=(jax.ShapeDtypeStruct((2, 2, rows, c), BF16), jax.ShapeDtypeStruct((2, FFN_TAPS, c), F32),
                         jax.ShapeDtypeStruct((2, 1, c), F32)),
        grid=(2, n_tiles),
        in_specs=[tile(0), pl.BlockSpec((None, halo, c), lambda h, i: (h, jnp.maximum(i * per - 1, 0), 0)), after(0),
                  tile(2), after(2), tile(0), after(0),
                  pl.BlockSpec((None, FFN_TAPS, c), lambda h, i: (h, 0, 0)), pl.BlockSpec((None, 1, c), lambda h, i: (h, 0, 0))],
        out_specs=(pl.BlockSpec((2, None, tm, c), lambda h, i: (0, h, i, 0)), pl.BlockSpec((None, FFN_TAPS, c), lambda h, i: (h, 0, 0)),
                   pl.BlockSpec((None, 1, c), lambda h, i: (h, 0, 0))),
        compiler_params=_params(("parallel", "arbitrary")), name=name)(up, up, up, up, up, dact, dact, w, b)


POOL_HALO = 16


def pool_fwd(name, proj, cblk, tm):
    rows = proj.shape[0]
    c = 128 * len(POOL_WINDOWS)
    per = tm // POOL_HALO

    def body(x_ref, h_ref, o_ref):
        i = pl.program_id(0)
        xv = x_ref[...]
        ext = jnp.concatenate([jnp.where(i > 0, h_ref[...], 0.0), xv], axis=0)
        t1 = (lax.broadcasted_iota(jnp.int32, (tm, 128), 0) + i * tm + 1).astype(F32)
        outs = []
        for gi, win in enumerate(POOL_WINDOWS):
            seg = ext[:, gi * 128:(gi + 1) * 128]
            s = seg[POOL_HALO:POOL_HALO + tm]
            for j in range(1, win):
                s = s + seg[POOL_HALO - j:POOL_HALO - j + tm]
            outs.append(s / jnp.minimum(t1, float(win)) - xv[:, gi * 128:(gi + 1) * 128])
        o_ref[...] = jnp.concatenate(outs, axis=-1)

    return pl.pallas_call(
        body, out_shape=jax.ShapeDtypeStruct((rows, c), F32), grid=(rows // tm,),
        in_specs=[pl.BlockSpec((tm, c), lambda i: (i, cblk)),
                  pl.BlockSpec((POOL_HALO, c), lambda i: (jnp.maximum(i * per - 1, 0), cblk))],
        out_specs=pl.BlockSpec((tm, c), lambda i: (i, 0)), compiler_params=_params(("parallel",)), name=name)(proj, proj)


def pool_bwd(name, dd, tm):
    rows, c = dd.shape
    per = tm // POOL_HALO
    n_tiles = rows // tm
    last_halo = rows // POOL_HALO - 1

    def body(d_ref, n_ref, o_ref):
        i = pl.program_id(0)
        dv = d_ref[...]
        nxt = jnp.where(i < n_tiles - 1, n_ref[...], 0.0)
        t1 = (lax.broadcasted_iota(jnp.int32, (tm, 128), 0) + i * tm + 1).astype(F32)
        t1n = (lax.broadcasted_iota(jnp.int32, (POOL_HALO, 128), 0) + (i + 1) * tm + 1).astype(F32)
        outs = []
        for gi, win in enumerate(POOL_WINDOWS):
            sl = slice(gi * 128, (gi + 1) * 128)
            q = jnp.concatenate([dv[:, sl] / jnp.minimum(t1, float(win)), nxt[:, sl] / jnp.minimum(t1n, float(win))], axis=0)
            s = q[0:tm]
            for j in range(1, win):
                s = s + q[j:j + tm]
            outs.append(s - dv[:, sl])
        o_ref[...] = jnp.concatenate(outs, axis=-1)

    return pl.pallas_call(
        body, out_shape=jax.ShapeDtypeStruct((rows, c), F32), grid=(n_tiles,),
        in_specs=[pl.BlockSpec((tm, c), lambda i: (i, 0)),
                  pl.BlockSpec((POOL_HALO, c), lambda i: (jnp.minimum((i + 1) * per, last_halo), 0))],
        out_specs=pl.BlockSpec((tm, c), lambda i: (i, 0)), compiler_params=_params(("parallel",)), name=name)(dd, dd)


def _shift_down(v, s, fill):
    r = lax.broadcasted_iota(jnp.int32, v.shape, 0)
    return jnp.where(r >= s, pltpu.roll(v, s, 0), fill)


def _shift_up(v, s, fill):
    n = v.shape[0]
    r = lax.broadcasted_iota(jnp.int32, v.shape, 0)
    return jnp.where(r < n - s, pltpu.roll(v, n - s, 0), fill)


def _cscan_chunk(vr, vi, powers, reverse):
    for k, (qr, qi) in enumerate(powers):
        s = 1 << k
        if reverse:
            sr, si = _shift_up(vr, s, 0.0), _shift_up(vi, s, 0.0)
            vr, vi = vr + qr * sr + qi * si, vi + qr * si - qi * sr
        else:
            sr, si = _shift_down(vr, s, 0.0), _shift_down(vi, s, 0.0)
            vr, vi = vr + qr * sr - qi * si, vi + qr * si + qi * sr
    return vr, vi


def _powers(pr, pi, n):
    out = [(pr, pi)]
    for _ in range(n - 1):
        pr, pi = pr * pr - pi * pi, 2.0 * pr * pi
        out.append((pr, pi))
    return out


def s5_scan_fwd(name, bu, a):
    _, rows, n = bu.shape
    t = min(SCAN_CHUNK, rows)
    steps = t.bit_length() - 1

    def body(bu_ref, a_ref, z_ref):
        pr, pi = a_ref[0], a_ref[1]
        powers = _powers(pr, pi, steps)
        r = lax.broadcasted_iota(jnp.int32, (t, 128), 0)
        tr, ti = _cscan_chunk(jnp.where(r == 0, pr, 0.0), jnp.where(r == 0, pi, 0.0), powers, False)

        def chunk(ci, carry):
            base = pl.multiple_of(ci * t, t)
            vr, vi = _cscan_chunk(bu_ref[0, pl.ds(base, t), :], bu_ref[1, pl.ds(base, t), :], powers, False)
            cr, cim = carry
            zr = vr + tr * cr - ti * cim
            zi = vi + tr * cim + ti * cr
            z_ref[0, pl.ds(base, t), :] = zr
            z_ref[1, pl.ds(base, t), :] = zi
            return zr[t - 1:t, :], zi[t - 1:t, :]

        zero = jnp.zeros((1, 128), F32)
        lax.fori_loop(0, rows // t, chunk, (zero, zero))

    return pl.pallas_call(
        body, out_shape=jax.ShapeDtypeStruct((2, rows, n), F32), grid=(n // 128,),
        in_specs=[pl.BlockSpec((2, rows, 128), lambda j: (0, 0, j)), pl.BlockSpec((2, 1, 128), lambda j: (0, 0, j))],
        out_specs=pl.BlockSpec((2, rows, 128), lambda j: (0, 0, j)), compiler_params=_params(("parallel",)), name=name)(bu, a)


def s5_scan_bwd(name, dz, z, a):
    _, rows, n = dz.shape
    t = min(SCAN_CHUNK, rows)
    steps = t.bit_length() - 1
    n_chunks = rows // t

    def body(dz_ref, z_ref, a_ref, lam_ref, da_ref):
        pr, pi = a_ref[0], a_ref[1]
        powers = _powers(pr, pi, steps)
        r = lax.broadcasted_iota(jnp.int32, (t, 128), 0)
        tr, ti = _cscan_chunk(jnp.where(r == t - 1, pr, 0.0), jnp.where(r == t - 1, -pi, 0.0), powers, True)

        def chunk(k, carry):
            ci = n_chunks - 1 - k
            base = pl.multiple_of(ci * t, t)
            vr, vi = _cscan_chunk(dz_ref[0, pl.ds(base, t), :], dz_ref[1, pl.ds(base, t), :], powers, True)
            cr, cim, dar, dai = carry
            lr = vr + tr * cr - ti * cim
            li = vi + tr * cim + ti * cr
            lam_ref[0, pl.ds(base, t), :] = lr
            lam_ref[1, pl.ds(base, t), :] = li
            pbase = pl.multiple_of(jnp.maximum(base - SUBLANES, 0), SUBLANES)
            keep = (ci > 0).astype(F32)
            pzr = z_ref[0, pl.ds(pbase, SUBLANES), :][SUBLANES - 1:SUBLANES, :] * keep
            pzi = z_ref[1, pl.ds(pbase, SUBLANES), :][SUBLANES - 1:SUBLANES, :] * keep
            zpr = _shift_down(z_ref[0, pl.ds(base, t), :], 1, pzr)
            zpi = _shift_down(z_ref[1, pl.ds(base, t), :], 1, pzi)
            dar = dar + jnp.sum(lr * zpr + li * zpi, axis=0, keepdims=True)
            dai = dai + jnp.sum(li * zpr - lr * zpi, axis=0, keepdims=True)
            return lr[0:1, :], li[0:1, :], dar, dai

        zero = jnp.zeros((1, 128), F32)
        _, _, dar, dai = lax.fori_loop(0, n_chunks, chunk, (zero, zero, zero, zero))
        da_ref[0] = dar
        da_ref[1] = dai

    seq = pl.BlockSpec((2, rows, 128), lambda j: (0, 0, j))
    vec = pl.BlockSpec((2, 1, 128), lambda j: (0, 0, j))
    return pl.pallas_call(
        body, out_shape=(jax.ShapeDtypeStruct((2, rows, n), F32), jax.ShapeDtypeStruct((2, 1, n), F32)), grid=(n // 128,),
        in_specs=[seq, seq, vec], out_specs=(seq, vec), compiler_params=_params(("parallel",)), name=name)(dz, z, a)


def _rscan_chunk(a, b, steps, reverse):
    shift = _shift_up if reverse else _shift_down
    for k in range(steps):
        s = 1 << k
        b = b + a * shift(b, s, 0.0)
        a = a * shift(a, s, 1.0)
    return a, b


def lru_scan_fwd(name, a, b):
    rows, n = a.shape
    t = min(SCAN_CHUNK, rows)
    steps = t.bit_length() - 1

    def body(a_ref, b_ref, h_ref):
        def chunk(ci, carry):
            base = pl.multiple_of(ci * t, t)
            pa, hb = _rscan_chunk(a_ref[pl.ds(base, t), :], b_ref[pl.ds(base, t), :], steps, False)
            h = hb + pa * carry
            h_ref[pl.ds(base, t), :] = h
            return h[t - 1:t, :]

        lax.fori_loop(0, rows // t, chunk, jnp.zeros((1, 128), F32))

    seq = pl.BlockSpec((rows, 128), lambda j: (0, j))
    return pl.pallas_call(body, out_shape=jax.ShapeDtypeStruct((rows, n), F32), grid=(n // 128,), in_specs=[seq, seq],
                          out_specs=seq, compiler_params=_params(("parallel",)), name=name)(a, b)


def lru_scan_bwd(name, dh, a, h):
    rows, n = a.shape
    t = min(SCAN_CHUNK, rows)
    steps = t.bit_length() - 1
    n_chunks = rows // t

    def body(dh_ref, a_ref, h_ref, da_ref, db_ref):
        def chunk(k, carry):
            ci = n_chunks - 1 - k
            base = pl.multiple_of(ci * t, t)
            nbase = pl.multiple_of(jnp.minimum(base + t, rows - SUBLANES), SUBLANES)
            a_next = a_ref[pl.ds(nbase, SUBLANES), :][0:1, :]
            an = _shift_up(a_ref[pl.ds(base, t), :], 1, a_next)
            pa, mb = _rscan_chunk(an, dh_ref[pl.ds(base, t), :], steps, True)
            mu = mb + pa * carry
            pbase = pl.multiple_of(jnp.maximum(base - SUBLANES, 0), SUBLANES)
            hp_row = h_ref[pl.ds(pbase, SUBLANES), :][SUBLANES - 1:SUBLANES, :] * (ci > 0).astype(F32)
            hp = _shift_down(h_ref[pl.ds(base, t), :], 1, hp_row)
            da_ref[pl.ds(base, t), :] = mu * hp
            db_ref[pl.ds(base, t), :] = mu
            return mu[0:1, :]

        lax.fori_loop(0, n_chunks, chunk, jnp.zeros((1, 128), F32))

    seq = pl.BlockSpec((rows, 128), lambda j: (0, j))
    return pl.pallas_call(
        body, out_shape=(jax.ShapeDtypeStruct((rows, n), F32), jax.ShapeDtypeStruct((rows, n), F32)), grid=(n // 128,),
        in_specs=[seq, seq, seq], out_specs=(seq, seq), compiler_params=_params(("parallel",)), name=name)(dh, a, h)


def _s5_param(lr, li, ls, bre, bim):
    st = jnp.exp(ls)
    er = jnp.exp(lr * st)
    th = li * st
    ar, ai = er * jnp.cos(th), er * jnp.sin(th)
    nr, ni = ar - 1.0, ai
    den = lr * lr + li * li
    cr, ci = (nr * lr + ni * li) / den, (ni * lr - nr * li) / den
    return ar, ai, cr * bre - ci * bim, cr * bim + ci * bre


def s5_param_fwd(name, lr, li, ls, bre, bim):
    gh, n = bre.shape

    def body(lr_ref, li_ref, ls_ref, bre_ref, bim_ref, a_ref, bb_ref):
        ar, ai, br, bi = _s5_param(lr_ref[...], li_ref[...], ls_ref[...], bre_ref[...], bim_ref[...])
        a_ref[0] = ar
        a_ref[1] = ai
        bb_ref[0] = br.astype(BF16)
        bb_ref[1] = bi.astype(BF16)

    return pl.pallas_call(body, out_shape=(jax.ShapeDtypeStruct((2, 1, n), F32), jax.ShapeDtypeStruct((2, gh, n), BF16)),
                          compiler_params=_params(), name=name)(lr, li, ls, bre, bim)


def s5_param_bwd(name, lr, li, ls, bre, bim, da, dbb, gsum):
    gh, n = bre.shape

    def body(lr_ref, li_ref, ls_ref, bre_ref, bim_ref, da_ref, dbb_ref, gs_ref, dlr_ref, dli_ref, dls_ref, dbre_ref, dbim_ref):
        _, vjp = jax.vjp(_s5_param, lr_ref[...], li_ref[...], ls_ref[...], bre_ref[...], bim_ref[...])
        dlr, dli, dls, dbre, dbim = vjp((da_ref[0], da_ref[1], dbb_ref[0], dbb_ref[1]))
        dlr_ref[...] = dlr
        dli_ref[...] = dli
        dls_ref[...] = jnp.dot(jnp.broadcast_to(dls, (SUBLANES, n)), gs_ref[...], preferred_element_type=F32,
                               precision=lax.Precision.HIGHEST)
        dbre_ref[...] = dbre
        dbim_ref[...] = dbim

    vec = jax.ShapeDtypeStruct((1, n), F32)
    mat = jax.ShapeDtypeStruct((gh, n), F32)
    return pl.pallas_call(body, out_shape=(vec, vec, jax.ShapeDtypeStruct((SUBLANES, 128), F32), mat, mat),
                          compiler_params=_params(), name=name)(lr, li, ls, bre, bim, da, dbb, gsum)


def sum_lead(name, x, tr):
    n, rows, cols = x.shape

    def body(x_ref, o_ref):
        acc = x_ref[0]
        for j in range(1, n):
            acc = acc + x_ref[j]
        o_ref[...] = acc

    return pl.pallas_call(
        body, out_shape=jax.ShapeDtypeStruct((rows, cols), x.dtype), grid=(rows // tr,),
        in_specs=[pl.BlockSpec((n, tr, cols), lambda i: (0, i, 0))], out_specs=pl.BlockSpec((tr, cols), lambda i: (i, 0)),
        compiler_params=_params(("parallel",)), name=name)(x)


def _adamw(w, g, m, v):
    m = ADAM_B1 * m + (1.0 - ADAM_B1) * g
    v = ADAM_B2 * v + (1.0 - ADAM_B2) * jnp.square(g)
    m_hat = m / (1.0 - ADAM_B1 ** ADAM_STEP)
    v_hat = v / (1.0 - ADAM_B2 ** ADAM_STEP)
    delta = -ADAM_LR * (m_hat / (jnp.sqrt(v_hat) + ADAM_EPS) + ADAM_WD * w)
    return delta, m, v


def adamw_sharded(name, w, m, v, g0, g1, split_cols, tile):
    _, r, c = w.shape
    if split_cols:
        nt = c // tile
        per = (c // 2) // tile
        wspec = pl.BlockSpec((None, r, tile), lambda l, t: (l, 0, t))
        gspec = pl.BlockSpec((None, r, tile), lambda l, t: (t // per, 0, t % per))
    else:
        nt = r // tile
        per = (r // 2) // tile
        wspec = pl.BlockSpec((None, tile, c), lambda l, t: (l, t, 0))
        gspec = pl.BlockSpec((None, tile, c), lambda l, t: (t // per, t % per, 0))

    def body(w_ref, m_ref, v_ref, g0_ref, g1_ref, g_ref, d_ref, nm_ref, nv_ref):
        g = jnp.where(pl.program_id(0) == 0, g0_ref[...], g1_ref[...])
        d, nm, nv = _adamw(w_ref[...], g, m_ref[...], v_ref[...])
        g_ref[...] = g
        d_ref[...] = d
        nm_ref[...] = nm
        nv_ref[...] = nv

    sds = jax.ShapeDtypeStruct(w.shape, F32)
    return pl.pallas_call(body, out_shape=(sds,) * 4, grid=(2, nt), in_specs=[wspec, wspec, wspec, gspec, gspec],
                          out_specs=(wspec,) * 4, compiler_params=_params(("parallel", "parallel")), name=name)(w, m, v, g0, g1)


def adamw_flat(name, w, g, m, v, tr):
    rows, cols = w.shape

    def body(w_ref, g_ref, m_ref, v_ref, d_ref, nm_ref, nv_ref):
        d, nm, nv = _adamw(w_ref[...], g_ref[...], m_ref[...], v_ref[...])
        d_ref[...] = d
        nm_ref[...] = nm
        nv_ref[...] = nv

    blk = pl.BlockSpec((tr, cols), lambda i: (i, 0))
    sds = jax.ShapeDtypeStruct((rows, cols), F32)
    return pl.pallas_call(body, out_shape=(sds,) * 3, grid=(rows // tr,), in_specs=[blk] * 4, out_specs=(blk,) * 3,
                          compiler_params=_params(("parallel",)), name=name)(w, g, m, v)


def _flips(axes):
    out = []
    for fx in ((0, 1) if "x" in axes else (0,)):
        for fy in ((0, 1) if "y" in axes else (0,)):
            for fc in ((0, 1) if "c" in axes else (0,)):
                if fx or fy or fc:
                    out.append((fx, fy, fc))
    return out


def _slot(pos, axes):
    s = 0
    for name, p in zip(("x", "y", "c"), pos):
        if name in axes:
            s = 2 * s + p
    return s


def _exchange(name, arrs, axes, scatter):
    flips = _flips(axes)
    n = len(flips) + 1
    na = len(arrs)

    def body(*refs):
        ins, outs = refs[:na], refs[na:2 * na]
        send_sems, recv_sems, local_sems = refs[2 * na:]
        me = (lax.axis_index("x"), lax.axis_index("y"), lax.axis_index("c"))
        my = _slot(me, axes)
        peers = [tuple((1 - p) if f else p for p, f in zip(me, fl)) for fl in flips]

        def src(a, dest_slot):
            return ins[a].at[dest_slot] if scatter else ins[a]

        local = [pltpu.make_async_copy(src(a, my), outs[a].at[my], local_sems.at[a]) for a in range(na)]
        for cp in local:
            cp.start()

        def remote(a, j, landing_slot, dest_slot):
            return pltpu.make_async_remote_copy(
                src_ref=src(a, dest_slot), dst_ref=outs[a].at[landing_slot], send_sem=send_sems.at[a * len(flips) + j],
                recv_sem=recv_sems.at[a * len(flips) + j], device_id=peers[j], device_id_type=pl.DeviceIdType.MESH)

        sends = [remote(a, j, my, _slot(peers[j], axes)) for a in range(na) for j in range(len(flips))]
        for cp in sends:
            cp.start()
        for a in range(na):
            for j in range(len(flips)):
                remote(a, j, _slot(peers[j], axes), _slot(peers[j], axes)).wait_recv()
        for cp in sends:
            cp.wait_send()
        for cp in local:
            cp.wait()

    if scatter:
        out_shape = tuple(jax.ShapeDtypeStruct(a.shape, a.dtype) for a in arrs)
    else:
        out_shape = tuple(jax.ShapeDtypeStruct((n,) + a.shape, a.dtype) for a in arrs)
    anyspec = pl.BlockSpec(memory_space=pl.ANY)
    return pl.pallas_call(
        body, out_shape=out_shape, in_specs=[anyspec] * na, out_specs=(anyspec,) * na,
        scratch_shapes=[pltpu.SemaphoreType.DMA((na * len(flips),)), pltpu.SemaphoreType.DMA((na * len(flips),)),
                        pltpu.SemaphoreType.DMA((na,))],
        name=name)(*arrs)


def all_gather(name, arrs, axes):
    return _exchange(name, arrs, axes, False)


def all_to_all(name, arrs, axes):
    return _exchange(name, arrs, axes, True)


_HBM = pl.BlockSpec(memory_space=pltpu.HBM)
_SEM = pl.BlockSpec(memory_space=pltpu.SEMAPHORE)
_EFFECT = pltpu.SideEffectType.DATAFLOW_SIDE_EFFECTING


def place_own(name, arrs, axes, scatter):
    n = len(_flips(axes)) + 1
    na = len(arrs)

    def body(*refs):
        ins, outs, sems = refs[:na], refs[na:2 * na], refs[2 * na]
        my = _slot((lax.axis_index("x"), lax.axis_index("y"), lax.axis_index("c")), axes)
        copies = [pltpu.make_async_copy(ins[a].at[my] if scatter else ins[a], outs[a].at[my], sems.at[a]) for a in range(na)]
        for cp in copies:
            cp.start()
        for cp in copies:
            cp.wait()

    out_shape = tuple(jax.ShapeDtypeStruct(a.shape if scatter else (n,) + a.shape, a.dtype) for a in arrs)
    anyspec = pl.BlockSpec(memory_space=pl.ANY)
    return pl.pallas_call(body, out_shape=out_shape, in_specs=[anyspec] * na, out_specs=(anyspec,) * na,
                          scratch_shapes=[pltpu.SemaphoreType.DMA((na,))], name=name)(*arrs)


def _peers(axes):
    me = (lax.axis_index("x"), lax.axis_index("y"), lax.axis_index("c"))
    return me, [tuple((1 - p) if f else p for p, f in zip(me, fl)) for fl in _flips(axes)]


def place_tile(name, arr, layer, my, slots=4, dtype=BF16):
    _, r, cols = arr.shape
    tr = _tile_rows(r, cols)

    def body(my_ref, x_ref, o_ref):
        o_ref[...] = x_ref[...].astype(dtype)

    grid_spec = pltpu.PrefetchScalarGridSpec(
        num_scalar_prefetch=1, grid=(r // tr,), in_specs=[pl.BlockSpec((None, tr, cols), lambda i, my: (layer, i, 0))],
        out_specs=pl.BlockSpec((None, tr, cols), lambda i, my: (my[0], i, 0)))
    return pl.pallas_call(body, out_shape=jax.ShapeDtypeStruct((slots, r, cols), dtype), grid_spec=grid_spec,
                          compiler_params=_params(("parallel",)), name=name)(my, arr)


def exchange_start(name, groups, axes, scatter):
    flat = [(p if scatter else (p,)) for grp in groups for p in grp]
    per = 2 if scatter else 1
    na, ng, npeer = len(flat), len(groups), len(_flips(axes))

    def body(*refs):
        ops = refs[:per * na]
        zones = ops[(per - 1) * na:]
        sems, token = refs[per * na:per * na + 2 * ng], refs[-1]
        me, peers = _peers(axes)
        my = _slot(me, axes)
        ai = 0
        for g, grp in enumerate(groups):
            for k in range(len(grp)):
                for j, peer in enumerate(peers):
                    src = ops[ai].at[_slot(peer, axes)] if scatter else zones[ai].at[my]
                    dst = zones[ai].at[j] if scatter else zones[ai].at[my]
                    pltpu.make_async_remote_copy(
                        src_ref=src, dst_ref=dst, send_sem=sems[2 * g].at[k * npeer + j],
                        recv_sem=sems[2 * g + 1].at[k * npeer + j], device_id=peer, device_id_type=pl.DeviceIdType.MESH).start()
                ai += 1
        token[...] = jnp.zeros_like(token)

    out_shape, out_specs = [], []
    for grp in groups:
        out_shape += [pltpu.SemaphoreType.DMA((npeer * len(grp),))] * 2
        out_specs += [_SEM, _SEM]
    for idx in range(per):
        out_shape += [pltpu.HBM(p[idx].shape, p[idx].dtype) for p in flat]
        out_specs += [_HBM] * na
    out_shape.append(jax.ShapeDtypeStruct((SUBLANES, 128), F32))
    out_specs.append(pl.BlockSpec(memory_space=pltpu.VMEM))
    args = [pltpu.with_memory_space_constraint(p[idx], pltpu.HBM) for idx in range(per) for p in flat]
    res = pl.pallas_call(body, out_shape=tuple(out_shape), in_specs=[_HBM] * (per * na), out_specs=tuple(out_specs),
                         input_output_aliases={i: 2 * ng + i for i in range(per * na)},
                         compiler_params=pltpu.CompilerParams(has_side_effects=_EFFECT), name=name)(*args)
    thru = res[2 * ng:2 * ng + per * na]
    out, ai = [], 0
    for g, grp in enumerate(groups):
        srcs = list(thru[ai:ai + len(grp)]) if scatter else []
        zones = list(thru[(per - 1) * na + ai:(per - 1) * na + ai + len(grp)])
        out.append(((res[2 * g], res[2 * g + 1]), srcs, zones))
        ai += len(grp)
    return out, res[-1]


def exchange_wait(name, group, after, axes, scatter):
    (send_sems, recv_sems), srcs, zones = group
    n, ns = len(zones), len(srcs)
    npeer = len(_flips(axes))

    def body(*refs):
        z_refs = refs[ns:ns + n]
        ssem, rsem = refs[ns + n], refs[ns + n + 1]
        _, peers = _peers(axes)
        for k in range(n):
            for j, peer in enumerate(peers):
                part = z_refs[k].at[j if scatter else _slot(peer, axes)]
                copy = pltpu.make_async_remote_copy(
                    src_ref=part, dst_ref=part, send_sem=ssem.at[k * npeer + j], recv_sem=rsem.at[k * npeer + j],
                    device_id=peer, device_id_type=pl.DeviceIdType.MESH)
                copy.wait_send()
                copy.wait_recv()

    ops = list(srcs) + list(zones)
    out_shape = tuple(pltpu.HBM(a.shape, a.dtype) for a in ops)
    res = pl.pallas_call(body, out_shape=out_shape, in_specs=[_HBM] * len(ops) + [_SEM, _SEM, pl.BlockSpec(memory_space=pl.ANY)],
                         out_specs=(_HBM,) * len(ops), input_output_aliases={i: i for i in range(len(ops))},
                         compiler_params=pltpu.CompilerParams(has_side_effects=_EFFECT), name=name)(*ops, send_sems, recv_sems, after)
    return list(res[:ns]), list(res[ns:])


def _pair_exchange(name, ins, in_specs, n_steps, tile, fn_send, fn_out, out_shape, out_spec, prefetch=None, wire=F32):
    n_in = len(ins)

    def body(*refs):
        if prefetch is not None:
            refs = refs[1:]
        in_refs, o_ref = refs[:n_in], refs[n_in]
        send_buf, recv_buf, send_sems, recv_sems, credit = refs[n_in + 1:]
        i = pl.program_id(0)
        slot = lax.rem(i, 2)
        c = lax.axis_index("c")
        sibling = (lax.axis_index("x"), lax.axis_index("y"), 1 - c)
        vals = [r[...] for r in in_refs]
        send_buf[slot] = fn_send(*vals, c).astype(wire)

        @pl.when(i >= 2)
        def _():
            pl.semaphore_wait(credit, 1)

        copy = pltpu.make_async_remote_copy(
            src_ref=send_buf.at[slot], dst_ref=recv_buf.at[slot], send_sem=send_sems.at[slot], recv_sem=recv_sems.at[slot],
            device_id=sibling, device_id_type=pl.DeviceIdType.MESH)
        copy.start()
        copy.wait_recv()
        o_ref[...] = fn_out(*vals, recv_buf[slot], c).astype(o_ref.dtype)
        copy.wait_send()

        @pl.when(i < n_steps - 2)
        def _():
            pl.semaphore_signal(credit, inc=1, device_id=sibling, device_id_type=pl.DeviceIdType.MESH)

    scratch = [pltpu.VMEM((2,) + tile, wire), pltpu.VMEM((2,) + tile, wire), pltpu.SemaphoreType.DMA((2,)),
               pltpu.SemaphoreType.DMA((2,)), pltpu.SemaphoreType.REGULAR]
    if prefetch is None:
        return pl.pallas_call(body, out_shape=out_shape, grid=(n_steps,), in_specs=in_specs, out_specs=out_spec,
                              scratch_shapes=scratch, compiler_params=_params(("arbitrary",)), name=name)(*ins)
    grid_spec = pltpu.PrefetchScalarGridSpec(num_scalar_prefetch=1, grid=(n_steps,), in_specs=in_specs, out_specs=out_spec,
                                             scratch_shapes=scratch)
    return pl.pallas_call(body, out_shape=out_shape, grid_spec=grid_spec, compiler_params=_params(("arbitrary",)),
                          name=name)(prefetch, *ins)


def _tile_rows(rows, cols):
    return _row_tile(rows, max(2 * SUBLANES, (3 << 19) // (4 * cols)), 2 * SUBLANES)


def reduce_cores(name, g):
    _, m, cols = g.shape
    tr = _tile_rows(m, cols)

    def fn_send(g0, g1, c):
        return jnp.where(c == 0, g1, g0)

    def fn_out(g0, g1, got, c):
        return jnp.where(c == 0, g0, g1) + got.astype(F32)

    return _pair_exchange(
        name, [g, g], [pl.BlockSpec((None, tr, cols), lambda i: (0, i, 0)), pl.BlockSpec((None, tr, cols), lambda i: (1, i, 0))],
        m // tr, (tr, cols), fn_send, fn_out, jax.ShapeDtypeStruct((m, cols), BF16), pl.BlockSpec((tr, cols), lambda i: (i, 0)),
        wire=BF16)


def sum_and_share(name, own, parts, my):
    n, r, cols = parts.shape
    tr = _tile_rows(r, cols)

    def total(o, p):
        acc = o.astype(F32)
        for j in range(n):
            acc = acc + p[j].astype(F32)
        return acc

    def fn_send(o, p, c):
        return total(o, p)

    def fn_out(o, p, got, c):
        mine = total(o, p)
        return jnp.stack([jnp.where(c == 0, mine, got), jnp.where(c == 0, got, mine)])

    return _pair_exchange(
        name, [own, parts], [pl.BlockSpec((None, tr, cols), lambda i, my_ref: (my_ref[0], i, 0)), pl.BlockSpec((n, tr, cols), lambda i, my_ref: (0, i, 0))],
        r // tr, (tr, cols), fn_send, fn_out, jax.ShapeDtypeStruct((2, r, cols), F32),
        pl.BlockSpec((2, tr, cols), lambda i, my_ref: (0, i, 0)), prefetch=my)


def _block_diag(blocks):
    g, r, c = blocks.shape
    eye = jnp.eye(g, dtype=blocks.dtype)
    return (blocks[:, :, None, :] * eye[:, None, :, None]).reshape(g * r, g * c)


def _diag_blocks(mat, g):
    r, c = mat.shape[0] // g, mat.shape[1] // g
    eye = jnp.eye(g, dtype=mat.dtype)
    return (mat.reshape(g, r, g, c) * eye[:, None, :, None]).sum(axis=2)


def _halves(gfull, shards):
    rows, cols = gfull.shape
    return gfull.reshape(shards, 2, rows // shards // 2, cols).transpose(1, 0, 2, 3)


def _step(inp):
    x = inp['x'][0]
    target = inp['loss_target'][0]
    rows, d = x.shape
    depth = inp['w_in'].shape[0]
    mix_w = d // 4
    n_state = S5_GROUPS * S5_STATE
    ffn_half = inp['ffn_w_up'].shape[2]
    tm = min(512, rows)
    tc = min(256, rows)
    tl = min(512, rows)
    xy = ("x", "y")

    my_chip = (2 * lax.axis_index("x") + lax.axis_index("y")).astype(jnp.int32).reshape(1)
    zones = {}
    for l in range(depth):
        for nme in BIG:
            if nme == 'ffn_w_up':
                zones[(nme, l)] = place_tile(f"place_{nme}{l}", inp[nme][l].astype(BF16)[None], 0, my_chip)
            else:
                zones[(nme, l)] = place_tile(f"place_{nme}{l}", inp[nme], l, my_chip)
    small_keys = [(nme, None) for nme in SMALL_SHARDED]
    zones.update(zip(small_keys, place_own("place_small", [inp[nme] for nme in SMALL_SHARDED], xy, False)))
    group_keys = []
    for l in range(depth):
        group_keys += [[('w_in', l)] + (small_keys if l == 0 else []),
                       [('w_out', l), ('s5_w_glu', l), ('cv_w_pw', l)], [('ffn_w_up', l)], [('ffn_w_down', l)]]
    gather_groups, gather_token = exchange_start("gather_start", [[zones[key] for key in grp] for grp in group_keys], xy, False)

    def gathered(gi, after):
        return dict(zip(group_keys[gi], exchange_wait(f"gather_wait{gi}", gather_groups[gi], after, xy, False)[1]))

    def full_small(g):
        return g.transpose(1, 2, 0, 3).reshape(g.shape[1], g.shape[2], 4 * g.shape[3])

    gsum = jnp.repeat(jnp.eye(128, dtype=F32)[:S5_GROUPS], S5_STATE, axis=0)

    saved = []
    grads = {nme: [None] * depth for nme in WEIGHTS}
    xcur = x
    for l in range(depth):
        vec = lambda a: a[l].reshape(1, -1)
        gain = vec(inp['norm_mix_g']) + (gather_token[0, 0] if l == 0 else 0.0)
        h = rms_fwd(f"rms_mix{l}", xcur, gain, tm)
        got = gathered(4 * l, h)
        w_in = got[('w_in', l)]
        if l == 0:
            cv_w_dw, lru_w_conv, ffn_w_dw = (full_small(got[(nme, None)]) for nme in ('cv_w_dw', 'lru_w_conv', 'ffn_w_dw'))
        ncol = w_in.shape[2]

        lam_re, lam_im = vec(inp['s5_lam_re']), vec(inp['s5_lam_im'])
        log_step = jnp.broadcast_to(inp['s5_log_step'][l][:, None], (S5_GROUPS, S5_STATE)).reshape(1, n_state)
        b_re = _block_diag(inp['s5_b_re'][l].transpose(0, 2, 1))
        b_im = _block_diag(inp['s5_b_im'][l].transpose(0, 2, 1))
        c_cat = jnp.stack([_block_diag(inp['s5_c_re'][l].transpose(0, 2, 1)),
                           -_block_diag(inp['s5_c_im'][l].transpose(0, 2, 1))]).astype(BF16)
        a_bar, b_bar = s5_param_fwd(f"s5_param_fwd{l}", lam_re, lam_im, log_step, b_re, b_im)
        w_r = _block_diag(inp['lru_w_r'][l]).astype(BF16)
        w_i = _block_diag(inp['lru_w_i'][l]).astype(BF16)
        pool_bd = _block_diag(inp['pool_w'][l]).astype(BF16)
        gate_pars = [w_r, w_i, vec(inp['lru_b_r']), vec(inp['lru_b_i']), vec(inp['lru_lam'])]

        proj = _mm(f"proj{l}", h, w_in, jax.ShapeDtypeStruct((rows, 4 * ncol), F32), (4, rows // tm),
                   pl.BlockSpec((tm, d), lambda j, i: (i, 0)), pl.BlockSpec((None, d, ncol), lambda j, i: (j, 0, 0)),
                   pl.BlockSpec((tm, ncol), lambda j, i: (i, j)), NN)
        proj3 = proj.reshape(1, rows, 4 * ncol)
        nh = n_state // 2
        bu = _mm(f"s5_bu{l}", proj, b_bar, jax.ShapeDtypeStruct((2, rows, n_state), F32), (rows // tm, 2, 2),
                 pl.BlockSpec((tm, mix_w), lambda i, c, n: (i, 0)), pl.BlockSpec((None, mix_w, nh), lambda i, c, n: (c, 0, n)),
                 pl.BlockSpec((None, tm, nh), lambda i, c, n: (c, i, n)), NN)
        z = s5_scan_fwd(f"s5_scan{l}", bu, a_bar)
        y_ssm = _mm(f"s5_read{l}", z, c_cat, jax.ShapeDtypeStruct((rows, mix_w), F32), (rows // tm, 4),
                    pl.BlockSpec((None, tm, nh), lambda i, k: (k // 2, i, k % 2)),
                    pl.BlockSpec((None, nh, mix_w), lambda i, k: (k // 2, k % 2, 0)),
                    pl.BlockSpec((tm, mix_w), lambda i, k: (i, 0)), NN, k_axis=1)
        (h0,) = _rowwise(f"cv_glu{l}", _glu, [(proj, 1, mix_w), (proj, 2, mix_w)], [], 1, [(mix_w, F32)], tm)
        h1 = dwconv_fwd(f"cv_conv{l}", h0.reshape(1, rows, mix_w), 0, mix_w, cv_w_dw[l][None], vec(inp['cv_b_dw'])[None],
                        CV_TAPS, tc)[0]
        xc = dwconv_fwd(f"lru_conv{l}", proj3, 3, mix_w, lru_w_conv[l][None], vec(inp['lru_b_conv'])[None], LRU_TAPS, tc)[0]
        a_t, b_t = _rowwise(f"lru_gate{l}", _lru_gate, [(xc, 0, mix_w)], gate_pars, 2, [(mix_w, F32), (mix_w, F32)], tm)
        hseq = lru_scan_fwd(f"lru_scan{l}", a_t, b_t)
        dgp = pool_fwd(f"pool{l}", proj, 5, tc)
        got = gathered(4 * l + 1, proj)
        w_out = got[('w_out', l)].reshape(d, d)
        w_glu, w_pw = got[('s5_w_glu', l)].reshape(mix_w, mix_w), got[('cv_w_pw', l)].reshape(mix_w, mix_w)
        post_pars = [vec(inp['s5_d']), w_glu, vec(inp['s5_b_glu']), vec(inp['cv_ln_g']), vec(inp['cv_ln_b']), w_pw,
                     vec(inp['cv_b_pw']), pool_bd, vec(inp['pool_scale'])]
        post_rows = [(y_ssm, 0, mix_w), (proj, 0, mix_w), (h1, 0, mix_w), (hseq, 0, mix_w), (proj, 4, mix_w), (dgp, 0, mix_w)]
        (mixed,) = _rowwise(f"mix_post{l}", _mix_post, post_rows, post_pars, 1, [(d, BF16)], tm)
        x1 = _mm(f"out_proj{l}", mixed, w_out, jax.ShapeDtypeStruct((rows, d), F32), (2, rows // tm),
                 pl.BlockSpec((tm, d), lambda j, i: (i, 0)), pl.BlockSpec((d, d // 2), lambda j, i: (0, j)),
                 pl.BlockSpec((tm, d // 2), lambda j, i: (i, j)), NN,
                 add=xcur, add_spec=pl.BlockSpec((tm, d // 2), lambda j, i: (i, j)))

        h2 = rms_fwd(f"rms_ffn{l}", x1, vec(inp['norm_ffn_g']), tm)
        tu = min(256, rows)
        w_up = gathered(4 * l + 2, x1)[('ffn_w_up', l)]
        up = _mm(f"ffn_up{l}", h2, w_up, jax.ShapeDtypeStruct((4, rows, ffn_half), F32), (4, rows // tu),
                 pl.BlockSpec((tu, d), lambda k, i: (i, 0)), pl.BlockSpec((None, d, ffn_half), lambda k, i: (k, 0, 0)),
                 pl.BlockSpec((None, tu, ffn_half), lambda k, i: (k, i, 0)), NN)
        w_dw = ffn_w_dw[l].reshape(FFN_TAPS, 2, ffn_half).transpose(1, 0, 2)
        b_dw = inp['ffn_b_dw'][l].reshape(2, 1, ffn_half)
        act = ffn_gate_fwd(f"ffn_gate{l}", up, w_dw, b_dw, tc)
        w_down = gathered(4 * l + 3, up)[('ffn_w_down', l)].reshape(2, ffn_half, d)
        x2 = _mm(f"ffn_down{l}", act, w_down, jax.ShapeDtypeStruct((rows, d), F32), (rows // tm, 4),
                 pl.BlockSpec((2, tm, ffn_half), lambda i, j: (0, i, 0)), pl.BlockSpec((2, ffn_half, d // 4), lambda i, j: (0, 0, j)),
                 pl.BlockSpec((tm, d // 4), lambda i, j: (i, j)), NN, inner=("lead", 2),
                 add=x1, add_spec=pl.BlockSpec((tm, d // 4), lambda i, j: (i, j)))
        saved.append(dict(x=xcur, h=h, proj=proj, z=z, y_ssm=y_ssm, h0=h0, h1=h1, xc=xc, a_t=a_t, hseq=hseq, dgp=dgp,
                          mixed=mixed, x1=x1, h2=h2, up=up, act=act, w_in=w_in, w_out=w_out, w_up=w_up, w_down=w_down,
                          a_bar=a_bar, b_bar=b_bar, c_cat=c_cat, post_pars=post_pars, gate_pars=gate_pars, w_dw=w_dw, b_dw=b_dw,
                          s5=(lam_re, lam_im, log_step, b_re, b_im), cv_w=cv_w_dw[l][None], lru_w=lru_w_conv[l][None]))
        xcur = x2

    loss_row, dx, dg_final = final_loss("final_loss", xcur, inp['norm_final_g'].reshape(1, d), target, tm)
    grads['norm_final_g'] = dg_final.reshape(d)

    big_g = {nme: [None] * depth for nme in BIG}
    reduce_groups = []

    def start_reduce(tag, keys):
        pieces = []
        for nme, lyr in keys:
            g = big_g[nme][lyr]
            if nme == 'ffn_w_down':
                g = g.reshape(2, 4, ffn_half // 2, d // 2)
            pieces.append(reduce_cores(f"reduce_cores_{nme}{lyr}", g.reshape(2, -1, g.shape[-1])).reshape(g.shape[1:]))
        landing = [lax.empty((3,) + p.shape[1:], p.dtype) for p in pieces]
        groups, token = exchange_start(f"reduce_start_{tag}", [list(zip(pieces, landing))], xy, True)
        reduce_groups.append((tag, keys, groups[0]))
        return token

    for l in reversed(range(depth)):
        s = saved[l]
        ncol = s['w_in'].shape[2]
        nh = n_state // 2
        tu = min(256, rows)
        dact = _mm(f"d_act{l}", dx, s['w_down'], jax.ShapeDtypeStruct((2, rows, ffn_half), F32), (2, rows // tu),
                   pl.BlockSpec((tu, d), lambda k, i: (i, 0)), pl.BlockSpec((None, ffn_half, d), lambda k, i: (k, 0, 0)),
                   pl.BlockSpec((None, tu, ffn_half), lambda k, i: (k, i, 0)), NT)
        tn = d // 4
        tk = min(1024, rows)
        big_g['ffn_w_down'][l] = _mm(
            f"dw_down{l}", s['act'], dx, jax.ShapeDtypeStruct((2, 2, ffn_half, d // 2), F32), (2, 4, rows // tk),
            pl.BlockSpec((None, tk, ffn_half), lambda hh, n, k: (hh, k, 0)), pl.BlockSpec((tk, tn), lambda hh, n, k: (k, n)),
            pl.BlockSpec((None, None, ffn_half, tn), lambda hh, n, k: (n // 2, hh, 0, n % 2)), TN, k_axis=2)
        dup, dw_dw, db_dw = ffn_gate_bwd(f"ffn_gate_bwd{l}", s['up'], dact, s['w_dw'], s['b_dw'], tc)
        grads['ffn_w_dw'][l] = dw_dw.transpose(1, 0, 2).reshape(FFN_TAPS, 2 * ffn_half)
        grads['ffn_b_dw'][l] = db_dw.reshape(2 * ffn_half)
        dup = dup.reshape(4, rows, ffn_half)
        tm2 = min(1024, rows)
        dh2 = _mm(f"d_h2{l}", dup, s['w_up'], jax.ShapeDtypeStruct((rows, d), F32), (rows // tm2, 2, 4),
                  pl.BlockSpec((None, tm2, ffn_half), lambda i, j, k: (k, i, 0)), pl.BlockSpec((None, d // 2, ffn_half), lambda i, j, k: (k, j, 0)),
                  pl.BlockSpec((tm2, d // 2), lambda i, j, k: (i, j)), NT, k_axis=2)
        tmm = d // 4
        big_g['ffn_w_up'][l] = _mm(
            f"dw_up{l}", dup, s['h2'], jax.ShapeDtypeStruct((2, 4, ffn_half, d // 2), F32), (4, 4, rows // tk),
            pl.BlockSpec((None, tk, ffn_half), lambda k4, n, k: (k4, k, 0)), pl.BlockSpec((tk, tn), lambda k4, n, k: (k, n)),
            pl.BlockSpec((None, None, ffn_half, tn), lambda k4, n, k: (n // 2, k4, 0, n % 2)), TN, k_axis=2)
        token = start_reduce(f"ffn{l}", [('ffn_w_down', l), ('ffn_w_up', l)])
        dx1, dg = rms_bwd(f"rms_ffn_bwd{l}", s['x1'], inp['norm_ffn_g'][l].reshape(1, d) + token[0, 0], dh2, dx, tm)
        grads['norm_ffn_g'][l] = dg.reshape(d)
        dmixed = _mm(f"d_mixed{l}", dx1, s['w_out'], jax.ShapeDtypeStruct((rows, d), F32), (4, rows // tm),
                     pl.BlockSpec((tm, d), lambda j, i: (i, 0)), pl.BlockSpec((d // 4, d), lambda j, i: (j, 0)),
                     pl.BlockSpec((tm, d // 4), lambda j, i: (i, j)), NT)
        tq = mix_w // 2
        big_g['w_out'][l] = _mm(
            f"dw_out{l}", s['mixed'], dx1, jax.ShapeDtypeStruct((2, 4, tq, d), F32), (8, rows // tk),
            pl.BlockSpec((tk, tq), lambda t, k: (k, t)), pl.BlockSpec((tk, d), lambda t, k: (k, 0)),
            pl.BlockSpec((None, None, tq, d), lambda t, k: (t % 2, t // 2, 0, 0)), TN, k_axis=1)
        post_rows = [(s['y_ssm'], 0, mix_w), (s['proj'], 0, mix_w), (s['h1'], 0, mix_w), (s['hseq'], 0, mix_w),
                     (s['proj'], 4, mix_w), (s['dgp'], 0, mix_w), (dmixed, 0, d)]
        res = _rowwise(f"mix_post_bwd{l}", _mix_post, post_rows, s['post_pars'], 1, [(mix_w, F32)] * 6, tm, with_grads=True)
        dy_ssm, du_dir, dh1, dhseq, dlru_g, ddgp = res[:6]
        dd, dwglu, dbglu, dlng, dlnb, dwpw, dbpw, dpoolbd, dscale = res[6:]
        grads['s5_d'][l], grads['s5_b_glu'][l] = dd.reshape(mix_w), dbglu.reshape(mix_w)
        grads['cv_ln_g'][l], grads['cv_ln_b'][l], grads['cv_b_pw'][l] = dlng.reshape(mix_w), dlnb.reshape(mix_w), dbpw.reshape(mix_w)
        grads['pool_w'][l] = _diag_blocks(dpoolbd, len(POOL_WINDOWS))
        grads['pool_scale'][l] = dscale.reshape(mix_w)
        big_g['s5_w_glu'][l] = _halves(dwglu, 4)
        big_g['cv_w_pw'][l] = _halves(dwpw, 4)
        dz = _mm(f"s5_dz{l}", dy_ssm, s['c_cat'], jax.ShapeDtypeStruct((2, rows, n_state), F32), (rows // tm, 2, 2),
                 pl.BlockSpec((tm, mix_w), lambda i, c, n: (i, 0)), pl.BlockSpec((None, nh, mix_w), lambda i, c, n: (c, n, 0)),
                 pl.BlockSpec((None, tm, nh), lambda i, c, n: (c, i, n)), NT)
        dccat = _mm(f"s5_dc{l}", s['z'], dy_ssm, jax.ShapeDtypeStruct((2, n_state, mix_w), F32), (2, n_state // mix_w, rows // tl),
                    pl.BlockSpec((None, tl, mix_w), lambda c, m, k: (c, k, m)), pl.BlockSpec((tl, mix_w), lambda c, m, k: (k, 0)),
                    pl.BlockSpec((None, mix_w, mix_w), lambda c, m, k: (c, m, 0)), TN, k_axis=2)
        grads['s5_c_re'][l] = _diag_blocks(dccat[0], S5_GROUPS).transpose(0, 2, 1)
        grads['s5_c_im'][l] = -_diag_blocks(dccat[1], S5_GROUPS).transpose(0, 2, 1)
        lam, da_bar = s5_scan_bwd(f"s5_scan_bwd{l}", dz, s['z'], s['a_bar'])
        du = _mm(f"s5_du{l}", lam, s['b_bar'], jax.ShapeDtypeStruct((rows, mix_w), F32), (rows // tm, 4),
                 pl.BlockSpec((None, tm, nh), lambda i, k: (k // 2, i, k % 2)), pl.BlockSpec((None, mix_w, nh), lambda i, k: (k // 2, 0, k % 2)),
                 pl.BlockSpec((tm, mix_w), lambda i, k: (i, 0)), NT, k_axis=1,
                 add=du_dir, add_spec=pl.BlockSpec((tm, mix_w), lambda i, k: (i, 0)))
        dbbar = _mm(f"s5_db{l}", s['proj'], lam, jax.ShapeDtypeStruct((2, mix_w, n_state), F32), (2, 2, rows // tl),
                    pl.BlockSpec((tl, mix_w), lambda c, n, k: (k, 0)), pl.BlockSpec((None, tl, nh), lambda c, n, k: (c, k, n)),
                    pl.BlockSpec((None, mix_w, nh), lambda c, n, k: (c, 0, n)), TN, k_axis=2)
        dlr, dli, dls, dbre, dbim = s5_param_bwd(f"s5_param_bwd{l}", *s['s5'], da_bar, dbbar, gsum)
        grads['s5_lam_re'][l] = dlr.reshape(S5_GROUPS, S5_STATE)
        grads['s5_lam_im'][l] = dli.reshape(S5_GROUPS, S5_STATE)
        grads['s5_log_step'][l] = dls[0, :S5_GROUPS]
        grads['s5_b_re'][l] = _diag_blocks(dbre, S5_GROUPS).transpose(0, 2, 1)
        grads['s5_b_im'][l] = _diag_blocks(dbim, S5_GROUPS).transpose(0, 2, 1)
        dh0, dw_cv, db_cv = dwconv_bwd(f"cv_conv_bwd{l}", dh1.reshape(1, rows, mix_w), s['h0'].reshape(1, rows, mix_w), 0, mix_w,
                                       s['cv_w'], CV_TAPS, tc)
        grads['cv_w_dw'][l], grads['cv_b_dw'][l] = dw_cv[0], db_cv.reshape(mix_w)
        dv, dgg = _rowwise(f"cv_glu_bwd{l}", _glu, [(s['proj'], 1, mix_w), (s['proj'], 2, mix_w), (dh0[0], 0, mix_w)], [], 1,
                           [(mix_w, F32)] * 2, tm, with_grads=True)
        da_t, db_t = lru_scan_bwd(f"lru_scan_bwd{l}", dhseq, s['a_t'], s['hseq'])
        res = _rowwise(f"lru_gate_bwd{l}", _lru_gate, [(s['xc'], 0, mix_w), (da_t, 0, mix_w), (db_t, 0, mix_w)], s['gate_pars'], 2,
                       [(mix_w, F32)], tm, with_grads=True)
        dxc, dwr, dwi, dbr, dbi, dlam = res
        grads['lru_w_r'][l], grads['lru_w_i'][l] = _diag_blocks(dwr, LRU_HEADS), _diag_blocks(dwi, LRU_HEADS)
        grads['lru_b_r'][l], grads['lru_b_i'][l], grads['lru_lam'][l] = dbr.reshape(mix_w), dbi.reshape(mix_w), dlam.reshape(mix_w)
        dlx, dw_lc, db_lc = dwconv_bwd(f"lru_conv_bwd{l}", dxc.reshape(1, rows, mix_w), s['proj'].reshape(1, rows, 4 * ncol), 3, mix_w,
                                       s['lru_w'], LRU_TAPS, tc)
        grads['lru_w_conv'][l], grads['lru_b_conv'][l] = dw_lc[0], db_lc.reshape(mix_w)
        dpx = pool_bwd(f"pool_bwd{l}", ddgp, tc)
        dproj = jnp.concatenate([du, dv, dgg, dlx[0], dlru_g, dpx], axis=-1)
        dh = _mm(f"d_h{l}", dproj, s['w_in'], jax.ShapeDtypeStruct((rows, d), F32), (4, rows // tm),
                 pl.BlockSpec((tm, 4 * ncol), lambda j, i: (i, 0)), pl.BlockSpec((4, d // 4, ncol), lambda j, i: (0, j, 0)),
                 pl.BlockSpec((tm, d // 4), lambda j, i: (i, j)), NT, inner=("cols", 4))
        tk2 = min(2048, rows)
        big_g['w_in'][l] = _mm(
            f"dw_in{l}", s['h'], dproj, jax.ShapeDtypeStruct((2, 4, d // 2, ncol), F32), (4, 4, rows // tk2),
            pl.BlockSpec((tk2, tmm), lambda k4, m, k: (k, m)), pl.BlockSpec((tk2, ncol), lambda k4, m, k: (k, k4)),
            pl.BlockSpec((None, None, tmm, ncol), lambda k4, m, k: (m // 2, k4, m % 2, 0)), TN, k_axis=2)
        token = start_reduce(f"mix{l}", [('w_out', l), ('s5_w_glu', l), ('cv_w_pw', l), ('w_in', l)])
        dx, dg = rms_bwd(f"rms_mix_bwd{l}", s['x'], inp['norm_mix_g'][l].reshape(1, d) + token[0, 0], dh, dx1, tm)
        grads['norm_mix_g'][l] = dg.reshape(d)

    small = [nme for nme in WEIGHTS if nme not in BIG]
    full_g = {nme: (grads[nme] if nme == 'norm_final_g' else jnp.stack(grads[nme])) for nme in small}
    flat = jnp.concatenate([full_g[nme].reshape(-1) for nme in small])
    packed = jnp.pad(flat, (0, (-flat.shape[0]) % (128 * 64))).reshape(1, -1, 128)
    my_device = (4 * lax.axis_index("x") + 2 * lax.axis_index("y") + lax.axis_index("c")).astype(jnp.int32).reshape(1)
    small_zone = place_tile("place_small_grads", packed, 0, my_device, slots=8, dtype=F32)
    (small_group,), small_token = exchange_start("small_start", [[small_zone]], ("x", "y", "c"), False)

    t_full = {}
    for tag, keys, group in reduce_groups:
        pieces, parts = exchange_wait(f"reduce_wait_{tag}", group, small_token, xy, True)
        for key, own, got in zip(keys, pieces, parts):
            t_full[key] = sum_and_share(f"share_cores_{key[0]}{key[1]}", own, got, my_chip)

    outs = {}
    tiles = {'w_in': 256, 'w_out': 128, 'ffn_w_up': 128, 'ffn_w_down': 256, 's5_w_glu': 64, 'cv_w_pw': 64}
    for nme in BIG:
        g0, g1 = t_full[(nme, 0)], t_full[(nme, 1)]
        if nme == 'ffn_w_up':
            res = adamw_sharded(f"adamw_{nme}", *(jnp.swapaxes(inp[p + nme], 1, 2) for p in ('', 'm_', 'v_')), g0, g1, True, tiles[nme])
            outs[nme] = tuple(jnp.swapaxes(r, 1, 2) for r in res)
        else:
            outs[nme] = adamw_sharded(f"adamw_{nme}", inp[nme], inp['m_' + nme], inp['v_' + nme], g0, g1, nme == 'ffn_w_down', tiles[nme])

    (g8,) = exchange_wait("small_wait", small_group, outs[BIG[-1]][1], ("x", "y", "c"), False)[1]
    gsum_small = sum_lead("sum_small", g8, 64).reshape(-1)
    red, off = {}, 0
    for nme in small:
        g = gsum_small[off:off + full_g[nme].size].reshape(full_g[nme].shape)
        off += full_g[nme].size
        if nme in SMALL_SHARDED:
            width = inp[nme].shape[2]
            g = lax.dynamic_slice_in_dim(g, my_chip[0] * width, width, axis=2)
        red[nme] = g

    def pack(tree):
        f = jnp.concatenate([tree[nme].reshape(-1) for nme in small])
        return jnp.pad(f, (0, (-f.shape[0]) % (128 * 64))).reshape(-1, 128)

    pd, pm, pv = adamw_flat("adamw_small", pack({n_: inp[n_] for n_ in small}), pack(red), pack({n_: inp['m_' + n_] for n_ in small}),
                            pack({n_: inp['v_' + n_] for n_ in small}), 64)
    off = 0
    for nme in small:
        size, shape = inp[nme].size, inp[nme].shape
        outs[nme] = (red[nme],) + tuple(p.reshape(-1)[off:off + size].reshape(shape) for p in (pd, pm, pv))
        off += size

    loss = lax.psum(loss_row[0, 0], ("x", "y", "c"))
    result = [loss, dx[None]]
    for part in range(4):
        result += [outs[nme][part] for nme in WEIGHTS]
    return tuple(result)


def kernel(x, norm_mix_g, w_in, s5_lam_re, s5_lam_im, s5_log_step, s5_b_re, s5_b_im, s5_c_re, s5_c_im, s5_d, s5_w_glu, s5_b_glu, cv_w_dw, cv_b_dw, cv_ln_g, cv_ln_b, cv_w_pw, cv_b_pw, lru_w_conv, lru_b_conv, lru_w_r, lru_b_r, lru_w_i, lru_b_i, lru_lam, pool_w, pool_scale, w_out, norm_ffn_g, ffn_w_up, ffn_w_dw, ffn_b_dw, ffn_w_down, norm_final_g, loss_target, m_norm_mix_g, m_w_in, m_s5_lam_re, m_s5_lam_im, m_s5_log_step, m_s5_b_re, m_s5_b_im, m_s5_c_re, m_s5_c_im, m_s5_d, m_s5_w_glu, m_s5_b_glu, m_cv_w_dw, m_cv_b_dw, m_cv_ln_g, m_cv_ln_b, m_cv_w_pw, m_cv_b_pw, m_lru_w_conv, m_lru_b_conv, m_lru_w_r, m_lru_b_r, m_lru_w_i, m_lru_b_i, m_lru_lam, m_pool_w, m_pool_scale, m_w_out, m_norm_ffn_g, m_ffn_w_up, m_ffn_w_dw, m_ffn_b_dw, m_ffn_w_down, m_norm_final_g, v_norm_mix_g, v_w_in, v_s5_lam_re, v_s5_lam_im, v_s5_log_step, v_s5_b_re, v_s5_b_im, v_s5_c_re, v_s5_c_im, v_s5_d, v_s5_w_glu, v_s5_b_glu, v_cv_w_dw, v_cv_b_dw, v_cv_ln_g, v_cv_ln_b, v_cv_w_pw, v_cv_b_pw, v_lru_w_conv, v_lru_b_conv, v_lru_w_r, v_lru_b_r, v_lru_w_i, v_lru_b_i, v_lru_lam, v_pool_w, v_pool_scale, v_w_out, v_norm_ffn_g, v_ffn_w_up, v_ffn_w_dw, v_ffn_b_dw, v_ffn_w_down, v_norm_final_g):
    inp = dict(locals())
    return _step(inp)
```

```python
import functools

import jax
import jax.numpy as jnp
from jax import lax
from jax.experimental import pallas as pl
from jax.experimental.pallas import tpu as pltpu

F32 = jnp.float32
BF16 = jnp.bfloat16

VMEM_LIMIT_BYTES = 56 * 1024 * 1024
SUBLANES = 8

EPS = 1e-6
S5_GROUPS, S5_STATE, S5_GROUP_CH = 32, 64, 16
LRU_HEADS, LRU_C = 8, 8.0
POOL_WINDOWS = (2, 4, 8, 16)
CV_TAPS, LRU_TAPS, FFN_TAPS = 31, 4, 3
SCAN_CHUNK = 64
GELU_K0, GELU_K1 = 0.7978845608028654, 0.044715

ADAM_LR, ADAM_B1, ADAM_B2, ADAM_EPS, ADAM_WD, ADAM_STEP = 0.001, 0.9, 0.999, 1e-08, 0.01, 10

NN = ((1,), (0,))
NT = ((1,), (1,))
TN = ((0,), (0,))

WEIGHTS = ['norm_mix_g', 'w_in', 's5_lam_re', 's5_lam_im', 's5_log_step', 's5_b_re', 's5_b_im', 's5_c_re', 's5_c_im',
           's5_d', 's5_w_glu', 's5_b_glu', 'cv_w_dw', 'cv_b_dw', 'cv_ln_g', 'cv_ln_b', 'cv_w_pw', 'cv_b_pw',
           'lru_w_conv', 'lru_b_conv', 'lru_w_r', 'lru_b_r', 'lru_w_i', 'lru_b_i', 'lru_lam', 'pool_w', 'pool_scale',
           'w_out', 'norm_ffn_g', 'ffn_w_up', 'ffn_w_dw', 'ffn_b_dw', 'ffn_w_down', 'norm_final_g']
BIG = ('w_in', 'w_out', 'ffn_w_up', 'ffn_w_down', 's5_w_glu', 'cv_w_pw')
SMALL_SHARDED = {'cv_w_dw': 2, 'lru_w_conv': 2, 'ffn_w_dw': 2}


def _params(sem=None):
    if sem is None:
        return pltpu.CompilerParams(vmem_limit_bytes=VMEM_LIMIT_BYTES)
    return pltpu.CompilerParams(dimension_semantics=sem, vmem_limit_bytes=VMEM_LIMIT_BYTES)


def _row_tile(rows, cap, mult=SUBLANES):
    best = mult
    for t in range(mult, min(rows, cap) + 1, mult):
        if rows % t == 0:
            best = t
    return best


def _bdot(a, b, dims=NN):
    return lax.dot_general(a.astype(BF16), b.astype(BF16), (dims, ((), ())), preferred_element_type=F32)


@jax.custom_vjp
def bdot(a, b):
    return _bdot(a, b)


def _bdot_fwd(a, b):
    return _bdot(a, b), (a, b)


def _bdot_bwd(res, g):
    a, b = res
    return _bdot(g, b, NT).astype(a.dtype), _bdot(a, g, TN).astype(b.dtype)


bdot.defvjp(_bdot_fwd, _bdot_bwd)


def _mm(name, a, b, out_sds, grid, a_spec, b_spec, o_spec, dims, k_axis=None, add=None, add_spec=None, inner=None):
    nk = grid[k_axis] if k_axis is not None else 1
    has_add = add is not None
    acc_shape = tuple(d for d in o_spec.block_shape if d is not None)
    acc_in_out = out_sds.dtype == F32

    def product(a_ref, b_ref):
        if inner is None:
            return _bdot(a_ref[...], b_ref[...], dims)
        kind, n = inner
        width = a_ref.shape[-1] // n
        acc = None
        for j in range(n):
            a_j = a_ref[j] if kind == "lead" else a_ref[:, j * width:(j + 1) * width]
            p = _bdot(a_j, b_ref[j], dims)
            acc = p if acc is None else acc + p
        return acc

    def body(*refs):
        a_ref, b_ref = refs[0], refs[1]
        add_ref = refs[2] if has_add else None
        o_ref = refs[3] if has_add else refs[2]
        prod = product(a_ref, b_ref)
        if k_axis is None:
            if has_add:
                prod = prod + add_ref[...]
            o_ref[...] = prod.astype(o_ref.dtype)
        else:
            acc_ref = o_ref if acc_in_out else refs[-1]
            k = pl.program_id(k_axis)

            @pl.when(k == 0)
            def _():
                acc_ref[...] = prod

            @pl.when(k > 0)
            def _():
                acc_ref[...] += prod

            if has_add or not acc_in_out:
                @pl.when(k == nk - 1)
                def _():
                    r = acc_ref[...]
                    if has_add:
                        r = r + add_ref[...]
                    o_ref[...] = r.astype(o_ref.dtype)

    sem = tuple("arbitrary" if d == k_axis else "parallel" for d in range(len(grid)))
    in_specs = [a_spec, b_spec] + ([add_spec] if has_add else [])
    args = (a, b) + ((add,) if has_add else ())
    scratch = [pltpu.VMEM(acc_shape, F32)] if (k_axis is not None and not acc_in_out) else []
    return pl.pallas_call(body, out_shape=out_sds, grid=grid, in_specs=in_specs, out_specs=o_spec,
                          scratch_shapes=scratch, compiler_params=_params(sem), name=name)(*args)


def _rms(x, g):
    return x * lax.rsqrt(jnp.mean(x * x, axis=-1, keepdims=True) + EPS) * g


def rms_fwd(name, x, g, tm):
    rows, d = x.shape

    def body(x_ref, g_ref, o_ref):
        o_ref[...] = _rms(x_ref[...], g_ref[...]).astype(BF16)

    return pl.pallas_call(
        body, out_shape=jax.ShapeDtypeStruct((rows, d), BF16), grid=(rows // tm,),
        in_specs=[pl.BlockSpec((tm, d), lambda i: (i, 0)), pl.BlockSpec((1, d), lambda i: (0, 0))],
        out_specs=pl.BlockSpec((tm, d), lambda i: (i, 0)), compiler_params=_params(("parallel",)), name=name)(x, g)


def rms_bwd(name, x, g, dh, dres, tm):
    rows, d = x.shape

    def body(x_ref, g_ref, dh_ref, dres_ref, dx_ref, dg_ref):
        _, vjp = jax.vjp(_rms, x_ref[...], g_ref[...])
        dx, dg = vjp(dh_ref[...])
        dx_ref[...] = dx + dres_ref[...]

        @pl.when(pl.program_id(0) == 0)
        def _():
            dg_ref[...] = jnp.zeros_like(dg_ref)

        dg_ref[...] += dg

    row = pl.BlockSpec((tm, d), lambda i: (i, 0))
    vec = pl.BlockSpec((1, d), lambda i: (0, 0))
    return pl.pallas_call(
        body, out_shape=(jax.ShapeDtypeStruct((rows, d), F32), jax.ShapeDtypeStruct((1, d), F32)), grid=(rows // tm,),
        in_specs=[row, vec, row, row], out_specs=(row, vec), compiler_params=_params(("arbitrary",)), name=name)(x, g, dh, dres)


def final_loss(name, x, g, target, tm):
    rows, d = x.shape

    def body(x_ref, g_ref, t_ref, l_ref, dx_ref, dg_ref):
        def f(xv, gv):
            e = _rms(xv, gv) - t_ref[...]
            return 0.5 * jnp.sum(jnp.mean(e * e, axis=-1))

        loss, (dx, dg) = jax.value_and_grad(f, argnums=(0, 1))(x_ref[...], g_ref[...])
        dx_ref[...] = dx

        @pl.when(pl.program_id(0) == 0)
        def _():
            l_ref[...] = jnp.zeros_like(l_ref)
            dg_ref[...] = jnp.zeros_like(dg_ref)

        l_ref[...] += jnp.full(l_ref.shape, loss, F32)
        dg_ref[...] += dg

    row = pl.BlockSpec((tm, d), lambda i: (i, 0))
    vec = pl.BlockSpec((1, d), lambda i: (0, 0))
    lspec = pl.BlockSpec((1, 128), lambda i: (0, 0))
    return pl.pallas_call(
        body, out_shape=(jax.ShapeDtypeStruct((1, 128), F32), jax.ShapeDtypeStruct((rows, d), F32), jax.ShapeDtypeStruct((1, d), F32)),
        grid=(rows // tm,), in_specs=[row, vec, row], out_specs=(lspec, row, vec),
        compiler_params=_params(("arbitrary",)), name=name)(x, g, target)


def _rowwise(name, fn, row_ins, par_ins, n_row_out, row_out_dtypes, tm, with_grads=False):
    rows = row_ins[0][0].shape[0]
    n_prim = len(row_ins) - (n_row_out if with_grads else 0)
    n_par = len(par_ins)

    def body(*refs):
        ins = [r[...] for r in refs[:len(row_ins) + n_par]]
        outs = refs[len(row_ins) + n_par:]
        prim, cts, pars = ins[:n_prim], ins[n_prim:len(row_ins)], ins[len(row_ins):]
        if not with_grads:
            res = fn(*prim, *pars)
            for o_ref, r in zip(outs, res):
                o_ref[...] = r.astype(o_ref.dtype)
            return
        _, vjp = jax.vjp(fn, *prim, *[p.astype(F32) for p in pars])
        grads = vjp(tuple(cts))
        for o_ref, gr in zip(outs[:n_prim], grads[:n_prim]):
            o_ref[...] = gr.astype(o_ref.dtype)

        @pl.when(pl.program_id(0) == 0)
        def _():
            for o_ref in outs[n_prim:]:
                o_ref[...] = jnp.zeros_like(o_ref)

        for o_ref, gr in zip(outs[n_prim:], grads[n_prim:]):
            o_ref[...] += gr.astype(F32)

    in_specs = [pl.BlockSpec((tm, w), (lambda i, c=c: (i, c))) for (_, c, w) in row_ins]
    in_specs += [pl.BlockSpec(p.shape, (lambda i, n=p.ndim: (0,) * n)) for p in par_ins]
    args = [a for (a, _, _) in row_ins] + list(par_ins)
    if not with_grads:
        out_shape = tuple(jax.ShapeDtypeStruct((rows, w), dt) for (w, dt) in row_out_dtypes)
        out_specs = tuple(pl.BlockSpec((tm, w), lambda i: (i, 0)) for (w, _) in row_out_dtypes)
        sem = ("parallel",)
    else:
        out_shape = tuple(jax.ShapeDtypeStruct((rows, w), dt) for (w, dt) in row_out_dtypes)
        out_shape += tuple(jax.ShapeDtypeStruct(p.shape, F32) for p in par_ins)
        out_specs = tuple(pl.BlockSpec((tm, w), lambda i: (i, 0)) for (w, _) in row_out_dtypes)
        out_specs += tuple(pl.BlockSpec(p.shape, (lambda i, n=p.ndim: (0,) * n)) for p in par_ins)
        sem = ("arbitrary",)
    return pl.pallas_call(body, out_shape=out_shape, grid=(rows // tm,), in_specs=in_specs, out_specs=out_specs,
                          compiler_params=_params(sem), name=name)(*args)


def _glu(v, g):
    return (v * jax.nn.sigmoid(g),)


def _neg_expm1(z):
    return -jnp.tanh(0.5 * z) * (jnp.exp(z) + 1.0)


def _lru_gate(xc, w_r, w_i, b_r, b_i, lam):
    r = jax.nn.sigmoid(bdot(xc, w_r) + b_r)
    i = jax.nn.sigmoid(bdot(xc, w_i) + b_i)
    log_a = -LRU_C * r * jax.nn.softplus(-lam)
    a = jnp.exp(log_a)
    mult = jnp.sqrt(_neg_expm1(2.0 * log_a))
    return a, mult * (i * xc)


def _layernorm(x, g, b):
    mu = jnp.mean(x, axis=-1, keepdims=True)
    var = jnp.mean(jnp.square(x - mu), axis=-1, keepdims=True)
    return (x - mu) * lax.rsqrt(var + EPS) * g + b


def _mix_post(y_ssm, u, h1, hseq, lru_g, dgp, s5_d, w_glu, b_glu, ln_g, ln_b, w_pw, b_pw, pool_bd, pool_scale):
    y = y_ssm + s5_d * u
    gl = jax.nn.gelu(y, approximate=True)
    out_s5 = gl * jax.nn.sigmoid(bdot(gl, w_glu) + b_glu)
    out_cv = bdot(jax.nn.silu(_layernorm(h1, ln_g, ln_b)), w_pw) + b_pw
    out_lru = hseq * jax.nn.gelu(lru_g, approximate=True)
    out_pool = bdot(dgp, pool_bd) * pool_scale
    return (jnp.concatenate([out_s5, out_cv, out_lru, out_pool], axis=-1),)


def _ffn_act(gc, val):
    return (jax.nn.gelu(gc, approximate=True) * val,)


def ffn_act_fwd(name, gc, up, tm):
    _, rows, c = gc.shape

    def body(g_ref, v_ref, o_ref):
        o_ref[...] = _ffn_act(g_ref[...], v_ref[...])[0].astype(BF16)

    return pl.pallas_call(
        body, out_shape=jax.ShapeDtypeStruct((2, rows, c), BF16), grid=(2, rows // tm),
        in_specs=[pl.BlockSpec((None, tm, c), lambda h, i: (h, i, 0)), pl.BlockSpec((None, tm, c), lambda h, i: (h + 2, i, 0))],
        out_specs=pl.BlockSpec((None, tm, c), lambda h, i: (h, i, 0)),
        compiler_params=_params(("parallel", "parallel")), name=name)(gc, up)


def ffn_act_bwd(name, gc, up, dact, tm):
    _, rows, c = gc.shape

    def body(g_ref, v_ref, d_ref, dg_ref, dv_ref):
        _, vjp = jax.vjp(_ffn_act, g_ref[...], v_ref[...])
        dg, dv = vjp((d_ref[...],))
        dg_ref[...] = dg
        dv_ref[...] = dv.astype(BF16)

    blk = pl.BlockSpec((None, tm, c), lambda h, i: (h, i, 0))
    return pl.pallas_call(
        body, out_shape=(jax.ShapeDtypeStruct((2, rows, c), F32), jax.ShapeDtypeStruct((2, rows, c), BF16)), grid=(2, rows // tm),
        in_specs=[blk, pl.BlockSpec((None, tm, c), lambda h, i: (h + 2, i, 0)), blk], out_specs=(blk, blk),
        compiler_params=_params(("parallel", "parallel")), name=name)(gc, up, dact)


def _halo_rows(taps):
    return -(-(taps - 1) // SUBLANES) * SUBLANES


def dwconv_fwd(name, x, cblk, c, w, b, taps, tm, out_dtype=F32):
    nb = w.shape[0]
    rows = x.shape[1]
    halo = _halo_rows(taps)
    per = tm // halo

    def body(x_ref, h_ref, w_ref, b_ref, o_ref):
        i = pl.program_id(1)
        prev = jnp.where(i > 0, h_ref[...], 0.0)
        ext = jnp.concatenate([prev, x_ref[...]], axis=0)
        acc = jnp.broadcast_to(b_ref[...], (tm, c))
        for k in range(taps):
            off = halo - (taps - 1) + k
            acc = acc + w_ref[k:k + 1, :] * ext[off:off + tm]
        o_ref[...] = acc.astype(o_ref.dtype)

    return pl.pallas_call(
        body, out_shape=jax.ShapeDtypeStruct((nb, rows, c), out_dtype), grid=(nb, rows // tm),
        in_specs=[pl.BlockSpec((None, tm, c), lambda n, i: (n, i, cblk)),
                  pl.BlockSpec((None, halo, c), lambda n, i: (n, jnp.maximum(i * per - 1, 0), cblk)),
                  pl.BlockSpec((None, taps, c), lambda n, i: (n, 0, 0)),
                  pl.BlockSpec((None, 1, c), lambda n, i: (n, 0, 0))],
        out_specs=pl.BlockSpec((None, tm, c), lambda n, i: (n, i, 0)),
        compiler_params=_params(("parallel", "parallel")), name=name)(x, x, w, b)


def dwconv_bwd(name, dy, x, cblk, c, w, taps, tm, dx_dtype=F32):
    nb = w.shape[0]
    rows = x.shape[1]
    halo = _halo_rows(taps)
    per = tm // halo
    n_tiles = rows // tm
    last_halo = rows // halo - 1

    def body(dy_ref, dn_ref, x_ref, xp_ref, w_ref, dx_ref, dw_ref, db_ref):
        i = pl.program_id(1)
        dyv = dy_ref[...]
        nxt = jnp.where(i < n_tiles - 1, dn_ref[...], 0.0)
        dext = jnp.concatenate([dyv, nxt], axis=0)
        prev = jnp.where(i > 0, xp_ref[...], 0.0)
        xext = jnp.concatenate([prev, x_ref[...]], axis=0)
        acc = jnp.zeros((tm, c), F32)

        @pl.when(i == 0)
        def _():
            dw_ref[...] = jnp.zeros_like(dw_ref)
            db_ref[...] = jnp.zeros_like(db_ref)

        for k in range(taps):
            acc = acc + w_ref[k:k + 1, :] * dext[taps - 1 - k:taps - 1 - k + tm]
            off = halo - (taps - 1) + k
            dw_ref[k:k + 1, :] += jnp.sum(dyv * xext[off:off + tm], axis=0, keepdims=True)
        dx_ref[...] = acc.astype(dx_ref.dtype)
        db_ref[...] += jnp.sum(dyv, axis=0, keepdims=True)

    return pl.pallas_call(
        body, out_shape=(jax.ShapeDtypeStruct((nb, rows, c), dx_dtype), jax.ShapeDtypeStruct((nb, taps, c), F32),
                         jax.ShapeDtypeStruct((nb, 1, c), F32)),
        grid=(nb, n_tiles),
        in_specs=[pl.BlockSpec((None, tm, c), lambda n, i: (n, i, 0)),
                  pl.BlockSpec((None, halo, c), lambda n, i: (n, jnp.minimum((i + 1) * per, last_halo), 0)),
                  pl.BlockSpec((None, tm, c), lambda n, i: (n, i, cblk)),
                  pl.BlockSpec((None, halo, c), lambda n, i: (n, jnp.maximum(i * per - 1, 0), cblk)),
                  pl.BlockSpec((None, taps, c), lambda n, i: (n, 0, 0))],
        out_specs=(pl.BlockSpec((None, tm, c), lambda n, i: (n, i, 0)), pl.BlockSpec((None, taps, c), lambda n, i: (n, 0, 0)),
                   pl.BlockSpec((None, 1, c), lambda n, i: (n, 0, 0))),
        compiler_params=_params(("parallel", "arbitrary")), name=name)(dy, dy, x, x, w)


def ffn_gate_fwd(name, up, w, b, tm):
    _, rows, c = up.shape
    halo = _halo_rows(FFN_TAPS)
    per = tm // halo

    def body(g_ref, gp_ref, v_ref, w_ref, b_ref, o_ref):
        i = pl.program_id(1)
        ext = jnp.concatenate([jnp.where(i > 0, gp_ref[...], 0.0), g_ref[...]], axis=0)
        gc = jnp.broadcast_to(b_ref[...], (tm, c))
        for k in range(FFN_TAPS):
            off = halo - (FFN_TAPS - 1) + k
            gc = gc + w_ref[k:k + 1, :] * ext[off:off + tm]
        o_ref[...] = _ffn_act(gc, v_ref[...])[0].astype(BF16)

    return pl.pallas_call(
        body, out_shape=jax.ShapeDtypeStruct((2, rows, c), BF16), grid=(2, rows // tm),
        in_specs=[pl.BlockSpec((None, tm, c), lambda h, i: (h, i, 0)),
                  pl.BlockSpec((None, halo, c), lambda h, i: (h, jnp.maximum(i * per - 1, 0), 0)),
                  pl.BlockSpec((None, tm, c), lambda h, i: (h + 2, i, 0)),
                  pl.BlockSpec((None, FFN_TAPS, c), lambda h, i: (h, 0, 0)), pl.BlockSpec((None, 1, c), lambda h, i: (h, 0, 0))],
        out_specs=pl.BlockSpec((None, tm, c), lambda h, i: (h, i, 0)),
        compiler_params=_params(("parallel", "parallel")), name=name)(up, up, up, w, b)


def ffn_gate_bwd(name, up, dact, w, b, tm):
    _, rows, c = up.shape
    halo = _halo_rows(FFN_TAPS)
    per = tm // halo
    n_tiles = rows // tm
    last_halo = rows // halo - 1
    n_ext = tm + halo

    def body(g_ref, gp_ref, gn_ref, v_ref, vn_ref, d_ref, dn_ref, w_ref, b_ref, dup_ref, dw_ref, db_ref):
        i = pl.program_id(1)
        gext = jnp.concatenate([jnp.where(i > 0, gp_ref[...], 0.0), g_ref[...], gn_ref[...]], axis=0)
        shifted = [gext[halo - (FFN_TAPS - 1) + k:halo - (FFN_TAPS - 1) + k + n_ext] for k in range(FFN_TAPS)]
        gc = jnp.broadcast_to(b_ref[...], (n_ext, c))
        for k in range(FFN_TAPS):
            gc = gc + w_ref[k:k + 1, :] * shifted[k]
        vext = jnp.concatenate([v_ref[...], vn_ref[...]], axis=0)
        dext = jnp.concatenate([d_ref[...], dn_ref[...]], axis=0)
        sq = gc * gc
        t = jnp.tanh(GELU_K0 * gc * (1.0 + GELU_K1 * sq))
        half = 0.5 * (1.0 + t)
        dval = dext * (gc * half)
        dgc = dext * vext * (half + 0.5 * gc * (1.0 - t * t) * (GELU_K0 * (1.0 + 3.0 * GELU_K1 * sq)))
        r = lax.broadcasted_iota(jnp.int32, (n_ext, c), 0)
        dgc = jnp.where((r < tm) | (i < n_tiles - 1), dgc, 0.0)
        dgate = jnp.zeros((tm, c), F32)
        for k in range(FFN_TAPS):
            dgate = dgate + w_ref[k:k + 1, :] * dgc[FFN_TAPS - 1 - k:FFN_TAPS - 1 - k + tm]
        dup_ref[0] = dgate.astype(BF16)
        dup_ref[1] = dval[:tm].astype(BF16)

        @pl.when(i == 0)
        def _():
            dw_ref[...] = jnp.zeros_like(dw_ref)
            db_ref[...] = jnp.zeros_like(db_ref)

        dgc_t = dgc[:tm]
        for k in range(FFN_TAPS):
            dw_ref[k:k + 1, :] += jnp.sum(dgc_t * shifted[k][:tm], axis=0, keepdims=True)
        db_ref[...] += jnp.sum(dgc_t, axis=0, keepdims=True)

    def tile(shift):
        return pl.BlockSpec((None, tm, c), lambda h, i: (h + shift, i, 0))

    def after(shift):
        return pl.BlockSpec((None, halo, c), lambda h, i: (h + shift, jnp.minimum((i + 1) * per, last_halo), 0))

    return pl.pallas_call(
        body, out_shape=(jax.ShapeDtypeStruct((2, 2, rows, c), BF16), jax.ShapeDtypeStruct((2, FFN_TAPS, c), F32),
                         jax.ShapeDtypeStruct((2, 1, c), F32)),
        grid=(2, n_tiles),
        in_specs=[tile(0), pl.BlockSpec((None, halo, c), lambda h, i: (h, jnp.maximum(i * per - 1, 0), 0)), after(0),
                  tile(2), after(2), tile(0), after(0),
                  pl.BlockSpec((None, FFN_TAPS, c), lambda h, i: (h, 0, 0)), pl.BlockSpec((None, 1, c), lambda h, i: (h, 0, 0))],
        out_specs=(pl.BlockSpec((2, None, tm, c), lambda h, i: (0, h, i, 0)), pl.BlockSpec((None, FFN_TAPS, c), lambda h, i: (h, 0, 0)),
                   pl.BlockSpec((None, 1, c), lambda h, i: (h, 0, 0))),
        compiler_params=_params(("parallel", "arbitrary")), name=name)(up, up, up, up, up, dact, dact, w, b)


POOL_HALO = 16


def pool_fwd(name, proj, cblk, tm):
    rows = proj.shape[0]
    c = 128 * len(POOL_WINDOWS)
    per = tm // POOL_HALO

    def body(x_ref, h_ref, o_ref):
        i = pl.program_id(0)
        xv = x_ref[...]
        ext = jnp.concatenate([jnp.where(i > 0, h_ref[...], 0.0), xv], axis=0)
        t1 = (lax.broadcasted_iota(jnp.int32, (tm, 128), 0) + i * tm + 1).astype(F32)
        outs = []
        for gi, win in enumerate(POOL_WINDOWS):
            seg = ext[:, gi * 128:(gi + 1) * 128]
            s = seg[POOL_HALO:POOL_HALO + tm]
            for j in range(1, win):
                s = s + seg[POOL_HALO - j:POOL_HALO - j + tm]
            outs.append(s / jnp.minimum(t1, float(win)) - xv[:, gi * 128:(gi + 1) * 128])
        o_ref[...] = jnp.concatenate(outs, axis=-1)

    return pl.pallas_call(
        body, out_shape=jax.ShapeDtypeStruct((rows, c), F32), grid=(rows // tm,),
        in_specs=[pl.BlockSpec((tm, c), lambda i: (i, cblk)),
                  pl.BlockSpec((POOL_HALO, c), lambda i: (jnp.maximum(i * per - 1, 0), cblk))],
        out_specs=pl.BlockSpec((tm, c), lambda i: (i, 0)), compiler_params=_params(("parallel",)), name=name)(proj, proj)


def pool_bwd(name, dd, tm):
    rows, c = dd.shape
    per = tm // POOL_HALO
    n_tiles = rows // tm
    last_halo = rows // POOL_HALO - 1

    def body(d_ref, n_ref, o_ref):
        i = pl.program_id(0)
        dv = d_ref[...]
        nxt = jnp.where(i < n_tiles - 1, n_ref[...], 0.0)
        t1 = (lax.broadcasted_iota(jnp.int32, (tm, 128), 0) + i * tm + 1).astype(F32)
        t1n = (lax.broadcasted_iota(jnp.int32, (POOL_HALO, 128), 0) + (i + 1) * tm + 1).astype(F32)
        outs = []
        for gi, win in enumerate(POOL_WINDOWS):
            sl = slice(gi * 128, (gi + 1) * 128)
            q = jnp.concatenate([dv[:, sl] / jnp.minimum(t1, float(win)), nxt[:, sl] / jnp.minimum(t1n, float(win))], axis=0)
            s = q[0:tm]
            for j in range(1, win):
                s = s + q[j:j + tm]
            outs.append(s - dv[:, sl])
        o_ref[...] = jnp.concatenate(outs, axis=-1)

    return pl.pallas_call(
        body, out_shape=jax.ShapeDtypeStruct((rows, c), F32), grid=(n_tiles,),
        in_specs=[pl.BlockSpec((tm, c), lambda i: (i, 0)),
                  pl.BlockSpec((POOL_HALO, c), lambda i: (jnp.minimum((i + 1) * per, last_halo), 0))],
        out_specs=pl.BlockSpec((tm, c), lambda i: (i, 0)), compiler_params=_params(("parallel",)), name=name)(dd, dd)


def _shift_down(v, s, fill):
    r = lax.broadcasted_iota(jnp.int32, v.shape, 0)
    return jnp.where(r >= s, pltpu.roll(v, s, 0), fill)


def _shift_up(v, s, fill):
    n = v.shape[0]
    r = lax.broadcasted_iota(jnp.int32, v.shape, 0)
    return jnp.where(r < n - s, pltpu.roll(v, n - s, 0), fill)


def _cscan_chunk(vr, vi, powers, reverse):
    for k, (qr, qi) in enumerate(powers):
        s = 1 << k
        if reverse:
            sr, si = _shift_up(vr, s, 0.0), _shift_up(vi, s, 0.0)
            vr, vi = vr + qr * sr + qi * si, vi + qr * si - qi * sr
        else:
            sr, si = _shift_down(vr, s, 0.0), _shift_down(vi, s, 0.0)
            vr, vi = vr + qr * sr - qi * si, vi + qr * si + qi * sr
    return vr, vi


def _powers(pr, pi, n):
    out = [(pr, pi)]
    for _ in range(n - 1):
        pr, pi = pr * pr - pi * pi, 2.0 * pr * pi
        out.append((pr, pi))
    return out


def s5_scan_fwd(name, bu, a):
    _, rows, n = bu.shape
    t = min(SCAN_CHUNK, rows)
    steps = t.bit_length() - 1

    def body(bu_ref, a_ref, z_ref):
        pr, pi = a_ref[0], a_ref[1]
        powers = _powers(pr, pi, steps)
        r = lax.broadcasted_iota(jnp.int32, (t, 128), 0)
        tr, ti = _cscan_chunk(jnp.where(r == 0, pr, 0.0), jnp.where(r == 0, pi, 0.0), powers, False)

        def chunk(ci, carry):
            base = pl.multiple_of(ci * t, t)
            vr, vi = _cscan_chunk(bu_ref[0, pl.ds(base, t), :], bu_ref[1, pl.ds(base, t), :], powers, False)
            cr, cim = carry
            zr = vr + tr * cr - ti * cim
            zi = vi + tr * cim + ti * cr
            z_ref[0, pl.ds(base, t), :] = zr
            z_ref[1, pl.ds(base, t), :] = zi
            return zr[t - 1:t, :], zi[t - 1:t, :]

        zero = jnp.zeros((1, 128), F32)
        lax.fori_loop(0, rows // t, chunk, (zero, zero))

    return pl.pallas_call(
        body, out_shape=jax.ShapeDtypeStruct((2, rows, n), F32), grid=(n // 128,),
        in_specs=[pl.BlockSpec((2, rows, 128), lambda j: (0, 0, j)), pl.BlockSpec((2, 1, 128), lambda j: (0, 0, j))],
        out_specs=pl.BlockSpec((2, rows, 128), lambda j: (0, 0, j)), compiler_params=_params(("parallel",)), name=name)(bu, a)


def s5_scan_bwd(name, dz, z, a):
    _, rows, n = dz.shape
    t = min(SCAN_CHUNK, rows)
    steps = t.bit_length() - 1
    n_chunks = rows // t

    def body(dz_ref, z_ref, a_ref, lam_ref, da_ref):
        pr, pi = a_ref[0], a_ref[1]
        powers = _powers(pr, pi, steps)
        r = lax.broadcasted_iota(jnp.int32, (t, 128), 0)
        tr, ti = _cscan_chunk(jnp.where(r == t - 1, pr, 0.0), jnp.where(r == t - 1, -pi, 0.0), powers, True)

        def chunk(k, carry):
            ci = n_chunks - 1 - k
            base = pl.multiple_of(ci * t, t)
            vr, vi = _cscan_chunk(dz_ref[0, pl.ds(base, t), :], dz_ref[1, pl.ds(base, t), :], powers, True)
            cr, cim, dar, dai = carry
            lr = vr + tr * cr - ti * cim
            li = vi + tr * cim + ti * cr
            lam_ref[0, pl.ds(base, t), :] = lr
            lam_ref[1, pl.ds(base, t), :] = li
            pbase = pl.multiple_of(jnp.maximum(base - SUBLANES, 0), SUBLANES)
            keep = (ci > 0).astype(F32)
            pzr = z_ref[0, pl.ds(pbase, SUBLANES), :][SUBLANES - 1:SUBLANES, :] * keep
            pzi = z_ref[1, pl.ds(pbase, SUBLANES), :][SUBLANES - 1:SUBLANES, :] * keep
            zpr = _shift_down(z_ref[0, pl.ds(base, t), :], 1, pzr)
            zpi = _shift_down(z_ref[1, pl.ds(base, t), :], 1, pzi)
            dar = dar + jnp.sum(lr * zpr + li * zpi, axis=0, keepdims=True)
            dai = dai + jnp.sum(li * zpr - lr * zpi, axis=0, keepdims=True)
            return lr[0:1, :], li[0:1, :], dar, dai

        zero = jnp.zeros((1, 128), F32)
        _, _, dar, dai = lax.fori_loop(0, n_chunks, chunk, (zero, zero, zero, zero))
        da_ref[0] = dar
        da_ref[1] = dai

    seq = pl.BlockSpec((2, rows, 128), lambda j: (0, 0, j))
    vec = pl.BlockSpec((2, 1, 128), lambda j: (0, 0, j))
    return pl.pallas_call(
        body, out_shape=(jax.ShapeDtypeStruct((2, rows, n), F32), jax.ShapeDtypeStruct((2, 1, n), F32)), grid=(n // 128,),
        in_specs=[seq, seq, vec], out_specs=(seq, vec), compiler_params=_params(("parallel",)), name=name)(dz, z, a)


def _rscan_chunk(a, b, steps, reverse):
    shift = _shift_up if reverse else _shift_down
    for k in range(steps):
        s = 1 << k
        b = b + a * shift(b, s, 0.0)
        a = a * shift(a, s, 1.0)
    return a, b


def lru_scan_fwd(name, a, b):
    rows, n = a.shape
    t = min(SCAN_CHUNK, rows)
    steps = t.bit_length() - 1

    def body(a_ref, b_ref, h_ref):
        def chunk(ci, carry):
            base = pl.multiple_of(ci * t, t)
            pa, hb = _rscan_chunk(a_ref[pl.ds(base, t), :], b_ref[pl.ds(base, t), :], steps, False)
            h = hb + pa * carry
            h_ref[pl.ds(base, t), :] = h
            return h[t - 1:t, :]

        lax.fori_loop(0, rows // t, chunk, jnp.zeros((1, 128), F32))

    seq = pl.BlockSpec((rows, 128), lambda j: (0, j))
    return pl.pallas_call(body, out_shape=jax.ShapeDtypeStruct((rows, n), F32), grid=(n // 128,), in_specs=[seq, seq],
                          out_specs=seq, compiler_params=_params(("parallel",)), name=name)(a, b)


def lru_scan_bwd(name, dh, a, h):
    rows, n = a.shape
    t = min(SCAN_CHUNK, rows)
    steps = t.bit_length() - 1
    n_chunks = rows // t

    def body(dh_ref, a_ref, h_ref, da_ref, db_ref):
        def chunk(k, carry):
            ci = n_chunks - 1 - k
            base = pl.multiple_of(ci * t, t)
            nbase = pl.multiple_of(jnp.minimum(base + t, rows - SUBLANES), SUBLANES)
            a_next = a_ref[pl.ds(nbase, SUBLANES), :][0:1, :]
            an = _shift_up(a_ref[pl.ds(base, t), :], 1, a_next)
            pa, mb = _rscan_chunk(an, dh_ref[pl.ds(base, t), :], steps, True)
            mu = mb + pa * carry
            pbase = pl.multiple_of(jnp.maximum(base - SUBLANES, 0), SUBLANES)
            hp_row = h_ref[pl.ds(pbase, SUBLANES), :][SUBLANES - 1:SUBLANES, :] * (ci > 0).astype(F32)
            hp = _shift_down(h_ref[pl.ds(base, t), :], 1, hp_row)
            da_ref[pl.ds(base, t), :] = mu * hp
            db_ref[pl.ds(base, t), :] = mu
            return mu[0:1, :]

        lax.fori_loop(0, n_chunks, chunk, jnp.zeros((1, 128), F32))

    seq = pl.BlockSpec((rows, 128), lambda j: (0, j))
    return pl.pallas_call(
        body, out_shape=(jax.ShapeDtypeStruct((rows, n), F32), jax.ShapeDtypeStruct((rows, n), F32)), grid=(n // 128,),
        in_specs=[seq, seq, seq], out_specs=(seq, seq), compiler_params=_params(("parallel",)), name=name)(dh, a, h)


def _s5_param(lr, li, ls, bre, bim):
    st = jnp.exp(ls)
    er = jnp.exp(lr * st)
    th = li * st
    ar, ai = er * jnp.cos(th), er * jnp.sin(th)
    nr, ni = ar - 1.0, ai
    den = lr * lr + li * li
    cr, ci = (nr * lr + ni * li) / den, (ni * lr - nr * li) / den
    return ar, ai, cr * bre - ci * bim, cr * bim + ci * bre


def s5_param_fwd(name, lr, li, ls, bre, bim):
    gh, n = bre.shape

    def body(lr_ref, li_ref, ls_ref, bre_ref, bim_ref, a_ref, bb_ref):
        ar, ai, br, bi = _s5_param(lr_ref[...], li_ref[...], ls_ref[...], bre_ref[...], bim_ref[...])
        a_ref[0] = ar
        a_ref[1] = ai
        bb_ref[0] = br.astype(BF16)
        bb_ref[1] = bi.astype(BF16)

    return pl.pallas_call(body, out_shape=(jax.ShapeDtypeStruct((2, 1, n), F32), jax.ShapeDtypeStruct((2, gh, n), BF16)),
                          compiler_params=_params(), name=name)(lr, li, ls, bre, bim)


def s5_param_bwd(name, lr, li, ls, bre, bim, da, dbb, gsum):
    gh, n = bre.shape

    def body(lr_ref, li_ref, ls_ref, bre_ref, bim_ref, da_ref, dbb_ref, gs_ref, dlr_ref, dli_ref, dls_ref, dbre_ref, dbim_ref):
        _, vjp = jax.vjp(_s5_param, lr_ref[...], li_ref[...], ls_ref[...], bre_ref[...], bim_ref[...])
        dlr, dli, dls, dbre, dbim = vjp((da_ref[0], da_ref[1], dbb_ref[0], dbb_ref[1]))
        dlr_ref[...] = dlr
        dli_ref[...] = dli
        dls_ref[...] = jnp.dot(jnp.broadcast_to(dls, (SUBLANES, n)), gs_ref[...], preferred_element_type=F32,
                               precision=lax.Precision.HIGHEST)
        dbre_ref[...] = dbre
        dbim_ref[...] = dbim

    vec = jax.ShapeDtypeStruct((1, n), F32)
    mat = jax.ShapeDtypeStruct((gh, n), F32)
    return pl.pallas_call(body, out_shape=(vec, vec, jax.ShapeDtypeStruct((SUBLANES, 128), F32), mat, mat),
                          compiler_params=_params(), name=name)(lr, li, ls, bre, bim, da, dbb, gsum)


def sum_lead(name, x, tr):
    n, rows, cols = x.shape

    def body(x_ref, o_ref):
        acc = x_ref[0]
        for j in range(1, n):
            acc = acc + x_ref[j]
        o_ref[...] = acc

    return pl.pallas_call(
        body, out_shape=jax.ShapeDtypeStruct((rows, cols), x.dtype), grid=(rows // tr,),
        in_specs=[pl.BlockSpec((n, tr, cols), lambda i: (0, i, 0))], out_specs=pl.BlockSpec((tr, cols), lambda i: (i, 0)),
        compiler_params=_params(("parallel",)), name=name)(x)


def _adamw(w, g, m, v):
    m = ADAM_B1 * m + (1.0 - ADAM_B1) * g
    v = ADAM_B2 * v + (1.0 - ADAM_B2) * jnp.square(g)
    m_hat = m / (1.0 - ADAM_B1 ** ADAM_STEP)
    v_hat = v / (1.0 - ADAM_B2 ** ADAM_STEP)
    delta = -ADAM_LR * (m_hat / (jnp.sqrt(v_hat) + ADAM_EPS) + ADAM_WD * w)
    return delta, m, v


def adamw_sharded(name, w, m, v, g0, g1, split_cols, tile):
    _, r, c = w.shape
    if split_cols:
        nt = c // tile
        per = (c // 2) // tile
        wspec = pl.BlockSpec((None, r, tile), lambda l, t: (l, 0, t))
        gspec = pl.BlockSpec((None, r, tile), lambda l, t: (t // per, 0, t % per))
    else:
        nt = r // tile
        per = (r // 2) // tile
        wspec = pl.BlockSpec((None, tile, c), lambda l, t: (l, t, 0))
        gspec = pl.BlockSpec((None, tile, c), lambda l, t: (t // per, t % per, 0))

    def body(w_ref, m_ref, v_ref, g0_ref, g1_ref, g_ref, d_ref, nm_ref, nv_ref):
        g = jnp.where(pl.program_id(0) == 0, g0_ref[...], g1_ref[...])
        d, nm, nv = _adamw(w_ref[...], g, m_ref[...], v_ref[...])
        g_ref[...] = g
        d_ref[...] = d
        nm_ref[...] = nm
        nv_ref[...] = nv

    sds = jax.ShapeDtypeStruct(w.shape, F32)
    return pl.pallas_call(body, out_shape=(sds,) * 4, grid=(2, nt), in_specs=[wspec, wspec, wspec, gspec, gspec],
                          out_specs=(wspec,) * 4, compiler_params=_params(("parallel", "parallel")), name=name)(w, m, v, g0, g1)


def adamw_flat(name, w, g, m, v, tr):
    rows, cols = w.shape

    def body(w_ref, g_ref, m_ref, v_ref, d_ref, nm_ref, nv_ref):
        d, nm, nv = _adamw(w_ref[...], g_ref[...], m_ref[...], v_ref[...])
        d_ref[...] = d
        nm_ref[...] = nm
        nv_ref[...] = nv

    blk = pl.BlockSpec((tr, cols), lambda i: (i, 0))
    sds = jax.ShapeDtypeStruct((rows, cols), F32)
    return pl.pallas_call(body, out_shape=(sds,) * 3, grid=(rows // tr,), in_specs=[blk] * 4, out_specs=(blk,) * 3,
                          compiler_params=_params(("parallel",)), name=name)(w, g, m, v)


def _flips(axes):
    out = []
    for fx in ((0, 1) if "x" in axes else (0,)):
        for fy in ((0, 1) if "y" in axes else (0,)):
            for fc in ((0, 1) if "c" in axes else (0,)):
                if fx or fy or fc:
                    out.append((fx, fy, fc))
    return out


def _slot(pos, axes):
    s = 0
    for name, p in zip(("x", "y", "c"), pos):
        if name in axes:
            s = 2 * s + p
    return s


def _exchange(name, arrs, axes, scatter):
    flips = _flips(axes)
    n = len(flips) + 1
    na = len(arrs)

    def body(*refs):
        ins, outs = refs[:na], refs[na:2 * na]
        send_sems, recv_sems, local_sems = refs[2 * na:]
        me = (lax.axis_index("x"), lax.axis_index("y"), lax.axis_index("c"))
        my = _slot(me, axes)
        peers = [tuple((1 - p) if f else p for p, f in zip(me, fl)) for fl in flips]

        def src(a, dest_slot):
            return ins[a].at[dest_slot] if scatter else ins[a]

        local = [pltpu.make_async_copy(src(a, my), outs[a].at[my], local_sems.at[a]) for a in range(na)]
        for cp in local:
            cp.start()

        def remote(a, j, landing_slot, dest_slot):
            return pltpu.make_async_remote_copy(
                src_ref=src(a, dest_slot), dst_ref=outs[a].at[landing_slot], send_sem=send_sems.at[a * len(flips) + j],
                recv_sem=recv_sems.at[a * len(flips) + j], device_id=peers[j], device_id_type=pl.DeviceIdType.MESH)

        sends = [remote(a, j, my, _slot(peers[j], axes)) for a in range(na) for j in range(len(flips))]
        for cp in sends:
            cp.start()
        for a in range(na):
            for j in range(len(flips)):
                remote(a, j, _slot(peers[j], axes), _slot(peers[j], axes)).wait_recv()
        for cp in sends:
            cp.wait_send()
        for cp in local:
            cp.wait()

    if scatter:
        out_shape = tuple(jax.ShapeDtypeStruct(a.shape, a.dtype) for a in arrs)
    else:
        out_shape = tuple(jax.ShapeDtypeStruct((n,) + a.shape, a.dtype) for a in arrs)
    anyspec = pl.BlockSpec(memory_space=pl.ANY)
    return pl.pallas_call(
        body, out_shape=out_shape, in_specs=[anyspec] * na, out_specs=(anyspec,) * na,
        scratch_shapes=[pltpu.SemaphoreType.DMA((na * len(flips),)), pltpu.SemaphoreType.DMA((na * len(flips),)),
                        pltpu.SemaphoreType.DMA((na,))],
        name=name)(*arrs)


def all_gather(name, arrs, axes):
    return _exchange(name, arrs, axes, False)


def all_to_all(name, arrs, axes):
    return _exchange(name, arrs, axes, True)


_HBM = pl.BlockSpec(memory_space=pltpu.HBM)
_SEM = pl.BlockSpec(memory_space=pltpu.SEMAPHORE)
_EFFECT = pltpu.SideEffectType.DATAFLOW_SIDE_EFFECTING


def place_own(name, arrs, axes, scatter):
    n = len(_flips(axes)) + 1
    na = len(arrs)

    def body(*refs):
        ins, outs, sems = refs[:na], refs[na:2 * na], refs[2 * na]
        my = _slot((lax.axis_index("x"), lax.axis_index("y"), lax.axis_index("c")), axes)
        copies = [pltpu.make_async_copy(ins[a].at[my] if scatter else ins[a], outs[a].at[my], sems.at[a]) for a in range(na)]
        for cp in copies:
            cp.start()
        for cp in copies:
            cp.wait()

    out_shape = tuple(jax.ShapeDtypeStruct(a.shape if scatter else (n,) + a.shape, a.dtype) for a in arrs)
    anyspec = pl.BlockSpec(memory_space=pl.ANY)
    return pl.pallas_call(body, out_shape=out_shape, in_specs=[anyspec] * na, out_specs=(anyspec,) * na,
                          scratch_shapes=[pltpu.SemaphoreType.DMA((na,))], name=name)(*arrs)


def _peers(axes):
    me = (lax.axis_index("x"), lax.axis_index("y"), lax.axis_index("c"))
    return me, [tuple((1 - p) if f else p for p, f in zip(me, fl)) for fl in _flips(axes)]


def place_tile(name, arr, layer, my, slots=4, dtype=BF16):
    _, r, cols = arr.shape
    tr = _tile_rows(r, cols)

    def body(my_ref, x_ref, o_ref):
        o_ref[...] = x_ref[...].astype(dtype)

    grid_spec = pltpu.PrefetchScalarGridSpec(
        num_scalar_prefetch=1, grid=(r // tr,), in_specs=[pl.BlockSpec((None, tr, cols), lambda i, my: (layer, i, 0))],
        out_specs=pl.BlockSpec((None, tr, cols), lambda i, my: (my[0], i, 0)))
    return pl.pallas_call(body, out_shape=jax.ShapeDtypeStruct((slots, r, cols), dtype), grid_spec=grid_spec,
                          compiler_params=_params(("parallel",)), name=name)(my, arr)


def exchange_start(name, groups, axes, scatter):
    flat = [(p if scatter else (p,)) for grp in groups for p in grp]
    per = 2 if scatter else 1
    na, ng, npeer = len(flat), len(groups), len(_flips(axes))

    def body(*refs):
        ops = refs[:per * na]
        zones = ops[(per - 1) * na:]
        sems, token = refs[per * na:per * na + 2 * ng], refs[-1]
        me, peers = _peers(axes)
        my = _slot(me, axes)
        ai = 0
        for g, grp in enumerate(groups):
            for k in range(len(grp)):
                for j, peer in enumerate(peers):
                    src = ops[ai].at[_slot(peer, axes)] if scatter else zones[ai].at[my]
                    dst = zones[ai].at[j] if scatter else zones[ai].at[my]
                    pltpu.make_async_remote_copy(
                        src_ref=src, dst_ref=dst, send_sem=sems[2 * g].at[k * npeer + j],
                        recv_sem=sems[2 * g + 1].at[k * npeer + j], device_id=peer, device_id_type=pl.DeviceIdType.MESH).start()
                ai += 1
        token[...] = jnp.zeros_like(token)

    out_shape, out_specs = [], []
    for grp in groups:
        out_shape += [pltpu.SemaphoreType.DMA((npeer * len(grp),))] * 2
        out_specs += [_SEM, _SEM]
    for idx in range(per):
        out_shape += [pltpu.HBM(p[idx].shape, p[idx].dtype) for p in flat]
        out_specs += [_HBM] * na
    out_shape.append(jax.ShapeDtypeStruct((SUBLANES, 128), F32))
    out_specs.append(pl.BlockSpec(memory_space=pltpu.VMEM))
    args = [pltpu.with_memory_space_constraint(p[idx], pltpu.HBM) for idx in range(per) for p in flat]
    res = pl.pallas_call(body, out_shape=tuple(out_shape), in_specs=[_HBM] * (per * na), out_specs=tuple(out_specs),
                         input_output_aliases={i: 2 * ng + i for i in range(per * na)},
                         compiler_params=pltpu.CompilerParams(has_side_effects=_EFFECT), name=name)(*args)
    thru = res[2 * ng:2 * ng + per * na]
    out, ai = [], 0
    for g, grp in enumerate(groups):
        srcs = list(thru[ai:ai + len(grp)]) if scatter else []
        zones = list(thru[(per - 1) * na + ai:(per - 1) * na + ai + len(grp)])
        out.append(((res[2 * g], res[2 * g + 1]), srcs, zones))
        ai += len(grp)
    return out, res[-1]


def exchange_wait(name, group, after, axes, scatter):
    (send_sems, recv_sems), srcs, zones = group
    n, ns = len(zones), len(srcs)
    npeer = len(_flips(axes))

    def body(*refs):
        z_refs = refs[ns:ns + n]
        ssem, rsem = refs[ns + n], refs[ns + n + 1]
        _, peers = _peers(axes)
        for k in range(n):
            for j, peer in enumerate(peers):
                part = z_refs[k].at[j if scatter else _slot(peer, axes)]
                copy = pltpu.make_async_remote_copy(
                    src_ref=part, dst_ref=part, send_sem=ssem.at[k * npeer + j], recv_sem=rsem.at[k * npeer + j],
                    device_id=peer, device_id_type=pl.DeviceIdType.MESH)
                copy.wait_send()
                copy.wait_recv()

    ops = list(srcs) + list(zones)
    out_shape = tuple(pltpu.HBM(a.shape, a.dtype) for a in ops)
    res = pl.pallas_call(body, out_shape=out_shape, in_specs=[_HBM] * len(ops) + [_SEM, _SEM, pl.BlockSpec(memory_space=pl.ANY)],
                         out_specs=(_HBM,) * len(ops), input_output_aliases={i: i for i in range(len(ops))},
                         compiler_params=pltpu.CompilerParams(has_side_effects=_EFFECT), name=name)(*ops, send_sems, recv_sems, after)
    return list(res[:ns]), list(res[ns:])


def _pair_exchange(name, ins, in_specs, n_steps, tile, fn_send, fn_out, out_shape, out_spec, prefetch=None, wire=F32):
    n_in = len(ins)

    def body(*refs):
        if prefetch is not None:
            refs = refs[1:]
        in_refs, o_ref = refs[:n_in], refs[n_in]
        send_buf, recv_buf, send_sems, recv_sems, credit = refs[n_in + 1:]
        i = pl.program_id(0)
        slot = lax.rem(i, 2)
        c = lax.axis_index("c")
        sibling = (lax.axis_index("x"), lax.axis_index("y"), 1 - c)
        vals = [r[...] for r in in_refs]
        send_buf[slot] = fn_send(*vals, c).astype(wire)

        @pl.when(i >= 2)
        def _():
            pl.semaphore_wait(credit, 1)

        copy = pltpu.make_async_remote_copy(
            src_ref=send_buf.at[slot], dst_ref=recv_buf.at[slot], send_sem=send_sems.at[slot], recv_sem=recv_sems.at[slot],
            device_id=sibling, device_id_type=pl.DeviceIdType.MESH)
        copy.start()
        copy.wait_recv()
        o_ref[...] = fn_out(*vals, recv_buf[slot], c).astype(o_ref.dtype)
        copy.wait_send()

        @pl.when(i < n_steps - 2)
        def _():
            pl.semaphore_signal(credit, inc=1, device_id=sibling, device_id_type=pl.DeviceIdType.MESH)

    scratch = [pltpu.VMEM((2,) + tile, wire), pltpu.VMEM((2,) + tile, wire), pltpu.SemaphoreType.DMA((2,)),
               pltpu.SemaphoreType.DMA((2,)), pltpu.SemaphoreType.REGULAR]
    if prefetch is None:
        return pl.pallas_call(body, out_shape=out_shape, grid=(n_steps,), in_specs=in_specs, out_specs=out_spec,
                              scratch_shapes=scratch, compiler_params=_params(("arbitrary",)), name=name)(*ins)
    grid_spec = pltpu.PrefetchScalarGridSpec(num_scalar_prefetch=1, grid=(n_steps,), in_specs=in_specs, out_specs=out_spec,
                                             scratch_shapes=scratch)
    return pl.pallas_call(body, out_shape=out_shape, grid_spec=grid_spec, compiler_params=_params(("arbitrary",)),
                          name=name)(prefetch, *ins)


def _tile_rows(rows, cols, f32_bytes=3 << 19):
    return _row_tile(rows, max(2 * SUBLANES, f32_bytes // (4 * cols)), 2 * SUBLANES)


def pair_sum(name, x):
    rows, cols = x.shape
    tr = _tile_rows(rows, cols)
    return _pair_exchange(name, [x], [pl.BlockSpec((tr, cols), lambda i: (i, 0))], rows // tr, (tr, cols),
                          lambda v, c: v, lambda v, got, c: v + got, jax.ShapeDtypeStruct((rows, cols), F32),
                          pl.BlockSpec((tr, cols), lambda i: (i, 0)))


def reduce_cores(name, g):
    _, m, cols = g.shape
    tr = _tile_rows(m, cols, 3 << 20)

    def fn_send(g0, g1, c):
        return jnp.where(c == 0, g1, g0)

    def fn_out(g0, g1, got, c):
        return jnp.where(c == 0, g0, g1) + got.astype(F32)

    return _pair_exchange(
        name, [g, g], [pl.BlockSpec((None, tr, cols), lambda i: (0, i, 0)), pl.BlockSpec((None, tr, cols), lambda i: (1, i, 0))],
        m // tr, (tr, cols), fn_send, fn_out, jax.ShapeDtypeStruct((m, cols), BF16), pl.BlockSpec((tr, cols), lambda i: (i, 0)),
        wire=BF16)


def sum_and_share(name, own, parts, my):
    n, r, cols = parts.shape
    tr = _tile_rows(r, cols)

    def total(o, p):
        acc = o.astype(F32)
        for j in range(n):
            acc = acc + p[j].astype(F32)
        return acc

    def fn_send(o, p, c):
        return total(o, p)

    def fn_out(o, p, got, c):
        mine = total(o, p)
        return jnp.stack([jnp.where(c == 0, mine, got), jnp.where(c == 0, got, mine)])

    return _pair_exchange(
        name, [own, parts], [pl.BlockSpec((None, tr, cols), lambda i, my_ref: (my_ref[0], i, 0)), pl.BlockSpec((n, tr, cols), lambda i, my_ref: (0, i, 0))],
        r // tr, (tr, cols), fn_send, fn_out, jax.ShapeDtypeStruct((2, r, cols), F32),
        pl.BlockSpec((2, tr, cols), lambda i, my_ref: (0, i, 0)), prefetch=my)


def _block_diag(blocks):
    g, r, c = blocks.shape
    eye = jnp.eye(g, dtype=blocks.dtype)
    return (blocks[:, :, None, :] * eye[:, None, :, None]).reshape(g * r, g * c)


def _diag_blocks(mat, g):
    r, c = mat.shape[0] // g, mat.shape[1] // g
    eye = jnp.eye(g, dtype=mat.dtype)
    return (mat.reshape(g, r, g, c) * eye[:, None, :, None]).sum(axis=2)


def _halves(gfull, shards):
    rows, cols = gfull.shape
    return gfull.reshape(shards, 2, rows // shards // 2, cols).transpose(1, 0, 2, 3)


def _step(inp):
    x = inp['x'][0]
    target = inp['loss_target'][0]
    rows, d = x.shape
    depth = inp['w_in'].shape[0]
    mix_w = d // 4
    n_state = S5_GROUPS * S5_STATE
    ffn_half = inp['ffn_w_up'].shape[2]
    tm = min(512, rows)
    tc = min(256, rows)
    tl = min(512, rows)
    xy = ("x", "y")

    my_chip = (2 * lax.axis_index("x") + lax.axis_index("y")).astype(jnp.int32).reshape(1)
    zones = {}
    for l in range(depth):
        for nme in BIG:
            if nme == 'ffn_w_up':
                zones[(nme, l)] = place_tile(f"place_{nme}{l}", inp[nme][l].astype(BF16)[None], 0, my_chip)
            else:
                zones[(nme, l)] = place_tile(f"place_{nme}{l}", inp[nme], l, my_chip)
    small_keys = [(nme, None) for nme in SMALL_SHARDED]
    zones.update(zip(small_keys, place_own("place_small", [inp[nme] for nme in SMALL_SHARDED], xy, False)))
    group_keys = []
    for l in range(depth):
        group_keys += [[('w_in', l)] + (small_keys if l == 0 else []),
                       [('w_out', l), ('s5_w_glu', l), ('cv_w_pw', l)], [('ffn_w_up', l)], [('ffn_w_down', l)]]
    gather_groups, gather_token = exchange_start("gather_start", [[zones[key] for key in grp] for grp in group_keys], xy, False)

    def gathered(gi, after):
        return dict(zip(group_keys[gi], exchange_wait(f"gather_wait{gi}", gather_groups[gi], after, xy, False)[1]))

    def full_small(g):
        return g.transpose(1, 2, 0, 3).reshape(g.shape[1], g.shape[2], 4 * g.shape[3])

    gsum = jnp.repeat(jnp.eye(128, dtype=F32)[:S5_GROUPS], S5_STATE, axis=0)

    saved = []
    grads = {nme: [None] * depth for nme in WEIGHTS}
    xcur = x
    for l in range(depth):
        vec = lambda a: a[l].reshape(1, -1)
        gain = vec(inp['norm_mix_g']) + (gather_token[0, 0] if l == 0 else 0.0)
        h = rms_fwd(f"rms_mix{l}", xcur, gain, tm)
        got = gathered(4 * l, h)
        w_in = got[('w_in', l)]
        if l == 0:
            cv_w_dw, lru_w_conv, ffn_w_dw = (full_small(got[(nme, None)]) for nme in ('cv_w_dw', 'lru_w_conv', 'ffn_w_dw'))
        ncol = w_in.shape[2]

        lam_re, lam_im = vec(inp['s5_lam_re']), vec(inp['s5_lam_im'])
        log_step = jnp.broadcast_to(inp['s5_log_step'][l][:, None], (S5_GROUPS, S5_STATE)).reshape(1, n_state)
        b_re = _block_diag(inp['s5_b_re'][l].transpose(0, 2, 1))
        b_im = _block_diag(inp['s5_b_im'][l].transpose(0, 2, 1))
        c_cat = jnp.stack([_block_diag(inp['s5_c_re'][l].transpose(0, 2, 1)),
                           -_block_diag(inp['s5_c_im'][l].transpose(0, 2, 1))]).astype(BF16)
        a_bar, b_bar = s5_param_fwd(f"s5_param_fwd{l}", lam_re, lam_im, log_step, b_re, b_im)
        w_r = _block_diag(inp['lru_w_r'][l]).astype(BF16)
        w_i = _block_diag(inp['lru_w_i'][l]).astype(BF16)
        pool_bd = _block_diag(inp['pool_w'][l]).astype(BF16)
        gate_pars = [w_r, w_i, vec(inp['lru_b_r']), vec(inp['lru_b_i']), vec(inp['lru_lam'])]

        proj = _mm(f"proj{l}", h, w_in, jax.ShapeDtypeStruct((rows, 4 * ncol), F32), (4, rows // tm),
                   pl.BlockSpec((tm, d), lambda j, i: (i, 0)), pl.BlockSpec((None, d, ncol), lambda j, i: (j, 0, 0)),
                   pl.BlockSpec((tm, ncol), lambda j, i: (i, j)), NN)
        proj3 = proj.reshape(1, rows, 4 * ncol)
        nh = n_state // 2
        ts = min(2048, rows)
        cw, sw = mix_w // 4, n_state // 4
        bu = _mm(f"s5_bu{l}", proj, b_bar, jax.ShapeDtypeStruct((2, rows, n_state), F32), (rows // ts, 2, 4),
                 pl.BlockSpec((ts, cw), lambda i, c, s: (i, s)), pl.BlockSpec((None, cw, sw), lambda i, c, s: (c, s, s)),
                 pl.BlockSpec((None, ts, sw), lambda i, c, s: (c, i, s)), NN)
        z = s5_scan_fwd(f"s5_scan{l}", bu, a_bar)
        y_ssm = _mm(f"s5_read{l}", z, c_cat, jax.ShapeDtypeStruct((rows, mix_w), F32), (rows // ts, 4, 2),
                    pl.BlockSpec((None, ts, sw), lambda i, s, c: (c, i, s)), pl.BlockSpec((None, sw, cw), lambda i, s, c: (c, s, s)),
                    pl.BlockSpec((ts, cw), lambda i, s, c: (i, s)), NN, k_axis=2)
        (h0,) = _rowwise(f"cv_glu{l}", _glu, [(proj, 1, mix_w), (proj, 2, mix_w)], [], 1, [(mix_w, F32)], tm)
        h1 = dwconv_fwd(f"cv_conv{l}", h0.reshape(1, rows, mix_w), 0, mix_w, cv_w_dw[l][None], vec(inp['cv_b_dw'])[None],
                        CV_TAPS, tc)[0]
        xc = dwconv_fwd(f"lru_conv{l}", proj3, 3, mix_w, lru_w_conv[l][None], vec(inp['lru_b_conv'])[None], LRU_TAPS, tc)[0]
        a_t, b_t = _rowwise(f"lru_gate{l}", _lru_gate, [(xc, 0, mix_w)], gate_pars, 2, [(mix_w, F32), (mix_w, F32)], tm)
        hseq = lru_scan_fwd(f"lru_scan{l}", a_t, b_t)
        dgp = pool_fwd(f"pool{l}", proj, 5, tc)
        got = gathered(4 * l + 1, proj)
        w_out = got[('w_out', l)].reshape(d, d)
        w_glu, w_pw = got[('s5_w_glu', l)].reshape(mix_w, mix_w), got[('cv_w_pw', l)].reshape(mix_w, mix_w)
        post_pars = [vec(inp['s5_d']), w_glu, vec(inp['s5_b_glu']), vec(inp['cv_ln_g']), vec(inp['cv_ln_b']), w_pw,
                     vec(inp['cv_b_pw']), pool_bd, vec(inp['pool_scale'])]
        post_rows = [(y_ssm, 0, mix_w), (proj, 0, mix_w), (h1, 0, mix_w), (hseq, 0, mix_w), (proj, 4, mix_w), (dgp, 0, mix_w)]
        (mixed,) = _rowwise(f"mix_post{l}", _mix_post, post_rows, post_pars, 1, [(d, BF16)], tm)
        x1 = _mm(f"out_proj{l}", mixed, w_out, jax.ShapeDtypeStruct((rows, d), F32), (2, rows // tm),
                 pl.BlockSpec((tm, d), lambda j, i: (i, 0)), pl.BlockSpec((d, d // 2), lambda j, i: (0, j)),
                 pl.BlockSpec((tm, d // 2), lambda j, i: (i, j)), NN,
                 add=xcur, add_spec=pl.BlockSpec((tm, d // 2), lambda j, i: (i, j)))

        h2 = rms_fwd(f"rms_ffn{l}", x1, vec(inp['norm_ffn_g']), tm)
        tu = min(256, rows)
        w_up = gathered(4 * l + 2, x1)[('ffn_w_up', l)]
        up = _mm(f"ffn_up{l}", h2, w_up, jax.ShapeDtypeStruct((4, rows, ffn_half), F32), (4, rows // tu),
                 pl.BlockSpec((tu, d), lambda k, i: (i, 0)), pl.BlockSpec((None, d, ffn_half), lambda k, i: (k, 0, 0)),
                 pl.BlockSpec((None, tu, ffn_half), lambda k, i: (k, i, 0)), NN)
        w_dw = ffn_w_dw[l].reshape(FFN_TAPS, 2, ffn_half).transpose(1, 0, 2)
        b_dw = inp['ffn_b_dw'][l].reshape(2, 1, ffn_half)
        act = ffn_gate_fwd(f"ffn_gate{l}", up, w_dw, b_dw, tc)
        w_down = gathered(4 * l + 3, up)[('ffn_w_down', l)].reshape(2, ffn_half, d)
        x2 = _mm(f"ffn_down{l}", act, w_down, jax.ShapeDtypeStruct((rows, d), F32), (rows // tm, 4),
                 pl.BlockSpec((2, tm, ffn_half), lambda i, j: (0, i, 0)), pl.BlockSpec((2, ffn_half, d // 4), lambda i, j: (0, 0, j)),
                 pl.BlockSpec((tm, d // 4), lambda i, j: (i, j)), NN, inner=("lead", 2),
                 add=x1, add_spec=pl.BlockSpec((tm, d // 4), lambda i, j: (i, j)))
        saved.append(dict(x=xcur, h=h, proj=proj, z=z, y_ssm=y_ssm, h0=h0, h1=h1, xc=xc, a_t=a_t, hseq=hseq, dgp=dgp,
                          mixed=mixed, x1=x1, h2=h2, up=up, act=act, w_in=w_in, w_out=w_out, w_up=w_up, w_down=w_down,
                          a_bar=a_bar, b_bar=b_bar, c_cat=c_cat, post_pars=post_pars, gate_pars=gate_pars, w_dw=w_dw, b_dw=b_dw,
                          s5=(lam_re, lam_im, log_step, b_re, b_im), cv_w=cv_w_dw[l][None], lru_w=lru_w_conv[l][None]))
        xcur = x2

    loss_row, dx, dg_final = final_loss("final_loss", xcur, inp['norm_final_g'].reshape(1, d), target, tm)
    grads['norm_final_g'] = dg_final.reshape(d)

    big_g = {nme: [None] * depth for nme in BIG}
    reduce_groups = []

    def start_reduce(tag, keys):
        pieces = []
        for nme, lyr in keys:
            g = big_g[nme][lyr]
            if nme == 'ffn_w_down':
                g = g.reshape(2, 4, ffn_half // 2, d // 2)
            pieces.append(reduce_cores(f"reduce_cores_{nme}{lyr}", g.reshape(2, -1, g.shape[-1])).reshape(g.shape[1:]))
        landing = [lax.empty((3,) + p.shape[1:], p.dtype) for p in pieces]
        groups, token = exchange_start(f"reduce_start_{tag}", [list(zip(pieces, landing))], xy, True)
        reduce_groups.append((tag, keys, groups[0]))
        return token

    for l in reversed(range(depth)):
        s = saved[l]
        ncol = s['w_in'].shape[2]
        nh = n_state // 2
        tu = min(256, rows)
        dact = _mm(f"d_act{l}", dx, s['w_down'], jax.ShapeDtypeStruct((2, rows, ffn_half), F32), (2, rows // tu),
                   pl.BlockSpec((tu, d), lambda k, i: (i, 0)), pl.BlockSpec((None, ffn_half, d), lambda k, i: (k, 0, 0)),
                   pl.BlockSpec((None, tu, ffn_half), lambda k, i: (k, i, 0)), NT)
        tn = d // 4
        tk = min(1024, rows)
        big_g['ffn_w_down'][l] = _mm(
            f"dw_down{l}", s['act'], dx, jax.ShapeDtypeStruct((2, 2, ffn_half, d // 2), F32), (2, 4, rows // tk),
            pl.BlockSpec((None, tk, ffn_half), lambda hh, n, k: (hh, k, 0)), pl.BlockSpec((tk, tn), lambda hh, n, k: (k, n)),
            pl.BlockSpec((None, None, ffn_half, tn), lambda hh, n, k: (n // 2, hh, 0, n % 2)), TN, k_axis=2)
        dup, dw_dw, db_dw = ffn_gate_bwd(f"ffn_gate_bwd{l}", s['up'], dact, s['w_dw'], s['b_dw'], tc)
        grads['ffn_w_dw'][l] = dw_dw.transpose(1, 0, 2).reshape(FFN_TAPS, 2 * ffn_half)
        grads['ffn_b_dw'][l] = db_dw.reshape(2 * ffn_half)
        dup = dup.reshape(4, rows, ffn_half)
        tm2 = min(1024, rows)
        dh2 = _mm(f"d_h2{l}", dup, s['w_up'], jax.ShapeDtypeStruct((rows, d), F32), (rows // tm2, 2, 4),
                  pl.BlockSpec((None, tm2, ffn_half), lambda i, j, k: (k, i, 0)), pl.BlockSpec((None, d // 2, ffn_half), lambda i, j, k: (k, j, 0)),
                  pl.BlockSpec((tm2, d // 2), lambda i, j, k: (i, j)), NT, k_axis=2)
        tmm = d // 4
        big_g['ffn_w_up'][l] = _mm(
            f"dw_up{l}", dup, s['h2'], jax.ShapeDtypeStruct((2, 4, ffn_half, d // 2), F32), (4, 4, rows // tk),
            pl.BlockSpec((None, tk, ffn_half), lambda k4, n, k: (k4, k, 0)), pl.BlockSpec((tk, tn), lambda k4, n, k: (k, n)),
            pl.BlockSpec((None, None, ffn_half, tn), lambda k4, n, k: (n // 2, k4, 0, n % 2)), TN, k_axis=2)
        token = start_reduce(f"ffn{l}", [('ffn_w_down', l), ('ffn_w_up', l)])
        dx1, dg = rms_bwd(f"rms_ffn_bwd{l}", s['x1'], inp['norm_ffn_g'][l].reshape(1, d) + token[0, 0], dh2, dx, tm)
        grads['norm_ffn_g'][l] = dg.reshape(d)
        dmixed = _mm(f"d_mixed{l}", dx1, s['w_out'], jax.ShapeDtypeStruct((rows, d), F32), (4, rows // tm),
                     pl.BlockSpec((tm, d), lambda j, i: (i, 0)), pl.BlockSpec((d // 4, d), lambda j, i: (j, 0)),
                     pl.BlockSpec((tm, d // 4), lambda j, i: (i, j)), NT)
        tq = mix_w // 2
        big_g['w_out'][l] = _mm(
            f"dw_out{l}", s['mixed'], dx1, jax.ShapeDtypeStruct((2, 4, tq, d), F32), (8, rows // tk),
            pl.BlockSpec((tk, tq), lambda t, k: (k, t)), pl.BlockSpec((tk, d), lambda t, k: (k, 0)),
            pl.BlockSpec((None, None, tq, d), lambda t, k: (t % 2, t // 2, 0, 0)), TN, k_axis=1)
        post_rows = [(s['y_ssm'], 0, mix_w), (s['proj'], 0, mix_w), (s['h1'], 0, mix_w), (s['hseq'], 0, mix_w),
                     (s['proj'], 4, mix_w), (s['dgp'], 0, mix_w), (dmixed, 0, d)]
        res = _rowwise(f"mix_post_bwd{l}", _mix_post, post_rows, s['post_pars'], 1, [(mix_w, F32)] * 6, tm, with_grads=True)
        dy_ssm, du_dir, dh1, dhseq, dlru_g, ddgp = res[:6]
        dd, dwglu, dbglu, dlng, dlnb, dwpw, dbpw, dpoolbd, dscale = res[6:]
        grads['s5_d'][l], grads['s5_b_glu'][l] = dd.reshape(mix_w), dbglu.reshape(mix_w)
        grads['cv_ln_g'][l], grads['cv_ln_b'][l], grads['cv_b_pw'][l] = dlng.reshape(mix_w), dlnb.reshape(mix_w), dbpw.reshape(mix_w)
        grads['pool_w'][l] = _diag_blocks(dpoolbd, len(POOL_WINDOWS))
        grads['pool_scale'][l] = dscale.reshape(mix_w)
        big_g['s5_w_glu'][l] = _halves(dwglu, 4)
        big_g['cv_w_pw'][l] = _halves(dwpw, 4)
        ts = min(2048, rows)
        cw, sw = mix_w // 4, n_state // 4
        slab = jnp.arange(mix_w)[:, None] // cw == jnp.arange(n_state)[None, :] // sw
        dz = _mm(f"s5_dz{l}", dy_ssm, s['c_cat'], jax.ShapeDtypeStruct((2, rows, n_state), F32), (rows // ts, 2, 4),
                 pl.BlockSpec((ts, cw), lambda i, c, q: (i, q)), pl.BlockSpec((None, sw, cw), lambda i, c, q: (c, q, q)),
                 pl.BlockSpec((None, ts, sw), lambda i, c, q: (c, i, q)), NT)
        dccat = _mm(f"s5_dc{l}", s['z'], dy_ssm, jax.ShapeDtypeStruct((2, n_state, mix_w), F32), (2, 4, rows // tk),
                    pl.BlockSpec((None, tk, sw), lambda c, q, k: (c, k, q)), pl.BlockSpec((tk, cw), lambda c, q, k: (k, q)),
                    pl.BlockSpec((None, sw, cw), lambda c, q, k: (c, q, q)), TN, k_axis=2)
        dccat = jnp.where(slab.T, dccat, 0.0)
        grads['s5_c_re'][l] = _diag_blocks(dccat[0], S5_GROUPS).transpose(0, 2, 1)
        grads['s5_c_im'][l] = -_diag_blocks(dccat[1], S5_GROUPS).transpose(0, 2, 1)
        lam, da_bar = s5_scan_bwd(f"s5_scan_bwd{l}", dz, s['z'], s['a_bar'])
        du = _mm(f"s5_du{l}", lam, s['b_bar'], jax.ShapeDtypeStruct((rows, mix_w), F32), (rows // ts, 4, 2),
                 pl.BlockSpec((None, ts, sw), lambda i, q, c: (c, i, q)), pl.BlockSpec((None, cw, sw), lambda i, q, c: (c, q, q)),
                 pl.BlockSpec((ts, cw), lambda i, q, c: (i, q)), NT, k_axis=2,
                 add=du_dir, add_spec=pl.BlockSpec((ts, cw), lambda i, q, c: (i, q)))
        dbbar = _mm(f"s5_db{l}", s['proj'], lam, jax.ShapeDtypeStruct((2, mix_w, n_state), F32), (2, 4, rows // tk),
                    pl.BlockSpec((tk, cw), lambda c, q, k: (k, q)), pl.BlockSpec((None, tk, sw), lambda c, q, k: (c, k, q)),
                    pl.BlockSpec((None, cw, sw), lambda c, q, k: (c, q, q)), TN, k_axis=2)
        dbbar = jnp.where(slab, dbbar, 0.0)
        dlr, dli, dls, dbre, dbim = s5_param_bwd(f"s5_param_bwd{l}", *s['s5'], da_bar, dbbar, gsum)
        grads['s5_lam_re'][l] = dlr.reshape(S5_GROUPS, S5_STATE)
        grads['s5_lam_im'][l] = dli.reshape(S5_GROUPS, S5_STATE)
        grads['s5_log_step'][l] = dls[0, :S5_GROUPS]
        grads['s5_b_re'][l] = _diag_blocks(dbre, S5_GROUPS).transpose(0, 2, 1)
        grads['s5_b_im'][l] = _diag_blocks(dbim, S5_GROUPS).transpose(0, 2, 1)
        dh0, dw_cv, db_cv = dwconv_bwd(f"cv_conv_bwd{l}", dh1.reshape(1, rows, mix_w), s['h0'].reshape(1, rows, mix_w), 0, mix_w,
                                       s['cv_w'], CV_TAPS, tc)
        grads['cv_w_dw'][l], grads['cv_b_dw'][l] = dw_cv[0], db_cv.reshape(mix_w)
        dv, dgg = _rowwise(f"cv_glu_bwd{l}", _glu, [(s['proj'], 1, mix_w), (s['proj'], 2, mix_w), (dh0[0], 0, mix_w)], [], 1,
                           [(mix_w, F32)] * 2, tm, with_grads=True)
        da_t, db_t = lru_scan_bwd(f"lru_scan_bwd{l}", dhseq, s['a_t'], s['hseq'])
        res = _rowwise(f"lru_gate_bwd{l}", _lru_gate, [(s['xc'], 0, mix_w), (da_t, 0, mix_w), (db_t, 0, mix_w)], s['gate_pars'], 2,
                       [(mix_w, F32)], tm, with_grads=True)
        dxc, dwr, dwi, dbr, dbi, dlam = res
        grads['lru_w_r'][l], grads['lru_w_i'][l] = _diag_blocks(dwr, LRU_HEADS), _diag_blocks(dwi, LRU_HEADS)
        grads['lru_b_r'][l], grads['lru_b_i'][l], grads['lru_lam'][l] = dbr.reshape(mix_w), dbi.reshape(mix_w), dlam.reshape(mix_w)
        dlx, dw_lc, db_lc = dwconv_bwd(f"lru_conv_bwd{l}", dxc.reshape(1, rows, mix_w), s['proj'].reshape(1, rows, 4 * ncol), 3, mix_w,
                                       s['lru_w'], LRU_TAPS, tc)
        grads['lru_w_conv'][l], grads['lru_b_conv'][l] = dw_lc[0], db_lc.reshape(mix_w)
        dpx = pool_bwd(f"pool_bwd{l}", ddgp, tc)
        dproj = jnp.concatenate([du, dv, dgg, dlx[0], dlru_g, dpx], axis=-1)
        dh = _mm(f"d_h{l}", dproj, s['w_in'], jax.ShapeDtypeStruct((rows, d), F32), (4, rows // tm),
                 pl.BlockSpec((tm, 4 * ncol), lambda j, i: (i, 0)), pl.BlockSpec((4, d // 4, ncol), lambda j, i: (0, j, 0)),
                 pl.BlockSpec((tm, d // 4), lambda j, i: (i, j)), NT, inner=("cols", 4))
        tk2 = min(2048, rows)
        big_g['w_in'][l] = _mm(
            f"dw_in{l}", s['h'], dproj, jax.ShapeDtypeStruct((2, 4, d // 2, ncol), F32), (4, 4, rows // tk2),
            pl.BlockSpec((tk2, tmm), lambda k4, m, k: (k, m)), pl.BlockSpec((tk2, ncol), lambda k4, m, k: (k, k4)),
            pl.BlockSpec((None, None, tmm, ncol), lambda k4, m, k: (m // 2, k4, m % 2, 0)), TN, k_axis=2)
        token = start_reduce(f"mix{l}", [('w_out', l), ('s5_w_glu', l), ('cv_w_pw', l), ('w_in', l)])
        dx, dg = rms_bwd(f"rms_mix_bwd{l}", s['x'], inp['norm_mix_g'][l].reshape(1, d) + token[0, 0], dh, dx1, tm)
        grads['norm_mix_g'][l] = dg.reshape(d)

    small = [nme for nme in WEIGHTS if nme not in BIG]
    full_g = {nme: (grads[nme] if nme == 'norm_final_g' else jnp.stack(grads[nme])) for nme in small}
    flat = jnp.concatenate([full_g[nme].reshape(-1) for nme in small])
    packed = jnp.pad(flat, (0, (-flat.shape[0]) % (128 * 64))).reshape(-1, 128)
    chip_sum = pair_sum("small_pair_sum", packed)
    small_zone = place_tile("place_small_grads", chip_sum[None], 0, my_chip, dtype=F32)
    (small_group,), small_token = exchange_start("small_start", [[small_zone]], xy, False)

    t_full = {}
    for tag, keys, group in reduce_groups:
        pieces, parts = exchange_wait(f"reduce_wait_{tag}", group, small_token, xy, True)
        for key, own, got in zip(keys, pieces, parts):
            t_full[key] = sum_and_share(f"share_cores_{key[0]}{key[1]}", own, got, my_chip)

    outs = {}
    tiles = {'w_in': 256, 'w_out': 128, 'ffn_w_up': 128, 'ffn_w_down': 256, 's5_w_glu': 64, 'cv_w_pw': 64}
    for nme in BIG:
        g0, g1 = t_full[(nme, 0)], t_full[(nme, 1)]
        if nme == 'ffn_w_up':
            res = adamw_sharded(f"adamw_{nme}", *(jnp.swapaxes(inp[p + nme], 1, 2) for p in ('', 'm_', 'v_')), g0, g1, True, tiles[nme])
            outs[nme] = tuple(jnp.swapaxes(r, 1, 2) for r in res)
        else:
            outs[nme] = adamw_sharded(f"adamw_{nme}", inp[nme], inp['m_' + nme], inp['v_' + nme], g0, g1, nme == 'ffn_w_down', tiles[nme])

    after_big = sum(outs[nme][1].reshape(-1)[:1] for nme in BIG)
    (g4,) = exchange_wait("small_wait", small_group, after_big, xy, False)[1]
    gsum_small = sum_lead("sum_small", g4, 64).reshape(-1)
    red, off = {}, 0
    for nme in small:
        g = gsum_small[off:off + full_g[nme].size].reshape(full_g[nme].shape)
        off += full_g[nme].size
        if nme in SMALL_SHARDED:
            width = inp[nme].shape[2]
            g = lax.dynamic_slice_in_dim(g, my_chip[0] * width, width, axis=2)
        red[nme] = g

    def pack(tree):
        f = jnp.concatenate([tree[nme].reshape(-1) for nme in small])
        return jnp.pad(f, (0, (-f.shape[0]) % (128 * 64))).reshape(-1, 128)

    pd, pm, pv = adamw_flat("adamw_small", pack({n_: inp[n_] for n_ in small}), pack(red), pack({n_: inp['m_' + n_] for n_ in small}),
                            pack({n_: inp['v_' + n_] for n_ in small}), 64)
    off = 0
    for nme in small:
        size, shape = inp[nme].size, inp[nme].shape
        outs[nme] = (red[nme],) + tuple(p.reshape(-1)[off:off + size].reshape(shape) for p in (pd, pm, pv))
        off += size

    loss = lax.psum(loss_row[0, 0], ("x", "y", "c"))
    result = [loss, dx[None]]
    for part in range(4):
        result += [outs[nme][part] for nme in WEIGHTS]
    return tuple(result)


def kernel(x, norm_mix_g, w_in, s5_lam_re, s5_lam_im, s5_log_step, s5_b_re, s5_b_im, s5_c_re, s5_c_im, s5_d, s5_w_glu, s5_b_glu, cv_w_dw, cv_b_dw, cv_ln_g, cv_ln_b, cv_w_pw, cv_b_pw, lru_w_conv, lru_b_conv, lru_w_r, lru_b_r, lru_w_i, lru_b_i, lru_lam, pool_w, pool_scale, w_out, norm_ffn_g, ffn_w_up, ffn_w_dw, ffn_b_dw, ffn_w_down, norm_final_g, loss_target, m_norm_mix_g, m_w_in, m_s5_lam_re, m_s5_lam_im, m_s5_log_step, m_s5_b_re, m_s5_b_im, m_s5_c_re, m_s5_c_im, m_s5_d, m_s5_w_glu, m_s5_b_glu, m_cv_w_dw, m_cv_b_dw, m_cv_ln_g, m_cv_ln_b, m_cv_w_pw, m_cv_b_pw, m_lru_w_conv, m_lru_b_conv, m_lru_w_r, m_lru_b_r, m_lru_w_i, m_lru_b_i, m_lru_lam, m_pool_w, m_pool_scale, m_w_out, m_norm_ffn_g, m_ffn_w_up, m_ffn_w_dw, m_ffn_b_dw, m_ffn_w_down, m_norm_final_g, v_norm_mix_g, v_w_in, v_s5_lam_re, v_s5_lam_im, v_s5_log_step, v_s5_b_re, v_s5_b_im, v_s5_c_re, v_s5_c_im, v_s5_d, v_s5_w_glu, v_s5_b_glu, v_cv_w_dw, v_cv_b_dw, v_cv_ln_g, v_cv_ln_b, v_cv_w_pw, v_cv_b_pw, v_lru_w_conv, v_lru_b_conv, v_lru_w_r, v_lru_b_r, v_lru_w_i, v_lru_b_i, v_lru_lam, v_pool_w, v_pool_scale, v_w_out, v_norm_ffn_g, v_ffn_w_up, v_ffn_w_dw, v_ffn_b_dw, v_ffn_w_down, v_norm_final_g):
    inp = dict(locals())
    return _step(inp)
```

```python
import functools

import jax
import jax.numpy as jnp
from jax import lax
from jax.experimental import pallas as pl
from jax.experimental.pallas import tpu as pltpu

F32 = jnp.float32
BF16 = jnp.bfloat16

VMEM_LIMIT_BYTES = 56 * 1024 * 1024
SUBLANES = 8

EPS = 1e-6
S5_GROUPS, S5_STATE, S5_GROUP_CH = 32, 64, 16
LRU_HEADS, LRU_C = 8, 8.0
POOL_WINDOWS = (2, 4, 8, 16)
CV_TAPS, LRU_TAPS, FFN_TAPS = 31, 4, 3
SCAN_CHUNK = 64
GELU_K0, GELU_K1 = 0.7978845608028654, 0.044715

ADAM_LR, ADAM_B1, ADAM_B2, ADAM_EPS, ADAM_WD, ADAM_STEP = 0.001, 0.9, 0.999, 1e-08, 0.01, 10

NN = ((1,), (0,))
NT = ((1,), (1,))
TN = ((0,), (0,))

WEIGHTS = ['norm_mix_g', 'w_in', 's5_lam_re', 's5_lam_im', 's5_log_step', 's5_b_re', 's5_b_im', 's5_c_re', 's5_c_im',
           's5_d', 's5_w_glu', 's5_b_glu', 'cv_w_dw', 'cv_b_dw', 'cv_ln_g', 'cv_ln_b', 'cv_w_pw', 'cv_b_pw',
           'lru_w_conv', 'lru_b_conv', 'lru_w_r', 'lru_b_r', 'lru_w_i', 'lru_b_i', 'lru_lam', 'pool_w', 'pool_scale',
           'w_out', 'norm_ffn_g', 'ffn_w_up', 'ffn_w_dw', 'ffn_b_dw', 'ffn_w_down', 'norm_final_g']
BIG = ('w_in', 'w_out', 'ffn_w_up', 'ffn_w_down', 's5_w_glu', 'cv_w_pw')
SMALL_SHARDED = {'cv_w_dw': 2, 'lru_w_conv': 2, 'ffn_w_dw': 2}


def _params(sem=None):
    if sem is None:
        return pltpu.CompilerParams(vmem_limit_bytes=VMEM_LIMIT_BYTES)
    return pltpu.CompilerParams(dimension_semantics=sem, vmem_limit_bytes=VMEM_LIMIT_BYTES)


def _row_tile(rows, cap, mult=SUBLANES):
    best = mult
    for t in range(mult, min(rows, cap) + 1, mult):
        if rows % t == 0:
            best = t
    return best


def _bdot(a, b, dims=NN):
    return lax.dot_general(a.astype(BF16), b.astype(BF16), (dims, ((), ())), preferred_element_type=F32)


@jax.custom_vjp
def bdot(a, b):
    return _bdot(a, b)


def _bdot_fwd(a, b):
    return _bdot(a, b), (a, b)


def _bdot_bwd(res, g):
    a, b = res
    return _bdot(g, b, NT).astype(a.dtype), _bdot(a, g, TN).astype(b.dtype)


bdot.defvjp(_bdot_fwd, _bdot_bwd)


def _mm(name, a, b, out_sds, grid, a_spec, b_spec, o_spec, dims, k_axis=None, add=None, add_spec=None, inner=None):
    nk = grid[k_axis] if k_axis is not None else 1
    has_add = add is not None
    acc_shape = tuple(d for d in o_spec.block_shape if d is not None)
    acc_in_out = out_sds.dtype == F32

    def product(a_ref, b_ref):
        if inner is None:
            return _bdot(a_ref[...], b_ref[...], dims)
        kind, n = inner
        width = a_ref.shape[-1] // n
        acc = None
        for j in range(n):
            a_j = a_ref[j] if kind == "lead" else a_ref[:, j * width:(j + 1) * width]
            p = _bdot(a_j, b_ref[j], dims)
            acc = p if acc is None else acc + p
        return acc

    def body(*refs):
        a_ref, b_ref = refs[0], refs[1]
        add_ref = refs[2] if has_add else None
        o_ref = refs[3] if has_add else refs[2]
        prod = product(a_ref, b_ref)
        if k_axis is None:
            if has_add:
                prod = prod + add_ref[...]
            o_ref[...] = prod.astype(o_ref.dtype)
        else:
            acc_ref = o_ref if acc_in_out else refs[-1]
            k = pl.program_id(k_axis)

            @pl.when(k == 0)
            def _():
                acc_ref[...] = prod

            @pl.when(k > 0)
            def _():
                acc_ref[...] += prod

            if has_add or not acc_in_out:
                @pl.when(k == nk - 1)
                def _():
                    r = acc_ref[...]
                    if has_add:
                        r = r + add_ref[...]
                    o_ref[...] = r.astype(o_ref.dtype)

    sem = tuple("arbitrary" if d == k_axis else "parallel" for d in range(len(grid)))
    in_specs = [a_spec, b_spec] + ([add_spec] if has_add else [])
    args = (a, b) + ((add,) if has_add else ())
    scratch = [pltpu.VMEM(acc_shape, F32)] if (k_axis is not None and not acc_in_out) else []
    return pl.pallas_call(body, out_shape=out_sds, grid=grid, in_specs=in_specs, out_specs=o_spec,
                          scratch_shapes=scratch, compiler_params=_params(sem), name=name)(*args)


def _rms(x, g):
    return x * lax.rsqrt(jnp.mean(x * x, axis=-1, keepdims=True) + EPS) * g


def rms_fwd(name, x, g, tm):
    rows, d = x.shape

    def body(x_ref, g_ref, o_ref):
        o_ref[...] = _rms(x_ref[...], g_ref[...]).astype(BF16)

    return pl.pallas_call(
        body, out_shape=jax.ShapeDtypeStruct((rows, d), BF16), grid=(rows // tm,),
        in_specs=[pl.BlockSpec((tm, d), lambda i: (i, 0)), pl.BlockSpec((1, d), lambda i: (0, 0))],
        out_specs=pl.BlockSpec((tm, d), lambda i: (i, 0)), compiler_params=_params(("parallel",)), name=name)(x, g)


def rms_bwd(name, x, g, dh, dres, tm):
    rows, d = x.shape

    def body(x_ref, g_ref, dh_ref, dres_ref, dx_ref, dg_ref):
        _, vjp = jax.vjp(_rms, x_ref[...], g_ref[...])
        dx, dg = vjp(dh_ref[...])
        dx_ref[...] = dx + dres_ref[...]

        @pl.when(pl.program_id(0) == 0)
        def _():
            dg_ref[...] = jnp.zeros_like(dg_ref)

        dg_ref[...] += dg

    row = pl.BlockSpec((tm, d), lambda i: (i, 0))
    vec = pl.BlockSpec((1, d), lambda i: (0, 0))
    return pl.pallas_call(
        body, out_shape=(jax.ShapeDtypeStruct((rows, d), F32), jax.ShapeDtypeStruct((1, d), F32)), grid=(rows // tm,),
        in_specs=[row, vec, row, row], out_specs=(row, vec), compiler_params=_params(("arbitrary",)), name=name)(x, g, dh, dres)


def final_loss(name, x, g, target, tm):
    rows, d = x.shape

    def body(x_ref, g_ref, t_ref, l_ref, dx_ref, dg_ref):
        def f(xv, gv):
            e = _rms(xv, gv) - t_ref[...]
            return 0.5 * jnp.sum(jnp.mean(e * e, axis=-1))

        loss, (dx, dg) = jax.value_and_grad(f, argnums=(0, 1))(x_ref[...], g_ref[...])
        dx_ref[...] = dx

        @pl.when(pl.program_id(0) == 0)
        def _():
            l_ref[...] = jnp.zeros_like(l_ref)
            dg_ref[...] = jnp.zeros_like(dg_ref)

        l_ref[...] += jnp.full(l_ref.shape, loss, F32)
        dg_ref[...] += dg

    row = pl.BlockSpec((tm, d), lambda i: (i, 0))
    vec = pl.BlockSpec((1, d), lambda i: (0, 0))
    lspec = pl.BlockSpec((1, 128), lambda i: (0, 0))
    return pl.pallas_call(
        body, out_shape=(jax.ShapeDtypeStruct((1, 128), F32), jax.ShapeDtypeStruct((rows, d), F32), jax.ShapeDtypeStruct((1, d), F32)),
        grid=(rows // tm,), in_specs=[row, vec, row], out_specs=(lspec, row, vec),
        compiler_params=_params(("arbitrary",)), name=name)(x, g, target)


def _rowwise(name, fn, row_ins, par_ins, n_row_out, row_out_dtypes, tm, with_grads=False):
    rows = row_ins[0][0].shape[0]
    n_prim = len(row_ins) - (n_row_out if with_grads else 0)
    n_par = len(par_ins)

    def body(*refs):
        ins = [r[...] for r in refs[:len(row_ins) + n_par]]
        outs = refs[len(row_ins) + n_par:]
        prim, cts, pars = ins[:n_prim], ins[n_prim:len(row_ins)], ins[len(row_ins):]
        if not with_grads:
            res = fn(*prim, *pars)
            for o_ref, r in zip(outs, res):
                o_ref[...] = r.astype(o_ref.dtype)
            return
        _, vjp = jax.vjp(fn, *prim, *[p.astype(F32) for p in pars])
        grads = vjp(tuple(cts))
        for o_ref, gr in zip(outs[:n_prim], grads[:n_prim]):
            o_ref[...] = gr.astype(o_ref.dtype)

        @pl.when(pl.program_id(0) == 0)
        def _():
            for o_ref in outs[n_prim:]:
                o_ref[...] = jnp.zeros_like(o_ref)

        for o_ref, gr in zip(outs[n_prim:], grads[n_prim:]):
            o_ref[...] += gr.astype(F32)

    in_specs = [pl.BlockSpec((tm, w), (lambda i, c=c: (i, c))) for (_, c, w) in row_ins]
    in_specs += [pl.BlockSpec(p.shape, (lambda i, n=p.ndim: (0,) * n)) for p in par_ins]
    args = [a for (a, _, _) in row_ins] + list(par_ins)
    if not with_grads:
        out_shape = tuple(jax.ShapeDtypeStruct((rows, w), dt) for (w, dt) in row_out_dtypes)
        out_specs = tuple(pl.BlockSpec((tm, w), lambda i: (i, 0)) for (w, _) in row_out_dtypes)
        sem = ("parallel",)
    else:
        out_shape = tuple(jax.ShapeDtypeStruct((rows, w), dt) for (w, dt) in row_out_dtypes)
        out_shape += tuple(jax.ShapeDtypeStruct(p.shape, F32) for p in par_ins)
        out_specs = tuple(pl.BlockSpec((tm, w), lambda i: (i, 0)) for (w, _) in row_out_dtypes)
        out_specs += tuple(pl.BlockSpec(p.shape, (lambda i, n=p.ndim: (0,) * n)) for p in par_ins)
        sem = ("arbitrary",)
    return pl.pallas_call(body, out_shape=out_shape, grid=(rows // tm,), in_specs=in_specs, out_specs=out_specs,
                          compiler_params=_params(sem), name=name)(*args)


def _glu(v, g):
    return (v * jax.nn.sigmoid(g),)


def _neg_expm1(z):
    return -jnp.tanh(0.5 * z) * (jnp.exp(z) + 1.0)


def _lru_gate(xc, w_r, w_i, b_r, b_i, lam):
    r = jax.nn.sigmoid(bdot(xc, w_r) + b_r)
    i = jax.nn.sigmoid(bdot(xc, w_i) + b_i)
    log_a = -LRU_C * r * jax.nn.softplus(-lam)
    a = jnp.exp(log_a)
    mult = jnp.sqrt(_neg_expm1(2.0 * log_a))
    return a, mult * (i * xc)


def _layernorm(x, g, b):
    mu = jnp.mean(x, axis=-1, keepdims=True)
    var = jnp.mean(jnp.square(x - mu), axis=-1, keepdims=True)
    return (x - mu) * lax.rsqrt(var + EPS) * g + b


def _mix_post(y_ssm, u, h1, hseq, lru_g, dgp, s5_d, w_glu, b_glu, ln_g, ln_b, w_pw, b_pw, pool_bd, pool_scale):
    y = y_ssm + s5_d * u
    gl = jax.nn.gelu(y, approximate=True)
    out_s5 = gl * jax.nn.sigmoid(bdot(gl, w_glu) + b_glu)
    out_cv = bdot(jax.nn.silu(_layernorm(h1, ln_g, ln_b)), w_pw) + b_pw
    out_lru = hseq * jax.nn.gelu(lru_g, approximate=True)
    out_pool = bdot(dgp, pool_bd) * pool_scale
    return (jnp.concatenate([out_s5, out_cv, out_lru, out_pool], axis=-1),)


def _ffn_act(gc, val):
    return (jax.nn.gelu(gc, approximate=True) * val,)


def ffn_act_fwd(name, gc, up, tm):
    _, rows, c = gc.shape

    def body(g_ref, v_ref, o_ref):
        o_ref[...] = _ffn_act(g_ref[...], v_ref[...])[0].astype(BF16)

    return pl.pallas_call(
        body, out_shape=jax.ShapeDtypeStruct((2, rows, c), BF16), grid=(2, rows // tm),
        in_specs=[pl.BlockSpec((None, tm, c), lambda h, i: (h, i, 0)), pl.BlockSpec((None, tm, c), lambda h, i: (h + 2, i, 0))],
        out_specs=pl.BlockSpec((None, tm, c), lambda h, i: (h, i, 0)),
        compiler_params=_params(("parallel", "parallel")), name=name)(gc, up)


def ffn_act_bwd(name, gc, up, dact, tm):
    _, rows, c = gc.shape

    def body(g_ref, v_ref, d_ref, dg_ref, dv_ref):
        _, vjp = jax.vjp(_ffn_act, g_ref[...], v_ref[...])
        dg, dv = vjp((d_ref[...],))
        dg_ref[...] = dg
        dv_ref[...] = dv.astype(BF16)

    blk = pl.BlockSpec((None, tm, c), lambda h, i: (h, i, 0))
    return pl.pallas_call(
        body, out_shape=(jax.ShapeDtypeStruct((2, rows, c), F32), jax.ShapeDtypeStruct((2, rows, c), BF16)), grid=(2, rows // tm),
        in_specs=[blk, pl.BlockSpec((None, tm, c), lambda h, i: (h + 2, i, 0)), blk], out_specs=(blk, blk),
        compiler_params=_params(("parallel", "parallel")), name=name)(gc, up, dact)


def _halo_rows(taps):
    return -(-(taps - 1) // SUBLANES) * SUBLANES


def dwconv_fwd(name, x, cblk, c, w, b, taps, tm, out_dtype=F32):
    nb = w.shape[0]
    rows = x.shape[1]
    halo = _halo_rows(taps)
    per = tm // halo

    def body(x_ref, h_ref, w_ref, b_ref, o_ref):
        i = pl.program_id(1)
        prev = jnp.where(i > 0, h_ref[...], 0.0)
        ext = jnp.concatenate([prev, x_ref[...]], axis=0)
        acc = jnp.broadcast_to(b_ref[...], (tm, c))
        for k in range(taps):
            off = halo - (taps - 1) + k
            acc = acc + w_ref[k:k + 1, :] * ext[off:off + tm]
        o_ref[...] = acc.astype(o_ref.dtype)

    return pl.pallas_call(
        body, out_shape=jax.ShapeDtypeStruct((nb, rows, c), out_dtype), grid=(nb, rows // tm),
        in_specs=[pl.BlockSpec((None, tm, c), lambda n, i: (n, i, cblk)),
                  pl.BlockSpec((None, halo, c), lambda n, i: (n, jnp.maximum(i * per - 1, 0), cblk)),
                  pl.BlockSpec((None, taps, c), lambda n, i: (n, 0, 0)),
                  pl.BlockSpec((None, 1, c), lambda n, i: (n, 0, 0))],
        out_specs=pl.BlockSpec((None, tm, c), lambda n, i: (n, i, 0)),
        compiler_params=_params(("parallel", "parallel")), name=name)(x, x, w, b)


def dwconv_bwd(name, dy, x, cblk, c, w, taps, tm, dx_dtype=F32):
    nb = w.shape[0]
    rows = x.shape[1]
    halo = _halo_rows(taps)
    per = tm // halo
    n_tiles = rows // tm
    last_halo = rows // halo - 1

    def body(dy_ref, dn_ref, x_ref, xp_ref, w_ref, dx_ref, dw_ref, db_ref):
        i = pl.program_id(1)
        dyv = dy_ref[...]
        nxt = jnp.where(i < n_tiles - 1, dn_ref[...], 0.0)
        dext = jnp.concatenate([dyv, nxt], axis=0)
        prev = jnp.where(i > 0, xp_ref[...], 0.0)
        xext = jnp.concatenate([prev, x_ref[...]], axis=0)
        acc = jnp.zeros((tm, c), F32)

        @pl.when(i == 0)
        def _():
            dw_ref[...] = jnp.zeros_like(dw_ref)
            db_ref[...] = jnp.zeros_like(db_ref)

        for k in range(taps):
            acc = acc + w_ref[k:k + 1, :] * dext[taps - 1 - k:taps - 1 - k + tm]
            off = halo - (taps - 1) + k
            dw_ref[k:k + 1, :] += jnp.sum(dyv * xext[off:off + tm], axis=0, keepdims=True)
        dx_ref[...] = acc.astype(dx_ref.dtype)
        db_ref[...] += jnp.sum(dyv, axis=0, keepdims=True)

    return pl.pallas_call(
        body, out_shape=(jax.ShapeDtypeStruct((nb, rows, c), dx_dtype), jax.ShapeDtypeStruct((nb, taps, c), F32),
                         jax.ShapeDtypeStruct((nb, 1, c), F32)),
        grid=(nb, n_tiles),
        in_specs=[pl.BlockSpec((None, tm, c), lambda n, i: (n, i, 0)),
                  pl.BlockSpec((None, halo, c), lambda n, i: (n, jnp.minimum((i + 1) * per, last_halo), 0)),
                  pl.BlockSpec((None, tm, c), lambda n, i: (n, i, cblk)),
                  pl.BlockSpec((None, halo, c), lambda n, i: (n, jnp.maximum(i * per - 1, 0), cblk)),
                  pl.BlockSpec((None, taps, c), lambda n, i: (n, 0, 0))],
        out_specs=(pl.BlockSpec((None, tm, c), lambda n, i: (n, i, 0)), pl.BlockSpec((None, taps, c), lambda n, i: (n, 0, 0)),
                   pl.BlockSpec((None, 1, c), lambda n, i: (n, 0, 0))),
        compiler_params=_params(("parallel", "arbitrary")), name=name)(dy, dy, x, x, w)


def ffn_gate_fwd(name, up, w, b, tm):
    _, rows, c = up.shape
    halo = _halo_rows(FFN_TAPS)
    per = tm // halo

    def body(g_ref, gp_ref, v_ref, w_ref, b_ref, o_ref):
        i = pl.program_id(1)
        ext = jnp.concatenate([jnp.where(i > 0, gp_ref[...], 0.0), g_ref[...]], axis=0)
        gc = jnp.broadcast_to(b_ref[...], (tm, c))
        for k in range(FFN_TAPS):
            off = halo - (FFN_TAPS - 1) + k
            gc = gc + w_ref[k:k + 1, :] * ext[off:off + tm]
        o_ref[...] = _ffn_act(gc, v_ref[...])[0].astype(BF16)

    return pl.pallas_call(
        body, out_shape=jax.ShapeDtypeStruct((2, rows, c), BF16), grid=(2, rows // tm),
        in_specs=[pl.BlockSpec((None, tm, c), lambda h, i: (h, i, 0)),
                  pl.BlockSpec((None, halo, c), lambda h, i: (h, jnp.maximum(i * per - 1, 0), 0)),
                  pl.BlockSpec((None, tm, c), lambda h, i: (h + 2, i, 0)),
                  pl.BlockSpec((None, FFN_TAPS, c), lambda h, i: (h, 0, 0)), pl.BlockSpec((None, 1, c), lambda h, i: (h, 0, 0))],
        out_specs=pl.BlockSpec((None, tm, c), lambda h, i: (h, i, 0)),
        compiler_params=_params(("parallel", "parallel")), name=name)(up, up, up, w, b)


def ffn_gate_bwd(name, up, dact, w, b, tm):
    _, rows, c = up.shape
    halo = _halo_rows(FFN_TAPS)
    per = tm // halo
    n_tiles = rows // tm
    last_halo = rows // halo - 1
    n_ext = tm + halo

    def body(g_ref, gp_ref, gn_ref, v_ref, vn_ref, d_ref, dn_ref, w_ref, b_ref, dup_ref, dw_ref, db_ref):
        i = pl.program_id(1)
        gext = jnp.concatenate([jnp.where(i > 0, gp_ref[...], 0.0), g_ref[...], gn_ref[...]], axis=0)
        shifted = [gext[halo - (FFN_TAPS - 1) + k:halo - (FFN_TAPS - 1) + k + n_ext] for k in range(FFN_TAPS)]
        gc = jnp.broadcast_to(b_ref[...], (n_ext, c))
        for k in range(FFN_TAPS):
            gc = gc + w_ref[k:k + 1, :] * shifted[k]
        vext = jnp.concatenate([v_ref[...], vn_ref[...]], axis=0)
        dext = jnp.concatenate([d_ref[...], dn_ref[...]], axis=0)
        sq = gc * gc
        t = jnp.tanh(GELU_K0 * gc * (1.0 + GELU_K1 * sq))
        half = 0.5 * (1.0 + t)
        dval = dext * (gc * half)
        dgc = dext * vext * (half + 0.5 * gc * (1.0 - t * t) * (GELU_K0 * (1.0 + 3.0 * GELU_K1 * sq)))
        r = lax.broadcasted_iota(jnp.int32, (n_ext, c), 0)
        dgc = jnp.where((r < tm) | (i < n_tiles - 1), dgc, 0.0)
        dgate = jnp.zeros((tm, c), F32)
        for k in range(FFN_TAPS):
            dgate = dgate + w_ref[k:k + 1, :] * dgc[FFN_TAPS - 1 - k:FFN_TAPS - 1 - k + tm]
        dup_ref[0] = dgate.astype(BF16)
        dup_ref[1] = dval[:tm].astype(BF16)

        @pl.when(i == 0)
        def _():
            dw_ref[...] = jnp.zeros_like(dw_ref)
            db_ref[...] = jnp.zeros_like(db_ref)

        dgc_t = dgc[:tm]
        for k in range(FFN_TAPS):
            dw_ref[k:k + 1, :] += jnp.sum(dgc_t * shifted[k][:tm], axis=0, keepdims=True)
        db_ref[...] += jnp.sum(dgc_t, axis=0, keepdims=True)

    def tile(shift):
        return pl.BlockSpec((None, tm, c), lambda h, i: (h + shift, i, 0))

    def after(shift):
        return pl.BlockSpec((None, halo, c), lambda h, i: (h + shift, jnp.minimum((i + 1) * per, last_halo), 0))

    return pl.pallas_call(
        body, out_shape=(jax.ShapeDtypeStruct((2, 2, rows, c), BF16), jax.ShapeDtypeStruct((2, FFN_TAPS, c), F32),
                         jax.ShapeDtypeStruct((2, 1, c), F32)),
        grid=(2, n_tiles),
        in_specs=[tile(0), pl.BlockSpec((None, halo, c), lambda h, i: (h, jnp.maximum(i * per - 1, 0), 0)), after(0),
                  tile(2), after(2), tile(0), after(0),
                  pl.BlockSpec((None, FFN_TAPS, c), lambda h, i: (h, 0, 0)), pl.BlockSpec((None, 1, c), lambda h, i: (h, 0, 0))],
        out_specs=(pl.BlockSpec((2, None, tm, c), lambda h, i: (0, h, i, 0)), pl.BlockSpec((None, FFN_TAPS, c), lambda h, i: (h, 0, 0)),
                   pl.BlockSpec((None, 1, c), lambda h, i: (h, 0, 0))),
        compiler_params=_params(("parallel", "arbitrary")), name=name)(up, up, up, up, up, dact, dact, w, b)


POOL_HALO = 16


def pool_fwd(name, proj, cblk, tm):
    rows = proj.shape[0]
    c = 128 * len(POOL_WINDOWS)
    per = tm // POOL_HALO

    def body(x_ref, h_ref, o_ref):
        i = pl.program_id(0)
        xv = x_ref[...]
        ext = jnp.concatenate([jnp.where(i > 0, h_ref[...], 0.0), xv], axis=0)
        t1 = (lax.broadcasted_iota(jnp.int32, (tm, 128), 0) + i * tm + 1).astype(F32)
        outs = []
        for gi, win in enumerate(POOL_WINDOWS):
            seg = ext[:, gi * 128:(gi + 1) * 128]
            s = seg[POOL_HALO:POOL_HALO + tm]
            for j in range(1, win):
                s = s + seg[POOL_HALO - j:POOL_HALO - j + tm]
            outs.append(s / jnp.minimum(t1, float(win)) - xv[:, gi * 128:(gi + 1) * 128])
        o_ref[...] = jnp.concatenate(outs, axis=-1)

    return pl.pallas_call(
        body, out_shape=jax.ShapeDtypeStruct((rows, c), F32), grid=(rows // tm,),
        in_specs=[pl.BlockSpec((tm, c), lambda i: (i, cblk)),
                  pl.BlockSpec((POOL_HALO, c), lambda i: (jnp.maximum(i * per - 1, 0), cblk))],
        out_specs=pl.BlockSpec((tm, c), lambda i: (i, 0)), compiler_params=_params(("parallel",)), name=name)(proj, proj)


def pool_bwd(name, dd, tm):
    rows, c = dd.shape
    per = tm // POOL_HALO
    n_tiles = rows // tm
    last_halo = rows // POOL_HALO - 1

    def body(d_ref, n_ref, o_ref):
        i = pl.program_id(0)
        dv = d_ref[...]
        nxt = jnp.where(i < n_tiles - 1, n_ref[...], 0.0)
        t1 = (lax.broadcasted_iota(jnp.int32, (tm, 128), 0) + i * tm + 1).astype(F32)
        t1n = (lax.broadcasted_iota(jnp.int32, (POOL_HALO, 128), 0) + (i + 1) * tm + 1).astype(F32)
        outs = []
        for gi, win in enumerate(POOL_WINDOWS):
            sl = slice(gi * 128, (gi + 1) * 128)
            q = jnp.concatenate([dv[:, sl] / jnp.minimum(t1, float(win)), nxt[:, sl] / jnp.minimum(t1n, float(win))], axis=0)
            s = q[0:tm]
            for j in range(1, win):
                s = s + q[j:j + tm]
            outs.append(s - dv[:, sl])
        o_ref[...] = jnp.concatenate(outs, axis=-1)

    return pl.pallas_call(
        body, out_shape=jax.ShapeDtypeStruct((rows, c), F32), grid=(n_tiles,),
        in_specs=[pl.BlockSpec((tm, c), lambda i: (i, 0)),
                  pl.BlockSpec((POOL_HALO, c), lambda i: (jnp.minimum((i + 1) * per, last_halo), 0))],
        out_specs=pl.BlockSpec((tm, c), lambda i: (i, 0)), compiler_params=_params(("parallel",)), name=name)(dd, dd)


def _shift_down(v, s, fill):
    r = lax.broadcasted_iota(jnp.int32, v.shape, 0)
    return jnp.where(r >= s, pltpu.roll(v, s, 0), fill)


def _shift_up(v, s, fill):
    n = v.shape[0]
    r = lax.broadcasted_iota(jnp.int32, v.shape, 0)
    return jnp.where(r < n - s, pltpu.roll(v, n - s, 0), fill)


def _cscan_chunk(vr, vi, powers, reverse):
    for k, (qr, qi) in enumerate(powers):
        s = 1 << k
        if reverse:
            sr, si = _shift_up(vr, s, 0.0), _shift_up(vi, s, 0.0)
            vr, vi = vr + qr * sr + qi * si, vi + qr * si - qi * sr
        else:
            sr, si = _shift_down(vr, s, 0.0), _shift_down(vi, s, 0.0)
            vr, vi = vr + qr * sr - qi * si, vi + qr * si + qi * sr
    return vr, vi


def _powers(pr, pi, n):
    out = [(pr, pi)]
    for _ in range(n - 1):
        pr, pi = pr * pr - pi * pi, 2.0 * pr * pi
        out.append((pr, pi))
    return out


def s5_scan_fwd(name, bu, a):
    _, rows, n = bu.shape
    t = min(SCAN_CHUNK, rows)
    steps = t.bit_length() - 1

    def body(bu_ref, a_ref, z_ref):
        pr, pi = a_ref[0], a_ref[1]
        powers = _powers(pr, pi, steps)
        r = lax.broadcasted_iota(jnp.int32, (t, 128), 0)
        tr, ti = _cscan_chunk(jnp.where(r == 0, pr, 0.0), jnp.where(r == 0, pi, 0.0), powers, False)

        def chunk(ci, carry):
            base = pl.multiple_of(ci * t, t)
            vr, vi = _cscan_chunk(bu_ref[0, pl.ds(base, t), :], bu_ref[1, pl.ds(base, t), :], powers, False)
            cr, cim = carry
            zr = vr + tr * cr - ti * cim
            zi = vi + tr * cim + ti * cr
            z_ref[0, pl.ds(base, t), :] = zr
            z_ref[1, pl.ds(base, t), :] = zi
            return zr[t - 1:t, :], zi[t - 1:t, :]

        zero = jnp.zeros((1, 128), F32)
        lax.fori_loop(0, rows // t, chunk, (zero, zero))

    return pl.pallas_call(
        body, out_shape=jax.ShapeDtypeStruct((2, rows, n), F32), grid=(n // 128,),
        in_specs=[pl.BlockSpec((2, rows, 128), lambda j: (0, 0, j)), pl.BlockSpec((2, 1, 128), lambda j: (0, 0, j))],
        out_specs=pl.BlockSpec((2, rows, 128), lambda j: (0, 0, j)), compiler_params=_params(("parallel",)), name=name)(bu, a)


def s5_scan_bwd(name, dz, z, a):
    _, rows, n = dz.shape
    t = min(SCAN_CHUNK, rows)
    steps = t.bit_length() - 1
    n_chunks = rows // t

    def body(dz_ref, z_ref, a_ref, lam_ref, da_ref):
        pr, pi = a_ref[0], a_ref[1]
        powers = _powers(pr, pi, steps)
        r = lax.broadcasted_iota(jnp.int32, (t, 128), 0)
        tr, ti = _cscan_chunk(jnp.where(r == t - 1, pr, 0.0), jnp.where(r == t - 1, -pi, 0.0), powers, True)

        def chunk(k, carry):
            ci = n_chunks - 1 - k
            base = pl.multiple_of(ci * t, t)
            vr, vi = _cscan_chunk(dz_ref[0, pl.ds(base, t), :], dz_ref[1, pl.ds(base, t), :], powers, True)
            cr, cim, dar, dai = carry
            lr = vr + tr * cr - ti * cim
            li = vi + tr * cim + ti * cr
            lam_ref[0, pl.ds(base, t), :] = lr
            lam_ref[1, pl.ds(base, t), :] = li
            pbase = pl.multiple_of(jnp.maximum(base - SUBLANES, 0), SUBLANES)
            keep = (ci > 0).astype(F32)
            pzr = z_ref[0, pl.ds(pbase, SUBLANES), :][SUBLANES - 1:SUBLANES, :] * keep
            pzi = z_ref[1, pl.ds(pbase, SUBLANES), :][SUBLANES - 1:SUBLANES, :] * keep
            zpr = _shift_down(z_ref[0, pl.ds(base, t), :], 1, pzr)
            zpi = _shift_down(z_ref[1, pl.ds(base, t), :], 1, pzi)
            dar = dar + jnp.sum(lr * zpr + li * zpi, axis=0, keepdims=True)
            dai = dai + jnp.sum(li * zpr - lr * zpi, axis=0, keepdims=True)
            return lr[0:1, :], li[0:1, :], dar, dai

        zero = jnp.zeros((1, 128), F32)
        _, _, dar, dai = lax.fori_loop(0, n_chunks, chunk, (zero, zero, zero, zero))
        da_ref[0] = dar
        da_ref[1] = dai

    seq = pl.BlockSpec((2, rows, 128), lambda j: (0, 0, j))
    vec = pl.BlockSpec((2, 1, 128), lambda j: (0, 0, j))
    return pl.pallas_call(
        body, out_shape=(jax.ShapeDtypeStruct((2, rows, n), F32), jax.ShapeDtypeStruct((2, 1, n), F32)), grid=(n // 128,),
        in_specs=[seq, seq, vec], out_specs=(seq, vec), compiler_params=_params(("parallel",)), name=name)(dz, z, a)


def _rscan_chunk(a, b, steps, reverse):
    shift = _shift_up if reverse else _shift_down
    for k in range(steps):
        s = 1 << k
        b = b + a * shift(b, s, 0.0)
        a = a * shift(a, s, 1.0)
    return a, b


def lru_scan_fwd(name, a, b):
    rows, n = a.shape
    t = min(SCAN_CHUNK, rows)
    steps = t.bit_length() - 1

    def body(a_ref, b_ref, h_ref):
        def chunk(ci, carry):
            base = pl.multiple_of(ci * t, t)
            pa, hb = _rscan_chunk(a_ref[pl.ds(base, t), :], b_ref[pl.ds(base, t), :], steps, False)
            h = hb + pa * carry
            h_ref[pl.ds(base, t), :] = h
            return h[t - 1:t, :]

        lax.fori_loop(0, rows // t, chunk, jnp.zeros((1, 128), F32))

    seq = pl.BlockSpec((rows, 128), lambda j: (0, j))
    return pl.pallas_call(body, out_shape=jax.ShapeDtypeStruct((rows, n), F32), grid=(n // 128,), in_specs=[seq, seq],
                          out_specs=seq, compiler_params=_params(("parallel",)), name=name)(a, b)


def lru_scan_bwd(name, dh, a, h):
    rows, n = a.shape
    t = min(SCAN_CHUNK, rows)
    steps = t.bit_length() - 1
    n_chunks = rows // t

    def body(dh_ref, a_ref, h_ref, da_ref, db_ref):
        def chunk(k, carry):
            ci = n_chunks - 1 - k
            base = pl.multiple_of(ci * t, t)
            nbase = pl.multiple_of(jnp.minimum(base + t, rows - SUBLANES), SUBLANES)
            a_next = a_ref[pl.ds(nbase, SUBLANES), :][0:1, :]
            an = _shift_up(a_ref[pl.ds(base, t), :], 1, a_next)
            pa, mb = _rscan_chunk(an, dh_ref[pl.ds(base, t), :], steps, True)
            mu = mb + pa * carry
            pbase = pl.multiple_of(jnp.maximum(base - SUBLANES, 0), SUBLANES)
            hp_row = h_ref[pl.ds(pbase, SUBLANES), :][SUBLANES - 1:SUBLANES, :] * (ci > 0).astype(F32)
            hp = _shift_down(h_ref[pl.ds(base, t), :], 1, hp_row)
            da_ref[pl.ds(base, t), :] = mu * hp
            db_ref[pl.ds(base, t), :] = mu
            return mu[0:1, :]

        lax.fori_loop(0, n_chunks, chunk, jnp.zeros((1, 128), F32))

    seq = pl.BlockSpec((rows, 128), lambda j: (0, j))
    return pl.pallas_call(
        body, out_shape=(jax.ShapeDtypeStruct((rows, n), F32), jax.ShapeDtypeStruct((rows, n), F32)), grid=(n // 128,),
        in_specs=[seq, seq, seq], out_specs=(seq, seq), compiler_params=_params(("parallel",)), name=name)(dh, a, h)


def _s5_param(lr, li, ls, bre, bim):
    st = jnp.exp(ls)
    er = jnp.exp(lr * st)
    th = li * st
    ar, ai = er * jnp.cos(th), er * jnp.sin(th)
    nr, ni = ar - 1.0, ai
    den = lr * lr + li * li
    cr, ci = (nr * lr + ni * li) / den, (ni * lr - nr * li) / den
    return ar, ai, cr * bre - ci * bim, cr * bim + ci * bre


def s5_param_fwd(name, lr, li, ls, bre, bim):
    gh, n = bre.shape

    def body(lr_ref, li_ref, ls_ref, bre_ref, bim_ref, a_ref, bb_ref):
        ar, ai, br, bi = _s5_param(lr_ref[...], li_ref[...], ls_ref[...], bre_ref[...], bim_ref[...])
        a_ref[0] = ar
        a_ref[1] = ai
        bb_ref[0] = br.astype(BF16)
        bb_ref[1] = bi.astype(BF16)

    return pl.pallas_call(body, out_shape=(jax.ShapeDtypeStruct((2, 1, n), F32), jax.ShapeDtypeStruct((2, gh, n), BF16)),
                          compiler_params=_params(), name=name)(lr, li, ls, bre, bim)


def s5_param_bwd(name, lr, li, ls, bre, bim, da, dbb, gsum):
    gh, n = bre.shape

    def body(lr_ref, li_ref, ls_ref, bre_ref, bim_ref, da_ref, dbb_ref, gs_ref, dlr_ref, dli_ref, dls_ref, dbre_ref, dbim_ref):
        _, vjp = jax.vjp(_s5_param, lr_ref[...], li_ref[...], ls_ref[...], bre_ref[...], bim_ref[...])
        dlr, dli, dls, dbre, dbim = vjp((da_ref[0], da_ref[1], dbb_ref[0], dbb_ref[1]))
        dlr_ref[...] = dlr
        dli_ref[...] = dli
        dls_ref[...] = jnp.dot(jnp.broadcast_to(dls, (SUBLANES, n)), gs_ref[...], preferred_element_type=F32,
                               precision=lax.Precision.HIGHEST)
        dbre_ref[...] = dbre
        dbim_ref[...] = dbim

    vec = jax.ShapeDtypeStruct((1, n), F32)
    mat = jax.ShapeDtypeStruct((gh, n), F32)
    return pl.pallas_call(body, out_shape=(vec, vec, jax.ShapeDtypeStruct((SUBLANES, 128), F32), mat, mat),
                          compiler_params=_params(), name=name)(lr, li, ls, bre, bim, da, dbb, gsum)


def sum_lead(name, x, tr):
    n, rows, cols = x.shape

    def body(x_ref, o_ref):
        acc = x_ref[0]
        for j in range(1, n):
            acc = acc + x_ref[j]
        o_ref[...] = acc

    return pl.pallas_call(
        body, out_shape=jax.ShapeDtypeStruct((rows, cols), x.dtype), grid=(rows // tr,),
        in_specs=[pl.BlockSpec((n, tr, cols), lambda i: (0, i, 0))], out_specs=pl.BlockSpec((tr, cols), lambda i: (i, 0)),
        compiler_params=_params(("parallel",)), name=name)(x)


def _adamw(w, g, m, v):
    m = ADAM_B1 * m + (1.0 - ADAM_B1) * g
    v = ADAM_B2 * v + (1.0 - ADAM_B2) * jnp.square(g)
    m_hat = m / (1.0 - ADAM_B1 ** ADAM_STEP)
    v_hat = v / (1.0 - ADAM_B2 ** ADAM_STEP)
    delta = -ADAM_LR * (m_hat / (jnp.sqrt(v_hat) + ADAM_EPS) + ADAM_WD * w)
    return delta, m, v


def adamw_sharded(name, w, m, v, g0, g1, split_cols, tile):
    _, r, c = w.shape
    if split_cols:
        nt = c // tile
        per = (c // 2) // tile
        wspec = pl.BlockSpec((None, r, tile), lambda l, t: (l, 0, t))
        gspec = pl.BlockSpec((None, r, tile), lambda l, t: (t // per, 0, t % per))
    else:
        nt = r // tile
        per = (r // 2) // tile
        wspec = pl.BlockSpec((None, tile, c), lambda l, t: (l, t, 0))
        gspec = pl.BlockSpec((None, tile, c), lambda l, t: (t // per, t % per, 0))

    def body(w_ref, m_ref, v_ref, g0_ref, g1_ref, g_ref, d_ref, nm_ref, nv_ref):
        g = jnp.where(pl.program_id(0) == 0, g0_ref[...], g1_ref[...])
        d, nm, nv = _adamw(w_ref[...], g, m_ref[...], v_ref[...])
        g_ref[...] = g
        d_ref[...] = d
        nm_ref[...] = nm
        nv_ref[...] = nv

    sds = jax.ShapeDtypeStruct(w.shape, F32)
    return pl.pallas_call(body, out_shape=(sds,) * 4, grid=(2, nt), in_specs=[wspec, wspec, wspec, gspec, gspec],
                          out_specs=(wspec,) * 4, compiler_params=_params(("parallel", "parallel")), name=name)(w, m, v, g0, g1)


def adamw_flat(name, w, g, m, v, tr):
    rows, cols = w.shape

    def body(w_ref, g_ref, m_ref, v_ref, d_ref, nm_ref, nv_ref):
        d, nm, nv = _adamw(w_ref[...], g_ref[...], m_ref[...], v_ref[...])
        d_ref[...] = d
        nm_ref[...] = nm
        nv_ref[...] = nv

    blk = pl.BlockSpec((tr, cols), lambda i: (i, 0))
    sds = jax.ShapeDtypeStruct((rows, cols), F32)
    return pl.pallas_call(body, out_shape=(sds,) * 3, grid=(rows // tr,), in_specs=[blk] * 4, out_specs=(blk,) * 3,
                          compiler_params=_params(("parallel",)), name=name)(w, g, m, v)


def _flips(axes):
    out = []
    for fx in ((0, 1) if "x" in axes else (0,)):
        for fy in ((0, 1) if "y" in axes else (0,)):
            for fc in ((0, 1) if "c" in axes else (0,)):
                if fx or fy or fc:
                    out.append((fx, fy, fc))
    return out


def _slot(pos, axes):
    s = 0
    for name, p in zip(("x", "y", "c"), pos):
        if name in axes:
            s = 2 * s + p
    return s


def _exchange(name, arrs, axes, scatter):
    flips = _flips(axes)
    n = len(flips) + 1
    na = len(arrs)

    def body(*refs):
        ins, outs = refs[:na], refs[na:2 * na]
        send_sems, recv_sems, local_sems = refs[2 * na:]
        me = (lax.axis_index("x"), lax.axis_index("y"), lax.axis_index("c"))
        my = _slot(me, axes)
        peers = [tuple((1 - p) if f else p for p, f in zip(me, fl)) for fl in flips]

        def src(a, dest_slot):
            return ins[a].at[dest_slot] if scatter else ins[a]

        local = [pltpu.make_async_copy(src(a, my), outs[a].at[my], local_sems.at[a]) for a in range(na)]
        for cp in local:
            cp.start()

        def remote(a, j, landing_slot, dest_slot):
            return pltpu.make_async_remote_copy(
                src_ref=src(a, dest_slot), dst_ref=outs[a].at[landing_slot], send_sem=send_sems.at[a * len(flips) + j],
                recv_sem=recv_sems.at[a * len(flips) + j], device_id=peers[j], device_id_type=pl.DeviceIdType.MESH)

        sends = [remote(a, j, my, _slot(peers[j], axes)) for a in range(na) for j in range(len(flips))]
        for cp in sends:
            cp.start()
        for a in range(na):
            for j in range(len(flips)):
                remote(a, j, _slot(peers[j], axes), _slot(peers[j], axes)).wait_recv()
        for cp in sends:
            cp.wait_send()
        for cp in local:
            cp.wait()

    if scatter:
        out_shape = tuple(jax.ShapeDtypeStruct(a.shape, a.dtype) for a in arrs)
    else:
        out_shape = tuple(jax.ShapeDtypeStruct((n,) + a.shape, a.dtype) for a in arrs)
    anyspec = pl.BlockSpec(memory_space=pl.ANY)
    return pl.pallas_call(
        body, out_shape=out_shape, in_specs=[anyspec] * na, out_specs=(anyspec,) * na,
        scratch_shapes=[pltpu.SemaphoreType.DMA((na * len(flips),)), pltpu.SemaphoreType.DMA((na * len(flips),)),
                        pltpu.SemaphoreType.DMA((na,))],
        name=name)(*arrs)


def all_gather(name, arrs, axes):
    return _exchange(name, arrs, axes, False)


def all_to_all(name, arrs, axes):
    return _exchange(name, arrs, axes, True)


_HBM = pl.BlockSpec(memory_space=pltpu.HBM)
_SEM = pl.BlockSpec(memory_space=pltpu.SEMAPHORE)
_EFFECT = pltpu.SideEffectType.DATAFLOW_SIDE_EFFECTING


def place_own(name, arrs, axes, scatter):
    n = len(_flips(axes)) + 1
    na = len(arrs)

    def body(*refs):
        ins, outs, sems = refs[:na], refs[na:2 * na], refs[2 * na]
        my = _slot((lax.axis_index("x"), lax.axis_index("y"), lax.axis_index("c")), axes)
        copies = [pltpu.make_async_copy(ins[a].at[my] if scatter else ins[a], outs[a].at[my], sems.at[a]) for a in range(na)]
        for cp in copies:
            cp.start()
        for cp in copies:
            cp.wait()

    out_shape = tuple(jax.ShapeDtypeStruct(a.shape if scatter else (n,) + a.shape, a.dtype) for a in arrs)
    anyspec = pl.BlockSpec(memory_space=pl.ANY)
    return pl.pallas_call(body, out_shape=out_shape, in_specs=[anyspec] * na, out_specs=(anyspec,) * na,
                          scratch_shapes=[pltpu.SemaphoreType.DMA((na,))], name=name)(*arrs)


def _peers(axes):
    me = (lax.axis_index("x"), lax.axis_index("y"), lax.axis_index("c"))
    return me, [tuple((1 - p) if f else p for p, f in zip(me, fl)) for fl in _flips(axes)]


def place_tile(name, arr, layer, my, slots=4, dtype=BF16):
    _, r, cols = arr.shape
    tr = _tile_rows(r, cols)

    def body(my_ref, x_ref, o_ref):
        o_ref[...] = x_ref[...].astype(dtype)

    grid_spec = pltpu.PrefetchScalarGridSpec(
        num_scalar_prefetch=1, grid=(r // tr,), in_specs=[pl.BlockSpec((None, tr, cols), lambda i, my: (layer, i, 0))],
        out_specs=pl.BlockSpec((None, tr, cols), lambda i, my: (my[0], i, 0)))
    return pl.pallas_call(body, out_shape=jax.ShapeDtypeStruct((slots, r, cols), dtype), grid_spec=grid_spec,
                          compiler_params=_params(("parallel",)), name=name)(my, arr)


def exchange_start(name, groups, axes, scatter):
    flat = [(p if scatter else (p,)) for grp in groups for p in grp]
    per = 2 if scatter else 1
    na, ng, npeer = len(flat), len(groups), len(_flips(axes))

    def body(*refs):
        ops = refs[:per * na]
        zones = ops[(per - 1) * na:]
        sems, token = refs[per * na:per * na + 2 * ng], refs[-1]
        me, peers = _peers(axes)
        my = _slot(me, axes)
        ai = 0
        for g, grp in enumerate(groups):
            for k in range(len(grp)):
                for j, peer in enumerate(peers):
                    src = ops[ai].at[_slot(peer, axes)] if scatter else zones[ai].at[my]
                    dst = zones[ai].at[j] if scatter else zones[ai].at[my]
                    pltpu.make_async_remote_copy(
                        src_ref=src, dst_ref=dst, send_sem=sems[2 * g].at[k * npeer + j],
                        recv_sem=sems[2 * g + 1].at[k * npeer + j], device_id=peer, device_id_type=pl.DeviceIdType.MESH).start()
                ai += 1
        token[...] = jnp.zeros_like(token)

    out_shape, out_specs = [], []
    for grp in groups:
        out_shape += [pltpu.SemaphoreType.DMA((npeer * len(grp),))] * 2
        out_specs += [_SEM, _SEM]
    for idx in range(per):
        out_shape += [pltpu.HBM(p[idx].shape, p[idx].dtype) for p in flat]
        out_specs += [_HBM] * na
    out_shape.append(jax.ShapeDtypeStruct((SUBLANES, 128), F32))
    out_specs.append(pl.BlockSpec(memory_space=pltpu.VMEM))
    args = [pltpu.with_memory_space_constraint(p[idx], pltpu.HBM) for idx in range(per) for p in flat]
    res = pl.pallas_call(body, out_shape=tuple(out_shape), in_specs=[_HBM] * (per * na), out_specs=tuple(out_specs),
                         input_output_aliases={i: 2 * ng + i for i in range(per * na)},
                         compiler_params=pltpu.CompilerParams(has_side_effects=_EFFECT), name=name)(*args)
    thru = res[2 * ng:2 * ng + per * na]
    out, ai = [], 0
    for g, grp in enumerate(groups):
        srcs = list(thru[ai:ai + len(grp)]) if scatter else []
        zones = list(thru[(per - 1) * na + ai:(per - 1) * na + ai + len(grp)])
        out.append(((res[2 * g], res[2 * g + 1]), srcs, zones))
        ai += len(grp)
    return out, res[-1]


def exchange_wait(name, group, after, axes, scatter):
    (send_sems, recv_sems), srcs, zones = group
    n, ns = len(zones), len(srcs)
    npeer = len(_flips(axes))

    def body(*refs):
        z_refs = refs[ns:ns + n]
        ssem, rsem = refs[ns + n], refs[ns + n + 1]
        _, peers = _peers(axes)
        for k in range(n):
            for j, peer in enumerate(peers):
                part = z_refs[k].at[j if scatter else _slot(peer, axes)]
                copy = pltpu.make_async_remote_copy(
                    src_ref=part, dst_ref=part, send_sem=ssem.at[k * npeer + j], recv_sem=rsem.at[k * npeer + j],
                    device_id=peer, device_id_type=pl.DeviceIdType.MESH)
                copy.wait_send()
                copy.wait_recv()

    ops = list(srcs) + list(zones)
    out_shape = tuple(pltpu.HBM(a.shape, a.dtype) for a in ops)
    res = pl.pallas_call(body, out_shape=out_shape, in_specs=[_HBM] * len(ops) + [_SEM, _SEM, pl.BlockSpec(memory_space=pl.ANY)],
                         out_specs=(_HBM,) * len(ops), input_output_aliases={i: i for i in range(len(ops))},
                         compiler_params=pltpu.CompilerParams(has_side_effects=_EFFECT), name=name)(*ops, send_sems, recv_sems, after)
    return list(res[:ns]), list(res[ns:])


def _pair_exchange(name, ins, in_specs, n_steps, tile, fn_send, fn_out, out_shape, out_spec, prefetch=None, wire=F32):
    n_in = len(ins)

    def body(*refs):
        if prefetch is not None:
            refs = refs[1:]
        in_refs, o_ref = refs[:n_in], refs[n_in]
        send_buf, recv_buf, send_sems, recv_sems, credit = refs[n_in + 1:]
        i = pl.program_id(0)
        slot = lax.rem(i, 2)
        c = lax.axis_index("c")
        sibling = (lax.axis_index("x"), lax.axis_index("y"), 1 - c)
        vals = [r[...] for r in in_refs]
        send_buf[slot] = fn_send(*vals, c).astype(wire)

        @pl.when(i >= 2)
        def _():
            pl.semaphore_wait(credit, 1)

        copy = pltpu.make_async_remote_copy(
            src_ref=send_buf.at[slot], dst_ref=recv_buf.at[slot], send_sem=send_sems.at[slot], recv_sem=recv_sems.at[slot],
            device_id=sibling, device_id_type=pl.DeviceIdType.MESH)
        copy.start()
        copy.wait_recv()
        o_ref[...] = fn_out(*vals, recv_buf[slot], c).astype(o_ref.dtype)
        copy.wait_send()

        @pl.when(i < n_steps - 2)
        def _():
            pl.semaphore_signal(credit, inc=1, device_id=sibling, device_id_type=pl.DeviceIdType.MESH)

    scratch = [pltpu.VMEM((2,) + tile, wire), pltpu.VMEM((2,) + tile, wire), pltpu.SemaphoreType.DMA((2,)),
               pltpu.SemaphoreType.DMA((2,)), pltpu.SemaphoreType.REGULAR]
    if prefetch is None:
        return pl.pallas_call(body, out_shape=out_shape, grid=(n_steps,), in_specs=in_specs, out_specs=out_spec,
                              scratch_shapes=scratch, compiler_params=_params(("arbitrary",)), name=name)(*ins)
    grid_spec = pltpu.PrefetchScalarGridSpec(num_scalar_prefetch=1, grid=(n_steps,), in_specs=in_specs, out_specs=out_spec,
                                             scratch_shapes=scratch)
    return pl.pallas_call(body, out_shape=out_shape, grid_spec=grid_spec, compiler_params=_params(("arbitrary",)),
                          name=name)(prefetch, *ins)


def _tile_rows(rows, cols, f32_bytes=3 << 19):
    return _row_tile(rows, max(2 * SUBLANES, f32_bytes // (4 * cols)), 2 * SUBLANES)


def pair_sum(name, x):
    rows, cols = x.shape
    tr = _tile_rows(rows, cols)
    return _pair_exchange(name, [x], [pl.BlockSpec((tr, cols), lambda i: (i, 0))], rows // tr, (tr, cols),
                          lambda v, c: v, lambda v, got, c: v + got, jax.ShapeDtypeStruct((rows, cols), F32),
                          pl.BlockSpec((tr, cols), lambda i: (i, 0)))


def reduce_cores(name, g):
    _, m, cols = g.shape
    tr = _tile_rows(m, cols, 3 << 20)

    def fn_send(g0, g1, c):
        return jnp.where(c == 0, g1, g0)

    def fn_out(g0, g1, got, c):
        return jnp.where(c == 0, g0, g1) + got.astype(F32)

    return _pair_exchange(
        name, [g, g], [pl.BlockSpec((None, tr, cols), lambda i: (0, i, 0)), pl.BlockSpec((None, tr, cols), lambda i: (1, i, 0))],
        m // tr, (tr, cols), fn_send, fn_out, jax.ShapeDtypeStruct((m, cols), BF16), pl.BlockSpec((tr, cols), lambda i: (i, 0)),
        wire=BF16)


def sum_and_share(name, own, parts, my):
    n, r, cols = parts.shape
    tr = _tile_rows(r, cols, 3 << 20)

    def total(o, p):
        acc = o.astype(F32)
        for j in range(n):
            acc = acc + p[j].astype(F32)
        return acc

    def fn_send(o, p, c):
        return total(o, p)

    def fn_out(o, p, got, c):
        mine = total(o, p)
        return jnp.stack([jnp.where(c == 0, mine, got), jnp.where(c == 0, got, mine)])

    return _pair_exchange(
        name, [own, parts], [pl.BlockSpec((None, tr, cols), lambda i, my_ref: (my_ref[0], i, 0)), pl.BlockSpec((n, tr, cols), lambda i, my_ref: (0, i, 0))],
        r // tr, (tr, cols), fn_send, fn_out, jax.ShapeDtypeStruct((2, r, cols), F32),
        pl.BlockSpec((2, tr, cols), lambda i, my_ref: (0, i, 0)), prefetch=my)


def _block_diag(blocks):
    g, r, c = blocks.shape
    eye = jnp.eye(g, dtype=blocks.dtype)
    return (blocks[:, :, None, :] * eye[:, None, :, None]).reshape(g * r, g * c)


def _diag_blocks(mat, g):
    r, c = mat.shape[0] // g, mat.shape[1] // g
    eye = jnp.eye(g, dtype=mat.dtype)
    return (mat.reshape(g, r, g, c) * eye[:, None, :, None]).sum(axis=2)


def _halves(gfull, shards):
    rows, cols = gfull.shape
    return gfull.reshape(shards, 2, rows // shards // 2, cols).transpose(1, 0, 2, 3)


def _step(inp):
    x = inp['x'][0]
    target = inp['loss_target'][0]
    rows, d = x.shape
    depth = inp['w_in'].shape[0]
    mix_w = d // 4
    n_state = S5_GROUPS * S5_STATE
    ffn_half = inp['ffn_w_up'].shape[2]
    tm = min(512, rows)
    tc = min(256, rows)
    tl = min(512, rows)
    xy = ("x", "y")

    my_chip = (2 * lax.axis_index("x") + lax.axis_index("y")).astype(jnp.int32).reshape(1)
    zones = {}
    for l in range(depth):
        for nme in BIG:
            if nme == 'ffn_w_up':
                zones[(nme, l)] = place_tile(f"place_{nme}{l}", inp[nme][l].astype(BF16)[None], 0, my_chip)
            else:
                zones[(nme, l)] = place_tile(f"place_{nme}{l}", inp[nme], l, my_chip)
    small_keys = [(nme, None) for nme in SMALL_SHARDED]
    zones.update(zip(small_keys, place_own("place_small", [inp[nme] for nme in SMALL_SHARDED], xy, False)))
    group_keys = []
    for l in range(depth):
        group_keys += [[('w_in', l)] + (small_keys if l == 0 else []),
                       [('w_out', l), ('s5_w_glu', l), ('cv_w_pw', l)], [('ffn_w_up', l)], [('ffn_w_down', l)]]
    gather_groups, gather_token = exchange_start("gather_start", [[zones[key] for key in grp] for grp in group_keys], xy, False)

    def gathered(gi, after):
        return dict(zip(group_keys[gi], exchange_wait(f"gather_wait{gi}", gather_groups[gi], after, xy, False)[1]))

    def full_small(g):
        return g.transpose(1, 2, 0, 3).reshape(g.shape[1], g.shape[2], 4 * g.shape[3])

    gsum = jnp.repeat(jnp.eye(128, dtype=F32)[:S5_GROUPS], S5_STATE, axis=0)

    saved = []
    grads = {nme: [None] * depth for nme in WEIGHTS}
    xcur = x
    for l in range(depth):
        vec = lambda a: a[l].reshape(1, -1)
        gain = vec(inp['norm_mix_g']) + (gather_token[0, 0] if l == 0 else 0.0)
        h = rms_fwd(f"rms_mix{l}", xcur, gain, tm)
        got = gathered(4 * l, h)
        w_in = got[('w_in', l)]
        if l == 0:
            cv_w_dw, lru_w_conv, ffn_w_dw = (full_small(got[(nme, None)]) for nme in ('cv_w_dw', 'lru_w_conv', 'ffn_w_dw'))
        ncol = w_in.shape[2]

        lam_re, lam_im = vec(inp['s5_lam_re']), vec(inp['s5_lam_im'])
        log_step = jnp.broadcast_to(inp['s5_log_step'][l][:, None], (S5_GROUPS, S5_STATE)).reshape(1, n_state)
        b_re = _block_diag(inp['s5_b_re'][l].transpose(0, 2, 1))
        b_im = _block_diag(inp['s5_b_im'][l].transpose(0, 2, 1))
        c_cat = jnp.stack([_block_diag(inp['s5_c_re'][l].transpose(0, 2, 1)),
                           -_block_diag(inp['s5_c_im'][l].transpose(0, 2, 1))]).astype(BF16)
        a_bar, b_bar = s5_param_fwd(f"s5_param_fwd{l}", lam_re, lam_im, log_step, b_re, b_im)
        w_r = _block_diag(inp['lru_w_r'][l]).astype(BF16)
        w_i = _block_diag(inp['lru_w_i'][l]).astype(BF16)
        pool_bd = _block_diag(inp['pool_w'][l]).astype(BF16)
        gate_pars = [w_r, w_i, vec(inp['lru_b_r']), vec(inp['lru_b_i']), vec(inp['lru_lam'])]

        proj = _mm(f"proj{l}", h, w_in, jax.ShapeDtypeStruct((rows, 4 * ncol), F32), (4, rows // tm),
                   pl.BlockSpec((tm, d), lambda j, i: (i, 0)), pl.BlockSpec((None, d, ncol), lambda j, i: (j, 0, 0)),
                   pl.BlockSpec((tm, ncol), lambda j, i: (i, j)), NN)
        proj3 = proj.reshape(1, rows, 4 * ncol)
        nh = n_state // 2
        ts = min(2048, rows)
        cw, sw = mix_w // 4, n_state // 4
        bu = _mm(f"s5_bu{l}", proj, b_bar, jax.ShapeDtypeStruct((2, rows, n_state), F32), (rows // ts, 2, 4),
                 pl.BlockSpec((ts, cw), lambda i, c, s: (i, s)), pl.BlockSpec((None, cw, sw), lambda i, c, s: (c, s, s)),
                 pl.BlockSpec((None, ts, sw), lambda i, c, s: (c, i, s)), NN)
        z = s5_scan_fwd(f"s5_scan{l}", bu, a_bar)
        y_ssm = _mm(f"s5_read{l}", z, c_cat, jax.ShapeDtypeStruct((rows, mix_w), F32), (rows // ts, 4, 2),
                    pl.BlockSpec((None, ts, sw), lambda i, s, c: (c, i, s)), pl.BlockSpec((None, sw, cw), lambda i, s, c: (c, s, s)),
                    pl.BlockSpec((ts, cw), lambda i, s, c: (i, s)), NN, k_axis=2)
        (h0,) = _rowwise(f"cv_glu{l}", _glu, [(proj, 1, mix_w), (proj, 2, mix_w)], [], 1, [(mix_w, F32)], tm)
        h1 = dwconv_fwd(f"cv_conv{l}", h0.reshape(1, rows, mix_w), 0, mix_w, cv_w_dw[l][None], vec(inp['cv_b_dw'])[None],
                        CV_TAPS, tc)[0]
        xc = dwconv_fwd(f"lru_conv{l}", proj3, 3, mix_w, lru_w_conv[l][None], vec(inp['lru_b_conv'])[None], LRU_TAPS, tc)[0]
        a_t, b_t = _rowwise(f"lru_gate{l}", _lru_gate, [(xc, 0, mix_w)], gate_pars, 2, [(mix_w, F32), (mix_w, F32)], tm)
        hseq = lru_scan_fwd(f"lru_scan{l}", a_t, b_t)
        dgp = pool_fwd(f"pool{l}", proj, 5, tc)
        got = gathered(4 * l + 1, proj)
        w_out = got[('w_out', l)].reshape(d, d)
        w_glu, w_pw = got[('s5_w_glu', l)].reshape(mix_w, mix_w), got[('cv_w_pw', l)].reshape(mix_w, mix_w)
        post_pars = [vec(inp['s5_d']), w_glu, vec(inp['s5_b_glu']), vec(inp['cv_ln_g']), vec(inp['cv_ln_b']), w_pw,
                     vec(inp['cv_b_pw']), pool_bd, vec(inp['pool_scale'])]
        post_rows = [(y_ssm, 0, mix_w), (proj, 0, mix_w), (h1, 0, mix_w), (hseq, 0, mix_w), (proj, 4, mix_w), (dgp, 0, mix_w)]
        (mixed,) = _rowwise(f"mix_post{l}", _mix_post, post_rows, post_pars, 1, [(d, BF16)], tm)
        x1 = _mm(f"out_proj{l}", mixed, w_out, jax.ShapeDtypeStruct((rows, d), F32), (2, rows // tm),
                 pl.BlockSpec((tm, d), lambda j, i: (i, 0)), pl.BlockSpec((d, d // 2), lambda j, i: (0, j)),
                 pl.BlockSpec((tm, d // 2), lambda j, i: (i, j)), NN,
                 add=xcur, add_spec=pl.BlockSpec((tm, d // 2), lambda j, i: (i, j)))

        h2 = rms_fwd(f"rms_ffn{l}", x1, vec(inp['norm_ffn_g']), tm)
        tu = min(256, rows)
        w_up = gathered(4 * l + 2, x1)[('ffn_w_up', l)]
        up = _mm(f"ffn_up{l}", h2, w_up, jax.ShapeDtypeStruct((4, rows, ffn_half), F32), (4, rows // tu),
                 pl.BlockSpec((tu, d), lambda k, i: (i, 0)), pl.BlockSpec((None, d, ffn_half), lambda k, i: (k, 0, 0)),
                 pl.BlockSpec((None, tu, ffn_half), lambda k, i: (k, i, 0)), NN)
        w_dw = ffn_w_dw[l].reshape(FFN_TAPS, 2, ffn_half).transpose(1, 0, 2)
        b_dw = inp['ffn_b_dw'][l].reshape(2, 1, ffn_half)
        act = ffn_gate_fwd(f"ffn_gate{l}", up, w_dw, b_dw, tc)
        w_down = gathered(4 * l + 3, up)[('ffn_w_down', l)].reshape(2, ffn_half, d)
        x2 = _mm(f"ffn_down{l}", act, w_down, jax.ShapeDtypeStruct((rows, d), F32), (rows // tm, 4),
                 pl.BlockSpec((2, tm, ffn_half), lambda i, j: (0, i, 0)), pl.BlockSpec((2, ffn_half, d // 4), lambda i, j: (0, 0, j)),
                 pl.BlockSpec((tm, d // 4), lambda i, j: (i, j)), NN, inner=("lead", 2),
                 add=x1, add_spec=pl.BlockSpec((tm, d // 4), lambda i, j: (i, j)))
        saved.append(dict(x=xcur, h=h, proj=proj, z=z, y_ssm=y_ssm, h0=h0, h1=h1, xc=xc, a_t=a_t, hseq=hseq, dgp=dgp,
                          mixed=mixed, x1=x1, h2=h2, up=up, act=act, w_in=w_in, w_out=w_out, w_up=w_up, w_down=w_down,
                          a_bar=a_bar, b_bar=b_bar, c_cat=c_cat, post_pars=post_pars, gate_pars=gate_pars, w_dw=w_dw, b_dw=b_dw,
                          s5=(lam_re, lam_im, log_step, b_re, b_im), cv_w=cv_w_dw[l][None], lru_w=lru_w_conv[l][None]))
        xcur = x2

    loss_row, dx, dg_final = final_loss("final_loss", xcur, inp['norm_final_g'].reshape(1, d), target, tm)
    grads['norm_final_g'] = dg_final.reshape(d)

    big_g = {nme: [None] * depth for nme in BIG}
    reduce_groups = []

    def start_reduce(tag, keys):
        pieces = []
        for nme, lyr in keys:
            g = big_g[nme][lyr]
            if nme == 'ffn_w_down':
                g = g.reshape(2, 4, ffn_half // 2, d // 2)
            pieces.append(reduce_cores(f"reduce_cores_{nme}{lyr}", g.reshape(2, -1, g.shape[-1])).reshape(g.shape[1:]))
        landing = [lax.empty((3,) + p.shape[1:], p.dtype) for p in pieces]
        groups, token = exchange_start(f"reduce_start_{tag}", [list(zip(pieces, landing))], xy, True)
        reduce_groups.append((tag, keys, groups[0]))
        return token

    for l in reversed(range(depth)):
        s = saved[l]
        ncol = s['w_in'].shape[2]
        nh = n_state // 2
        tu = min(256, rows)
        dact = _mm(f"d_act{l}", dx, s['w_down'], jax.ShapeDtypeStruct((2, rows, ffn_half), F32), (2, rows // tu),
                   pl.BlockSpec((tu, d), lambda k, i: (i, 0)), pl.BlockSpec((None, ffn_half, d), lambda k, i: (k, 0, 0)),
                   pl.BlockSpec((None, tu, ffn_half), lambda k, i: (k, i, 0)), NT)
        tn = d // 4
        tk = min(1024, rows)
        tkb = min(2048, rows)
        big_g['ffn_w_down'][l] = _mm(
            f"dw_down{l}", s['act'], dx, jax.ShapeDtypeStruct((2, 2, ffn_half, d // 2), F32), (2, 4, rows // tkb),
            pl.BlockSpec((None, tkb, ffn_half), lambda hh, n, k: (hh, k, 0)), pl.BlockSpec((tkb, tn), lambda hh, n, k: (k, n)),
            pl.BlockSpec((None, None, ffn_half, tn), lambda hh, n, k: (n // 2, hh, 0, n % 2)), TN, k_axis=2)
        dup, dw_dw, db_dw = ffn_gate_bwd(f"ffn_gate_bwd{l}", s['up'], dact, s['w_dw'], s['b_dw'], tc)
        grads['ffn_w_dw'][l] = dw_dw.transpose(1, 0, 2).reshape(FFN_TAPS, 2 * ffn_half)
        grads['ffn_b_dw'][l] = db_dw.reshape(2 * ffn_half)
        dup = dup.reshape(4, rows, ffn_half)
        tm2 = min(1024, rows)
        dh2 = _mm(f"d_h2{l}", dup, s['w_up'], jax.ShapeDtypeStruct((rows, d), F32), (rows // tm2, 2, 4),
                  pl.BlockSpec((None, tm2, ffn_half), lambda i, j, k: (k, i, 0)), pl.BlockSpec((None, d // 2, ffn_half), lambda i, j, k: (k, j, 0)),
                  pl.BlockSpec((tm2, d // 2), lambda i, j, k: (i, j)), NT, k_axis=2)
        tmm = d // 4
        big_g['ffn_w_up'][l] = _mm(
            f"dw_up{l}", dup, s['h2'], jax.ShapeDtypeStruct((2, 4, ffn_half, d // 2), F32), (4, 4, rows // tkb),
            pl.BlockSpec((None, tkb, ffn_half), lambda k4, n, k: (k4, k, 0)), pl.BlockSpec((tkb, tn), lambda k4, n, k: (k, n)),
            pl.BlockSpec((None, None, ffn_half, tn), lambda k4, n, k: (n // 2, k4, 0, n % 2)), TN, k_axis=2)
        token = start_reduce(f"ffn{l}", [('ffn_w_down', l), ('ffn_w_up', l)])
        dx1, dg = rms_bwd(f"rms_ffn_bwd{l}", s['x1'], inp['norm_ffn_g'][l].reshape(1, d) + token[0, 0], dh2, dx, tm)
        grads['norm_ffn_g'][l] = dg.reshape(d)
        dmixed = _mm(f"d_mixed{l}", dx1, s['w_out'], jax.ShapeDtypeStruct((rows, d), F32), (rows // tm, 4),
                     pl.BlockSpec((tm, d), lambda i, j: (i, 0)), pl.BlockSpec((d // 4, d), lambda i, j: (j, 0)),
                     pl.BlockSpec((tm, d // 4), lambda i, j: (i, j)), NT)
        tq = mix_w // 2
        big_g['w_out'][l] = _mm(
            f"dw_out{l}", s['mixed'], dx1, jax.ShapeDtypeStruct((2, 4, tq, d), F32), (8, rows // tk),
            pl.BlockSpec((tk, tq), lambda t, k: (k, t)), pl.BlockSpec((tk, d), lambda t, k: (k, 0)),
            pl.BlockSpec((None, None, tq, d), lambda t, k: (t % 2, t // 2, 0, 0)), TN, k_axis=1)
        post_rows = [(s['y_ssm'], 0, mix_w), (s['proj'], 0, mix_w), (s['h1'], 0, mix_w), (s['hseq'], 0, mix_w),
                     (s['proj'], 4, mix_w), (s['dgp'], 0, mix_w), (dmixed, 0, d)]
        res = _rowwise(f"mix_post_bwd{l}", _mix_post, post_rows, s['post_pars'], 1, [(mix_w, F32)] * 6, tm, with_grads=True)
        dy_ssm, du_dir, dh1, dhseq, dlru_g, ddgp = res[:6]
        dd, dwglu, dbglu, dlng, dlnb, dwpw, dbpw, dpoolbd, dscale = res[6:]
        grads['s5_d'][l], grads['s5_b_glu'][l] = dd.reshape(mix_w), dbglu.reshape(mix_w)
        grads['cv_ln_g'][l], grads['cv_ln_b'][l], grads['cv_b_pw'][l] = dlng.reshape(mix_w), dlnb.reshape(mix_w), dbpw.reshape(mix_w)
        grads['pool_w'][l] = _diag_blocks(dpoolbd, len(POOL_WINDOWS))
        grads['pool_scale'][l] = dscale.reshape(mix_w)
        big_g['s5_w_glu'][l] = _halves(dwglu, 4)
        big_g['cv_w_pw'][l] = _halves(dwpw, 4)
        ts = min(2048, rows)
        cw, sw = mix_w // 4, n_state // 4
        slab = jnp.arange(mix_w)[:, None] // cw == jnp.arange(n_state)[None, :] // sw
        dz = _mm(f"s5_dz{l}", dy_ssm, s['c_cat'], jax.ShapeDtypeStruct((2, rows, n_state), F32), (rows // ts, 2, 4),
                 pl.BlockSpec((ts, cw), lambda i, c, q: (i, q)), pl.BlockSpec((None, sw, cw), lambda i, c, q: (c, q, q)),
                 pl.BlockSpec((None, ts, sw), lambda i, c, q: (c, i, q)), NT)
        dccat = _mm(f"s5_dc{l}", s['z'], dy_ssm, jax.ShapeDtypeStruct((2, n_state, mix_w), F32), (2, 4, rows // tk),
                    pl.BlockSpec((None, tk, sw), lambda c, q, k: (c, k, q)), pl.BlockSpec((tk, cw), lambda c, q, k: (k, q)),
                    pl.BlockSpec((None, sw, cw), lambda c, q, k: (c, q, q)), TN, k_axis=2)
        dccat = jnp.where(slab.T, dccat, 0.0)
        grads['s5_c_re'][l] = _diag_blocks(dccat[0], S5_GROUPS).transpose(0, 2, 1)
        grads['s5_c_im'][l] = -_diag_blocks(dccat[1], S5_GROUPS).transpose(0, 2, 1)
        lam, da_bar = s5_scan_bwd(f"s5_scan_bwd{l}", dz, s['z'], s['a_bar'])
        du = _mm(f"s5_du{l}", lam, s['b_bar'], jax.ShapeDtypeStruct((rows, mix_w), F32), (rows // ts, 4, 2),
                 pl.BlockSpec((None, ts, sw), lambda i, q, c: (c, i, q)), pl.BlockSpec((None, cw, sw), lambda i, q, c: (c, q, q)),
                 pl.BlockSpec((ts, cw), lambda i, q, c: (i, q)), NT, k_axis=2,
                 add=du_dir, add_spec=pl.BlockSpec((ts, cw), lambda i, q, c: (i, q)))
        dbbar = _mm(f"s5_db{l}", s['proj'], lam, jax.ShapeDtypeStruct((2, mix_w, n_state), F32), (2, 4, rows // tk),
                    pl.BlockSpec((tk, cw), lambda c, q, k: (k, q)), pl.BlockSpec((None, tk, sw), lambda c, q, k: (c, k, q)),
                    pl.BlockSpec((None, cw, sw), lambda c, q, k: (c, q, q)), TN, k_axis=2)
        dbbar = jnp.where(slab, dbbar, 0.0)
        dlr, dli, dls, dbre, dbim = s5_param_bwd(f"s5_param_bwd{l}", *s['s5'], da_bar, dbbar, gsum)
        grads['s5_lam_re'][l] = dlr.reshape(S5_GROUPS, S5_STATE)
        grads['s5_lam_im'][l] = dli.reshape(S5_GROUPS, S5_STATE)
        grads['s5_log_step'][l] = dls[0, :S5_GROUPS]
        grads['s5_b_re'][l] = _diag_blocks(dbre, S5_GROUPS).transpose(0, 2, 1)
        grads['s5_b_im'][l] = _diag_blocks(dbim, S5_GROUPS).transpose(0, 2, 1)
        dh0, dw_cv, db_cv = dwconv_bwd(f"cv_conv_bwd{l}", dh1.reshape(1, rows, mix_w), s['h0'].reshape(1, rows, mix_w), 0, mix_w,
                                       s['cv_w'], CV_TAPS, tc)
        grads['cv_w_dw'][l], grads['cv_b_dw'][l] = dw_cv[0], db_cv.reshape(mix_w)
        dv, dgg = _rowwise(f"cv_glu_bwd{l}", _glu, [(s['proj'], 1, mix_w), (s['proj'], 2, mix_w), (dh0[0], 0, mix_w)], [], 1,
                           [(mix_w, F32)] * 2, tm, with_grads=True)
        da_t, db_t = lru_scan_bwd(f"lru_scan_bwd{l}", dhseq, s['a_t'], s['hseq'])
        res = _rowwise(f"lru_gate_bwd{l}", _lru_gate, [(s['xc'], 0, mix_w), (da_t, 0, mix_w), (db_t, 0, mix_w)], s['gate_pars'], 2,
                       [(mix_w, F32)], tm, with_grads=True)
        dxc, dwr, dwi, dbr, dbi, dlam = res
        grads['lru_w_r'][l], grads['lru_w_i'][l] = _diag_blocks(dwr, LRU_HEADS), _diag_blocks(dwi, LRU_HEADS)
        grads['lru_b_r'][l], grads['lru_b_i'][l], grads['lru_lam'][l] = dbr.reshape(mix_w), dbi.reshape(mix_w), dlam.reshape(mix_w)
        dlx, dw_lc, db_lc = dwconv_bwd(f"lru_conv_bwd{l}", dxc.reshape(1, rows, mix_w), s['proj'].reshape(1, rows, 4 * ncol), 3, mix_w,
                                       s['lru_w'], LRU_TAPS, tc)
        grads['lru_w_conv'][l], grads['lru_b_conv'][l] = dw_lc[0], db_lc.reshape(mix_w)
        dpx = pool_bwd(f"pool_bwd{l}", ddgp, tc)
        dproj = jnp.concatenate([du, dv, dgg, dlx[0], dlru_g, dpx], axis=-1)
        dh = _mm(f"d_h{l}", dproj, s['w_in'], jax.ShapeDtypeStruct((rows, d), F32), (rows // tm, 4),
                 pl.BlockSpec((tm, 4 * ncol), lambda i, j: (i, 0)), pl.BlockSpec((4, d // 4, ncol), lambda i, j: (0, j, 0)),
                 pl.BlockSpec((tm, d // 4), lambda i, j: (i, j)), NT, inner=("cols", 4))
        tk2 = min(2048, rows)
        big_g['w_in'][l] = _mm(
            f"dw_in{l}", s['h'], dproj, jax.ShapeDtypeStruct((2, 4, d // 2, ncol), F32), (4, 2, rows // tk2),
            pl.BlockSpec((tk2, d // 2), lambda k4, m, k: (k, m)), pl.BlockSpec((tk2, ncol), lambda k4, m, k: (k, k4)),
            pl.BlockSpec((None, None, d // 2, ncol), lambda k4, m, k: (m, k4, 0, 0)), TN, k_axis=2)
        token = start_reduce(f"mix{l}", [('w_out', l), ('s5_w_glu', l), ('cv_w_pw', l), ('w_in', l)])
        dx, dg = rms_bwd(f"rms_mix_bwd{l}", s['x'], inp['norm_mix_g'][l].reshape(1, d) + token[0, 0], dh, dx1, tm)
        grads['norm_mix_g'][l] = dg.reshape(d)

    small = [nme for nme in WEIGHTS if nme not in BIG]
    full_g = {nme: (grads[nme] if nme == 'norm_final_g' else jnp.stack(grads[nme])) for nme in small}
    flat = jnp.concatenate([full_g[nme].reshape(-1) for nme in small])
    packed = jnp.pad(flat, (0, (-flat.shape[0]) % (128 * 64))).reshape(-1, 128)
    chip_sum = pair_sum("small_pair_sum", packed)
    small_zone = place_tile("place_small_grads", chip_sum[None], 0, my_chip, dtype=F32)
    (small_group,), small_token = exchange_start("small_start", [[small_zone]], xy, False)

    t_full = {}
    for tag, keys, group in reduce_groups:
        pieces, parts = exchange_wait(f"reduce_wait_{tag}", group, small_token, xy, True)
        for key, own, got in zip(keys, pieces, parts):
            t_full[key] = sum_and_share(f"share_cores_{key[0]}{key[1]}", own, got, my_chip)

    outs = {}
    tiles = {'w_in': 256, 'w_out': 128, 'ffn_w_up': 128, 'ffn_w_down': 256, 's5_w_glu': 64, 'cv_w_pw': 64}
    for nme in BIG:
        g0, g1 = t_full[(nme, 0)], t_full[(nme, 1)]
        if nme == 'ffn_w_up':
            res = adamw_sharded(f"adamw_{nme}", *(jnp.swapaxes(inp[p + nme], 1, 2) for p in ('', 'm_', 'v_')), g0, g1, True, tiles[nme])
            outs[nme] = tuple(jnp.swapaxes(r, 1, 2) for r in res)
        else:
            outs[nme] = adamw_sharded(f"adamw_{nme}", inp[nme], inp['m_' + nme], inp['v_' + nme], g0, g1, nme == 'ffn_w_down', tiles[nme])

    after_big = sum(outs[nme][1][:1, :1, :1].reshape(1) for nme in BIG)
    (g4,) = exchange_wait("small_wait", small_group, after_big, xy, False)[1]
    gsum_small = sum_lead("sum_small", g4, 64).reshape(-1)
    red, off = {}, 0
    for nme in small:
        g = gsum_small[off:off + full_g[nme].size].reshape(full_g[nme].shape)
        off += full_g[nme].size
        if nme in SMALL_SHARDED:
            width = inp[nme].shape[2]
            g = lax.dynamic_slice_in_dim(g, my_chip[0] * width, width, axis=2)
        red[nme] = g

    def pack(tree):
        f = jnp.concatenate([tree[nme].reshape(-1) for nme in small])
        return jnp.pad(f, (0, (-f.shape[0]) % (128 * 64))).reshape(-1, 128)

    pd, pm, pv = adamw_flat("adamw_small", pack({n_: inp[n_] for n_ in small}), pack(red), pack({n_: inp['m_' + n_] for n_ in small}),
                            pack({n_: inp['v_' + n_] for n_ in small}), 64)
    off = 0
    for nme in small:
        size, shape = inp[nme].size, inp[nme].shape
        outs[nme] = (red[nme],) + tuple(p.reshape(-1)[off:off + size].reshape(shape) for p in (pd, pm, pv))
        off += size

    loss = lax.psum(loss_row[0, 0], ("x", "y", "c"))
    result = [loss, dx[None]]
    for part in range(4):
        result += [outs[nme][part] for nme in WEIGHTS]
    return tuple(result)


def kernel(x, norm_mix_g, w_in, s5_lam_re, s5_lam_im, s5_log_step, s5_b_re, s5_b_im, s5_c_re, s5_c_im, s5_d, s5_w_glu, s5_b_glu, cv_w_dw, cv_b_dw, cv_ln_g, cv_ln_b, cv_w_pw, cv_b_pw, lru_w_conv, lru_b_conv, lru_w_r, lru_b_r, lru_w_i, lru_b_i, lru_lam, pool_w, pool_scale, w_out, norm_ffn_g, ffn_w_up, ffn_w_dw, ffn_b_dw, ffn_w_down, norm_final_g, loss_target, m_norm_mix_g, m_w_in, m_s5_lam_re, m_s5_lam_im, m_s5_log_step, m_s5_b_re, m_s5_b_im, m_s5_c_re, m_s5_c_im, m_s5_d, m_s5_w_glu, m_s5_b_glu, m_cv_w_dw, m_cv_b_dw, m_cv_ln_g, m_cv_ln_b, m_cv_w_pw, m_cv_b_pw, m_lru_w_conv, m_lru_b_conv, m_lru_w_r, m_lru_b_r, m_lru_w_i, m_lru_b_i, m_lru_lam, m_pool_w, m_pool_scale, m_w_out, m_norm_ffn_g, m_ffn_w_up, m_ffn_w_dw, m_ffn_b_dw, m_ffn_w_down, m_norm_final_g, v_norm_mix_g, v_w_in, v_s5_lam_re, v_s5_lam_im, v_s5_log_step, v_s5_b_re, v_s5_b_im, v_s5_c_re, v_s5_c_im, v_s5_d, v_s5_w_glu, v_s5_b_glu, v_cv_w_dw, v_cv_b_dw, v_cv_ln_g, v_cv_ln_b, v_cv_w_pw, v_cv_b_pw, v_lru_w_conv, v_lru_b_conv, v_lru_w_r, v_lru_b_r, v_lru_w_i, v_lru_b_i, v_lru_lam, v_pool_w, v_pool_scale, v_w_out, v_norm_ffn_g, v_ffn_w_up, v_ffn_w_dw, v_ffn_b_dw, v_ffn_w_down, v_norm_final_g):
    inp = dict(locals())
    return _step(inp)
```

```python
import functools

import jax
import jax.numpy as jnp
from jax import lax
from jax.experimental import pallas as pl
from jax.experimental.pallas import tpu as pltpu

F32 = jnp.float32
BF16 = jnp.bfloat16

VMEM_LIMIT_BYTES = 56 * 1024 * 1024
SUBLANES = 8

EPS = 1e-6
S5_GROUPS, S5_STATE, S5_GROUP_CH = 32, 64, 16
LRU_HEADS, LRU_C = 8, 8.0
POOL_WINDOWS = (2, 4, 8, 16)
CV_TAPS, LRU_TAPS, FFN_TAPS = 31, 4, 3
SCAN_CHUNK = 64
GELU_K0, GELU_K1 = 0.7978845608028654, 0.044715

ADAM_LR, ADAM_B1, ADAM_B2, ADAM_EPS, ADAM_WD, ADAM_STEP = 0.001, 0.9, 0.999, 1e-08, 0.01, 10

NN = ((1,), (0,))
NT = ((1,), (1,))
TN = ((0,), (0,))

WEIGHTS = ['norm_mix_g', 'w_in', 's5_lam_re', 's5_lam_im', 's5_log_step', 's5_b_re', 's5_b_im', 's5_c_re', 's5_c_im',
           's5_d', 's5_w_glu', 's5_b_glu', 'cv_w_dw', 'cv_b_dw', 'cv_ln_g', 'cv_ln_b', 'cv_w_pw', 'cv_b_pw',
           'lru_w_conv', 'lru_b_conv', 'lru_w_r', 'lru_b_r', 'lru_w_i', 'lru_b_i', 'lru_lam', 'pool_w', 'pool_scale',
           'w_out', 'norm_ffn_g', 'ffn_w_up', 'ffn_w_dw', 'ffn_b_dw', 'ffn_w_down', 'norm_final_g']
BIG = ('w_in', 'w_out', 'ffn_w_up', 'ffn_w_down', 's5_w_glu', 'cv_w_pw')
SMALL_SHARDED = {'cv_w_dw': 2, 'lru_w_conv': 2, 'ffn_w_dw': 2}


def _params(sem=None):
    if sem is None:
        return pltpu.CompilerParams(vmem_limit_bytes=VMEM_LIMIT_BYTES)
    return pltpu.CompilerParams(dimension_semantics=sem, vmem_limit_bytes=VMEM_LIMIT_BYTES)


def _row_tile(rows, cap, mult=SUBLANES):
    best = mult
    for t in range(mult, min(rows, cap) + 1, mult):
        if rows % t == 0:
            best = t
    return best


def _bdot(a, b, dims=NN):
    return lax.dot_general(a.astype(BF16), b.astype(BF16), (dims, ((), ())), preferred_element_type=F32)


@jax.custom_vjp
def bdot(a, b):
    return _bdot(a, b)


def _bdot_fwd(a, b):
    return _bdot(a, b), (a, b)


def _bdot_bwd(res, g):
    a, b = res
    return _bdot(g, b, NT).astype(a.dtype), _bdot(a, g, TN).astype(b.dtype)


bdot.defvjp(_bdot_fwd, _bdot_bwd)


def _mm(name, a, b, out_sds, grid, a_spec, b_spec, o_spec, dims, k_axis=None, add=None, add_spec=None, inner=None):
    nk = grid[k_axis] if k_axis is not None else 1
    has_add = add is not None
    acc_shape = tuple(d for d in o_spec.block_shape if d is not None)
    acc_in_out = out_sds.dtype == F32

    def product(a_ref, b_ref):
        if inner is None:
            return _bdot(a_ref[...], b_ref[...], dims)
        kind, n = inner
        width = a_ref.shape[-1] // n
        acc = None
        for j in range(n):
            a_j = a_ref[j] if kind == "lead" else a_ref[:, j * width:(j + 1) * width]
            p = _bdot(a_j, b_ref[j], dims)
            acc = p if acc is None else acc + p
        return acc

    def body(*refs):
        a_ref, b_ref = refs[0], refs[1]
        add_ref = refs[2] if has_add else None
        o_ref = refs[3] if has_add else refs[2]
        prod = product(a_ref, b_ref).reshape(acc_shape)
        if k_axis is None:
            if has_add:
                prod = prod + add_ref[...]
            o_ref[...] = prod.astype(o_ref.dtype)
        else:
            acc_ref = o_ref if acc_in_out else refs[-1]
            k = pl.program_id(k_axis)

            @pl.when(k == 0)
            def _():
                acc_ref[...] = prod

            @pl.when(k > 0)
            def _():
                acc_ref[...] += prod

            if has_add or not acc_in_out:
                @pl.when(k == nk - 1)
                def _():
                    r = acc_ref[...]
                    if has_add:
                        r = r + add_ref[...]
                    o_ref[...] = r.astype(o_ref.dtype)

    sem = tuple("arbitrary" if d == k_axis else "parallel" for d in range(len(grid)))
    in_specs = [a_spec, b_spec] + ([add_spec] if has_add else [])
    args = (a, b) + ((add,) if has_add else ())
    scratch = [pltpu.VMEM(acc_shape, F32)] if (k_axis is not None and not acc_in_out) else []
    return pl.pallas_call(body, out_shape=out_sds, grid=grid, in_specs=in_specs, out_specs=o_spec,
                          scratch_shapes=scratch, compiler_params=_params(sem), name=name)(*args)


def _rms(x, g):
    return x * lax.rsqrt(jnp.mean(x * x, axis=-1, keepdims=True) + EPS) * g


def rms_fwd(name, x, g, tm):
    rows, d = x.shape

    def body(x_ref, g_ref, o_ref):
        o_ref[...] = _rms(x_ref[...], g_ref[...]).astype(BF16)

    return pl.pallas_call(
        body, out_shape=jax.ShapeDtypeStruct((rows, d), BF16), grid=(rows // tm,),
        in_specs=[pl.BlockSpec((tm, d), lambda i: (i, 0)), pl.BlockSpec((1, d), lambda i: (0, 0))],
        out_specs=pl.BlockSpec((tm, d), lambda i: (i, 0)), compiler_params=_params(("parallel",)), name=name)(x, g)


def rms_bwd(name, x, g, dh, dres, tm):
    rows, d = x.shape

    def body(x_ref, g_ref, dh_ref, dres_ref, dx_ref, dg_ref):
        _, vjp = jax.vjp(_rms, x_ref[...], g_ref[...])
        dx, dg = vjp(dh_ref[...])
        dx_ref[...] = dx + dres_ref[...]

        @pl.when(pl.program_id(0) == 0)
        def _():
            dg_ref[...] = jnp.zeros_like(dg_ref)

        dg_ref[...] += dg

    row = pl.BlockSpec((tm, d), lambda i: (i, 0))
    vec = pl.BlockSpec((1, d), lambda i: (0, 0))
    return pl.pallas_call(
        body, out_shape=(jax.ShapeDtypeStruct((rows, d), F32), jax.ShapeDtypeStruct((1, d), F32)), grid=(rows // tm,),
        in_specs=[row, vec, row, row], out_specs=(row, vec), compiler_params=_params(("arbitrary",)), name=name)(x, g, dh, dres)


def final_loss(name, x, g, target, tm):
    rows, d = x.shape

    def body(x_ref, g_ref, t_ref, l_ref, dx_ref, dg_ref):
        def f(xv, gv):
            e = _rms(xv, gv) - t_ref[...]
            return 0.5 * jnp.sum(jnp.mean(e * e, axis=-1))

        loss, (dx, dg) = jax.value_and_grad(f, argnums=(0, 1))(x_ref[...], g_ref[...])
        dx_ref[...] = dx

        @pl.when(pl.program_id(0) == 0)
        def _():
            l_ref[...] = jnp.zeros_like(l_ref)
            dg_ref[...] = jnp.zeros_like(dg_ref)

        l_ref[...] += jnp.full(l_ref.shape, loss, F32)
        dg_ref[...] += dg

    row = pl.BlockSpec((tm, d), lambda i: (i, 0))
    vec = pl.BlockSpec((1, d), lambda i: (0, 0))
    lspec = pl.BlockSpec((1, 128), lambda i: (0, 0))
    return pl.pallas_call(
        body, out_shape=(jax.ShapeDtypeStruct((1, 128), F32), jax.ShapeDtypeStruct((rows, d), F32), jax.ShapeDtypeStruct((1, d), F32)),
        grid=(rows // tm,), in_specs=[row, vec, row], out_specs=(lspec, row, vec),
        compiler_params=_params(("arbitrary",)), name=name)(x, g, target)


def _rowwise(name, fn, row_ins, par_ins, n_row_out, row_out_dtypes, tm, with_grads=False):
    rows = row_ins[0][0].shape[0]
    n_prim = len(row_ins) - (n_row_out if with_grads else 0)
    n_par = len(par_ins)

    def body(*refs):
        ins = [r[...] for r in refs[:len(row_ins) + n_par]]
        outs = refs[len(row_ins) + n_par:]
        prim, cts, pars = ins[:n_prim], ins[n_prim:len(row_ins)], ins[len(row_ins):]
        if not with_grads:
            res = fn(*prim, *pars)
            for o_ref, r in zip(outs, res):
                o_ref[...] = r.astype(o_ref.dtype)
            return
        _, vjp = jax.vjp(fn, *prim, *[p.astype(F32) for p in pars])
        grads = vjp(tuple(cts))
        for o_ref, gr in zip(outs[:n_prim], grads[:n_prim]):
            o_ref[...] = gr.astype(o_ref.dtype)

        @pl.when(pl.program_id(0) == 0)
        def _():
            for o_ref in outs[n_prim:]:
                o_ref[...] = jnp.zeros_like(o_ref)

        for o_ref, gr in zip(outs[n_prim:], grads[n_prim:]):
            o_ref[...] += gr.astype(F32)

    in_specs = [pl.BlockSpec((tm, w), (lambda i, c=c: (i, c))) for (_, c, w) in row_ins]
    in_specs += [pl.BlockSpec(p.shape, (lambda i, n=p.ndim: (0,) * n)) for p in par_ins]
    args = [a for (a, _, _) in row_ins] + list(par_ins)
    if not with_grads:
        out_shape = tuple(jax.ShapeDtypeStruct((rows, w), dt) for (w, dt) in row_out_dtypes)
        out_specs = tuple(pl.BlockSpec((tm, w), lambda i: (i, 0)) for (w, _) in row_out_dtypes)
        sem = ("parallel",)
    else:
        out_shape = tuple(jax.ShapeDtypeStruct((rows, w), dt) for (w, dt) in row_out_dtypes)
        out_shape += tuple(jax.ShapeDtypeStruct(p.shape, F32) for p in par_ins)
        out_specs = tuple(pl.BlockSpec((tm, w), lambda i: (i, 0)) for (w, _) in row_out_dtypes)
        out_specs += tuple(pl.BlockSpec(p.shape, (lambda i, n=p.ndim: (0,) * n)) for p in par_ins)
        sem = ("arbitrary",)
    return pl.pallas_call(body, out_shape=out_shape, grid=(rows // tm,), in_specs=in_specs, out_specs=out_specs,
                          compiler_params=_params(sem), name=name)(*args)


def _glu(v, g):
    return (v * jax.nn.sigmoid(g),)


def _neg_expm1(z):
    return -jnp.tanh(0.5 * z) * (jnp.exp(z) + 1.0)


def _lru_gate(xc, w_r, w_i, b_r, b_i, lam):
    r = jax.nn.sigmoid(bdot(xc, w_r) + b_r)
    i = jax.nn.sigmoid(bdot(xc, w_i) + b_i)
    log_a = -LRU_C * r * jax.nn.softplus(-lam)
    a = jnp.exp(log_a)
    mult = jnp.sqrt(_neg_expm1(2.0 * log_a))
    return a, mult * (i * xc)


def _layernorm(x, g, b):
    mu = jnp.mean(x, axis=-1, keepdims=True)
    var = jnp.mean(jnp.square(x - mu), axis=-1, keepdims=True)
    return (x - mu) * lax.rsqrt(var + EPS) * g + b


def _mix_post(y_ssm, u, h1, hseq, lru_g, dgp, s5_d, w_glu, b_glu, ln_g, ln_b, w_pw, b_pw, pool_bd, pool_scale):
    y = y_ssm + s5_d * u
    gl = jax.nn.gelu(y, approximate=True)
    out_s5 = gl * jax.nn.sigmoid(bdot(gl, w_glu) + b_glu)
    out_cv = bdot(jax.nn.silu(_layernorm(h1, ln_g, ln_b)), w_pw) + b_pw
    out_lru = hseq * jax.nn.gelu(lru_g, approximate=True)
    out_pool = bdot(dgp, pool_bd) * pool_scale
    return (jnp.concatenate([out_s5, out_cv, out_lru, out_pool], axis=-1),)


def _ffn_act(gc, val):
    return (jax.nn.gelu(gc, approximate=True) * val,)


def ffn_act_fwd(name, gc, up, tm):
    _, rows, c = gc.shape

    def body(g_ref, v_ref, o_ref):
        o_ref[...] = _ffn_act(g_ref[...], v_ref[...])[0].astype(BF16)

    return pl.pallas_call(
        body, out_shape=jax.ShapeDtypeStruct((2, rows, c), BF16), grid=(2, rows // tm),
        in_specs=[pl.BlockSpec((None, tm, c), lambda h, i: (h, i, 0)), pl.BlockSpec((None, tm, c), lambda h, i: (h + 2, i, 0))],
        out_specs=pl.BlockSpec((None, tm, c), lambda h, i: (h, i, 0)),
        compiler_params=_params(("parallel", "parallel")), name=name)(gc, up)


def ffn_act_bwd(name, gc, up, dact, tm):
    _, rows, c = gc.shape

    def body(g_ref, v_ref, d_ref, dg_ref, dv_ref):
        _, vjp = jax.vjp(_ffn_act, g_ref[...], v_ref[...])
        dg, dv = vjp((d_ref[...],))
        dg_ref[...] = dg
        dv_ref[...] = dv.astype(BF16)

    blk = pl.BlockSpec((None, tm, c), lambda h, i: (h, i, 0))
    return pl.pallas_call(
        body, out_shape=(jax.ShapeDtypeStruct((2, rows, c), F32), jax.ShapeDtypeStruct((2, rows, c), BF16)), grid=(2, rows // tm),
        in_specs=[blk, pl.BlockSpec((None, tm, c), lambda h, i: (h + 2, i, 0)), blk], out_specs=(blk, blk),
        compiler_params=_params(("parallel", "parallel")), name=name)(gc, up, dact)


def _halo_rows(taps):
    return -(-(taps - 1) // SUBLANES) * SUBLANES


def _row_windows(ext, offsets, n):
    room = ext.shape[0] - SUBLANES
    variants, out = {}, {}
    for off in offsets:
        r = off % SUBLANES
        if r == 0:
            out[off] = ext[off:off + n]
            continue
        if r not in variants:
            variants[r] = ext[r:r + room]
        out[off] = variants[r][off - r:off - r + n]
    return out


def dwconv_fwd(name, x, cblk, c, w, b, taps, tm, out_dtype=F32):
    nb = w.shape[0]
    rows = x.shape[1]
    halo = _halo_rows(taps)
    per = tm // halo

    def body(x_ref, h_ref, w_ref, b_ref, o_ref):
        i = pl.program_id(1)
        prev = jnp.where(i > 0, h_ref[...], 0.0)
        ext = jnp.concatenate([prev, x_ref[...]], axis=0)
        win = _row_windows(ext, [halo - (taps - 1) + k for k in range(taps)], tm)
        acc = jnp.broadcast_to(b_ref[...], (tm, c))
        for k in range(taps):
            acc = acc + w_ref[k:k + 1, :] * win[halo - (taps - 1) + k]
        o_ref[...] = acc.astype(o_ref.dtype)

    return pl.pallas_call(
        body, out_shape=jax.ShapeDtypeStruct((nb, rows, c), out_dtype), grid=(nb, rows // tm),
        in_specs=[pl.BlockSpec((None, tm, c), lambda n, i: (n, i, cblk)),
                  pl.BlockSpec((None, halo, c), lambda n, i: (n, jnp.maximum(i * per - 1, 0), cblk)),
                  pl.BlockSpec((None, taps, c), lambda n, i: (n, 0, 0)),
                  pl.BlockSpec((None, 1, c), lambda n, i: (n, 0, 0))],
        out_specs=pl.BlockSpec((None, tm, c), lambda n, i: (n, i, 0)),
        compiler_params=_params(("parallel", "parallel")), name=name)(x, x, w, b)


def dwconv_bwd(name, dy, x, cblk, c, w, taps, tm, dx_dtype=F32):
    nb = w.shape[0]
    rows = x.shape[1]
    halo = _halo_rows(taps)
    per = tm // halo
    n_tiles = rows // tm
    last_halo = rows // halo - 1

    def body(dy_ref, dn_ref, x_ref, xp_ref, w_ref, dx_ref, dw_ref, db_ref):
        i = pl.program_id(1)
        dyv = dy_ref[...]
        nxt = jnp.where(i < n_tiles - 1, dn_ref[...], 0.0)
        dext = jnp.concatenate([dyv, nxt], axis=0)
        prev = jnp.where(i > 0, xp_ref[...], 0.0)
        xext = jnp.concatenate([prev, x_ref[...]], axis=0)
        acc = jnp.zeros((tm, c), F32)

        @pl.when(i == 0)
        def _():
            dw_ref[...] = jnp.zeros_like(dw_ref)
            db_ref[...] = jnp.zeros_like(db_ref)

        dwin = _row_windows(dext, list(range(taps)), tm)
        xwin = _row_windows(xext, [halo - (taps - 1) + k for k in range(taps)], tm)
        for k in range(taps):
            acc = acc + w_ref[k:k + 1, :] * dwin[taps - 1 - k]
            dw_ref[k:k + 1, :] += jnp.sum(dyv * xwin[halo - (taps - 1) + k], axis=0, keepdims=True)
        dx_ref[...] = acc.astype(dx_ref.dtype)
        db_ref[...] += jnp.sum(dyv, axis=0, keepdims=True)

    return pl.pallas_call(
        body, out_shape=(jax.ShapeDtypeStruct((nb, rows, c), dx_dtype), jax.ShapeDtypeStruct((nb, taps, c), F32),
                         jax.ShapeDtypeStruct((nb, 1, c), F32)),
        grid=(nb, n_tiles),
        in_specs=[pl.BlockSpec((None, tm, c), lambda n, i: (n, i, 0)),
                  pl.BlockSpec((None, halo, c), lambda n, i: (n, jnp.minimum((i + 1) * per, last_halo), 0)),
                  pl.BlockSpec((None, tm, c), lambda n, i: (n, i, cblk)),
                  pl.BlockSpec((None, halo, c), lambda n, i: (n, jnp.maximum(i * per - 1, 0), cblk)),
                  pl.BlockSpec((None, taps, c), lambda n, i: (n, 0, 0))],
        out_specs=(pl.BlockSpec((None, tm, c), lambda n, i: (n, i, 0)), pl.BlockSpec((None, taps, c), lambda n, i: (n, 0, 0)),
                   pl.BlockSpec((None, 1, c), lambda n, i: (n, 0, 0))),
        compiler_params=_params(("parallel", "arbitrary")), name=name)(dy, dy, x, x, w)


def ffn_gate_fwd(name, up, w, b, tm):
    _, rows, c = up.shape
    halo = _halo_rows(FFN_TAPS)
    per = tm // halo

    def body(g_ref, gp_ref, v_ref, w_ref, b_ref, o_ref):
        i = pl.program_id(1)
        ext = jnp.concatenate([jnp.where(i > 0, gp_ref[...], 0.0), g_ref[...]], axis=0)
        gc = jnp.broadcast_to(b_ref[...], (tm, c))
        for k in range(FFN_TAPS):
            off = halo - (FFN_TAPS - 1) + k
            gc = gc + w_ref[k:k + 1, :] * ext[off:off + tm]
        o_ref[...] = _ffn_act(gc, v_ref[...])[0].astype(BF16)

    return pl.pallas_call(
        body, out_shape=jax.ShapeDtypeStruct((2, rows, c), BF16), grid=(2, rows // tm),
        in_specs=[pl.BlockSpec((None, tm, c), lambda h, i: (h, i, 0)),
                  pl.BlockSpec((None, halo, c), lambda h, i: (h, jnp.maximum(i * per - 1, 0), 0)),
                  pl.BlockSpec((None, tm, c), lambda h, i: (h + 2, i, 0)),
                  pl.BlockSpec((None, FFN_TAPS, c), lambda h, i: (h, 0, 0)), pl.BlockSpec((None, 1, c), lambda h, i: (h, 0, 0))],
        out_specs=pl.BlockSpec((None, tm, c), lambda h, i: (h, i, 0)),
        compiler_params=_params(("parallel", "parallel")), name=name)(up, up, up, w, b)


def ffn_gate_bwd(name, up, dact, w, b, tm):
    _, rows, c = up.shape
    halo = _halo_rows(FFN_TAPS)
    per = tm // halo
    n_tiles = rows // tm
    last_halo = rows // halo - 1
    n_ext = tm + halo

    def body(g_ref, gp_ref, gn_ref, v_ref, vn_ref, d_ref, dn_ref, w_ref, b_ref, dup_ref, dw_ref, db_ref):
        i = pl.program_id(1)
        gext = jnp.concatenate([jnp.where(i > 0, gp_ref[...], 0.0), g_ref[...], gn_ref[...]], axis=0)
        shifted = [gext[halo - (FFN_TAPS - 1) + k:halo - (FFN_TAPS - 1) + k + n_ext] for k in range(FFN_TAPS)]
        gc = jnp.broadcast_to(b_ref[...], (n_ext, c))
        for k in range(FFN_TAPS):
            gc = gc + w_ref[k:k + 1, :] * shifted[k]
        vext = jnp.concatenate([v_ref[...], vn_ref[...]], axis=0)
        dext = jnp.concatenate([d_ref[...], dn_ref[...]], axis=0)
        sq = gc * gc
        t = jnp.tanh(GELU_K0 * gc * (1.0 + GELU_K1 * sq))
        half = 0.5 * (1.0 + t)
        dval = dext * (gc * half)
        dgc = dext * vext * (half + 0.5 * gc * (1.0 - t * t) * (GELU_K0 * (1.0 + 3.0 * GELU_K1 * sq)))
        r = lax.broadcasted_iota(jnp.int32, (n_ext, c), 0)
        dgc = jnp.where((r < tm) | (i < n_tiles - 1), dgc, 0.0)
        dgate = jnp.zeros((tm, c), F32)
        for k in range(FFN_TAPS):
            dgate = dgate + w_ref[k:k + 1, :] * dgc[FFN_TAPS - 1 - k:FFN_TAPS - 1 - k + tm]
        dup_ref[0] = dgate.astype(BF16)
        dup_ref[1] = dval[:tm].astype(BF16)

        @pl.when(i == 0)
        def _():
            dw_ref[...] = jnp.zeros_like(dw_ref)
            db_ref[...] = jnp.zeros_like(db_ref)

        dgc_t = dgc[:tm]
        for k in range(FFN_TAPS):
            dw_ref[k:k + 1, :] += jnp.sum(dgc_t * shifted[k][:tm], axis=0, keepdims=True)
        db_ref[...] += jnp.sum(dgc_t, axis=0, keepdims=True)

    def tile(shift):
        return pl.BlockSpec((None, tm, c), lambda h, i: (h + shift, i, 0))

    def after(shift):
        return pl.BlockSpec((None, halo, c), lambda h, i: (h + shift, jnp.minimum((i + 1) * per, last_halo), 0))

    return pl.pallas_call(
        body, out_shape=(jax.ShapeDtypeStruct((2, 2, rows, c), BF16), jax.ShapeDtypeStruct((2, FFN_TAPS, c), F32),
                         jax.ShapeDtypeStruct((2, 1, c), F32)),
        grid=(2, n_tiles),
        in_specs=[tile(0), pl.BlockSpec((None, halo, c), lambda h, i: (h, jnp.maximum(i * per - 1, 0), 0)), after(0),
                  tile(2), after(2), tile(0), after(0),
                  pl.BlockSpec((None, FFN_TAPS, c), lambda h, i: (h, 0, 0)), pl.BlockSpec((None, 1, c), lambda h, i: (h, 0, 0))],
        out_specs=(pl.BlockSpec((2, None, tm, c), lambda h, i: (0, h, i, 0)), pl.BlockSpec((None, FFN_TAPS, c), lambda h, i: (h, 0, 0)),
                   pl.BlockSpec((None, 1, c), lambda h, i: (h, 0, 0))),
        compiler_params=_params(("parallel", "arbitrary")), name=name)(up, up, up, up, up, dact, dact, w, b)


POOL_HALO = 16


def pool_fwd(name, proj, cblk, tm):
    rows = proj.shape[0]
    c = 128 * len(POOL_WINDOWS)
    per = tm // POOL_HALO

    def body(x_ref, h_ref, o_ref):
        i = pl.program_id(0)
        xv = x_ref[...]
        ext = jnp.concatenate([jnp.where(i > 0, h_ref[...], 0.0), xv], axis=0)
        t1 = (lax.broadcasted_iota(jnp.int32, (tm, 128), 0) + i * tm + 1).astype(F32)
        outs = []
        for gi, win in enumerate(POOL_WINDOWS):
            seg = ext[:, gi * 128:(gi + 1) * 128]
            s = seg[POOL_HALO:POOL_HALO + tm]
            for j in range(1, win):
                s = s + seg[POOL_HALO - j:POOL_HALO - j + tm]
            outs.append(s / jnp.minimum(t1, float(win)) - xv[:, gi * 128:(gi + 1) * 128])
        o_ref[...] = jnp.concatenate(outs, axis=-1)

    return pl.pallas_call(
        body, out_shape=jax.ShapeDtypeStruct((rows, c), F32), grid=(rows // tm,),
        in_specs=[pl.BlockSpec((tm, c), lambda i: (i, cblk)),
                  pl.BlockSpec((POOL_HALO, c), lambda i: (jnp.maximum(i * per - 1, 0), cblk))],
        out_specs=pl.BlockSpec((tm, c), lambda i: (i, 0)), compiler_params=_params(("parallel",)), name=name)(proj, proj)


def pool_bwd(name, dd, tm):
    rows, c = dd.shape
    per = tm // POOL_HALO
    n_tiles = rows // tm
    last_halo = rows // POOL_HALO - 1

    def body(d_ref, n_ref, o_ref):
        i = pl.program_id(0)
        dv = d_ref[...]
        nxt = jnp.where(i < n_tiles - 1, n_ref[...], 0.0)
        t1 = (lax.broadcasted_iota(jnp.int32, (tm, 128), 0) + i * tm + 1).astype(F32)
        t1n = (lax.broadcasted_iota(jnp.int32, (POOL_HALO, 128), 0) + (i + 1) * tm + 1).astype(F32)
        outs = []
        for gi, win in enumerate(POOL_WINDOWS):
            sl = slice(gi * 128, (gi + 1) * 128)
            q = jnp.concatenate([dv[:, sl] / jnp.minimum(t1, float(win)), nxt[:, sl] / jnp.minimum(t1n, float(win))], axis=0)
            s = q[0:tm]
            for j in range(1, win):
                s = s + q[j:j + tm]
            outs.append(s - dv[:, sl])
        o_ref[...] = jnp.concatenate(outs, axis=-1)

    return pl.pallas_call(
        body, out_shape=jax.ShapeDtypeStruct((rows, c), F32), grid=(n_tiles,),
        in_specs=[pl.BlockSpec((tm, c), lambda i: (i, 0)),
                  pl.BlockSpec((POOL_HALO, c), lambda i: (jnp.minimum((i + 1) * per, last_halo), 0))],
        out_specs=pl.BlockSpec((tm, c), lambda i: (i, 0)), compiler_params=_params(("parallel",)), name=name)(dd, dd)


def _shift_down(v, s, fill):
    r = lax.broadcasted_iota(jnp.int32, v.shape, 0)
    return jnp.where(r >= s, pltpu.roll(v, s, 0), fill)


def _shift_up(v, s, fill):
    n = v.shape[0]
    r = lax.broadcasted_iota(jnp.int32, v.shape, 0)
    return jnp.where(r < n - s, pltpu.roll(v, n - s, 0), fill)


def _cscan_chunk(vr, vi, powers, reverse):
    for k, (qr, qi) in enumerate(powers):
        s = 1 << k
        if reverse:
            sr, si = _shift_up(vr, s, 0.0), _shift_up(vi, s, 0.0)
            vr, vi = vr + qr * sr + qi * si, vi + qr * si - qi * sr
        else:
            sr, si = _shift_down(vr, s, 0.0), _shift_down(vi, s, 0.0)
            vr, vi = vr + qr * sr - qi * si, vi + qr * si + qi * sr
    return vr, vi


def _powers(pr, pi, n):
    out = [(pr, pi)]
    for _ in range(n - 1):
        pr, pi = pr * pr - pi * pi, 2.0 * pr * pi
        out.append((pr, pi))
    return out


def s5_scan_fwd(name, bu, a):
    _, rows, n = bu.shape
    t = min(SCAN_CHUNK, rows)
    steps = t.bit_length() - 1

    def body(bu_ref, a_ref, z_ref):
        pr, pi = a_ref[0], a_ref[1]
        powers = _powers(pr, pi, steps)
        r = lax.broadcasted_iota(jnp.int32, (t, 128), 0)
        tr, ti = _cscan_chunk(jnp.where(r == 0, pr, 0.0), jnp.where(r == 0, pi, 0.0), powers, False)

        def chunk(ci, carry):
            base = pl.multiple_of(ci * t, t)
            vr, vi = _cscan_chunk(bu_ref[0, pl.ds(base, t), :], bu_ref[1, pl.ds(base, t), :], powers, False)
            cr, cim = carry
            zr = vr + tr * cr - ti * cim
            zi = vi + tr * cim + ti * cr
            z_ref[0, pl.ds(base, t), :] = zr
            z_ref[1, pl.ds(base, t), :] = zi
            return zr[t - 1:t, :], zi[t - 1:t, :]

        zero = jnp.zeros((1, 128), F32)
        lax.fori_loop(0, rows // t, chunk, (zero, zero))

    return pl.pallas_call(
        body, out_shape=jax.ShapeDtypeStruct((2, rows, n), F32), grid=(n // 128,),
        in_specs=[pl.BlockSpec((2, rows, 128), lambda j: (0, 0, j)), pl.BlockSpec((2, 1, 128), lambda j: (0, 0, j))],
        out_specs=pl.BlockSpec((2, rows, 128), lambda j: (0, 0, j)), compiler_params=_params(("parallel",)), name=name)(bu, a)


def s5_scan_bwd(name, dz, z, a):
    _, rows, n = dz.shape
    t = min(SCAN_CHUNK, rows)
    steps = t.bit_length() - 1
    n_chunks = rows // t

    def body(dz_ref, z_ref, a_ref, lam_ref, da_ref):
        pr, pi = a_ref[0], a_ref[1]
        powers = _powers(pr, pi, steps)
        r = lax.broadcasted_iota(jnp.int32, (t, 128), 0)
        tr, ti = _cscan_chunk(jnp.where(r == t - 1, pr, 0.0), jnp.where(r == t - 1, -pi, 0.0), powers, True)

        def chunk(k, carry):
            ci = n_chunks - 1 - k
            base = pl.multiple_of(ci * t, t)
            vr, vi = _cscan_chunk(dz_ref[0, pl.ds(base, t), :], dz_ref[1, pl.ds(base, t), :], powers, True)
            cr, cim, dar, dai = carry
            lr = vr + tr * cr - ti * cim
            li = vi + tr * cim + ti * cr
            lam_ref[0, pl.ds(base, t), :] = lr
            lam_ref[1, pl.ds(base, t), :] = li
            pbase = pl.multiple_of(jnp.maximum(base - SUBLANES, 0), SUBLANES)
            keep = (ci > 0).astype(F32)
            pzr = z_ref[0, pl.ds(pbase, SUBLANES), :][SUBLANES - 1:SUBLANES, :] * keep
            pzi = z_ref[1, pl.ds(pbase, SUBLANES), :][SUBLANES - 1:SUBLANES, :] * keep
            zpr = _shift_down(z_ref[0, pl.ds(base, t), :], 1, pzr)
            zpi = _shift_down(z_ref[1, pl.ds(base, t), :], 1, pzi)
            dar = dar + jnp.sum(lr * zpr + li * zpi, axis=0, keepdims=True)
            dai = dai + jnp.sum(li * zpr - lr * zpi, axis=0, keepdims=True)
            return lr[0:1, :], li[0:1, :], dar, dai

        zero = jnp.zeros((1, 128), F32)
        _, _, dar, dai = lax.fori_loop(0, n_chunks, chunk, (zero, zero, zero, zero))
        da_ref[0] = dar
        da_ref[1] = dai

    seq = pl.BlockSpec((2, rows, 128), lambda j: (0, 0, j))
    vec = pl.BlockSpec((2, 1, 128), lambda j: (0, 0, j))
    return pl.pallas_call(
        body, out_shape=(jax.ShapeDtypeStruct((2, rows, n), F32), jax.ShapeDtypeStruct((2, 1, n), F32)), grid=(n // 128,),
        in_specs=[seq, seq, vec], out_specs=(seq, vec), compiler_params=_params(("parallel",)), name=name)(dz, z, a)


def _rscan_chunk(a, b, steps, reverse):
    shift = _shift_up if reverse else _shift_down
    for k in range(steps):
        s = 1 << k
        b = b + a * shift(b, s, 0.0)
        a = a * shift(a, s, 1.0)
    return a, b


def lru_scan_fwd(name, a, b):
    rows, n = a.shape
    t = min(SCAN_CHUNK, rows)
    steps = t.bit_length() - 1

    def body(a_ref, b_ref, h_ref):
        def chunk(ci, carry):
            base = pl.multiple_of(ci * t, t)
            pa, hb = _rscan_chunk(a_ref[pl.ds(base, t), :], b_ref[pl.ds(base, t), :], steps, False)
            h = hb + pa * carry
            h_ref[pl.ds(base, t), :] = h
            return h[t - 1:t, :]

        lax.fori_loop(0, rows // t, chunk, jnp.zeros((1, 128), F32))

    seq = pl.BlockSpec((rows, 128), lambda j: (0, j))
    return pl.pallas_call(body, out_shape=jax.ShapeDtypeStruct((rows, n), F32), grid=(n // 128,), in_specs=[seq, seq],
                          out_specs=seq, compiler_params=_params(("parallel",)), name=name)(a, b)


def lru_scan_bwd(name, dh, a, h):
    rows, n = a.shape
    t = min(SCAN_CHUNK, rows)
    steps = t.bit_length() - 1
    n_chunks = rows // t

    def body(dh_ref, a_ref, h_ref, da_ref, db_ref):
        def chunk(k, carry):
            ci = n_chunks - 1 - k
            base = pl.multiple_of(ci * t, t)
            nbase = pl.multiple_of(jnp.minimum(base + t, rows - SUBLANES), SUBLANES)
            a_next = a_ref[pl.ds(nbase, SUBLANES), :][0:1, :]
            an = _shift_up(a_ref[pl.ds(base, t), :], 1, a_next)
            pa, mb = _rscan_chunk(an, dh_ref[pl.ds(base, t), :], steps, True)
            mu = mb + pa * carry
            pbase = pl.multiple_of(jnp.maximum(base - SUBLANES, 0), SUBLANES)
            hp_row = h_ref[pl.ds(pbase, SUBLANES), :][SUBLANES - 1:SUBLANES, :] * (ci > 0).astype(F32)
            hp = _shift_down(h_ref[pl.ds(base, t), :], 1, hp_row)
            da_ref[pl.ds(base, t), :] = mu * hp
            db_ref[pl.ds(base, t), :] = mu
            return mu[0:1, :]

        lax.fori_loop(0, n_chunks, chunk, jnp.zeros((1, 128), F32))

    seq = pl.BlockSpec((rows, 128), lambda j: (0, j))
    return pl.pallas_call(
        body, out_shape=(jax.ShapeDtypeStruct((rows, n), F32), jax.ShapeDtypeStruct((rows, n), F32)), grid=(n // 128,),
        in_specs=[seq, seq, seq], out_specs=(seq, seq), compiler_params=_params(("parallel",)), name=name)(dh, a, h)


def _s5_param(lr, li, ls, bre, bim):
    st = jnp.exp(ls)
    er = jnp.exp(lr * st)
    th = li * st
    ar, ai = er * jnp.cos(th), er * jnp.sin(th)
    nr, ni = ar - 1.0, ai
    den = lr * lr + li * li
    cr, ci = (nr * lr + ni * li) / den, (ni * lr - nr * li) / den
    return ar, ai, cr * bre - ci * bim, cr * bim + ci * bre


def s5_param_fwd(name, lr, li, ls, bre, bim):
    gh, n = bre.shape

    def body(lr_ref, li_ref, ls_ref, bre_ref, bim_ref, a_ref, bb_ref):
        ar, ai, br, bi = _s5_param(lr_ref[...], li_ref[...], ls_ref[...], bre_ref[...], bim_ref[...])
        a_ref[0] = ar
        a_ref[1] = ai
        bb_ref[0] = br.astype(BF16)
        bb_ref[1] = bi.astype(BF16)

    return pl.pallas_call(body, out_shape=(jax.ShapeDtypeStruct((2, 1, n), F32), jax.ShapeDtypeStruct((2, gh, n), BF16)),
                          compiler_params=_params(), name=name)(lr, li, ls, bre, bim)


def s5_param_bwd(name, lr, li, ls, bre, bim, da, dbb, gsum):
    gh, n = bre.shape

    def body(lr_ref, li_ref, ls_ref, bre_ref, bim_ref, da_ref, dbb_ref, gs_ref, dlr_ref, dli_ref, dls_ref, dbre_ref, dbim_ref):
        _, vjp = jax.vjp(_s5_param, lr_ref[...], li_ref[...], ls_ref[...], bre_ref[...], bim_ref[...])
        dlr, dli, dls, dbre, dbim = vjp((da_ref[0], da_ref[1], dbb_ref[0], dbb_ref[1]))
        dlr_ref[...] = dlr
        dli_ref[...] = dli
        dls_ref[...] = jnp.dot(jnp.broadcast_to(dls, (SUBLANES, n)), gs_ref[...], preferred_element_type=F32,
                               precision=lax.Precision.HIGHEST)
        dbre_ref[...] = dbre
        dbim_ref[...] = dbim

    vec = jax.ShapeDtypeStruct((1, n), F32)
    mat = jax.ShapeDtypeStruct((gh, n), F32)
    return pl.pallas_call(body, out_shape=(vec, vec, jax.ShapeDtypeStruct((SUBLANES, 128), F32), mat, mat),
                          compiler_params=_params(), name=name)(lr, li, ls, bre, bim, da, dbb, gsum)


def sum_lead(name, x, tr):
    n, rows, cols = x.shape

    def body(x_ref, o_ref):
        acc = x_ref[0]
        for j in range(1, n):
            acc = acc + x_ref[j]
        o_ref[...] = acc

    return pl.pallas_call(
        body, out_shape=jax.ShapeDtypeStruct((rows, cols), x.dtype), grid=(rows // tr,),
        in_specs=[pl.BlockSpec((n, tr, cols), lambda i: (0, i, 0))], out_specs=pl.BlockSpec((tr, cols), lambda i: (i, 0)),
        compiler_params=_params(("parallel",)), name=name)(x)


def _adamw(w, g, m, v):
    m = ADAM_B1 * m + (1.0 - ADAM_B1) * g
    v = ADAM_B2 * v + (1.0 - ADAM_B2) * jnp.square(g)
    m_hat = m / (1.0 - ADAM_B1 ** ADAM_STEP)
    v_hat = v / (1.0 - ADAM_B2 ** ADAM_STEP)
    delta = -ADAM_LR * (m_hat / (jnp.sqrt(v_hat) + ADAM_EPS) + ADAM_WD * w)
    return delta, m, v


def adamw_sharded(name, w, m, v, g0, g1, split_cols, tile):
    _, r, c = w.shape
    if split_cols:
        nt = c // tile
        per = (c // 2) // tile
        wspec = pl.BlockSpec((None, r, tile), lambda l, t: (l, 0, t))
        gspec = pl.BlockSpec((None, r, tile), lambda l, t: (t // per, 0, t % per))
    else:
        nt = r // tile
        per = (r // 2) // tile
        wspec = pl.BlockSpec((None, tile, c), lambda l, t: (l, t, 0))
        gspec = pl.BlockSpec((None, tile, c), lambda l, t: (t // per, t % per, 0))

    def body(w_ref, m_ref, v_ref, g0_ref, g1_ref, g_ref, d_ref, nm_ref, nv_ref):
        g = jnp.where(pl.program_id(0) == 0, g0_ref[...], g1_ref[...])
        d, nm, nv = _adamw(w_ref[...], g, m_ref[...], v_ref[...])
        g_ref[...] = g
        d_ref[...] = d
        nm_ref[...] = nm
        nv_ref[...] = nv

    sds = jax.ShapeDtypeStruct(w.shape, F32)
    return pl.pallas_call(body, out_shape=(sds,) * 4, grid=(2, nt), in_specs=[wspec, wspec, wspec, gspec, gspec],
                          out_specs=(wspec,) * 4, compiler_params=_params(("parallel", "parallel")), name=name)(w, m, v, g0, g1)


def adamw_flat(name, w, g, m, v, tr):
    rows, cols = w.shape

    def body(w_ref, g_ref, m_ref, v_ref, d_ref, nm_ref, nv_ref):
        d, nm, nv = _adamw(w_ref[...], g_ref[...], m_ref[...], v_ref[...])
        d_ref[...] = d
        nm_ref[...] = nm
        nv_ref[...] = nv

    blk = pl.BlockSpec((tr, cols), lambda i: (i, 0))
    sds = jax.ShapeDtypeStruct((rows, cols), F32)
    return pl.pallas_call(body, out_shape=(sds,) * 3, grid=(rows // tr,), in_specs=[blk] * 4, out_specs=(blk,) * 3,
                          compiler_params=_params(("parallel",)), name=name)(w, g, m, v)


def _flips(axes):
    out = []
    for fx in ((0, 1) if "x" in axes else (0,)):
        for fy in ((0, 1) if "y" in axes else (0,)):
            for fc in ((0, 1) if "c" in axes else (0,)):
                if fx or fy or fc:
                    out.append((fx, fy, fc))
    return out


def _slot(pos, axes):
    s = 0
    for name, p in zip(("x", "y", "c"), pos):
        if name in axes:
            s = 2 * s + p
    return s


def _exchange(name, arrs, axes, scatter):
    flips = _flips(axes)
    n = len(flips) + 1
    na = len(arrs)

    def body(*refs):
        ins, outs = refs[:na], refs[na:2 * na]
        send_sems, recv_sems, local_sems = refs[2 * na:]
        me = (lax.axis_index("x"), lax.axis_index("y"), lax.axis_index("c"))
        my = _slot(me, axes)
        peers = [tuple((1 - p) if f else p for p, f in zip(me, fl)) for fl in flips]

        def src(a, dest_slot):
            return ins[a].at[dest_slot] if scatter else ins[a]

        local = [pltpu.make_async_copy(src(a, my), outs[a].at[my], local_sems.at[a]) for a in range(na)]
        for cp in local:
            cp.start()

        def remote(a, j, landing_slot, dest_slot):
            return pltpu.make_async_remote_copy(
                src_ref=src(a, dest_slot), dst_ref=outs[a].at[landing_slot], send_sem=send_sems.at[a * len(flips) + j],
                recv_sem=recv_sems.at[a * len(flips) + j], device_id=peers[j], device_id_type=pl.DeviceIdType.MESH)

        sends = [remote(a, j, my, _slot(peers[j], axes)) for a in range(na) for j in range(len(flips))]
        for cp in sends:
            cp.start()
        for a in range(na):
            for j in range(len(flips)):
                remote(a, j, _slot(peers[j], axes), _slot(peers[j], axes)).wait_recv()
        for cp in sends:
            cp.wait_send()
        for cp in local:
            cp.wait()

    if scatter:
        out_shape = tuple(jax.ShapeDtypeStruct(a.shape, a.dtype) for a in arrs)
    else:
        out_shape = tuple(jax.ShapeDtypeStruct((n,) + a.shape, a.dtype) for a in arrs)
    anyspec = pl.BlockSpec(memory_space=pl.ANY)
    return pl.pallas_call(
        body, out_shape=out_shape, in_specs=[anyspec] * na, out_specs=(anyspec,) * na,
        scratch_shapes=[pltpu.SemaphoreType.DMA((na * len(flips),)), pltpu.SemaphoreType.DMA((na * len(flips),)),
                        pltpu.SemaphoreType.DMA((na,))],
        name=name)(*arrs)


def all_gather(name, arrs, axes):
    return _exchange(name, arrs, axes, False)


def all_to_all(name, arrs, axes):
    return _exchange(name, arrs, axes, True)


_HBM = pl.BlockSpec(memory_space=pltpu.HBM)
_SEM = pl.BlockSpec(memory_space=pltpu.SEMAPHORE)
_EFFECT = pltpu.SideEffectType.DATAFLOW_SIDE_EFFECTING


def place_own(name, arrs, axes, scatter):
    n = len(_flips(axes)) + 1
    na = len(arrs)

    def body(*refs):
        ins, outs, sems = refs[:na], refs[na:2 * na], refs[2 * na]
        my = _slot((lax.axis_index("x"), lax.axis_index("y"), lax.axis_index("c")), axes)
        copies = [pltpu.make_async_copy(ins[a].at[my] if scatter else ins[a], outs[a].at[my], sems.at[a]) for a in range(na)]
        for cp in copies:
            cp.start()
        for cp in copies:
            cp.wait()

    out_shape = tuple(jax.ShapeDtypeStruct(a.shape if scatter else (n,) + a.shape, a.dtype) for a in arrs)
    anyspec = pl.BlockSpec(memory_space=pl.ANY)
    return pl.pallas_call(body, out_shape=out_shape, in_specs=[anyspec] * na, out_specs=(anyspec,) * na,
                          scratch_shapes=[pltpu.SemaphoreType.DMA((na,))], name=name)(*arrs)


def _peers(axes):
    me = (lax.axis_index("x"), lax.axis_index("y"), lax.axis_index("c"))
    return me, [tuple((1 - p) if f else p for p, f in zip(me, fl)) for fl in _flips(axes)]


def place_tile(name, arr, layer, my, slots=4, dtype=BF16, after=None):
    _, r, cols = arr.shape
    tr = _tile_rows(r, cols)

    def body(my_ref, x_ref, *rest):
        rest[-1][...] = x_ref[...].astype(dtype)

    in_specs = [pl.BlockSpec((None, tr, cols), lambda i, my: (layer, i, 0))]
    args = [arr]
    if after is not None:
        in_specs.append(pl.BlockSpec(after.shape, lambda i, my: (0, 0)))
        args.append(after)
    grid_spec = pltpu.PrefetchScalarGridSpec(num_scalar_prefetch=1, grid=(r // tr,), in_specs=in_specs,
                                             out_specs=pl.BlockSpec((None, tr, cols), lambda i, my: (my[0], i, 0)))
    return pl.pallas_call(body, out_shape=jax.ShapeDtypeStruct((slots, r, cols), dtype), grid_spec=grid_spec,
                          compiler_params=_params(("parallel",)), name=name)(my, *args)


def exchange_start(name, groups, axes, scatter):
    flat = [(p if scatter else (p,)) for grp in groups for p in grp]
    per = 2 if scatter else 1
    na, ng, npeer = len(flat), len(groups), len(_flips(axes))

    def body(*refs):
        ops = refs[:per * na]
        zones = ops[(per - 1) * na:]
        sems, token = refs[per * na:per * na + 2 * ng], refs[-1]
        me, peers = _peers(axes)
        my = _slot(me, axes)
        ai = 0
        for g, grp in enumerate(groups):
            for k in range(len(grp)):
                for j, peer in enumerate(peers):
                    src = ops[ai].at[_slot(peer, axes)] if scatter else zones[ai].at[my]
                    dst = zones[ai].at[j] if scatter else zones[ai].at[my]
                    pltpu.make_async_remote_copy(
                        src_ref=src, dst_ref=dst, send_sem=sems[2 * g].at[k * npeer + j],
                        recv_sem=sems[2 * g + 1].at[k * npeer + j], device_id=peer, device_id_type=pl.DeviceIdType.MESH).start()
                ai += 1
        token[...] = jnp.zeros_like(token)

    out_shape, out_specs = [], []
    for grp in groups:
        out_shape += [pltpu.SemaphoreType.DMA((npeer * len(grp),))] * 2
        out_specs += [_SEM, _SEM]
    for idx in range(per):
        out_shape += [pltpu.HBM(p[idx].shape, p[idx].dtype) for p in flat]
        out_specs += [_HBM] * na
    out_shape.append(jax.ShapeDtypeStruct((SUBLANES, 128), F32))
    out_specs.append(pl.BlockSpec(memory_space=pltpu.VMEM))
    args = [pltpu.with_memory_space_constraint(p[idx], pltpu.HBM) for idx in range(per) for p in flat]
    res = pl.pallas_call(body, out_shape=tuple(out_shape), in_specs=[_HBM] * (per * na), out_specs=tuple(out_specs),
                         input_output_aliases={i: 2 * ng + i for i in range(per * na)},
                         compiler_params=pltpu.CompilerParams(has_side_effects=_EFFECT), name=name)(*args)
    thru = res[2 * ng:2 * ng + per * na]
    out, ai = [], 0
    for g, grp in enumerate(groups):
        srcs = list(thru[ai:ai + len(grp)]) if scatter else []
        zones = list(thru[(per - 1) * na + ai:(per - 1) * na + ai + len(grp)])
        out.append(((res[2 * g], res[2 * g + 1]), srcs, zones))
        ai += len(grp)
    return out, res[-1]


def exchange_wait(name, group, after, axes, scatter):
    (send_sems, recv_sems), srcs, zones = group
    n, ns = len(zones), len(srcs)
    npeer = len(_flips(axes))

    def body(*refs):
        z_refs = refs[ns:ns + n]
        ssem, rsem = refs[ns + n], refs[ns + n + 1]
        _, peers = _peers(axes)
        for k in range(n):
            for j, peer in enumerate(peers):
                part = z_refs[k].at[j if scatter else _slot(peer, axes)]
                copy = pltpu.make_async_remote_copy(
                    src_ref=part, dst_ref=part, send_sem=ssem.at[k * npeer + j], recv_sem=rsem.at[k * npeer + j],
                    device_id=peer, device_id_type=pl.DeviceIdType.MESH)
                copy.wait_send()
                copy.wait_recv()

    ops = list(srcs) + list(zones)
    out_shape = tuple(pltpu.HBM(a.shape, a.dtype) for a in ops)
    res = pl.pallas_call(body, out_shape=out_shape, in_specs=[_HBM] * len(ops) + [_SEM, _SEM, pl.BlockSpec(memory_space=pl.ANY)],
                         out_specs=(_HBM,) * len(ops), input_output_aliases={i: i for i in range(len(ops))},
                         compiler_params=pltpu.CompilerParams(has_side_effects=_EFFECT), name=name)(*ops, send_sems, recv_sems, after)
    return list(res[:ns]), list(res[ns:])


def _pair_exchange(name, ins, in_specs, n_steps, tile, fn_send, fn_out, out_shape, out_spec, prefetch=None, wire=F32):
    n_in = len(ins)

    def body(*refs):
        if prefetch is not None:
            refs = refs[1:]
        in_refs, o_ref = refs[:n_in], refs[n_in]
        send_buf, recv_buf, send_sems, recv_sems, credit = refs[n_in + 1:]
        i = pl.program_id(0)
        slot = lax.rem(i, 2)
        c = lax.axis_index("c")
        sibling = (lax.axis_index("x"), lax.axis_index("y"), 1 - c)
        vals = [r[...] for r in in_refs]
        send_buf[slot] = fn_send(*vals, c).astype(wire)

        @pl.when(i >= 2)
        def _():
            pl.semaphore_wait(credit, 1)

        copy = pltpu.make_async_remote_copy(
            src_ref=send_buf.at[slot], dst_ref=recv_buf.at[slot], send_sem=send_sems.at[slot], recv_sem=recv_sems.at[slot],
            device_id=sibling, device_id_type=pl.DeviceIdType.MESH)
        copy.start()
        copy.wait_recv()
        o_ref[...] = fn_out(*vals, recv_buf[slot], c).astype(o_ref.dtype)
        copy.wait_send()

        @pl.when(i < n_steps - 2)
        def _():
            pl.semaphore_signal(credit, inc=1, device_id=sibling, device_id_type=pl.DeviceIdType.MESH)

    scratch = [pltpu.VMEM((2,) + tile, wire), pltpu.VMEM((2,) + tile, wire), pltpu.SemaphoreType.DMA((2,)),
               pltpu.SemaphoreType.DMA((2,)), pltpu.SemaphoreType.REGULAR]
    if prefetch is None:
        return pl.pallas_call(body, out_shape=out_shape, grid=(n_steps,), in_specs=in_specs, out_specs=out_spec,
                              scratch_shapes=scratch, compiler_params=_params(("arbitrary",)), name=name)(*ins)
    grid_spec = pltpu.PrefetchScalarGridSpec(num_scalar_prefetch=1, grid=(n_steps,), in_specs=in_specs, out_specs=out_spec,
                                             scratch_shapes=scratch)
    return pl.pallas_call(body, out_shape=out_shape, grid_spec=grid_spec, compiler_params=_params(("arbitrary",)),
                          name=name)(prefetch, *ins)


def _tile_rows(rows, cols, f32_bytes=3 << 19):
    return _row_tile(rows, max(2 * SUBLANES, f32_bytes // (4 * cols)), 2 * SUBLANES)


def pair_sum(name, x):
    rows, cols = x.shape
    tr = _tile_rows(rows, cols)
    return _pair_exchange(name, [x], [pl.BlockSpec((tr, cols), lambda i: (i, 0))], rows // tr, (tr, cols),
                          lambda v, c: v, lambda v, got, c: v + got, jax.ShapeDtypeStruct((rows, cols), F32),
                          pl.BlockSpec((tr, cols), lambda i: (i, 0)))


def reduce_cores(name, g):
    _, m, cols = g.shape
    tr = _tile_rows(m, cols, 3 << 20)

    def fn_send(g0, g1, c):
        return jnp.where(c == 0, g1, g0)

    def fn_out(g0, g1, got, c):
        return jnp.where(c == 0, g0, g1) + got.astype(F32)

    return _pair_exchange(
        name, [g, g], [pl.BlockSpec((None, tr, cols), lambda i: (0, i, 0)), pl.BlockSpec((None, tr, cols), lambda i: (1, i, 0))],
        m // tr, (tr, cols), fn_send, fn_out, jax.ShapeDtypeStruct((m, cols), BF16), pl.BlockSpec((tr, cols), lambda i: (i, 0)),
        wire=BF16)


def sum_and_share(name, own, parts, my):
    n, r, cols = parts.shape
    tr = _tile_rows(r, cols, 3 << 20)

    def total(o, p):
        acc = o.astype(F32)
        for j in range(n):
            acc = acc + p[j].astype(F32)
        return acc

    def fn_send(o, p, c):
        return total(o, p)

    def fn_out(o, p, got, c):
        mine = total(o, p)
        return jnp.stack([jnp.where(c == 0, mine, got), jnp.where(c == 0, got, mine)])

    return _pair_exchange(
        name, [own, parts], [pl.BlockSpec((None, tr, cols), lambda i, my_ref: (my_ref[0], i, 0)), pl.BlockSpec((n, tr, cols), lambda i, my_ref: (0, i, 0))],
        r // tr, (tr, cols), fn_send, fn_out, jax.ShapeDtypeStruct((2, r, cols), F32),
        pl.BlockSpec((2, tr, cols), lambda i, my_ref: (0, i, 0)), prefetch=my)


def _block_diag(blocks):
    g, r, c = blocks.shape
    eye = jnp.eye(g, dtype=blocks.dtype)
    return (blocks[:, :, None, :] * eye[:, None, :, None]).reshape(g * r, g * c)


def _diag_blocks(mat, g):
    r, c = mat.shape[0] // g, mat.shape[1] // g
    eye = jnp.eye(g, dtype=mat.dtype)
    return (mat.reshape(g, r, g, c) * eye[:, None, :, None]).sum(axis=2)


def _halves(gfull, shards):
    rows, cols = gfull.shape
    return gfull.reshape(shards, 2, rows // shards // 2, cols).transpose(1, 0, 2, 3)


def _step(inp):
    x = inp['x'][0]
    target = inp['loss_target'][0]
    rows, d = x.shape
    depth = inp['w_in'].shape[0]
    mix_w = d // 4
    n_state = S5_GROUPS * S5_STATE
    ffn_half = inp['ffn_w_up'].shape[2]
    tm = min(512, rows)
    tc = min(256, rows)
    tl = min(512, rows)
    xy = ("x", "y")

    my_chip = (2 * lax.axis_index("x") + lax.axis_index("y")).astype(jnp.int32).reshape(1)
    small_keys = [(nme, None) for nme in SMALL_SHARDED]
    group_keys = []
    for l in range(depth):
        group_keys += [[('w_in', l)] + (small_keys if l == 0 else []),
                       [('w_out', l), ('s5_w_glu', l), ('cv_w_pw', l)], [('ffn_w_up', l)], [('ffn_w_down', l)]]

    def zone_of(key, after=None):
        nme, l = key
        src = jnp.swapaxes(inp[nme], 1, 2) if nme == 'ffn_w_up' else inp[nme]
        return place_tile(f"place_{nme}{l}", src, l, my_chip, after=after)

    zones = {('w_in', 0): zone_of(('w_in', 0))}
    zones.update(zip(small_keys, place_own("place_small", [inp[nme] for nme in SMALL_SHARDED], xy, False)))
    first_group, first_token = exchange_start("gather_start_first", [[zones[key] for key in group_keys[0]]], xy, False)
    for grp in group_keys[1:]:
        zones.update({key: zone_of(key, first_token) for key in grp})
    rest_groups, gather_token = exchange_start("gather_start", [[zones[key] for key in grp] for grp in group_keys[1:]], xy, False)
    gather_groups = first_group + rest_groups

    def gathered(gi, after):
        return dict(zip(group_keys[gi], exchange_wait(f"gather_wait{gi}", gather_groups[gi], after, xy, False)[1]))

    def full_small(g):
        return g.transpose(1, 2, 0, 3).reshape(g.shape[1], g.shape[2], 4 * g.shape[3])

    gsum = jnp.repeat(jnp.eye(128, dtype=F32)[:S5_GROUPS], S5_STATE, axis=0)

    saved = []
    grads = {nme: [None] * depth for nme in WEIGHTS}
    xcur = x
    for l in range(depth):
        vec = lambda a: a[l].reshape(1, -1)
        gain = vec(inp['norm_mix_g']) + (gather_token[0, 0] if l == 0 else 0.0)
        h = rms_fwd(f"rms_mix{l}", xcur, gain, tm)
        got = gathered(4 * l, h)
        w_in = got[('w_in', l)]
        if l == 0:
            cv_w_dw, lru_w_conv, ffn_w_dw = (full_small(got[(nme, None)]) for nme in ('cv_w_dw', 'lru_w_conv', 'ffn_w_dw'))
        ncol = w_in.shape[2]

        lam_re, lam_im = vec(inp['s5_lam_re']), vec(inp['s5_lam_im'])
        log_step = jnp.broadcast_to(inp['s5_log_step'][l][:, None], (S5_GROUPS, S5_STATE)).reshape(1, n_state)
        b_re = _block_diag(inp['s5_b_re'][l].transpose(0, 2, 1))
        b_im = _block_diag(inp['s5_b_im'][l].transpose(0, 2, 1))
        c_cat = jnp.stack([_block_diag(inp['s5_c_re'][l].transpose(0, 2, 1)),
                           -_block_diag(inp['s5_c_im'][l].transpose(0, 2, 1))]).astype(BF16)
        a_bar, b_bar = s5_param_fwd(f"s5_param_fwd{l}", lam_re, lam_im, log_step, b_re, b_im)
        w_r = _block_diag(inp['lru_w_r'][l]).astype(BF16)
        w_i = _block_diag(inp['lru_w_i'][l]).astype(BF16)
        pool_bd = _block_diag(inp['pool_w'][l]).astype(BF16)
        gate_pars = [w_r, w_i, vec(inp['lru_b_r']), vec(inp['lru_b_i']), vec(inp['lru_lam'])]

        proj = _mm(f"proj{l}", h, w_in, jax.ShapeDtypeStruct((rows, 4 * ncol), F32), (4, rows // tm),
                   pl.BlockSpec((tm, d), lambda j, i: (i, 0)), pl.BlockSpec((None, d, ncol), lambda j, i: (j, 0, 0)),
                   pl.BlockSpec((tm, ncol), lambda j, i: (i, j)), NN)
        proj3 = proj.reshape(1, rows, 4 * ncol)
        nh = n_state // 2
        ts = min(2048, rows)
        cw, sw = mix_w // 4, n_state // 4
        bu = _mm(f"s5_bu{l}", proj, b_bar, jax.ShapeDtypeStruct((2, rows, n_state), F32), (rows // ts, 2, 4),
                 pl.BlockSpec((ts, cw), lambda i, c, s: (i, s)), pl.BlockSpec((None, cw, sw), lambda i, c, s: (c, s, s)),
                 pl.BlockSpec((None, ts, sw), lambda i, c, s: (c, i, s)), NN)
        z = s5_scan_fwd(f"s5_scan{l}", bu, a_bar)
        y_ssm = _mm(f"s5_read{l}", z, c_cat, jax.ShapeDtypeStruct((rows, mix_w), F32), (rows // ts, 4, 2),
                    pl.BlockSpec((None, ts, sw), lambda i, s, c: (c, i, s)), pl.BlockSpec((None, sw, cw), lambda i, s, c: (c, s, s)),
                    pl.BlockSpec((ts, cw), lambda i, s, c: (i, s)), NN, k_axis=2)
        (h0,) = _rowwise(f"cv_glu{l}", _glu, [(proj, 1, mix_w), (proj, 2, mix_w)], [], 1, [(mix_w, F32)], tm)
        h1 = dwconv_fwd(f"cv_conv{l}", h0.reshape(1, rows, mix_w), 0, mix_w, cv_w_dw[l][None], vec(inp['cv_b_dw'])[None],
                        CV_TAPS, tc)[0]
        xc = dwconv_fwd(f"lru_conv{l}", proj3, 3, mix_w, lru_w_conv[l][None], vec(inp['lru_b_conv'])[None], LRU_TAPS, tc)[0]
        a_t, b_t = _rowwise(f"lru_gate{l}", _lru_gate, [(xc, 0, mix_w)], gate_pars, 2, [(mix_w, F32), (mix_w, F32)], tm)
        hseq = lru_scan_fwd(f"lru_scan{l}", a_t, b_t)
        dgp = pool_fwd(f"pool{l}", proj, 5, tc)
        got = gathered(4 * l + 1, proj)
        w_out = got[('w_out', l)].reshape(d, d)
        w_glu, w_pw = got[('s5_w_glu', l)].reshape(mix_w, mix_w), got[('cv_w_pw', l)].reshape(mix_w, mix_w)
        post_pars = [vec(inp['s5_d']), w_glu, vec(inp['s5_b_glu']), vec(inp['cv_ln_g']), vec(inp['cv_ln_b']), w_pw,
                     vec(inp['cv_b_pw']), pool_bd, vec(inp['pool_scale'])]
        post_rows = [(y_ssm, 0, mix_w), (proj, 0, mix_w), (h1, 0, mix_w), (hseq, 0, mix_w), (proj, 4, mix_w), (dgp, 0, mix_w)]
        (mixed,) = _rowwise(f"mix_post{l}", _mix_post, post_rows, post_pars, 1, [(d, BF16)], tm)
        x1 = _mm(f"out_proj{l}", mixed, w_out, jax.ShapeDtypeStruct((rows, d), F32), (2, rows // tm),
                 pl.BlockSpec((tm, d), lambda j, i: (i, 0)), pl.BlockSpec((d, d // 2), lambda j, i: (0, j)),
                 pl.BlockSpec((tm, d // 2), lambda j, i: (i, j)), NN,
                 add=xcur, add_spec=pl.BlockSpec((tm, d // 2), lambda j, i: (i, j)))

        h2 = rms_fwd(f"rms_ffn{l}", x1, vec(inp['norm_ffn_g']), tm)
        tu = min(256, rows)
        w_up = gathered(4 * l + 2, x1)[('ffn_w_up', l)]
        up = _mm(f"ffn_up{l}", h2, w_up, jax.ShapeDtypeStruct((4, rows, ffn_half), F32), (4, rows // tu),
                 pl.BlockSpec((tu, d), lambda k, i: (i, 0)), pl.BlockSpec((None, ffn_half, d), lambda k, i: (k, 0, 0)),
                 pl.BlockSpec((None, tu, ffn_half), lambda k, i: (k, i, 0)), NT)
        w_dw = ffn_w_dw[l].reshape(FFN_TAPS, 2, ffn_half).transpose(1, 0, 2)
        b_dw = inp['ffn_b_dw'][l].reshape(2, 1, ffn_half)
        act = ffn_gate_fwd(f"ffn_gate{l}", up, w_dw, b_dw, tc)
        w_down = gathered(4 * l + 3, up)[('ffn_w_down', l)].reshape(2, ffn_half, d)
        x2 = _mm(f"ffn_down{l}", act, w_down, jax.ShapeDtypeStruct((rows, d), F32), (rows // tm, 4),
                 pl.BlockSpec((2, tm, ffn_half), lambda i, j: (0, i, 0)), pl.BlockSpec((2, ffn_half, d // 4), lambda i, j: (0, 0, j)),
                 pl.BlockSpec((tm, d // 4), lambda i, j: (i, j)), NN, inner=("lead", 2),
                 add=x1, add_spec=pl.BlockSpec((tm, d // 4), lambda i, j: (i, j)))
        saved.append(dict(x=xcur, h=h, proj=proj, z=z, y_ssm=y_ssm, h0=h0, h1=h1, xc=xc, a_t=a_t, hseq=hseq, dgp=dgp,
                          mixed=mixed, x1=x1, h2=h2, up=up, act=act, w_in=w_in, w_out=w_out, w_up=w_up, w_down=w_down,
                          a_bar=a_bar, b_bar=b_bar, c_cat=c_cat, post_pars=post_pars, gate_pars=gate_pars, w_dw=w_dw, b_dw=b_dw,
                          s5=(lam_re, lam_im, log_step, b_re, b_im), cv_w=cv_w_dw[l][None], lru_w=lru_w_conv[l][None]))
        xcur = x2

    loss_row, dx, dg_final = final_loss("final_loss", xcur, inp['norm_final_g'].reshape(1, d), target, tm)
    grads['norm_final_g'] = dg_final.reshape(d)

    big_g = {nme: [None] * depth for nme in BIG}
    reduce_groups = []

    def start_reduce(tag, keys):
        pieces = []
        for nme, lyr in keys:
            g = big_g[nme][lyr]
            if nme == 'ffn_w_down':
                g = g.reshape(2, 4, ffn_half // 2, d // 2)
            pieces.append(reduce_cores(f"reduce_cores_{nme}{lyr}", g.reshape(2, -1, g.shape[-1])).reshape(g.shape[1:]))
        landing = [lax.empty((3,) + p.shape[1:], p.dtype) for p in pieces]
        groups, token = exchange_start(f"reduce_start_{tag}", [list(zip(pieces, landing))], xy, True)
        reduce_groups.append((tag, keys, groups[0]))
        return token

    for l in reversed(range(depth)):
        s = saved[l]
        ncol = s['w_in'].shape[2]
        nh = n_state // 2
        tu = min(256, rows)
        dact = _mm(f"d_act{l}", dx, s['w_down'], jax.ShapeDtypeStruct((2, rows, ffn_half), F32), (2, rows // tu),
                   pl.BlockSpec((tu, d), lambda k, i: (i, 0)), pl.BlockSpec((None, ffn_half, d), lambda k, i: (k, 0, 0)),
                   pl.BlockSpec((None, tu, ffn_half), lambda k, i: (k, i, 0)), NT)
        tn = d // 4
        tk = min(1024, rows)
        tkb = min(2048, rows)
        big_g['ffn_w_down'][l] = _mm(
            f"dw_down{l}", s['act'], dx, jax.ShapeDtypeStruct((2, 2, ffn_half, d // 2), F32), (2, 4, rows // tkb),
            pl.BlockSpec((None, tkb, ffn_half), lambda hh, n, k: (hh, k, 0)), pl.BlockSpec((tkb, tn), lambda hh, n, k: (k, n)),
            pl.BlockSpec((None, None, ffn_half, tn), lambda hh, n, k: (n // 2, hh, 0, n % 2)), TN, k_axis=2)
        dup, dw_dw, db_dw = ffn_gate_bwd(f"ffn_gate_bwd{l}", s['up'], dact, s['w_dw'], s['b_dw'], tc)
        grads['ffn_w_dw'][l] = dw_dw.transpose(1, 0, 2).reshape(FFN_TAPS, 2 * ffn_half)
        grads['ffn_b_dw'][l] = db_dw.reshape(2 * ffn_half)
        dup = dup.reshape(4, rows, ffn_half)
        tm2 = min(1024, rows)
        dh2 = _mm(f"d_h2{l}", dup, s['w_up'], jax.ShapeDtypeStruct((rows, d), F32), (rows // tm2, 2, 4),
                  pl.BlockSpec((None, tm2, ffn_half), lambda i, j, k: (k, i, 0)), pl.BlockSpec((None, ffn_half, d // 2), lambda i, j, k: (k, 0, j)),
                  pl.BlockSpec((tm2, d // 2), lambda i, j, k: (i, j)), NN, k_axis=2)
        tmm = d // 4
        big_g['ffn_w_up'][l] = _mm(
            f"dw_up{l}", dup, s['h2'], jax.ShapeDtypeStruct((2, 4, ffn_half, d // 2), F32), (4, 4, rows // tkb),
            pl.BlockSpec((None, tkb, ffn_half), lambda k4, n, k: (k4, k, 0)), pl.BlockSpec((tkb, tn), lambda k4, n, k: (k, n)),
            pl.BlockSpec((None, None, ffn_half, tn), lambda k4, n, k: (n // 2, k4, 0, n % 2)), TN, k_axis=2)
        token = start_reduce(f"ffn{l}", [('ffn_w_down', l), ('ffn_w_up', l)])
        dx1, dg = rms_bwd(f"rms_ffn_bwd{l}", s['x1'], inp['norm_ffn_g'][l].reshape(1, d) + token[0, 0], dh2, dx, tm)
        grads['norm_ffn_g'][l] = dg.reshape(d)
        dmixed = _mm(f"d_mixed{l}", dx1, s['w_out'], jax.ShapeDtypeStruct((rows, d), F32), (rows // tm, 4),
                     pl.BlockSpec((tm, d), lambda i, j: (i, 0)), pl.BlockSpec((d // 4, d), lambda i, j: (j, 0)),
                     pl.BlockSpec((tm, d // 4), lambda i, j: (i, j)), NT)
        tq = mix_w // 2
        big_g['w_out'][l] = _mm(
            f"dw_out{l}", s['mixed'], dx1, jax.ShapeDtypeStruct((2, 4, tq, d), F32), (4, rows // tk),
            pl.BlockSpec((tk, 2 * tq), lambda t, k: (k, t)), pl.BlockSpec((tk, d), lambda t, k: (k, 0)),
            pl.BlockSpec((2, None, tq, d), lambda t, k: (0, t, 0, 0)), TN, k_axis=1)
        post_rows = [(s['y_ssm'], 0, mix_w), (s['proj'], 0, mix_w), (s['h1'], 0, mix_w), (s['hseq'], 0, mix_w),
                     (s['proj'], 4, mix_w), (s['dgp'], 0, mix_w), (dmixed, 0, d)]
        res = _rowwise(f"mix_post_bwd{l}", _mix_post, post_rows, s['post_pars'], 1, [(mix_w, F32)] * 6, tm, with_grads=True)
        dy_ssm, du_dir, dh1, dhseq, dlru_g, ddgp = res[:6]
        dd, dwglu, dbglu, dlng, dlnb, dwpw, dbpw, dpoolbd, dscale = res[6:]
        grads['s5_d'][l], grads['s5_b_glu'][l] = dd.reshape(mix_w), dbglu.reshape(mix_w)
        grads['cv_ln_g'][l], grads['cv_ln_b'][l], grads['cv_b_pw'][l] = dlng.reshape(mix_w), dlnb.reshape(mix_w), dbpw.reshape(mix_w)
        grads['pool_w'][l] = _diag_blocks(dpoolbd, len(POOL_WINDOWS))
        grads['pool_scale'][l] = dscale.reshape(mix_w)
        big_g['s5_w_glu'][l] = _halves(dwglu, 4)
        big_g['cv_w_pw'][l] = _halves(dwpw, 4)
        ts = min(2048, rows)
        cw, sw = mix_w // 4, n_state // 4
        slab = jnp.arange(mix_w)[:, None] // cw == jnp.arange(n_state)[None, :] // sw
        dz = _mm(f"s5_dz{l}", dy_ssm, s['c_cat'], jax.ShapeDtypeStruct((2, rows, n_state), F32), (rows // ts, 2, 4),
                 pl.BlockSpec((ts, cw), lambda i, c, q: (i, q)), pl.BlockSpec((None, sw, cw), lambda i, c, q: (c, q, q)),
                 pl.BlockSpec((None, ts, sw), lambda i, c, q: (c, i, q)), NT)
        dccat = _mm(f"s5_dc{l}", s['z'], dy_ssm, jax.ShapeDtypeStruct((2, n_state, mix_w), F32), (2, 4, rows // tk),
                    pl.BlockSpec((None, tk, sw), lambda c, q, k: (c, k, q)), pl.BlockSpec((tk, cw), lambda c, q, k: (k, q)),
                    pl.BlockSpec((None, sw, cw), lambda c, q, k: (c, q, q)), TN, k_axis=2)
        dccat = jnp.where(slab.T, dccat, 0.0)
        grads['s5_c_re'][l] = _diag_blocks(dccat[0], S5_GROUPS).transpose(0, 2, 1)
        grads['s5_c_im'][l] = -_diag_blocks(dccat[1], S5_GROUPS).transpose(0, 2, 1)
        lam, da_bar = s5_scan_bwd(f"s5_scan_bwd{l}", dz, s['z'], s['a_bar'])
        du = _mm(f"s5_du{l}", lam, s['b_bar'], jax.ShapeDtypeStruct((rows, mix_w), F32), (rows // ts, 4, 2),
                 pl.BlockSpec((None, ts, sw), lambda i, q, c: (c, i, q)), pl.BlockSpec((None, cw, sw), lambda i, q, c: (c, q, q)),
                 pl.BlockSpec((ts, cw), lambda i, q, c: (i, q)), NT, k_axis=2,
                 add=du_dir, add_spec=pl.BlockSpec((ts, cw), lambda i, q, c: (i, q)))
        dbbar = _mm(f"s5_db{l}", s['proj'], lam, jax.ShapeDtypeStruct((2, mix_w, n_state), F32), (2, 4, rows // tk),
                    pl.BlockSpec((tk, cw), lambda c, q, k: (k, q)), pl.BlockSpec((None, tk, sw), lambda c, q, k: (c, k, q)),
                    pl.BlockSpec((None, cw, sw), lambda c, q, k: (c, q, q)), TN, k_axis=2)
        dbbar = jnp.where(slab, dbbar, 0.0)
        dlr, dli, dls, dbre, dbim = s5_param_bwd(f"s5_param_bwd{l}", *s['s5'], da_bar, dbbar, gsum)
        grads['s5_lam_re'][l] = dlr.reshape(S5_GROUPS, S5_STATE)
        grads['s5_lam_im'][l] = dli.reshape(S5_GROUPS, S5_STATE)
        grads['s5_log_step'][l] = dls[0, :S5_GROUPS]
        grads['s5_b_re'][l] = _diag_blocks(dbre, S5_GROUPS).transpose(0, 2, 1)
        grads['s5_b_im'][l] = _diag_blocks(dbim, S5_GROUPS).transpose(0, 2, 1)
        dh0, dw_cv, db_cv = dwconv_bwd(f"cv_conv_bwd{l}", dh1.reshape(1, rows, mix_w), s['h0'].reshape(1, rows, mix_w), 0, mix_w,
                                       s['cv_w'], CV_TAPS, tc)
        grads['cv_w_dw'][l], grads['cv_b_dw'][l] = dw_cv[0], db_cv.reshape(mix_w)
        dv, dgg = _rowwise(f"cv_glu_bwd{l}", _glu, [(s['proj'], 1, mix_w), (s['proj'], 2, mix_w), (dh0[0], 0, mix_w)], [], 1,
                           [(mix_w, F32)] * 2, tm, with_grads=True)
        da_t, db_t = lru_scan_bwd(f"lru_scan_bwd{l}", dhseq, s['a_t'], s['hseq'])
        res = _rowwise(f"lru_gate_bwd{l}", _lru_gate, [(s['xc'], 0, mix_w), (da_t, 0, mix_w), (db_t, 0, mix_w)], s['gate_pars'], 2,
                       [(mix_w, F32)], tm, with_grads=True)
        dxc, dwr, dwi, dbr, dbi, dlam = res
        grads['lru_w_r'][l], grads['lru_w_i'][l] = _diag_blocks(dwr, LRU_HEADS), _diag_blocks(dwi, LRU_HEADS)
        grads['lru_b_r'][l], grads['lru_b_i'][l], grads['lru_lam'][l] = dbr.reshape(mix_w), dbi.reshape(mix_w), dlam.reshape(mix_w)
        dlx, dw_lc, db_lc = dwconv_bwd(f"lru_conv_bwd{l}", dxc.reshape(1, rows, mix_w), s['proj'].reshape(1, rows, 4 * ncol), 3, mix_w,
                                       s['lru_w'], LRU_TAPS, tc)
        grads['lru_w_conv'][l], grads['lru_b_conv'][l] = dw_lc[0], db_lc.reshape(mix_w)
        dpx = pool_bwd(f"pool_bwd{l}", ddgp, tc)
        dproj = jnp.concatenate([du, dv, dgg, dlx[0], dlru_g, dpx], axis=-1)
        dh = _mm(f"d_h{l}", dproj, s['w_in'], jax.ShapeDtypeStruct((rows, d), F32), (rows // tm, 4),
                 pl.BlockSpec((tm, 4 * ncol), lambda i, j: (i, 0)), pl.BlockSpec((4, d // 4, ncol), lambda i, j: (0, j, 0)),
                 pl.BlockSpec((tm, d // 4), lambda i, j: (i, j)), NT, inner=("cols", 4))
        tk2 = min(2048, rows)
        big_g['w_in'][l] = _mm(
            f"dw_in{l}", s['h'], dproj, jax.ShapeDtypeStruct((2, 4, d // 2, ncol), F32), (4, 2, rows // tk2),
            pl.BlockSpec((tk2, d // 2), lambda k4, m, k: (k, m)), pl.BlockSpec((tk2, ncol), lambda k4, m, k: (k, k4)),
            pl.BlockSpec((None, None, d // 2, ncol), lambda k4, m, k: (m, k4, 0, 0)), TN, k_axis=2)
        token = start_reduce(f"mix{l}", [('w_out', l), ('s5_w_glu', l), ('cv_w_pw', l), ('w_in', l)])
        dx, dg = rms_bwd(f"rms_mix_bwd{l}", s['x'], inp['norm_mix_g'][l].reshape(1, d) + token[0, 0], dh, dx1, tm)
        grads['norm_mix_g'][l] = dg.reshape(d)

    small = [nme for nme in WEIGHTS if nme not in BIG]
    full_g = {nme: (grads[nme] if nme == 'norm_final_g' else jnp.stack(grads[nme])) for nme in small}
    flat = jnp.concatenate([full_g[nme].reshape(-1) for nme in small])
    packed = jnp.pad(flat, (0, (-flat.shape[0]) % (128 * 64))).reshape(-1, 128)
    chip_sum = pair_sum("small_pair_sum", packed)
    small_zone = place_tile("place_small_grads", chip_sum[None], 0, my_chip, dtype=F32)
    (small_group,), small_token = exchange_start("small_start", [[small_zone]], xy, False)

    t_full = {}
    for tag, keys, group in reduce_groups:
        pieces, parts = exchange_wait(f"reduce_wait_{tag}", group, small_token, xy, True)
        for key, own, got in zip(keys, pieces, parts):
            t_full[key] = sum_and_share(f"share_cores_{key[0]}{key[1]}", own, got, my_chip)

    outs, done_big = {}, []
    tiles = {'w_in': 256, 'w_out': 128, 'ffn_w_up': 128, 'ffn_w_down': 256, 's5_w_glu': 64, 'cv_w_pw': 64}
    for nme in BIG:
        g0, g1 = t_full[(nme, 0)], t_full[(nme, 1)]
        if nme == 'ffn_w_up':
            res = adamw_sharded(f"adamw_{nme}", *(jnp.swapaxes(inp[p + nme], 1, 2) for p in ('', 'm_', 'v_')), g0, g1, True, tiles[nme])
            outs[nme] = tuple(jnp.swapaxes(r, 1, 2) for r in res)
        else:
            res = adamw_sharded(f"adamw_{nme}", inp[nme], inp['m_' + nme], inp['v_' + nme], g0, g1, nme == 'ffn_w_down', tiles[nme])
            outs[nme] = res
        done_big.append(res[1][:1, :1, :1].reshape(1))

    after_big = sum(done_big)
    (g4,) = exchange_wait("small_wait", small_group, after_big, xy, False)[1]
    gsum_small = sum_lead("sum_small", g4, 64).reshape(-1)
    red, off = {}, 0
    for nme in small:
        g = gsum_small[off:off + full_g[nme].size].reshape(full_g[nme].shape)
        off += full_g[nme].size
        if nme in SMALL_SHARDED:
            width = inp[nme].shape[2]
            g = lax.dynamic_slice_in_dim(g, my_chip[0] * width, width, axis=2)
        red[nme] = g

    def pack(tree):
        f = jnp.concatenate([tree[nme].reshape(-1) for nme in small])
        return jnp.pad(f, (0, (-f.shape[0]) % (128 * 64))).reshape(-1, 128)

    pd, pm, pv = adamw_flat("adamw_small", pack({n_: inp[n_] for n_ in small}), pack(red), pack({n_: inp['m_' + n_] for n_ in small}),
                            pack({n_: inp['v_' + n_] for n_ in small}), 64)
    off = 0
    for nme in small:
        size, shape = inp[nme].size, inp[nme].shape
        outs[nme] = (red[nme],) + tuple(p.reshape(-1)[off:off + size].reshape(shape) for p in (pd, pm, pv))
        off += size

    loss = lax.psum(loss_row[0, 0], ("x", "y", "c"))
    result = [loss, dx[None]]
    for part in range(4):
        result += [outs[nme][part] for nme in WEIGHTS]
    return tuple(result)


def kernel(x, norm_mix_g, w_in, s5_lam_re, s5_lam_im, s5_log_step, s5_b_re, s5_b_im, s5_c_re, s5_c_im, s5_d, s5_w_glu, s5_b_glu, cv_w_dw, cv_b_dw, cv_ln_g, cv_ln_b, cv_w_pw, cv_b_pw, lru_w_conv, lru_b_conv, lru_w_r, lru_b_r, lru_w_i, lru_b_i, lru_lam, pool_w, pool_scale, w_out, norm_ffn_g, ffn_w_up, ffn_w_dw, ffn_b_dw, ffn_w_down, norm_final_g, loss_target, m_norm_mix_g, m_w_in, m_s5_lam_re, m_s5_lam_im, m_s5_log_step, m_s5_b_re, m_s5_b_im, m_s5_c_re, m_s5_c_im, m_s5_d, m_s5_w_glu, m_s5_b_glu, m_cv_w_dw, m_cv_b_dw, m_cv_ln_g, m_cv_ln_b, m_cv_w_pw, m_cv_b_pw, m_lru_w_conv, m_lru_b_conv, m_lru_w_r, m_lru_b_r, m_lru_w_i, m_lru_b_i, m_lru_lam, m_pool_w, m_pool_scale, m_w_out, m_norm_ffn_g, m_ffn_w_up, m_ffn_w_dw, m_ffn_b_dw, m_ffn_w_down, m_norm_final_g, v_norm_mix_g, v_w_in, v_s5_lam_re, v_s5_lam_im, v_s5_log_step, v_s5_b_re, v_s5_b_im, v_s5_c_re, v_s5_c_im, v_s5_d, v_s5_w_glu, v_s5_b_glu, v_cv_w_dw, v_cv_b_dw, v_cv_ln_g, v_cv_ln_b, v_cv_w_pw, v_cv_b_pw, v_lru_w_conv, v_lru_b_conv, v_lru_w_r, v_lru_b_r, v_lru_w_i, v_lru_b_i, v_lru_lam, v_pool_w, v_pool_scale, v_w_out, v_norm_ffn_g, v_ffn_w_up, v_ffn_w_dw, v_ffn_b_dw, v_ffn_w_down, v_norm_final_g):
    inp = dict(locals())
    return _step(inp)
```

```python
import functools

import jax
import jax.numpy as jnp
from jax import lax
from jax.experimental import pallas as pl
from jax.experimental.pallas import tpu as pltpu

F32 = jnp.float32
BF16 = jnp.bfloat16

VMEM_LIMIT_BYTES = 56 * 1024 * 1024
SUBLANES = 8

EPS = 1e-6
S5_GROUPS, S5_STATE, S5_GROUP_CH = 32, 64, 16
LRU_HEADS, LRU_C = 8, 8.0
POOL_WINDOWS = (2, 4, 8, 16)
CV_TAPS, LRU_TAPS, FFN_TAPS = 31, 4, 3
SCAN_CHUNK = 64
GELU_K0, GELU_K1 = 0.7978845608028654, 0.044715

ADAM_LR, ADAM_B1, ADAM_B2, ADAM_EPS, ADAM_WD, ADAM_STEP = 0.001, 0.9, 0.999, 1e-08, 0.01, 10

NN = ((1,), (0,))
NT = ((1,), (1,))
TN = ((0,), (0,))

WEIGHTS = ['norm_mix_g', 'w_in', 's5_lam_re', 's5_lam_im', 's5_log_step', 's5_b_re', 's5_b_im', 's5_c_re', 's5_c_im',
           's5_d', 's5_w_glu', 's5_b_glu', 'cv_w_dw', 'cv_b_dw', 'cv_ln_g', 'cv_ln_b', 'cv_w_pw', 'cv_b_pw',
           'lru_w_conv', 'lru_b_conv', 'lru_w_r', 'lru_b_r', 'lru_w_i', 'lru_b_i', 'lru_lam', 'pool_w', 'pool_scale',
           'w_out', 'norm_ffn_g', 'ffn_w_up', 'ffn_w_dw', 'ffn_b_dw', 'ffn_w_down', 'norm_final_g']
BIG = ('w_in', 'w_out', 'ffn_w_up', 'ffn_w_down', 's5_w_glu', 'cv_w_pw')
SMALL_SHARDED = {'cv_w_dw': 2, 'lru_w_conv': 2, 'ffn_w_dw': 2}


def _params(sem=None):
    if sem is None:
        return pltpu.CompilerParams(vmem_limit_bytes=VMEM_LIMIT_BYTES)
    return pltpu.CompilerParams(dimension_semantics=sem, vmem_limit_bytes=VMEM_LIMIT_BYTES)


def _row_tile(rows, cap, mult=SUBLANES):
    best = mult
    for t in range(mult, min(rows, cap) + 1, mult):
        if rows % t == 0:
            best = t
    return best


def _bdot(a, b, dims=NN):
    return lax.dot_general(a.astype(BF16), b.astype(BF16), (dims, ((), ())), preferred_element_type=F32)


@jax.custom_vjp
def bdot(a, b):
    return _bdot(a, b)


def _bdot_fwd(a, b):
    return _bdot(a, b), (a, b)


def _bdot_bwd(res, g):
    a, b = res
    return _bdot(g, b, NT).astype(a.dtype), _bdot(a, g, TN).astype(b.dtype)


bdot.defvjp(_bdot_fwd, _bdot_bwd)


def _mm(name, a, b, out_sds, grid, a_spec, b_spec, o_spec, dims, k_axis=None, add=None, add_spec=None, inner=None):
    nk = grid[k_axis] if k_axis is not None else 1
    has_add = add is not None
    acc_shape = tuple(d for d in o_spec.block_shape if d is not None)
    acc_in_out = out_sds.dtype == F32

    def product(a_ref, b_ref):
        if inner is None:
            return _bdot(a_ref[...], b_ref[...], dims)
        kind, n = inner
        width = a_ref.shape[-1] // n
        acc = None
        for j in range(n):
            a_j = a_ref[j] if kind == "lead" else a_ref[:, j * width:(j + 1) * width]
            p = _bdot(a_j, b_ref[j], dims)
            acc = p if acc is None else acc + p
        return acc

    def body(*refs):
        a_ref, b_ref = refs[0], refs[1]
        add_ref = refs[2] if has_add else None
        o_ref = refs[3] if has_add else refs[2]
        prod = product(a_ref, b_ref).reshape(acc_shape)
        if k_axis is None:
            if has_add:
                prod = prod + add_ref[...]
            o_ref[...] = prod.astype(o_ref.dtype)
        else:
            acc_ref = o_ref if acc_in_out else refs[-1]
            k = pl.program_id(k_axis)

            @pl.when(k == 0)
            def _():
                acc_ref[...] = prod

            @pl.when(k > 0)
            def _():
                acc_ref[...] += prod

            if has_add or not acc_in_out:
                @pl.when(k == nk - 1)
                def _():
                    r = acc_ref[...]
                    if has_add:
                        r = r + add_ref[...]
                    o_ref[...] = r.astype(o_ref.dtype)

    sem = tuple("arbitrary" if d == k_axis else "parallel" for d in range(len(grid)))
    in_specs = [a_spec, b_spec] + ([add_spec] if has_add else [])
    args = (a, b) + ((add,) if has_add else ())
    scratch = [pltpu.VMEM(acc_shape, F32)] if (k_axis is not None and not acc_in_out) else []
    return pl.pallas_call(body, out_shape=out_sds, grid=grid, in_specs=in_specs, out_specs=o_spec,
                          scratch_shapes=scratch, compiler_params=_params(sem), name=name)(*args)


def _rms(x, g):
    return x * lax.rsqrt(jnp.mean(x * x, axis=-1, keepdims=True) + EPS) * g


def rms_fwd(name, x, g, tm):
    rows, d = x.shape

    def body(x_ref, g_ref, o_ref):
        o_ref[...] = _rms(x_ref[...], g_ref[...]).astype(BF16)

    return pl.pallas_call(
        body, out_shape=jax.ShapeDtypeStruct((rows, d), BF16), grid=(rows // tm,),
        in_specs=[pl.BlockSpec((tm, d), lambda i: (i, 0)), pl.BlockSpec((1, d), lambda i: (0, 0))],
        out_specs=pl.BlockSpec((tm, d), lambda i: (i, 0)), compiler_params=_params(("parallel",)), name=name)(x, g)


def rms_bwd(name, x, g, dh, dres, tm):
    rows, d = x.shape

    def body(x_ref, g_ref, dh_ref, dres_ref, dx_ref, dg_ref):
        _, vjp = jax.vjp(_rms, x_ref[...], g_ref[...])
        dx, dg = vjp(dh_ref[...])
        dx_ref[...] = dx + dres_ref[...]

        @pl.when(pl.program_id(0) == 0)
        def _():
            dg_ref[...] = jnp.zeros_like(dg_ref)

        dg_ref[...] += dg

    row = pl.BlockSpec((tm, d), lambda i: (i, 0))
    vec = pl.BlockSpec((1, d), lambda i: (0, 0))
    return pl.pallas_call(
        body, out_shape=(jax.ShapeDtypeStruct((rows, d), F32), jax.ShapeDtypeStruct((1, d), F32)), grid=(rows // tm,),
        in_specs=[row, vec, row, row], out_specs=(row, vec), compiler_params=_params(("arbitrary",)), name=name)(x, g, dh, dres)


def final_loss(name, x, g, target, tm):
    rows, d = x.shape

    def body(x_ref, g_ref, t_ref, l_ref, dx_ref, dg_ref):
        def f(xv, gv):
            e = _rms(xv, gv) - t_ref[...]
            return 0.5 * jnp.sum(jnp.mean(e * e, axis=-1))

        loss, (dx, dg) = jax.value_and_grad(f, argnums=(0, 1))(x_ref[...], g_ref[...])
        dx_ref[...] = dx

        @pl.when(pl.program_id(0) == 0)
        def _():
            l_ref[...] = jnp.zeros_like(l_ref)
            dg_ref[...] = jnp.zeros_like(dg_ref)

        l_ref[...] += jnp.full(l_ref.shape, loss, F32)
        dg_ref[...] += dg

    row = pl.BlockSpec((tm, d), lambda i: (i, 0))
    vec = pl.BlockSpec((1, d), lambda i: (0, 0))
    lspec = pl.BlockSpec((1, 128), lambda i: (0, 0))
    return pl.pallas_call(
        body, out_shape=(jax.ShapeDtypeStruct((1, 128), F32), jax.ShapeDtypeStruct((rows, d), F32), jax.ShapeDtypeStruct((1, d), F32)),
        grid=(rows // tm,), in_specs=[row, vec, row], out_specs=(lspec, row, vec),
        compiler_params=_params(("arbitrary",)), name=name)(x, g, target)


def _rowwise(name, fn, row_ins, par_ins, n_row_out, row_out_dtypes, tm, with_grads=False):
    rows = row_ins[0][0].shape[0]
    n_prim = len(row_ins) - (n_row_out if with_grads else 0)
    n_par = len(par_ins)

    def body(*refs):
        ins = [r[...] for r in refs[:len(row_ins) + n_par]]
        outs = refs[len(row_ins) + n_par:]
        prim, cts, pars = ins[:n_prim], ins[n_prim:len(row_ins)], ins[len(row_ins):]
        if not with_grads:
            res = fn(*prim, *pars)
            for o_ref, r in zip(outs, res):
                o_ref[...] = r.astype(o_ref.dtype)
            return
        _, vjp = jax.vjp(fn, *prim, *[p.astype(F32) for p in pars])
        grads = vjp(tuple(cts))
        for o_ref, gr in zip(outs[:n_prim], grads[:n_prim]):
            o_ref[...] = gr.astype(o_ref.dtype)

        @pl.when(pl.program_id(0) == 0)
        def _():
            for o_ref in outs[n_prim:]:
                o_ref[...] = jnp.zeros_like(o_ref)

        for o_ref, gr in zip(outs[n_prim:], grads[n_prim:]):
            o_ref[...] += gr.astype(F32)

    in_specs = [pl.BlockSpec((tm, w), (lambda i, c=c: (i, c))) for (_, c, w) in row_ins]
    in_specs += [pl.BlockSpec(p.shape, (lambda i, n=p.ndim: (0,) * n)) for p in par_ins]
    args = [a for (a, _, _) in row_ins] + list(par_ins)
    if not with_grads:
        out_shape = tuple(jax.ShapeDtypeStruct((rows, w), dt) for (w, dt) in row_out_dtypes)
        out_specs = tuple(pl.BlockSpec((tm, w), lambda i: (i, 0)) for (w, _) in row_out_dtypes)
        sem = ("parallel",)
    else:
        out_shape = tuple(jax.ShapeDtypeStruct((rows, w), dt) for (w, dt) in row_out_dtypes)
        out_shape += tuple(jax.ShapeDtypeStruct(p.shape, F32) for p in par_ins)
        out_specs = tuple(pl.BlockSpec((tm, w), lambda i: (i, 0)) for (w, _) in row_out_dtypes)
        out_specs += tuple(pl.BlockSpec(p.shape, (lambda i, n=p.ndim: (0,) * n)) for p in par_ins)
        sem = ("arbitrary",)
    return pl.pallas_call(body, out_shape=out_shape, grid=(rows // tm,), in_specs=in_specs, out_specs=out_specs,
                          compiler_params=_params(sem), name=name)(*args)


def _glu(v, g):
    return (v * jax.nn.sigmoid(g),)


def _neg_expm1(z):
    return -jnp.tanh(0.5 * z) * (jnp.exp(z) + 1.0)


def _lru_gate(xc, w_r, w_i, b_r, b_i, lam):
    r = jax.nn.sigmoid(bdot(xc, w_r) + b_r)
    i = jax.nn.sigmoid(bdot(xc, w_i) + b_i)
    log_a = -LRU_C * r * jax.nn.softplus(-lam)
    a = jnp.exp(log_a)
    mult = jnp.sqrt(_neg_expm1(2.0 * log_a))
    return a, mult * (i * xc)


def _layernorm(x, g, b):
    mu = jnp.mean(x, axis=-1, keepdims=True)
    var = jnp.mean(jnp.square(x - mu), axis=-1, keepdims=True)
    return (x - mu) * lax.rsqrt(var + EPS) * g + b


def _mix_post(y_ssm, u, h1, hseq, lru_g, dgp, s5_d, w_glu, b_glu, ln_g, ln_b, w_pw, b_pw, pool_bd, pool_scale):
    y = y_ssm + s5_d * u
    gl = jax.nn.gelu(y, approximate=True)
    out_s5 = gl * jax.nn.sigmoid(bdot(gl, w_glu) + b_glu)
    out_cv = bdot(jax.nn.silu(_layernorm(h1, ln_g, ln_b)), w_pw) + b_pw
    out_lru = hseq * jax.nn.gelu(lru_g, approximate=True)
    out_pool = bdot(dgp, pool_bd) * pool_scale
    return (jnp.concatenate([out_s5, out_cv, out_lru, out_pool], axis=-1),)


def _ffn_act(gc, val):
    return (jax.nn.gelu(gc, approximate=True) * val,)


def _halo_rows(taps):
    return -(-(taps - 1) // SUBLANES) * SUBLANES


def _row_windows(ext, offsets, n, shifted_ref=None):
    if shifted_ref is None:
        return {off: ext[off:off + n] for off in offsets}
    room = ext.shape[0] - SUBLANES
    slots, out = {}, {}
    for off in offsets:
        r = off % SUBLANES
        if r == 0:
            out[off] = ext[off:off + n]
            continue
        if r not in slots:
            slots[r] = len(slots)
            shifted_ref[slots[r]] = ext[r:r + room]
        out[off] = shifted_ref[slots[r], off - r:off - r + n, :]
    return out


def _shift_scratch(taps, tm, c):
    return [pltpu.VMEM((SUBLANES - 1, _halo_rows(taps) + tm - SUBLANES, c), F32)] if taps > SUBLANES else []


def dwconv_fwd(name, x, cblk, c, w, b, taps, tm, out_dtype=F32):
    nb = w.shape[0]
    rows = x.shape[1]
    halo = _halo_rows(taps)
    per = tm // halo

    def body(x_ref, h_ref, w_ref, b_ref, o_ref, *shifted):
        i = pl.program_id(1)
        prev = jnp.where(i > 0, h_ref[...], 0.0)
        ext = jnp.concatenate([prev, x_ref[...]], axis=0)
        win = _row_windows(ext, [halo - (taps - 1) + k for k in range(taps)], tm, *shifted)
        acc = jnp.broadcast_to(b_ref[...], (tm, c))
        for k in range(taps):
            acc = acc + w_ref[k:k + 1, :] * win[halo - (taps - 1) + k]
        o_ref[...] = acc.astype(o_ref.dtype)

    return pl.pallas_call(
        body, out_shape=jax.ShapeDtypeStruct((nb, rows, c), out_dtype), grid=(nb, rows // tm),
        in_specs=[pl.BlockSpec((None, tm, c), lambda n, i: (n, i, cblk)),
                  pl.BlockSpec((None, halo, c), lambda n, i: (n, jnp.maximum(i * per - 1, 0), cblk)),
                  pl.BlockSpec((None, taps, c), lambda n, i: (n, 0, 0)),
                  pl.BlockSpec((None, 1, c), lambda n, i: (n, 0, 0))],
        out_specs=pl.BlockSpec((None, tm, c), lambda n, i: (n, i, 0)), scratch_shapes=_shift_scratch(taps, tm, c),
        compiler_params=_params(("parallel", "parallel")), name=name)(x, x, w, b)


def dwconv_bwd(name, dy, x, cblk, c, w, taps, tm, dx_dtype=F32):
    nb = w.shape[0]
    rows = x.shape[1]
    halo = _halo_rows(taps)
    per = tm // halo
    n_tiles = rows // tm
    last_halo = rows // halo - 1

    def body(dy_ref, dn_ref, x_ref, xp_ref, w_ref, dx_ref, dw_ref, db_ref, *shifted):
        i = pl.program_id(1)
        dyv = dy_ref[...]
        nxt = jnp.where(i < n_tiles - 1, dn_ref[...], 0.0)
        dext = jnp.concatenate([dyv, nxt], axis=0)
        prev = jnp.where(i > 0, xp_ref[...], 0.0)
        xext = jnp.concatenate([prev, x_ref[...]], axis=0)
        acc = jnp.zeros((tm, c), F32)

        @pl.when(i == 0)
        def _():
            dw_ref[...] = jnp.zeros_like(dw_ref)
            db_ref[...] = jnp.zeros_like(db_ref)

        dwin = _row_windows(dext, list(range(taps)), tm, *shifted[:1])
        xwin = _row_windows(xext, [halo - (taps - 1) + k for k in range(taps)], tm, *shifted[1:])
        for k in range(taps):
            acc = acc + w_ref[k:k + 1, :] * dwin[taps - 1 - k]
            dw_ref[k:k + 1, :] += jnp.sum(dyv * xwin[halo - (taps - 1) + k], axis=0, keepdims=True)
        dx_ref[...] = acc.astype(dx_ref.dtype)
        db_ref[...] += jnp.sum(dyv, axis=0, keepdims=True)

    return pl.pallas_call(
        body, out_shape=(jax.ShapeDtypeStruct((nb, rows, c), dx_dtype), jax.ShapeDtypeStruct((nb, taps, c), F32),
                         jax.ShapeDtypeStruct((nb, 1, c), F32)),
        grid=(nb, n_tiles),
        in_specs=[pl.BlockSpec((None, tm, c), lambda n, i: (n, i, 0)),
                  pl.BlockSpec((None, halo, c), lambda n, i: (n, jnp.minimum((i + 1) * per, last_halo), 0)),
                  pl.BlockSpec((None, tm, c), lambda n, i: (n, i, cblk)),
                  pl.BlockSpec((None, halo, c), lambda n, i: (n, jnp.maximum(i * per - 1, 0), cblk)),
                  pl.BlockSpec((None, taps, c), lambda n, i: (n, 0, 0))],
        out_specs=(pl.BlockSpec((None, tm, c), lambda n, i: (n, i, 0)), pl.BlockSpec((None, taps, c), lambda n, i: (n, 0, 0)),
                   pl.BlockSpec((None, 1, c), lambda n, i: (n, 0, 0))),
        scratch_shapes=2 * _shift_scratch(taps, tm, c),
        compiler_params=_params(("parallel", "arbitrary")), name=name)(dy, dy, x, x, w)


def ffn_gate_fwd(name, up, w, b, tm):
    _, rows, c = up.shape
    halo = _halo_rows(FFN_TAPS)
    per = tm // halo

    def body(g_ref, gp_ref, v_ref, w_ref, b_ref, o_ref):
        i = pl.program_id(1)
        ext = jnp.concatenate([jnp.where(i > 0, gp_ref[...], 0.0), g_ref[...]], axis=0)
        gc = jnp.broadcast_to(b_ref[...], (tm, c))
        for k in range(FFN_TAPS):
            off = halo - (FFN_TAPS - 1) + k
            gc = gc + w_ref[k:k + 1, :] * ext[off:off + tm]
        o_ref[...] = _ffn_act(gc, v_ref[...])[0].astype(BF16)

    return pl.pallas_call(
        body, out_shape=jax.ShapeDtypeStruct((2, rows, c), BF16), grid=(2, rows // tm),
        in_specs=[pl.BlockSpec((None, tm, c), lambda h, i: (h, i, 0)),
                  pl.BlockSpec((None, halo, c), lambda h, i: (h, jnp.maximum(i * per - 1, 0), 0)),
                  pl.BlockSpec((None, tm, c), lambda h, i: (h + 2, i, 0)),
                  pl.BlockSpec((None, FFN_TAPS, c), lambda h, i: (h, 0, 0)), pl.BlockSpec((None, 1, c), lambda h, i: (h, 0, 0))],
        out_specs=pl.BlockSpec((None, tm, c), lambda h, i: (h, i, 0)),
        compiler_params=_params(("parallel", "parallel")), name=name)(up, up, up, w, b)


def ffn_gate_bwd(name, up, dact, w, b, tm):
    _, rows, c = up.shape
    halo = _halo_rows(FFN_TAPS)
    per = tm // halo
    n_tiles = rows // tm
    last_halo = rows // halo - 1
    n_ext = tm + halo

    def body(g_ref, gp_ref, gn_ref, v_ref, vn_ref, d_ref, dn_ref, w_ref, b_ref, dup_ref, dw_ref, db_ref):
        i = pl.program_id(1)
        gext = jnp.concatenate([jnp.where(i > 0, gp_ref[...], 0.0), g_ref[...], gn_ref[...]], axis=0)
        shifted = [gext[halo - (FFN_TAPS - 1) + k:halo - (FFN_TAPS - 1) + k + n_ext] for k in range(FFN_TAPS)]
        gc = jnp.broadcast_to(b_ref[...], (n_ext, c))
        for k in range(FFN_TAPS):
            gc = gc + w_ref[k:k + 1, :] * shifted[k]
        vext = jnp.concatenate([v_ref[...], vn_ref[...]], axis=0)
        dext = jnp.concatenate([d_ref[...], dn_ref[...]], axis=0)
        sq = gc * gc
        t = jnp.tanh(gc * (GELU_K0 + (GELU_K0 * GELU_K1) * sq))
        half = 0.5 + 0.5 * t
        dval = dext * (gc * half)
        dgc = (dext * vext) * (half + (0.5 * gc) * (1.0 - t * t) * (GELU_K0 + (3.0 * GELU_K0 * GELU_K1) * sq))
        r = lax.broadcasted_iota(jnp.int32, (n_ext, c), 0)
        dgc = jnp.where((r < tm) | (i < n_tiles - 1), dgc, 0.0)
        dgate = jnp.zeros((tm, c), F32)
        for k in range(FFN_TAPS):
            dgate = dgate + w_ref[k:k + 1, :] * dgc[FFN_TAPS - 1 - k:FFN_TAPS - 1 - k + tm]
        dup_ref[0] = dgate.astype(BF16)
        dup_ref[1] = dval[:tm].astype(BF16)

        @pl.when(i == 0)
        def _():
            dw_ref[...] = jnp.zeros_like(dw_ref)
            db_ref[...] = jnp.zeros_like(db_ref)

        dgc_t = dgc[:tm]
        for k in range(FFN_TAPS):
            dw_ref[k:k + 1, :] += jnp.sum(dgc_t * shifted[k][:tm], axis=0, keepdims=True)
        db_ref[...] += jnp.sum(dgc_t, axis=0, keepdims=True)

    def tile(shift):
        return pl.BlockSpec((None, tm, c), lambda h, i: (h + shift, i, 0))

    def after(shift):
        return pl.BlockSpec((None, halo, c), lambda h, i: (h + shift, jnp.minimum((i + 1) * per, last_halo), 0))

    return pl.pallas_call(
        body, out_shape=(jax.ShapeDtypeStruct((2, 2, rows, c), BF16), jax.ShapeDtypeStruct((2, FFN_TAPS, c), F32),
                         jax.ShapeDtypeStruct((2, 1, c), F32)),
        grid=(2, n_tiles),
        in_specs=[tile(0), pl.BlockSpec((None, halo, c), lambda h, i: (h, jnp.maximum(i * per - 1, 0), 0)), after(0),
                  tile(2), after(2), tile(0), after(0),
                  pl.BlockSpec((None, FFN_TAPS, c), lambda h, i: (h, 0, 0)), pl.BlockSpec((None, 1, c), lambda h, i: (h, 0, 0))],
        out_specs=(pl.BlockSpec((2, None, tm, c), lambda h, i: (0, h, i, 0)), pl.BlockSpec((None, FFN_TAPS, c), lambda h, i: (h, 0, 0)),
                   pl.BlockSpec((None, 1, c), lambda h, i: (h, 0, 0))),
        compiler_params=_params(("parallel", "arbitrary")), name=name)(up, up, up, up, up, dact, dact, w, b)


POOL_HALO = 16


def pool_fwd(name, proj, cblk, tm):
    rows = proj.shape[0]
    c = 128 * len(POOL_WINDOWS)
    per = tm // POOL_HALO

    def body(x_ref, h_ref, o_ref):
        i = pl.program_id(0)
        xv = x_ref[...]
        ext = jnp.concatenate([jnp.where(i > 0, h_ref[...], 0.0), xv], axis=0)
        t1 = (lax.broadcasted_iota(jnp.int32, (tm, 128), 0) + i * tm + 1).astype(F32)
        outs = []
        for gi, win in enumerate(POOL_WINDOWS):
            seg = ext[:, gi * 128:(gi + 1) * 128]
            s = seg[POOL_HALO:POOL_HALO + tm]
            for j in range(1, win):
                s = s + seg[POOL_HALO - j:POOL_HALO - j + tm]
            outs.append(s / jnp.minimum(t1, float(win)) - xv[:, gi * 128:(gi + 1) * 128])
        o_ref[...] = jnp.concatenate(outs, axis=-1)

    return pl.pallas_call(
        body, out_shape=jax.ShapeDtypeStruct((rows, c), F32), grid=(rows // tm,),
        in_specs=[pl.BlockSpec((tm, c), lambda i: (i, cblk)),
                  pl.BlockSpec((POOL_HALO, c), lambda i: (jnp.maximum(i * per - 1, 0), cblk))],
        out_specs=pl.BlockSpec((tm, c), lambda i: (i, 0)), compiler_params=_params(("parallel",)), name=name)(proj, proj)


def pool_bwd(name, dd, tm):
    rows, c = dd.shape
    per = tm // POOL_HALO
    n_tiles = rows // tm
    last_halo = rows // POOL_HALO - 1

    def body(d_ref, n_ref, o_ref):
        i = pl.program_id(0)
        dv = d_ref[...]
        nxt = jnp.where(i < n_tiles - 1, n_ref[...], 0.0)
        t1 = (lax.broadcasted_iota(jnp.int32, (tm, 128), 0) + i * tm + 1).astype(F32)
        t1n = (lax.broadcasted_iota(jnp.int32, (POOL_HALO, 128), 0) + (i + 1) * tm + 1).astype(F32)
        outs = []
        for gi, win in enumerate(POOL_WINDOWS):
            sl = slice(gi * 128, (gi + 1) * 128)
            q = jnp.concatenate([dv[:, sl] / jnp.minimum(t1, float(win)), nxt[:, sl] / jnp.minimum(t1n, float(win))], axis=0)
            s = q[0:tm]
            for j in range(1, win):
                s = s + q[j:j + tm]
            outs.append(s - dv[:, sl])
        o_ref[...] = jnp.concatenate(outs, axis=-1)

    return pl.pallas_call(
        body, out_shape=jax.ShapeDtypeStruct((rows, c), F32), grid=(n_tiles,),
        in_specs=[pl.BlockSpec((tm, c), lambda i: (i, 0)),
                  pl.BlockSpec((POOL_HALO, c), lambda i: (jnp.minimum((i + 1) * per, last_halo), 0))],
        out_specs=pl.BlockSpec((tm, c), lambda i: (i, 0)), compiler_params=_params(("parallel",)), name=name)(dd, dd)


BLOCK_STEPS = 3


def _shift_down(v, s, fill):
    r = lax.broadcasted_iota(jnp.int32, v.shape, 0)
    return jnp.where(r >= s, pltpu.roll(v, s, 0), fill)


def _shift_up(v, s, fill):
    n = v.shape[0]
    r = lax.broadcasted_iota(jnp.int32, v.shape, 0)
    return jnp.where(r < n - s, pltpu.roll(v, n - s, 0), fill)


def _shift_in_blocks(v, s, fill, reverse):
    n = v.shape[0]
    q = lax.broadcasted_iota(jnp.int32, v.shape, 0) & (SUBLANES - 1)
    if reverse:
        return jnp.where(q < SUBLANES - s, pltpu.roll(v, n - s, 0), fill)
    return jnp.where(q >= s, pltpu.roll(v, s, 0), fill)


def _cscan_blocks(vr, vi, powers, reverse):
    for k, (qr, qi) in enumerate(powers):
        s = 1 << k
        sr, si = _shift_in_blocks(vr, s, 0.0, reverse), _shift_in_blocks(vi, s, 0.0, reverse)
        if reverse:
            vr, vi = vr + qr * sr + qi * si, vi + qr * si - qi * sr
        else:
            vr, vi = vr + qr * sr - qi * si, vi + qr * si + qi * sr
    return vr, vi


def _cscan_table(pr, pi, powers, reverse):
    r = lax.broadcasted_iota(jnp.int32, (SUBLANES, 128), 0)
    at = (SUBLANES - 1) if reverse else 0
    return _cscan_blocks(jnp.where(r == at, pr, 0.0), jnp.where(r == at, -pi if reverse else pi, 0.0), powers, reverse)


def _cscan_chunk(vr, vi, powers, table, carry, reverse):
    vr, vi = _cscan_blocks(vr, vi, powers, reverse)
    tr, ti = table
    cr, ci = carry
    nb = vr.shape[0] // SUBLANES
    outr, outi = [None] * nb, [None] * nb
    edge = 0 if reverse else SUBLANES - 1
    for j in (reversed(range(nb)) if reverse else range(nb)):
        rows = slice(j * SUBLANES, (j + 1) * SUBLANES)
        zr = vr[rows] + tr * cr - ti * ci
        zi = vi[rows] + tr * ci + ti * cr
        outr[j], outi[j] = zr, zi
        cr, ci = zr[edge:edge + 1], zi[edge:edge + 1]
    return jnp.concatenate(outr, axis=0), jnp.concatenate(outi, axis=0), (cr, ci)


def _powers(pr, pi, n):
    out = [(pr, pi)]
    for _ in range(n - 1):
        pr, pi = pr * pr - pi * pi, 2.0 * pr * pi
        out.append((pr, pi))
    return out


def s5_scan_fwd(name, bu, a):
    _, rows, n = bu.shape
    t = min(SCAN_CHUNK, rows)

    def body(bu_ref, a_ref, z_ref):
        pr, pi = a_ref[0], a_ref[1]
        powers = _powers(pr, pi, BLOCK_STEPS)
        table = _cscan_table(pr, pi, powers, False)

        def chunk(ci, carry):
            base = pl.multiple_of(ci * t, t)
            zr, zi, carry = _cscan_chunk(bu_ref[0, pl.ds(base, t), :], bu_ref[1, pl.ds(base, t), :], powers, table, carry, False)
            z_ref[0, pl.ds(base, t), :] = zr
            z_ref[1, pl.ds(base, t), :] = zi
            return carry

        zero = jnp.zeros((1, 128), F32)
        lax.fori_loop(0, rows // t, chunk, (zero, zero))

    return pl.pallas_call(
        body, out_shape=jax.ShapeDtypeStruct((2, rows, n), F32), grid=(n // 128,),
        in_specs=[pl.BlockSpec((2, rows, 128), lambda j: (0, 0, j)), pl.BlockSpec((2, 1, 128), lambda j: (0, 0, j))],
        out_specs=pl.BlockSpec((2, rows, 128), lambda j: (0, 0, j)), compiler_params=_params(("parallel",)), name=name)(bu, a)


def s5_scan_bwd(name, dz, z, a):
    _, rows, n = dz.shape
    t = min(SCAN_CHUNK, rows)
    n_chunks = rows // t

    def body(dz_ref, z_ref, a_ref, lam_ref, da_ref):
        pr, pi = a_ref[0], a_ref[1]
        powers = _powers(pr, pi, BLOCK_STEPS)
        table = _cscan_table(pr, pi, powers, True)

        def chunk(k, carry):
            ci = n_chunks - 1 - k
            base = pl.multiple_of(ci * t, t)
            cr, cim, dar, dai = carry
            lr, li, (cr, cim) = _cscan_chunk(dz_ref[0, pl.ds(base, t), :], dz_ref[1, pl.ds(base, t), :], powers, table, (cr, cim), True)
            lam_ref[0, pl.ds(base, t), :] = lr
            lam_ref[1, pl.ds(base, t), :] = li
            pbase = pl.multiple_of(jnp.maximum(base - SUBLANES, 0), SUBLANES)
            keep = (ci > 0).astype(F32)
            pzr = z_ref[0, pl.ds(pbase, SUBLANES), :][SUBLANES - 1:SUBLANES, :] * keep
            pzi = z_ref[1, pl.ds(pbase, SUBLANES), :][SUBLANES - 1:SUBLANES, :] * keep
            zpr = _shift_down(z_ref[0, pl.ds(base, t), :], 1, pzr)
            zpi = _shift_down(z_ref[1, pl.ds(base, t), :], 1, pzi)
            dar = dar + jnp.sum(lr * zpr + li * zpi, axis=0, keepdims=True)
            dai = dai + jnp.sum(li * zpr - lr * zpi, axis=0, keepdims=True)
            return cr, cim, dar, dai

        zero = jnp.zeros((1, 128), F32)
        _, _, dar, dai = lax.fori_loop(0, n_chunks, chunk, (zero, zero, zero, zero))
        da_ref[0] = dar
        da_ref[1] = dai

    seq = pl.BlockSpec((2, rows, 128), lambda j: (0, 0, j))
    vec = pl.BlockSpec((2, 1, 128), lambda j: (0, 0, j))
    return pl.pallas_call(
        body, out_shape=(jax.ShapeDtypeStruct((2, rows, n), F32), jax.ShapeDtypeStruct((2, 1, n), F32)), grid=(n // 128,),
        in_specs=[seq, seq, vec], out_specs=(seq, vec), compiler_params=_params(("parallel",)), name=name)(dz, z, a)


def _rscan_chunk(a, b, carry, reverse):
    n = a.shape[0]
    shift = _shift_up if reverse else _shift_down
    for k in range(n.bit_length() - 1):
        s = 1 << k
        b = b + a * shift(b, s, 0.0)
        a = a * shift(a, s, 1.0)
    h = b + a * carry
    edge = 0 if reverse else n - 1
    return h, h[edge:edge + 1]


def lru_scan_fwd(name, a, b):
    rows, n = a.shape
    t = min(SCAN_CHUNK, rows)

    def body(a_ref, b_ref, h_ref):
        def chunk(ci, carry):
            base = pl.multiple_of(ci * t, t)
            h, carry = _rscan_chunk(a_ref[pl.ds(base, t), :], b_ref[pl.ds(base, t), :], carry, False)
            h_ref[pl.ds(base, t), :] = h
            return carry

        lax.fori_loop(0, rows // t, chunk, jnp.zeros((1, 128), F32))

    seq = pl.BlockSpec((rows, 128), lambda j: (0, j))
    return pl.pallas_call(body, out_shape=jax.ShapeDtypeStruct((rows, n), F32), grid=(n // 128,), in_specs=[seq, seq],
                          out_specs=seq, compiler_params=_params(("parallel",)), name=name)(a, b)


def lru_scan_bwd(name, dh, a, h):
    rows, n = a.shape
    t = min(SCAN_CHUNK, rows)
    n_chunks = rows // t

    def body(dh_ref, a_ref, h_ref, da_ref, db_ref):
        def chunk(k, carry):
            ci = n_chunks - 1 - k
            base = pl.multiple_of(ci * t, t)
            nbase = pl.multiple_of(jnp.minimum(base + t, rows - SUBLANES), SUBLANES)
            a_next = a_ref[pl.ds(nbase, SUBLANES), :][0:1, :]
            an = _shift_up(a_ref[pl.ds(base, t), :], 1, a_next)
            mu, carry = _rscan_chunk(an, dh_ref[pl.ds(base, t), :], carry, True)
            pbase = pl.multiple_of(jnp.maximum(base - SUBLANES, 0), SUBLANES)
            hp_row = h_ref[pl.ds(pbase, SUBLANES), :][SUBLANES - 1:SUBLANES, :] * (ci > 0).astype(F32)
            hp = _shift_down(h_ref[pl.ds(base, t), :], 1, hp_row)
            da_ref[pl.ds(base, t), :] = mu * hp
            db_ref[pl.ds(base, t), :] = mu
            return carry

        lax.fori_loop(0, n_chunks, chunk, jnp.zeros((1, 128), F32))

    seq = pl.BlockSpec((rows, 128), lambda j: (0, j))
    return pl.pallas_call(
        body, out_shape=(jax.ShapeDtypeStruct((rows, n), F32), jax.ShapeDtypeStruct((rows, n), F32)), grid=(n // 128,),
        in_specs=[seq, seq, seq], out_specs=(seq, seq), compiler_params=_params(("parallel",)), name=name)(dh, a, h)


def _s5_param(lr, li, ls, bre, bim):
    st = jnp.exp(ls)
    er = jnp.exp(lr * st)
    th = li * st
    ar, ai = er * jnp.cos(th), er * jnp.sin(th)
    nr, ni = ar - 1.0, ai
    den = lr * lr + li * li
    cr, ci = (nr * lr + ni * li) / den, (ni * lr - nr * li) / den
    return ar, ai, cr * bre - ci * bim, cr * bim + ci * bre


def s5_param_fwd(name, lr, li, ls, bre, bim):
    gh, n = bre.shape

    def body(lr_ref, li_ref, ls_ref, bre_ref, bim_ref, a_ref, bb_ref):
        ar, ai, br, bi = _s5_param(lr_ref[...], li_ref[...], ls_ref[...], bre_ref[...], bim_ref[...])
        a_ref[0] = ar
        a_ref[1] = ai
        bb_ref[0] = br.astype(BF16)
        bb_ref[1] = bi.astype(BF16)

    return pl.pallas_call(body, out_shape=(jax.ShapeDtypeStruct((2, 1, n), F32), jax.ShapeDtypeStruct((2, gh, n), BF16)),
                          compiler_params=_params(), name=name)(lr, li, ls, bre, bim)


def s5_param_bwd(name, lr, li, ls, bre, bim, da, dbb, gsum):
    gh, n = bre.shape

    def body(lr_ref, li_ref, ls_ref, bre_ref, bim_ref, da_ref, dbb_ref, gs_ref, dlr_ref, dli_ref, dls_ref, dbre_ref, dbim_ref):
        _, vjp = jax.vjp(_s5_param, lr_ref[...], li_ref[...], ls_ref[...], bre_ref[...], bim_ref[...])
        dlr, dli, dls, dbre, dbim = vjp((da_ref[0], da_ref[1], dbb_ref[0], dbb_ref[1]))
        dlr_ref[...] = dlr
        dli_ref[...] = dli
        dls_ref[...] = jnp.dot(jnp.broadcast_to(dls, (SUBLANES, n)), gs_ref[...], preferred_element_type=F32,
                               precision=lax.Precision.HIGHEST)
        dbre_ref[...] = dbre
        dbim_ref[...] = dbim

    vec = jax.ShapeDtypeStruct((1, n), F32)
    mat = jax.ShapeDtypeStruct((gh, n), F32)
    return pl.pallas_call(body, out_shape=(vec, vec, jax.ShapeDtypeStruct((SUBLANES, 128), F32), mat, mat),
                          compiler_params=_params(), name=name)(lr, li, ls, bre, bim, da, dbb, gsum)


def sum_lead(name, x, tr):
    n, rows, cols = x.shape

    def body(x_ref, o_ref):
        acc = x_ref[0]
        for j in range(1, n):
            acc = acc + x_ref[j]
        o_ref[...] = acc

    return pl.pallas_call(
        body, out_shape=jax.ShapeDtypeStruct((rows, cols), x.dtype), grid=(rows // tr,),
        in_specs=[pl.BlockSpec((n, tr, cols), lambda i: (0, i, 0))], out_specs=pl.BlockSpec((tr, cols), lambda i: (i, 0)),
        compiler_params=_params(("parallel",)), name=name)(x)


def _adamw(w, g, m, v):
    m = ADAM_B1 * m + (1.0 - ADAM_B1) * g
    v = ADAM_B2 * v + (1.0 - ADAM_B2) * jnp.square(g)
    m_hat = m / (1.0 - ADAM_B1 ** ADAM_STEP)
    v_hat = v / (1.0 - ADAM_B2 ** ADAM_STEP)
    delta = -ADAM_LR * (m_hat / (jnp.sqrt(v_hat) + ADAM_EPS) + ADAM_WD * w)
    return delta, m, v


def adamw_sharded(name, w, m, v, g0, g1, split_cols, tile):
    _, r, c = w.shape
    if split_cols:
        nt = c // tile
        per = (c // 2) // tile
        wspec = pl.BlockSpec((None, r, tile), lambda l, t: (l, 0, t))
        gspec = pl.BlockSpec((None, r, tile), lambda l, t: (t // per, 0, t % per))
    else:
        nt = r // tile
        per = (r // 2) // tile
        wspec = pl.BlockSpec((None, tile, c), lambda l, t: (l, t, 0))
        gspec = pl.BlockSpec((None, tile, c), lambda l, t: (t // per, t % per, 0))

    def body(w_ref, m_ref, v_ref, g0_ref, g1_ref, g_ref, d_ref, nm_ref, nv_ref):
        g = jnp.where(pl.program_id(0) == 0, g0_ref[...], g1_ref[...])
        d, nm, nv = _adamw(w_ref[...], g, m_ref[...], v_ref[...])
        g_ref[...] = g
        d_ref[...] = d
        nm_ref[...] = nm
        nv_ref[...] = nv

    sds = jax.ShapeDtypeStruct(w.shape, F32)
    return pl.pallas_call(body, out_shape=(sds,) * 4, grid=(2, nt), in_specs=[wspec, wspec, wspec, gspec, gspec],
                          out_specs=(wspec,) * 4, compiler_params=_params(("parallel", "parallel")), name=name)(w, m, v, g0, g1)


def adamw_flat(name, w, g, m, v, tr):
    rows, cols = w.shape

    def body(w_ref, g_ref, m_ref, v_ref, d_ref, nm_ref, nv_ref):
        d, nm, nv = _adamw(w_ref[...], g_ref[...], m_ref[...], v_ref[...])
        d_ref[...] = d
        nm_ref[...] = nm
        nv_ref[...] = nv

    blk = pl.BlockSpec((tr, cols), lambda i: (i, 0))
    sds = jax.ShapeDtypeStruct((rows, cols), F32)
    return pl.pallas_call(body, out_shape=(sds,) * 3, grid=(rows // tr,), in_specs=[blk] * 4, out_specs=(blk,) * 3,
                          compiler_params=_params(("parallel",)), name=name)(w, g, m, v)


def _flips(axes):
    out = []
    for fx in ((0, 1) if "x" in axes else (0,)):
        for fy in ((0, 1) if "y" in axes else (0,)):
            for fc in ((0, 1) if "c" in axes else (0,)):
                if fx or fy or fc:
                    out.append((fx, fy, fc))
    return out


def _slot(pos, axes):
    s = 0
    for name, p in zip(("x", "y", "c"), pos):
        if name in axes:
            s = 2 * s + p
    return s


_HBM = pl.BlockSpec(memory_space=pltpu.HBM)
_SEM = pl.BlockSpec(memory_space=pltpu.SEMAPHORE)
_EFFECT = pltpu.SideEffectType.DATAFLOW_SIDE_EFFECTING


def place_own(name, arrs, axes):
    n = len(_flips(axes)) + 1
    na = len(arrs)

    def body(*refs):
        ins, outs, sems = refs[:na], refs[na:2 * na], refs[2 * na]
        my = _slot((lax.axis_index("x"), lax.axis_index("y"), lax.axis_index("c")), axes)
        copies = [pltpu.make_async_copy(ins[a], outs[a].at[my], sems.at[a]) for a in range(na)]
        for cp in copies:
            cp.start()
        for cp in copies:
            cp.wait()

    out_shape = tuple(jax.ShapeDtypeStruct((n,) + a.shape, a.dtype) for a in arrs)
    anyspec = pl.BlockSpec(memory_space=pl.ANY)
    return pl.pallas_call(body, out_shape=out_shape, in_specs=[anyspec] * na, out_specs=(anyspec,) * na,
                          scratch_shapes=[pltpu.SemaphoreType.DMA((na,))], name=name)(*arrs)


def _peers(axes):
    me = (lax.axis_index("x"), lax.axis_index("y"), lax.axis_index("c"))
    return me, [tuple((1 - p) if f else p for p, f in zip(me, fl)) for fl in _flips(axes)]


def place_tile(name, arr, layer, my, slots=4, dtype=BF16, after=None):
    _, r, cols = arr.shape
    tr = _tile_rows(r, cols)

    def body(my_ref, x_ref, *rest):
        rest[-1][...] = x_ref[...].astype(dtype)

    in_specs = [pl.BlockSpec((None, tr, cols), lambda i, my: (layer, i, 0))]
    args = [arr]
    if after is not None:
        in_specs.append(pl.BlockSpec(after.shape, lambda i, my: (0, 0)))
        args.append(after)
    grid_spec = pltpu.PrefetchScalarGridSpec(num_scalar_prefetch=1, grid=(r // tr,), in_specs=in_specs,
                                             out_specs=pl.BlockSpec((None, tr, cols), lambda i, my: (my[0], i, 0)))
    return pl.pallas_call(body, out_shape=jax.ShapeDtypeStruct((slots, r, cols), dtype), grid_spec=grid_spec,
                          compiler_params=_params(("parallel",)), name=name)(my, *args)


def exchange_start(name, groups, axes, scatter):
    flat = [(p if scatter else (p,)) for grp in groups for p in grp]
    per = 2 if scatter else 1
    na, ng, npeer = len(flat), len(groups), len(_flips(axes))

    def body(*refs):
        ops = refs[:per * na]
        zones = ops[(per - 1) * na:]
        sems, token = refs[per * na:per * na + 2 * ng], refs[-1]
        me, peers = _peers(axes)
        my = _slot(me, axes)
        ai = 0
        for g, grp in enumerate(groups):
            for k in range(len(grp)):
                for j, peer in enumerate(peers):
                    src = ops[ai].at[_slot(peer, axes)] if scatter else zones[ai].at[my]
                    dst = zones[ai].at[j] if scatter else zones[ai].at[my]
                    pltpu.make_async_remote_copy(
                        src_ref=src, dst_ref=dst, send_sem=sems[2 * g].at[k * npeer + j],
                        recv_sem=sems[2 * g + 1].at[k * npeer + j], device_id=peer, device_id_type=pl.DeviceIdType.MESH).start()
                ai += 1
        token[...] = jnp.zeros_like(token)

    out_shape, out_specs = [], []
    for grp in groups:
        out_shape += [pltpu.SemaphoreType.DMA((npeer * len(grp),))] * 2
        out_specs += [_SEM, _SEM]
    for idx in range(per):
        out_shape += [pltpu.HBM(p[idx].shape, p[idx].dtype) for p in flat]
        out_specs += [_HBM] * na
    out_shape.append(jax.ShapeDtypeStruct((SUBLANES, 128), F32))
    out_specs.append(pl.BlockSpec(memory_space=pltpu.VMEM))
    args = [pltpu.with_memory_space_constraint(p[idx], pltpu.HBM) for idx in range(per) for p in flat]
    res = pl.pallas_call(body, out_shape=tuple(out_shape), in_specs=[_HBM] * (per * na), out_specs=tuple(out_specs),
                         input_output_aliases={i: 2 * ng + i for i in range(per * na)},
                         compiler_params=pltpu.CompilerParams(has_side_effects=_EFFECT), name=name)(*args)
    thru = res[2 * ng:2 * ng + per * na]
    out, ai = [], 0
    for g, grp in enumerate(groups):
        srcs = list(thru[ai:ai + len(grp)]) if scatter else []
        zones = list(thru[(per - 1) * na + ai:(per - 1) * na + ai + len(grp)])
        out.append(((res[2 * g], res[2 * g + 1]), srcs, zones))
        ai += len(grp)
    return out, res[-1]


def exchange_wait(name, group, after, axes, scatter):
    (send_sems, recv_sems), srcs, zones = group
    n, ns = len(zones), len(srcs)
    npeer = len(_flips(axes))

    def body(*refs):
        z_refs = refs[ns:ns + n]
        ssem, rsem = refs[ns + n], refs[ns + n + 1]
        _, peers = _peers(axes)
        for k in range(n):
            for j, peer in enumerate(peers):
                part = z_refs[k].at[j if scatter else _slot(peer, axes)]
                copy = pltpu.make_async_remote_copy(
                    src_ref=part, dst_ref=part, send_sem=ssem.at[k * npeer + j], recv_sem=rsem.at[k * npeer + j],
                    device_id=peer, device_id_type=pl.DeviceIdType.MESH)
                copy.wait_send()
                copy.wait_recv()

    ops = list(srcs) + list(zones)
    out_shape = tuple(pltpu.HBM(a.shape, a.dtype) for a in ops)
    res = pl.pallas_call(body, out_shape=out_shape, in_specs=[_HBM] * len(ops) + [_SEM, _SEM, pl.BlockSpec(memory_space=pl.ANY)],
                         out_specs=(_HBM,) * len(ops), input_output_aliases={i: i for i in range(len(ops))},
                         compiler_params=pltpu.CompilerParams(has_side_effects=_EFFECT), name=name)(*ops, send_sems, recv_sems, after)
    return list(res[:ns]), list(res[ns:])


def _pair_exchange(name, ins, in_specs, n_steps, tile, fn_send, fn_out, out_shape, out_spec, prefetch=None, wire=F32):
    n_in = len(ins)

    def body(*refs):
        if prefetch is not None:
            refs = refs[1:]
        in_refs, o_ref = refs[:n_in], refs[n_in]
        send_buf, recv_buf, send_sems, recv_sems, credit = refs[n_in + 1:]
        i = pl.program_id(0)
        slot = lax.rem(i, 2)
        c = lax.axis_index("c")
        sibling = (lax.axis_index("x"), lax.axis_index("y"), 1 - c)
        vals = [r[...] for r in in_refs]
        send_buf[slot] = fn_send(*vals, c).astype(wire)

        @pl.when(i >= 2)
        def _():
            pl.semaphore_wait(credit, 1)

        copy = pltpu.make_async_remote_copy(
            src_ref=send_buf.at[slot], dst_ref=recv_buf.at[slot], send_sem=send_sems.at[slot], recv_sem=recv_sems.at[slot],
            device_id=sibling, device_id_type=pl.DeviceIdType.MESH)
        copy.start()
        copy.wait_recv()
        o_ref[...] = fn_out(*vals, recv_buf[slot], c).astype(o_ref.dtype)
        copy.wait_send()

        @pl.when(i < n_steps - 2)
        def _():
            pl.semaphore_signal(credit, inc=1, device_id=sibling, device_id_type=pl.DeviceIdType.MESH)

    scratch = [pltpu.VMEM((2,) + tile, wire), pltpu.VMEM((2,) + tile, wire), pltpu.SemaphoreType.DMA((2,)),
               pltpu.SemaphoreType.DMA((2,)), pltpu.SemaphoreType.REGULAR]
    if prefetch is None:
        return pl.pallas_call(body, out_shape=out_shape, grid=(n_steps,), in_specs=in_specs, out_specs=out_spec,
                              scratch_shapes=scratch, compiler_params=_params(("arbitrary",)), name=name)(*ins)
    grid_spec = pltpu.PrefetchScalarGridSpec(num_scalar_prefetch=1, grid=(n_steps,), in_specs=in_specs, out_specs=out_spec,
                                             scratch_shapes=scratch)
    return pl.pallas_call(body, out_shape=out_shape, grid_spec=grid_spec, compiler_params=_params(("arbitrary",)),
                          name=name)(prefetch, *ins)


def _tile_rows(rows, cols, f32_bytes=3 << 19):
    return _row_tile(rows, max(2 * SUBLANES, f32_bytes // (4 * cols)), 2 * SUBLANES)


def pair_sum(name, x):
    rows, cols = x.shape
    tr = _tile_rows(rows, cols)
    return _pair_exchange(name, [x], [pl.BlockSpec((tr, cols), lambda i: (i, 0))], rows // tr, (tr, cols),
                          lambda v, c: v, lambda v, got, c: v + got, jax.ShapeDtypeStruct((rows, cols), F32),
                          pl.BlockSpec((tr, cols), lambda i: (i, 0)))


def reduce_cores(name, g):
    _, m, cols = g.shape
    tr = _tile_rows(m, cols, 3 << 20)

    def fn_send(g0, g1, c):
        return jnp.where(c == 0, g1, g0)

    def fn_out(g0, g1, got, c):
        return jnp.where(c == 0, g0, g1) + got.astype(F32)

    return _pair_exchange(
        name, [g, g], [pl.BlockSpec((None, tr, cols), lambda i: (0, i, 0)), pl.BlockSpec((None, tr, cols), lambda i: (1, i, 0))],
        m // tr, (tr, cols), fn_send, fn_out, jax.ShapeDtypeStruct((m, cols), BF16), pl.BlockSpec((tr, cols), lambda i: (i, 0)),
        wire=BF16)


def sum_and_share(name, own, parts, my):
    n, r, cols = parts.shape
    tr = _tile_rows(r, cols, 3 << 20)

    def total(o, p):
        acc = o.astype(F32)
        for j in range(n):
            acc = acc + p[j].astype(F32)
        return acc

    def fn_send(o, p, c):
        return total(o, p)

    def fn_out(o, p, got, c):
        mine = total(o, p)
        return jnp.stack([jnp.where(c == 0, mine, got), jnp.where(c == 0, got, mine)])

    return _pair_exchange(
        name, [own, parts], [pl.BlockSpec((None, tr, cols), lambda i, my_ref: (my_ref[0], i, 0)), pl.BlockSpec((n, tr, cols), lambda i, my_ref: (0, i, 0))],
        r // tr, (tr, cols), fn_send, fn_out, jax.ShapeDtypeStruct((2, r, cols), F32),
        pl.BlockSpec((2, tr, cols), lambda i, my_ref: (0, i, 0)), prefetch=my)


def _block_diag(blocks):
    g, r, c = blocks.shape
    eye = jnp.eye(g, dtype=blocks.dtype)
    return (blocks[:, :, None, :] * eye[:, None, :, None]).reshape(g * r, g * c)


def _diag_blocks(mat, g):
    r, c = mat.shape[0] // g, mat.shape[1] // g
    eye = jnp.eye(g, dtype=mat.dtype)
    return (mat.reshape(g, r, g, c) * eye[:, None, :, None]).sum(axis=2)


def _halves(gfull, shards):
    rows, cols = gfull.shape
    return gfull.reshape(shards, 2, rows // shards // 2, cols).transpose(1, 0, 2, 3)


def _step(inp):
    x = inp['x'][0]
    target = inp['loss_target'][0]
    rows, d = x.shape
    depth = inp['w_in'].shape[0]
    mix_w = d // 4
    n_state = S5_GROUPS * S5_STATE
    ffn_half = inp['ffn_w_up'].shape[2]
    tm = min(512, rows)
    tc = min(256, rows)
    xy = ("x", "y")

    my_chip = (2 * lax.axis_index("x") + lax.axis_index("y")).astype(jnp.int32).reshape(1)
    small_keys = [(nme, None) for nme in SMALL_SHARDED]
    group_keys = []
    for l in range(depth):
        group_keys += [[('w_in', l)] + (small_keys if l == 0 else []),
                       [('w_out', l), ('s5_w_glu', l), ('cv_w_pw', l)], [('ffn_w_up', l)], [('ffn_w_down', l)]]

    def zone_of(key, after=None):
        nme, l = key
        src = jnp.swapaxes(inp[nme], 1, 2) if nme == 'ffn_w_up' else inp[nme]
        return place_tile(f"place_{nme}{l}", src, l, my_chip, after=after)

    zones = {('w_in', 0): zone_of(('w_in', 0))}
    zones.update(zip(small_keys, place_own("place_small", [inp[nme] for nme in SMALL_SHARDED], xy)))
    first_group, first_token = exchange_start("gather_start_first", [[zones[key] for key in group_keys[0]]], xy, False)
    for grp in group_keys[1:]:
        zones.update({key: zone_of(key, first_token) for key in grp})
    rest_groups, gather_token = exchange_start("gather_start", [[zones[key] for key in grp] for grp in group_keys[1:]], xy, False)
    gather_groups = first_group + rest_groups

    def gathered(gi, after):
        return dict(zip(group_keys[gi], exchange_wait(f"gather_wait{gi}", gather_groups[gi], after, xy, False)[1]))

    def full_small(g):
        return g.transpose(1, 2, 0, 3).reshape(g.shape[1], g.shape[2], 4 * g.shape[3])

    gsum = jnp.repeat(jnp.eye(128, dtype=F32)[:S5_GROUPS], S5_STATE, axis=0)

    saved = []
    grads = {nme: [None] * depth for nme in WEIGHTS}
    xcur = x
    for l in range(depth):
        vec = lambda a: a[l].reshape(1, -1)
        gain = vec(inp['norm_mix_g']) + (gather_token[0, 0] if l == 0 else 0.0)
        h = rms_fwd(f"rms_mix{l}", xcur, gain, tm)
        got = gathered(4 * l, h)
        w_in = got[('w_in', l)]
        if l == 0:
            cv_w_dw, lru_w_conv, ffn_w_dw = (full_small(got[(nme, None)]) for nme in ('cv_w_dw', 'lru_w_conv', 'ffn_w_dw'))
        ncol = w_in.shape[2]

        lam_re, lam_im = vec(inp['s5_lam_re']), vec(inp['s5_lam_im'])
        log_step = jnp.broadcast_to(inp['s5_log_step'][l][:, None], (S5_GROUPS, S5_STATE)).reshape(1, n_state)
        b_re = _block_diag(inp['s5_b_re'][l].transpose(0, 2, 1))
        b_im = _block_diag(inp['s5_b_im'][l].transpose(0, 2, 1))
        c_cat = jnp.stack([_block_diag(inp['s5_c_re'][l].transpose(0, 2, 1)),
                           -_block_diag(inp['s5_c_im'][l].transpose(0, 2, 1))]).astype(BF16)
        a_bar, b_bar = s5_param_fwd(f"s5_param_fwd{l}", lam_re, lam_im, log_step, b_re, b_im)
        w_r = _block_diag(inp['lru_w_r'][l]).astype(BF16)
        w_i = _block_diag(inp['lru_w_i'][l]).astype(BF16)
        pool_bd = _block_diag(inp['pool_w'][l]).astype(BF16)
        gate_pars = [w_r, w_i, vec(inp['lru_b_r']), vec(inp['lru_b_i']), vec(inp['lru_lam'])]

        proj = _mm(f"proj{l}", h, w_in, jax.ShapeDtypeStruct((rows, 4 * ncol), F32), (4, rows // tm),
                   pl.BlockSpec((tm, d), lambda j, i: (i, 0)), pl.BlockSpec((None, d, ncol), lambda j, i: (j, 0, 0)),
                   pl.BlockSpec((tm, ncol), lambda j, i: (i, j)), NN)
        proj3 = proj.reshape(1, rows, 4 * ncol)
        ts = min(2048, rows)
        cw, sw = mix_w // 4, n_state // 4
        bu = _mm(f"s5_bu{l}", proj, b_bar, jax.ShapeDtypeStruct((2, rows, n_state), F32), (rows // ts, 2, 4),
                 pl.BlockSpec((ts, cw), lambda i, c, s: (i, s)), pl.BlockSpec((None, cw, sw), lambda i, c, s: (c, s, s)),
                 pl.BlockSpec((None, ts, sw), lambda i, c, s: (c, i, s)), NN)
        z = s5_scan_fwd(f"s5_scan{l}", bu, a_bar)
        y_ssm = _mm(f"s5_read{l}", z, c_cat, jax.ShapeDtypeStruct((rows, mix_w), F32), (rows // ts, 4, 2),
                    pl.BlockSpec((None, ts, sw), lambda i, s, c: (c, i, s)), pl.BlockSpec((None, sw, cw), lambda i, s, c: (c, s, s)),
                    pl.BlockSpec((ts, cw), lambda i, s, c: (i, s)), NN, k_axis=2)
        (h0,) = _rowwise(f"cv_glu{l}", _glu, [(proj, 1, mix_w), (proj, 2, mix_w)], [], 1, [(mix_w, F32)], tm)
        h1 = dwconv_fwd(f"cv_conv{l}", h0.reshape(1, rows, mix_w), 0, mix_w, cv_w_dw[l][None], vec(inp['cv_b_dw'])[None],
                        CV_TAPS, tc)[0]
        xc = dwconv_fwd(f"lru_conv{l}", proj3, 3, mix_w, lru_w_conv[l][None], vec(inp['lru_b_conv'])[None], LRU_TAPS, tc)[0]
        a_t, b_t = _rowwise(f"lru_gate{l}", _lru_gate, [(xc, 0, mix_w)], gate_pars, 2, [(mix_w, F32), (mix_w, F32)], tm)
        hseq = lru_scan_fwd(f"lru_scan{l}", a_t, b_t)
        dgp = pool_fwd(f"pool{l}", proj, 5, tc)
        got = gathered(4 * l + 1, proj)
        w_out = got[('w_out', l)].reshape(d, d)
        w_glu, w_pw = got[('s5_w_glu', l)].reshape(mix_w, mix_w), got[('cv_w_pw', l)].reshape(mix_w, mix_w)
        post_pars = [vec(inp['s5_d']), w_glu, vec(inp['s5_b_glu']), vec(inp['cv_ln_g']), vec(inp['cv_ln_b']), w_pw,
                     vec(inp['cv_b_pw']), pool_bd, vec(inp['pool_scale'])]
        post_rows = [(y_ssm, 0, mix_w), (proj, 0, mix_w), (h1, 0, mix_w), (hseq, 0, mix_w), (proj, 4, mix_w), (dgp, 0, mix_w)]
        (mixed,) = _rowwise(f"mix_post{l}", _mix_post, post_rows, post_pars, 1, [(d, BF16)], tm)
        x1 = _mm(f"out_proj{l}", mixed, w_out, jax.ShapeDtypeStruct((rows, d), F32), (2, rows // tm),
                 pl.BlockSpec((tm, d), lambda j, i: (i, 0)), pl.BlockSpec((d, d // 2), lambda j, i: (0, j)),
                 pl.BlockSpec((tm, d // 2), lambda j, i: (i, j)), NN,
                 add=xcur, add_spec=pl.BlockSpec((tm, d // 2), lambda j, i: (i, j)))

        h2 = rms_fwd(f"rms_ffn{l}", x1, vec(inp['norm_ffn_g']), tm)
        tu = min(256, rows)
        w_up = gathered(4 * l + 2, x1)[('ffn_w_up', l)]
        up = _mm(f"ffn_up{l}", h2, w_up, jax.ShapeDtypeStruct((4, rows, ffn_half), F32), (4, rows // tu),
                 pl.BlockSpec((tu, d), lambda k, i: (i, 0)), pl.BlockSpec((None, ffn_half, d), lambda k, i: (k, 0, 0)),
                 pl.BlockSpec((None, tu, ffn_half), lambda k, i: (k, i, 0)), NT)
        w_dw = ffn_w_dw[l].reshape(FFN_TAPS, 2, ffn_half).transpose(1, 0, 2)
        b_dw = inp['ffn_b_dw'][l].reshape(2, 1, ffn_half)
        act = ffn_gate_fwd(f"ffn_gate{l}", up, w_dw, b_dw, tc)
        w_down = gathered(4 * l + 3, up)[('ffn_w_down', l)].reshape(2, ffn_half, d)
        x2 = _mm(f"ffn_down{l}", act, w_down, jax.ShapeDtypeStruct((rows, d), F32), (rows // tm, 4),
                 pl.BlockSpec((2, tm, ffn_half), lambda i, j: (0, i, 0)), pl.BlockSpec((2, ffn_half, d // 4), lambda i, j: (0, 0, j)),
                 pl.BlockSpec((tm, d // 4), lambda i, j: (i, j)), NN, inner=("lead", 2),
                 add=x1, add_spec=pl.BlockSpec((tm, d // 4), lambda i, j: (i, j)))
        saved.append(dict(x=xcur, h=h, proj=proj, z=z, y_ssm=y_ssm, h0=h0, h1=h1, xc=xc, a_t=a_t, hseq=hseq, dgp=dgp,
                          mixed=mixed, x1=x1, h2=h2, up=up, act=act, w_in=w_in, w_out=w_out, w_up=w_up, w_down=w_down,
                          a_bar=a_bar, b_bar=b_bar, c_cat=c_cat, post_pars=post_pars, gate_pars=gate_pars, w_dw=w_dw, b_dw=b_dw,
                          s5=(lam_re, lam_im, log_step, b_re, b_im), cv_w=cv_w_dw[l][None], lru_w=lru_w_conv[l][None]))
        xcur = x2

    loss_row, dx, dg_final = final_loss("final_loss", xcur, inp['norm_final_g'].reshape(1, d), target, tm)
    grads['norm_final_g'] = dg_final.reshape(d)

    big_g = {nme: [None] * depth for nme in BIG}
    reduce_groups = []

    def start_reduce(tag, keys):
        pieces = []
        for nme, lyr in keys:
            g = big_g[nme][lyr]
            if nme == 'ffn_w_down':
                g = g.reshape(2, 4, ffn_half // 2, d // 2)
            pieces.append(reduce_cores(f"reduce_cores_{nme}{lyr}", g.reshape(2, -1, g.shape[-1])).reshape(g.shape[1:]))
        landing = [lax.empty((3,) + p.shape[1:], p.dtype) for p in pieces]
        groups, token = exchange_start(f"reduce_start_{tag}", [list(zip(pieces, landing))], xy, True)
        reduce_groups.append((tag, keys, groups[0]))
        return token

    for l in reversed(range(depth)):
        s = saved[l]
        ncol = s['w_in'].shape[2]
        tu = min(256, rows)
        dact = _mm(f"d_act{l}", dx, s['w_down'], jax.ShapeDtypeStruct((2, rows, ffn_half), F32), (2, rows // tu),
                   pl.BlockSpec((tu, d), lambda k, i: (i, 0)), pl.BlockSpec((None, ffn_half, d), lambda k, i: (k, 0, 0)),
                   pl.BlockSpec((None, tu, ffn_half), lambda k, i: (k, i, 0)), NT)
        tn = d // 4
        tk = min(1024, rows)
        tkb = min(2048, rows)
        big_g['ffn_w_down'][l] = _mm(
            f"dw_down{l}", s['act'], dx, jax.ShapeDtypeStruct((2, 2, ffn_half, d // 2), F32), (2, 4, rows // tkb),
            pl.BlockSpec((None, tkb, ffn_half), lambda hh, n, k: (hh, k, 0)), pl.BlockSpec((tkb, tn), lambda hh, n, k: (k, n)),
            pl.BlockSpec((None, None, ffn_half, tn), lambda hh, n, k: (n // 2, hh, 0, n % 2)), TN, k_axis=2)
        dup, dw_dw, db_dw = ffn_gate_bwd(f"ffn_gate_bwd{l}", s['up'], dact, s['w_dw'], s['b_dw'], tc)
        grads['ffn_w_dw'][l] = dw_dw.transpose(1, 0, 2).reshape(FFN_TAPS, 2 * ffn_half)
        grads['ffn_b_dw'][l] = db_dw.reshape(2 * ffn_half)
        dup = dup.reshape(4, rows, ffn_half)
        tm2 = min(1024, rows)
        dh2 = _mm(f"d_h2{l}", dup, s['w_up'], jax.ShapeDtypeStruct((rows, d), F32), (rows // tm2, 2, 4),
                  pl.BlockSpec((None, tm2, ffn_half), lambda i, j, k: (k, i, 0)), pl.BlockSpec((None, ffn_half, d // 2), lambda i, j, k: (k, 0, j)),
                  pl.BlockSpec((tm2, d // 2), lambda i, j, k: (i, j)), NN, k_axis=2)
        tmm = d // 4
        big_g['ffn_w_up'][l] = _mm(
            f"dw_up{l}", dup, s['h2'], jax.ShapeDtypeStruct((2, 4, ffn_half, d // 2), F32), (4, 4, rows // tkb),
            pl.BlockSpec((None, tkb, ffn_half), lambda k4, n, k: (k4, k, 0)), pl.BlockSpec((tkb, tn), lambda k4, n, k: (k, n)),
            pl.BlockSpec((None, None, ffn_half, tn), lambda k4, n, k: (n // 2, k4, 0, n % 2)), TN, k_axis=2)
        token = start_reduce(f"ffn{l}", [('ffn_w_down', l), ('ffn_w_up', l)])
        dx1, dg = rms_bwd(f"rms_ffn_bwd{l}", s['x1'], inp['norm_ffn_g'][l].reshape(1, d) + token[0, 0], dh2, dx, tm)
        grads['norm_ffn_g'][l] = dg.reshape(d)
        dmixed = _mm(f"d_mixed{l}", dx1, s['w_out'], jax.ShapeDtypeStruct((rows, d), F32), (rows // tm, 4),
                     pl.BlockSpec((tm, d), lambda i, j: (i, 0)), pl.BlockSpec((d // 4, d), lambda i, j: (j, 0)),
                     pl.BlockSpec((tm, d // 4), lambda i, j: (i, j)), NT)
        tq = mix_w // 2
        big_g['w_out'][l] = _mm(
            f"dw_out{l}", s['mixed'], dx1, jax.ShapeDtypeStruct((2, 4, tq, d), F32), (4, rows // tk),
            pl.BlockSpec((tk, 2 * tq), lambda t, k: (k, t)), pl.BlockSpec((tk, d), lambda t, k: (k, 0)),
            pl.BlockSpec((2, None, tq, d), lambda t, k: (0, t, 0, 0)), TN, k_axis=1)
        post_rows = [(s['y_ssm'], 0, mix_w), (s['proj'], 0, mix_w), (s['h1'], 0, mix_w), (s['hseq'], 0, mix_w),
                     (s['proj'], 4, mix_w), (s['dgp'], 0, mix_w), (dmixed, 0, d)]
        res = _rowwise(f"mix_post_bwd{l}", _mix_post, post_rows, s['post_pars'], 1, [(mix_w, F32)] * 6, tm, with_grads=True)
        dy_ssm, du_dir, dh1, dhseq, dlru_g, ddgp = res[:6]
        dd, dwglu, dbglu, dlng, dlnb, dwpw, dbpw, dpoolbd, dscale = res[6:]
        grads['s5_d'][l], grads['s5_b_glu'][l] = dd.reshape(mix_w), dbglu.reshape(mix_w)
        grads['cv_ln_g'][l], grads['cv_ln_b'][l], grads['cv_b_pw'][l] = dlng.reshape(mix_w), dlnb.reshape(mix_w), dbpw.reshape(mix_w)
        grads['pool_w'][l] = _diag_blocks(dpoolbd, len(POOL_WINDOWS))
        grads['pool_scale'][l] = dscale.reshape(mix_w)
        big_g['s5_w_glu'][l] = _halves(dwglu, 4)
        big_g['cv_w_pw'][l] = _halves(dwpw, 4)
        ts = min(2048, rows)
        cw, sw = mix_w // 4, n_state // 4
        slab = jnp.arange(mix_w)[:, None] // cw == jnp.arange(n_state)[None, :] // sw
        dz = _mm(f"s5_dz{l}", dy_ssm, s['c_cat'], jax.ShapeDtypeStruct((2, rows, n_state), F32), (rows // ts, 2, 4),
                 pl.BlockSpec((ts, cw), lambda i, c, q: (i, q)), pl.BlockSpec((None, sw, cw), lambda i, c, q: (c, q, q)),
                 pl.BlockSpec((None, ts, sw), lambda i, c, q: (c, i, q)), NT)
        dccat = _mm(f"s5_dc{l}", s['z'], dy_ssm, jax.ShapeDtypeStruct((2, n_state, mix_w), F32), (2, 4, rows // tk),
                    pl.BlockSpec((None, tk, sw), lambda c, q, k: (c, k, q)), pl.BlockSpec((tk, cw), lambda c, q, k: (k, q)),
                    pl.BlockSpec((None, sw, cw), lambda c, q, k: (c, q, q)), TN, k_axis=2)
        dccat = jnp.where(slab.T, dccat, 0.0)
        grads['s5_c_re'][l] = _diag_blocks(dccat[0], S5_GROUPS).transpose(0, 2, 1)
        grads['s5_c_im'][l] = -_diag_blocks(dccat[1], S5_GROUPS).transpose(0, 2, 1)
        lam, da_bar = s5_scan_bwd(f"s5_scan_bwd{l}", dz, s['z'], s['a_bar'])
        du = _mm(f"s5_du{l}", lam, s['b_bar'], jax.ShapeDtypeStruct((rows, mix_w), F32), (rows // ts, 4, 2),
                 pl.BlockSpec((None, ts, sw), lambda i, q, c: (c, i, q)), pl.BlockSpec((None, cw, sw), lambda i, q, c: (c, q, q)),
                 pl.BlockSpec((ts, cw), lambda i, q, c: (i, q)), NT, k_axis=2,
                 add=du_dir, add_spec=pl.BlockSpec((ts, cw), lambda i, q, c: (i, q)))
        dbbar = _mm(f"s5_db{l}", s['proj'], lam, jax.ShapeDtypeStruct((2, mix_w, n_state), F32), (2, 4, rows // tk),
                    pl.BlockSpec((tk, cw), lambda c, q, k: (k, q)), pl.BlockSpec((None, tk, sw), lambda c, q, k: (c, k, q)),
                    pl.BlockSpec((None, cw, sw), lambda c, q, k: (c, q, q)), TN, k_axis=2)
        dbbar = jnp.where(slab, dbbar, 0.0)
        dlr, dli, dls, dbre, dbim = s5_param_bwd(f"s5_param_bwd{l}", *s['s5'], da_bar, dbbar, gsum)
        grads['s5_lam_re'][l] = dlr.reshape(S5_GROUPS, S5_STATE)
        grads['s5_lam_im'][l] = dli.reshape(S5_GROUPS, S5_STATE)
        grads['s5_log_step'][l] = dls[0, :S5_GROUPS]
        grads['s5_b_re'][l] = _diag_blocks(dbre, S5_GROUPS).transpose(0, 2, 1)
        grads['s5_b_im'][l] = _diag_blocks(dbim, S5_GROUPS).transpose(0, 2, 1)
        dh0, dw_cv, db_cv = dwconv_bwd(f"cv_conv_bwd{l}", dh1.reshape(1, rows, mix_w), s['h0'].reshape(1, rows, mix_w), 0, mix_w,
                                       s['cv_w'], CV_TAPS, tc)
        grads['cv_w_dw'][l], grads['cv_b_dw'][l] = dw_cv[0], db_cv.reshape(mix_w)
        dv, dgg = _rowwise(f"cv_glu_bwd{l}", _glu, [(s['proj'], 1, mix_w), (s['proj'], 2, mix_w), (dh0[0], 0, mix_w)], [], 1,
                           [(mix_w, F32)] * 2, tm, with_grads=True)
        da_t, db_t = lru_scan_bwd(f"lru_scan_bwd{l}", dhseq, s['a_t'], s['hseq'])
        res = _rowwise(f"lru_gate_bwd{l}", _lru_gate, [(s['xc'], 0, mix_w), (da_t, 0, mix_w), (db_t, 0, mix_w)], s['gate_pars'], 2,
                       [(mix_w, F32)], tm, with_grads=True)
        dxc, dwr, dwi, dbr, dbi, dlam = res
        grads['lru_w_r'][l], grads['lru_w_i'][l] = _diag_blocks(dwr, LRU_HEADS), _diag_blocks(dwi, LRU_HEADS)
        grads['lru_b_r'][l], grads['lru_b_i'][l], grads['lru_lam'][l] = dbr.reshape(mix_w), dbi.reshape(mix_w), dlam.reshape(mix_w)
        dlx, dw_lc, db_lc = dwconv_bwd(f"lru_conv_bwd{l}", dxc.reshape(1, rows, mix_w), s['proj'].reshape(1, rows, 4 * ncol), 3, mix_w,
                                       s['lru_w'], LRU_TAPS, tc)
        grads['lru_w_conv'][l], grads['lru_b_conv'][l] = dw_lc[0], db_lc.reshape(mix_w)
        dpx = pool_bwd(f"pool_bwd{l}", ddgp, tc)
        dproj = jnp.concatenate([du, dv, dgg, dlx[0], dlru_g, dpx], axis=-1)
        dh = _mm(f"d_h{l}", dproj, s['w_in'], jax.ShapeDtypeStruct((rows, d), F32), (rows // tm, 4),
                 pl.BlockSpec((tm, 4 * ncol), lambda i, j: (i, 0)), pl.BlockSpec((4, d // 4, ncol), lambda i, j: (0, j, 0)),
                 pl.BlockSpec((tm, d // 4), lambda i, j: (i, j)), NT, inner=("cols", 4))
        tk2 = min(2048, rows)
        big_g['w_in'][l] = _mm(
            f"dw_in{l}", s['h'], dproj, jax.ShapeDtypeStruct((2, 4, d // 2, ncol), F32), (4, 2, rows // tk2),
            pl.BlockSpec((tk2, d // 2), lambda k4, m, k: (k, m)), pl.BlockSpec((tk2, ncol), lambda k4, m, k: (k, k4)),
            pl.BlockSpec((None, None, d // 2, ncol), lambda k4, m, k: (m, k4, 0, 0)), TN, k_axis=2)
        token = start_reduce(f"mix{l}", [('w_out', l), ('s5_w_glu', l), ('cv_w_pw', l), ('w_in', l)])
        dx, dg = rms_bwd(f"rms_mix_bwd{l}", s['x'], inp['norm_mix_g'][l].reshape(1, d) + token[0, 0], dh, dx1, tm)
        grads['norm_mix_g'][l] = dg.reshape(d)

    small = [nme for nme in WEIGHTS if nme not in BIG]
    full_g = {nme: (grads[nme] if nme == 'norm_final_g' else jnp.stack(grads[nme])) for nme in small}
    flat = jnp.concatenate([full_g[nme].reshape(-1) for nme in small])
    packed = jnp.pad(flat, (0, (-flat.shape[0]) % (128 * 64))).reshape(-1, 128)
    chip_sum = pair_sum("small_pair_sum", packed)
    small_zone = place_tile("place_small_grads", chip_sum[None], 0, my_chip, dtype=F32)
    (small_group,), small_token = exchange_start("small_start", [[small_zone]], xy, False)

    t_full = {}
    for tag, keys, group in reduce_groups:
        pieces, parts = exchange_wait(f"reduce_wait_{tag}", group, small_token, xy, True)
        for key, own, got in zip(keys, pieces, parts):
            t_full[key] = sum_and_share(f"share_cores_{key[0]}{key[1]}", own, got, my_chip)

    outs, done_big = {}, []
    tiles = {'w_in': 256, 'w_out': 128, 'ffn_w_up': 128, 'ffn_w_down': 256, 's5_w_glu': 64, 'cv_w_pw': 64}
    for nme in BIG:
        g0, g1 = t_full[(nme, 0)], t_full[(nme, 1)]
        if nme == 'ffn_w_up':
            res = adamw_sharded(f"adamw_{nme}", *(jnp.swapaxes(inp[p + nme], 1, 2) for p in ('', 'm_', 'v_')), g0, g1, True, tiles[nme])
            outs[nme] = tuple(jnp.swapaxes(r, 1, 2) for r in res)
        else:
            res = adamw_sharded(f"adamw_{nme}", inp[nme], inp['m_' + nme], inp['v_' + nme], g0, g1, nme == 'ffn_w_down', tiles[nme])
            outs[nme] = res
        done_big.append(res[1][:1, :1, :1].reshape(1))

    after_big = sum(done_big)
    (g4,) = exchange_wait("small_wait", small_group, after_big, xy, False)[1]
    gsum_small = sum_lead("sum_small", g4, 64).reshape(-1)
    red, off = {}, 0
    for nme in small:
        g = gsum_small[off:off + full_g[nme].size].reshape(full_g[nme].shape)
        off += full_g[nme].size
        if nme in SMALL_SHARDED:
            width = inp[nme].shape[2]
            g = lax.dynamic_slice_in_dim(g, my_chip[0] * width, width, axis=2)
        red[nme] = g

    def pack(tree):
        f = jnp.concatenate([tree[nme].reshape(-1) for nme in small])
        return jnp.pad(f, (0, (-f.shape[0]) % (128 * 64))).reshape(-1, 128)

    pd, pm, pv = adamw_flat("adamw_small", pack({n_: inp[n_] for n_ in small}), pack(red), pack({n_: inp['m_' + n_] for n_ in small}),
                            pack({n_: inp['v_' + n_] for n_ in small}), 64)
    off = 0
    for nme in small:
        size, shape = inp[nme].size, inp[nme].shape
        outs[nme] = (red[nme],) + tuple(p.reshape(-1)[off:off + size].reshape(shape) for p in (pd, pm, pv))
        off += size

    loss = lax.psum(loss_row[0, 0], ("x", "y", "c"))
    result = [loss, dx[None]]
    for part in range(4):
        result += [outs[nme][part] for nme in WEIGHTS]
    return tuple(result)


def kernel(x, norm_mix_g, w_in, s5_lam_re, s5_lam_im, s5_log_step, s5_b_re, s5_b_im, s5_c_re, s5_c_im, s5_d, s5_w_glu, s5_b_glu, cv_w_dw, cv_b_dw, cv_ln_g, cv_ln_b, cv_w_pw, cv_b_pw, lru_w_conv, lru_b_conv, lru_w_r, lru_b_r, lru_w_i, lru_b_i, lru_lam, pool_w, pool_scale, w_out, norm_ffn_g, ffn_w_up, ffn_w_dw, ffn_b_dw, ffn_w_down, norm_final_g, loss_target, m_norm_mix_g, m_w_in, m_s5_lam_re, m_s5_lam_im, m_s5_log_step, m_s5_b_re, m_s5_b_im, m_s5_c_re, m_s5_c_im, m_s5_d, m_s5_w_glu, m_s5_b_glu, m_cv_w_dw, m_cv_b_dw, m_cv_ln_g, m_cv_ln_b, m_cv_w_pw, m_cv_b_pw, m_lru_w_conv, m_lru_b_conv, m_lru_w_r, m_lru_b_r, m_lru_w_i, m_lru_b_i, m_lru_lam, m_pool_w, m_pool_scale, m_w_out, m_norm_ffn_g, m_ffn_w_up, m_ffn_w_dw, m_ffn_b_dw, m_ffn_w_down, m_norm_final_g, v_norm_mix_g, v_w_in, v_s5_lam_re, v_s5_lam_im, v_s5_log_step, v_s5_b_re, v_s5_b_im, v_s5_c_re, v_s5_c_im, v_s5_d, v_s5_w_glu, v_s5_b_glu, v_cv_w_dw, v_cv_b_dw, v_cv_ln_g, v_cv_ln_b, v_cv_w_pw, v_cv_b_pw, v_lru_w_conv, v_lru_b_conv, v_lru_w_r, v_lru_b_r, v_lru_w_i, v_lru_b_i, v_lru_lam, v_pool_w, v_pool_scale, v_w_out, v_norm_ffn_g, v_ffn_w_up, v_ffn_w_dw, v_ffn_b_dw, v_ffn_w_down, v_norm_final_g):
    inp = dict(locals())
    return _step(inp)
```

```python
import functools

import jax
import jax.numpy as jnp
from jax import lax
from jax.experimental import pallas as pl
from jax.experimental.pallas import tpu as pltpu

F32 = jnp.float32
BF16 = jnp.bfloat16

VMEM_LIMIT_BYTES = 56 * 1024 * 1024
SUBLANES = 8

EPS = 1e-6
S5_GROUPS, S5_STATE, S5_GROUP_CH = 32, 64, 16
LRU_HEADS, LRU_C = 8, 8.0
POOL_WINDOWS = (2, 4, 8, 16)
CV_TAPS, LRU_TAPS, FFN_TAPS = 31, 4, 3
SCAN_CHUNK = 64
GELU_K0, GELU_K1 = 0.7978845608028654, 0.044715

ADAM_LR, ADAM_B1, ADAM_B2, ADAM_EPS, ADAM_WD, ADAM_STEP = 0.001, 0.9, 0.999, 1e-08, 0.01, 10

NN = ((1,), (0,))
NT = ((1,), (1,))
TN = ((0,), (0,))

WEIGHTS = ['norm_mix_g', 'w_in', 's5_lam_re', 's5_lam_im', 's5_log_step', 's5_b_re', 's5_b_im', 's5_c_re', 's5_c_im',
           's5_d', 's5_w_glu', 's5_b_glu', 'cv_w_dw', 'cv_b_dw', 'cv_ln_g', 'cv_ln_b', 'cv_w_pw', 'cv_b_pw',
           'lru_w_conv', 'lru_b_conv', 'lru_w_r', 'lru_b_r', 'lru_w_i', 'lru_b_i', 'lru_lam', 'pool_w', 'pool_scale',
           'w_out', 'norm_ffn_g', 'ffn_w_up', 'ffn_w_dw', 'ffn_b_dw', 'ffn_w_down', 'norm_final_g']
BIG = ('w_in', 'w_out', 'ffn_w_up', 'ffn_w_down', 's5_w_glu', 'cv_w_pw')
SMALL_SHARDED = {'cv_w_dw': 2, 'lru_w_conv': 2, 'ffn_w_dw': 2}


def _params(sem=None):
    if sem is None:
        return pltpu.CompilerParams(vmem_limit_bytes=VMEM_LIMIT_BYTES)
    return pltpu.CompilerParams(dimension_semantics=sem, vmem_limit_bytes=VMEM_LIMIT_BYTES)


def _row_tile(rows, cap, mult=SUBLANES):
    best = mult
    for t in range(mult, min(rows, cap) + 1, mult):
        if rows % t == 0:
            best = t
    return best


def _bdot(a, b, dims=NN):
    return lax.dot_general(a.astype(BF16), b.astype(BF16), (dims, ((), ())), preferred_element_type=F32)


@jax.custom_vjp
def bdot(a, b):
    return _bdot(a, b)


def _bdot_fwd(a, b):
    return _bdot(a, b), (a, b)


def _bdot_bwd(res, g):
    a, b = res
    return _bdot(g, b, NT).astype(a.dtype), _bdot(a, g, TN).astype(b.dtype)


bdot.defvjp(_bdot_fwd, _bdot_bwd)


def _mm(name, a, b, out_sds, grid, a_spec, b_spec, o_spec, dims, k_axis=None, add=None, add_spec=None, inner=None):
    nk = grid[k_axis] if k_axis is not None else 1
    has_add = add is not None
    acc_shape = tuple(d for d in o_spec.block_shape if d is not None)
    acc_in_out = out_sds.dtype == F32

    def product(a_ref, b_ref):
        if inner is None:
            return _bdot(a_ref[...], b_ref[...], dims)
        kind, n = inner
        width = a_ref.shape[-1] // n
        acc = None
        for j in range(n):
            a_j = a_ref[j] if kind == "lead" else a_ref[:, j * width:(j + 1) * width]
            p = _bdot(a_j, b_ref[j], dims)
            acc = p if acc is None else acc + p
        return acc

    def body(*refs):
        a_ref, b_ref = refs[0], refs[1]
        add_ref = refs[2] if has_add else None
        o_ref = refs[3] if has_add else refs[2]
        prod = product(a_ref, b_ref).reshape(acc_shape)
        if k_axis is None:
            if has_add:
                prod = prod + add_ref[...]
            o_ref[...] = prod.astype(o_ref.dtype)
        else:
            acc_ref = o_ref if acc_in_out else refs[-1]
            k = pl.program_id(k_axis)

            @pl.when(k == 0)
            def _():
                acc_ref[...] = prod

            @pl.when(k > 0)
            def _():
                acc_ref[...] += prod

            if has_add or not acc_in_out:
                @pl.when(k == nk - 1)
                def _():
                    r = acc_ref[...]
                    if has_add:
                        r = r + add_ref[...]
                    o_ref[...] = r.astype(o_ref.dtype)

    sem = tuple("arbitrary" if d == k_axis else "parallel" for d in range(len(grid)))
    in_specs = [a_spec, b_spec] + ([add_spec] if has_add else [])
    args = (a, b) + ((add,) if has_add else ())
    scratch = [pltpu.VMEM(acc_shape, F32)] if (k_axis is not None and not acc_in_out) else []
    return pl.pallas_call(body, out_shape=out_sds, grid=grid, in_specs=in_specs, out_specs=o_spec,
                          scratch_shapes=scratch, compiler_params=_params(sem), name=name)(*args)


def _rms(x, g):
    return x * lax.rsqrt(jnp.mean(x * x, axis=-1, keepdims=True) + EPS) * g


def rms_fwd(name, x, g, tm):
    rows, d = x.shape

    def body(x_ref, g_ref, o_ref):
        o_ref[...] = _rms(x_ref[...], g_ref[...]).astype(BF16)

    return pl.pallas_call(
        body, out_shape=jax.ShapeDtypeStruct((rows, d), BF16), grid=(rows // tm,),
        in_specs=[pl.BlockSpec((tm, d), lambda i: (i, 0)), pl.BlockSpec((1, d), lambda i: (0, 0))],
        out_specs=pl.BlockSpec((tm, d), lambda i: (i, 0)), compiler_params=_params(("parallel",)), name=name)(x, g)


def rms_bwd(name, x, g, dh, dres, tm):
    rows, d = x.shape

    def body(x_ref, g_ref, dh_ref, dres_ref, dx_ref, dg_ref):
        xv, dy = x_ref[...], dh_ref[...]
        r = lax.rsqrt(jnp.mean(xv * xv, axis=-1, keepdims=True) + EPS)
        dyg = dy * g_ref[...]
        s = jnp.mean(dyg * xv, axis=-1, keepdims=True)
        dx_ref[...] = r * dyg - xv * (r * r * r * s) + dres_ref[...]

        @pl.when(pl.program_id(0) == 0)
        def _():
            dg_ref[...] = jnp.zeros_like(dg_ref)

        dg_ref[...] += jnp.sum(dy * xv * r, axis=0, keepdims=True)

    row = pl.BlockSpec((tm, d), lambda i: (i, 0))
    vec = pl.BlockSpec((1, d), lambda i: (0, 0))
    return pl.pallas_call(
        body, out_shape=(jax.ShapeDtypeStruct((rows, d), F32), jax.ShapeDtypeStruct((1, d), F32)), grid=(rows // tm,),
        in_specs=[row, vec, row, row], out_specs=(row, vec), compiler_params=_params(("arbitrary",)), name=name)(x, g, dh, dres)


def final_loss(name, x, g, target, tm):
    rows, d = x.shape

    def body(x_ref, g_ref, t_ref, l_ref, dx_ref, dg_ref):
        def f(xv, gv):
            e = _rms(xv, gv) - t_ref[...]
            return 0.5 * jnp.sum(jnp.mean(e * e, axis=-1))

        loss, (dx, dg) = jax.value_and_grad(f, argnums=(0, 1))(x_ref[...], g_ref[...])
        dx_ref[...] = dx

        @pl.when(pl.program_id(0) == 0)
        def _():
            l_ref[...] = jnp.zeros_like(l_ref)
            dg_ref[...] = jnp.zeros_like(dg_ref)

        l_ref[...] += jnp.full(l_ref.shape, loss, F32)
        dg_ref[...] += dg

    row = pl.BlockSpec((tm, d), lambda i: (i, 0))
    vec = pl.BlockSpec((1, d), lambda i: (0, 0))
    lspec = pl.BlockSpec((1, 128), lambda i: (0, 0))
    return pl.pallas_call(
        body, out_shape=(jax.ShapeDtypeStruct((1, 128), F32), jax.ShapeDtypeStruct((rows, d), F32), jax.ShapeDtypeStruct((1, d), F32)),
        grid=(rows // tm,), in_specs=[row, vec, row], out_specs=(lspec, row, vec),
        compiler_params=_params(("arbitrary",)), name=name)(x, g, target)


def _rowwise(name, fn, row_ins, par_ins, n_row_out, row_out_dtypes, tm, with_grads=False):
    rows = row_ins[0][0].shape[0]
    n_prim = len(row_ins) - (n_row_out if with_grads else 0)
    n_par = len(par_ins)

    def body(*refs):
        ins = [r[...] for r in refs[:len(row_ins) + n_par]]
        outs = refs[len(row_ins) + n_par:]
        prim, cts, pars = ins[:n_prim], ins[n_prim:len(row_ins)], ins[len(row_ins):]
        if not with_grads:
            res = fn(*prim, *pars)
            for o_ref, r in zip(outs, res):
                o_ref[...] = r.astype(o_ref.dtype)
            return
        _, vjp = jax.vjp(fn, *prim, *[p.astype(F32) for p in pars])
        grads = vjp(tuple(cts))
        for o_ref, gr in zip(outs[:n_prim], grads[:n_prim]):
            o_ref[...] = gr.astype(o_ref.dtype)

        @pl.when(pl.program_id(0) == 0)
        def _():
            for o_ref in outs[n_prim:]:
                o_ref[...] = jnp.zeros_like(o_ref)

        for o_ref, gr in zip(outs[n_prim:], grads[n_prim:]):
            o_ref[...] += gr.astype(F32)

    in_specs = [pl.BlockSpec((tm, w), (lambda i, c=c: (i, c))) for (_, c, w) in row_ins]
    in_specs += [pl.BlockSpec(p.shape, (lambda i, n=p.ndim: (0,) * n)) for p in par_ins]
    args = [a for (a, _, _) in row_ins] + list(par_ins)
    if not with_grads:
        out_shape = tuple(jax.ShapeDtypeStruct((rows, w), dt) for (w, dt) in row_out_dtypes)
        out_specs = tuple(pl.BlockSpec((tm, w), lambda i: (i, 0)) for (w, _) in row_out_dtypes)
        sem = ("parallel",)
    else:
        out_shape = tuple(jax.ShapeDtypeStruct((rows, w), dt) for (w, dt) in row_out_dtypes)
        out_shape += tuple(jax.ShapeDtypeStruct(p.shape, F32) for p in par_ins)
        out_specs = tuple(pl.BlockSpec((tm, w), lambda i: (i, 0)) for (w, _) in row_out_dtypes)
        out_specs += tuple(pl.BlockSpec(p.shape, (lambda i, n=p.ndim: (0,) * n)) for p in par_ins)
        sem = ("arbitrary",)
    return pl.pallas_call(body, out_shape=out_shape, grid=(rows // tm,), in_specs=in_specs, out_specs=out_specs,
                          compiler_params=_params(sem), name=name)(*args)


def _glu(v, g):
    return (v * jax.nn.sigmoid(g),)


def _neg_expm1(z):
    return -jnp.tanh(0.5 * z) * (jnp.exp(z) + 1.0)


def _lru_gate(xc, w_r, w_i, b_r, b_i, lam):
    r = jax.nn.sigmoid(bdot(xc, w_r) + b_r)
    i = jax.nn.sigmoid(bdot(xc, w_i) + b_i)
    log_a = -LRU_C * r * jax.nn.softplus(-lam)
    a = jnp.exp(log_a)
    mult = jnp.sqrt(_neg_expm1(2.0 * log_a))
    return a, mult * (i * xc)


def _layernorm(x, g, b):
    mu = jnp.mean(x, axis=-1, keepdims=True)
    var = jnp.mean(jnp.square(x - mu), axis=-1, keepdims=True)
    return (x - mu) * lax.rsqrt(var + EPS) * g + b


def _mix_post(y_ssm, u, h1, hseq, lru_g, dgp, s5_d, w_glu, b_glu, ln_g, ln_b, w_pw, b_pw, pool_bd, pool_scale):
    y = y_ssm + s5_d * u
    gl = jax.nn.gelu(y, approximate=True)
    out_s5 = gl * jax.nn.sigmoid(bdot(gl, w_glu) + b_glu)
    out_cv = bdot(jax.nn.silu(_layernorm(h1, ln_g, ln_b)), w_pw) + b_pw
    out_lru = hseq * jax.nn.gelu(lru_g, approximate=True)
    out_pool = bdot(dgp, pool_bd) * pool_scale
    return (jnp.concatenate([out_s5, out_cv, out_lru, out_pool], axis=-1),)


def _gelu_terms(x):
    sq = x * x
    t = jnp.tanh(x * (GELU_K0 + (GELU_K0 * GELU_K1) * sq))
    return sq, t, 0.5 + 0.5 * t


def _halo_rows(taps):
    return -(-(taps - 1) // SUBLANES) * SUBLANES


def _row_windows(ext, offsets, n, shifted_ref=None):
    if shifted_ref is None:
        return {off: ext[off:off + n] for off in offsets}
    room = ext.shape[0] - SUBLANES
    slots, out = {}, {}
    for off in offsets:
        r = off % SUBLANES
        if r == 0:
            out[off] = ext[off:off + n]
            continue
        if r not in slots:
            slots[r] = len(slots)
            shifted_ref[slots[r]] = ext[r:r + room]
        out[off] = shifted_ref[slots[r], off - r:off - r + n, :]
    return out


def _shift_scratch(taps, tm, c):
    return [pltpu.VMEM((SUBLANES - 1, _halo_rows(taps) + tm - SUBLANES, c), F32)] if taps > SUBLANES else []


def dwconv_fwd(name, x, cblk, c, w, b, taps, tm, out_dtype=F32):
    nb = w.shape[0]
    rows = x.shape[1]
    halo = _halo_rows(taps)
    per = tm // halo

    def body(x_ref, h_ref, w_ref, b_ref, o_ref, *shifted):
        i = pl.program_id(1)
        prev = jnp.where(i > 0, h_ref[...], 0.0)
        ext = jnp.concatenate([prev, x_ref[...]], axis=0)
        win = _row_windows(ext, [halo - (taps - 1) + k for k in range(taps)], tm, *shifted)
        acc = jnp.broadcast_to(b_ref[...], (tm, c))
        for k in range(taps):
            acc = acc + w_ref[k:k + 1, :] * win[halo - (taps - 1) + k]
        o_ref[...] = acc.astype(o_ref.dtype)

    return pl.pallas_call(
        body, out_shape=jax.ShapeDtypeStruct((nb, rows, c), out_dtype), grid=(nb, rows // tm),
        in_specs=[pl.BlockSpec((None, tm, c), lambda n, i: (n, i, cblk)),
                  pl.BlockSpec((None, halo, c), lambda n, i: (n, jnp.maximum(i * per - 1, 0), cblk)),
                  pl.BlockSpec((None, taps, c), lambda n, i: (n, 0, 0)),
                  pl.BlockSpec((None, 1, c), lambda n, i: (n, 0, 0))],
        out_specs=pl.BlockSpec((None, tm, c), lambda n, i: (n, i, 0)), scratch_shapes=_shift_scratch(taps, tm, c),
        compiler_params=_params(("parallel", "parallel")), name=name)(x, x, w, b)


def dwconv_bwd(name, dy, x, cblk, c, w, taps, tm, dx_dtype=F32):
    nb = w.shape[0]
    rows = x.shape[1]
    halo = _halo_rows(taps)
    per = tm // halo
    n_tiles = rows // tm
    last_halo = rows // halo - 1

    def body(dy_ref, dn_ref, x_ref, xp_ref, w_ref, dx_ref, dw_ref, db_ref, *shifted):
        i = pl.program_id(1)
        dyv = dy_ref[...]
        nxt = jnp.where(i < n_tiles - 1, dn_ref[...], 0.0)
        dext = jnp.concatenate([dyv, nxt], axis=0)
        prev = jnp.where(i > 0, xp_ref[...], 0.0)
        xext = jnp.concatenate([prev, x_ref[...]], axis=0)
        acc = jnp.zeros((tm, c), F32)

        @pl.when(i == 0)
        def _():
            dw_ref[...] = jnp.zeros_like(dw_ref)
            db_ref[...] = jnp.zeros_like(db_ref)

        dwin = _row_windows(dext, list(range(taps)), tm, *shifted[:1])
        xwin = _row_windows(xext, [halo - (taps - 1) + k for k in range(taps)], tm, *shifted[1:])
        for k in range(taps):
            acc = acc + w_ref[k:k + 1, :] * dwin[taps - 1 - k]
            dw_ref[k:k + 1, :] += jnp.sum(dyv * xwin[halo - (taps - 1) + k], axis=0, keepdims=True)
        dx_ref[...] = acc.astype(dx_ref.dtype)
        db_ref[...] += jnp.sum(dyv, axis=0, keepdims=True)

    return pl.pallas_call(
        body, out_shape=(jax.ShapeDtypeStruct((nb, rows, c), dx_dtype), jax.ShapeDtypeStruct((nb, taps, c), F32),
                         jax.ShapeDtypeStruct((nb, 1, c), F32)),
        grid=(nb, n_tiles),
        in_specs=[pl.BlockSpec((None, tm, c), lambda n, i: (n, i, 0)),
                  pl.BlockSpec((None, halo, c), lambda n, i: (n, jnp.minimum((i + 1) * per, last_halo), 0)),
                  pl.BlockSpec((None, tm, c), lambda n, i: (n, i, cblk)),
                  pl.BlockSpec((None, halo, c), lambda n, i: (n, jnp.maximum(i * per - 1, 0), cblk)),
                  pl.BlockSpec((None, taps, c), lambda n, i: (n, 0, 0))],
        out_specs=(pl.BlockSpec((None, tm, c), lambda n, i: (n, i, 0)), pl.BlockSpec((None, taps, c), lambda n, i: (n, 0, 0)),
                   pl.BlockSpec((None, 1, c), lambda n, i: (n, 0, 0))),
        scratch_shapes=2 * _shift_scratch(taps, tm, c),
        compiler_params=_params(("parallel", "arbitrary")), name=name)(dy, dy, x, x, w)


def ffn_gate_fwd(name, up, w, b, tm):
    _, rows, c = up.shape
    halo = _halo_rows(FFN_TAPS)
    per = tm // halo

    def body(g_ref, gp_ref, v_ref, w_ref, b_ref, o_ref):
        i = pl.program_id(1)
        ext = jnp.concatenate([jnp.where(i > 0, gp_ref[...], 0.0), g_ref[...]], axis=0)
        gc = jnp.broadcast_to(b_ref[...], (tm, c))
        for k in range(FFN_TAPS):
            off = halo - (FFN_TAPS - 1) + k
            gc = gc + w_ref[k:k + 1, :] * ext[off:off + tm]
        o_ref[...] = (gc * _gelu_terms(gc)[2] * v_ref[...]).astype(BF16)

    return pl.pallas_call(
        body, out_shape=jax.ShapeDtypeStruct((2, rows, c), BF16), grid=(2, rows // tm),
        in_specs=[pl.BlockSpec((None, tm, c), lambda h, i: (h, i, 0)),
                  pl.BlockSpec((None, halo, c), lambda h, i: (h, jnp.maximum(i * per - 1, 0), 0)),
                  pl.BlockSpec((None, tm, c), lambda h, i: (h + 2, i, 0)),
                  pl.BlockSpec((None, FFN_TAPS, c), lambda h, i: (h, 0, 0)), pl.BlockSpec((None, 1, c), lambda h, i: (h, 0, 0))],
        out_specs=pl.BlockSpec((None, tm, c), lambda h, i: (h, i, 0)),
        compiler_params=_params(("parallel", "parallel")), name=name)(up, up, up, w, b)


def ffn_gate_bwd(name, up, dact, w, b, tm):
    _, rows, c = up.shape
    halo = _halo_rows(FFN_TAPS)
    per = tm // halo
    n_tiles = rows // tm
    last_halo = rows // halo - 1
    n_ext = tm + halo

    def body(g_ref, gp_ref, gn_ref, v_ref, vn_ref, d_ref, dn_ref, w_ref, b_ref, dup_ref, dw_ref, db_ref):
        i = pl.program_id(1)
        gext = jnp.concatenate([jnp.where(i > 0, gp_ref[...], 0.0), g_ref[...], gn_ref[...]], axis=0)
        shifted = [gext[halo - (FFN_TAPS - 1) + k:halo - (FFN_TAPS - 1) + k + n_ext] for k in range(FFN_TAPS)]
        gc = jnp.broadcast_to(b_ref[...], (n_ext, c))
        for k in range(FFN_TAPS):
            gc = gc + w_ref[k:k + 1, :] * shifted[k]
        vext = jnp.concatenate([v_ref[...], vn_ref[...]], axis=0)
        dext = jnp.concatenate([d_ref[...], dn_ref[...]], axis=0)
        sq, t, half = _gelu_terms(gc)
        dval = dext * (gc * half)
        dgc = (dext * vext) * (half + (0.5 * gc) * (1.0 - t * t) * (GELU_K0 + (3.0 * GELU_K0 * GELU_K1) * sq))
        r = lax.broadcasted_iota(jnp.int32, (n_ext, c), 0)
        dgc = jnp.where((r < tm) | (i < n_tiles - 1), dgc, 0.0)
        dgate = jnp.zeros((tm, c), F32)
        for k in range(FFN_TAPS):
            dgate = dgate + w_ref[k:k + 1, :] * dgc[FFN_TAPS - 1 - k:FFN_TAPS - 1 - k + tm]
        dup_ref[0] = dgate.astype(BF16)
        dup_ref[1] = dval[:tm].astype(BF16)

        @pl.when(i == 0)
        def _():
            dw_ref[...] = jnp.zeros_like(dw_ref)
            db_ref[...] = jnp.zeros_like(db_ref)

        dgc_t = dgc[:tm]
        for k in range(FFN_TAPS):
            dw_ref[k:k + 1, :] += jnp.sum(dgc_t * shifted[k][:tm], axis=0, keepdims=True)
        db_ref[...] += jnp.sum(dgc_t, axis=0, keepdims=True)

    def tile(shift):
        return pl.BlockSpec((None, tm, c), lambda h, i: (h + shift, i, 0))

    def after(shift):
        return pl.BlockSpec((None, halo, c), lambda h, i: (h + shift, jnp.minimum((i + 1) * per, last_halo), 0))

    return pl.pallas_call(
        body, out_shape=(jax.ShapeDtypeStruct((2, 2, rows, c), BF16), jax.ShapeDtypeStruct((2, FFN_TAPS, c), F32),
                         jax.ShapeDtypeStruct((2, 1, c), F32)),
        grid=(2, n_tiles),
        in_specs=[tile(0), pl.BlockSpec((None, halo, c), lambda h, i: (h, jnp.maximum(i * per - 1, 0), 0)), after(0),
                  tile(2), after(2), tile(0), after(0),
                  pl.BlockSpec((None, FFN_TAPS, c), lambda h, i: (h, 0, 0)), pl.BlockSpec((None, 1, c), lambda h, i: (h, 0, 0))],
        out_specs=(pl.BlockSpec((2, None, tm, c), lambda h, i: (0, h, i, 0)), pl.BlockSpec((None, FFN_TAPS, c), lambda h, i: (h, 0, 0)),
                   pl.BlockSpec((None, 1, c), lambda h, i: (h, 0, 0))),
        compiler_params=_params(("parallel", "arbitrary")), name=name)(up, up, up, up, up, dact, dact, w, b)


POOL_HALO = 16


def pool_fwd(name, proj, cblk, tm):
    rows = proj.shape[0]
    c = 128 * len(POOL_WINDOWS)
    per = tm // POOL_HALO

    def body(x_ref, h_ref, o_ref):
        i = pl.program_id(0)
        xv = x_ref[...]
        ext = jnp.concatenate([jnp.where(i > 0, h_ref[...], 0.0), xv], axis=0)
        t1 = (lax.broadcasted_iota(jnp.int32, (tm, 128), 0) + i * tm + 1).astype(F32)
        outs = []
        for gi, win in enumerate(POOL_WINDOWS):
            seg = ext[:, gi * 128:(gi + 1) * 128]
            s = seg[POOL_HALO:POOL_HALO + tm]
            for j in range(1, win):
                s = s + seg[POOL_HALO - j:POOL_HALO - j + tm]
            outs.append(s / jnp.minimum(t1, float(win)) - xv[:, gi * 128:(gi + 1) * 128])
        o_ref[...] = jnp.concatenate(outs, axis=-1)

    return pl.pallas_call(
        body, out_shape=jax.ShapeDtypeStruct((rows, c), F32), grid=(rows // tm,),
        in_specs=[pl.BlockSpec((tm, c), lambda i: (i, cblk)),
                  pl.BlockSpec((POOL_HALO, c), lambda i: (jnp.maximum(i * per - 1, 0), cblk))],
        out_specs=pl.BlockSpec((tm, c), lambda i: (i, 0)), compiler_params=_params(("parallel",)), name=name)(proj, proj)


def pool_bwd(name, dd, tm):
    rows, c = dd.shape
    per = tm // POOL_HALO
    n_tiles = rows // tm
    last_halo = rows // POOL_HALO - 1

    def body(d_ref, n_ref, o_ref):
        i = pl.program_id(0)
        dv = d_ref[...]
        nxt = jnp.where(i < n_tiles - 1, n_ref[...], 0.0)
        t1 = (lax.broadcasted_iota(jnp.int32, (tm, 128), 0) + i * tm + 1).astype(F32)
        t1n = (lax.broadcasted_iota(jnp.int32, (POOL_HALO, 128), 0) + (i + 1) * tm + 1).astype(F32)
        outs = []
        for gi, win in enumerate(POOL_WINDOWS):
            sl = slice(gi * 128, (gi + 1) * 128)
            q = jnp.concatenate([dv[:, sl] / jnp.minimum(t1, float(win)), nxt[:, sl] / jnp.minimum(t1n, float(win))], axis=0)
            s = q[0:tm]
            for j in range(1, win):
                s = s + q[j:j + tm]
            outs.append(s - dv[:, sl])
        o_ref[...] = jnp.concatenate(outs, axis=-1)

    return pl.pallas_call(
        body, out_shape=jax.ShapeDtypeStruct((rows, c), F32), grid=(n_tiles,),
        in_specs=[pl.BlockSpec((tm, c), lambda i: (i, 0)),
                  pl.BlockSpec((POOL_HALO, c), lambda i: (jnp.minimum((i + 1) * per, last_halo), 0))],
        out_specs=pl.BlockSpec((tm, c), lambda i: (i, 0)), compiler_params=_params(("parallel",)), name=name)(dd, dd)


BLOCK_STEPS = 3


def _shift_down(v, s, fill):
    r = lax.broadcasted_iota(jnp.int32, v.shape, 0)
    return jnp.where(r >= s, pltpu.roll(v, s, 0), fill)


def _shift_up(v, s, fill):
    n = v.shape[0]
    r = lax.broadcasted_iota(jnp.int32, v.shape, 0)
    return jnp.where(r < n - s, pltpu.roll(v, n - s, 0), fill)


def _shift_in_blocks(v, s, fill, reverse):
    n = v.shape[0]
    q = lax.broadcasted_iota(jnp.int32, v.shape, 0) & (SUBLANES - 1)
    if reverse:
        return jnp.where(q < SUBLANES - s, pltpu.roll(v, n - s, 0), fill)
    return jnp.where(q >= s, pltpu.roll(v, s, 0), fill)


def _cscan_blocks(vr, vi, powers, reverse):
    for k, (qr, qi) in enumerate(powers):
        s = 1 << k
        sr, si = _shift_in_blocks(vr, s, 0.0, reverse), _shift_in_blocks(vi, s, 0.0, reverse)
        if reverse:
            vr, vi = vr + qr * sr + qi * si, vi + qr * si - qi * sr
        else:
            vr, vi = vr + qr * sr - qi * si, vi + qr * si + qi * sr
    return vr, vi


def _cscan_table(pr, pi, powers, reverse):
    r = lax.broadcasted_iota(jnp.int32, (SUBLANES, 128), 0)
    at = (SUBLANES - 1) if reverse else 0
    return _cscan_blocks(jnp.where(r == at, pr, 0.0), jnp.where(r == at, -pi if reverse else pi, 0.0), powers, reverse)


def _cscan_chunk(vr, vi, powers, table, carry, reverse):
    vr, vi = _cscan_blocks(vr, vi, powers, reverse)
    tr, ti = table
    cr, ci = carry
    nb = vr.shape[0] // SUBLANES
    outr, outi = [None] * nb, [None] * nb
    edge = 0 if reverse else SUBLANES - 1
    for j in (reversed(range(nb)) if reverse else range(nb)):
        rows = slice(j * SUBLANES, (j + 1) * SUBLANES)
        zr = vr[rows] + tr * cr - ti * ci
        zi = vi[rows] + tr * ci + ti * cr
        outr[j], outi[j] = zr, zi
        cr, ci = zr[edge:edge + 1], zi[edge:edge + 1]
    return jnp.concatenate(outr, axis=0), jnp.concatenate(outi, axis=0), (cr, ci)


def _powers(pr, pi, n):
    out = [(pr, pi)]
    for _ in range(n - 1):
        pr, pi = pr * pr - pi * pi, 2.0 * pr * pi
        out.append((pr, pi))
    return out


def s5_scan_fwd(name, bu, a):
    _, rows, n = bu.shape
    t = min(SCAN_CHUNK, rows)

    def body(bu_ref, a_ref, z_ref):
        pr, pi = a_ref[0], a_ref[1]
        powers = _powers(pr, pi, BLOCK_STEPS)
        table = _cscan_table(pr, pi, powers, False)

        def chunk(ci, carry):
            base = pl.multiple_of(ci * t, t)
            zr, zi, carry = _cscan_chunk(bu_ref[0, pl.ds(base, t), :], bu_ref[1, pl.ds(base, t), :], powers, table, carry, False)
            z_ref[0, pl.ds(base, t), :] = zr
            z_ref[1, pl.ds(base, t), :] = zi
            return carry

        zero = jnp.zeros((1, 128), F32)
        lax.fori_loop(0, rows // t, chunk, (zero, zero))

    return pl.pallas_call(
        body, out_shape=jax.ShapeDtypeStruct((2, rows, n), F32), grid=(n // 128,),
        in_specs=[pl.BlockSpec((2, rows, 128), lambda j: (0, 0, j)), pl.BlockSpec((2, 1, 128), lambda j: (0, 0, j))],
        out_specs=pl.BlockSpec((2, rows, 128), lambda j: (0, 0, j)), compiler_params=_params(("parallel",)), name=name)(bu, a)


def s5_scan_bwd(name, dz, z, a):
    _, rows, n = dz.shape
    t = min(SCAN_CHUNK, rows)
    n_chunks = rows // t

    def body(dz_ref, z_ref, a_ref, lam_ref, da_ref):
        pr, pi = a_ref[0], a_ref[1]
        powers = _powers(pr, pi, BLOCK_STEPS)
        table = _cscan_table(pr, pi, powers, True)

        def chunk(k, carry):
            ci = n_chunks - 1 - k
            base = pl.multiple_of(ci * t, t)
            cr, cim, dar, dai = carry
            lr, li, (cr, cim) = _cscan_chunk(dz_ref[0, pl.ds(base, t), :], dz_ref[1, pl.ds(base, t), :], powers, table, (cr, cim), True)
            lam_ref[0, pl.ds(base, t), :] = lr
            lam_ref[1, pl.ds(base, t), :] = li
            pbase = pl.multiple_of(jnp.maximum(base - SUBLANES, 0), SUBLANES)
            keep = (ci > 0).astype(F32)
            pzr = z_ref[0, pl.ds(pbase, SUBLANES), :][SUBLANES - 1:SUBLANES, :] * keep
            pzi = z_ref[1, pl.ds(pbase, SUBLANES), :][SUBLANES - 1:SUBLANES, :] * keep
            zpr = _shift_down(z_ref[0, pl.ds(base, t), :], 1, pzr)
            zpi = _shift_down(z_ref[1, pl.ds(base, t), :], 1, pzi)
            dar = dar + jnp.sum(lr * zpr + li * zpi, axis=0, keepdims=True)
            dai = dai + jnp.sum(li * zpr - lr * zpi, axis=0, keepdims=True)
            return cr, cim, dar, dai

        zero = jnp.zeros((1, 128), F32)
        _, _, dar, dai = lax.fori_loop(0, n_chunks, chunk, (zero, zero, zero, zero))
        da_ref[0] = dar
        da_ref[1] = dai

    seq = pl.BlockSpec((2, rows, 128), lambda j: (0, 0, j))
    vec = pl.BlockSpec((2, 1, 128), lambda j: (0, 0, j))
    return pl.pallas_call(
        body, out_shape=(jax.ShapeDtypeStruct((2, rows, n), F32), jax.ShapeDtypeStruct((2, 1, n), F32)), grid=(n // 128,),
        in_specs=[seq, seq, vec], out_specs=(seq, vec), compiler_params=_params(("parallel",)), name=name)(dz, z, a)


def _rscan_chunk(a, b, carry, reverse):
    n = a.shape[0]
    shift = _shift_up if reverse else _shift_down
    for k in range(n.bit_length() - 1):
        s = 1 << k
        b = b + a * shift(b, s, 0.0)
        a = a * shift(a, s, 1.0)
    h = b + a * carry
    edge = 0 if reverse else n - 1
    return h, h[edge:edge + 1]


def lru_scan_fwd(name, a, b):
    rows, n = a.shape
    t = min(SCAN_CHUNK, rows)

    def body(a_ref, b_ref, h_ref):
        def chunk(ci, carry):
            base = pl.multiple_of(ci * t, t)
            h, carry = _rscan_chunk(a_ref[pl.ds(base, t), :], b_ref[pl.ds(base, t), :], carry, False)
            h_ref[pl.ds(base, t), :] = h
            return carry

        lax.fori_loop(0, rows // t, chunk, jnp.zeros((1, 128), F32))

    seq = pl.BlockSpec((rows, 128), lambda j: (0, j))
    return pl.pallas_call(body, out_shape=jax.ShapeDtypeStruct((rows, n), F32), grid=(n // 128,), in_specs=[seq, seq],
                          out_specs=seq, compiler_params=_params(("parallel",)), name=name)(a, b)


def lru_scan_bwd(name, dh, a, h):
    rows, n = a.shape
    t = min(SCAN_CHUNK, rows)
    n_chunks = rows // t

    def body(dh_ref, a_ref, h_ref, da_ref, db_ref):
        def chunk(k, carry):
            ci = n_chunks - 1 - k
            base = pl.multiple_of(ci * t, t)
            nbase = pl.multiple_of(jnp.minimum(base + t, rows - SUBLANES), SUBLANES)
            a_next = a_ref[pl.ds(nbase, SUBLANES), :][0:1, :]
            an = _shift_up(a_ref[pl.ds(base, t), :], 1, a_next)
            mu, carry = _rscan_chunk(an, dh_ref[pl.ds(base, t), :], carry, True)
            pbase = pl.multiple_of(jnp.maximum(base - SUBLANES, 0), SUBLANES)
            hp_row = h_ref[pl.ds(pbase, SUBLANES), :][SUBLANES - 1:SUBLANES, :] * (ci > 0).astype(F32)
            hp = _shift_down(h_ref[pl.ds(base, t), :], 1, hp_row)
            da_ref[pl.ds(base, t), :] = mu * hp
            db_ref[pl.ds(base, t), :] = mu
            return carry

        lax.fori_loop(0, n_chunks, chunk, jnp.zeros((1, 128), F32))

    seq = pl.BlockSpec((rows, 128), lambda j: (0, j))
    return pl.pallas_call(
        body, out_shape=(jax.ShapeDtypeStruct((rows, n), F32), jax.ShapeDtypeStruct((rows, n), F32)), grid=(n // 128,),
        in_specs=[seq, seq, seq], out_specs=(seq, seq), compiler_params=_params(("parallel",)), name=name)(dh, a, h)


def _s5_param(lr, li, ls, bre, bim):
    st = jnp.exp(ls)
    er = jnp.exp(lr * st)
    th = li * st
    ar, ai = er * jnp.cos(th), er * jnp.sin(th)
    nr, ni = ar - 1.0, ai
    den = lr * lr + li * li
    cr, ci = (nr * lr + ni * li) / den, (ni * lr - nr * li) / den
    return ar, ai, cr * bre - ci * bim, cr * bim + ci * bre


def s5_param_fwd(name, lr, li, ls, bre, bim):
    gh, n = bre.shape

    def body(lr_ref, li_ref, ls_ref, bre_ref, bim_ref, a_ref, bb_ref):
        ar, ai, br, bi = _s5_param(lr_ref[...], li_ref[...], ls_ref[...], bre_ref[...], bim_ref[...])
        a_ref[0] = ar
        a_ref[1] = ai
        bb_ref[0] = br.astype(BF16)
        bb_ref[1] = bi.astype(BF16)

    return pl.pallas_call(body, out_shape=(jax.ShapeDtypeStruct((2, 1, n), F32), jax.ShapeDtypeStruct((2, gh, n), BF16)),
                          compiler_params=_params(), name=name)(lr, li, ls, bre, bim)


def s5_param_bwd(name, lr, li, ls, bre, bim, da, dbb, gsum):
    gh, n = bre.shape

    def body(lr_ref, li_ref, ls_ref, bre_ref, bim_ref, da_ref, dbb_ref, gs_ref, dlr_ref, dli_ref, dls_ref, dbre_ref, dbim_ref):
        _, vjp = jax.vjp(_s5_param, lr_ref[...], li_ref[...], ls_ref[...], bre_ref[...], bim_ref[...])
        dlr, dli, dls, dbre, dbim = vjp((da_ref[0], da_ref[1], dbb_ref[0], dbb_ref[1]))
        dlr_ref[...] = dlr
        dli_ref[...] = dli
        dls_ref[...] = jnp.dot(jnp.broadcast_to(dls, (SUBLANES, n)), gs_ref[...], preferred_element_type=F32,
                               precision=lax.Precision.HIGHEST)
        dbre_ref[...] = dbre
        dbim_ref[...] = dbim

    vec = jax.ShapeDtypeStruct((1, n), F32)
    mat = jax.ShapeDtypeStruct((gh, n), F32)
    return pl.pallas_call(body, out_shape=(vec, vec, jax.ShapeDtypeStruct((SUBLANES, 128), F32), mat, mat),
                          compiler_params=_params(), name=name)(lr, li, ls, bre, bim, da, dbb, gsum)


def sum_lead(name, x, tr):
    n, rows, cols = x.shape

    def body(x_ref, o_ref):
        acc = x_ref[0]
        for j in range(1, n):
            acc = acc + x_ref[j]
        o_ref[...] = acc

    return pl.pallas_call(
        body, out_shape=jax.ShapeDtypeStruct((rows, cols), x.dtype), grid=(rows // tr,),
        in_specs=[pl.BlockSpec((n, tr, cols), lambda i: (0, i, 0))], out_specs=pl.BlockSpec((tr, cols), lambda i: (i, 0)),
        compiler_params=_params(("parallel",)), name=name)(x)


def _adamw(w, g, m, v):
    m = ADAM_B1 * m + (1.0 - ADAM_B1) * g
    v = ADAM_B2 * v + (1.0 - ADAM_B2) * jnp.square(g)
    m_hat = m / (1.0 - ADAM_B1 ** ADAM_STEP)
    v_hat = v / (1.0 - ADAM_B2 ** ADAM_STEP)
    delta = -ADAM_LR * (m_hat / (jnp.sqrt(v_hat) + ADAM_EPS) + ADAM_WD * w)
    return delta, m, v


def adamw_sharded(name, w, m, v, g0, g1, split_cols, tile):
    _, r, c = w.shape
    if split_cols:
        nt = c // tile
        per = (c // 2) // tile
        wspec = pl.BlockSpec((None, r, tile), lambda l, t: (l, 0, t))
        gspec = pl.BlockSpec((None, r, tile), lambda l, t: (t // per, 0, t % per))
    else:
        nt = r // tile
        per = (r // 2) // tile
        wspec = pl.BlockSpec((None, tile, c), lambda l, t: (l, t, 0))
        gspec = pl.BlockSpec((None, tile, c), lambda l, t: (t // per, t % per, 0))

    def body(w_ref, m_ref, v_ref, g0_ref, g1_ref, g_ref, d_ref, nm_ref, nv_ref):
        g = jnp.where(pl.program_id(0) == 0, g0_ref[...], g1_ref[...])
        d, nm, nv = _adamw(w_ref[...], g, m_ref[...], v_ref[...])
        g_ref[...] = g
        d_ref[...] = d
        nm_ref[...] = nm
        nv_ref[...] = nv

    sds = jax.ShapeDtypeStruct(w.shape, F32)
    return pl.pallas_call(body, out_shape=(sds,) * 4, grid=(2, nt), in_specs=[wspec, wspec, wspec, gspec, gspec],
                          out_specs=(wspec,) * 4, compiler_params=_params(("parallel", "parallel")), name=name)(w, m, v, g0, g1)


def adamw_flat(name, w, g, m, v, tr):
    rows, cols = w.shape

    def body(w_ref, g_ref, m_ref, v_ref, d_ref, nm_ref, nv_ref):
        d, nm, nv = _adamw(w_ref[...], g_ref[...], m_ref[...], v_ref[...])
        d_ref[...] = d
        nm_ref[...] = nm
        nv_ref[...] = nv

    blk = pl.BlockSpec((tr, cols), lambda i: (i, 0))
    sds = jax.ShapeDtypeStruct((rows, cols), F32)
    return pl.pallas_call(body, out_shape=(sds,) * 3, grid=(rows // tr,), in_specs=[blk] * 4, out_specs=(blk,) * 3,
                          compiler_params=_params(("parallel",)), name=name)(w, g, m, v)


def _flips(axes):
    out = []
    for fx in ((0, 1) if "x" in axes else (0,)):
        for fy in ((0, 1) if "y" in axes else (0,)):
            for fc in ((0, 1) if "c" in axes else (0,)):
                if fx or fy or fc:
                    out.append((fx, fy, fc))
    return out


def _slot(pos, axes):
    s = 0
    for name, p in zip(("x", "y", "c"), pos):
        if name in axes:
            s = 2 * s + p
    return s


_HBM = pl.BlockSpec(memory_space=pltpu.HBM)
_SEM = pl.BlockSpec(memory_space=pltpu.SEMAPHORE)
_EFFECT = pltpu.SideEffectType.DATAFLOW_SIDE_EFFECTING


def place_own(name, arrs, axes):
    n = len(_flips(axes)) + 1
    na = len(arrs)

    def body(*refs):
        ins, outs, sems = refs[:na], refs[na:2 * na], refs[2 * na]
        my = _slot((lax.axis_index("x"), lax.axis_index("y"), lax.axis_index("c")), axes)
        copies = [pltpu.make_async_copy(ins[a], outs[a].at[my], sems.at[a]) for a in range(na)]
        for cp in copies:
            cp.start()
        for cp in copies:
            cp.wait()

    out_shape = tuple(jax.ShapeDtypeStruct((n,) + a.shape, a.dtype) for a in arrs)
    anyspec = pl.BlockSpec(memory_space=pl.ANY)
    return pl.pallas_call(body, out_shape=out_shape, in_specs=[anyspec] * na, out_specs=(anyspec,) * na,
                          scratch_shapes=[pltpu.SemaphoreType.DMA((na,))], name=name)(*arrs)


def _peers(axes):
    me = (lax.axis_index("x"), lax.axis_index("y"), lax.axis_index("c"))
    return me, [tuple((1 - p) if f else p for p, f in zip(me, fl)) for fl in _flips(axes)]


def place_tile(name, arr, layer, my, slots=4, dtype=BF16, after=None):
    _, r, cols = arr.shape
    tr = _tile_rows(r, cols)

    def body(my_ref, x_ref, *rest):
        rest[-1][...] = x_ref[...].astype(dtype)

    in_specs = [pl.BlockSpec((None, tr, cols), lambda i, my: (layer, i, 0))]
    args = [arr]
    if after is not None:
        in_specs.append(pl.BlockSpec(after.shape, lambda i, my: (0, 0)))
        args.append(after)
    grid_spec = pltpu.PrefetchScalarGridSpec(num_scalar_prefetch=1, grid=(r // tr,), in_specs=in_specs,
                                             out_specs=pl.BlockSpec((None, tr, cols), lambda i, my: (my[0], i, 0)))
    return pl.pallas_call(body, out_shape=jax.ShapeDtypeStruct((slots, r, cols), dtype), grid_spec=grid_spec,
                          compiler_params=_params(("parallel",)), name=name)(my, *args)


def exchange_start(name, groups, axes, scatter):
    flat = [(p if scatter else (p,)) for grp in groups for p in grp]
    per = 2 if scatter else 1
    na, ng, npeer = len(flat), len(groups), len(_flips(axes))

    def body(*refs):
        ops = refs[:per * na]
        zones = ops[(per - 1) * na:]
        sems, token = refs[per * na:per * na + 2 * ng], refs[-1]
        me, peers = _peers(axes)
        my = _slot(me, axes)
        ai = 0
        for g, grp in enumerate(groups):
            for k in range(len(grp)):
                for j, peer in enumerate(peers):
                    src = ops[ai].at[_slot(peer, axes)] if scatter else zones[ai].at[my]
                    dst = zones[ai].at[j] if scatter else zones[ai].at[my]
                    pltpu.make_async_remote_copy(
                        src_ref=src, dst_ref=dst, send_sem=sems[2 * g].at[k * npeer + j],
                        recv_sem=sems[2 * g + 1].at[k * npeer + j], device_id=peer, device_id_type=pl.DeviceIdType.MESH).start()
                ai += 1
        token[...] = jnp.zeros_like(token)

    out_shape, out_specs = [], []
    for grp in groups:
        out_shape += [pltpu.SemaphoreType.DMA((npeer * len(grp),))] * 2
        out_specs += [_SEM, _SEM]
    for idx in range(per):
        out_shape += [pltpu.HBM(p[idx].shape, p[idx].dtype) for p in flat]
        out_specs += [_HBM] * na
    out_shape.append(jax.ShapeDtypeStruct((SUBLANES, 128), F32))
    out_specs.append(pl.BlockSpec(memory_space=pltpu.VMEM))
    args = [pltpu.with_memory_space_constraint(p[idx], pltpu.HBM) for idx in range(per) for p in flat]
    res = pl.pallas_call(body, out_shape=tuple(out_shape), in_specs=[_HBM] * (per * na), out_specs=tuple(out_specs),
                         input_output_aliases={i: 2 * ng + i for i in range(per * na)},
                         compiler_params=pltpu.CompilerParams(has_side_effects=_EFFECT), name=name)(*args)
    thru = res[2 * ng:2 * ng + per * na]
    out, ai = [], 0
    for g, grp in enumerate(groups):
        srcs = list(thru[ai:ai + len(grp)]) if scatter else []
        zones = list(thru[(per - 1) * na + ai:(per - 1) * na + ai + len(grp)])
        out.append(((res[2 * g], res[2 * g + 1]), srcs, zones))
        ai += len(grp)
    return out, res[-1]


def exchange_wait(name, group, after, axes, scatter):
    (send_sems, recv_sems), srcs, zones = group
    n, ns = len(zones), len(srcs)
    npeer = len(_flips(axes))

    def body(*refs):
        z_refs = refs[ns:ns + n]
        ssem, rsem = refs[ns + n], refs[ns + n + 1]
        _, peers = _peers(axes)
        for k in range(n):
            for j, peer in enumerate(peers):
                part = z_refs[k].at[j if scatter else _slot(peer, axes)]
                copy = pltpu.make_async_remote_copy(
                    src_ref=part, dst_ref=part, send_sem=ssem.at[k * npeer + j], recv_sem=rsem.at[k * npeer + j],
                    device_id=peer, device_id_type=pl.DeviceIdType.MESH)
                copy.wait_send()
                copy.wait_recv()

    ops = list(srcs) + list(zones)
    out_shape = tuple(pltpu.HBM(a.shape, a.dtype) for a in ops)
    res = pl.pallas_call(body, out_shape=out_shape, in_specs=[_HBM] * len(ops) + [_SEM, _SEM, pl.BlockSpec(memory_space=pl.ANY)],
                         out_specs=(_HBM,) * len(ops), input_output_aliases={i: i for i in range(len(ops))},
                         compiler_params=pltpu.CompilerParams(has_side_effects=_EFFECT), name=name)(*ops, send_sems, recv_sems, after)
    return list(res[:ns]), list(res[ns:])


def _pair_exchange(name, ins, in_specs, n_steps, tile, fn_send, fn_out, out_shape, out_spec, prefetch=None, wire=F32):
    n_in = len(ins)

    def body(*refs):
        if prefetch is not None:
            refs = refs[1:]
        in_refs, o_ref = refs[:n_in], refs[n_in]
        send_buf, recv_buf, send_sems, recv_sems, credit = refs[n_in + 1:]
        i = pl.program_id(0)
        slot = lax.rem(i, 2)
        c = lax.axis_index("c")
        sibling = (lax.axis_index("x"), lax.axis_index("y"), 1 - c)
        vals = [r[...] for r in in_refs]
        send_buf[slot] = fn_send(*vals, c).astype(wire)

        @pl.when(i >= 2)
        def _():
            pl.semaphore_wait(credit, 1)

        copy = pltpu.make_async_remote_copy(
            src_ref=send_buf.at[slot], dst_ref=recv_buf.at[slot], send_sem=send_sems.at[slot], recv_sem=recv_sems.at[slot],
            device_id=sibling, device_id_type=pl.DeviceIdType.MESH)
        copy.start()
        copy.wait_recv()
        o_ref[...] = fn_out(*vals, recv_buf[slot], c).astype(o_ref.dtype)
        copy.wait_send()

        @pl.when(i < n_steps - 2)
        def _():
            pl.semaphore_signal(credit, inc=1, device_id=sibling, device_id_type=pl.DeviceIdType.MESH)

    scratch = [pltpu.VMEM((2,) + tile, wire), pltpu.VMEM((2,) + tile, wire), pltpu.SemaphoreType.DMA((2,)),
               pltpu.SemaphoreType.DMA((2,)), pltpu.SemaphoreType.REGULAR]
    if prefetch is None:
        return pl.pallas_call(body, out_shape=out_shape, grid=(n_steps,), in_specs=in_specs, out_specs=out_spec,
                              scratch_shapes=scratch, compiler_params=_params(("arbitrary",)), name=name)(*ins)
    grid_spec = pltpu.PrefetchScalarGridSpec(num_scalar_prefetch=1, grid=(n_steps,), in_specs=in_specs, out_specs=out_spec,
                                             scratch_shapes=scratch)
    return pl.pallas_call(body, out_shape=out_shape, grid_spec=grid_spec, compiler_params=_params(("arbitrary",)),
                          name=name)(prefetch, *ins)


def _tile_rows(rows, cols, f32_bytes=3 << 19):
    return _row_tile(rows, max(2 * SUBLANES, f32_bytes // (4 * cols)), 2 * SUBLANES)


def pair_sum(name, x):
    rows, cols = x.shape
    tr = _tile_rows(rows, cols)
    return _pair_exchange(name, [x], [pl.BlockSpec((tr, cols), lambda i: (i, 0))], rows // tr, (tr, cols),
                          lambda v, c: v, lambda v, got, c: v + got, jax.ShapeDtypeStruct((rows, cols), F32),
                          pl.BlockSpec((tr, cols), lambda i: (i, 0)))


def reduce_cores(name, g):
    _, m, cols = g.shape
    tr = _tile_rows(m, cols, 6 << 20)

    def fn_send(g0, g1, c):
        return jnp.where(c == 0, g1, g0)

    def fn_out(g0, g1, got, c):
        return jnp.where(c == 0, g0, g1) + got.astype(F32)

    return _pair_exchange(
        name, [g, g], [pl.BlockSpec((None, tr, cols), lambda i: (0, i, 0)), pl.BlockSpec((None, tr, cols), lambda i: (1, i, 0))],
        m // tr, (tr, cols), fn_send, fn_out, jax.ShapeDtypeStruct((m, cols), BF16), pl.BlockSpec((tr, cols), lambda i: (i, 0)),
        wire=BF16)


def sum_and_share(name, own, parts, my):
    n, r, cols = parts.shape
    tr = _tile_rows(r, cols, 3 << 20)

    def total(o, p):
        acc = o.astype(F32)
        for j in range(n):
            acc = acc + p[j].astype(F32)
        return acc

    def fn_send(o, p, c):
        return total(o, p)

    def fn_out(o, p, got, c):
        mine = total(o, p)
        return jnp.stack([jnp.where(c == 0, mine, got), jnp.where(c == 0, got, mine)])

    return _pair_exchange(
        name, [own, parts], [pl.BlockSpec((None, tr, cols), lambda i, my_ref: (my_ref[0], i, 0)), pl.BlockSpec((n, tr, cols), lambda i, my_ref: (0, i, 0))],
        r // tr, (tr, cols), fn_send, fn_out, jax.ShapeDtypeStruct((2, r, cols), F32),
        pl.BlockSpec((2, tr, cols), lambda i, my_ref: (0, i, 0)), prefetch=my)


def _block_diag(blocks):
    g, r, c = blocks.shape
    eye = jnp.eye(g, dtype=blocks.dtype)
    return (blocks[:, :, None, :] * eye[:, None, :, None]).reshape(g * r, g * c)


def _diag_blocks(mat, g):
    r, c = mat.shape[0] // g, mat.shape[1] // g
    eye = jnp.eye(g, dtype=mat.dtype)
    return (mat.reshape(g, r, g, c) * eye[:, None, :, None]).sum(axis=2)


def _halves(gfull, shards):
    rows, cols = gfull.shape
    return gfull.reshape(shards, 2, rows // shards // 2, cols).transpose(1, 0, 2, 3)


def _step(inp):
    x = inp['x'][0]
    target = inp['loss_target'][0]
    rows, d = x.shape
    depth = inp['w_in'].shape[0]
    mix_w = d // 4
    n_state = S5_GROUPS * S5_STATE
    ffn_half = inp['ffn_w_up'].shape[2]
    tm = min(512, rows)
    tc = min(256, rows)
    xy = ("x", "y")

    my_chip = (2 * lax.axis_index("x") + lax.axis_index("y")).astype(jnp.int32).reshape(1)
    small_keys = [(nme, None) for nme in SMALL_SHARDED]
    group_keys = []
    for l in range(depth):
        group_keys += [[('w_in', l)] + (small_keys if l == 0 else []),
                       [('w_out', l), ('s5_w_glu', l), ('cv_w_pw', l)], [('ffn_w_up', l)], [('ffn_w_down', l)]]

    def zone_of(key, after=None):
        nme, l = key
        src = jnp.swapaxes(inp[nme], 1, 2) if nme == 'ffn_w_up' else inp[nme]
        return place_tile(f"place_{nme}{l}", src, l, my_chip, after=after)

    zones = {('w_in', 0): zone_of(('w_in', 0))}
    zones.update(zip(small_keys, place_own("place_small", [inp[nme] for nme in SMALL_SHARDED], xy)))
    first_group, first_token = exchange_start("gather_start_first", [[zones[key] for key in group_keys[0]]], xy, False)
    for grp in group_keys[1:]:
        zones.update({key: zone_of(key, first_token) for key in grp})
    rest_groups, gather_token = exchange_start("gather_start", [[zones[key] for key in grp] for grp in group_keys[1:]], xy, False)
    gather_groups = first_group + rest_groups

    def gathered(gi, after):
        return dict(zip(group_keys[gi], exchange_wait(f"gather_wait{gi}", gather_groups[gi], after, xy, False)[1]))

    def full_small(g):
        return g.transpose(1, 2, 0, 3).reshape(g.shape[1], g.shape[2], 4 * g.shape[3])

    gsum = jnp.repeat(jnp.eye(128, dtype=F32)[:S5_GROUPS], S5_STATE, axis=0)

    saved = []
    grads = {nme: [None] * depth for nme in WEIGHTS}
    xcur = x
    for l in range(depth):
        vec = lambda a: a[l].reshape(1, -1)
        gain = vec(inp['norm_mix_g']) + (gather_token[0, 0] if l == 0 else 0.0)
        h = rms_fwd(f"rms_mix{l}", xcur, gain, tm)
        got = gathered(4 * l, h)
        w_in = got[('w_in', l)]
        if l == 0:
            cv_w_dw, lru_w_conv, ffn_w_dw = (full_small(got[(nme, None)]) for nme in ('cv_w_dw', 'lru_w_conv', 'ffn_w_dw'))
        ncol = w_in.shape[2]

        lam_re, lam_im = vec(inp['s5_lam_re']), vec(inp['s5_lam_im'])
        log_step = jnp.broadcast_to(inp['s5_log_step'][l][:, None], (S5_GROUPS, S5_STATE)).reshape(1, n_state)
        b_re = _block_diag(inp['s5_b_re'][l].transpose(0, 2, 1))
        b_im = _block_diag(inp['s5_b_im'][l].transpose(0, 2, 1))
        c_cat = jnp.stack([_block_diag(inp['s5_c_re'][l].transpose(0, 2, 1)),
                           -_block_diag(inp['s5_c_im'][l].transpose(0, 2, 1))]).astype(BF16)
        a_bar, b_bar = s5_param_fwd(f"s5_param_fwd{l}", lam_re, lam_im, log_step, b_re, b_im)
        w_r = _block_diag(inp['lru_w_r'][l]).astype(BF16)
        w_i = _block_diag(inp['lru_w_i'][l]).astype(BF16)
        pool_bd = _block_diag(inp['pool_w'][l]).astype(BF16)
        gate_pars = [w_r, w_i, vec(inp['lru_b_r']), vec(inp['lru_b_i']), vec(inp['lru_lam'])]

        proj = _mm(f"proj{l}", h, w_in, jax.ShapeDtypeStruct((rows, 4 * ncol), F32), (4, rows // tm),
                   pl.BlockSpec((tm, d), lambda j, i: (i, 0)), pl.BlockSpec((None, d, ncol), lambda j, i: (j, 0, 0)),
                   pl.BlockSpec((tm, ncol), lambda j, i: (i, j)), NN)
        proj3 = proj.reshape(1, rows, 4 * ncol)
        ts = min(2048, rows)
        cw, sw = mix_w // 4, n_state // 4
        bu = _mm(f"s5_bu{l}", proj, b_bar, jax.ShapeDtypeStruct((2, rows, n_state), F32), (rows // ts, 2, 4),
                 pl.BlockSpec((ts, cw), lambda i, c, s: (i, s)), pl.BlockSpec((None, cw, sw), lambda i, c, s: (c, s, s)),
                 pl.BlockSpec((None, ts, sw), lambda i, c, s: (c, i, s)), NN)
        z = s5_scan_fwd(f"s5_scan{l}", bu, a_bar)
        y_ssm = _mm(f"s5_read{l}", z, c_cat, jax.ShapeDtypeStruct((rows, mix_w), F32), (rows // ts, 4, 2),
                    pl.BlockSpec((None, ts, sw), lambda i, s, c: (c, i, s)), pl.BlockSpec((None, sw, cw), lambda i, s, c: (c, s, s)),
                    pl.BlockSpec((ts, cw), lambda i, s, c: (i, s)), NN, k_axis=2)
        (h0,) = _rowwise(f"cv_glu{l}", _glu, [(proj, 1, mix_w), (proj, 2, mix_w)], [], 1, [(mix_w, F32)], tm)
        h1 = dwconv_fwd(f"cv_conv{l}", h0.reshape(1, rows, mix_w), 0, mix_w, cv_w_dw[l][None], vec(inp['cv_b_dw'])[None],
                        CV_TAPS, tc)[0]
        xc = dwconv_fwd(f"lru_conv{l}", proj3, 3, mix_w, lru_w_conv[l][None], vec(inp['lru_b_conv'])[None], LRU_TAPS, tc)[0]
        a_t, b_t = _rowwise(f"lru_gate{l}", _lru_gate, [(xc, 0, mix_w)], gate_pars, 2, [(mix_w, F32), (mix_w, F32)], tm)
        hseq = lru_scan_fwd(f"lru_scan{l}", a_t, b_t)
        dgp = pool_fwd(f"pool{l}", proj, 5, tc)
        got = gathered(4 * l + 1, proj)
        w_out = got[('w_out', l)].reshape(d, d)
        w_glu, w_pw = got[('s5_w_glu', l)].reshape(mix_w, mix_w), got[('cv_w_pw', l)].reshape(mix_w, mix_w)
        post_pars = [vec(inp['s5_d']), w_glu, vec(inp['s5_b_glu']), vec(inp['cv_ln_g']), vec(inp['cv_ln_b']), w_pw,
                     vec(inp['cv_b_pw']), pool_bd, vec(inp['pool_scale'])]
        post_rows = [(y_ssm, 0, mix_w), (proj, 0, mix_w), (h1, 0, mix_w), (hseq, 0, mix_w), (proj, 4, mix_w), (dgp, 0, mix_w)]
        (mixed,) = _rowwise(f"mix_post{l}", _mix_post, post_rows, post_pars, 1, [(d, BF16)], tm)
        x1 = _mm(f"out_proj{l}", mixed, w_out, jax.ShapeDtypeStruct((rows, d), F32), (2, rows // tm),
                 pl.BlockSpec((tm, d), lambda j, i: (i, 0)), pl.BlockSpec((d, d // 2), lambda j, i: (0, j)),
                 pl.BlockSpec((tm, d // 2), lambda j, i: (i, j)), NN,
                 add=xcur, add_spec=pl.BlockSpec((tm, d // 2), lambda j, i: (i, j)))

        h2 = rms_fwd(f"rms_ffn{l}", x1, vec(inp['norm_ffn_g']), tm)
        tu = min(256, rows)
        w_up = gathered(4 * l + 2, x1)[('ffn_w_up', l)]
        up = _mm(f"ffn_up{l}", h2, w_up, jax.ShapeDtypeStruct((4, rows, ffn_half), F32), (4, rows // tu),
                 pl.BlockSpec((tu, d), lambda k, i: (i, 0)), pl.BlockSpec((None, ffn_half, d), lambda k, i: (k, 0, 0)),
                 pl.BlockSpec((None, tu, ffn_half), lambda k, i: (k, i, 0)), NT)
        w_dw = ffn_w_dw[l].reshape(FFN_TAPS, 2, ffn_half).transpose(1, 0, 2)
        b_dw = inp['ffn_b_dw'][l].reshape(2, 1, ffn_half)
        act = ffn_gate_fwd(f"ffn_gate{l}", up, w_dw, b_dw, tc)
        w_down = gathered(4 * l + 3, up)[('ffn_w_down', l)].reshape(2, ffn_half, d)
        x2 = _mm(f"ffn_down{l}", act, w_down, jax.ShapeDtypeStruct((rows, d), F32), (rows // tm, 4),
                 pl.BlockSpec((2, tm, ffn_half), lambda i, j: (0, i, 0)), pl.BlockSpec((2, ffn_half, d // 4), lambda i, j: (0, 0, j)),
                 pl.BlockSpec((tm, d // 4), lambda i, j: (i, j)), NN, inner=("lead", 2),
                 add=x1, add_spec=pl.BlockSpec((tm, d // 4), lambda i, j: (i, j)))
        saved.append(dict(x=xcur, h=h, proj=proj, z=z, y_ssm=y_ssm, h0=h0, h1=h1, xc=xc, a_t=a_t, hseq=hseq, dgp=dgp,
                          mixed=mixed, x1=x1, h2=h2, up=up, act=act, w_in=w_in, w_out=w_out, w_up=w_up, w_down=w_down,
                          a_bar=a_bar, b_bar=b_bar, c_cat=c_cat, post_pars=post_pars, gate_pars=gate_pars, w_dw=w_dw, b_dw=b_dw,
                          s5=(lam_re, lam_im, log_step, b_re, b_im), cv_w=cv_w_dw[l][None], lru_w=lru_w_conv[l][None]))
        xcur = x2

    loss_row, dx, dg_final = final_loss("final_loss", xcur, inp['norm_final_g'].reshape(1, d), target, tm)
    grads['norm_final_g'] = dg_final.reshape(d)

    big_g = {nme: [None] * depth for nme in BIG}
    reduce_groups = []

    def start_reduce(tag, keys):
        pieces = []
        for nme, lyr in keys:
            g = big_g[nme][lyr]
            if nme == 'ffn_w_down':
                g = g.reshape(2, 4, ffn_half // 2, d // 2)
            pieces.append(reduce_cores(f"reduce_cores_{nme}{lyr}", g.reshape(2, -1, g.shape[-1])).reshape(g.shape[1:]))
        landing = [lax.empty((3,) + p.shape[1:], p.dtype) for p in pieces]
        groups, token = exchange_start(f"reduce_start_{tag}", [list(zip(pieces, landing))], xy, True)
        reduce_groups.append((tag, keys, groups[0]))
        return token

    for l in reversed(range(depth)):
        s = saved[l]
        ncol = s['w_in'].shape[2]
        tu = min(256, rows)
        dact = _mm(f"d_act{l}", dx, s['w_down'], jax.ShapeDtypeStruct((2, rows, ffn_half), F32), (2, rows // tu),
                   pl.BlockSpec((tu, d), lambda k, i: (i, 0)), pl.BlockSpec((None, ffn_half, d), lambda k, i: (k, 0, 0)),
                   pl.BlockSpec((None, tu, ffn_half), lambda k, i: (k, i, 0)), NT)
        tn = d // 4
        tk = min(1024, rows)
        tkb = min(2048, rows)
        big_g['ffn_w_down'][l] = _mm(
            f"dw_down{l}", s['act'], dx, jax.ShapeDtypeStruct((2, 2, ffn_half, d // 2), F32), (2, 4, rows // tkb),
            pl.BlockSpec((None, tkb, ffn_half), lambda hh, n, k: (hh, k, 0)), pl.BlockSpec((tkb, tn), lambda hh, n, k: (k, n)),
            pl.BlockSpec((None, None, ffn_half, tn), lambda hh, n, k: (n // 2, hh, 0, n % 2)), TN, k_axis=2)
        dup, dw_dw, db_dw = ffn_gate_bwd(f"ffn_gate_bwd{l}", s['up'], dact, s['w_dw'], s['b_dw'], tc)
        grads['ffn_w_dw'][l] = dw_dw.transpose(1, 0, 2).reshape(FFN_TAPS, 2 * ffn_half)
        grads['ffn_b_dw'][l] = db_dw.reshape(2 * ffn_half)
        dup = dup.reshape(4, rows, ffn_half)
        tm2 = min(1024, rows)
        dh2 = _mm(f"d_h2{l}", dup, s['w_up'], jax.ShapeDtypeStruct((rows, d), F32), (rows // tm2, 2, 4),
                  pl.BlockSpec((None, tm2, ffn_half), lambda i, j, k: (k, i, 0)), pl.BlockSpec((None, ffn_half, d // 2), lambda i, j, k: (k, 0, j)),
                  pl.BlockSpec((tm2, d // 2), lambda i, j, k: (i, j)), NN, k_axis=2)
        tmm = d // 4
        big_g['ffn_w_up'][l] = _mm(
            f"dw_up{l}", dup, s['h2'], jax.ShapeDtypeStruct((2, 4, ffn_half, d // 2), F32), (4, 4, rows // tkb),
            pl.BlockSpec((None, tkb, ffn_half), lambda k4, n, k: (k4, k, 0)), pl.BlockSpec((tkb, tn), lambda k4, n, k: (k, n)),
            pl.BlockSpec((None, None, ffn_half, tn), lambda k4, n, k: (n // 2, k4, 0, n % 2)), TN, k_axis=2)
        token = start_reduce(f"ffn{l}", [('ffn_w_down', l), ('ffn_w_up', l)])
        dx1, dg = rms_bwd(f"rms_ffn_bwd{l}", s['x1'], inp['norm_ffn_g'][l].reshape(1, d) + token[0, 0], dh2, dx, tm)
        grads['norm_ffn_g'][l] = dg.reshape(d)
        dmixed = _mm(f"d_mixed{l}", dx1, s['w_out'], jax.ShapeDtypeStruct((rows, d), F32), (rows // tm, 4),
                     pl.BlockSpec((tm, d), lambda i, j: (i, 0)), pl.BlockSpec((d // 4, d), lambda i, j: (j, 0)),
                     pl.BlockSpec((tm, d // 4), lambda i, j: (i, j)), NT)
        tq = mix_w // 2
        big_g['w_out'][l] = _mm(
            f"dw_out{l}", s['mixed'], dx1, jax.ShapeDtypeStruct((2, 4, tq, d), F32), (4, rows // tk),
            pl.BlockSpec((tk, 2 * tq), lambda t, k: (k, t)), pl.BlockSpec((tk, d), lambda t, k: (k, 0)),
            pl.BlockSpec((2, None, tq, d), lambda t, k: (0, t, 0, 0)), TN, k_axis=1)
        post_rows = [(s['y_ssm'], 0, mix_w), (s['proj'], 0, mix_w), (s['h1'], 0, mix_w), (s['hseq'], 0, mix_w),
                     (s['proj'], 4, mix_w), (s['dgp'], 0, mix_w), (dmixed, 0, d)]
        res = _rowwise(f"mix_post_bwd{l}", _mix_post, post_rows, s['post_pars'], 1, [(mix_w, F32)] * 6, tm, with_grads=True)
        dy_ssm, du_dir, dh1, dhseq, dlru_g, ddgp = res[:6]
        dd, dwglu, dbglu, dlng, dlnb, dwpw, dbpw, dpoolbd, dscale = res[6:]
        grads['s5_d'][l], grads['s5_b_glu'][l] = dd.reshape(mix_w), dbglu.reshape(mix_w)
        grads['cv_ln_g'][l], grads['cv_ln_b'][l], grads['cv_b_pw'][l] = dlng.reshape(mix_w), dlnb.reshape(mix_w), dbpw.reshape(mix_w)
        grads['pool_w'][l] = _diag_blocks(dpoolbd, len(POOL_WINDOWS))
        grads['pool_scale'][l] = dscale.reshape(mix_w)
        big_g['s5_w_glu'][l] = _halves(dwglu, 4)
        big_g['cv_w_pw'][l] = _halves(dwpw, 4)
        ts = min(2048, rows)
        cw, sw = mix_w // 4, n_state // 4
        slab = jnp.arange(mix_w)[:, None] // cw == jnp.arange(n_state)[None, :] // sw
        dz = _mm(f"s5_dz{l}", dy_ssm, s['c_cat'], jax.ShapeDtypeStruct((2, rows, n_state), F32), (rows // ts, 2, 4),
                 pl.BlockSpec((ts, cw), lambda i, c, q: (i, q)), pl.BlockSpec((None, sw, cw), lambda i, c, q: (c, q, q)),
                 pl.BlockSpec((None, ts, sw), lambda i, c, q: (c, i, q)), NT)
        dccat = _mm(f"s5_dc{l}", s['z'], dy_ssm, jax.ShapeDtypeStruct((2, n_state, mix_w), F32), (2, 4, rows // tk),
                    pl.BlockSpec((None, tk, sw), lambda c, q, k: (c, k, q)), pl.BlockSpec((tk, cw), lambda c, q, k: (k, q)),
                    pl.BlockSpec((None, sw, cw), lambda c, q, k: (c, q, q)), TN, k_axis=2)
        dccat = jnp.where(slab.T, dccat, 0.0)
        grads['s5_c_re'][l] = _diag_blocks(dccat[0], S5_GROUPS).transpose(0, 2, 1)
        grads['s5_c_im'][l] = -_diag_blocks(dccat[1], S5_GROUPS).transpose(0, 2, 1)
        lam, da_bar = s5_scan_bwd(f"s5_scan_bwd{l}", dz, s['z'], s['a_bar'])
        du = _mm(f"s5_du{l}", lam, s['b_bar'], jax.ShapeDtypeStruct((rows, mix_w), F32), (rows // ts, 4, 2),
                 pl.BlockSpec((None, ts, sw), lambda i, q, c: (c, i, q)), pl.BlockSpec((None, cw, sw), lambda i, q, c: (c, q, q)),
                 pl.BlockSpec((ts, cw), lambda i, q, c: (i, q)), NT, k_axis=2,
                 add=du_dir, add_spec=pl.BlockSpec((ts, cw), lambda i, q, c: (i, q)))
        dbbar = _mm(f"s5_db{l}", s['proj'], lam, jax.ShapeDtypeStruct((2, mix_w, n_state), F32), (2, 4, rows // tk),
                    pl.BlockSpec((tk, cw), lambda c, q, k: (k, q)), pl.BlockSpec((None, tk, sw), lambda c, q, k: (c, k, q)),
                    pl.BlockSpec((None, cw, sw), lambda c, q, k: (c, q, q)), TN, k_axis=2)
        dbbar = jnp.where(slab, dbbar, 0.0)
        dlr, dli, dls, dbre, dbim = s5_param_bwd(f"s5_param_bwd{l}", *s['s5'], da_bar, dbbar, gsum)
        grads['s5_lam_re'][l] = dlr.reshape(S5_GROUPS, S5_STATE)
        grads['s5_lam_im'][l] = dli.reshape(S5_GROUPS, S5_STATE)
        grads['s5_log_step'][l] = dls[0, :S5_GROUPS]
        grads['s5_b_re'][l] = _diag_blocks(dbre, S5_GROUPS).transpose(0, 2, 1)
        grads['s5_b_im'][l] = _diag_blocks(dbim, S5_GROUPS).transpose(0, 2, 1)
        dh0, dw_cv, db_cv = dwconv_bwd(f"cv_conv_bwd{l}", dh1.reshape(1, rows, mix_w), s['h0'].reshape(1, rows, mix_w), 0, mix_w,
                                       s['cv_w'], CV_TAPS, tc)
        grads['cv_w_dw'][l], grads['cv_b_dw'][l] = dw_cv[0], db_cv.reshape(mix_w)
        dv, dgg = _rowwise(f"cv_glu_bwd{l}", _glu, [(s['proj'], 1, mix_w), (s['proj'], 2, mix_w), (dh0[0], 0, mix_w)], [], 1,
                           [(mix_w, F32)] * 2, tm, with_grads=True)
        da_t, db_t = lru_scan_bwd(f"lru_scan_bwd{l}", dhseq, s['a_t'], s['hseq'])
        res = _rowwise(f"lru_gate_bwd{l}", _lru_gate, [(s['xc'], 0, mix_w), (da_t, 0, mix_w), (db_t, 0, mix_w)], s['gate_pars'], 2,
                       [(mix_w, F32)], tm, with_grads=True)
        dxc, dwr, dwi, dbr, dbi, dlam = res
        grads['lru_w_r'][l], grads['lru_w_i'][l] = _diag_blocks(dwr, LRU_HEADS), _diag_blocks(dwi, LRU_HEADS)
        grads['lru_b_r'][l], grads['lru_b_i'][l], grads['lru_lam'][l] = dbr.reshape(mix_w), dbi.reshape(mix_w), dlam.reshape(mix_w)
        dlx, dw_lc, db_lc = dwconv_bwd(f"lru_conv_bwd{l}", dxc.reshape(1, rows, mix_w), s['proj'].reshape(1, rows, 4 * ncol), 3, mix_w,
                                       s['lru_w'], LRU_TAPS, tc)
        grads['lru_w_conv'][l], grads['lru_b_conv'][l] = dw_lc[0], db_lc.reshape(mix_w)
        dpx = pool_bwd(f"pool_bwd{l}", ddgp, tc)
        dproj = jnp.concatenate([du, dv, dgg, dlx[0], dlru_g, dpx], axis=-1)
        dh = _mm(f"d_h{l}", dproj, s['w_in'], jax.ShapeDtypeStruct((rows, d), F32), (rows // tm, 4),
                 pl.BlockSpec((tm, 4 * ncol), lambda i, j: (i, 0)), pl.BlockSpec((4, d // 4, ncol), lambda i, j: (0, j, 0)),
                 pl.BlockSpec((tm, d // 4), lambda i, j: (i, j)), NT, inner=("cols", 4))
        tk2 = min(2048, rows)
        big_g['w_in'][l] = _mm(
            f"dw_in{l}", s['h'], dproj, jax.ShapeDtypeStruct((2, 4, d // 2, ncol), F32), (4, 2, rows // tk2),
            pl.BlockSpec((tk2, d // 2), lambda k4, m, k: (k, m)), pl.BlockSpec((tk2, ncol), lambda k4, m, k: (k, k4)),
            pl.BlockSpec((None, None, d // 2, ncol), lambda k4, m, k: (m, k4, 0, 0)), TN, k_axis=2)
        token = start_reduce(f"mix{l}", [('w_out', l), ('s5_w_glu', l), ('cv_w_pw', l), ('w_in', l)])
        dx, dg = rms_bwd(f"rms_mix_bwd{l}", s['x'], inp['norm_mix_g'][l].reshape(1, d) + token[0, 0], dh, dx1, tm)
        grads['norm_mix_g'][l] = dg.reshape(d)

    small = [nme for nme in WEIGHTS if nme not in BIG]
    full_g = {nme: (grads[nme] if nme == 'norm_final_g' else jnp.stack(grads[nme])) for nme in small}
    flat = jnp.concatenate([full_g[nme].reshape(-1) for nme in small])
    packed = jnp.pad(flat, (0, (-flat.shape[0]) % (128 * 64))).reshape(-1, 128)
    chip_sum = pair_sum("small_pair_sum", packed)
    small_zone = place_tile("place_small_grads", chip_sum[None], 0, my_chip, dtype=F32)
    (small_group,), small_token = exchange_start("small_start", [[small_zone]], xy, False)

    t_full = {}
    for tag, keys, group in reduce_groups:
        pieces, parts = exchange_wait(f"reduce_wait_{tag}", group, small_token, xy, True)
        for key, own, got in zip(keys, pieces, parts):
            t_full[key] = sum_and_share(f"share_cores_{key[0]}{key[1]}", own, got, my_chip)

    outs, done_big = {}, []
    tiles = {'w_in': 256, 'w_out': 128, 'ffn_w_up': 128, 'ffn_w_down': 256, 's5_w_glu': 64, 'cv_w_pw': 64}
    for nme in BIG:
        g0, g1 = t_full[(nme, 0)], t_full[(nme, 1)]
        if nme == 'ffn_w_up':
            res = adamw_sharded(f"adamw_{nme}", *(jnp.swapaxes(inp[p + nme], 1, 2) for p in ('', 'm_', 'v_')), g0, g1, True, tiles[nme])
            outs[nme] = tuple(jnp.swapaxes(r, 1, 2) for r in res)
        else:
            res = adamw_sharded(f"adamw_{nme}", inp[nme], inp['m_' + nme], inp['v_' + nme], g0, g1, nme == 'ffn_w_down', tiles[nme])
            outs[nme] = res
        done_big.append(res[1][:1, :1, :1].reshape(1))

    after_big = sum(done_big)
    (g4,) = exchange_wait("small_wait", small_group, after_big, xy, False)[1]
    gsum_small = sum_lead("sum_small", g4, 64).reshape(-1)
    red, off = {}, 0
    for nme in small:
        g = gsum_small[off:off + full_g[nme].size].reshape(full_g[nme].shape)
        off += full_g[nme].size
        if nme in SMALL_SHARDED:
            width = inp[nme].shape[2]
            g = lax.dynamic_slice_in_dim(g, my_chip[0] * width, width, axis=2)
        red[nme] = g

    def pack(tree):
        f = jnp.concatenate([tree[nme].reshape(-1) for nme in small])
        return jnp.pad(f, (0, (-f.shape[0]) % (128 * 64))).reshape(-1, 128)

    pd, pm, pv = adamw_flat("adamw_small", pack({n_: inp[n_] for n_ in small}), pack(red), pack({n_: inp['m_' + n_] for n_ in small}),
                            pack({n_: inp['v_' + n_] for n_ in small}), 64)
    off = 0
    for nme in small:
        size, shape = inp[nme].size, inp[nme].shape
        outs[nme] = (red[nme],) + tuple(p.reshape(-1)[off:off + size].reshape(shape) for p in (pd, pm, pv))
        off += size

    loss = lax.psum(loss_row[0, 0], ("x", "y", "c"))
    result = [loss, dx[None]]
    for part in range(4):
        result += [outs[nme][part] for nme in WEIGHTS]
    return tuple(result)


def kernel(x, norm_mix_g, w_in, s5_lam_re, s5_lam_im, s5_log_step, s5_b_re, s5_b_im, s5_c_re, s5_c_im, s5_d, s5_w_glu, s5_b_glu, cv_w_dw, cv_b_dw, cv_ln_g, cv_ln_b, cv_w_pw, cv_b_pw, lru_w_conv, lru_b_conv, lru_w_r, lru_b_r, lru_w_i, lru_b_i, lru_lam, pool_w, pool_scale, w_out, norm_ffn_g, ffn_w_up, ffn_w_dw, ffn_b_dw, ffn_w_down, norm_final_g, loss_target, m_norm_mix_g, m_w_in, m_s5_lam_re, m_s5_lam_im, m_s5_log_step, m_s5_b_re, m_s5_b_im, m_s5_c_re, m_s5_c_im, m_s5_d, m_s5_w_glu, m_s5_b_glu, m_cv_w_dw, m_cv_b_dw, m_cv_ln_g, m_cv_ln_b, m_cv_w_pw, m_cv_b_pw, m_lru_w_conv, m_lru_b_conv, m_lru_w_r, m_lru_b_r, m_lru_w_i, m_lru_b_i, m_lru_lam, m_pool_w, m_pool_scale, m_w_out, m_norm_ffn_g, m_ffn_w_up, m_ffn_w_dw, m_ffn_b_dw, m_ffn_w_down, m_norm_final_g, v_norm_mix_g, v_w_in, v_s5_lam_re, v_s5_lam_im, v_s5_log_step, v_s5_b_re, v_s5_b_im, v_s5_c_re, v_s5_c_im, v_s5_d, v_s5_w_glu, v_s5_b_glu, v_cv_w_dw, v_cv_b_dw, v_cv_ln_g, v_cv_ln_b, v_cv_w_pw, v_cv_b_pw, v_lru_w_conv, v_lru_b_conv, v_lru_w_r, v_lru_b_r, v_lru_w_i, v_lru_b_i, v_lru_lam, v_pool_w, v_pool_scale, v_w_out, v_norm_ffn_g, v_ffn_w_up, v_ffn_w_dw, v_ffn_b_dw, v_ffn_w_down, v_norm_final_g):
    inp = dict(locals())
    return _step(inp)
```

```python
import functools

import jax
import jax.numpy as jnp
from jax import lax
from jax.experimental import pallas as pl
from jax.experimental.pallas import tpu as pltpu

F32 = jnp.float32
BF16 = jnp.bfloat16

VMEM_LIMIT_BYTES = 56 * 1024 * 1024
SUBLANES = 8

EPS = 1e-6
S5_GROUPS, S5_STATE, S5_GROUP_CH = 32, 64, 16
LRU_HEADS, LRU_C = 8, 8.0
POOL_WINDOWS = (2, 4, 8, 16)
CV_TAPS, LRU_TAPS, FFN_TAPS = 31, 4, 3
SCAN_CHUNK = 64
GELU_K0, GELU_K1 = 0.7978845608028654, 0.044715

ADAM_LR, ADAM_B1, ADAM_B2, ADAM_EPS, ADAM_WD, ADAM_STEP = 0.001, 0.9, 0.999, 1e-08, 0.01, 10

NN = ((1,), (0,))
NT = ((1,), (1,))
TN = ((0,), (0,))

WEIGHTS = ['norm_mix_g', 'w_in', 's5_lam_re', 's5_lam_im', 's5_log_step', 's5_b_re', 's5_b_im', 's5_c_re', 's5_c_im',
           's5_d', 's5_w_glu', 's5_b_glu', 'cv_w_dw', 'cv_b_dw', 'cv_ln_g', 'cv_ln_b', 'cv_w_pw', 'cv_b_pw',
           'lru_w_conv', 'lru_b_conv', 'lru_w_r', 'lru_b_r', 'lru_w_i', 'lru_b_i', 'lru_lam', 'pool_w', 'pool_scale',
           'w_out', 'norm_ffn_g', 'ffn_w_up', 'ffn_w_dw', 'ffn_b_dw', 'ffn_w_down', 'norm_final_g']
BIG = ('w_in', 'w_out', 'ffn_w_up', 'ffn_w_down', 's5_w_glu', 'cv_w_pw')
SMALL_SHARDED = {'cv_w_dw': 2, 'lru_w_conv': 2, 'ffn_w_dw': 2}


def _params(sem=None):
    if sem is None:
        return pltpu.CompilerParams(vmem_limit_bytes=VMEM_LIMIT_BYTES)
    return pltpu.CompilerParams(dimension_semantics=sem, vmem_limit_bytes=VMEM_LIMIT_BYTES)


def _row_tile(rows, cap, mult=SUBLANES):
    best = mult
    for t in range(mult, min(rows, cap) + 1, mult):
        if rows % t == 0:
            best = t
    return best


def _bdot(a, b, dims=NN):
    return lax.dot_general(a.astype(BF16), b.astype(BF16), (dims, ((), ())), preferred_element_type=F32)


@jax.custom_vjp
def bdot(a, b):
    return _bdot(a, b)


def _bdot_fwd(a, b):
    return _bdot(a, b), (a, b)


def _bdot_bwd(res, g):
    a, b = res
    return _bdot(g, b, NT).astype(a.dtype), _bdot(a, g, TN).astype(b.dtype)


bdot.defvjp(_bdot_fwd, _bdot_bwd)


def _mm(name, a, b, out_sds, grid, a_spec, b_spec, o_spec, dims, k_axis=None, add=None, add_spec=None, inner=None):
    nk = grid[k_axis] if k_axis is not None else 1
    has_add = add is not None
    acc_shape = tuple(d for d in o_spec.block_shape if d is not None)
    acc_in_out = out_sds.dtype == F32

    def product(a_ref, b_ref):
        if inner is None:
            return _bdot(a_ref[...], b_ref[...], dims)
        kind, n = inner
        width = a_ref.shape[-1] // n
        acc = None
        for j in range(n):
            a_j = a_ref[j] if kind == "lead" else a_ref[:, j * width:(j + 1) * width]
            p = _bdot(a_j, b_ref[j], dims)
            acc = p if acc is None else acc + p
        return acc

    def body(*refs):
        a_ref, b_ref = refs[0], refs[1]
        add_ref = refs[2] if has_add else None
        o_ref = refs[3] if has_add else refs[2]
        prod = product(a_ref, b_ref).reshape(acc_shape)
        if k_axis is None:
            if has_add:
                prod = prod + add_ref[...]
            o_ref[...] = prod.astype(o_ref.dtype)
        else:
            acc_ref = o_ref if acc_in_out else refs[-1]
            k = pl.program_id(k_axis)

            @pl.when(k == 0)
            def _():
                acc_ref[...] = prod

            @pl.when(k > 0)
            def _():
                acc_ref[...] += prod

            if has_add or not acc_in_out:
                @pl.when(k == nk - 1)
                def _():
                    r = acc_ref[...]
                    if has_add:
                        r = r + add_ref[...]
                    o_ref[...] = r.astype(o_ref.dtype)

    sem = tuple("arbitrary" if d == k_axis else "parallel" for d in range(len(grid)))
    in_specs = [a_spec, b_spec] + ([add_spec] if has_add else [])
    args = (a, b) + ((add,) if has_add else ())
    scratch = [pltpu.VMEM(acc_shape, F32)] if (k_axis is not None and not acc_in_out) else []
    return pl.pallas_call(body, out_shape=out_sds, grid=grid, in_specs=in_specs, out_specs=o_spec,
                          scratch_shapes=scratch, compiler_params=_params(sem), name=name)(*args)


def _rms(x, g):
    return x * lax.rsqrt(jnp.mean(x * x, axis=-1, keepdims=True) + EPS) * g


def rms_fwd(name, x, g, tm):
    rows, d = x.shape

    def body(x_ref, g_ref, o_ref):
        o_ref[...] = _rms(x_ref[...], g_ref[...]).astype(BF16)

    return pl.pallas_call(
        body, out_shape=jax.ShapeDtypeStruct((rows, d), BF16), grid=(rows // tm,),
        in_specs=[pl.BlockSpec((tm, d), lambda i: (i, 0)), pl.BlockSpec((1, d), lambda i: (0, 0))],
        out_specs=pl.BlockSpec((tm, d), lambda i: (i, 0)), compiler_params=_params(("parallel",)), name=name)(x, g)


def rms_bwd(name, x, g, dh, dres, tm):
    rows, d = x.shape

    def body(x_ref, g_ref, dh_ref, dres_ref, dx_ref, dg_ref):
        xv, dy = x_ref[...], dh_ref[...]
        r = lax.rsqrt(jnp.mean(xv * xv, axis=-1, keepdims=True) + EPS)
        dyg = dy * g_ref[...]
        s = jnp.mean(dyg * xv, axis=-1, keepdims=True)
        dx_ref[...] = r * dyg - xv * (r * r * r * s) + dres_ref[...]

        @pl.when(pl.program_id(0) == 0)
        def _():
            dg_ref[...] = jnp.zeros_like(dg_ref)

        dg_ref[...] += jnp.sum(dy * xv * r, axis=0, keepdims=True)

    row = pl.BlockSpec((tm, d), lambda i: (i, 0))
    vec = pl.BlockSpec((1, d), lambda i: (0, 0))
    return pl.pallas_call(
        body, out_shape=(jax.ShapeDtypeStruct((rows, d), F32), jax.ShapeDtypeStruct((1, d), F32)), grid=(rows // tm,),
        in_specs=[row, vec, row, row], out_specs=(row, vec), compiler_params=_params(("arbitrary",)), name=name)(x, g, dh, dres)


def final_loss(name, x, g, target, tm):
    rows, d = x.shape

    def body(x_ref, g_ref, t_ref, l_ref, dx_ref, dg_ref):
        def f(xv, gv):
            e = _rms(xv, gv) - t_ref[...]
            return 0.5 * jnp.sum(jnp.mean(e * e, axis=-1))

        loss, (dx, dg) = jax.value_and_grad(f, argnums=(0, 1))(x_ref[...], g_ref[...])
        dx_ref[...] = dx

        @pl.when(pl.program_id(0) == 0)
        def _():
            l_ref[...] = jnp.zeros_like(l_ref)
            dg_ref[...] = jnp.zeros_like(dg_ref)

        l_ref[...] += jnp.full(l_ref.shape, loss, F32)
        dg_ref[...] += dg

    row = pl.BlockSpec((tm, d), lambda i: (i, 0))
    vec = pl.BlockSpec((1, d), lambda i: (0, 0))
    lspec = pl.BlockSpec((1, 128), lambda i: (0, 0))
    return pl.pallas_call(
        body, out_shape=(jax.ShapeDtypeStruct((1, 128), F32), jax.ShapeDtypeStruct((rows, d), F32), jax.ShapeDtypeStruct((1, d), F32)),
        grid=(rows // tm,), in_specs=[row, vec, row], out_specs=(lspec, row, vec),
        compiler_params=_params(("arbitrary",)), name=name)(x, g, target)


def _rowwise(name, fn, row_ins, par_ins, n_row_out, row_out_dtypes, tm, with_grads=False):
    rows = row_ins[0][0].shape[0]
    n_prim = len(row_ins) - (n_row_out if with_grads else 0)
    n_par = len(par_ins)

    def body(*refs):
        ins = [r[...] for r in refs[:len(row_ins) + n_par]]
        outs = refs[len(row_ins) + n_par:]
        prim, cts, pars = ins[:n_prim], ins[n_prim:len(row_ins)], ins[len(row_ins):]
        if not with_grads:
            res = fn(*prim, *pars)
            for o_ref, r in zip(outs, res):
                o_ref[...] = r.astype(o_ref.dtype)
            return
        _, vjp = jax.vjp(fn, *prim, *[p.astype(F32) for p in pars])
        grads = vjp(tuple(cts))
        for o_ref, gr in zip(outs[:n_prim], grads[:n_prim]):
            o_ref[...] = gr.astype(o_ref.dtype)

        @pl.when(pl.program_id(0) == 0)
        def _():
            for o_ref in outs[n_prim:]:
                o_ref[...] = jnp.zeros_like(o_ref)

        for o_ref, gr in zip(outs[n_prim:], grads[n_prim:]):
            o_ref[...] += gr.astype(F32)

    in_specs = [pl.BlockSpec((tm, w), (lambda i, c=c: (i, c))) for (_, c, w) in row_ins]
    in_specs += [pl.BlockSpec(p.shape, (lambda i, n=p.ndim: (0,) * n)) for p in par_ins]
    args = [a for (a, _, _) in row_ins] + list(par_ins)
    if not with_grads:
        out_shape = tuple(jax.ShapeDtypeStruct((rows, w), dt) for (w, dt) in row_out_dtypes)
        out_specs = tuple(pl.BlockSpec((tm, w), lambda i: (i, 0)) for (w, _) in row_out_dtypes)
        sem = ("parallel",)
    else:
        out_shape = tuple(jax.ShapeDtypeStruct((rows, w), dt) for (w, dt) in row_out_dtypes)
        out_shape += tuple(jax.ShapeDtypeStruct(p.shape, F32) for p in par_ins)
        out_specs = tuple(pl.BlockSpec((tm, w), lambda i: (i, 0)) for (w, _) in row_out_dtypes)
        out_specs += tuple(pl.BlockSpec(p.shape, (lambda i, n=p.ndim: (0,) * n)) for p in par_ins)
        sem = ("arbitrary",)
    return pl.pallas_call(body, out_shape=out_shape, grid=(rows // tm,), in_specs=in_specs, out_specs=out_specs,
                          compiler_params=_params(sem), name=name)(*args)


def _glu(v, g):
    return (v * jax.nn.sigmoid(g),)


def _neg_expm1(z):
    return -jnp.tanh(0.5 * z) * (jnp.exp(z) + 1.0)


def _lru_gate(xc, w_r, w_i, b_r, b_i, lam):
    r = jax.nn.sigmoid(bdot(xc, w_r) + b_r)
    i = jax.nn.sigmoid(bdot(xc, w_i) + b_i)
    log_a = -LRU_C * r * jax.nn.softplus(-lam)
    a = jnp.exp(log_a)
    mult = jnp.sqrt(_neg_expm1(2.0 * log_a))
    return a, mult * (i * xc)


def _layernorm(x, g, b):
    mu = jnp.mean(x, axis=-1, keepdims=True)
    var = jnp.mean(jnp.square(x - mu), axis=-1, keepdims=True)
    return (x - mu) * lax.rsqrt(var + EPS) * g + b


def _mix_post(y_ssm, u, h1, hseq, lru_g, dgp, s5_d, w_glu, b_glu, ln_g, ln_b, w_pw, b_pw, pool_bd, pool_scale):
    y = y_ssm + s5_d * u
    gl = jax.nn.gelu(y, approximate=True)
    out_s5 = gl * jax.nn.sigmoid(bdot(gl, w_glu) + b_glu)
    out_cv = bdot(jax.nn.silu(_layernorm(h1, ln_g, ln_b)), w_pw) + b_pw
    out_lru = hseq * jax.nn.gelu(lru_g, approximate=True)
    out_pool = bdot(dgp, pool_bd) * pool_scale
    return (jnp.concatenate([out_s5, out_cv, out_lru, out_pool], axis=-1),)


def _gelu_terms(x):
    sq = x * x
    t = jnp.tanh(x * (GELU_K0 + (GELU_K0 * GELU_K1) * sq))
    return sq, t, 0.5 + 0.5 * t


def _halo_rows(taps):
    return -(-(taps - 1) // SUBLANES) * SUBLANES


def _row_windows(ext, offsets, n, shifted_ref=None):
    if shifted_ref is None:
        return {off: ext[off:off + n] for off in offsets}
    room = ext.shape[0] - SUBLANES
    slots, out = {}, {}
    for off in offsets:
        r = off % SUBLANES
        if r == 0:
            out[off] = ext[off:off + n]
            continue
        if r not in slots:
            slots[r] = len(slots)
            shifted_ref[slots[r]] = ext[r:r + room]
        out[off] = shifted_ref[slots[r], off - r:off - r + n, :]
    return out


def _shift_scratch(taps, tm, c):
    return [pltpu.VMEM((SUBLANES - 1, _halo_rows(taps) + tm - SUBLANES, c), F32)] if taps > SUBLANES else []


def dwconv_fwd(name, x, cblk, c, w, b, taps, tm, out_dtype=F32):
    nb = w.shape[0]
    rows = x.shape[1]
    halo = _halo_rows(taps)
    per = tm // halo

    def body(x_ref, h_ref, w_ref, b_ref, o_ref, *shifted):
        i = pl.program_id(1)
        prev = jnp.where(i > 0, h_ref[...], 0.0)
        ext = jnp.concatenate([prev, x_ref[...]], axis=0)
        win = _row_windows(ext, [halo - (taps - 1) + k for k in range(taps)], tm, *shifted)
        acc = jnp.broadcast_to(b_ref[...], (tm, c))
        for k in range(taps):
            acc = acc + w_ref[k:k + 1, :] * win[halo - (taps - 1) + k]
        o_ref[...] = acc.astype(o_ref.dtype)

    return pl.pallas_call(
        body, out_shape=jax.ShapeDtypeStruct((nb, rows, c), out_dtype), grid=(nb, rows // tm),
        in_specs=[pl.BlockSpec((None, tm, c), lambda n, i: (n, i, cblk)),
                  pl.BlockSpec((None, halo, c), lambda n, i: (n, jnp.maximum(i * per - 1, 0), cblk)),
                  pl.BlockSpec((None, taps, c), lambda n, i: (n, 0, 0)),
                  pl.BlockSpec((None, 1, c), lambda n, i: (n, 0, 0))],
        out_specs=pl.BlockSpec((None, tm, c), lambda n, i: (n, i, 0)), scratch_shapes=_shift_scratch(taps, tm, c),
        compiler_params=_params(("parallel", "parallel")), name=name)(x, x, w, b)


def dwconv_bwd(name, dy, x, cblk, c, w, taps, tm, dx_dtype=F32):
    nb = w.shape[0]
    rows = x.shape[1]
    halo = _halo_rows(taps)
    per = tm // halo
    n_tiles = rows // tm
    last_halo = rows // halo - 1

    def body(dy_ref, dn_ref, x_ref, xp_ref, w_ref, dx_ref, dw_ref, db_ref, *shifted):
        i = pl.program_id(1)
        dyv = dy_ref[...]
        nxt = jnp.where(i < n_tiles - 1, dn_ref[...], 0.0)
        dext = jnp.concatenate([dyv, nxt], axis=0)
        prev = jnp.where(i > 0, xp_ref[...], 0.0)
        xext = jnp.concatenate([prev, x_ref[...]], axis=0)
        acc = jnp.zeros((tm, c), F32)

        @pl.when(i == 0)
        def _():
            dw_ref[...] = jnp.zeros_like(dw_ref)
            db_ref[...] = jnp.zeros_like(db_ref)

        dwin = _row_windows(dext, list(range(taps)), tm, *shifted[:1])
        xwin = _row_windows(xext, [halo - (taps - 1) + k for k in range(taps)], tm, *shifted[1:])
        for k in range(taps):
            acc = acc + w_ref[k:k + 1, :] * dwin[taps - 1 - k]
            dw_ref[k:k + 1, :] += jnp.sum(dyv * xwin[halo - (taps - 1) + k], axis=0, keepdims=True)
        dx_ref[...] = acc.astype(dx_ref.dtype)
        db_ref[...] += jnp.sum(dyv, axis=0, keepdims=True)

    return pl.pallas_call(
        body, out_shape=(jax.ShapeDtypeStruct((nb, rows, c), dx_dtype), jax.ShapeDtypeStruct((nb, taps, c), F32),
                         jax.ShapeDtypeStruct((nb, 1, c), F32)),
        grid=(nb, n_tiles),
        in_specs=[pl.BlockSpec((None, tm, c), lambda n, i: (n, i, 0)),
                  pl.BlockSpec((None, halo, c), lambda n, i: (n, jnp.minimum((i + 1) * per, last_halo), 0)),
                  pl.BlockSpec((None, tm, c), lambda n, i: (n, i, cblk)),
                  pl.BlockSpec((None, halo, c), lambda n, i: (n, jnp.maximum(i * per - 1, 0), cblk)),
                  pl.BlockSpec((None, taps, c), lambda n, i: (n, 0, 0))],
        out_specs=(pl.BlockSpec((None, tm, c), lambda n, i: (n, i, 0)), pl.BlockSpec((None, taps, c), lambda n, i: (n, 0, 0)),
                   pl.BlockSpec((None, 1, c), lambda n, i: (n, 0, 0))),
        scratch_shapes=2 * _shift_scratch(taps, tm, c),
        compiler_params=_params(("parallel", "arbitrary")), name=name)(dy, dy, x, x, w)


def ffn_gate_fwd(name, up, w, b, tm):
    _, rows, c = up.shape
    halo = _halo_rows(FFN_TAPS)
    per = tm // halo

    def body(g_ref, gp_ref, v_ref, w_ref, b_ref, o_ref):
        i = pl.program_id(1)
        ext = jnp.concatenate([jnp.where(i > 0, gp_ref[...], 0.0), g_ref[...]], axis=0)
        gc = jnp.broadcast_to(b_ref[...], (tm, c))
        for k in range(FFN_TAPS):
            off = halo - (FFN_TAPS - 1) + k
            gc = gc + w_ref[k:k + 1, :] * ext[off:off + tm]
        o_ref[...] = (gc * _gelu_terms(gc)[2] * v_ref[...]).astype(BF16)

    return pl.pallas_call(
        body, out_shape=jax.ShapeDtypeStruct((2, rows, c), BF16), grid=(2, rows // tm),
        in_specs=[pl.BlockSpec((None, tm, c), lambda h, i: (h, i, 0)),
                  pl.BlockSpec((None, halo, c), lambda h, i: (h, jnp.maximum(i * per - 1, 0), 0)),
                  pl.BlockSpec((None, tm, c), lambda h, i: (h + 2, i, 0)),
                  pl.BlockSpec((None, FFN_TAPS, c), lambda h, i: (h, 0, 0)), pl.BlockSpec((None, 1, c), lambda h, i: (h, 0, 0))],
        out_specs=pl.BlockSpec((None, tm, c), lambda h, i: (h, i, 0)),
        compiler_params=_params(("parallel", "parallel")), name=name)(up, up, up, w, b)


def ffn_gate_bwd(name, up, dact, w, b, tm):
    _, rows, c = up.shape
    halo = _halo_rows(FFN_TAPS)
    per = tm // halo
    n_tiles = rows // tm
    last_halo = rows // halo - 1
    n_ext = tm + halo

    def body(g_ref, gp_ref, gn_ref, v_ref, vn_ref, d_ref, dn_ref, w_ref, b_ref, dup_ref, dw_ref, db_ref):
        i = pl.program_id(1)
        gext = jnp.concatenate([jnp.where(i > 0, gp_ref[...], 0.0), g_ref[...], gn_ref[...]], axis=0)
        shifted = [gext[halo - (FFN_TAPS - 1) + k:halo - (FFN_TAPS - 1) + k + n_ext] for k in range(FFN_TAPS)]
        gc = jnp.broadcast_to(b_ref[...], (n_ext, c))
        for k in range(FFN_TAPS):
            gc = gc + w_ref[k:k + 1, :] * shifted[k]
        vext = jnp.concatenate([v_ref[...], vn_ref[...]], axis=0)
        dext = jnp.concatenate([d_ref[...], dn_ref[...]], axis=0)
        sq, t, half = _gelu_terms(gc)
        dval = dext * (gc * half)
        dgc = (dext * vext) * (half + (0.5 * gc) * (1.0 - t * t) * (GELU_K0 + (3.0 * GELU_K0 * GELU_K1) * sq))
        r = lax.broadcasted_iota(jnp.int32, (n_ext, c), 0)
        dgc = jnp.where((r < tm) | (i < n_tiles - 1), dgc, 0.0)
        dgate = jnp.zeros((tm, c), F32)
        for k in range(FFN_TAPS):
            dgate = dgate + w_ref[k:k + 1, :] * dgc[FFN_TAPS - 1 - k:FFN_TAPS - 1 - k + tm]
        dup_ref[0] = dgate.astype(BF16)
        dup_ref[1] = dval[:tm].astype(BF16)

        @pl.when(i == 0)
        def _():
            dw_ref[...] = jnp.zeros_like(dw_ref)
            db_ref[...] = jnp.zeros_like(db_ref)

        dgc_t = dgc[:tm]
        for k in range(FFN_TAPS):
            dw_ref[k:k + 1, :] += jnp.sum(dgc_t * shifted[k][:tm], axis=0, keepdims=True)
        db_ref[...] += jnp.sum(dgc_t, axis=0, keepdims=True)

    def tile(shift):
        return pl.BlockSpec((None, tm, c), lambda h, i: (h + shift, i, 0))

    def after(shift):
        return pl.BlockSpec((None, halo, c), lambda h, i: (h + shift, jnp.minimum((i + 1) * per, last_halo), 0))

    return pl.pallas_call(
        body, out_shape=(jax.ShapeDtypeStruct((2, 2, rows, c), BF16), jax.ShapeDtypeStruct((2, FFN_TAPS, c), F32),
                         jax.ShapeDtypeStruct((2, 1, c), F32)),
        grid=(2, n_tiles),
        in_specs=[tile(0), pl.BlockSpec((None, halo, c), lambda h, i: (h, jnp.maximum(i * per - 1, 0), 0)), after(0),
                  tile(2), after(2), tile(0), after(0),
                  pl.BlockSpec((None, FFN_TAPS, c), lambda h, i: (h, 0, 0)), pl.BlockSpec((None, 1, c), lambda h, i: (h, 0, 0))],
        out_specs=(pl.BlockSpec((2, None, tm, c), lambda h, i: (0, h, i, 0)), pl.BlockSpec((None, FFN_TAPS, c), lambda h, i: (h, 0, 0)),
                   pl.BlockSpec((None, 1, c), lambda h, i: (h, 0, 0))),
        compiler_params=_params(("parallel", "arbitrary")), name=name)(up, up, up, up, up, dact, dact, w, b)


POOL_HALO = 16


def pool_fwd(name, proj, cblk, tm):
    rows = proj.shape[0]
    c = 128 * len(POOL_WINDOWS)
    per = tm // POOL_HALO

    def body(x_ref, h_ref, o_ref):
        i = pl.program_id(0)
        xv = x_ref[...]
        ext = jnp.concatenate([jnp.where(i > 0, h_ref[...], 0.0), xv], axis=0)
        t1 = (lax.broadcasted_iota(jnp.int32, (tm, 128), 0) + i * tm + 1).astype(F32)
        outs = []
        for gi, win in enumerate(POOL_WINDOWS):
            seg = ext[:, gi * 128:(gi + 1) * 128]
            s = seg[POOL_HALO:POOL_HALO + tm]
            for j in range(1, win):
                s = s + seg[POOL_HALO - j:POOL_HALO - j + tm]
            outs.append(s / jnp.minimum(t1, float(win)) - xv[:, gi * 128:(gi + 1) * 128])
        o_ref[...] = jnp.concatenate(outs, axis=-1)

    return pl.pallas_call(
        body, out_shape=jax.ShapeDtypeStruct((rows, c), F32), grid=(rows // tm,),
        in_specs=[pl.BlockSpec((tm, c), lambda i: (i, cblk)),
                  pl.BlockSpec((POOL_HALO, c), lambda i: (jnp.maximum(i * per - 1, 0), cblk))],
        out_specs=pl.BlockSpec((tm, c), lambda i: (i, 0)), compiler_params=_params(("parallel",)), name=name)(proj, proj)


def pool_bwd(name, dd, tm):
    rows, c = dd.shape
    per = tm // POOL_HALO
    n_tiles = rows // tm
    last_halo = rows // POOL_HALO - 1

    def body(d_ref, n_ref, o_ref):
        i = pl.program_id(0)
        dv = d_ref[...]
        nxt = jnp.where(i < n_tiles - 1, n_ref[...], 0.0)
        t1 = (lax.broadcasted_iota(jnp.int32, (tm, 128), 0) + i * tm + 1).astype(F32)
        t1n = (lax.broadcasted_iota(jnp.int32, (POOL_HALO, 128), 0) + (i + 1) * tm + 1).astype(F32)
        outs = []
        for gi, win in enumerate(POOL_WINDOWS):
            sl = slice(gi * 128, (gi + 1) * 128)
            q = jnp.concatenate([dv[:, sl] / jnp.minimum(t1, float(win)), nxt[:, sl] / jnp.minimum(t1n, float(win))], axis=0)
            s = q[0:tm]
            for j in range(1, win):
                s = s + q[j:j + tm]
            outs.append(s - dv[:, sl])
        o_ref[...] = jnp.concatenate(outs, axis=-1)

    return pl.pallas_call(
        body, out_shape=jax.ShapeDtypeStruct((rows, c), F32), grid=(n_tiles,),
        in_specs=[pl.BlockSpec((tm, c), lambda i: (i, 0)),
                  pl.BlockSpec((POOL_HALO, c), lambda i: (jnp.minimum((i + 1) * per, last_halo), 0))],
        out_specs=pl.BlockSpec((tm, c), lambda i: (i, 0)), compiler_params=_params(("parallel",)), name=name)(dd, dd)


BLOCK_STEPS = 3


def _shift_down(v, s, fill):
    r = lax.broadcasted_iota(jnp.int32, v.shape, 0)
    return jnp.where(r >= s, pltpu.roll(v, s, 0), fill)


def _shift_up(v, s, fill):
    n = v.shape[0]
    r = lax.broadcasted_iota(jnp.int32, v.shape, 0)
    return jnp.where(r < n - s, pltpu.roll(v, n - s, 0), fill)


def _shift_in_blocks(v, s, fill, reverse):
    n = v.shape[0]
    q = lax.broadcasted_iota(jnp.int32, v.shape, 0) & (SUBLANES - 1)
    if reverse:
        return jnp.where(q < SUBLANES - s, pltpu.roll(v, n - s, 0), fill)
    return jnp.where(q >= s, pltpu.roll(v, s, 0), fill)


def _cscan_blocks(vr, vi, powers, reverse):
    for k, (qr, qi) in enumerate(powers):
        s = 1 << k
        sr, si = _shift_in_blocks(vr, s, 0.0, reverse), _shift_in_blocks(vi, s, 0.0, reverse)
        if reverse:
            vr, vi = vr + qr * sr + qi * si, vi + qr * si - qi * sr
        else:
            vr, vi = vr + qr * sr - qi * si, vi + qr * si + qi * sr
    return vr, vi


def _cscan_table(pr, pi, powers, reverse):
    r = lax.broadcasted_iota(jnp.int32, (SUBLANES, 128), 0)
    at = (SUBLANES - 1) if reverse else 0
    return _cscan_blocks(jnp.where(r == at, pr, 0.0), jnp.where(r == at, -pi if reverse else pi, 0.0), powers, reverse)


def _cscan_chunk(vr, vi, powers, table, carry, reverse):
    vr, vi = _cscan_blocks(vr, vi, powers, reverse)
    tr, ti = table
    cr, ci = carry
    nb = vr.shape[0] // SUBLANES
    outr, outi = [None] * nb, [None] * nb
    edge = 0 if reverse else SUBLANES - 1
    for j in (reversed(range(nb)) if reverse else range(nb)):
        rows = slice(j * SUBLANES, (j + 1) * SUBLANES)
        zr = vr[rows] + tr * cr - ti * ci
        zi = vi[rows] + tr * ci + ti * cr
        outr[j], outi[j] = zr, zi
        cr, ci = zr[edge:edge + 1], zi[edge:edge + 1]
    return jnp.concatenate(outr, axis=0), jnp.concatenate(outi, axis=0), (cr, ci)


def _powers(pr, pi, n):
    out = [(pr, pi)]
    for _ in range(n - 1):
        pr, pi = pr * pr - pi * pi, 2.0 * pr * pi
        out.append((pr, pi))
    return out


def s5_scan_fwd(name, bu, a):
    _, rows, n = bu.shape
    t = min(SCAN_CHUNK, rows)

    def body(bu_ref, a_ref, z_ref):
        pr, pi = a_ref[0], a_ref[1]
        powers = _powers(pr, pi, BLOCK_STEPS)
        table = _cscan_table(pr, pi, powers, False)

        def chunk(ci, carry):
            base = pl.multiple_of(ci * t, t)
            zr, zi, carry = _cscan_chunk(bu_ref[0, pl.ds(base, t), :], bu_ref[1, pl.ds(base, t), :], powers, table, carry, False)
            z_ref[0, pl.ds(base, t), :] = zr
            z_ref[1, pl.ds(base, t), :] = zi
            return carry

        zero = jnp.zeros((1, 128), F32)
        lax.fori_loop(0, rows // t, chunk, (zero, zero))

    return pl.pallas_call(
        body, out_shape=jax.ShapeDtypeStruct((2, rows, n), F32), grid=(n // 128,),
        in_specs=[pl.BlockSpec((2, rows, 128), lambda j: (0, 0, j)), pl.BlockSpec((2, 1, 128), lambda j: (0, 0, j))],
        out_specs=pl.BlockSpec((2, rows, 128), lambda j: (0, 0, j)), compiler_params=_params(("parallel",)), name=name)(bu, a)


def s5_scan_bwd(name, dz, z, a):
    _, rows, n = dz.shape
    t = min(SCAN_CHUNK, rows)
    n_chunks = rows // t

    def body(dz_ref, z_ref, a_ref, lam_ref, da_ref):
        pr, pi = a_ref[0], a_ref[1]
        powers = _powers(pr, pi, BLOCK_STEPS)
        table = _cscan_table(pr, pi, powers, True)

        def chunk(k, carry):
            ci = n_chunks - 1 - k
            base = pl.multiple_of(ci * t, t)
            cr, cim, dar, dai = carry
            lr, li, (cr, cim) = _cscan_chunk(dz_ref[0, pl.ds(base, t), :], dz_ref[1, pl.ds(base, t), :], powers, table, (cr, cim), True)
            lam_ref[0, pl.ds(base, t), :] = lr
            lam_ref[1, pl.ds(base, t), :] = li
            pbase = pl.multiple_of(jnp.maximum(base - SUBLANES, 0), SUBLANES)
            keep = (ci > 0).astype(F32)
            pzr = z_ref[0, pl.ds(pbase, SUBLANES), :][SUBLANES - 1:SUBLANES, :] * keep
            pzi = z_ref[1, pl.ds(pbase, SUBLANES), :][SUBLANES - 1:SUBLANES, :] * keep
            zpr = _shift_down(z_ref[0, pl.ds(base, t), :], 1, pzr)
            zpi = _shift_down(z_ref[1, pl.ds(base, t), :], 1, pzi)
            dar = dar + jnp.sum(lr * zpr + li * zpi, axis=0, keepdims=True)
            dai = dai + jnp.sum(li * zpr - lr * zpi, axis=0, keepdims=True)
            return cr, cim, dar, dai

        zero = jnp.zeros((1, 128), F32)
        _, _, dar, dai = lax.fori_loop(0, n_chunks, chunk, (zero, zero, zero, zero))
        da_ref[0] = dar
        da_ref[1] = dai

    seq = pl.BlockSpec((2, rows, 128), lambda j: (0, 0, j))
    vec = pl.BlockSpec((2, 1, 128), lambda j: (0, 0, j))
    return pl.pallas_call(
        body, out_shape=(jax.ShapeDtypeStruct((2, rows, n), F32), jax.ShapeDtypeStruct((2, 1, n), F32)), grid=(n // 128,),
        in_specs=[seq, seq, vec], out_specs=(seq, vec), compiler_params=_params(("parallel",)), name=name)(dz, z, a)


def _rscan_chunk(a, b, carry, reverse):
    n = a.shape[0]
    shift = _shift_up if reverse else _shift_down
    for k in range(n.bit_length() - 1):
        s = 1 << k
        b = b + a * shift(b, s, 0.0)
        a = a * shift(a, s, 1.0)
    h = b + a * carry
    edge = 0 if reverse else n - 1
    return h, h[edge:edge + 1]


def lru_scan_fwd(name, a, b):
    rows, n = a.shape
    t = min(SCAN_CHUNK, rows)

    def body(a_ref, b_ref, h_ref):
        def chunk(ci, carry):
            base = pl.multiple_of(ci * t, t)
            h, carry = _rscan_chunk(a_ref[pl.ds(base, t), :], b_ref[pl.ds(base, t), :], carry, False)
            h_ref[pl.ds(base, t), :] = h
            return carry

        lax.fori_loop(0, rows // t, chunk, jnp.zeros((1, 128), F32))

    seq = pl.BlockSpec((rows, 128), lambda j: (0, j))
    return pl.pallas_call(body, out_shape=jax.ShapeDtypeStruct((rows, n), F32), grid=(n // 128,), in_specs=[seq, seq],
                          out_specs=seq, compiler_params=_params(("parallel",)), name=name)(a, b)


def lru_scan_bwd(name, dh, a, h):
    rows, n = a.shape
    t = min(SCAN_CHUNK, rows)
    n_chunks = rows // t

    def body(dh_ref, a_ref, h_ref, da_ref, db_ref):
        def chunk(k, carry):
            ci = n_chunks - 1 - k
            base = pl.multiple_of(ci * t, t)
            nbase = pl.multiple_of(jnp.minimum(base + t, rows - SUBLANES), SUBLANES)
            a_next = a_ref[pl.ds(nbase, SUBLANES), :][0:1, :]
            an = _shift_up(a_ref[pl.ds(base, t), :], 1, a_next)
            mu, carry = _rscan_chunk(an, dh_ref[pl.ds(base, t), :], carry, True)
            pbase = pl.multiple_of(jnp.maximum(base - SUBLANES, 0), SUBLANES)
            hp_row = h_ref[pl.ds(pbase, SUBLANES), :][SUBLANES - 1:SUBLANES, :] * (ci > 0).astype(F32)
            hp = _shift_down(h_ref[pl.ds(base, t), :], 1, hp_row)
            da_ref[pl.ds(base, t), :] = mu * hp
            db_ref[pl.ds(base, t), :] = mu
            return carry

        lax.fori_loop(0, n_chunks, chunk, jnp.zeros((1, 128), F32))

    seq = pl.BlockSpec((rows, 128), lambda j: (0, j))
    return pl.pallas_call(
        body, out_shape=(jax.ShapeDtypeStruct((rows, n), F32), jax.ShapeDtypeStruct((rows, n), F32)), grid=(n // 128,),
        in_specs=[seq, seq, seq], out_specs=(seq, seq), compiler_params=_params(("parallel",)), name=name)(dh, a, h)


def _s5_param(lr, li, ls, bre, bim):
    st = jnp.exp(ls)
    er = jnp.exp(lr * st)
    th = li * st
    ar, ai = er * jnp.cos(th), er * jnp.sin(th)
    nr, ni = ar - 1.0, ai
    den = lr * lr + li * li
    cr, ci = (nr * lr + ni * li) / den, (ni * lr - nr * li) / den
    return ar, ai, cr * bre - ci * bim, cr * bim + ci * bre


def s5_param_fwd(name, lr, li, ls, bre, bim):
    gh, n = bre.shape

    def body(lr_ref, li_ref, ls_ref, bre_ref, bim_ref, a_ref, bb_ref):
        ar, ai, br, bi = _s5_param(lr_ref[...], li_ref[...], ls_ref[...], bre_ref[...], bim_ref[...])
        a_ref[0] = ar
        a_ref[1] = ai
        bb_ref[0] = br.astype(BF16)
        bb_ref[1] = bi.astype(BF16)

    return pl.pallas_call(body, out_shape=(jax.ShapeDtypeStruct((2, 1, n), F32), jax.ShapeDtypeStruct((2, gh, n), BF16)),
                          compiler_params=_params(), name=name)(lr, li, ls, bre, bim)


def s5_param_bwd(name, lr, li, ls, bre, bim, da, dbb, gsum):
    gh, n = bre.shape

    def body(lr_ref, li_ref, ls_ref, bre_ref, bim_ref, da_ref, dbb_ref, gs_ref, dlr_ref, dli_ref, dls_ref, dbre_ref, dbim_ref):
        _, vjp = jax.vjp(_s5_param, lr_ref[...], li_ref[...], ls_ref[...], bre_ref[...], bim_ref[...])
        dlr, dli, dls, dbre, dbim = vjp((da_ref[0], da_ref[1], dbb_ref[0], dbb_ref[1]))
        dlr_ref[...] = dlr
        dli_ref[...] = dli
        dls_ref[...] = jnp.dot(jnp.broadcast_to(dls, (SUBLANES, n)), gs_ref[...], preferred_element_type=F32,
                               precision=lax.Precision.HIGHEST)
        dbre_ref[...] = dbre
        dbim_ref[...] = dbim

    vec = jax.ShapeDtypeStruct((1, n), F32)
    mat = jax.ShapeDtypeStruct((gh, n), F32)
    return pl.pallas_call(body, out_shape=(vec, vec, jax.ShapeDtypeStruct((SUBLANES, 128), F32), mat, mat),
                          compiler_params=_params(), name=name)(lr, li, ls, bre, bim, da, dbb, gsum)


def sum_lead(name, x, tr):
    n, rows, cols = x.shape

    def body(x_ref, o_ref):
        acc = x_ref[0]
        for j in range(1, n):
            acc = acc + x_ref[j]
        o_ref[...] = acc

    return pl.pallas_call(
        body, out_shape=jax.ShapeDtypeStruct((rows, cols), x.dtype), grid=(rows // tr,),
        in_specs=[pl.BlockSpec((n, tr, cols), lambda i: (0, i, 0))], out_specs=pl.BlockSpec((tr, cols), lambda i: (i, 0)),
        compiler_params=_params(("parallel",)), name=name)(x)


def _adamw(w, g, m, v):
    m = ADAM_B1 * m + (1.0 - ADAM_B1) * g
    v = ADAM_B2 * v + (1.0 - ADAM_B2) * jnp.square(g)
    m_hat = m / (1.0 - ADAM_B1 ** ADAM_STEP)
    v_hat = v / (1.0 - ADAM_B2 ** ADAM_STEP)
    delta = -ADAM_LR * (m_hat / (jnp.sqrt(v_hat) + ADAM_EPS) + ADAM_WD * w)
    return delta, m, v


def adamw_sharded(name, w, m, v, g0, g1, split_cols, tile):
    _, r, c = w.shape
    if split_cols:
        nt = c // tile
        per = (c // 2) // tile
        wspec = pl.BlockSpec((None, r, tile), lambda l, t: (l, 0, t))
        gspec = pl.BlockSpec((None, r, tile), lambda l, t: (t // per, 0, t % per))
    else:
        nt = r // tile
        per = (r // 2) // tile
        wspec = pl.BlockSpec((None, tile, c), lambda l, t: (l, t, 0))
        gspec = pl.BlockSpec((None, tile, c), lambda l, t: (t // per, t % per, 0))

    def body(w_ref, m_ref, v_ref, g0_ref, g1_ref, g_ref, d_ref, nm_ref, nv_ref):
        g = jnp.where(pl.program_id(0) == 0, g0_ref[...], g1_ref[...])
        d, nm, nv = _adamw(w_ref[...], g, m_ref[...], v_ref[...])
        g_ref[...] = g
        d_ref[...] = d
        nm_ref[...] = nm
        nv_ref[...] = nv

    sds = jax.ShapeDtypeStruct(w.shape, F32)
    return pl.pallas_call(body, out_shape=(sds,) * 4, grid=(2, nt), in_specs=[wspec, wspec, wspec, gspec, gspec],
                          out_specs=(wspec,) * 4, compiler_params=_params(("parallel", "parallel")), name=name)(w, m, v, g0, g1)


def adamw_flat(name, w, g, m, v, tr):
    rows, cols = w.shape

    def body(w_ref, g_ref, m_ref, v_ref, d_ref, nm_ref, nv_ref):
        d, nm, nv = _adamw(w_ref[...], g_ref[...], m_ref[...], v_ref[...])
        d_ref[...] = d
        nm_ref[...] = nm
        nv_ref[...] = nv

    blk = pl.BlockSpec((tr, cols), lambda i: (i, 0))
    sds = jax.ShapeDtypeStruct((rows, cols), F32)
    return pl.pallas_call(body, out_shape=(sds,) * 3, grid=(rows // tr,), in_specs=[blk] * 4, out_specs=(blk,) * 3,
                          compiler_params=_params(("parallel",)), name=name)(w, g, m, v)


def _flips(axes):
    out = []
    for fx in ((0, 1) if "x" in axes else (0,)):
        for fy in ((0, 1) if "y" in axes else (0,)):
            for fc in ((0, 1) if "c" in axes else (0,)):
                if fx or fy or fc:
                    out.append((fx, fy, fc))
    return out


def _slot(pos, axes):
    s = 0
    for name, p in zip(("x", "y", "c"), pos):
        if name in axes:
            s = 2 * s + p
    return s


_HBM = pl.BlockSpec(memory_space=pltpu.HBM)
_SEM = pl.BlockSpec(memory_space=pltpu.SEMAPHORE)
_EFFECT = pltpu.SideEffectType.DATAFLOW_SIDE_EFFECTING


def place_own(name, arrs, axes):
    n = len(_flips(axes)) + 1
    na = len(arrs)

    def body(*refs):
        ins, outs, sems = refs[:na], refs[na:2 * na], refs[2 * na]
        my = _slot((lax.axis_index("x"), lax.axis_index("y"), lax.axis_index("c")), axes)
        copies = [pltpu.make_async_copy(ins[a], outs[a].at[my], sems.at[a]) for a in range(na)]
        for cp in copies:
            cp.start()
        for cp in copies:
            cp.wait()

    out_shape = tuple(jax.ShapeDtypeStruct((n,) + a.shape, a.dtype) for a in arrs)
    anyspec = pl.BlockSpec(memory_space=pl.ANY)
    return pl.pallas_call(body, out_shape=out_shape, in_specs=[anyspec] * na, out_specs=(anyspec,) * na,
                          scratch_shapes=[pltpu.SemaphoreType.DMA((na,))], name=name)(*arrs)


def _peers(axes):
    me = (lax.axis_index("x"), lax.axis_index("y"), lax.axis_index("c"))
    return me, [tuple((1 - p) if f else p for p, f in zip(me, fl)) for fl in _flips(axes)]


def place_tile(name, arr, layer, my, slots=4, dtype=BF16, after=None):
    _, r, cols = arr.shape
    tr = _tile_rows(r, cols)

    def body(my_ref, x_ref, *rest):
        rest[-1][...] = x_ref[...].astype(dtype)

    in_specs = [pl.BlockSpec((None, tr, cols), lambda i, my: (layer, i, 0))]
    args = [arr]
    if after is not None:
        in_specs.append(pl.BlockSpec(after.shape, lambda i, my: (0, 0)))
        args.append(after)
    grid_spec = pltpu.PrefetchScalarGridSpec(num_scalar_prefetch=1, grid=(r // tr,), in_specs=in_specs,
                                             out_specs=pl.BlockSpec((None, tr, cols), lambda i, my: (my[0], i, 0)))
    return pl.pallas_call(body, out_shape=jax.ShapeDtypeStruct((slots, r, cols), dtype), grid_spec=grid_spec,
                          compiler_params=_params(("parallel",)), name=name)(my, *args)


def exchange_start(name, groups, axes, scatter):
    flat = [(p if scatter else (p,)) for grp in groups for p in grp]
    per = 2 if scatter else 1
    na, ng, npeer = len(flat), len(groups), len(_flips(axes))

    def body(*refs):
        ops = refs[:per * na]
        zones = ops[(per - 1) * na:]
        sems, token = refs[per * na:per * na + 2 * ng], refs[-1]
        me, peers = _peers(axes)
        my = _slot(me, axes)
        ai = 0
        for g, grp in enumerate(groups):
            for k in range(len(grp)):
                for j, peer in enumerate(peers):
                    src = ops[ai].at[_slot(peer, axes)] if scatter else zones[ai].at[my]
                    dst = zones[ai].at[j] if scatter else zones[ai].at[my]
                    pltpu.make_async_remote_copy(
                        src_ref=src, dst_ref=dst, send_sem=sems[2 * g].at[k * npeer + j],
                        recv_sem=sems[2 * g + 1].at[k * npeer + j], device_id=peer, device_id_type=pl.DeviceIdType.MESH).start()
                ai += 1
        token[...] = jnp.zeros_like(token)

    out_shape, out_specs = [], []
    for grp in groups:
        out_shape += [pltpu.SemaphoreType.DMA((npeer * len(grp),))] * 2
        out_specs += [_SEM, _SEM]
    for idx in range(per):
        out_shape += [pltpu.HBM(p[idx].shape, p[idx].dtype) for p in flat]
        out_specs += [_HBM] * na
    out_shape.append(jax.ShapeDtypeStruct((SUBLANES, 128), F32))
    out_specs.append(pl.BlockSpec(memory_space=pltpu.VMEM))
    args = [pltpu.with_memory_space_constraint(p[idx], pltpu.HBM) for idx in range(per) for p in flat]
    res = pl.pallas_call(body, out_shape=tuple(out_shape), in_specs=[_HBM] * (per * na), out_specs=tuple(out_specs),
                         input_output_aliases={i: 2 * ng + i for i in range(per * na)},
                         compiler_params=pltpu.CompilerParams(has_side_effects=_EFFECT), name=name)(*args)
    thru = res[2 * ng:2 * ng + per * na]
    out, ai = [], 0
    for g, grp in enumerate(groups):
        srcs = list(thru[ai:ai + len(grp)]) if scatter else []
        zones = list(thru[(per - 1) * na + ai:(per - 1) * na + ai + len(grp)])
        out.append(((res[2 * g], res[2 * g + 1]), srcs, zones))
        ai += len(grp)
    return out, res[-1]


def exchange_wait(name, group, after, axes, scatter):
    (send_sems, recv_sems), srcs, zones = group
    n, ns = len(zones), len(srcs)
    npeer = len(_flips(axes))

    def body(*refs):
        z_refs = refs[ns:ns + n]
        ssem, rsem = refs[ns + n], refs[ns + n + 1]
        _, peers = _peers(axes)
        for k in range(n):
            for j, peer in enumerate(peers):
                part = z_refs[k].at[j if scatter else _slot(peer, axes)]
                copy = pltpu.make_async_remote_copy(
                    src_ref=part, dst_ref=part, send_sem=ssem.at[k * npeer + j], recv_sem=rsem.at[k * npeer + j],
                    device_id=peer, device_id_type=pl.DeviceIdType.MESH)
                copy.wait_send()
                copy.wait_recv()

    ops = list(srcs) + list(zones)
    out_shape = tuple(pltpu.HBM(a.shape, a.dtype) for a in ops)
    res = pl.pallas_call(body, out_shape=out_shape, in_specs=[_HBM] * len(ops) + [_SEM, _SEM, pl.BlockSpec(memory_space=pl.ANY)],
                         out_specs=(_HBM,) * len(ops), input_output_aliases={i: i for i in range(len(ops))},
                         compiler_params=pltpu.CompilerParams(has_side_effects=_EFFECT), name=name)(*ops, send_sems, recv_sems, after)
    return list(res[:ns]), list(res[ns:])


def _pair_exchange(name, ins, in_specs, n_steps, tile, fn_send, fn_out, out_shape, out_spec, prefetch=None, wire=F32):
    n_in = len(ins)

    def body(*refs):
        if prefetch is not None:
            refs = refs[1:]
        in_refs, o_ref = refs[:n_in], refs[n_in]
        send_buf, recv_buf, send_sems, recv_sems, credit = refs[n_in + 1:]
        i = pl.program_id(0)
        slot = lax.rem(i, 2)
        c = lax.axis_index("c")
        sibling = (lax.axis_index("x"), lax.axis_index("y"), 1 - c)
        vals = [r[...] for r in in_refs]
        send_buf[slot] = fn_send(*vals, c).astype(wire)

        @pl.when(i >= 2)
        def _():
            pl.semaphore_wait(credit, 1)

        copy = pltpu.make_async_remote_copy(
            src_ref=send_buf.at[slot], dst_ref=recv_buf.at[slot], send_sem=send_sems.at[slot], recv_sem=recv_sems.at[slot],
            device_id=sibling, device_id_type=pl.DeviceIdType.MESH)
        copy.start()
        copy.wait_recv()
        o_ref[...] = fn_out(*vals, recv_buf[slot], c).astype(o_ref.dtype)
        copy.wait_send()

        @pl.when(i < n_steps - 2)
        def _():
            pl.semaphore_signal(credit, inc=1, device_id=sibling, device_id_type=pl.DeviceIdType.MESH)

    scratch = [pltpu.VMEM((2,) + tile, wire), pltpu.VMEM((2,) + tile, wire), pltpu.SemaphoreType.DMA((2,)),
               pltpu.SemaphoreType.DMA((2,)), pltpu.SemaphoreType.REGULAR]
    if prefetch is None:
        return pl.pallas_call(body, out_shape=out_shape, grid=(n_steps,), in_specs=in_specs, out_specs=out_spec,
                              scratch_shapes=scratch, compiler_params=_params(("arbitrary",)), name=name)(*ins)
    grid_spec = pltpu.PrefetchScalarGridSpec(num_scalar_prefetch=1, grid=(n_steps,), in_specs=in_specs, out_specs=out_spec,
                                             scratch_shapes=scratch)
    return pl.pallas_call(body, out_shape=out_shape, grid_spec=grid_spec, compiler_params=_params(("arbitrary",)),
                          name=name)(prefetch, *ins)


def _tile_rows(rows, cols, f32_bytes=3 << 19):
    return _row_tile(rows, max(2 * SUBLANES, f32_bytes // (4 * cols)), 2 * SUBLANES)


def pair_sum(name, x):
    rows, cols = x.shape
    tr = _tile_rows(rows, cols)
    return _pair_exchange(name, [x], [pl.BlockSpec((tr, cols), lambda i: (i, 0))], rows // tr, (tr, cols),
                          lambda v, c: v, lambda v, got, c: v + got, jax.ShapeDtypeStruct((rows, cols), F32),
                          pl.BlockSpec((tr, cols), lambda i: (i, 0)))


def reduce_cores(name, g):
    _, m, cols = g.shape
    tr = _tile_rows(m, cols, 6 << 20)

    def fn_send(g0, g1, c):
        return jnp.where(c == 0, g1, g0)

    def fn_out(g0, g1, got, c):
        return jnp.where(c == 0, g0, g1) + got.astype(F32)

    return _pair_exchange(
        name, [g, g], [pl.BlockSpec((None, tr, cols), lambda i: (0, i, 0)), pl.BlockSpec((None, tr, cols), lambda i: (1, i, 0))],
        m // tr, (tr, cols), fn_send, fn_out, jax.ShapeDtypeStruct((m, cols), BF16), pl.BlockSpec((tr, cols), lambda i: (i, 0)),
        wire=BF16)


def sum_and_share(name, own, parts, my):
    n, r, cols = parts.shape
    tr = _tile_rows(r, cols, 3 << 20)

    def total(o, p):
        acc = o.astype(F32)
        for j in range(n):
            acc = acc + p[j].astype(F32)
        return acc

    def fn_send(o, p, c):
        return total(o, p)

    def fn_out(o, p, got, c):
        mine = total(o, p)
        return jnp.stack([jnp.where(c == 0, mine, got), jnp.where(c == 0, got, mine)])

    return _pair_exchange(
        name, [own, parts], [pl.BlockSpec((None, tr, cols), lambda i, my_ref: (my_ref[0], i, 0)), pl.BlockSpec((n, tr, cols), lambda i, my_ref: (0, i, 0))],
        r // tr, (tr, cols), fn_send, fn_out, jax.ShapeDtypeStruct((2, r, cols), F32),
        pl.BlockSpec((2, tr, cols), lambda i, my_ref: (0, i, 0)), prefetch=my)


def _block_diag(blocks):
    g, r, c = blocks.shape
    eye = jnp.eye(g, dtype=blocks.dtype)
    return (blocks[:, :, None, :] * eye[:, None, :, None]).reshape(g * r, g * c)


def _diag_blocks(mat, g):
    r, c = mat.shape[0] // g, mat.shape[1] // g
    eye = jnp.eye(g, dtype=mat.dtype)
    return (mat.reshape(g, r, g, c) * eye[:, None, :, None]).sum(axis=2)


def _halves(gfull, shards):
    rows, cols = gfull.shape
    return gfull.reshape(shards, 2, rows // shards // 2, cols).transpose(1, 0, 2, 3)


def _step(inp):
    x = inp['x'][0]
    target = inp['loss_target'][0]
    rows, d = x.shape
    depth = inp['w_in'].shape[0]
    mix_w = d // 4
    n_state = S5_GROUPS * S5_STATE
    ffn_half = inp['ffn_w_up'].shape[2]
    tm = min(512, rows)
    tc = min(256, rows)
    xy = ("x", "y")

    my_chip = (2 * lax.axis_index("x") + lax.axis_index("y")).astype(jnp.int32).reshape(1)
    small_keys = [(nme, None) for nme in SMALL_SHARDED]
    group_keys = []
    for l in range(depth):
        group_keys += [[('w_in', l)] + (small_keys if l == 0 else []),
                       [('w_out', l), ('s5_w_glu', l), ('cv_w_pw', l)], [('ffn_w_up', l)], [('ffn_w_down', l)]]

    def zone_of(key, after=None):
        nme, l = key
        src = jnp.swapaxes(inp[nme], 1, 2) if nme == 'ffn_w_up' else inp[nme]
        return place_tile(f"place_{nme}{l}", src, l, my_chip, after=after)

    zones = {('w_in', 0): zone_of(('w_in', 0))}
    zones.update(zip(small_keys, place_own("place_small", [inp[nme] for nme in SMALL_SHARDED], xy)))
    first_group, first_token = exchange_start("gather_start_first", [[zones[key] for key in group_keys[0]]], xy, False)
    for grp in group_keys[1:]:
        zones.update({key: zone_of(key, first_token) for key in grp})
    rest_groups, gather_token = exchange_start("gather_start", [[zones[key] for key in grp] for grp in group_keys[1:]], xy, False)
    gather_groups = first_group + rest_groups

    def gathered(gi, after):
        return dict(zip(group_keys[gi], exchange_wait(f"gather_wait{gi}", gather_groups[gi], after, xy, False)[1]))

    def full_small(g):
        return g.transpose(1, 2, 0, 3).reshape(g.shape[1], g.shape[2], 4 * g.shape[3])

    gsum = jnp.repeat(jnp.eye(128, dtype=F32)[:S5_GROUPS], S5_STATE, axis=0)

    saved = []
    grads = {nme: [None] * depth for nme in WEIGHTS}
    xcur = x
    for l in range(depth):
        vec = lambda a: a[l].reshape(1, -1)
        gain = vec(inp['norm_mix_g']) + (gather_token[0, 0] if l == 0 else 0.0)
        h = rms_fwd(f"rms_mix{l}", xcur, gain, tm)
        got = gathered(4 * l, h)
        w_in = got[('w_in', l)]
        if l == 0:
            cv_w_dw, lru_w_conv, ffn_w_dw = (full_small(got[(nme, None)]) for nme in ('cv_w_dw', 'lru_w_conv', 'ffn_w_dw'))
        ncol = w_in.shape[2]

        lam_re, lam_im = vec(inp['s5_lam_re']), vec(inp['s5_lam_im'])
        log_step = jnp.broadcast_to(inp['s5_log_step'][l][:, None], (S5_GROUPS, S5_STATE)).reshape(1, n_state)
        b_re = _block_diag(inp['s5_b_re'][l].transpose(0, 2, 1))
        b_im = _block_diag(inp['s5_b_im'][l].transpose(0, 2, 1))
        c_cat = jnp.stack([_block_diag(inp['s5_c_re'][l].transpose(0, 2, 1)),
                           -_block_diag(inp['s5_c_im'][l].transpose(0, 2, 1))]).astype(BF16)
        a_bar, b_bar = s5_param_fwd(f"s5_param_fwd{l}", lam_re, lam_im, log_step, b_re, b_im)
        w_r = _block_diag(inp['lru_w_r'][l]).astype(BF16)
        w_i = _block_diag(inp['lru_w_i'][l]).astype(BF16)
        pool_bd = _block_diag(inp['pool_w'][l]).astype(BF16)
        gate_pars = [w_r, w_i, vec(inp['lru_b_r']), vec(inp['lru_b_i']), vec(inp['lru_lam'])]

        proj = _mm(f"proj{l}", h, w_in, jax.ShapeDtypeStruct((rows, 4 * ncol), F32), (4, rows // tm),
                   pl.BlockSpec((tm, d), lambda j, i: (i, 0)), pl.BlockSpec((None, d, ncol), lambda j, i: (j, 0, 0)),
                   pl.BlockSpec((tm, ncol), lambda j, i: (i, j)), NN)
        proj3 = proj.reshape(1, rows, 4 * ncol)
        ts = min(2048, rows)
        cw, sw = mix_w // 4, n_state // 4
        bu = _mm(f"s5_bu{l}", proj, b_bar, jax.ShapeDtypeStruct((2, rows, n_state), F32), (rows // ts, 2, 4),
                 pl.BlockSpec((ts, cw), lambda i, c, s: (i, s)), pl.BlockSpec((None, cw, sw), lambda i, c, s: (c, s, s)),
                 pl.BlockSpec((None, ts, sw), lambda i, c, s: (c, i, s)), NN)
        z = s5_scan_fwd(f"s5_scan{l}", bu, a_bar)
        y_ssm = _mm(f"s5_read{l}", z, c_cat, jax.ShapeDtypeStruct((rows, mix_w), F32), (rows // ts, 4, 2),
                    pl.BlockSpec((None, ts, sw), lambda i, s, c: (c, i, s)), pl.BlockSpec((None, sw, cw), lambda i, s, c: (c, s, s)),
                    pl.BlockSpec((ts, cw), lambda i, s, c: (i, s)), NN, k_axis=2)
        (h0,) = _rowwise(f"cv_glu{l}", _glu, [(proj, 1, mix_w), (proj, 2, mix_w)], [], 1, [(mix_w, F32)], tm)
        h1 = dwconv_fwd(f"cv_conv{l}", h0.reshape(1, rows, mix_w), 0, mix_w, cv_w_dw[l][None], vec(inp['cv_b_dw'])[None],
                        CV_TAPS, tc)[0]
        xc = dwconv_fwd(f"lru_conv{l}", proj3, 3, mix_w, lru_w_conv[l][None], vec(inp['lru_b_conv'])[None], LRU_TAPS, tc)[0]
        a_t, b_t = _rowwise(f"lru_gate{l}", _lru_gate, [(xc, 0, mix_w)], gate_pars, 2, [(mix_w, F32), (mix_w, F32)], tm)
        hseq = lru_scan_fwd(f"lru_scan{l}", a_t, b_t)
        dgp = pool_fwd(f"pool{l}", proj, 5, tc)
        got = gathered(4 * l + 1, proj)
        w_out = got[('w_out', l)].reshape(d, d)
        w_glu, w_pw = got[('s5_w_glu', l)].reshape(mix_w, mix_w), got[('cv_w_pw', l)].reshape(mix_w, mix_w)
        post_pars = [vec(inp['s5_d']), w_glu, vec(inp['s5_b_glu']), vec(inp['cv_ln_g']), vec(inp['cv_ln_b']), w_pw,
                     vec(inp['cv_b_pw']), pool_bd, vec(inp['pool_scale'])]
        post_rows = [(y_ssm, 0, mix_w), (proj, 0, mix_w), (h1, 0, mix_w), (hseq, 0, mix_w), (proj, 4, mix_w), (dgp, 0, mix_w)]
        (mixed,) = _rowwise(f"mix_post{l}", _mix_post, post_rows, post_pars, 1, [(d, BF16)], tm)
        x1 = _mm(f"out_proj{l}", mixed, w_out, jax.ShapeDtypeStruct((rows, d), F32), (2, rows // tm),
                 pl.BlockSpec((tm, d), lambda j, i: (i, 0)), pl.BlockSpec((d, d // 2), lambda j, i: (0, j)),
                 pl.BlockSpec((tm, d // 2), lambda j, i: (i, j)), NN,
                 add=xcur, add_spec=pl.BlockSpec((tm, d // 2), lambda j, i: (i, j)))

        h2 = rms_fwd(f"rms_ffn{l}", x1, vec(inp['norm_ffn_g']), tm)
        tu = min(512, rows)
        w_up = gathered(4 * l + 2, x1)[('ffn_w_up', l)]
        up = _mm(f"ffn_up{l}", h2, w_up, jax.ShapeDtypeStruct((4, rows, ffn_half), F32), (4, rows // tu),
                 pl.BlockSpec((tu, d), lambda k, i: (i, 0)), pl.BlockSpec((None, ffn_half, d), lambda k, i: (k, 0, 0)),
                 pl.BlockSpec((None, tu, ffn_half), lambda k, i: (k, i, 0)), NT)
        w_dw = ffn_w_dw[l].reshape(FFN_TAPS, 2, ffn_half).transpose(1, 0, 2)
        b_dw = inp['ffn_b_dw'][l].reshape(2, 1, ffn_half)
        act = ffn_gate_fwd(f"ffn_gate{l}", up, w_dw, b_dw, tc)
        w_down = gathered(4 * l + 3, up)[('ffn_w_down', l)].reshape(2, ffn_half, d)
        td = min(1024, rows)
        x2 = _mm(f"ffn_down{l}", act, w_down, jax.ShapeDtypeStruct((rows, d), F32), (rows // td, 4),
                 pl.BlockSpec((2, td, ffn_half), lambda i, j: (0, i, 0)), pl.BlockSpec((2, ffn_half, d // 4), lambda i, j: (0, 0, j)),
                 pl.BlockSpec((td, d // 4), lambda i, j: (i, j)), NN, inner=("lead", 2),
                 add=x1, add_spec=pl.BlockSpec((td, d // 4), lambda i, j: (i, j)))
        saved.append(dict(x=xcur, h=h, proj=proj, z=z, y_ssm=y_ssm, h0=h0, h1=h1, xc=xc, a_t=a_t, hseq=hseq, dgp=dgp,
                          mixed=mixed, x1=x1, h2=h2, up=up, act=act, w_in=w_in, w_out=w_out, w_up=w_up, w_down=w_down,
                          a_bar=a_bar, b_bar=b_bar, c_cat=c_cat, post_pars=post_pars, gate_pars=gate_pars, w_dw=w_dw, b_dw=b_dw,
                          s5=(lam_re, lam_im, log_step, b_re, b_im), cv_w=cv_w_dw[l][None], lru_w=lru_w_conv[l][None]))
        xcur = x2

    loss_row, dx, dg_final = final_loss("final_loss", xcur, inp['norm_final_g'].reshape(1, d), target, tm)
    grads['norm_final_g'] = dg_final.reshape(d)

    big_g = {nme: [None] * depth for nme in BIG}
    reduce_groups = []

    def start_reduce(tag, keys):
        pieces = []
        for nme, lyr in keys:
            g = big_g[nme][lyr]
            if nme == 'ffn_w_down':
                g = g.reshape(2, 4, ffn_half // 2, d // 2)
            pieces.append(reduce_cores(f"reduce_cores_{nme}{lyr}", g.reshape(2, -1, g.shape[-1])).reshape(g.shape[1:]))
        landing = [lax.empty((3,) + p.shape[1:], p.dtype) for p in pieces]
        groups, token = exchange_start(f"reduce_start_{tag}", [list(zip(pieces, landing))], xy, True)
        reduce_groups.append((tag, keys, groups[0]))
        return token

    for l in reversed(range(depth)):
        s = saved[l]
        ncol = s['w_in'].shape[2]
        tu = min(512, rows)
        dact = _mm(f"d_act{l}", dx, s['w_down'], jax.ShapeDtypeStruct((2, rows, ffn_half), F32), (2, rows // tu),
                   pl.BlockSpec((tu, d), lambda k, i: (i, 0)), pl.BlockSpec((None, ffn_half, d), lambda k, i: (k, 0, 0)),
                   pl.BlockSpec((None, tu, ffn_half), lambda k, i: (k, i, 0)), NT)
        tn = d // 4
        tk = min(1024, rows)
        tkb = min(2048, rows)
        big_g['ffn_w_down'][l] = _mm(
            f"dw_down{l}", s['act'], dx, jax.ShapeDtypeStruct((2, 2, ffn_half, d // 2), F32), (2, 4, rows // tkb),
            pl.BlockSpec((None, tkb, ffn_half), lambda hh, n, k: (hh, k, 0)), pl.BlockSpec((tkb, tn), lambda hh, n, k: (k, n)),
            pl.BlockSpec((None, None, ffn_half, tn), lambda hh, n, k: (n // 2, hh, 0, n % 2)), TN, k_axis=2)
        dup, dw_dw, db_dw = ffn_gate_bwd(f"ffn_gate_bwd{l}", s['up'], dact, s['w_dw'], s['b_dw'], tc)
        grads['ffn_w_dw'][l] = dw_dw.transpose(1, 0, 2).reshape(FFN_TAPS, 2 * ffn_half)
        grads['ffn_b_dw'][l] = db_dw.reshape(2 * ffn_half)
        dup = dup.reshape(4, rows, ffn_half)
        tm2 = min(1024, rows)
        dh2 = _mm(f"d_h2{l}", dup, s['w_up'], jax.ShapeDtypeStruct((rows, d), F32), (rows // tm2, 2, 4),
                  pl.BlockSpec((None, tm2, ffn_half), lambda i, j, k: (k, i, 0)), pl.BlockSpec((None, ffn_half, d // 2), lambda i, j, k: (k, 0, j)),
                  pl.BlockSpec((tm2, d // 2), lambda i, j, k: (i, j)), NN, k_axis=2)
        tmm = d // 4
        big_g['ffn_w_up'][l] = _mm(
            f"dw_up{l}", dup, s['h2'], jax.ShapeDtypeStruct((2, 4, ffn_half, d // 2), F32), (4, 4, rows // tkb),
            pl.BlockSpec((None, tkb, ffn_half), lambda k4, n, k: (k4, k, 0)), pl.BlockSpec((tkb, tn), lambda k4, n, k: (k, n)),
            pl.BlockSpec((None, None, ffn_half, tn), lambda k4, n, k: (n // 2, k4, 0, n % 2)), TN, k_axis=2)
        token = start_reduce(f"ffn{l}", [('ffn_w_down', l), ('ffn_w_up', l)])
        dx1, dg = rms_bwd(f"rms_ffn_bwd{l}", s['x1'], inp['norm_ffn_g'][l].reshape(1, d) + token[0, 0], dh2, dx, tm)
        grads['norm_ffn_g'][l] = dg.reshape(d)
        dmixed = _mm(f"d_mixed{l}", dx1, s['w_out'], jax.ShapeDtypeStruct((rows, d), F32), (rows // tm2, 4),
                     pl.BlockSpec((tm2, d), lambda i, j: (i, 0)), pl.BlockSpec((d // 4, d), lambda i, j: (j, 0)),
                     pl.BlockSpec((tm2, d // 4), lambda i, j: (i, j)), NT)
        tq = mix_w // 2
        big_g['w_out'][l] = _mm(
            f"dw_out{l}", s['mixed'], dx1, jax.ShapeDtypeStruct((2, 4, tq, d), F32), (4, rows // tk),
            pl.BlockSpec((tk, 2 * tq), lambda t, k: (k, t)), pl.BlockSpec((tk, d), lambda t, k: (k, 0)),
            pl.BlockSpec((2, None, tq, d), lambda t, k: (0, t, 0, 0)), TN, k_axis=1)
        post_rows = [(s['y_ssm'], 0, mix_w), (s['proj'], 0, mix_w), (s['h1'], 0, mix_w), (s['hseq'], 0, mix_w),
                     (s['proj'], 4, mix_w), (s['dgp'], 0, mix_w), (dmixed, 0, d)]
        res = _rowwise(f"mix_post_bwd{l}", _mix_post, post_rows, s['post_pars'], 1, [(mix_w, F32)] * 6, tm, with_grads=True)
        dy_ssm, du_dir, dh1, dhseq, dlru_g, ddgp = res[:6]
        dd, dwglu, dbglu, dlng, dlnb, dwpw, dbpw, dpoolbd, dscale = res[6:]
        grads['s5_d'][l], grads['s5_b_glu'][l] = dd.reshape(mix_w), dbglu.reshape(mix_w)
        grads['cv_ln_g'][l], grads['cv_ln_b'][l], grads['cv_b_pw'][l] = dlng.reshape(mix_w), dlnb.reshape(mix_w), dbpw.reshape(mix_w)
        grads['pool_w'][l] = _diag_blocks(dpoolbd, len(POOL_WINDOWS))
        grads['pool_scale'][l] = dscale.reshape(mix_w)
        big_g['s5_w_glu'][l] = _halves(dwglu, 4)
        big_g['cv_w_pw'][l] = _halves(dwpw, 4)
        ts = min(2048, rows)
        cw, sw = mix_w // 4, n_state // 4
        slab = jnp.arange(mix_w)[:, None] // cw == jnp.arange(n_state)[None, :] // sw
        dz = _mm(f"s5_dz{l}", dy_ssm, s['c_cat'], jax.ShapeDtypeStruct((2, rows, n_state), F32), (rows // ts, 2, 4),
                 pl.BlockSpec((ts, cw), lambda i, c, q: (i, q)), pl.BlockSpec((None, sw, cw), lambda i, c, q: (c, q, q)),
                 pl.BlockSpec((None, ts, sw), lambda i, c, q: (c, i, q)), NT)
        dccat = _mm(f"s5_dc{l}", s['z'], dy_ssm, jax.ShapeDtypeStruct((2, n_state, mix_w), F32), (2, 4, rows // tk),
                    pl.BlockSpec((None, tk, sw), lambda c, q, k: (c, k, q)), pl.BlockSpec((tk, cw), lambda c, q, k: (k, q)),
                    pl.BlockSpec((None, sw, cw), lambda c, q, k: (c, q, q)), TN, k_axis=2)
        dccat = jnp.where(slab.T, dccat, 0.0)
        grads['s5_c_re'][l] = _diag_blocks(dccat[0], S5_GROUPS).transpose(0, 2, 1)
        grads['s5_c_im'][l] = -_diag_blocks(dccat[1], S5_GROUPS).transpose(0, 2, 1)
        lam, da_bar = s5_scan_bwd(f"s5_scan_bwd{l}", dz, s['z'], s['a_bar'])
        du = _mm(f"s5_du{l}", lam, s['b_bar'], jax.ShapeDtypeStruct((rows, mix_w), F32), (rows // ts, 4, 2),
                 pl.BlockSpec((None, ts, sw), lambda i, q, c: (c, i, q)), pl.BlockSpec((None, cw, sw), lambda i, q, c: (c, q, q)),
                 pl.BlockSpec((ts, cw), lambda i, q, c: (i, q)), NT, k_axis=2,
                 add=du_dir, add_spec=pl.BlockSpec((ts, cw), lambda i, q, c: (i, q)))
        dbbar = _mm(f"s5_db{l}", s['proj'], lam, jax.ShapeDtypeStruct((2, mix_w, n_state), F32), (2, 4, rows // tk),
                    pl.BlockSpec((tk, cw), lambda c, q, k: (k, q)), pl.BlockSpec((None, tk, sw), lambda c, q, k: (c, k, q)),
                    pl.BlockSpec((None, cw, sw), lambda c, q, k: (c, q, q)), TN, k_axis=2)
        dbbar = jnp.where(slab, dbbar, 0.0)
        dlr, dli, dls, dbre, dbim = s5_param_bwd(f"s5_param_bwd{l}", *s['s5'], da_bar, dbbar, gsum)
        grads['s5_lam_re'][l] = dlr.reshape(S5_GROUPS, S5_STATE)
        grads['s5_lam_im'][l] = dli.reshape(S5_GROUPS, S5_STATE)
        grads['s5_log_step'][l] = dls[0, :S5_GROUPS]
        grads['s5_b_re'][l] = _diag_blocks(dbre, S5_GROUPS).transpose(0, 2, 1)
        grads['s5_b_im'][l] = _diag_blocks(dbim, S5_GROUPS).transpose(0, 2, 1)
        dh0, dw_cv, db_cv = dwconv_bwd(f"cv_conv_bwd{l}", dh1.reshape(1, rows, mix_w), s['h0'].reshape(1, rows, mix_w), 0, mix_w,
                                       s['cv_w'], CV_TAPS, tc)
        grads['cv_w_dw'][l], grads['cv_b_dw'][l] = dw_cv[0], db_cv.reshape(mix_w)
        dv, dgg = _rowwise(f"cv_glu_bwd{l}", _glu, [(s['proj'], 1, mix_w), (s['proj'], 2, mix_w), (dh0[0], 0, mix_w)], [], 1,
                           [(mix_w, F32)] * 2, tm, with_grads=True)
        da_t, db_t = lru_scan_bwd(f"lru_scan_bwd{l}", dhseq, s['a_t'], s['hseq'])
        res = _rowwise(f"lru_gate_bwd{l}", _lru_gate, [(s['xc'], 0, mix_w), (da_t, 0, mix_w), (db_t, 0, mix_w)], s['gate_pars'], 2,
                       [(mix_w, F32)], tm, with_grads=True)
        dxc, dwr, dwi, dbr, dbi, dlam = res
        grads['lru_w_r'][l], grads['lru_w_i'][l] = _diag_blocks(dwr, LRU_HEADS), _diag_blocks(dwi, LRU_HEADS)
        grads['lru_b_r'][l], grads['lru_b_i'][l], grads['lru_lam'][l] = dbr.reshape(mix_w), dbi.reshape(mix_w), dlam.reshape(mix_w)
        dlx, dw_lc, db_lc = dwconv_bwd(f"lru_conv_bwd{l}", dxc.reshape(1, rows, mix_w), s['proj'].reshape(1, rows, 4 * ncol), 3, mix_w,
                                       s['lru_w'], LRU_TAPS, tc)
        grads['lru_w_conv'][l], grads['lru_b_conv'][l] = dw_lc[0], db_lc.reshape(mix_w)
        dpx = pool_bwd(f"pool_bwd{l}", ddgp, tc)
        dproj = jnp.concatenate([du, dv, dgg, dlx[0], dlru_g, dpx], axis=-1)
        dh = _mm(f"d_h{l}", dproj, s['w_in'], jax.ShapeDtypeStruct((rows, d), F32), (rows // tm, 4),
                 pl.BlockSpec((tm, 4 * ncol), lambda i, j: (i, 0)), pl.BlockSpec((4, d // 4, ncol), lambda i, j: (0, j, 0)),
                 pl.BlockSpec((tm, d // 4), lambda i, j: (i, j)), NT, inner=("cols", 4))
        tk2 = min(2048, rows)
        big_g['w_in'][l] = _mm(
            f"dw_in{l}", s['h'], dproj, jax.ShapeDtypeStruct((2, 4, d // 2, ncol), F32), (4, 2, rows // tk2),
            pl.BlockSpec((tk2, d // 2), lambda k4, m, k: (k, m)), pl.BlockSpec((tk2, ncol), lambda k4, m, k: (k, k4)),
            pl.BlockSpec((None, None, d // 2, ncol), lambda k4, m, k: (m, k4, 0, 0)), TN, k_axis=2)
        token = start_reduce(f"mix{l}", [('w_out', l), ('s5_w_glu', l), ('cv_w_pw', l), ('w_in', l)])
        dx, dg = rms_bwd(f"rms_mix_bwd{l}", s['x'], inp['norm_mix_g'][l].reshape(1, d) + token[0, 0], dh, dx1, tm)
        grads['norm_mix_g'][l] = dg.reshape(d)

    small = [nme for nme in WEIGHTS if nme not in BIG]
    full_g = {nme: (grads[nme] if nme == 'norm_final_g' else jnp.stack(grads[nme])) for nme in small}
    flat = jnp.concatenate([full_g[nme].reshape(-1) for nme in small])
    packed = jnp.pad(flat, (0, (-flat.shape[0]) % (128 * 64))).reshape(-1, 128)
    chip_sum = pair_sum("small_pair_sum", packed)
    small_zone = place_tile("place_small_grads", chip_sum[None], 0, my_chip, dtype=F32)
    (small_group,), small_token = exchange_start("small_start", [[small_zone]], xy, False)

    t_full = {}
    for tag, keys, group in reduce_groups:
        pieces, parts = exchange_wait(f"reduce_wait_{tag}", group, small_token, xy, True)
        for key, own, got in zip(keys, pieces, parts):
            t_full[key] = sum_and_share(f"share_cores_{key[0]}{key[1]}", own, got, my_chip)

    outs, done_big = {}, []
    tiles = {'w_in': 256, 'w_out': 128, 'ffn_w_up': 128, 'ffn_w_down': 256, 's5_w_glu': 64, 'cv_w_pw': 64}
    for nme in BIG:
        g0, g1 = t_full[(nme, 0)], t_full[(nme, 1)]
        if nme == 'ffn_w_up':
            res = adamw_sharded(f"adamw_{nme}", *(jnp.swapaxes(inp[p + nme], 1, 2) for p in ('', 'm_', 'v_')), g0, g1, True, tiles[nme])
            outs[nme] = tuple(jnp.swapaxes(r, 1, 2) for r in res)
        else:
            res = adamw_sharded(f"adamw_{nme}", inp[nme], inp['m_' + nme], inp['v_' + nme], g0, g1, nme == 'ffn_w_down', tiles[nme])
            outs[nme] = res
        done_big.append(res[1][:1, :1, :1].reshape(1))

    after_big = sum(done_big)
    (g4,) = exchange_wait("small_wait", small_group, after_big, xy, False)[1]
    gsum_small = sum_lead("sum_small", g4, 64).reshape(-1)
    red, off = {}, 0
    for nme in small:
        g = gsum_small[off:off + full_g[nme].size].reshape(full_g[nme].shape)
        off += full_g[nme].size
        if nme in SMALL_SHARDED:
            width = inp[nme].shape[2]
            g = lax.dynamic_slice_in_dim(g, my_chip[0] * width, width, axis=2)
        red[nme] = g

    def pack(tree):
        f = jnp.concatenate([tree[nme].reshape(-1) for nme in small])
        return jnp.pad(f, (0, (-f.shape[0]) % (128 * 64))).reshape(-1, 128)

    pd, pm, pv = adamw_flat("adamw_small", pack({n_: inp[n_] for n_ in small}), pack(red), pack({n_: inp['m_' + n_] for n_ in small}),
                            pack({n_: inp['v_' + n_] for n_ in small}), 64)
    off = 0
    for nme in small:
        size, shape = inp[nme].size, inp[nme].shape
        outs[nme] = (red[nme],) + tuple(p.reshape(-1)[off:off + size].reshape(shape) for p in (pd, pm, pv))
        off += size

    loss = lax.psum(loss_row[0, 0], ("x", "y", "c"))
    result = [loss, dx[None]]
    for part in range(4):
        result += [outs[nme][part] for nme in WEIGHTS]
    return tuple(result)


def kernel(x, norm_mix_g, w_in, s5_lam_re, s5_lam_im, s5_log_step, s5_b_re, s5_b_im, s5_c_re, s5_c_im, s5_d, s5_w_glu, s5_b_glu, cv_w_dw, cv_b_dw, cv_ln_g, cv_ln_b, cv_w_pw, cv_b_pw, lru_w_conv, lru_b_conv, lru_w_r, lru_b_r, lru_w_i, lru_b_i, lru_lam, pool_w, pool_scale, w_out, norm_ffn_g, ffn_w_up, ffn_w_dw, ffn_b_dw, ffn_w_down, norm_final_g, loss_target, m_norm_mix_g, m_w_in, m_s5_lam_re, m_s5_lam_im, m_s5_log_step, m_s5_b_re, m_s5_b_im, m_s5_c_re, m_s5_c_im, m_s5_d, m_s5_w_glu, m_s5_b_glu, m_cv_w_dw, m_cv_b_dw, m_cv_ln_g, m_cv_ln_b, m_cv_w_pw, m_cv_b_pw, m_lru_w_conv, m_lru_b_conv, m_lru_w_r, m_lru_b_r, m_lru_w_i, m_lru_b_i, m_lru_lam, m_pool_w, m_pool_scale, m_w_out, m_norm_ffn_g, m_ffn_w_up, m_ffn_w_dw, m_ffn_b_dw, m_ffn_w_down, m_norm_final_g, v_norm_mix_g, v_w_in, v_s5_lam_re, v_s5_lam_im, v_s5_log_step, v_s5_b_re, v_s5_b_im, v_s5_c_re, v_s5_c_im, v_s5_d, v_s5_w_glu, v_s5_b_glu, v_cv_w_dw, v_cv_b_dw, v_cv_ln_g, v_cv_ln_b, v_cv_w_pw, v_cv_b_pw, v_lru_w_conv, v_lru_b_conv, v_lru_w_r, v_lru_b_r, v_lru_w_i, v_lru_b_i, v_lru_lam, v_pool_w, v_pool_scale, v_w_out, v_norm_ffn_g, v_ffn_w_up, v_ffn_w_dw, v_ffn_b_dw, v_ffn_w_down, v_norm_final_g):
    inp = dict(locals())
    return _step(inp)
```

```python
import functools

import jax
import jax.numpy as jnp
from jax import lax
from jax.experimental import pallas as pl
from jax.experimental.pallas import tpu as pltpu

F32 = jnp.float32
BF16 = jnp.bfloat16

VMEM_LIMIT_BYTES = 56 * 1024 * 1024
SUBLANES = 8

EPS = 1e-6
S5_GROUPS, S5_STATE, S5_GROUP_CH = 32, 64, 16
LRU_HEADS, LRU_C = 8, 8.0
POOL_WINDOWS = (2, 4, 8, 16)
CV_TAPS, LRU_TAPS, FFN_TAPS = 31, 4, 3
SCAN_CHUNK = 64
GELU_K0, GELU_K1 = 0.7978845608028654, 0.044715

ADAM_LR, ADAM_B1, ADAM_B2, ADAM_EPS, ADAM_WD, ADAM_STEP = 0.001, 0.9, 0.999, 1e-08, 0.01, 10

NN = ((1,), (0,))
NT = ((1,), (1,))
TN = ((0,), (0,))

WEIGHTS = ['norm_mix_g', 'w_in', 's5_lam_re', 's5_lam_im', 's5_log_step', 's5_b_re', 's5_b_im', 's5_c_re', 's5_c_im',
           's5_d', 's5_w_glu', 's5_b_glu', 'cv_w_dw', 'cv_b_dw', 'cv_ln_g', 'cv_ln_b', 'cv_w_pw', 'cv_b_pw',
           'lru_w_conv', 'lru_b_conv', 'lru_w_r', 'lru_b_r', 'lru_w_i', 'lru_b_i', 'lru_lam', 'pool_w', 'pool_scale',
           'w_out', 'norm_ffn_g', 'ffn_w_up', 'ffn_w_dw', 'ffn_b_dw', 'ffn_w_down', 'norm_final_g']
BIG = ('w_in', 'w_out', 'ffn_w_up', 'ffn_w_down', 's5_w_glu', 'cv_w_pw')
SMALL_SHARDED = {'cv_w_dw': 2, 'lru_w_conv': 2, 'ffn_w_dw': 2}


def _params(sem=None):
    if sem is None:
        return pltpu.CompilerParams(vmem_limit_bytes=VMEM_LIMIT_BYTES)
    return pltpu.CompilerParams(dimension_semantics=sem, vmem_limit_bytes=VMEM_LIMIT_BYTES)


def _row_tile(rows, cap, mult=SUBLANES):
    best = mult
    for t in range(mult, min(rows, cap) + 1, mult):
        if rows % t == 0:
            best = t
    return best


def _bdot(a, b, dims=NN):
    return lax.dot_general(a.astype(BF16), b.astype(BF16), (dims, ((), ())), preferred_element_type=F32)


@jax.custom_vjp
def bdot(a, b):
    return _bdot(a, b)


def _bdot_fwd(a, b):
    return _bdot(a, b), (a, b)


def _bdot_bwd(res, g):
    a, b = res
    return _bdot(g, b, NT).astype(a.dtype), _bdot(a, g, TN).astype(b.dtype)


bdot.defvjp(_bdot_fwd, _bdot_bwd)


def _mm(name, a, b, out_sds, grid, a_spec, b_spec, o_spec, dims, k_axis=None, add=None, add_spec=None, inner=None):
    nk = grid[k_axis] if k_axis is not None else 1
    has_add = add is not None
    acc_shape = tuple(d for d in o_spec.block_shape if d is not None)
    acc_in_out = out_sds.dtype == F32

    def product(a_ref, b_ref):
        if inner is None:
            return _bdot(a_ref[...], b_ref[...], dims)
        kind, n = inner
        width = a_ref.shape[-1] // n
        acc = None
        for j in range(n):
            a_j = a_ref[j] if kind == "lead" else a_ref[:, j * width:(j + 1) * width]
            p = _bdot(a_j, b_ref[j], dims)
            acc = p if acc is None else acc + p
        return acc

    def body(*refs):
        a_ref, b_ref = refs[0], refs[1]
        add_ref = refs[2] if has_add else None
        o_ref = refs[3] if has_add else refs[2]
        prod = product(a_ref, b_ref).reshape(acc_shape)
        if k_axis is None:
            if has_add:
                prod = prod + add_ref[...]
            o_ref[...] = prod.astype(o_ref.dtype)
        else:
            acc_ref = o_ref if acc_in_out else refs[-1]
            k = pl.program_id(k_axis)

            @pl.when(k == 0)
            def _():
                acc_ref[...] = prod

            @pl.when(k > 0)
            def _():
                acc_ref[...] += prod

            if has_add or not acc_in_out:
                @pl.when(k == nk - 1)
                def _():
                    r = acc_ref[...]
                    if has_add:
                        r = r + add_ref[...]
                    o_ref[...] = r.astype(o_ref.dtype)

    sem = tuple("arbitrary" if d == k_axis else "parallel" for d in range(len(grid)))
    in_specs = [a_spec, b_spec] + ([add_spec] if has_add else [])
    args = (a, b) + ((add,) if has_add else ())
    scratch = [pltpu.VMEM(acc_shape, F32)] if (k_axis is not None and not acc_in_out) else []
    return pl.pallas_call(body, out_shape=out_sds, grid=grid, in_specs=in_specs, out_specs=o_spec,
                          scratch_shapes=scratch, compiler_params=_params(sem), name=name)(*args)


def _rms(x, g):
    return x * lax.rsqrt(jnp.mean(x * x, axis=-1, keepdims=True) + EPS) * g


def rms_fwd(name, x, g, tm):
    rows, d = x.shape

    def body(x_ref, g_ref, o_ref):
        o_ref[...] = _rms(x_ref[...], g_ref[...]).astype(BF16)

    return pl.pallas_call(
        body, out_shape=jax.ShapeDtypeStruct((rows, d), BF16), grid=(rows // tm,),
        in_specs=[pl.BlockSpec((tm, d), lambda i: (i, 0)), pl.BlockSpec((1, d), lambda i: (0, 0))],
        out_specs=pl.BlockSpec((tm, d), lambda i: (i, 0)), compiler_params=_params(("parallel",)), name=name)(x, g)


def rms_bwd(name, x, g, dh, dres, tm):
    rows, d = x.shape

    def body(x_ref, g_ref, dh_ref, dres_ref, dx_ref, dg_ref):
        xv, dy = x_ref[...], dh_ref[...]
        r = lax.rsqrt(jnp.mean(xv * xv, axis=-1, keepdims=True) + EPS)
        dyg = dy * g_ref[...]
        s = jnp.mean(dyg * xv, axis=-1, keepdims=True)
        dx_ref[...] = r * dyg - xv * (r * r * r * s) + dres_ref[...]

        @pl.when(pl.program_id(0) == 0)
        def _():
            dg_ref[...] = jnp.zeros_like(dg_ref)

        dg_ref[...] += jnp.sum(dy * xv * r, axis=0, keepdims=True)

    row = pl.BlockSpec((tm, d), lambda i: (i, 0))
    vec = pl.BlockSpec((1, d), lambda i: (0, 0))
    return pl.pallas_call(
        body, out_shape=(jax.ShapeDtypeStruct((rows, d), F32), jax.ShapeDtypeStruct((1, d), F32)), grid=(rows // tm,),
        in_specs=[row, vec, row, row], out_specs=(row, vec), compiler_params=_params(("arbitrary",)), name=name)(x, g, dh, dres)


def final_loss(name, x, g, target, tm):
    rows, d = x.shape

    def body(x_ref, g_ref, t_ref, l_ref, dx_ref, dg_ref):
        def f(xv, gv):
            e = _rms(xv, gv) - t_ref[...]
            return 0.5 * jnp.sum(jnp.mean(e * e, axis=-1))

        loss, (dx, dg) = jax.value_and_grad(f, argnums=(0, 1))(x_ref[...], g_ref[...])
        dx_ref[...] = dx

        @pl.when(pl.program_id(0) == 0)
        def _():
            l_ref[...] = jnp.zeros_like(l_ref)
            dg_ref[...] = jnp.zeros_like(dg_ref)

        l_ref[...] += jnp.full(l_ref.shape, loss, F32)
        dg_ref[...] += dg

    row = pl.BlockSpec((tm, d), lambda i: (i, 0))
    vec = pl.BlockSpec((1, d), lambda i: (0, 0))
    lspec = pl.BlockSpec((1, 128), lambda i: (0, 0))
    return pl.pallas_call(
        body, out_shape=(jax.ShapeDtypeStruct((1, 128), F32), jax.ShapeDtypeStruct((rows, d), F32), jax.ShapeDtypeStruct((1, d), F32)),
        grid=(rows // tm,), in_specs=[row, vec, row], out_specs=(lspec, row, vec),
        compiler_params=_params(("arbitrary",)), name=name)(x, g, target)


def _rowwise(name, fn, row_ins, par_ins, n_row_out, row_out_dtypes, tm, with_grads=False):
    rows = row_ins[0][0].shape[0]
    n_prim = len(row_ins) - (n_row_out if with_grads else 0)
    n_par = len(par_ins)

    def body(*refs):
        ins = [r[...] for r in refs[:len(row_ins) + n_par]]
        outs = refs[len(row_ins) + n_par:]
        prim, cts, pars = ins[:n_prim], ins[n_prim:len(row_ins)], ins[len(row_ins):]
        if not with_grads:
            res = fn(*prim, *pars)
            for o_ref, r in zip(outs, res):
                o_ref[...] = r.astype(o_ref.dtype)
            return
        _, vjp = jax.vjp(fn, *prim, *[p.astype(F32) for p in pars])
        grads = vjp(tuple(cts))
        for o_ref, gr in zip(outs[:n_prim], grads[:n_prim]):
            o_ref[...] = gr.astype(o_ref.dtype)

        @pl.when(pl.program_id(0) == 0)
        def _():
            for o_ref in outs[n_prim:]:
                o_ref[...] = jnp.zeros_like(o_ref)

        for o_ref, gr in zip(outs[n_prim:], grads[n_prim:]):
            o_ref[...] += gr.astype(F32)

    in_specs = [pl.BlockSpec((tm, w), (lambda i, c=c: (i, c))) for (_, c, w) in row_ins]
    in_specs += [pl.BlockSpec(p.shape, (lambda i, n=p.ndim: (0,) * n)) for p in par_ins]
    args = [a for (a, _, _) in row_ins] + list(par_ins)
    if not with_grads:
        out_shape = tuple(jax.ShapeDtypeStruct((rows, w), dt) for (w, dt) in row_out_dtypes)
        out_specs = tuple(pl.BlockSpec((tm, w), lambda i: (i, 0)) for (w, _) in row_out_dtypes)
        sem = ("parallel",)
    else:
        out_shape = tuple(jax.ShapeDtypeStruct((rows, w), dt) for (w, dt) in row_out_dtypes)
        out_shape += tuple(jax.ShapeDtypeStruct(p.shape, F32) for p in par_ins)
        out_specs = tuple(pl.BlockSpec((tm, w), lambda i: (i, 0)) for (w, _) in row_out_dtypes)
        out_specs += tuple(pl.BlockSpec(p.shape, (lambda i, n=p.ndim: (0,) * n)) for p in par_ins)
        sem = ("arbitrary",)
    return pl.pallas_call(body, out_shape=out_shape, grid=(rows // tm,), in_specs=in_specs, out_specs=out_specs,
                          compiler_params=_params(sem), name=name)(*args)


def _glu(v, g):
    return (v * jax.nn.sigmoid(g),)


def _neg_expm1(z):
    return -jnp.tanh(0.5 * z) * (jnp.exp(z) + 1.0)


def _lru_gate(xc, w_r, w_i, b_r, b_i, lam):
    r = jax.nn.sigmoid(bdot(xc, w_r) + b_r)
    i = jax.nn.sigmoid(bdot(xc, w_i) + b_i)
    log_a = -LRU_C * r * jax.nn.softplus(-lam)
    a = jnp.exp(log_a)
    mult = jnp.sqrt(_neg_expm1(2.0 * log_a))
    return a, mult * (i * xc)


def _layernorm(x, g, b):
    mu = jnp.mean(x, axis=-1, keepdims=True)
    var = jnp.mean(jnp.square(x - mu), axis=-1, keepdims=True)
    return (x - mu) * lax.rsqrt(var + EPS) * g + b


def _mix_post(y_ssm, u, h1, hseq, lru_g, dgp, s5_d, w_glu, b_glu, ln_g, ln_b, w_pw, b_pw, pool_bd, pool_scale):
    y = y_ssm + s5_d * u
    gl = jax.nn.gelu(y, approximate=True)
    out_s5 = gl * jax.nn.sigmoid(bdot(gl, w_glu) + b_glu)
    out_cv = bdot(jax.nn.silu(_layernorm(h1, ln_g, ln_b)), w_pw) + b_pw
    out_lru = hseq * jax.nn.gelu(lru_g, approximate=True)
    out_pool = bdot(dgp, pool_bd) * pool_scale
    return (jnp.concatenate([out_s5, out_cv, out_lru, out_pool], axis=-1),)


def _gelu_terms(x):
    sq = x * x
    t = jnp.tanh(x * (GELU_K0 + (GELU_K0 * GELU_K1) * sq))
    return sq, t, 0.5 + 0.5 * t


def _halo_rows(taps):
    return -(-(taps - 1) // SUBLANES) * SUBLANES


def _row_windows(ext, offsets, n, shifted_ref=None):
    if shifted_ref is None:
        return {off: ext[off:off + n] for off in offsets}
    room = ext.shape[0] - SUBLANES
    slots, out = {}, {}
    for off in offsets:
        r = off % SUBLANES
        if r == 0:
            out[off] = ext[off:off + n]
            continue
        if r not in slots:
            slots[r] = len(slots)
            shifted_ref[slots[r]] = ext[r:r + room]
        out[off] = shifted_ref[slots[r], off - r:off - r + n, :]
    return out


def _shift_scratch(taps, tm, c):
    return [pltpu.VMEM((SUBLANES - 1, _halo_rows(taps) + tm - SUBLANES, c), F32)] if taps > SUBLANES else []


def dwconv_fwd(name, x, cblk, c, w, b, taps, tm, out_dtype=F32):
    nb = w.shape[0]
    rows = x.shape[1]
    halo = _halo_rows(taps)
    per = tm // halo

    def body(x_ref, h_ref, w_ref, b_ref, o_ref, *shifted):
        i = pl.program_id(1)
        prev = jnp.where(i > 0, h_ref[...], 0.0)
        ext = jnp.concatenate([prev, x_ref[...]], axis=0)
        win = _row_windows(ext, [halo - (taps - 1) + k for k in range(taps)], tm, *shifted)
        acc = jnp.broadcast_to(b_ref[...], (tm, c))
        for k in range(taps):
            acc = acc + w_ref[k:k + 1, :] * win[halo - (taps - 1) + k]
        o_ref[...] = acc.astype(o_ref.dtype)

    return pl.pallas_call(
        body, out_shape=jax.ShapeDtypeStruct((nb, rows, c), out_dtype), grid=(nb, rows // tm),
        in_specs=[pl.BlockSpec((None, tm, c), lambda n, i: (n, i, cblk)),
                  pl.BlockSpec((None, halo, c), lambda n, i: (n, jnp.maximum(i * per - 1, 0), cblk)),
                  pl.BlockSpec((None, taps, c), lambda n, i: (n, 0, 0)),
                  pl.BlockSpec((None, 1, c), lambda n, i: (n, 0, 0))],
        out_specs=pl.BlockSpec((None, tm, c), lambda n, i: (n, i, 0)), scratch_shapes=_shift_scratch(taps, tm, c),
        compiler_params=_params(("parallel", "parallel")), name=name)(x, x, w, b)


def dwconv_bwd(name, dy, x, cblk, c, w, taps, tm, dx_dtype=F32):
    nb = w.shape[0]
    rows = x.shape[1]
    halo = _halo_rows(taps)
    per = tm // halo
    n_tiles = rows // tm
    last_halo = rows // halo - 1

    def body(dy_ref, dn_ref, x_ref, xp_ref, w_ref, dx_ref, dw_ref, db_ref, *shifted):
        i = pl.program_id(1)
        dyv = dy_ref[...]
        nxt = jnp.where(i < n_tiles - 1, dn_ref[...], 0.0)
        dext = jnp.concatenate([dyv, nxt], axis=0)
        prev = jnp.where(i > 0, xp_ref[...], 0.0)
        xext = jnp.concatenate([prev, x_ref[...]], axis=0)
        acc = jnp.zeros((tm, c), F32)

        @pl.when(i == 0)
        def _():
            dw_ref[...] = jnp.zeros_like(dw_ref)
            db_ref[...] = jnp.zeros_like(db_ref)

        dwin = _row_windows(dext, list(range(taps)), tm, *shifted[:1])
        xwin = _row_windows(xext, [halo - (taps - 1) + k for k in range(taps)], tm, *shifted[1:])
        for k in range(taps):
            acc = acc + w_ref[k:k + 1, :] * dwin[taps - 1 - k]
            dw_ref[k:k + 1, :] += jnp.sum(dyv * xwin[halo - (taps - 1) + k], axis=0, keepdims=True)
        dx_ref[...] = acc.astype(dx_ref.dtype)
        db_ref[...] += jnp.sum(dyv, axis=0, keepdims=True)

    return pl.pallas_call(
        body, out_shape=(jax.ShapeDtypeStruct((nb, rows, c), dx_dtype), jax.ShapeDtypeStruct((nb, taps, c), F32),
                         jax.ShapeDtypeStruct((nb, 1, c), F32)),
        grid=(nb, n_tiles),
        in_specs=[pl.BlockSpec((None, tm, c), lambda n, i: (n, i, 0)),
                  pl.BlockSpec((None, halo, c), lambda n, i: (n, jnp.minimum((i + 1) * per, last_halo), 0)),
                  pl.BlockSpec((None, tm, c), lambda n, i: (n, i, cblk)),
                  pl.BlockSpec((None, halo, c), lambda n, i: (n, jnp.maximum(i * per - 1, 0), cblk)),
                  pl.BlockSpec((None, taps, c), lambda n, i: (n, 0, 0))],
        out_specs=(pl.BlockSpec((None, tm, c), lambda n, i: (n, i, 0)), pl.BlockSpec((None, taps, c), lambda n, i: (n, 0, 0)),
                   pl.BlockSpec((None, 1, c), lambda n, i: (n, 0, 0))),
        scratch_shapes=2 * _shift_scratch(taps, tm, c),
        compiler_params=_params(("parallel", "arbitrary")), name=name)(dy, dy, x, x, w)


def ffn_gate_fwd(name, up, w, b, tm):
    _, rows, c = up.shape
    halo = _halo_rows(FFN_TAPS)
    per = tm // halo

    def body(g_ref, gp_ref, v_ref, w_ref, b_ref, o_ref):
        i = pl.program_id(1)
        ext = jnp.concatenate([jnp.where(i > 0, gp_ref[...], 0.0), g_ref[...]], axis=0)
        gc = jnp.broadcast_to(b_ref[...], (tm, c))
        for k in range(FFN_TAPS):
            off = halo - (FFN_TAPS - 1) + k
            gc = gc + w_ref[k:k + 1, :] * ext[off:off + tm]
        o_ref[...] = (gc * _gelu_terms(gc)[2] * v_ref[...]).astype(BF16)

    return pl.pallas_call(
        body, out_shape=jax.ShapeDtypeStruct((2, rows, c), BF16), grid=(2, rows // tm),
        in_specs=[pl.BlockSpec((None, tm, c), lambda h, i: (h, i, 0)),
                  pl.BlockSpec((None, halo, c), lambda h, i: (h, jnp.maximum(i * per - 1, 0), 0)),
                  pl.BlockSpec((None, tm, c), lambda h, i: (h + 2, i, 0)),
                  pl.BlockSpec((None, FFN_TAPS, c), lambda h, i: (h, 0, 0)), pl.BlockSpec((None, 1, c), lambda h, i: (h, 0, 0))],
        out_specs=pl.BlockSpec((None, tm, c), lambda h, i: (h, i, 0)),
        compiler_params=_params(("parallel", "parallel")), name=name)(up, up, up, w, b)


def ffn_gate_bwd(name, up, dact, w, b, tm):
    _, rows, c = up.shape
    halo = _halo_rows(FFN_TAPS)
    per = tm // halo
    n_tiles = rows // tm
    last_halo = rows // halo - 1
    n_ext = tm + halo

    def body(g_ref, gp_ref, gn_ref, v_ref, vn_ref, d_ref, dn_ref, w_ref, b_ref, dup_ref, dw_ref, db_ref):
        i = pl.program_id(1)
        gext = jnp.concatenate([jnp.where(i > 0, gp_ref[...], 0.0), g_ref[...], gn_ref[...]], axis=0)
        shifted = [gext[halo - (FFN_TAPS - 1) + k:halo - (FFN_TAPS - 1) + k + n_ext] for k in range(FFN_TAPS)]
        gc = jnp.broadcast_to(b_ref[...], (n_ext, c))
        for k in range(FFN_TAPS):
            gc = gc + w_ref[k:k + 1, :] * shifted[k]
        vext = jnp.concatenate([v_ref[...], vn_ref[...]], axis=0)
        dext = jnp.concatenate([d_ref[...], dn_ref[...]], axis=0)
        sq, t, half = _gelu_terms(gc)
        dval = dext * (gc * half)
        dgc = (dext * vext) * (half + (0.5 * gc) * (1.0 - t * t) * (GELU_K0 + (3.0 * GELU_K0 * GELU_K1) * sq))
        r = lax.broadcasted_iota(jnp.int32, (n_ext, c), 0)
        dgc = jnp.where((r < tm) | (i < n_tiles - 1), dgc, 0.0)
        dgate = jnp.zeros((tm, c), F32)
        for k in range(FFN_TAPS):
            dgate = dgate + w_ref[k:k + 1, :] * dgc[FFN_TAPS - 1 - k:FFN_TAPS - 1 - k + tm]
        dup_ref[0] = dgate.astype(BF16)
        dup_ref[1] = dval[:tm].astype(BF16)

        @pl.when(i == 0)
        def _():
            dw_ref[...] = jnp.zeros_like(dw_ref)
            db_ref[...] = jnp.zeros_like(db_ref)

        dgc_t = dgc[:tm]
        for k in range(FFN_TAPS):
            dw_ref[k:k + 1, :] += jnp.sum(dgc_t * shifted[k][:tm], axis=0, keepdims=True)
        db_ref[...] += jnp.sum(dgc_t, axis=0, keepdims=True)

    def tile(shift):
        return pl.BlockSpec((None, tm, c), lambda h, i: (h + shift, i, 0))

    def after(shift):
        return pl.BlockSpec((None, halo, c), lambda h, i: (h + shift, jnp.minimum((i + 1) * per, last_halo), 0))

    return pl.pallas_call(
        body, out_shape=(jax.ShapeDtypeStruct((2, 2, rows, c), BF16), jax.ShapeDtypeStruct((2, FFN_TAPS, c), F32),
                         jax.ShapeDtypeStruct((2, 1, c), F32)),
        grid=(2, n_tiles),
        in_specs=[tile(0), pl.BlockSpec((None, halo, c), lambda h, i: (h, jnp.maximum(i * per - 1, 0), 0)), after(0),
                  tile(2), after(2), tile(0), after(0),
                  pl.BlockSpec((None, FFN_TAPS, c), lambda h, i: (h, 0, 0)), pl.BlockSpec((None, 1, c), lambda h, i: (h, 0, 0))],
        out_specs=(pl.BlockSpec((2, None, tm, c), lambda h, i: (0, h, i, 0)), pl.BlockSpec((None, FFN_TAPS, c), lambda h, i: (h, 0, 0)),
                   pl.BlockSpec((None, 1, c), lambda h, i: (h, 0, 0))),
        compiler_params=_params(("parallel", "arbitrary")), name=name)(up, up, up, up, up, dact, dact, w, b)


POOL_HALO = 16


def pool_fwd(name, proj, cblk, tm):
    rows = proj.shape[0]
    c = 128 * len(POOL_WINDOWS)
    per = tm // POOL_HALO

    def body(x_ref, h_ref, o_ref):
        i = pl.program_id(0)
        xv = x_ref[...]
        ext = jnp.concatenate([jnp.where(i > 0, h_ref[...], 0.0), xv], axis=0)
        t1 = (lax.broadcasted_iota(jnp.int32, (tm, 128), 0) + i * tm + 1).astype(F32)
        outs = []
        for gi, win in enumerate(POOL_WINDOWS):
            seg = ext[:, gi * 128:(gi + 1) * 128]
            s = seg[POOL_HALO:POOL_HALO + tm]
            for j in range(1, win):
                s = s + seg[POOL_HALO - j:POOL_HALO - j + tm]
            outs.append(s / jnp.minimum(t1, float(win)) - xv[:, gi * 128:(gi + 1) * 128])
        o_ref[...] = jnp.concatenate(outs, axis=-1)

    return pl.pallas_call(
        body, out_shape=jax.ShapeDtypeStruct((rows, c), F32), grid=(rows // tm,),
        in_specs=[pl.BlockSpec((tm, c), lambda i: (i, cblk)),
                  pl.BlockSpec((POOL_HALO, c), lambda i: (jnp.maximum(i * per - 1, 0), cblk))],
        out_specs=pl.BlockSpec((tm, c), lambda i: (i, 0)), compiler_params=_params(("parallel",)), name=name)(proj, proj)


def pool_bwd(name, dd, tm):
    rows, c = dd.shape
    per = tm // POOL_HALO
    n_tiles = rows // tm
    last_halo = rows // POOL_HALO - 1

    def body(d_ref, n_ref, o_ref):
        i = pl.program_id(0)
        dv = d_ref[...]
        nxt = jnp.where(i < n_tiles - 1, n_ref[...], 0.0)
        t1 = (lax.broadcasted_iota(jnp.int32, (tm, 128), 0) + i * tm + 1).astype(F32)
        t1n = (lax.broadcasted_iota(jnp.int32, (POOL_HALO, 128), 0) + (i + 1) * tm + 1).astype(F32)
        outs = []
        for gi, win in enumerate(POOL_WINDOWS):
            sl = slice(gi * 128, (gi + 1) * 128)
            q = jnp.concatenate([dv[:, sl] / jnp.minimum(t1, float(win)), nxt[:, sl] / jnp.minimum(t1n, float(win))], axis=0)
            s = q[0:tm]
            for j in range(1, win):
                s = s + q[j:j + tm]
            outs.append(s - dv[:, sl])
        o_ref[...] = jnp.concatenate(outs, axis=-1)

    return pl.pallas_call(
        body, out_shape=jax.ShapeDtypeStruct((rows, c), F32), grid=(n_tiles,),
        in_specs=[pl.BlockSpec((tm, c), lambda i: (i, 0)),
                  pl.BlockSpec((POOL_HALO, c), lambda i: (jnp.minimum((i + 1) * per, last_halo), 0))],
        out_specs=pl.BlockSpec((tm, c), lambda i: (i, 0)), compiler_params=_params(("parallel",)), name=name)(dd, dd)


BLOCK_STEPS = 3


def _shift_down(v, s, fill):
    r = lax.broadcasted_iota(jnp.int32, v.shape, 0)
    return jnp.where(r >= s, pltpu.roll(v, s, 0), fill)


def _shift_up(v, s, fill):
    n = v.shape[0]
    r = lax.broadcasted_iota(jnp.int32, v.shape, 0)
    return jnp.where(r < n - s, pltpu.roll(v, n - s, 0), fill)


def _shift_in_blocks(v, s, fill, reverse):
    n = v.shape[0]
    q = lax.broadcasted_iota(jnp.int32, v.shape, 0) & (SUBLANES - 1)
    if reverse:
        return jnp.where(q < SUBLANES - s, pltpu.roll(v, n - s, 0), fill)
    return jnp.where(q >= s, pltpu.roll(v, s, 0), fill)


def _cscan_blocks(vr, vi, powers, reverse):
    for k, (qr, qi) in enumerate(powers):
        s = 1 << k
        sr, si = _shift_in_blocks(vr, s, 0.0, reverse), _shift_in_blocks(vi, s, 0.0, reverse)
        if reverse:
            vr, vi = vr + qr * sr + qi * si, vi + qr * si - qi * sr
        else:
            vr, vi = vr + qr * sr - qi * si, vi + qr * si + qi * sr
    return vr, vi


def _cscan_table(pr, pi, powers, reverse):
    r = lax.broadcasted_iota(jnp.int32, (SUBLANES, 128), 0)
    at = (SUBLANES - 1) if reverse else 0
    return _cscan_blocks(jnp.where(r == at, pr, 0.0), jnp.where(r == at, -pi if reverse else pi, 0.0), powers, reverse)


def _cscan_chunk(vr, vi, powers, table, carry, reverse):
    vr, vi = _cscan_blocks(vr, vi, powers, reverse)
    tr, ti = table
    cr, ci = carry
    nb = vr.shape[0] // SUBLANES
    outr, outi = [None] * nb, [None] * nb
    edge = 0 if reverse else SUBLANES - 1
    for j in (reversed(range(nb)) if reverse else range(nb)):
        rows = slice(j * SUBLANES, (j + 1) * SUBLANES)
        zr = vr[rows] + tr * cr - ti * ci
        zi = vi[rows] + tr * ci + ti * cr
        outr[j], outi[j] = zr, zi
        cr, ci = zr[edge:edge + 1], zi[edge:edge + 1]
    return jnp.concatenate(outr, axis=0), jnp.concatenate(outi, axis=0), (cr, ci)


def _powers(pr, pi, n):
    out = [(pr, pi)]
    for _ in range(n - 1):
        pr, pi = pr * pr - pi * pi, 2.0 * pr * pi
        out.append((pr, pi))
    return out


def s5_scan_fwd(name, bu, a):
    _, rows, n = bu.shape
    t = min(SCAN_CHUNK, rows)

    def body(bu_ref, a_ref, z_ref):
        pr, pi = a_ref[0], a_ref[1]
        powers = _powers(pr, pi, BLOCK_STEPS)
        table = _cscan_table(pr, pi, powers, False)

        def chunk(ci, carry):
            base = pl.multiple_of(ci * t, t)
            zr, zi, carry = _cscan_chunk(bu_ref[0, pl.ds(base, t), :], bu_ref[1, pl.ds(base, t), :], powers, table, carry, False)
            z_ref[0, pl.ds(base, t), :] = zr
            z_ref[1, pl.ds(base, t), :] = zi
            return carry

        zero = jnp.zeros((1, 128), F32)
        lax.fori_loop(0, rows // t, chunk, (zero, zero))

    return pl.pallas_call(
        body, out_shape=jax.ShapeDtypeStruct((2, rows, n), F32), grid=(n // 128,),
        in_specs=[pl.BlockSpec((2, rows, 128), lambda j: (0, 0, j)), pl.BlockSpec((2, 1, 128), lambda j: (0, 0, j))],
        out_specs=pl.BlockSpec((2, rows, 128), lambda j: (0, 0, j)), compiler_params=_params(("parallel",)), name=name)(bu, a)


def s5_scan_bwd(name, dz, z, a):
    _, rows, n = dz.shape
    t = min(SCAN_CHUNK, rows)
    n_chunks = rows // t

    def body(dz_ref, z_ref, a_ref, lam_ref, da_ref):
        pr, pi = a_ref[0], a_ref[1]
        powers = _powers(pr, pi, BLOCK_STEPS)
        table = _cscan_table(pr, pi, powers, True)

        def chunk(k, carry):
            ci = n_chunks - 1 - k
            base = pl.multiple_of(ci * t, t)
            cr, cim, dar, dai = carry
            lr, li, (cr, cim) = _cscan_chunk(dz_ref[0, pl.ds(base, t), :], dz_ref[1, pl.ds(base, t), :], powers, table, (cr, cim), True)
            lam_ref[0, pl.ds(base, t), :] = lr
            lam_ref[1, pl.ds(base, t), :] = li
            pbase = pl.multiple_of(jnp.maximum(base - SUBLANES, 0), SUBLANES)
            keep = (ci > 0).astype(F32)
            pzr = z_ref[0, pl.ds(pbase, SUBLANES), :][SUBLANES - 1:SUBLANES, :] * keep
            pzi = z_ref[1, pl.ds(pbase, SUBLANES), :][SUBLANES - 1:SUBLANES, :] * keep
            zpr = _shift_down(z_ref[0, pl.ds(base, t), :], 1, pzr)
            zpi = _shift_down(z_ref[1, pl.ds(base, t), :], 1, pzi)
            dar = dar + jnp.sum(lr * zpr + li * zpi, axis=0, keepdims=True)
            dai = dai + jnp.sum(li * zpr - lr * zpi, axis=0, keepdims=True)
            return cr, cim, dar, dai

        zero = jnp.zeros((1, 128), F32)
        _, _, dar, dai = lax.fori_loop(0, n_chunks, chunk, (zero, zero, zero, zero))
        da_ref[0] = dar
        da_ref[1] = dai

    seq = pl.BlockSpec((2, rows, 128), lambda j: (0, 0, j))
    vec = pl.BlockSpec((2, 1, 128), lambda j: (0, 0, j))
    return pl.pallas_call(
        body, out_shape=(jax.ShapeDtypeStruct((2, rows, n), F32), jax.ShapeDtypeStruct((2, 1, n), F32)), grid=(n // 128,),
        in_specs=[seq, seq, vec], out_specs=(seq, vec), compiler_params=_params(("parallel",)), name=name)(dz, z, a)


def _rscan_chunk(a, b, carry, reverse):
    n = a.shape[0]
    shift = _shift_up if reverse else _shift_down
    for k in range(n.bit_length() - 1):
        s = 1 << k
        b = b + a * shift(b, s, 0.0)
        a = a * shift(a, s, 1.0)
    h = b + a * carry
    edge = 0 if reverse else n - 1
    return h, h[edge:edge + 1]


def lru_scan_fwd(name, a, b):
    rows, n = a.shape
    t = min(SCAN_CHUNK, rows)

    def body(a_ref, b_ref, h_ref):
        def chunk(ci, carry):
            base = pl.multiple_of(ci * t, t)
            h, carry = _rscan_chunk(a_ref[pl.ds(base, t), :], b_ref[pl.ds(base, t), :], carry, False)
            h_ref[pl.ds(base, t), :] = h
            return carry

        lax.fori_loop(0, rows // t, chunk, jnp.zeros((1, 128), F32))

    seq = pl.BlockSpec((rows, 128), lambda j: (0, j))
    return pl.pallas_call(body, out_shape=jax.ShapeDtypeStruct((rows, n), F32), grid=(n // 128,), in_specs=[seq, seq],
                          out_specs=seq, compiler_params=_params(("parallel",)), name=name)(a, b)


def lru_scan_bwd(name, dh, a, h):
    rows, n = a.shape
    t = min(SCAN_CHUNK, rows)
    n_chunks = rows // t

    def body(dh_ref, a_ref, h_ref, da_ref, db_ref):
        def chunk(k, carry):
            ci = n_chunks - 1 - k
            base = pl.multiple_of(ci * t, t)
            nbase = pl.multiple_of(jnp.minimum(base + t, rows - SUBLANES), SUBLANES)
            a_next = a_ref[pl.ds(nbase, SUBLANES), :][0:1, :]
            an = _shift_up(a_ref[pl.ds(base, t), :], 1, a_next)
            mu, carry = _rscan_chunk(an, dh_ref[pl.ds(base, t), :], carry, True)
            pbase = pl.multiple_of(jnp.maximum(base - SUBLANES, 0), SUBLANES)
            hp_row = h_ref[pl.ds(pbase, SUBLANES), :][SUBLANES - 1:SUBLANES, :] * (ci > 0).astype(F32)
            hp = _shift_down(h_ref[pl.ds(base, t), :], 1, hp_row)
            da_ref[pl.ds(base, t), :] = mu * hp
            db_ref[pl.ds(base, t), :] = mu
            return carry

        lax.fori_loop(0, n_chunks, chunk, jnp.zeros((1, 128), F32))

    seq = pl.BlockSpec((rows, 128), lambda j: (0, j))
    return pl.pallas_call(
        body, out_shape=(jax.ShapeDtypeStruct((rows, n), F32), jax.ShapeDtypeStruct((rows, n), F32)), grid=(n // 128,),
        in_specs=[seq, seq, seq], out_specs=(seq, seq), compiler_params=_params(("parallel",)), name=name)(dh, a, h)


def _s5_param(lr, li, ls, bre, bim):
    st = jnp.exp(ls)
    er = jnp.exp(lr * st)
    th = li * st
    ar, ai = er * jnp.cos(th), er * jnp.sin(th)
    nr, ni = ar - 1.0, ai
    den = lr * lr + li * li
    cr, ci = (nr * lr + ni * li) / den, (ni * lr - nr * li) / den
    return ar, ai, cr * bre - ci * bim, cr * bim + ci * bre


def s5_param_fwd(name, lr, li, ls, bre, bim):
    gh, n = bre.shape

    def body(lr_ref, li_ref, ls_ref, bre_ref, bim_ref, a_ref, bb_ref):
        ar, ai, br, bi = _s5_param(lr_ref[...], li_ref[...], ls_ref[...], bre_ref[...], bim_ref[...])
        a_ref[0] = ar
        a_ref[1] = ai
        bb_ref[0] = br.astype(BF16)
        bb_ref[1] = bi.astype(BF16)

    return pl.pallas_call(body, out_shape=(jax.ShapeDtypeStruct((2, 1, n), F32), jax.ShapeDtypeStruct((2, gh, n), BF16)),
                          compiler_params=_params(), name=name)(lr, li, ls, bre, bim)


def s5_param_bwd(name, lr, li, ls, bre, bim, da, dbb, gsum):
    gh, n = bre.shape

    def body(lr_ref, li_ref, ls_ref, bre_ref, bim_ref, da_ref, dbb_ref, gs_ref, dlr_ref, dli_ref, dls_ref, dbre_ref, dbim_ref):
        _, vjp = jax.vjp(_s5_param, lr_ref[...], li_ref[...], ls_ref[...], bre_ref[...], bim_ref[...])
        dlr, dli, dls, dbre, dbim = vjp((da_ref[0], da_ref[1], dbb_ref[0], dbb_ref[1]))
        dlr_ref[...] = dlr
        dli_ref[...] = dli
        dls_ref[...] = jnp.dot(jnp.broadcast_to(dls, (SUBLANES, n)), gs_ref[...], preferred_element_type=F32,
                               precision=lax.Precision.HIGHEST)
        dbre_ref[...] = dbre
        dbim_ref[...] = dbim

    vec = jax.ShapeDtypeStruct((1, n), F32)
    mat = jax.ShapeDtypeStruct((gh, n), F32)
    return pl.pallas_call(body, out_shape=(vec, vec, jax.ShapeDtypeStruct((SUBLANES, 128), F32), mat, mat),
                          compiler_params=_params(), name=name)(lr, li, ls, bre, bim, da, dbb, gsum)


def sum_lead(name, x, tr):
    n, rows, cols = x.shape

    def body(x_ref, o_ref):
        acc = x_ref[0]
        for j in range(1, n):
            acc = acc + x_ref[j]
        o_ref[...] = acc

    return pl.pallas_call(
        body, out_shape=jax.ShapeDtypeStruct((rows, cols), x.dtype), grid=(rows // tr,),
        in_specs=[pl.BlockSpec((n, tr, cols), lambda i: (0, i, 0))], out_specs=pl.BlockSpec((tr, cols), lambda i: (i, 0)),
        compiler_params=_params(("parallel",)), name=name)(x)


def _adamw(w, g, m, v):
    m = ADAM_B1 * m + (1.0 - ADAM_B1) * g
    v = ADAM_B2 * v + (1.0 - ADAM_B2) * jnp.square(g)
    m_hat = m / (1.0 - ADAM_B1 ** ADAM_STEP)
    v_hat = v / (1.0 - ADAM_B2 ** ADAM_STEP)
    delta = -ADAM_LR * (m_hat / (jnp.sqrt(v_hat) + ADAM_EPS) + ADAM_WD * w)
    return delta, m, v


def adamw_sharded(name, w, m, v, g0, g1, split_cols, tile):
    _, r, c = w.shape
    if split_cols:
        nt = c // tile
        per = (c // 2) // tile
        block = (None, r, tile)
        wspec = pl.BlockSpec(block, lambda l, t: (l, 0, t))
        gidx = lambda t: (t // per, 0, t % per)
    else:
        nt = r // tile
        per = (r // 2) // tile
        block = (None, tile, c)
        wspec = pl.BlockSpec(block, lambda l, t: (l, t, 0))
        gidx = lambda t: (t // per, t % per, 0)

    def gspec(layer):
        return pl.BlockSpec(block, lambda l, t: gidx(jnp.where(l == layer, t, (nt - 1) * (1 - layer))))

    def body(w_ref, m_ref, v_ref, g0_ref, g1_ref, g_ref, d_ref, nm_ref, nv_ref):
        g = jnp.where(pl.program_id(0) == 0, g0_ref[...], g1_ref[...])
        d, nm, nv = _adamw(w_ref[...], g, m_ref[...], v_ref[...])
        g_ref[...] = g
        d_ref[...] = d
        nm_ref[...] = nm
        nv_ref[...] = nv

    sds = jax.ShapeDtypeStruct(w.shape, F32)
    return pl.pallas_call(body, out_shape=(sds,) * 4, grid=(2, nt), in_specs=[wspec, wspec, wspec, gspec(0), gspec(1)],
                          out_specs=(wspec,) * 4, compiler_params=_params(("arbitrary", "arbitrary")), name=name)(w, m, v, g0, g1)


def adamw_flat(name, w, g, m, v):
    rows, cols = w.shape
    tr = _row_tile(rows, 1024)

    def body(w_ref, g_ref, m_ref, v_ref, d_ref, nm_ref, nv_ref):
        d, nm, nv = _adamw(w_ref[...], g_ref[...], m_ref[...], v_ref[...])
        d_ref[...] = d
        nm_ref[...] = nm
        nv_ref[...] = nv

    blk = pl.BlockSpec((tr, cols), lambda i: (i, 0))
    sds = jax.ShapeDtypeStruct((rows, cols), F32)
    return pl.pallas_call(body, out_shape=(sds,) * 3, grid=(rows // tr,), in_specs=[blk] * 4, out_specs=(blk,) * 3,
                          compiler_params=_params(("parallel",)), name=name)(w, g, m, v)


def _flips(axes):
    out = []
    for fx in ((0, 1) if "x" in axes else (0,)):
        for fy in ((0, 1) if "y" in axes else (0,)):
            for fc in ((0, 1) if "c" in axes else (0,)):
                if fx or fy or fc:
                    out.append((fx, fy, fc))
    return out


def _slot(pos, axes):
    s = 0
    for name, p in zip(("x", "y", "c"), pos):
        if name in axes:
            s = 2 * s + p
    return s


_HBM = pl.BlockSpec(memory_space=pltpu.HBM)
_SEM = pl.BlockSpec(memory_space=pltpu.SEMAPHORE)
_EFFECT = pltpu.SideEffectType.DATAFLOW_SIDE_EFFECTING


def place_own(name, arrs, axes):
    n = len(_flips(axes)) + 1
    na = len(arrs)

    def body(*refs):
        ins, outs, sems = refs[:na], refs[na:2 * na], refs[2 * na]
        my = _slot((lax.axis_index("x"), lax.axis_index("y"), lax.axis_index("c")), axes)
        copies = [pltpu.make_async_copy(ins[a], outs[a].at[my], sems.at[a]) for a in range(na)]
        for cp in copies:
            cp.start()
        for cp in copies:
            cp.wait()

    out_shape = tuple(jax.ShapeDtypeStruct((n,) + a.shape, a.dtype) for a in arrs)
    anyspec = pl.BlockSpec(memory_space=pl.ANY)
    return pl.pallas_call(body, out_shape=out_shape, in_specs=[anyspec] * na, out_specs=(anyspec,) * na,
                          scratch_shapes=[pltpu.SemaphoreType.DMA((na,))], name=name)(*arrs)


def _peers(axes):
    me = (lax.axis_index("x"), lax.axis_index("y"), lax.axis_index("c"))
    return me, [tuple((1 - p) if f else p for p, f in zip(me, fl)) for fl in _flips(axes)]


def place_tile(name, arr, layer, my, slots=4, dtype=BF16, after=None):
    _, r, cols = arr.shape
    tr = _tile_rows(r, cols)

    def body(my_ref, x_ref, *rest):
        rest[-1][...] = x_ref[...].astype(dtype)

    in_specs = [pl.BlockSpec((None, tr, cols), lambda i, my: (layer, i, 0))]
    args = [arr]
    if after is not None:
        in_specs.append(pl.BlockSpec(after.shape, lambda i, my: (0, 0)))
        args.append(after)
    grid_spec = pltpu.PrefetchScalarGridSpec(num_scalar_prefetch=1, grid=(r // tr,), in_specs=in_specs,
                                             out_specs=pl.BlockSpec((None, tr, cols), lambda i, my: (my[0], i, 0)))
    return pl.pallas_call(body, out_shape=jax.ShapeDtypeStruct((slots, r, cols), dtype), grid_spec=grid_spec,
                          compiler_params=_params(("parallel",)), name=name)(my, *args)


def exchange_start(name, groups, axes, scatter):
    flat = [(p if scatter else (p,)) for grp in groups for p in grp]
    per = 2 if scatter else 1
    na, ng, npeer = len(flat), len(groups), len(_flips(axes))

    def body(*refs):
        ops = refs[:per * na]
        zones = ops[(per - 1) * na:]
        sems, token = refs[per * na:per * na + 2 * ng], refs[-1]
        me, peers = _peers(axes)
        my = _slot(me, axes)
        ai = 0
        for g, grp in enumerate(groups):
            for k in range(len(grp)):
                for j, peer in enumerate(peers):
                    src = ops[ai].at[_slot(peer, axes)] if scatter else zones[ai].at[my]
                    dst = zones[ai].at[j] if scatter else zones[ai].at[my]
                    pltpu.make_async_remote_copy(
                        src_ref=src, dst_ref=dst, send_sem=sems[2 * g].at[k * npeer + j],
                        recv_sem=sems[2 * g + 1].at[k * npeer + j], device_id=peer, device_id_type=pl.DeviceIdType.MESH).start()
                ai += 1
        token[...] = jnp.zeros_like(token)

    out_shape, out_specs = [], []
    for grp in groups:
        out_shape += [pltpu.SemaphoreType.DMA((npeer * len(grp),))] * 2
        out_specs += [_SEM, _SEM]
    for idx in range(per):
        out_shape += [pltpu.HBM(p[idx].shape, p[idx].dtype) for p in flat]
        out_specs += [_HBM] * na
    out_shape.append(jax.ShapeDtypeStruct((SUBLANES, 128), F32))
    out_specs.append(pl.BlockSpec(memory_space=pltpu.VMEM))
    args = [pltpu.with_memory_space_constraint(p[idx], pltpu.HBM) for idx in range(per) for p in flat]
    res = pl.pallas_call(body, out_shape=tuple(out_shape), in_specs=[_HBM] * (per * na), out_specs=tuple(out_specs),
                         input_output_aliases={i: 2 * ng + i for i in range(per * na)},
                         compiler_params=pltpu.CompilerParams(has_side_effects=_EFFECT), name=name)(*args)
    thru = res[2 * ng:2 * ng + per * na]
    out, ai = [], 0
    for g, grp in enumerate(groups):
        srcs = list(thru[ai:ai + len(grp)]) if scatter else []
        zones = list(thru[(per - 1) * na + ai:(per - 1) * na + ai + len(grp)])
        out.append(((res[2 * g], res[2 * g + 1]), srcs, zones))
        ai += len(grp)
    return out, res[-1]


def exchange_wait(name, group, after, axes, scatter):
    (send_sems, recv_sems), srcs, zones = group
    n, ns = len(zones), len(srcs)
    npeer = len(_flips(axes))

    def body(*refs):
        z_refs = refs[ns:ns + n]
        ssem, rsem = refs[ns + n], refs[ns + n + 1]
        _, peers = _peers(axes)
        for k in range(n):
            for j, peer in enumerate(peers):
                part = z_refs[k].at[j if scatter else _slot(peer, axes)]
                copy = pltpu.make_async_remote_copy(
                    src_ref=part, dst_ref=part, send_sem=ssem.at[k * npeer + j], recv_sem=rsem.at[k * npeer + j],
                    device_id=peer, device_id_type=pl.DeviceIdType.MESH)
                copy.wait_send()
                copy.wait_recv()

    ops = list(srcs) + list(zones)
    out_shape = tuple(pltpu.HBM(a.shape, a.dtype) for a in ops)
    res = pl.pallas_call(body, out_shape=out_shape, in_specs=[_HBM] * len(ops) + [_SEM, _SEM, pl.BlockSpec(memory_space=pl.ANY)],
                         out_specs=(_HBM,) * len(ops), input_output_aliases={i: i for i in range(len(ops))},
                         compiler_params=pltpu.CompilerParams(has_side_effects=_EFFECT), name=name)(*ops, send_sems, recv_sems, after)
    return list(res[:ns]), list(res[ns:])


def _pair_exchange(name, ins, in_specs, n_steps, tile, fn_send, fn_out, out_shape, out_spec, prefetch=None, wire=F32):
    n_in = len(ins)

    def body(*refs):
        if prefetch is not None:
            refs = refs[1:]
        in_refs, o_ref = refs[:n_in], refs[n_in]
        send_buf, recv_buf, send_sems, recv_sems, credit = refs[n_in + 1:]
        i = pl.program_id(0)
        slot = lax.rem(i, 2)
        c = lax.axis_index("c")
        sibling = (lax.axis_index("x"), lax.axis_index("y"), 1 - c)
        vals = [r[...] for r in in_refs]
        send_buf[slot] = fn_send(*vals, c).astype(wire)

        @pl.when(i >= 2)
        def _():
            pl.semaphore_wait(credit, 1)

        copy = pltpu.make_async_remote_copy(
            src_ref=send_buf.at[slot], dst_ref=recv_buf.at[slot], send_sem=send_sems.at[slot], recv_sem=recv_sems.at[slot],
            device_id=sibling, device_id_type=pl.DeviceIdType.MESH)
        copy.start()
        copy.wait_recv()
        o_ref[...] = fn_out(*vals, recv_buf[slot], c).astype(o_ref.dtype)
        copy.wait_send()

        @pl.when(i < n_steps - 2)
        def _():
            pl.semaphore_signal(credit, inc=1, device_id=sibling, device_id_type=pl.DeviceIdType.MESH)

    scratch = [pltpu.VMEM((2,) + tile, wire), pltpu.VMEM((2,) + tile, wire), pltpu.SemaphoreType.DMA((2,)),
               pltpu.SemaphoreType.DMA((2,)), pltpu.SemaphoreType.REGULAR]
    if prefetch is None:
        return pl.pallas_call(body, out_shape=out_shape, grid=(n_steps,), in_specs=in_specs, out_specs=out_spec,
                              scratch_shapes=scratch, compiler_params=_params(("arbitrary",)), name=name)(*ins)
    grid_spec = pltpu.PrefetchScalarGridSpec(num_scalar_prefetch=1, grid=(n_steps,), in_specs=in_specs, out_specs=out_spec,
                                             scratch_shapes=scratch)
    return pl.pallas_call(body, out_shape=out_shape, grid_spec=grid_spec, compiler_params=_params(("arbitrary",)),
                          name=name)(prefetch, *ins)


def _tile_rows(rows, cols, f32_bytes=3 << 19):
    return _row_tile(rows, max(2 * SUBLANES, f32_bytes // (4 * cols)), 2 * SUBLANES)


def pair_sum(name, x):
    rows, cols = x.shape
    tr = _tile_rows(rows, cols)
    return _pair_exchange(name, [x], [pl.BlockSpec((tr, cols), lambda i: (i, 0))], rows // tr, (tr, cols),
                          lambda v, c: v, lambda v, got, c: v + got, jax.ShapeDtypeStruct((rows, cols), F32),
                          pl.BlockSpec((tr, cols), lambda i: (i, 0)))


def reduce_cores(name, g):
    _, m, cols = g.shape
    tr = _tile_rows(m, cols, 6 << 20)

    def fn_send(g0, g1, c):
        return jnp.where(c == 0, g1, g0)

    def fn_out(g0, g1, got, c):
        return jnp.where(c == 0, g0, g1) + got.astype(F32)

    return _pair_exchange(
        name, [g, g], [pl.BlockSpec((None, tr, cols), lambda i: (0, i, 0)), pl.BlockSpec((None, tr, cols), lambda i: (1, i, 0))],
        m // tr, (tr, cols), fn_send, fn_out, jax.ShapeDtypeStruct((m, cols), BF16), pl.BlockSpec((tr, cols), lambda i: (i, 0)),
        wire=BF16)


def sum_and_share(name, own, parts, my):
    n, r, cols = parts.shape
    tr = _tile_rows(r, cols, 3 << 20)

    def total(o, p):
        acc = o.astype(F32)
        for j in range(n):
            acc = acc + p[j].astype(F32)
        return acc

    def fn_send(o, p, c):
        return total(o, p)

    def fn_out(o, p, got, c):
        mine = total(o, p)
        return jnp.stack([jnp.where(c == 0, mine, got), jnp.where(c == 0, got, mine)])

    return _pair_exchange(
        name, [own, parts], [pl.BlockSpec((None, tr, cols), lambda i, my_ref: (my_ref[0], i, 0)), pl.BlockSpec((n, tr, cols), lambda i, my_ref: (0, i, 0))],
        r // tr, (tr, cols), fn_send, fn_out, jax.ShapeDtypeStruct((2, r, cols), F32),
        pl.BlockSpec((2, tr, cols), lambda i, my_ref: (0, i, 0)), prefetch=my)


def _block_diag(blocks):
    g, r, c = blocks.shape
    eye = jnp.eye(g, dtype=blocks.dtype)
    return (blocks[:, :, None, :] * eye[:, None, :, None]).reshape(g * r, g * c)


def _diag_blocks(mat, g):
    r, c = mat.shape[0] // g, mat.shape[1] // g
    eye = jnp.eye(g, dtype=mat.dtype)
    return (mat.reshape(g, r, g, c) * eye[:, None, :, None]).sum(axis=2)


def _halves(gfull, shards):
    rows, cols = gfull.shape
    return gfull.reshape(shards, 2, rows // shards // 2, cols).transpose(1, 0, 2, 3)


def _step(inp):
    x = inp['x'][0]
    target = inp['loss_target'][0]
    rows, d = x.shape
    depth = inp['w_in'].shape[0]
    mix_w = d // 4
    n_state = S5_GROUPS * S5_STATE
    ffn_half = inp['ffn_w_up'].shape[2]
    tm = min(512, rows)
    tc = min(256, rows)
    xy = ("x", "y")

    my_chip = (2 * lax.axis_index("x") + lax.axis_index("y")).astype(jnp.int32).reshape(1)
    small_keys = [(nme, None) for nme in SMALL_SHARDED]
    group_keys = []
    for l in range(depth):
        group_keys += [[('w_in', l)] + (small_keys if l == 0 else []),
                       [('w_out', l), ('s5_w_glu', l), ('cv_w_pw', l)], [('ffn_w_up', l)], [('ffn_w_down', l)]]

    def zone_of(key, after=None):
        nme, l = key
        src = jnp.swapaxes(inp[nme], 1, 2) if nme == 'ffn_w_up' else inp[nme]
        return place_tile(f"place_{nme}{l}", src, l, my_chip, after=after)

    zones = {('w_in', 0): zone_of(('w_in', 0))}
    zones.update(zip(small_keys, place_own("place_small", [inp[nme] for nme in SMALL_SHARDED], xy)))
    first_group, first_token = exchange_start("gather_start_first", [[zones[key] for key in group_keys[0]]], xy, False)
    for grp in group_keys[1:]:
        zones.update({key: zone_of(key, first_token) for key in grp})
    rest_groups, gather_token = exchange_start("gather_start", [[zones[key] for key in grp] for grp in group_keys[1:]], xy, False)
    gather_groups = first_group + rest_groups

    def gathered(gi, after):
        return dict(zip(group_keys[gi], exchange_wait(f"gather_wait{gi}", gather_groups[gi], after, xy, False)[1]))

    def full_small(g):
        return g.transpose(1, 2, 0, 3).reshape(g.shape[1], g.shape[2], 4 * g.shape[3])

    gsum = jnp.repeat(jnp.eye(128, dtype=F32)[:S5_GROUPS], S5_STATE, axis=0)

    saved = []
    grads = {nme: [None] * depth for nme in WEIGHTS}
    xcur = x
    for l in range(depth):
        vec = lambda a: a[l].reshape(1, -1)
        gain = vec(inp['norm_mix_g']) + (gather_token[0, 0] if l == 0 else 0.0)
        h = rms_fwd(f"rms_mix{l}", xcur, gain, tm)
        got = gathered(4 * l, h)
        w_in = got[('w_in', l)]
        if l == 0:
            cv_w_dw, lru_w_conv, ffn_w_dw = (full_small(got[(nme, None)]) for nme in ('cv_w_dw', 'lru_w_conv', 'ffn_w_dw'))
        ncol = w_in.shape[2]

        lam_re, lam_im = vec(inp['s5_lam_re']), vec(inp['s5_lam_im'])
        log_step = jnp.broadcast_to(inp['s5_log_step'][l][:, None], (S5_GROUPS, S5_STATE)).reshape(1, n_state)
        b_re = _block_diag(inp['s5_b_re'][l].transpose(0, 2, 1))
        b_im = _block_diag(inp['s5_b_im'][l].transpose(0, 2, 1))
        c_cat = jnp.stack([_block_diag(inp['s5_c_re'][l].transpose(0, 2, 1)),
                           -_block_diag(inp['s5_c_im'][l].transpose(0, 2, 1))]).astype(BF16)
        a_bar, b_bar = s5_param_fwd(f"s5_param_fwd{l}", lam_re, lam_im, log_step, b_re, b_im)
        w_r = _block_diag(inp['lru_w_r'][l]).astype(BF16)
        w_i = _block_diag(inp['lru_w_i'][l]).astype(BF16)
        pool_bd = _block_diag(inp['pool_w'][l]).astype(BF16)
        gate_pars = [w_r, w_i, vec(inp['lru_b_r']), vec(inp['lru_b_i']), vec(inp['lru_lam'])]

        proj = _mm(f"proj{l}", h, w_in, jax.ShapeDtypeStruct((rows, 4 * ncol), F32), (4, rows // tm),
                   pl.BlockSpec((tm, d), lambda j, i: (i, 0)), pl.BlockSpec((None, d, ncol), lambda j, i: (j, 0, 0)),
                   pl.BlockSpec((tm, ncol), lambda j, i: (i, j)), NN)
        proj3 = proj.reshape(1, rows, 4 * ncol)
        ts = min(2048, rows)
        cw, sw = mix_w // 4, n_state // 4
        bu = _mm(f"s5_bu{l}", proj, b_bar, jax.ShapeDtypeStruct((2, rows, n_state), F32), (rows // ts, 2, 4),
                 pl.BlockSpec((ts, cw), lambda i, c, s: (i, s)), pl.BlockSpec((None, cw, sw), lambda i, c, s: (c, s, s)),
                 pl.BlockSpec((None, ts, sw), lambda i, c, s: (c, i, s)), NN)
        z = s5_scan_fwd(f"s5_scan{l}", bu, a_bar)
        y_ssm = _mm(f"s5_read{l}", z, c_cat, jax.ShapeDtypeStruct((rows, mix_w), F32), (rows // ts, 4, 2),
                    pl.BlockSpec((None, ts, sw), lambda i, s, c: (c, i, s)), pl.BlockSpec((None, sw, cw), lambda i, s, c: (c, s, s)),
                    pl.BlockSpec((ts, cw), lambda i, s, c: (i, s)), NN, k_axis=2)
        (h0,) = _rowwise(f"cv_glu{l}", _glu, [(proj, 1, mix_w), (proj, 2, mix_w)], [], 1, [(mix_w, F32)], tm)
        h1 = dwconv_fwd(f"cv_conv{l}", h0.reshape(1, rows, mix_w), 0, mix_w, cv_w_dw[l][None], vec(inp['cv_b_dw'])[None],
                        CV_TAPS, tc)[0]
        xc = dwconv_fwd(f"lru_conv{l}", proj3, 3, mix_w, lru_w_conv[l][None], vec(inp['lru_b_conv'])[None], LRU_TAPS, tc)[0]
        a_t, b_t = _rowwise(f"lru_gate{l}", _lru_gate, [(xc, 0, mix_w)], gate_pars, 2, [(mix_w, F32), (mix_w, F32)], tm)
        hseq = lru_scan_fwd(f"lru_scan{l}", a_t, b_t)
        dgp = pool_fwd(f"pool{l}", proj, 5, tc)
        got = gathered(4 * l + 1, proj)
        w_out = got[('w_out', l)].reshape(d, d)
        w_glu, w_pw = got[('s5_w_glu', l)].reshape(mix_w, mix_w), got[('cv_w_pw', l)].reshape(mix_w, mix_w)
        post_pars = [vec(inp['s5_d']), w_glu, vec(inp['s5_b_glu']), vec(inp['cv_ln_g']), vec(inp['cv_ln_b']), w_pw,
                     vec(inp['cv_b_pw']), pool_bd, vec(inp['pool_scale'])]
        post_rows = [(y_ssm, 0, mix_w), (proj, 0, mix_w), (h1, 0, mix_w), (hseq, 0, mix_w), (proj, 4, mix_w), (dgp, 0, mix_w)]
        (mixed,) = _rowwise(f"mix_post{l}", _mix_post, post_rows, post_pars, 1, [(d, BF16)], tm)
        td = min(1024, rows)
        x1 = _mm(f"out_proj{l}", mixed, w_out, jax.ShapeDtypeStruct((rows, d), F32), (2, rows // td),
                 pl.BlockSpec((td, d), lambda j, i: (i, 0)), pl.BlockSpec((d, d // 2), lambda j, i: (0, j)),
                 pl.BlockSpec((td, d // 2), lambda j, i: (i, j)), NN,
                 add=xcur, add_spec=pl.BlockSpec((td, d // 2), lambda j, i: (i, j)))

        h2 = rms_fwd(f"rms_ffn{l}", x1, vec(inp['norm_ffn_g']), tm)
        tu = min(512, rows)
        w_up = gathered(4 * l + 2, x1)[('ffn_w_up', l)]
        up = _mm(f"ffn_up{l}", h2, w_up, jax.ShapeDtypeStruct((4, rows, ffn_half), F32), (4, rows // tu),
                 pl.BlockSpec((tu, d), lambda k, i: (i, 0)), pl.BlockSpec((None, ffn_half, d), lambda k, i: (k, 0, 0)),
                 pl.BlockSpec((None, tu, ffn_half), lambda k, i: (k, i, 0)), NT)
        w_dw = ffn_w_dw[l].reshape(FFN_TAPS, 2, ffn_half).transpose(1, 0, 2)
        b_dw = inp['ffn_b_dw'][l].reshape(2, 1, ffn_half)
        act = ffn_gate_fwd(f"ffn_gate{l}", up, w_dw, b_dw, tc)
        w_down = gathered(4 * l + 3, up)[('ffn_w_down', l)].reshape(2, ffn_half, d)
        x2 = _mm(f"ffn_down{l}", act, w_down, jax.ShapeDtypeStruct((rows, d), F32), (rows // td, 4),
                 pl.BlockSpec((2, td, ffn_half), lambda i, j: (0, i, 0)), pl.BlockSpec((2, ffn_half, d // 4), lambda i, j: (0, 0, j)),
                 pl.BlockSpec((td, d // 4), lambda i, j: (i, j)), NN, inner=("lead", 2),
                 add=x1, add_spec=pl.BlockSpec((td, d // 4), lambda i, j: (i, j)))
        saved.append(dict(x=xcur, h=h, proj=proj, z=z, y_ssm=y_ssm, h0=h0, h1=h1, xc=xc, a_t=a_t, hseq=hseq, dgp=dgp,
                          mixed=mixed, x1=x1, h2=h2, up=up, act=act, w_in=w_in, w_out=w_out, w_up=w_up, w_down=w_down,
                          a_bar=a_bar, b_bar=b_bar, c_cat=c_cat, post_pars=post_pars, gate_pars=gate_pars, w_dw=w_dw, b_dw=b_dw,
                          s5=(lam_re, lam_im, log_step, b_re, b_im), cv_w=cv_w_dw[l][None], lru_w=lru_w_conv[l][None]))
        xcur = x2

    loss_row, dx, dg_final = final_loss("final_loss", xcur, inp['norm_final_g'].reshape(1, d), target, tm)
    grads['norm_final_g'] = dg_final.reshape(d)

    big_g = {nme: [None] * depth for nme in BIG}
    reduce_groups = []

    def start_reduce(tag, keys):
        pieces = []
        for nme, lyr in keys:
            g = big_g[nme][lyr]
            if nme == 'ffn_w_down':
                g = g.reshape(2, 4, ffn_half // 2, d // 2)
            pieces.append(reduce_cores(f"reduce_cores_{nme}{lyr}", g.reshape(2, -1, g.shape[-1])).reshape(g.shape[1:]))
        landing = [lax.empty((3,) + p.shape[1:], p.dtype) for p in pieces]
        groups, token = exchange_start(f"reduce_start_{tag}", [list(zip(pieces, landing))], xy, True)
        reduce_groups.append((tag, keys, groups[0]))
        return token

    for l in reversed(range(depth)):
        s = saved[l]
        ncol = s['w_in'].shape[2]
        tu = min(512, rows)
        dact = _mm(f"d_act{l}", dx, s['w_down'], jax.ShapeDtypeStruct((2, rows, ffn_half), F32), (2, rows // tu),
                   pl.BlockSpec((tu, d), lambda k, i: (i, 0)), pl.BlockSpec((None, ffn_half, d), lambda k, i: (k, 0, 0)),
                   pl.BlockSpec((None, tu, ffn_half), lambda k, i: (k, i, 0)), NT)
        tn = d // 4
        tk = min(1024, rows)
        tkb = min(2048, rows)
        big_g['ffn_w_down'][l] = _mm(
            f"dw_down{l}", s['act'], dx, jax.ShapeDtypeStruct((2, 2, ffn_half, d // 2), F32), (2, 4, rows // tkb),
            pl.BlockSpec((None, tkb, ffn_half), lambda hh, n, k: (hh, k, 0)), pl.BlockSpec((tkb, tn), lambda hh, n, k: (k, n)),
            pl.BlockSpec((None, None, ffn_half, tn), lambda hh, n, k: (n // 2, hh, 0, n % 2)), TN, k_axis=2)
        dup, dw_dw, db_dw = ffn_gate_bwd(f"ffn_gate_bwd{l}", s['up'], dact, s['w_dw'], s['b_dw'], tc)
        grads['ffn_w_dw'][l] = dw_dw.transpose(1, 0, 2).reshape(FFN_TAPS, 2 * ffn_half)
        grads['ffn_b_dw'][l] = db_dw.reshape(2 * ffn_half)
        dup = dup.reshape(4, rows, ffn_half)
        tm2 = min(1024, rows)
        dh2 = _mm(f"d_h2{l}", dup, s['w_up'], jax.ShapeDtypeStruct((rows, d), F32), (rows // tm2, 2, 4),
                  pl.BlockSpec((None, tm2, ffn_half), lambda i, j, k: (k, i, 0)), pl.BlockSpec((None, ffn_half, d // 2), lambda i, j, k: (k, 0, j)),
                  pl.BlockSpec((tm2, d // 2), lambda i, j, k: (i, j)), NN, k_axis=2)
        tmm = d // 4
        big_g['ffn_w_up'][l] = _mm(
            f"dw_up{l}", dup, s['h2'], jax.ShapeDtypeStruct((2, 4, ffn_half, d // 2), F32), (4, 4, rows // tkb),
            pl.BlockSpec((None, tkb, ffn_half), lambda k4, n, k: (k4, k, 0)), pl.BlockSpec((tkb, tn), lambda k4, n, k: (k, n)),
            pl.BlockSpec((None, None, ffn_half, tn), lambda k4, n, k: (n // 2, k4, 0, n % 2)), TN, k_axis=2)
        token = start_reduce(f"ffn{l}", [('ffn_w_down', l), ('ffn_w_up', l)])
        dx1, dg = rms_bwd(f"rms_ffn_bwd{l}", s['x1'], inp['norm_ffn_g'][l].reshape(1, d) + token[0, 0], dh2, dx, tm)
        grads['norm_ffn_g'][l] = dg.reshape(d)
        dmixed = _mm(f"d_mixed{l}", dx1, s['w_out'], jax.ShapeDtypeStruct((rows, d), F32), (rows // tm2, 4),
                     pl.BlockSpec((tm2, d), lambda i, j: (i, 0)), pl.BlockSpec((d // 4, d), lambda i, j: (j, 0)),
                     pl.BlockSpec((tm2, d // 4), lambda i, j: (i, j)), NT)
        tq = mix_w // 2
        big_g['w_out'][l] = _mm(
            f"dw_out{l}", s['mixed'], dx1, jax.ShapeDtypeStruct((2, 4, tq, d), F32), (4, rows // tk),
            pl.BlockSpec((tk, 2 * tq), lambda t, k: (k, t)), pl.BlockSpec((tk, d), lambda t, k: (k, 0)),
            pl.BlockSpec((2, None, tq, d), lambda t, k: (0, t, 0, 0)), TN, k_axis=1)
        post_rows = [(s['y_ssm'], 0, mix_w), (s['proj'], 0, mix_w), (s['h1'], 0, mix_w), (s['hseq'], 0, mix_w),
                     (s['proj'], 4, mix_w), (s['dgp'], 0, mix_w), (dmixed, 0, d)]
        res = _rowwise(f"mix_post_bwd{l}", _mix_post, post_rows, s['post_pars'], 1, [(mix_w, F32)] * 6, tm, with_grads=True)
        dy_ssm, du_dir, dh1, dhseq, dlru_g, ddgp = res[:6]
        dd, dwglu, dbglu, dlng, dlnb, dwpw, dbpw, dpoolbd, dscale = res[6:]
        grads['s5_d'][l], grads['s5_b_glu'][l] = dd.reshape(mix_w), dbglu.reshape(mix_w)
        grads['cv_ln_g'][l], grads['cv_ln_b'][l], grads['cv_b_pw'][l] = dlng.reshape(mix_w), dlnb.reshape(mix_w), dbpw.reshape(mix_w)
        grads['pool_w'][l] = _diag_blocks(dpoolbd, len(POOL_WINDOWS))
        grads['pool_scale'][l] = dscale.reshape(mix_w)
        big_g['s5_w_glu'][l] = _halves(dwglu, 4)
        big_g['cv_w_pw'][l] = _halves(dwpw, 4)
        ts = min(2048, rows)
        cw, sw = mix_w // 4, n_state // 4
        slab = jnp.arange(mix_w)[:, None] // cw == jnp.arange(n_state)[None, :] // sw
        dz = _mm(f"s5_dz{l}", dy_ssm, s['c_cat'], jax.ShapeDtypeStruct((2, rows, n_state), F32), (rows // ts, 2, 4),
                 pl.BlockSpec((ts, cw), lambda i, c, q: (i, q)), pl.BlockSpec((None, sw, cw), lambda i, c, q: (c, q, q)),
                 pl.BlockSpec((None, ts, sw), lambda i, c, q: (c, i, q)), NT)
        dccat = _mm(f"s5_dc{l}", s['z'], dy_ssm, jax.ShapeDtypeStruct((2, n_state, mix_w), F32), (2, 4, rows // tk),
                    pl.BlockSpec((None, tk, sw), lambda c, q, k: (c, k, q)), pl.BlockSpec((tk, cw), lambda c, q, k: (k, q)),
                    pl.BlockSpec((None, sw, cw), lambda c, q, k: (c, q, q)), TN, k_axis=2)
        dccat = jnp.where(slab.T, dccat, 0.0)
        grads['s5_c_re'][l] = _diag_blocks(dccat[0], S5_GROUPS).transpose(0, 2, 1)
        grads['s5_c_im'][l] = -_diag_blocks(dccat[1], S5_GROUPS).transpose(0, 2, 1)
        lam, da_bar = s5_scan_bwd(f"s5_scan_bwd{l}", dz, s['z'], s['a_bar'])
        du = _mm(f"s5_du{l}", lam, s['b_bar'], jax.ShapeDtypeStruct((rows, mix_w), F32), (rows // ts, 4, 2),
                 pl.BlockSpec((None, ts, sw), lambda i, q, c: (c, i, q)), pl.BlockSpec((None, cw, sw), lambda i, q, c: (c, q, q)),
                 pl.BlockSpec((ts, cw), lambda i, q, c: (i, q)), NT, k_axis=2,
                 add=du_dir, add_spec=pl.BlockSpec((ts, cw), lambda i, q, c: (i, q)))
        dbbar = _mm(f"s5_db{l}", s['proj'], lam, jax.ShapeDtypeStruct((2, mix_w, n_state), F32), (2, 4, rows // tk),
                    pl.BlockSpec((tk, cw), lambda c, q, k: (k, q)), pl.BlockSpec((None, tk, sw), lambda c, q, k: (c, k, q)),
                    pl.BlockSpec((None, cw, sw), lambda c, q, k: (c, q, q)), TN, k_axis=2)
        dbbar = jnp.where(slab, dbbar, 0.0)
        dlr, dli, dls, dbre, dbim = s5_param_bwd(f"s5_param_bwd{l}", *s['s5'], da_bar, dbbar, gsum)
        grads['s5_lam_re'][l] = dlr.reshape(S5_GROUPS, S5_STATE)
        grads['s5_lam_im'][l] = dli.reshape(S5_GROUPS, S5_STATE)
        grads['s5_log_step'][l] = dls[0, :S5_GROUPS]
        grads['s5_b_re'][l] = _diag_blocks(dbre, S5_GROUPS).transpose(0, 2, 1)
        grads['s5_b_im'][l] = _diag_blocks(dbim, S5_GROUPS).transpose(0, 2, 1)
        dh0, dw_cv, db_cv = dwconv_bwd(f"cv_conv_bwd{l}", dh1.reshape(1, rows, mix_w), s['h0'].reshape(1, rows, mix_w), 0, mix_w,
                                       s['cv_w'], CV_TAPS, tc)
        grads['cv_w_dw'][l], grads['cv_b_dw'][l] = dw_cv[0], db_cv.reshape(mix_w)
        dv, dgg = _rowwise(f"cv_glu_bwd{l}", _glu, [(s['proj'], 1, mix_w), (s['proj'], 2, mix_w), (dh0[0], 0, mix_w)], [], 1,
                           [(mix_w, F32)] * 2, tm, with_grads=True)
        da_t, db_t = lru_scan_bwd(f"lru_scan_bwd{l}", dhseq, s['a_t'], s['hseq'])
        res = _rowwise(f"lru_gate_bwd{l}", _lru_gate, [(s['xc'], 0, mix_w), (da_t, 0, mix_w), (db_t, 0, mix_w)], s['gate_pars'], 2,
                       [(mix_w, F32)], tm, with_grads=True)
        dxc, dwr, dwi, dbr, dbi, dlam = res
        grads['lru_w_r'][l], grads['lru_w_i'][l] = _diag_blocks(dwr, LRU_HEADS), _diag_blocks(dwi, LRU_HEADS)
        grads['lru_b_r'][l], grads['lru_b_i'][l], grads['lru_lam'][l] = dbr.reshape(mix_w), dbi.reshape(mix_w), dlam.reshape(mix_w)
        dlx, dw_lc, db_lc = dwconv_bwd(f"lru_conv_bwd{l}", dxc.reshape(1, rows, mix_w), s['proj'].reshape(1, rows, 4 * ncol), 3, mix_w,
                                       s['lru_w'], LRU_TAPS, tc)
        grads['lru_w_conv'][l], grads['lru_b_conv'][l] = dw_lc[0], db_lc.reshape(mix_w)
        dpx = pool_bwd(f"pool_bwd{l}", ddgp, tc)
        dproj = jnp.concatenate([du, dv, dgg, dlx[0], dlru_g, dpx], axis=-1)
        dh = _mm(f"d_h{l}", dproj, s['w_in'], jax.ShapeDtypeStruct((rows, d), F32), (rows // tm, 4),
                 pl.BlockSpec((tm, 4 * ncol), lambda i, j: (i, 0)), pl.BlockSpec((4, d // 4, ncol), lambda i, j: (0, j, 0)),
                 pl.BlockSpec((tm, d // 4), lambda i, j: (i, j)), NT, inner=("cols", 4))
        tk2 = min(2048, rows)
        big_g['w_in'][l] = _mm(
            f"dw_in{l}", s['h'], dproj, jax.ShapeDtypeStruct((2, 4, d // 2, ncol), F32), (4, 2, rows // tk2),
            pl.BlockSpec((tk2, d // 2), lambda k4, m, k: (k, m)), pl.BlockSpec((tk2, ncol), lambda k4, m, k: (k, k4)),
            pl.BlockSpec((None, None, d // 2, ncol), lambda k4, m, k: (m, k4, 0, 0)), TN, k_axis=2)
        token = start_reduce(f"mix{l}", [('w_out', l), ('s5_w_glu', l), ('cv_w_pw', l), ('w_in', l)])
        dx, dg = rms_bwd(f"rms_mix_bwd{l}", s['x'], inp['norm_mix_g'][l].reshape(1, d) + token[0, 0], dh, dx1, tm)
        grads['norm_mix_g'][l] = dg.reshape(d)

    small = [nme for nme in WEIGHTS if nme not in BIG]
    full_g = {nme: (grads[nme] if nme == 'norm_final_g' else jnp.stack(grads[nme])) for nme in small}
    flat = jnp.concatenate([full_g[nme].reshape(-1) for nme in small])
    packed = jnp.pad(flat, (0, (-flat.shape[0]) % (128 * 64))).reshape(-1, 128)
    chip_sum = pair_sum("small_pair_sum", packed)
    small_zone = place_tile("place_small_grads", chip_sum[None], 0, my_chip, dtype=F32)
    (small_group,), small_token = exchange_start("small_start", [[small_zone]], xy, False)

    t_full = {}
    for tag, keys, group in reduce_groups:
        pieces, parts = exchange_wait(f"reduce_wait_{tag}", group, small_token, xy, True)
        for key, own, got in zip(keys, pieces, parts):
            t_full[key] = sum_and_share(f"share_cores_{key[0]}{key[1]}", own, got, my_chip)

    outs, done_big = {}, []
    tiles = {'w_in': 256, 'w_out': 128, 'ffn_w_up': 128, 'ffn_w_down': 256, 's5_w_glu': 64, 'cv_w_pw': 64}
    for nme in BIG:
        g0, g1 = t_full[(nme, 0)], t_full[(nme, 1)]
        if nme == 'ffn_w_up':
            res = adamw_sharded(f"adamw_{nme}", *(jnp.swapaxes(inp[p + nme], 1, 2) for p in ('', 'm_', 'v_')), g0, g1, True, tiles[nme])
            outs[nme] = tuple(jnp.swapaxes(r, 1, 2) for r in res)
        else:
            res = adamw_sharded(f"adamw_{nme}", inp[nme], inp['m_' + nme], inp['v_' + nme], g0, g1, nme == 'ffn_w_down', tiles[nme])
            outs[nme] = res
        done_big.append(res[1][:1, :1, :1].reshape(1))

    after_big = sum(done_big)
    (g4,) = exchange_wait("small_wait", small_group, after_big, xy, False)[1]
    gsum_small = sum_lead("sum_small", g4, _row_tile(g4.shape[1], 1024)).reshape(-1)
    red, off = {}, 0
    for nme in small:
        g = gsum_small[off:off + full_g[nme].size].reshape(full_g[nme].shape)
        off += full_g[nme].size
        if nme in SMALL_SHARDED:
            width = inp[nme].shape[2]
            g = lax.dynamic_slice_in_dim(g, my_chip[0] * width, width, axis=2)
        red[nme] = g

    def pack(tree):
        f = jnp.concatenate([tree[nme].reshape(-1) for nme in small])
        return jnp.pad(f, (0, (-f.shape[0]) % (128 * 64))).reshape(-1, 128)

    pd, pm, pv = adamw_flat("adamw_small", pack({n_: inp[n_] for n_ in small}), pack(red), pack({n_: inp['m_' + n_] for n_ in small}),
                            pack({n_: inp['v_' + n_] for n_ in small}))
    off = 0
    for nme in small:
        size, shape = inp[nme].size, inp[nme].shape
        outs[nme] = (red[nme],) + tuple(p.reshape(-1)[off:off + size].reshape(shape) for p in (pd, pm, pv))
        off += size

    loss = lax.psum(loss_row[0, 0], ("x", "y", "c"))
    result = [loss, dx[None]]
    for part in range(4):
        result += [outs[nme][part] for nme in WEIGHTS]
    return tuple(result)


def kernel(x, norm_mix_g, w_in, s5_lam_re, s5_lam_im, s5_log_step, s5_b_re, s5_b_im, s5_c_re, s5_c_im, s5_d, s5_w_glu, s5_b_glu, cv_w_dw, cv_b_dw, cv_ln_g, cv_ln_b, cv_w_pw, cv_b_pw, lru_w_conv, lru_b_conv, lru_w_r, lru_b_r, lru_w_i, lru_b_i, lru_lam, pool_w, pool_scale, w_out, norm_ffn_g, ffn_w_up, ffn_w_dw, ffn_b_dw, ffn_w_down, norm_final_g, loss_target, m_norm_mix_g, m_w_in, m_s5_lam_re, m_s5_lam_im, m_s5_log_step, m_s5_b_re, m_s5_b_im, m_s5_c_re, m_s5_c_im, m_s5_d, m_s5_w_glu, m_s5_b_glu, m_cv_w_dw, m_cv_b_dw, m_cv_ln_g, m_cv_ln_b, m_cv_w_pw, m_cv_b_pw, m_lru_w_conv, m_lru_b_conv, m_lru_w_r, m_lru_b_r, m_lru_w_i, m_lru_b_i, m_lru_lam, m_pool_w, m_pool_scale, m_w_out, m_norm_ffn_g, m_ffn_w_up, m_ffn_w_dw, m_ffn_b_dw, m_ffn_w_down, m_norm_final_g, v_norm_mix_g, v_w_in, v_s5_lam_re, v_s5_lam_im, v_s5_log_step, v_s5_b_re, v_s5_b_im, v_s5_c_re, v_s5_c_im, v_s5_d, v_s5_w_glu, v_s5_b_glu, v_cv_w_dw, v_cv_b_dw, v_cv_ln_g, v_cv_ln_b, v_cv_w_pw, v_cv_b_pw, v_lru_w_conv, v_lru_b_conv, v_lru_w_r, v_lru_b_r, v_lru_w_i, v_lru_b_i, v_lru_lam, v_pool_w, v_pool_scale, v_w_out, v_norm_ffn_g, v_ffn_w_up, v_ffn_w_dw, v_ffn_b_dw, v_ffn_w_down, v_norm_final_g):
    inp = dict(locals())
    return _step(inp)
```

```python
import functools

import jax
import jax.numpy as jnp
from jax import lax
from jax.experimental import pallas as pl
from jax.experimental.pallas import tpu as pltpu

F32 = jnp.float32
BF16 = jnp.bfloat16

VMEM_LIMIT_BYTES = 56 * 1024 * 1024
SUBLANES = 8

EPS = 1e-6
S5_GROUPS, S5_STATE, S5_GROUP_CH = 32, 64, 16
LRU_HEADS, LRU_C = 8, 8.0
POOL_WINDOWS = (2, 4, 8, 16)
CV_TAPS, LRU_TAPS, FFN_TAPS = 31, 4, 3
SCAN_CHUNK = 64
GELU_K0, GELU_K1 = 0.7978845608028654, 0.044715

ADAM_LR, ADAM_B1, ADAM_B2, ADAM_EPS, ADAM_WD, ADAM_STEP = 0.001, 0.9, 0.999, 1e-08, 0.01, 10

NN = ((1,), (0,))
NT = ((1,), (1,))
TN = ((0,), (0,))

WEIGHTS = ['norm_mix_g', 'w_in', 's5_lam_re', 's5_lam_im', 's5_log_step', 's5_b_re', 's5_b_im', 's5_c_re', 's5_c_im',
           's5_d', 's5_w_glu', 's5_b_glu', 'cv_w_dw', 'cv_b_dw', 'cv_ln_g', 'cv_ln_b', 'cv_w_pw', 'cv_b_pw',
           'lru_w_conv', 'lru_b_conv', 'lru_w_r', 'lru_b_r', 'lru_w_i', 'lru_b_i', 'lru_lam', 'pool_w', 'pool_scale',
           'w_out', 'norm_ffn_g', 'ffn_w_up', 'ffn_w_dw', 'ffn_b_dw', 'ffn_w_down', 'norm_final_g']
BIG = ('w_in', 'w_out', 'ffn_w_up', 'ffn_w_down', 's5_w_glu', 'cv_w_pw')
SMALL_SHARDED = {'cv_w_dw': 2, 'lru_w_conv': 2, 'ffn_w_dw': 2}


def _params(sem=None):
    if sem is None:
        return pltpu.CompilerParams(vmem_limit_bytes=VMEM_LIMIT_BYTES)
    return pltpu.CompilerParams(dimension_semantics=sem, vmem_limit_bytes=VMEM_LIMIT_BYTES)


def _row_tile(rows, cap, mult=SUBLANES):
    best = mult
    for t in range(mult, min(rows, cap) + 1, mult):
        if rows % t == 0:
            best = t
    return best


def _bdot(a, b, dims=NN):
    return lax.dot_general(a.astype(BF16), b.astype(BF16), (dims, ((), ())), preferred_element_type=F32)


@jax.custom_vjp
def bdot(a, b):
    return _bdot(a, b)


def _bdot_fwd(a, b):
    return _bdot(a, b), (a, b)


def _bdot_bwd(res, g):
    a, b = res
    return _bdot(g, b, NT).astype(a.dtype), _bdot(a, g, TN).astype(b.dtype)


bdot.defvjp(_bdot_fwd, _bdot_bwd)


def _mm(name, a, b, out_sds, grid, a_spec, b_spec, o_spec, dims, k_axis=None, add=None, add_spec=None, inner=None):
    nk = grid[k_axis] if k_axis is not None else 1
    has_add = add is not None
    acc_shape = tuple(d for d in o_spec.block_shape if d is not None)
    acc_in_out = out_sds.dtype == F32

    def product(a_ref, b_ref):
        if inner is None:
            return _bdot(a_ref[...], b_ref[...], dims)
        kind, n = inner
        width = a_ref.shape[-1] // n
        acc = None
        for j in range(n):
            a_j = a_ref[j] if kind == "lead" else a_ref[:, j * width:(j + 1) * width]
            p = _bdot(a_j, b_ref[j], dims)
            acc = p if acc is None else acc + p
        return acc

    def body(*refs):
        a_ref, b_ref = refs[0], refs[1]
        add_ref = refs[2] if has_add else None
        o_ref = refs[3] if has_add else refs[2]
        prod = product(a_ref, b_ref).reshape(acc_shape)
        if k_axis is None:
            if has_add:
                prod = prod + add_ref[...]
            o_ref[...] = prod.astype(o_ref.dtype)
        else:
            acc_ref = o_ref if acc_in_out else refs[-1]
            k = pl.program_id(k_axis)

            @pl.when(k == 0)
            def _():
                acc_ref[...] = prod

            @pl.when(k > 0)
            def _():
                acc_ref[...] += prod

            if has_add or not acc_in_out:
                @pl.when(k == nk - 1)
                def _():
                    r = acc_ref[...]
                    if has_add:
                        r = r + add_ref[...]
                    o_ref[...] = r.astype(o_ref.dtype)

    sem = tuple("arbitrary" if d == k_axis else "parallel" for d in range(len(grid)))
    in_specs = [a_spec, b_spec] + ([add_spec] if has_add else [])
    args = (a, b) + ((add,) if has_add else ())
    scratch = [pltpu.VMEM(acc_shape, F32)] if (k_axis is not None and not acc_in_out) else []
    return pl.pallas_call(body, out_shape=out_sds, grid=grid, in_specs=in_specs, out_specs=o_spec,
                          scratch_shapes=scratch, compiler_params=_params(sem), name=name)(*args)


def _rms(x, g):
    return x * lax.rsqrt(jnp.mean(x * x, axis=-1, keepdims=True) + EPS) * g


def rms_fwd(name, x, g, tm):
    rows, d = x.shape

    def body(x_ref, g_ref, o_ref):
        o_ref[...] = _rms(x_ref[...], g_ref[...]).astype(BF16)

    return pl.pallas_call(
        body, out_shape=jax.ShapeDtypeStruct((rows, d), BF16), grid=(rows // tm,),
        in_specs=[pl.BlockSpec((tm, d), lambda i: (i, 0)), pl.BlockSpec((1, d), lambda i: (0, 0))],
        out_specs=pl.BlockSpec((tm, d), lambda i: (i, 0)), compiler_params=_params(("parallel",)), name=name)(x, g)


def rms_bwd(name, x, g, dh, dres, tm):
    rows, d = x.shape

    def body(x_ref, g_ref, dh_ref, dres_ref, dx_ref, dxb_ref, dg_ref):
        xv, dy = x_ref[...], dh_ref[...]
        r = lax.rsqrt(jnp.mean(xv * xv, axis=-1, keepdims=True) + EPS)
        dyg = dy * g_ref[...]
        s = jnp.mean(dyg * xv, axis=-1, keepdims=True)
        dx = r * dyg - xv * (r * r * r * s) + dres_ref[...]
        dx_ref[...] = dx
        dxb_ref[...] = dx.astype(BF16)

        @pl.when(pl.program_id(0) == 0)
        def _():
            dg_ref[...] = jnp.zeros_like(dg_ref)

        dg_ref[...] += jnp.sum(dy * xv * r, axis=0, keepdims=True)

    row = pl.BlockSpec((tm, d), lambda i: (i, 0))
    vec = pl.BlockSpec((1, d), lambda i: (0, 0))
    return pl.pallas_call(
        body, out_shape=(jax.ShapeDtypeStruct((rows, d), F32), jax.ShapeDtypeStruct((rows, d), BF16), jax.ShapeDtypeStruct((1, d), F32)),
        grid=(rows // tm,), in_specs=[row, vec, row, row], out_specs=(row, row, vec),
        compiler_params=_params(("arbitrary",)), name=name)(x, g, dh, dres)


def final_loss(name, x, g, target, tm):
    rows, d = x.shape

    def body(x_ref, g_ref, t_ref, l_ref, dx_ref, dxb_ref, dg_ref):
        def f(xv, gv):
            e = _rms(xv, gv) - t_ref[...]
            return 0.5 * jnp.sum(jnp.mean(e * e, axis=-1))

        loss, (dx, dg) = jax.value_and_grad(f, argnums=(0, 1))(x_ref[...], g_ref[...])
        dx_ref[...] = dx
        dxb_ref[...] = dx.astype(BF16)

        @pl.when(pl.program_id(0) == 0)
        def _():
            l_ref[...] = jnp.zeros_like(l_ref)
            dg_ref[...] = jnp.zeros_like(dg_ref)

        l_ref[...] += jnp.full(l_ref.shape, loss, F32)
        dg_ref[...] += dg

    row = pl.BlockSpec((tm, d), lambda i: (i, 0))
    vec = pl.BlockSpec((1, d), lambda i: (0, 0))
    lspec = pl.BlockSpec((1, 128), lambda i: (0, 0))
    return pl.pallas_call(
        body, out_shape=(jax.ShapeDtypeStruct((1, 128), F32), jax.ShapeDtypeStruct((rows, d), F32), jax.ShapeDtypeStruct((rows, d), BF16),
                         jax.ShapeDtypeStruct((1, d), F32)),
        grid=(rows // tm,), in_specs=[row, vec, row], out_specs=(lspec, row, row, vec),
        compiler_params=_params(("arbitrary",)), name=name)(x, g, target)


def _rowwise(name, fn, row_ins, par_ins, n_row_out, row_out_dtypes, tm, with_grads=False):
    rows = row_ins[0][0].shape[0]
    n_prim = len(row_ins) - (n_row_out if with_grads else 0)
    n_par = len(par_ins)

    def body(*refs):
        ins = [r[...] for r in refs[:len(row_ins) + n_par]]
        outs = refs[len(row_ins) + n_par:]
        prim, cts, pars = ins[:n_prim], ins[n_prim:len(row_ins)], ins[len(row_ins):]
        if not with_grads:
            res = fn(*prim, *pars)
            for o_ref, r in zip(outs, res):
                o_ref[...] = r.astype(o_ref.dtype)
            return
        _, vjp = jax.vjp(fn, *prim, *[p.astype(F32) for p in pars])
        grads = vjp(tuple(cts))
        for o_ref, gr in zip(outs[:n_prim], grads[:n_prim]):
            o_ref[...] = gr.astype(o_ref.dtype)

        @pl.when(pl.program_id(0) == 0)
        def _():
            for o_ref in outs[n_prim:]:
                o_ref[...] = jnp.zeros_like(o_ref)

        for o_ref, gr in zip(outs[n_prim:], grads[n_prim:]):
            o_ref[...] += gr.astype(F32)

    in_specs = [pl.BlockSpec((tm, w), (lambda i, c=c: (i, c))) for (_, c, w) in row_ins]
    in_specs += [pl.BlockSpec(p.shape, (lambda i, n=p.ndim: (0,) * n)) for p in par_ins]
    args = [a for (a, _, _) in row_ins] + list(par_ins)
    if not with_grads:
        out_shape = tuple(jax.ShapeDtypeStruct((rows, w), dt) for (w, dt) in row_out_dtypes)
        out_specs = tuple(pl.BlockSpec((tm, w), lambda i: (i, 0)) for (w, _) in row_out_dtypes)
        sem = ("parallel",)
    else:
        out_shape = tuple(jax.ShapeDtypeStruct((rows, w), dt) for (w, dt) in row_out_dtypes)
        out_shape += tuple(jax.ShapeDtypeStruct(p.shape, F32) for p in par_ins)
        out_specs = tuple(pl.BlockSpec((tm, w), lambda i: (i, 0)) for (w, _) in row_out_dtypes)
        out_specs += tuple(pl.BlockSpec(p.shape, (lambda i, n=p.ndim: (0,) * n)) for p in par_ins)
        sem = ("arbitrary",)
    return pl.pallas_call(body, out_shape=out_shape, grid=(rows // tm,), in_specs=in_specs, out_specs=out_specs,
                          compiler_params=_params(sem), name=name)(*args)


def _glu(v, g):
    return (v * jax.nn.sigmoid(g),)


def _neg_expm1(z):
    return -jnp.tanh(0.5 * z) * (jnp.exp(z) + 1.0)


def _lru_gate(xc, w_r, w_i, b_r, b_i, lam):
    r = jax.nn.sigmoid(bdot(xc, w_r) + b_r)
    i = jax.nn.sigmoid(bdot(xc, w_i) + b_i)
    log_a = -LRU_C * r * jax.nn.softplus(-lam)
    a = jnp.exp(log_a)
    mult = jnp.sqrt(_neg_expm1(2.0 * log_a))
    return a, mult * (i * xc)


def _layernorm(x, g, b):
    mu = jnp.mean(x, axis=-1, keepdims=True)
    var = jnp.mean(jnp.square(x - mu), axis=-1, keepdims=True)
    return (x - mu) * lax.rsqrt(var + EPS) * g + b


def _mix_post(y_ssm, u, h1, hseq, lru_g, dgp, s5_d, w_glu, b_glu, ln_g, ln_b, w_pw, b_pw, pool_bd, pool_scale):
    y = y_ssm + s5_d * u
    gl = jax.nn.gelu(y, approximate=True)
    out_s5 = gl * jax.nn.sigmoid(bdot(gl, w_glu) + b_glu)
    out_cv = bdot(jax.nn.silu(_layernorm(h1, ln_g, ln_b)), w_pw) + b_pw
    out_lru = hseq * jax.nn.gelu(lru_g, approximate=True)
    out_pool = bdot(dgp, pool_bd) * pool_scale
    return (jnp.concatenate([out_s5, out_cv, out_lru, out_pool], axis=-1),)


def _gelu_terms(x):
    sq = x * x
    t = jnp.tanh(x * (GELU_K0 + (GELU_K0 * GELU_K1) * sq))
    return sq, t, 0.5 + 0.5 * t


def _halo_rows(taps):
    return -(-(taps - 1) // SUBLANES) * SUBLANES


def _row_windows(ext, offsets, n, shifted_ref=None):
    if shifted_ref is None:
        return {off: ext[off:off + n] for off in offsets}
    room = ext.shape[0] - SUBLANES
    slots, out = {}, {}
    for off in offsets:
        r = off % SUBLANES
        if r == 0:
            out[off] = ext[off:off + n]
            continue
        if r not in slots:
            slots[r] = len(slots)
            shifted_ref[slots[r]] = ext[r:r + room]
        out[off] = shifted_ref[slots[r], off - r:off - r + n, :]
    return out


def _shift_scratch(taps, tm, c):
    return [pltpu.VMEM((SUBLANES - 1, _halo_rows(taps) + tm - SUBLANES, c), F32)] if taps > SUBLANES else []


def dwconv_fwd(name, x, cblk, c, w, b, taps, tm, out_dtype=F32):
    nb = w.shape[0]
    rows = x.shape[1]
    halo = _halo_rows(taps)
    per = tm // halo

    def body(x_ref, h_ref, w_ref, b_ref, o_ref, *shifted):
        i = pl.program_id(1)
        prev = jnp.where(i > 0, h_ref[...], 0.0)
        ext = jnp.concatenate([prev, x_ref[...]], axis=0)
        win = _row_windows(ext, [halo - (taps - 1) + k for k in range(taps)], tm, *shifted)
        acc = jnp.broadcast_to(b_ref[...], (tm, c))
        for k in range(taps):
            acc = acc + w_ref[k:k + 1, :] * win[halo - (taps - 1) + k]
        o_ref[...] = acc.astype(o_ref.dtype)

    return pl.pallas_call(
        body, out_shape=jax.ShapeDtypeStruct((nb, rows, c), out_dtype), grid=(nb, rows // tm),
        in_specs=[pl.BlockSpec((None, tm, c), lambda n, i: (n, i, cblk)),
                  pl.BlockSpec((None, halo, c), lambda n, i: (n, jnp.maximum(i * per - 1, 0), cblk)),
                  pl.BlockSpec((None, taps, c), lambda n, i: (n, 0, 0)),
                  pl.BlockSpec((None, 1, c), lambda n, i: (n, 0, 0))],
        out_specs=pl.BlockSpec((None, tm, c), lambda n, i: (n, i, 0)), scratch_shapes=_shift_scratch(taps, tm, c),
        compiler_params=_params(("parallel", "parallel")), name=name)(x, x, w, b)


def dwconv_bwd(name, dy, x, cblk, c, w, taps, tm, dx_dtype=F32):
    nb = w.shape[0]
    rows = x.shape[1]
    halo = _halo_rows(taps)
    per = tm // halo
    n_tiles = rows // tm
    last_halo = rows // halo - 1

    def body(dy_ref, dn_ref, x_ref, xp_ref, w_ref, dx_ref, dw_ref, db_ref, *shifted):
        i = pl.program_id(1)
        dyv = dy_ref[...]
        nxt = jnp.where(i < n_tiles - 1, dn_ref[...], 0.0)
        dext = jnp.concatenate([dyv, nxt], axis=0)
        prev = jnp.where(i > 0, xp_ref[...], 0.0)
        xext = jnp.concatenate([prev, x_ref[...]], axis=0)
        acc = jnp.zeros((tm, c), F32)

        @pl.when(i == 0)
        def _():
            dw_ref[...] = jnp.zeros_like(dw_ref)
            db_ref[...] = jnp.zeros_like(db_ref)

        dwin = _row_windows(dext, list(range(taps)), tm, *shifted[:1])
        xwin = _row_windows(xext, [halo - (taps - 1) + k for k in range(taps)], tm, *shifted[1:])
        for k in range(taps):
            acc = acc + w_ref[k:k + 1, :] * dwin[taps - 1 - k]
            dw_ref[k:k + 1, :] += jnp.sum(dyv * xwin[halo - (taps - 1) + k], axis=0, keepdims=True)
        dx_ref[...] = acc.astype(dx_ref.dtype)
        db_ref[...] += jnp.sum(dyv, axis=0, keepdims=True)

    return pl.pallas_call(
        body, out_shape=(jax.ShapeDtypeStruct((nb, rows, c), dx_dtype), jax.ShapeDtypeStruct((nb, taps, c), F32),
                         jax.ShapeDtypeStruct((nb, 1, c), F32)),
        grid=(nb, n_tiles),
        in_specs=[pl.BlockSpec((None, tm, c), lambda n, i: (n, i, 0)),
                  pl.BlockSpec((None, halo, c), lambda n, i: (n, jnp.minimum((i + 1) * per, last_halo), 0)),
                  pl.BlockSpec((None, tm, c), lambda n, i: (n, i, cblk)),
                  pl.BlockSpec((None, halo, c), lambda n, i: (n, jnp.maximum(i * per - 1, 0), cblk)),
                  pl.BlockSpec((None, taps, c), lambda n, i: (n, 0, 0))],
        out_specs=(pl.BlockSpec((None, tm, c), lambda n, i: (n, i, 0)), pl.BlockSpec((None, taps, c), lambda n, i: (n, 0, 0)),
                   pl.BlockSpec((None, 1, c), lambda n, i: (n, 0, 0))),
        scratch_shapes=2 * _shift_scratch(taps, tm, c),
        compiler_params=_params(("parallel", "arbitrary")), name=name)(dy, dy, x, x, w)


def ffn_gate_fwd(name, up, w, b, tm):
    _, rows, c = up.shape
    halo = _halo_rows(FFN_TAPS)
    per = tm // halo

    def body(g_ref, gp_ref, v_ref, w_ref, b_ref, o_ref):
        i = pl.program_id(1)
        ext = jnp.concatenate([jnp.where(i > 0, gp_ref[...], 0.0), g_ref[...]], axis=0)
        gc = jnp.broadcast_to(b_ref[...], (tm, c))
        for k in range(FFN_TAPS):
            off = halo - (FFN_TAPS - 1) + k
            gc = gc + w_ref[k:k + 1, :] * ext[off:off + tm]
        o_ref[...] = (gc * _gelu_terms(gc)[2] * v_ref[...]).astype(BF16)

    return pl.pallas_call(
        body, out_shape=jax.ShapeDtypeStruct((2, rows, c), BF16), grid=(2, rows // tm),
        in_specs=[pl.BlockSpec((None, tm, c), lambda h, i: (h, i, 0)),
                  pl.BlockSpec((None, halo, c), lambda h, i: (h, jnp.maximum(i * per - 1, 0), 0)),
                  pl.BlockSpec((None, tm, c), lambda h, i: (h + 2, i, 0)),
                  pl.BlockSpec((None, FFN_TAPS, c), lambda h, i: (h, 0, 0)), pl.BlockSpec((None, 1, c), lambda h, i: (h, 0, 0))],
        out_specs=pl.BlockSpec((None, tm, c), lambda h, i: (h, i, 0)),
        compiler_params=_params(("parallel", "parallel")), name=name)(up, up, up, w, b)


def ffn_gate_bwd(name, up, dact, w, b, tm):
    _, rows, c = up.shape
    halo = _halo_rows(FFN_TAPS)
    per = tm // halo
    n_tiles = rows // tm
    last_halo = rows // halo - 1
    n_ext = tm + halo

    def body(g_ref, gp_ref, gn_ref, v_ref, vn_ref, d_ref, dn_ref, w_ref, b_ref, dup_ref, dw_ref, db_ref):
        i = pl.program_id(1)
        gext = jnp.concatenate([jnp.where(i > 0, gp_ref[...], 0.0), g_ref[...], gn_ref[...]], axis=0)
        shifted = [gext[halo - (FFN_TAPS - 1) + k:halo - (FFN_TAPS - 1) + k + n_ext] for k in range(FFN_TAPS)]
        gc = jnp.broadcast_to(b_ref[...], (n_ext, c))
        for k in range(FFN_TAPS):
            gc = gc + w_ref[k:k + 1, :] * shifted[k]
        vext = jnp.concatenate([v_ref[...], vn_ref[...]], axis=0)
        dext = jnp.concatenate([d_ref[...], dn_ref[...]], axis=0)
        sq, t, half = _gelu_terms(gc)
        dval = dext * (gc * half)
        dgc = (dext * vext) * (half + (0.5 * gc) * (1.0 - t * t) * (GELU_K0 + (3.0 * GELU_K0 * GELU_K1) * sq))
        r = lax.broadcasted_iota(jnp.int32, (n_ext, c), 0)
        dgc = jnp.where((r < tm) | (i < n_tiles - 1), dgc, 0.0)
        dgate = jnp.zeros((tm, c), F32)
        for k in range(FFN_TAPS):
            dgate = dgate + w_ref[k:k + 1, :] * dgc[FFN_TAPS - 1 - k:FFN_TAPS - 1 - k + tm]
        dup_ref[0] = dgate.astype(BF16)
        dup_ref[1] = dval[:tm].astype(BF16)

        @pl.when(i == 0)
        def _():
            dw_ref[...] = jnp.zeros_like(dw_ref)
            db_ref[...] = jnp.zeros_like(db_ref)

        dgc_t = dgc[:tm]
        for k in range(FFN_TAPS):
            dw_ref[k:k + 1, :] += jnp.sum(dgc_t * shifted[k][:tm], axis=0, keepdims=True)
        db_ref[...] += jnp.sum(dgc_t, axis=0, keepdims=True)

    def tile(shift):
        return pl.BlockSpec((None, tm, c), lambda h, i: (h + shift, i, 0))

    def after(shift):
        return pl.BlockSpec((None, halo, c), lambda h, i: (h + shift, jnp.minimum((i + 1) * per, last_halo), 0))

    return pl.pallas_call(
        body, out_shape=(jax.ShapeDtypeStruct((2, 2, rows, c), BF16), jax.ShapeDtypeStruct((2, FFN_TAPS, c), F32),
                         jax.ShapeDtypeStruct((2, 1, c), F32)),
        grid=(2, n_tiles),
        in_specs=[tile(0), pl.BlockSpec((None, halo, c), lambda h, i: (h, jnp.maximum(i * per - 1, 0), 0)), after(0),
                  tile(2), after(2), tile(0), after(0),
                  pl.BlockSpec((None, FFN_TAPS, c), lambda h, i: (h, 0, 0)), pl.BlockSpec((None, 1, c), lambda h, i: (h, 0, 0))],
        out_specs=(pl.BlockSpec((2, None, tm, c), lambda h, i: (0, h, i, 0)), pl.BlockSpec((None, FFN_TAPS, c), lambda h, i: (h, 0, 0)),
                   pl.BlockSpec((None, 1, c), lambda h, i: (h, 0, 0))),
        compiler_params=_params(("parallel", "arbitrary")), name=name)(up, up, up, up, up, dact, dact, w, b)


POOL_HALO = 16


def pool_fwd(name, proj, cblk, tm):
    rows = proj.shape[0]
    c = 128 * len(POOL_WINDOWS)
    per = tm // POOL_HALO

    def body(x_ref, h_ref, o_ref):
        i = pl.program_id(0)
        xv = x_ref[...]
        ext = jnp.concatenate([jnp.where(i > 0, h_ref[...], 0.0), xv], axis=0)
        t1 = (lax.broadcasted_iota(jnp.int32, (tm, 128), 0) + i * tm + 1).astype(F32)
        outs = []
        for gi, win in enumerate(POOL_WINDOWS):
            seg = ext[:, gi * 128:(gi + 1) * 128]
            s = seg[POOL_HALO:POOL_HALO + tm]
            for j in range(1, win):
                s = s + seg[POOL_HALO - j:POOL_HALO - j + tm]
            outs.append(s / jnp.minimum(t1, float(win)) - xv[:, gi * 128:(gi + 1) * 128])
        o_ref[...] = jnp.concatenate(outs, axis=-1)

    return pl.pallas_call(
        body, out_shape=jax.ShapeDtypeStruct((rows, c), F32), grid=(rows // tm,),
        in_specs=[pl.BlockSpec((tm, c), lambda i: (i, cblk)),
                  pl.BlockSpec((POOL_HALO, c), lambda i: (jnp.maximum(i * per - 1, 0), cblk))],
        out_specs=pl.BlockSpec((tm, c), lambda i: (i, 0)), compiler_params=_params(("parallel",)), name=name)(proj, proj)


def pool_bwd(name, dd, tm):
    rows, c = dd.shape
    per = tm // POOL_HALO
    n_tiles = rows // tm
    last_halo = rows // POOL_HALO - 1

    def body(d_ref, n_ref, o_ref):
        i = pl.program_id(0)
        dv = d_ref[...]
        nxt = jnp.where(i < n_tiles - 1, n_ref[...], 0.0)
        t1 = (lax.broadcasted_iota(jnp.int32, (tm, 128), 0) + i * tm + 1).astype(F32)
        t1n = (lax.broadcasted_iota(jnp.int32, (POOL_HALO, 128), 0) + (i + 1) * tm + 1).astype(F32)
        outs = []
        for gi, win in enumerate(POOL_WINDOWS):
            sl = slice(gi * 128, (gi + 1) * 128)
            q = jnp.concatenate([dv[:, sl] / jnp.minimum(t1, float(win)), nxt[:, sl] / jnp.minimum(t1n, float(win))], axis=0)
            s = q[0:tm]
            for j in range(1, win):
                s = s + q[j:j + tm]
            outs.append(s - dv[:, sl])
        o_ref[...] = jnp.concatenate(outs, axis=-1)

    return pl.pallas_call(
        body, out_shape=jax.ShapeDtypeStruct((rows, c), F32), grid=(n_tiles,),
        in_specs=[pl.BlockSpec((tm, c), lambda i: (i, 0)),
                  pl.BlockSpec((POOL_HALO, c), lambda i: (jnp.minimum((i + 1) * per, last_halo), 0))],
        out_specs=pl.BlockSpec((tm, c), lambda i: (i, 0)), compiler_params=_params(("parallel",)), name=name)(dd, dd)


BLOCK_STEPS = 3


def _shift_down(v, s, fill):
    r = lax.broadcasted_iota(jnp.int32, v.shape, 0)
    return jnp.where(r >= s, pltpu.roll(v, s, 0), fill)


def _shift_up(v, s, fill):
    n = v.shape[0]
    r = lax.broadcasted_iota(jnp.int32, v.shape, 0)
    return jnp.where(r < n - s, pltpu.roll(v, n - s, 0), fill)


def _shift_in_blocks(v, s, fill, reverse):
    n = v.shape[0]
    q = lax.broadcasted_iota(jnp.int32, v.shape, 0) & (SUBLANES - 1)
    if reverse:
        return jnp.where(q < SUBLANES - s, pltpu.roll(v, n - s, 0), fill)
    return jnp.where(q >= s, pltpu.roll(v, s, 0), fill)


def _cscan_blocks(vr, vi, powers, reverse):
    for k, (qr, qi) in enumerate(powers):
        s = 1 << k
        sr, si = _shift_in_blocks(vr, s, 0.0, reverse), _shift_in_blocks(vi, s, 0.0, reverse)
        if reverse:
            vr, vi = vr + qr * sr + qi * si, vi + qr * si - qi * sr
        else:
            vr, vi = vr + qr * sr - qi * si, vi + qr * si + qi * sr
    return vr, vi


def _cscan_table(pr, pi, powers, reverse):
    r = lax.broadcasted_iota(jnp.int32, (SUBLANES, 128), 0)
    at = (SUBLANES - 1) if reverse else 0
    return _cscan_blocks(jnp.where(r == at, pr, 0.0), jnp.where(r == at, -pi if reverse else pi, 0.0), powers, reverse)


def _cscan_chunk(vr, vi, powers, table, carry, reverse):
    vr, vi = _cscan_blocks(vr, vi, powers, reverse)
    tr, ti = table
    cr, ci = carry
    nb = vr.shape[0] // SUBLANES
    outr, outi = [None] * nb, [None] * nb
    edge = 0 if reverse else SUBLANES - 1
    for j in (reversed(range(nb)) if reverse else range(nb)):
        rows = slice(j * SUBLANES, (j + 1) * SUBLANES)
        zr = vr[rows] + tr * cr - ti * ci
        zi = vi[rows] + tr * ci + ti * cr
        outr[j], outi[j] = zr, zi
        cr, ci = zr[edge:edge + 1], zi[edge:edge + 1]
    return jnp.concatenate(outr, axis=0), jnp.concatenate(outi, axis=0), (cr, ci)


def _powers(pr, pi, n):
    out = [(pr, pi)]
    for _ in range(n - 1):
        pr, pi = pr * pr - pi * pi, 2.0 * pr * pi
        out.append((pr, pi))
    return out


def s5_scan_fwd(name, bu, a):
    _, rows, n = bu.shape
    t = min(SCAN_CHUNK, rows)

    def body(bu_ref, a_ref, z_ref):
        pr, pi = a_ref[0], a_ref[1]
        powers = _powers(pr, pi, BLOCK_STEPS)
        table = _cscan_table(pr, pi, powers, False)

        def chunk(ci, carry):
            base = pl.multiple_of(ci * t, t)
            zr, zi, carry = _cscan_chunk(bu_ref[0, pl.ds(base, t), :], bu_ref[1, pl.ds(base, t), :], powers, table, carry, False)
            z_ref[0, pl.ds(base, t), :] = zr
            z_ref[1, pl.ds(base, t), :] = zi
            return carry

        zero = jnp.zeros((1, 128), F32)
        lax.fori_loop(0, rows // t, chunk, (zero, zero))

    return pl.pallas_call(
        body, out_shape=jax.ShapeDtypeStruct((2, rows, n), F32), grid=(n // 128,),
        in_specs=[pl.BlockSpec((2, rows, 128), lambda j: (0, 0, j)), pl.BlockSpec((2, 1, 128), lambda j: (0, 0, j))],
        out_specs=pl.BlockSpec((2, rows, 128), lambda j: (0, 0, j)), compiler_params=_params(("parallel",)), name=name)(bu, a)


def s5_scan_bwd(name, dz, z, a):
    _, rows, n = dz.shape
    t = min(SCAN_CHUNK, rows)
    n_chunks = rows // t

    def body(dz_ref, z_ref, a_ref, lam_ref, da_ref):
        pr, pi = a_ref[0], a_ref[1]
        powers = _powers(pr, pi, BLOCK_STEPS)
        table = _cscan_table(pr, pi, powers, True)

        def chunk(k, carry):
            ci = n_chunks - 1 - k
            base = pl.multiple_of(ci * t, t)
            cr, cim, dar, dai = carry
            lr, li, (cr, cim) = _cscan_chunk(dz_ref[0, pl.ds(base, t), :], dz_ref[1, pl.ds(base, t), :], powers, table, (cr, cim), True)
            lam_ref[0, pl.ds(base, t), :] = lr
            lam_ref[1, pl.ds(base, t), :] = li
            pbase = pl.multiple_of(jnp.maximum(base - SUBLANES, 0), SUBLANES)
            keep = (ci > 0).astype(F32)
            pzr = z_ref[0, pl.ds(pbase, SUBLANES), :][SUBLANES - 1:SUBLANES, :] * keep
            pzi = z_ref[1, pl.ds(pbase, SUBLANES), :][SUBLANES - 1:SUBLANES, :] * keep
            zpr = _shift_down(z_ref[0, pl.ds(base, t), :], 1, pzr)
            zpi = _shift_down(z_ref[1, pl.ds(base, t), :], 1, pzi)
            dar = dar + jnp.sum(lr * zpr + li * zpi, axis=0, keepdims=True)
            dai = dai + jnp.sum(li * zpr - lr * zpi, axis=0, keepdims=True)
            return cr, cim, dar, dai

        zero = jnp.zeros((1, 128), F32)
        _, _, dar, dai = lax.fori_loop(0, n_chunks, chunk, (zero, zero, zero, zero))
        da_ref[0] = dar
        da_ref[1] = dai

    seq = pl.BlockSpec((2, rows, 128), lambda j: (0, 0, j))
    vec = pl.BlockSpec((2, 1, 128), lambda j: (0, 0, j))
    return pl.pallas_call(
        body, out_shape=(jax.ShapeDtypeStruct((2, rows, n), F32), jax.ShapeDtypeStruct((2, 1, n), F32)), grid=(n // 128,),
        in_specs=[seq, seq, vec], out_specs=(seq, vec), compiler_params=_params(("parallel",)), name=name)(dz, z, a)


def _rscan_chunk(a, b, carry, reverse):
    n = a.shape[0]
    shift = _shift_up if reverse else _shift_down
    for k in range(n.bit_length() - 1):
        s = 1 << k
        b = b + a * shift(b, s, 0.0)
        a = a * shift(a, s, 1.0)
    h = b + a * carry
    edge = 0 if reverse else n - 1
    return h, h[edge:edge + 1]


def lru_scan_fwd(name, a, b):
    rows, n = a.shape
    t = min(SCAN_CHUNK, rows)

    def body(a_ref, b_ref, h_ref):
        def chunk(ci, carry):
            base = pl.multiple_of(ci * t, t)
            h, carry = _rscan_chunk(a_ref[pl.ds(base, t), :], b_ref[pl.ds(base, t), :], carry, False)
            h_ref[pl.ds(base, t), :] = h
            return carry

        lax.fori_loop(0, rows // t, chunk, jnp.zeros((1, 128), F32))

    seq = pl.BlockSpec((rows, 128), lambda j: (0, j))
    return pl.pallas_call(body, out_shape=jax.ShapeDtypeStruct((rows, n), F32), grid=(n // 128,), in_specs=[seq, seq],
                          out_specs=seq, compiler_params=_params(("parallel",)), name=name)(a, b)


def lru_scan_bwd(name, dh, a, h):
    rows, n = a.shape
    t = min(SCAN_CHUNK, rows)
    n_chunks = rows // t

    def body(dh_ref, a_ref, h_ref, da_ref, db_ref):
        def chunk(k, carry):
            ci = n_chunks - 1 - k
            base = pl.multiple_of(ci * t, t)
            nbase = pl.multiple_of(jnp.minimum(base + t, rows - SUBLANES), SUBLANES)
            a_next = a_ref[pl.ds(nbase, SUBLANES), :][0:1, :]
            an = _shift_up(a_ref[pl.ds(base, t), :], 1, a_next)
            mu, carry = _rscan_chunk(an, dh_ref[pl.ds(base, t), :], carry, True)
            pbase = pl.multiple_of(jnp.maximum(base - SUBLANES, 0), SUBLANES)
            hp_row = h_ref[pl.ds(pbase, SUBLANES), :][SUBLANES - 1:SUBLANES, :] * (ci > 0).astype(F32)
            hp = _shift_down(h_ref[pl.ds(base, t), :], 1, hp_row)
            da_ref[pl.ds(base, t), :] = mu * hp
            db_ref[pl.ds(base, t), :] = mu
            return carry

        lax.fori_loop(0, n_chunks, chunk, jnp.zeros((1, 128), F32))

    seq = pl.BlockSpec((rows, 128), lambda j: (0, j))
    return pl.pallas_call(
        body, out_shape=(jax.ShapeDtypeStruct((rows, n), F32), jax.ShapeDtypeStruct((rows, n), F32)), grid=(n // 128,),
        in_specs=[seq, seq, seq], out_specs=(seq, seq), compiler_params=_params(("parallel",)), name=name)(dh, a, h)


def _s5_param(lr, li, ls, bre, bim):
    st = jnp.exp(ls)
    er = jnp.exp(lr * st)
    th = li * st
    ar, ai = er * jnp.cos(th), er * jnp.sin(th)
    nr, ni = ar - 1.0, ai
    den = lr * lr + li * li
    cr, ci = (nr * lr + ni * li) / den, (ni * lr - nr * li) / den
    return ar, ai, cr * bre - ci * bim, cr * bim + ci * bre


def s5_param_fwd(name, lr, li, ls, bre, bim):
    gh, n = bre.shape

    def body(lr_ref, li_ref, ls_ref, bre_ref, bim_ref, a_ref, bb_ref):
        ar, ai, br, bi = _s5_param(lr_ref[...], li_ref[...], ls_ref[...], bre_ref[...], bim_ref[...])
        a_ref[0] = ar
        a_ref[1] = ai
        bb_ref[0] = br.astype(BF16)
        bb_ref[1] = bi.astype(BF16)

    return pl.pallas_call(body, out_shape=(jax.ShapeDtypeStruct((2, 1, n), F32), jax.ShapeDtypeStruct((2, gh, n), BF16)),
                          compiler_params=_params(), name=name)(lr, li, ls, bre, bim)


def s5_param_bwd(name, lr, li, ls, bre, bim, da, dbb, gsum):
    gh, n = bre.shape

    def body(lr_ref, li_ref, ls_ref, bre_ref, bim_ref, da_ref, dbb_ref, gs_ref, dlr_ref, dli_ref, dls_ref, dbre_ref, dbim_ref):
        _, vjp = jax.vjp(_s5_param, lr_ref[...], li_ref[...], ls_ref[...], bre_ref[...], bim_ref[...])
        dlr, dli, dls, dbre, dbim = vjp((da_ref[0], da_ref[1], dbb_ref[0], dbb_ref[1]))
        dlr_ref[...] = dlr
        dli_ref[...] = dli
        dls_ref[...] = jnp.dot(jnp.broadcast_to(dls, (SUBLANES, n)), gs_ref[...], preferred_element_type=F32,
                               precision=lax.Precision.HIGHEST)
        dbre_ref[...] = dbre
        dbim_ref[...] = dbim

    vec = jax.ShapeDtypeStruct((1, n), F32)
    mat = jax.ShapeDtypeStruct((gh, n), F32)
    return pl.pallas_call(body, out_shape=(vec, vec, jax.ShapeDtypeStruct((SUBLANES, 128), F32), mat, mat),
                          compiler_params=_params(), name=name)(lr, li, ls, bre, bim, da, dbb, gsum)


def sum_lead(name, x, tr):
    n, rows, cols = x.shape

    def body(x_ref, o_ref):
        acc = x_ref[0]
        for j in range(1, n):
            acc = acc + x_ref[j]
        o_ref[...] = acc

    return pl.pallas_call(
        body, out_shape=jax.ShapeDtypeStruct((rows, cols), x.dtype), grid=(rows // tr,),
        in_specs=[pl.BlockSpec((n, tr, cols), lambda i: (0, i, 0))], out_specs=pl.BlockSpec((tr, cols), lambda i: (i, 0)),
        compiler_params=_params(("parallel",)), name=name)(x)


def _adamw(w, g, m, v):
    m = ADAM_B1 * m + (1.0 - ADAM_B1) * g
    v = ADAM_B2 * v + (1.0 - ADAM_B2) * jnp.square(g)
    m_hat = m / (1.0 - ADAM_B1 ** ADAM_STEP)
    v_hat = v / (1.0 - ADAM_B2 ** ADAM_STEP)
    delta = -ADAM_LR * (m_hat / (jnp.sqrt(v_hat) + ADAM_EPS) + ADAM_WD * w)
    return delta, m, v


def adamw_sharded(name, w, m, v, g0, g1, split_cols, tile):
    _, r, c = w.shape
    if split_cols:
        nt = c // tile
        per = (c // 2) // tile
        block = (None, r, tile)
        wspec = pl.BlockSpec(block, lambda l, t: (l, 0, t))
        gidx = lambda t: (t // per, 0, t % per)
    else:
        nt = r // tile
        per = (r // 2) // tile
        block = (None, tile, c)
        wspec = pl.BlockSpec(block, lambda l, t: (l, t, 0))
        gidx = lambda t: (t // per, t % per, 0)

    def gspec(layer):
        return pl.BlockSpec(block, lambda l, t: gidx(jnp.where(l == layer, t, (nt - 1) * (1 - layer))))

    def body(w_ref, m_ref, v_ref, g0_ref, g1_ref, g_ref, d_ref, nm_ref, nv_ref):
        g = jnp.where(pl.program_id(0) == 0, g0_ref[...], g1_ref[...])
        d, nm, nv = _adamw(w_ref[...], g, m_ref[...], v_ref[...])
        g_ref[...] = g
        d_ref[...] = d
        nm_ref[...] = nm
        nv_ref[...] = nv

    sds = jax.ShapeDtypeStruct(w.shape, F32)
    return pl.pallas_call(body, out_shape=(sds,) * 4, grid=(2, nt), in_specs=[wspec, wspec, wspec, gspec(0), gspec(1)],
                          out_specs=(wspec,) * 4, compiler_params=_params(("arbitrary", "arbitrary")), name=name)(w, m, v, g0, g1)


def adamw_flat(name, w, g, m, v):
    rows, cols = w.shape
    tr = _row_tile(rows, 1024)

    def body(w_ref, g_ref, m_ref, v_ref, d_ref, nm_ref, nv_ref):
        d, nm, nv = _adamw(w_ref[...], g_ref[...], m_ref[...], v_ref[...])
        d_ref[...] = d
        nm_ref[...] = nm
        nv_ref[...] = nv

    blk = pl.BlockSpec((tr, cols), lambda i: (i, 0))
    sds = jax.ShapeDtypeStruct((rows, cols), F32)
    return pl.pallas_call(body, out_shape=(sds,) * 3, grid=(rows // tr,), in_specs=[blk] * 4, out_specs=(blk,) * 3,
                          compiler_params=_params(("parallel",)), name=name)(w, g, m, v)


def _flips(axes):
    out = []
    for fx in ((0, 1) if "x" in axes else (0,)):
        for fy in ((0, 1) if "y" in axes else (0,)):
            for fc in ((0, 1) if "c" in axes else (0,)):
                if fx or fy or fc:
                    out.append((fx, fy, fc))
    return out


def _slot(pos, axes):
    s = 0
    for name, p in zip(("x", "y", "c"), pos):
        if name in axes:
            s = 2 * s + p
    return s


_HBM = pl.BlockSpec(memory_space=pltpu.HBM)
_SEM = pl.BlockSpec(memory_space=pltpu.SEMAPHORE)
_EFFECT = pltpu.SideEffectType.DATAFLOW_SIDE_EFFECTING


def place_own(name, arrs, axes):
    n = len(_flips(axes)) + 1
    na = len(arrs)

    def body(*refs):
        ins, outs, sems = refs[:na], refs[na:2 * na], refs[2 * na]
        my = _slot((lax.axis_index("x"), lax.axis_index("y"), lax.axis_index("c")), axes)
        copies = [pltpu.make_async_copy(ins[a], outs[a].at[my], sems.at[a]) for a in range(na)]
        for cp in copies:
            cp.start()
        for cp in copies:
            cp.wait()

    out_shape = tuple(jax.ShapeDtypeStruct((n,) + a.shape, a.dtype) for a in arrs)
    anyspec = pl.BlockSpec(memory_space=pl.ANY)
    return pl.pallas_call(body, out_shape=out_shape, in_specs=[anyspec] * na, out_specs=(anyspec,) * na,
                          scratch_shapes=[pltpu.SemaphoreType.DMA((na,))], name=name)(*arrs)


def _peers(axes):
    me = (lax.axis_index("x"), lax.axis_index("y"), lax.axis_index("c"))
    return me, [tuple((1 - p) if f else p for p, f in zip(me, fl)) for fl in _flips(axes)]


def place_tile(name, arr, layer, my, slots=4, dtype=BF16, after=None):
    _, r, cols = arr.shape
    tr = _tile_rows(r, cols)

    def body(my_ref, x_ref, *rest):
        rest[-1][...] = x_ref[...].astype(dtype)

    in_specs = [pl.BlockSpec((None, tr, cols), lambda i, my: (layer, i, 0))]
    args = [arr]
    if after is not None:
        in_specs.append(pl.BlockSpec(after.shape, lambda i, my: (0, 0)))
        args.append(after)
    grid_spec = pltpu.PrefetchScalarGridSpec(num_scalar_prefetch=1, grid=(r // tr,), in_specs=in_specs,
                                             out_specs=pl.BlockSpec((None, tr, cols), lambda i, my: (my[0], i, 0)))
    return pl.pallas_call(body, out_shape=jax.ShapeDtypeStruct((slots, r, cols), dtype), grid_spec=grid_spec,
                          compiler_params=_params(("parallel",)), name=name)(my, *args)


def exchange_start(name, groups, axes, scatter):
    flat = [(p if scatter else (p,)) for grp in groups for p in grp]
    per = 2 if scatter else 1
    na, ng, npeer = len(flat), len(groups), len(_flips(axes))

    def body(*refs):
        ops = refs[:per * na]
        zones = ops[(per - 1) * na:]
        sems, token = refs[per * na:per * na + 2 * ng], refs[-1]
        me, peers = _peers(axes)
        my = _slot(me, axes)
        ai = 0
        for g, grp in enumerate(groups):
            for k in range(len(grp)):
                for j, peer in enumerate(peers):
                    src = ops[ai].at[_slot(peer, axes)] if scatter else zones[ai].at[my]
                    dst = zones[ai].at[j] if scatter else zones[ai].at[my]
                    pltpu.make_async_remote_copy(
                        src_ref=src, dst_ref=dst, send_sem=sems[2 * g].at[k * npeer + j],
                        recv_sem=sems[2 * g + 1].at[k * npeer + j], device_id=peer, device_id_type=pl.DeviceIdType.MESH).start()
                ai += 1
        token[...] = jnp.zeros_like(token)

    out_shape, out_specs = [], []
    for grp in groups:
        out_shape += [pltpu.SemaphoreType.DMA((npeer * len(grp),))] * 2
        out_specs += [_SEM, _SEM]
    for idx in range(per):
        out_shape += [pltpu.HBM(p[idx].shape, p[idx].dtype) for p in flat]
        out_specs += [_HBM] * na
    out_shape.append(jax.ShapeDtypeStruct((SUBLANES, 128), F32))
    out_specs.append(pl.BlockSpec(memory_space=pltpu.VMEM))
    args = [pltpu.with_memory_space_constraint(p[idx], pltpu.HBM) for idx in range(per) for p in flat]
    res = pl.pallas_call(body, out_shape=tuple(out_shape), in_specs=[_HBM] * (per * na), out_specs=tuple(out_specs),
                         input_output_aliases={i: 2 * ng + i for i in range(per * na)},
                         compiler_params=pltpu.CompilerParams(has_side_effects=_EFFECT), name=name)(*args)
    thru = res[2 * ng:2 * ng + per * na]
    out, ai = [], 0
    for g, grp in enumerate(groups):
        srcs = list(thru[ai:ai + len(grp)]) if scatter else []
        zones = list(thru[(per - 1) * na + ai:(per - 1) * na + ai + len(grp)])
        out.append(((res[2 * g], res[2 * g + 1]), srcs, zones))
        ai += len(grp)
    return out, res[-1]


def exchange_wait(name, group, after, axes, scatter):
    (send_sems, recv_sems), srcs, zones = group
    n, ns = len(zones), len(srcs)
    npeer = len(_flips(axes))

    def body(*refs):
        z_refs = refs[ns:ns + n]
        ssem, rsem = refs[ns + n], refs[ns + n + 1]
        _, peers = _peers(axes)
        for k in range(n):
            for j, peer in enumerate(peers):
                part = z_refs[k].at[j if scatter else _slot(peer, axes)]
                copy = pltpu.make_async_remote_copy(
                    src_ref=part, dst_ref=part, send_sem=ssem.at[k * npeer + j], recv_sem=rsem.at[k * npeer + j],
                    device_id=peer, device_id_type=pl.DeviceIdType.MESH)
                copy.wait_send()
                copy.wait_recv()

    ops = list(srcs) + list(zones)
    out_shape = tuple(pltpu.HBM(a.shape, a.dtype) for a in ops)
    res = pl.pallas_call(body, out_shape=out_shape, in_specs=[_HBM] * len(ops) + [_SEM, _SEM, pl.BlockSpec(memory_space=pl.ANY)],
                         out_specs=(_HBM,) * len(ops), input_output_aliases={i: i for i in range(len(ops))},
                         compiler_params=pltpu.CompilerParams(has_side_effects=_EFFECT), name=name)(*ops, send_sems, recv_sems, after)
    return list(res[:ns]), list(res[ns:])


def _pair_exchange(name, ins, in_specs, n_steps, tile, fn_send, fn_out, out_shape, out_spec, prefetch=None, wire=F32):
    n_in = len(ins)

    def body(*refs):
        if prefetch is not None:
            refs = refs[1:]
        in_refs, o_ref = refs[:n_in], refs[n_in]
        send_buf, recv_buf, send_sems, recv_sems, credit = refs[n_in + 1:]
        i = pl.program_id(0)
        slot = lax.rem(i, 2)
        c = lax.axis_index("c")
        sibling = (lax.axis_index("x"), lax.axis_index("y"), 1 - c)
        vals = [r[...] for r in in_refs]
        send_buf[slot] = fn_send(*vals, c).astype(wire)

        @pl.when(i >= 2)
        def _():
            pl.semaphore_wait(credit, 1)

        copy = pltpu.make_async_remote_copy(
            src_ref=send_buf.at[slot], dst_ref=recv_buf.at[slot], send_sem=send_sems.at[slot], recv_sem=recv_sems.at[slot],
            device_id=sibling, device_id_type=pl.DeviceIdType.MESH)
        copy.start()
        copy.wait_recv()
        o_ref[...] = fn_out(*vals, recv_buf[slot], c).astype(o_ref.dtype)
        copy.wait_send()

        @pl.when(i < n_steps - 2)
        def _():
            pl.semaphore_signal(credit, inc=1, device_id=sibling, device_id_type=pl.DeviceIdType.MESH)

    scratch = [pltpu.VMEM((2,) + tile, wire), pltpu.VMEM((2,) + tile, wire), pltpu.SemaphoreType.DMA((2,)),
               pltpu.SemaphoreType.DMA((2,)), pltpu.SemaphoreType.REGULAR]
    if prefetch is None:
        return pl.pallas_call(body, out_shape=out_shape, grid=(n_steps,), in_specs=in_specs, out_specs=out_spec,
                              scratch_shapes=scratch, compiler_params=_params(("arbitrary",)), name=name)(*ins)
    grid_spec = pltpu.PrefetchScalarGridSpec(num_scalar_prefetch=1, grid=(n_steps,), in_specs=in_specs, out_specs=out_spec,
                                             scratch_shapes=scratch)
    return pl.pallas_call(body, out_shape=out_shape, grid_spec=grid_spec, compiler_params=_params(("arbitrary",)),
                          name=name)(prefetch, *ins)


def _tile_rows(rows, cols, f32_bytes=3 << 19):
    return _row_tile(rows, max(2 * SUBLANES, f32_bytes // (4 * cols)), 2 * SUBLANES)


def pair_sum(name, x):
    rows, cols = x.shape
    tr = _tile_rows(rows, cols)
    return _pair_exchange(name, [x], [pl.BlockSpec((tr, cols), lambda i: (i, 0))], rows // tr, (tr, cols),
                          lambda v, c: v, lambda v, got, c: v + got, jax.ShapeDtypeStruct((rows, cols), F32),
                          pl.BlockSpec((tr, cols), lambda i: (i, 0)))


def reduce_cores(name, g):
    _, m, cols = g.shape
    tr = _tile_rows(m, cols, 6 << 20)

    def fn_send(g0, g1, c):
        return jnp.where(c == 0, g1, g0)

    def fn_out(g0, g1, got, c):
        return jnp.where(c == 0, g0, g1) + got.astype(F32)

    return _pair_exchange(
        name, [g, g], [pl.BlockSpec((None, tr, cols), lambda i: (0, i, 0)), pl.BlockSpec((None, tr, cols), lambda i: (1, i, 0))],
        m // tr, (tr, cols), fn_send, fn_out, jax.ShapeDtypeStruct((m, cols), BF16), pl.BlockSpec((tr, cols), lambda i: (i, 0)),
        wire=BF16)


def sum_and_share(name, own, parts, my):
    n, r, cols = parts.shape
    tr = _tile_rows(r, cols, 3 << 20)

    def total(o, p):
        acc = o.astype(F32)
        for j in range(n):
            acc = acc + p[j].astype(F32)
        return acc

    def fn_send(o, p, c):
        return total(o, p)

    def fn_out(o, p, got, c):
        mine = total(o, p)
        return jnp.stack([jnp.where(c == 0, mine, got), jnp.where(c == 0, got, mine)])

    return _pair_exchange(
        name, [own, parts], [pl.BlockSpec((None, tr, cols), lambda i, my_ref: (my_ref[0], i, 0)), pl.BlockSpec((n, tr, cols), lambda i, my_ref: (0, i, 0))],
        r // tr, (tr, cols), fn_send, fn_out, jax.ShapeDtypeStruct((2, r, cols), F32),
        pl.BlockSpec((2, tr, cols), lambda i, my_ref: (0, i, 0)), prefetch=my)


def _block_diag(blocks):
    g, r, c = blocks.shape
    eye = jnp.eye(g, dtype=blocks.dtype)
    return (blocks[:, :, None, :] * eye[:, None, :, None]).reshape(g * r, g * c)


def _diag_blocks(mat, g):
    r, c = mat.shape[0] // g, mat.shape[1] // g
    eye = jnp.eye(g, dtype=mat.dtype)
    return (mat.reshape(g, r, g, c) * eye[:, None, :, None]).sum(axis=2)


def _halves(gfull, shards):
    rows, cols = gfull.shape
    return gfull.reshape(shards, 2, rows // shards // 2, cols).transpose(1, 0, 2, 3)


def _step(inp):
    x = inp['x'][0]
    target = inp['loss_target'][0]
    rows, d = x.shape
    depth = inp['w_in'].shape[0]
    mix_w = d // 4
    n_state = S5_GROUPS * S5_STATE
    ffn_half = inp['ffn_w_up'].shape[2]
    tm = min(512, rows)
    tc = min(256, rows)
    xy = ("x", "y")

    my_chip = (2 * lax.axis_index("x") + lax.axis_index("y")).astype(jnp.int32).reshape(1)
    small_keys = [(nme, None) for nme in SMALL_SHARDED]
    group_keys = []
    for l in range(depth):
        group_keys += [[('w_in', l)] + (small_keys if l == 0 else []),
                       [('w_out', l), ('s5_w_glu', l), ('cv_w_pw', l)], [('ffn_w_up', l)], [('ffn_w_down', l)]]

    def zone_of(key, after=None):
        nme, l = key
        src = jnp.swapaxes(inp[nme], 1, 2) if nme == 'ffn_w_up' else inp[nme]
        return place_tile(f"place_{nme}{l}", src, l, my_chip, after=after)

    zones = {('w_in', 0): zone_of(('w_in', 0))}
    zones.update(zip(small_keys, place_own("place_small", [inp[nme] for nme in SMALL_SHARDED], xy)))
    first_group, first_token = exchange_start("gather_start_first", [[zones[key] for key in group_keys[0]]], xy, False)
    for grp in group_keys[1:]:
        zones.update({key: zone_of(key, first_token) for key in grp})
    rest_groups, gather_token = exchange_start("gather_start", [[zones[key] for key in grp] for grp in group_keys[1:]], xy, False)
    gather_groups = first_group + rest_groups

    def gathered(gi, after):
        return dict(zip(group_keys[gi], exchange_wait(f"gather_wait{gi}", gather_groups[gi], after, xy, False)[1]))

    def full_small(g):
        return g.transpose(1, 2, 0, 3).reshape(g.shape[1], g.shape[2], 4 * g.shape[3])

    gsum = jnp.repeat(jnp.eye(128, dtype=F32)[:S5_GROUPS], S5_STATE, axis=0)

    saved = []
    grads = {nme: [None] * depth for nme in WEIGHTS}
    xcur = x
    for l in range(depth):
        vec = lambda a: a[l].reshape(1, -1)
        gain = vec(inp['norm_mix_g']) + (gather_token[0, 0] if l == 0 else 0.0)
        h = rms_fwd(f"rms_mix{l}", xcur, gain, tm)
        got = gathered(4 * l, h)
        w_in = got[('w_in', l)]
        if l == 0:
            cv_w_dw, lru_w_conv, ffn_w_dw = (full_small(got[(nme, None)]) for nme in ('cv_w_dw', 'lru_w_conv', 'ffn_w_dw'))
        ncol = w_in.shape[2]

        lam_re, lam_im = vec(inp['s5_lam_re']), vec(inp['s5_lam_im'])
        log_step = jnp.broadcast_to(inp['s5_log_step'][l][:, None], (S5_GROUPS, S5_STATE)).reshape(1, n_state)
        b_re = _block_diag(inp['s5_b_re'][l].transpose(0, 2, 1))
        b_im = _block_diag(inp['s5_b_im'][l].transpose(0, 2, 1))
        c_cat = jnp.stack([_block_diag(inp['s5_c_re'][l].transpose(0, 2, 1)),
                           -_block_diag(inp['s5_c_im'][l].transpose(0, 2, 1))]).astype(BF16)
        a_bar, b_bar = s5_param_fwd(f"s5_param_fwd{l}", lam_re, lam_im, log_step, b_re, b_im)
        w_r = _block_diag(inp['lru_w_r'][l]).astype(BF16)
        w_i = _block_diag(inp['lru_w_i'][l]).astype(BF16)
        pool_bd = _block_diag(inp['pool_w'][l]).astype(BF16)
        gate_pars = [w_r, w_i, vec(inp['lru_b_r']), vec(inp['lru_b_i']), vec(inp['lru_lam'])]

        proj = _mm(f"proj{l}", h, w_in, jax.ShapeDtypeStruct((rows, 4 * ncol), F32), (4, rows // tm),
                   pl.BlockSpec((tm, d), lambda j, i: (i, 0)), pl.BlockSpec((None, d, ncol), lambda j, i: (j, 0, 0)),
                   pl.BlockSpec((tm, ncol), lambda j, i: (i, j)), NN)
        proj3 = proj.reshape(1, rows, 4 * ncol)
        ts = min(2048, rows)
        cw, sw = mix_w // 4, n_state // 4
        bu = _mm(f"s5_bu{l}", proj, b_bar, jax.ShapeDtypeStruct((2, rows, n_state), F32), (rows // ts, 2, 4),
                 pl.BlockSpec((ts, cw), lambda i, c, s: (i, s)), pl.BlockSpec((None, cw, sw), lambda i, c, s: (c, s, s)),
                 pl.BlockSpec((None, ts, sw), lambda i, c, s: (c, i, s)), NN)
        z = s5_scan_fwd(f"s5_scan{l}", bu, a_bar)
        y_ssm = _mm(f"s5_read{l}", z, c_cat, jax.ShapeDtypeStruct((rows, mix_w), F32), (rows // ts, 4, 2),
                    pl.BlockSpec((None, ts, sw), lambda i, s, c: (c, i, s)), pl.BlockSpec((None, sw, cw), lambda i, s, c: (c, s, s)),
                    pl.BlockSpec((ts, cw), lambda i, s, c: (i, s)), NN, k_axis=2)
        (h0,) = _rowwise(f"cv_glu{l}", _glu, [(proj, 1, mix_w), (proj, 2, mix_w)], [], 1, [(mix_w, F32)], tm)
        h1 = dwconv_fwd(f"cv_conv{l}", h0.reshape(1, rows, mix_w), 0, mix_w, cv_w_dw[l][None], vec(inp['cv_b_dw'])[None],
                        CV_TAPS, tc)[0]
        xc = dwconv_fwd(f"lru_conv{l}", proj3, 3, mix_w, lru_w_conv[l][None], vec(inp['lru_b_conv'])[None], LRU_TAPS, tc)[0]
        a_t, b_t = _rowwise(f"lru_gate{l}", _lru_gate, [(xc, 0, mix_w)], gate_pars, 2, [(mix_w, F32), (mix_w, F32)], tm)
        hseq = lru_scan_fwd(f"lru_scan{l}", a_t, b_t)
        dgp = pool_fwd(f"pool{l}", proj, 5, tc)
        got = gathered(4 * l + 1, proj)
        w_out = got[('w_out', l)].reshape(d, d)
        w_glu, w_pw = got[('s5_w_glu', l)].reshape(mix_w, mix_w), got[('cv_w_pw', l)].reshape(mix_w, mix_w)
        post_pars = [vec(inp['s5_d']), w_glu, vec(inp['s5_b_glu']), vec(inp['cv_ln_g']), vec(inp['cv_ln_b']), w_pw,
                     vec(inp['cv_b_pw']), pool_bd, vec(inp['pool_scale'])]
        post_rows = [(y_ssm, 0, mix_w), (proj, 0, mix_w), (h1, 0, mix_w), (hseq, 0, mix_w), (proj, 4, mix_w), (dgp, 0, mix_w)]
        (mixed,) = _rowwise(f"mix_post{l}", _mix_post, post_rows, post_pars, 1, [(d, BF16)], tm)
        td = min(1024, rows)
        x1 = _mm(f"out_proj{l}", mixed, w_out, jax.ShapeDtypeStruct((rows, d), F32), (2, rows // td),
                 pl.BlockSpec((td, d), lambda j, i: (i, 0)), pl.BlockSpec((d, d // 2), lambda j, i: (0, j)),
                 pl.BlockSpec((td, d // 2), lambda j, i: (i, j)), NN,
                 add=xcur, add_spec=pl.BlockSpec((td, d // 2), lambda j, i: (i, j)))

        h2 = rms_fwd(f"rms_ffn{l}", x1, vec(inp['norm_ffn_g']), tm)
        tu = min(512, rows)
        w_up = gathered(4 * l + 2, x1)[('ffn_w_up', l)]
        up = _mm(f"ffn_up{l}", h2, w_up, jax.ShapeDtypeStruct((4, rows, ffn_half), F32), (4, rows // tu),
                 pl.BlockSpec((tu, d), lambda k, i: (i, 0)), pl.BlockSpec((None, ffn_half, d), lambda k, i: (k, 0, 0)),
                 pl.BlockSpec((None, tu, ffn_half), lambda k, i: (k, i, 0)), NT)
        w_dw = ffn_w_dw[l].reshape(FFN_TAPS, 2, ffn_half).transpose(1, 0, 2)
        b_dw = inp['ffn_b_dw'][l].reshape(2, 1, ffn_half)
        act = ffn_gate_fwd(f"ffn_gate{l}", up, w_dw, b_dw, tc)
        w_down = gathered(4 * l + 3, up)[('ffn_w_down', l)].reshape(2, ffn_half, d)
        x2 = _mm(f"ffn_down{l}", act, w_down, jax.ShapeDtypeStruct((rows, d), F32), (rows // td, 4),
                 pl.BlockSpec((2, td, ffn_half), lambda i, j: (0, i, 0)), pl.BlockSpec((2, ffn_half, d // 4), lambda i, j: (0, 0, j)),
                 pl.BlockSpec((td, d // 4), lambda i, j: (i, j)), NN, inner=("lead", 2),
                 add=x1, add_spec=pl.BlockSpec((td, d // 4), lambda i, j: (i, j)))
        saved.append(dict(x=xcur, h=h, proj=proj, z=z, y_ssm=y_ssm, h0=h0, h1=h1, xc=xc, a_t=a_t, hseq=hseq, dgp=dgp,
                          mixed=mixed, x1=x1, h2=h2, up=up, act=act, w_in=w_in, w_out=w_out, w_up=w_up, w_down=w_down,
                          a_bar=a_bar, b_bar=b_bar, c_cat=c_cat, post_pars=post_pars, gate_pars=gate_pars, w_dw=w_dw, b_dw=b_dw,
                          s5=(lam_re, lam_im, log_step, b_re, b_im), cv_w=cv_w_dw[l][None], lru_w=lru_w_conv[l][None]))
        xcur = x2

    loss_row, dx, dx_op, dg_final = final_loss("final_loss", xcur, inp['norm_final_g'].reshape(1, d), target, tm)
    grads['norm_final_g'] = dg_final.reshape(d)

    big_g = {nme: [None] * depth for nme in BIG}
    reduce_groups = []

    def start_reduce(tag, keys):
        pieces = []
        for nme, lyr in keys:
            g = big_g[nme][lyr]
            if nme == 'ffn_w_down':
                g = g.reshape(2, 4, ffn_half // 2, d // 2)
            pieces.append(reduce_cores(f"reduce_cores_{nme}{lyr}", g.reshape(2, -1, g.shape[-1])).reshape(g.shape[1:]))
        landing = [lax.empty((3,) + p.shape[1:], p.dtype) for p in pieces]
        groups, token = exchange_start(f"reduce_start_{tag}", [list(zip(pieces, landing))], xy, True)
        reduce_groups.append((tag, keys, groups[0]))
        return token

    for l in reversed(range(depth)):
        s = saved[l]
        ncol = s['w_in'].shape[2]
        tu = min(512, rows)
        dact = _mm(f"d_act{l}", dx_op, s['w_down'], jax.ShapeDtypeStruct((2, rows, ffn_half), F32), (2, rows // tu),
                   pl.BlockSpec((tu, d), lambda k, i: (i, 0)), pl.BlockSpec((None, ffn_half, d), lambda k, i: (k, 0, 0)),
                   pl.BlockSpec((None, tu, ffn_half), lambda k, i: (k, i, 0)), NT)
        tn = d // 4
        tk = min(1024, rows)
        tkb = min(2048, rows)
        big_g['ffn_w_down'][l] = _mm(
            f"dw_down{l}", s['act'], dx_op, jax.ShapeDtypeStruct((2, 2, ffn_half, d // 2), F32), (2, 4, rows // tkb),
            pl.BlockSpec((None, tkb, ffn_half), lambda hh, n, k: (hh, k, 0)), pl.BlockSpec((tkb, tn), lambda hh, n, k: (k, n)),
            pl.BlockSpec((None, None, ffn_half, tn), lambda hh, n, k: (n // 2, hh, 0, n % 2)), TN, k_axis=2)
        dup, dw_dw, db_dw = ffn_gate_bwd(f"ffn_gate_bwd{l}", s['up'], dact, s['w_dw'], s['b_dw'], tc)
        grads['ffn_w_dw'][l] = dw_dw.transpose(1, 0, 2).reshape(FFN_TAPS, 2 * ffn_half)
        grads['ffn_b_dw'][l] = db_dw.reshape(2 * ffn_half)
        dup = dup.reshape(4, rows, ffn_half)
        tm2 = min(1024, rows)
        dh2 = _mm(f"d_h2{l}", dup, s['w_up'], jax.ShapeDtypeStruct((rows, d), F32), (rows // tm2, 2, 4),
                  pl.BlockSpec((None, tm2, ffn_half), lambda i, j, k: (k, i, 0)), pl.BlockSpec((None, ffn_half, d // 2), lambda i, j, k: (k, 0, j)),
                  pl.BlockSpec((tm2, d // 2), lambda i, j, k: (i, j)), NN, k_axis=2)
        tmm = d // 4
        big_g['ffn_w_up'][l] = _mm(
            f"dw_up{l}", dup, s['h2'], jax.ShapeDtypeStruct((2, 4, ffn_half, d // 2), F32), (4, 4, rows // tkb),
            pl.BlockSpec((None, tkb, ffn_half), lambda k4, n, k: (k4, k, 0)), pl.BlockSpec((tkb, tn), lambda k4, n, k: (k, n)),
            pl.BlockSpec((None, None, ffn_half, tn), lambda k4, n, k: (n // 2, k4, 0, n % 2)), TN, k_axis=2)
        token = start_reduce(f"ffn{l}", [('ffn_w_down', l), ('ffn_w_up', l)])
        dx1, dx1_op, dg = rms_bwd(f"rms_ffn_bwd{l}", s['x1'], inp['norm_ffn_g'][l].reshape(1, d) + token[0, 0], dh2, dx, tm)
        grads['norm_ffn_g'][l] = dg.reshape(d)
        dmixed = _mm(f"d_mixed{l}", dx1_op, s['w_out'], jax.ShapeDtypeStruct((rows, d), F32), (rows // tm2, 4),
                     pl.BlockSpec((tm2, d), lambda i, j: (i, 0)), pl.BlockSpec((d // 4, d), lambda i, j: (j, 0)),
                     pl.BlockSpec((tm2, d // 4), lambda i, j: (i, j)), NT)
        tq = mix_w // 2
        big_g['w_out'][l] = _mm(
            f"dw_out{l}", s['mixed'], dx1_op, jax.ShapeDtypeStruct((2, 4, tq, d), F32), (4, rows // tk),
            pl.BlockSpec((tk, 2 * tq), lambda t, k: (k, t)), pl.BlockSpec((tk, d), lambda t, k: (k, 0)),
            pl.BlockSpec((2, None, tq, d), lambda t, k: (0, t, 0, 0)), TN, k_axis=1)
        post_rows = [(s['y_ssm'], 0, mix_w), (s['proj'], 0, mix_w), (s['h1'], 0, mix_w), (s['hseq'], 0, mix_w),
                     (s['proj'], 4, mix_w), (s['dgp'], 0, mix_w), (dmixed, 0, d)]
        res = _rowwise(f"mix_post_bwd{l}", _mix_post, post_rows, s['post_pars'], 1, [(mix_w, F32)] * 6, tm, with_grads=True)
        dy_ssm, du_dir, dh1, dhseq, dlru_g, ddgp = res[:6]
        dd, dwglu, dbglu, dlng, dlnb, dwpw, dbpw, dpoolbd, dscale = res[6:]
        grads['s5_d'][l], grads['s5_b_glu'][l] = dd.reshape(mix_w), dbglu.reshape(mix_w)
        grads['cv_ln_g'][l], grads['cv_ln_b'][l], grads['cv_b_pw'][l] = dlng.reshape(mix_w), dlnb.reshape(mix_w), dbpw.reshape(mix_w)
        grads['pool_w'][l] = _diag_blocks(dpoolbd, len(POOL_WINDOWS))
        grads['pool_scale'][l] = dscale.reshape(mix_w)
        big_g['s5_w_glu'][l] = _halves(dwglu, 4)
        big_g['cv_w_pw'][l] = _halves(dwpw, 4)
        ts = min(2048, rows)
        cw, sw = mix_w // 4, n_state // 4
        slab = jnp.arange(mix_w)[:, None] // cw == jnp.arange(n_state)[None, :] // sw
        dz = _mm(f"s5_dz{l}", dy_ssm, s['c_cat'], jax.ShapeDtypeStruct((2, rows, n_state), F32), (rows // ts, 2, 4),
                 pl.BlockSpec((ts, cw), lambda i, c, q: (i, q)), pl.BlockSpec((None, sw, cw), lambda i, c, q: (c, q, q)),
                 pl.BlockSpec((None, ts, sw), lambda i, c, q: (c, i, q)), NT)
        dccat = _mm(f"s5_dc{l}", s['z'], dy_ssm, jax.ShapeDtypeStruct((2, n_state, mix_w), F32), (2, 4, rows // tk),
                    pl.BlockSpec((None, tk, sw), lambda c, q, k: (c, k, q)), pl.BlockSpec((tk, cw), lambda c, q, k: (k, q)),
                    pl.BlockSpec((None, sw, cw), lambda c, q, k: (c, q, q)), TN, k_axis=2)
        dccat = jnp.where(slab.T, dccat, 0.0)
        grads['s5_c_re'][l] = _diag_blocks(dccat[0], S5_GROUPS).transpose(0, 2, 1)
        grads['s5_c_im'][l] = -_diag_blocks(dccat[1], S5_GROUPS).transpose(0, 2, 1)
        lam, da_bar = s5_scan_bwd(f"s5_scan_bwd{l}", dz, s['z'], s['a_bar'])
        du = _mm(f"s5_du{l}", lam, s['b_bar'], jax.ShapeDtypeStruct((rows, mix_w), F32), (rows // ts, 4, 2),
                 pl.BlockSpec((None, ts, sw), lambda i, q, c: (c, i, q)), pl.BlockSpec((None, cw, sw), lambda i, q, c: (c, q, q)),
                 pl.BlockSpec((ts, cw), lambda i, q, c: (i, q)), NT, k_axis=2,
                 add=du_dir, add_spec=pl.BlockSpec((ts, cw), lambda i, q, c: (i, q)))
        dbbar = _mm(f"s5_db{l}", s['proj'], lam, jax.ShapeDtypeStruct((2, mix_w, n_state), F32), (2, 4, rows // tk),
                    pl.BlockSpec((tk, cw), lambda c, q, k: (k, q)), pl.BlockSpec((None, tk, sw), lambda c, q, k: (c, k, q)),
                    pl.BlockSpec((None, cw, sw), lambda c, q, k: (c, q, q)), TN, k_axis=2)
        dbbar = jnp.where(slab, dbbar, 0.0)
        dlr, dli, dls, dbre, dbim = s5_param_bwd(f"s5_param_bwd{l}", *s['s5'], da_bar, dbbar, gsum)
        grads['s5_lam_re'][l] = dlr.reshape(S5_GROUPS, S5_STATE)
        grads['s5_lam_im'][l] = dli.reshape(S5_GROUPS, S5_STATE)
        grads['s5_log_step'][l] = dls[0, :S5_GROUPS]
        grads['s5_b_re'][l] = _diag_blocks(dbre, S5_GROUPS).transpose(0, 2, 1)
        grads['s5_b_im'][l] = _diag_blocks(dbim, S5_GROUPS).transpose(0, 2, 1)
        dh0, dw_cv, db_cv = dwconv_bwd(f"cv_conv_bwd{l}", dh1.reshape(1, rows, mix_w), s['h0'].reshape(1, rows, mix_w), 0, mix_w,
                                       s['cv_w'], CV_TAPS, tc)
        grads['cv_w_dw'][l], grads['cv_b_dw'][l] = dw_cv[0], db_cv.reshape(mix_w)
        dv, dgg = _rowwise(f"cv_glu_bwd{l}", _glu, [(s['proj'], 1, mix_w), (s['proj'], 2, mix_w), (dh0[0], 0, mix_w)], [], 1,
                           [(mix_w, F32)] * 2, tm, with_grads=True)
        da_t, db_t = lru_scan_bwd(f"lru_scan_bwd{l}", dhseq, s['a_t'], s['hseq'])
        res = _rowwise(f"lru_gate_bwd{l}", _lru_gate, [(s['xc'], 0, mix_w), (da_t, 0, mix_w), (db_t, 0, mix_w)], s['gate_pars'], 2,
                       [(mix_w, F32)], tm, with_grads=True)
        dxc, dwr, dwi, dbr, dbi, dlam = res
        grads['lru_w_r'][l], grads['lru_w_i'][l] = _diag_blocks(dwr, LRU_HEADS), _diag_blocks(dwi, LRU_HEADS)
        grads['lru_b_r'][l], grads['lru_b_i'][l], grads['lru_lam'][l] = dbr.reshape(mix_w), dbi.reshape(mix_w), dlam.reshape(mix_w)
        dlx, dw_lc, db_lc = dwconv_bwd(f"lru_conv_bwd{l}", dxc.reshape(1, rows, mix_w), s['proj'].reshape(1, rows, 4 * ncol), 3, mix_w,
                                       s['lru_w'], LRU_TAPS, tc)
        grads['lru_w_conv'][l], grads['lru_b_conv'][l] = dw_lc[0], db_lc.reshape(mix_w)
        dpx = pool_bwd(f"pool_bwd{l}", ddgp, tc)
        dproj = jnp.concatenate([du, dv, dgg, dlx[0], dlru_g, dpx], axis=-1)
        dh = _mm(f"d_h{l}", dproj, s['w_in'], jax.ShapeDtypeStruct((rows, d), F32), (rows // tm, 4),
                 pl.BlockSpec((tm, 4 * ncol), lambda i, j: (i, 0)), pl.BlockSpec((4, d // 4, ncol), lambda i, j: (0, j, 0)),
                 pl.BlockSpec((tm, d // 4), lambda i, j: (i, j)), NT, inner=("cols", 4))
        tk2 = min(2048, rows)
        big_g['w_in'][l] = _mm(
            f"dw_in{l}", s['h'], dproj, jax.ShapeDtypeStruct((2, 4, d // 2, ncol), F32), (4, 2, rows // tk2),
            pl.BlockSpec((tk2, d // 2), lambda k4, m, k: (k, m)), pl.BlockSpec((tk2, ncol), lambda k4, m, k: (k, k4)),
            pl.BlockSpec((None, None, d // 2, ncol), lambda k4, m, k: (m, k4, 0, 0)), TN, k_axis=2)
        token = start_reduce(f"mix{l}", [('w_out', l), ('s5_w_glu', l), ('cv_w_pw', l), ('w_in', l)])
        dx, dx_op, dg = rms_bwd(f"rms_mix_bwd{l}", s['x'], inp['norm_mix_g'][l].reshape(1, d) + token[0, 0], dh, dx1, tm)
        grads['norm_mix_g'][l] = dg.reshape(d)

    small = [nme for nme in WEIGHTS if nme not in BIG]
    full_g = {nme: (grads[nme] if nme == 'norm_final_g' else jnp.stack(grads[nme])) for nme in small}
    flat = jnp.concatenate([full_g[nme].reshape(-1) for nme in small])
    packed = jnp.pad(flat, (0, (-flat.shape[0]) % (128 * 64))).reshape(-1, 128)
    chip_sum = pair_sum("small_pair_sum", packed)
    small_zone = place_tile("place_small_grads", chip_sum[None], 0, my_chip, dtype=F32)
    (small_group,), small_token = exchange_start("small_start", [[small_zone]], xy, False)

    t_full = {}
    for tag, keys, group in reduce_groups:
        pieces, parts = exchange_wait(f"reduce_wait_{tag}", group, small_token, xy, True)
        for key, own, got in zip(keys, pieces, parts):
            t_full[key] = sum_and_share(f"share_cores_{key[0]}{key[1]}", own, got, my_chip)

    outs, done_big = {}, []
    tiles = {'w_in': 256, 'w_out': 128, 'ffn_w_up': 128, 'ffn_w_down': 256, 's5_w_glu': 64, 'cv_w_pw': 64}
    for nme in BIG:
        g0, g1 = t_full[(nme, 0)], t_full[(nme, 1)]
        if nme == 'ffn_w_up':
            res = adamw_sharded(f"adamw_{nme}", *(jnp.swapaxes(inp[p + nme], 1, 2) for p in ('', 'm_', 'v_')), g0, g1, True, tiles[nme])
            outs[nme] = tuple(jnp.swapaxes(r, 1, 2) for r in res)
        else:
            res = adamw_sharded(f"adamw_{nme}", inp[nme], inp['m_' + nme], inp['v_' + nme], g0, g1, nme == 'ffn_w_down', tiles[nme])
            outs[nme] = res
        done_big.append(res[1][:1, :1, :1].reshape(1))

    after_big = sum(done_big)
    (g4,) = exchange_wait("small_wait", small_group, after_big, xy, False)[1]
    gsum_small = sum_lead("sum_small", g4, _row_tile(g4.shape[1], 1024)).reshape(-1)
    red, off = {}, 0
    for nme in small:
        g = gsum_small[off:off + full_g[nme].size].reshape(full_g[nme].shape)
        off += full_g[nme].size
        if nme in SMALL_SHARDED:
            width = inp[nme].shape[2]
            g = lax.dynamic_slice_in_dim(g, my_chip[0] * width, width, axis=2)
        red[nme] = g

    def pack(tree):
        f = jnp.concatenate([tree[nme].reshape(-1) for nme in small])
        return jnp.pad(f, (0, (-f.shape[0]) % (128 * 64))).reshape(-1, 128)

    pd, pm, pv = adamw_flat("adamw_small", pack({n_: inp[n_] for n_ in small}), pack(red), pack({n_: inp['m_' + n_] for n_ in small}),
                            pack({n_: inp['v_' + n_] for n_ in small}))
    off = 0
    for nme in small:
        size, shape = inp[nme].size, inp[nme].shape
        outs[nme] = (red[nme],) + tuple(p.reshape(-1)[off:off + size].reshape(shape) for p in (pd, pm, pv))
        off += size

    loss = lax.psum(loss_row[0, 0], ("x", "y", "c"))
    result = [loss, dx[None]]
    for part in range(4):
        result += [outs[nme][part] for nme in WEIGHTS]
    return tuple(result)


def kernel(x, norm_mix_g, w_in, s5_lam_re, s5_lam_im, s5_log_step, s5_b_re, s5_b_im, s5_c_re, s5_c_im, s5_d, s5_w_glu, s5_b_glu, cv_w_dw, cv_b_dw, cv_ln_g, cv_ln_b, cv_w_pw, cv_b_pw, lru_w_conv, lru_b_conv, lru_w_r, lru_b_r, lru_w_i, lru_b_i, lru_lam, pool_w, pool_scale, w_out, norm_ffn_g, ffn_w_up, ffn_w_dw, ffn_b_dw, ffn_w_down, norm_final_g, loss_target, m_norm_mix_g, m_w_in, m_s5_lam_re, m_s5_lam_im, m_s5_log_step, m_s5_b_re, m_s5_b_im, m_s5_c_re, m_s5_c_im, m_s5_d, m_s5_w_glu, m_s5_b_glu, m_cv_w_dw, m_cv_b_dw, m_cv_ln_g, m_cv_ln_b, m_cv_w_pw, m_cv_b_pw, m_lru_w_conv, m_lru_b_conv, m_lru_w_r, m_lru_b_r, m_lru_w_i, m_lru_b_i, m_lru_lam, m_pool_w, m_pool_scale, m_w_out, m_norm_ffn_g, m_ffn_w_up, m_ffn_w_dw, m_ffn_b_dw, m_ffn_w_down, m_norm_final_g, v_norm_mix_g, v_w_in, v_s5_lam_re, v_s5_lam_im, v_s5_log_step, v_s5_b_re, v_s5_b_im, v_s5_c_re, v_s5_c_im, v_s5_d, v_s5_w_glu, v_s5_b_glu, v_cv_w_dw, v_cv_b_dw, v_cv_ln_g, v_cv_ln_b, v_cv_w_pw, v_cv_b_pw, v_lru_w_conv, v_lru_b_conv, v_lru_w_r, v_lru_b_r, v_lru_w_i, v_lru_b_i, v_lru_lam, v_pool_w, v_pool_scale, v_w_out, v_norm_ffn_g, v_ffn_w_up, v_ffn_w_dw, v_ffn_b_dw, v_ffn_w_down, v_norm_final_g):
    inp = dict(locals())
    return _step(inp)
```

```python
import functools

import jax
import jax.numpy as jnp
from jax import lax
from jax.experimental import pallas as pl
from jax.experimental.pallas import tpu as pltpu

F32 = jnp.float32
BF16 = jnp.bfloat16

VMEM_LIMIT_BYTES = 56 * 1024 * 1024
SUBLANES = 8

EPS = 1e-6
S5_GROUPS, S5_STATE, S5_GROUP_CH = 32, 64, 16
LRU_HEADS, LRU_C = 8, 8.0
POOL_WINDOWS = (2, 4, 8, 16)
CV_TAPS, LRU_TAPS, FFN_TAPS = 31, 4, 3
SCAN_CHUNK = 64
GELU_K0, GELU_K1 = 0.7978845608028654, 0.044715

ADAM_LR, ADAM_B1, ADAM_B2, ADAM_EPS, ADAM_WD, ADAM_STEP = 0.001, 0.9, 0.999, 1e-08, 0.01, 10

NN = ((1,), (0,))
NT = ((1,), (1,))
TN = ((0,), (0,))

WEIGHTS = ['norm_mix_g', 'w_in', 's5_lam_re', 's5_lam_im', 's5_log_step', 's5_b_re', 's5_b_im', 's5_c_re', 's5_c_im',
           's5_d', 's5_w_glu', 's5_b_glu', 'cv_w_dw', 'cv_b_dw', 'cv_ln_g', 'cv_ln_b', 'cv_w_pw', 'cv_b_pw',
           'lru_w_conv', 'lru_b_conv', 'lru_w_r', 'lru_b_r', 'lru_w_i', 'lru_b_i', 'lru_lam', 'pool_w', 'pool_scale',
           'w_out', 'norm_ffn_g', 'ffn_w_up', 'ffn_w_dw', 'ffn_b_dw', 'ffn_w_down', 'norm_final_g']
BIG = ('w_in', 'w_out', 'ffn_w_up', 'ffn_w_down', 's5_w_glu', 'cv_w_pw')
SMALL_SHARDED = {'cv_w_dw': 2, 'lru_w_conv': 2, 'ffn_w_dw': 2}


def _params(sem=None):
    if sem is None:
        return pltpu.CompilerParams(vmem_limit_bytes=VMEM_LIMIT_BYTES)
    return pltpu.CompilerParams(dimension_semantics=sem, vmem_limit_bytes=VMEM_LIMIT_BYTES)


def _row_tile(rows, cap, mult=SUBLANES):
    best = mult
    for t in range(mult, min(rows, cap) + 1, mult):
        if rows % t == 0:
            best = t
    return best


def _bdot(a, b, dims=NN):
    return lax.dot_general(a.astype(BF16), b.astype(BF16), (dims, ((), ())), preferred_element_type=F32)


@jax.custom_vjp
def bdot(a, b):
    return _bdot(a, b)


def _bdot_fwd(a, b):
    return _bdot(a, b), (a, b)


def _bdot_bwd(res, g):
    a, b = res
    return _bdot(g, b, NT).astype(a.dtype), _bdot(a, g, TN).astype(b.dtype)


bdot.defvjp(_bdot_fwd, _bdot_bwd)


def _mm(name, a, b, out_sds, grid, a_spec, b_spec, o_spec, dims, k_axis=None, add=None, add_spec=None, inner=None):
    nk = grid[k_axis] if k_axis is not None else 1
    has_add = add is not None
    acc_shape = tuple(d for d in o_spec.block_shape if d is not None)
    acc_in_out = out_sds.dtype == F32

    def product(a_ref, b_ref):
        if inner is None:
            return _bdot(a_ref[...], b_ref[...], dims)
        kind, n = inner
        width = a_ref.shape[-1] // n
        acc = None
        for j in range(n):
            a_j = a_ref[j] if kind == "lead" else a_ref[:, j * width:(j + 1) * width]
            p = _bdot(a_j, b_ref[j], dims)
            acc = p if acc is None else acc + p
        return acc

    def body(*refs):
        a_ref, b_ref = refs[0], refs[1]
        add_ref = refs[2] if has_add else None
        o_ref = refs[3] if has_add else refs[2]
        prod = product(a_ref, b_ref).reshape(acc_shape)
        if k_axis is None:
            if has_add:
                prod = prod + add_ref[...]
            o_ref[...] = prod.astype(o_ref.dtype)
        else:
            acc_ref = o_ref if acc_in_out else refs[-1]
            k = pl.program_id(k_axis)

            @pl.when(k == 0)
            def _():
                acc_ref[...] = prod

            @pl.when(k > 0)
            def _():
                acc_ref[...] += prod

            if has_add or not acc_in_out:
                @pl.when(k == nk - 1)
                def _():
                    r = acc_ref[...]
                    if has_add:
                        r = r + add_ref[...]
                    o_ref[...] = r.astype(o_ref.dtype)

    sem = tuple("arbitrary" if d == k_axis else "parallel" for d in range(len(grid)))
    in_specs = [a_spec, b_spec] + ([add_spec] if has_add else [])
    args = (a, b) + ((add,) if has_add else ())
    scratch = [pltpu.VMEM(acc_shape, F32)] if (k_axis is not None and not acc_in_out) else []
    return pl.pallas_call(body, out_shape=out_sds, grid=grid, in_specs=in_specs, out_specs=o_spec,
                          scratch_shapes=scratch, compiler_params=_params(sem), name=name)(*args)


def _rms(x, g):
    return x * lax.rsqrt(jnp.mean(x * x, axis=-1, keepdims=True) + EPS) * g


def rms_fwd(name, x, g, tm):
    rows, d = x.shape

    def body(x_ref, g_ref, o_ref):
        o_ref[...] = _rms(x_ref[...], g_ref[...]).astype(BF16)

    return pl.pallas_call(
        body, out_shape=jax.ShapeDtypeStruct((rows, d), BF16), grid=(rows // tm,),
        in_specs=[pl.BlockSpec((tm, d), lambda i: (i, 0)), pl.BlockSpec((1, d), lambda i: (0, 0))],
        out_specs=pl.BlockSpec((tm, d), lambda i: (i, 0)), compiler_params=_params(("parallel",)), name=name)(x, g)


def rms_bwd(name, x, g, dh, dres, tm):
    rows, d = x.shape

    def body(x_ref, g_ref, dh_ref, dres_ref, dx_ref, dxb_ref, dg_ref):
        xv, dy = x_ref[...], dh_ref[...]
        r = lax.rsqrt(jnp.mean(xv * xv, axis=-1, keepdims=True) + EPS)
        dyg = dy * g_ref[...]
        s = jnp.mean(dyg * xv, axis=-1, keepdims=True)
        dx = r * dyg - xv * (r * r * r * s) + dres_ref[...]
        dx_ref[...] = dx
        dxb_ref[...] = dx.astype(BF16)

        @pl.when(pl.program_id(0) == 0)
        def _():
            dg_ref[...] = jnp.zeros_like(dg_ref)

        dg_ref[...] += jnp.sum(dy * xv * r, axis=0, keepdims=True)

    row = pl.BlockSpec((tm, d), lambda i: (i, 0))
    vec = pl.BlockSpec((1, d), lambda i: (0, 0))
    return pl.pallas_call(
        body, out_shape=(jax.ShapeDtypeStruct((rows, d), F32), jax.ShapeDtypeStruct((rows, d), BF16), jax.ShapeDtypeStruct((1, d), F32)),
        grid=(rows // tm,), in_specs=[row, vec, row, row], out_specs=(row, row, vec),
        compiler_params=_params(("arbitrary",)), name=name)(x, g, dh, dres)


def final_loss(name, x, g, target, tm):
    rows, d = x.shape

    def body(x_ref, g_ref, t_ref, l_ref, dx_ref, dxb_ref, dg_ref):
        def f(xv, gv):
            e = _rms(xv, gv) - t_ref[...]
            return 0.5 * jnp.sum(jnp.mean(e * e, axis=-1))

        loss, (dx, dg) = jax.value_and_grad(f, argnums=(0, 1))(x_ref[...], g_ref[...])
        dx_ref[...] = dx
        dxb_ref[...] = dx.astype(BF16)

        @pl.when(pl.program_id(0) == 0)
        def _():
            l_ref[...] = jnp.zeros_like(l_ref)
            dg_ref[...] = jnp.zeros_like(dg_ref)

        l_ref[...] += jnp.full(l_ref.shape, loss, F32)
        dg_ref[...] += dg

    row = pl.BlockSpec((tm, d), lambda i: (i, 0))
    vec = pl.BlockSpec((1, d), lambda i: (0, 0))
    lspec = pl.BlockSpec((1, 128), lambda i: (0, 0))
    return pl.pallas_call(
        body, out_shape=(jax.ShapeDtypeStruct((1, 128), F32), jax.ShapeDtypeStruct((rows, d), F32), jax.ShapeDtypeStruct((rows, d), BF16),
                         jax.ShapeDtypeStruct((1, d), F32)),
        grid=(rows // tm,), in_specs=[row, vec, row], out_specs=(lspec, row, row, vec),
        compiler_params=_params(("arbitrary",)), name=name)(x, g, target)


def _rowwise(name, fn, row_ins, par_ins, n_row_out, row_out_dtypes, tm, with_grads=False):
    rows = row_ins[0][0].shape[0]
    n_prim = len(row_ins) - (n_row_out if with_grads else 0)
    n_par = len(par_ins)

    def body(*refs):
        ins = [r[...] for r in refs[:len(row_ins) + n_par]]
        outs = refs[len(row_ins) + n_par:]
        prim, cts, pars = ins[:n_prim], ins[n_prim:len(row_ins)], ins[len(row_ins):]
        if not with_grads:
            res = fn(*prim, *pars)
            for o_ref, r in zip(outs, res):
                o_ref[...] = r.astype(o_ref.dtype)
            return
        _, vjp = jax.vjp(fn, *prim, *[p.astype(F32) for p in pars])
        grads = vjp(tuple(cts))
        for o_ref, gr in zip(outs[:n_prim], grads[:n_prim]):
            o_ref[...] = gr.astype(o_ref.dtype)

        @pl.when(pl.program_id(0) == 0)
        def _():
            for o_ref in outs[n_prim:]:
                o_ref[...] = jnp.zeros_like(o_ref)

        for o_ref, gr in zip(outs[n_prim:], grads[n_prim:]):
            o_ref[...] += gr.astype(F32)

    in_specs = [pl.BlockSpec((tm, w), (lambda i, c=c: (i, c))) for (_, c, w) in row_ins]
    in_specs += [pl.BlockSpec(p.shape, (lambda i, n=p.ndim: (0,) * n)) for p in par_ins]
    args = [a for (a, _, _) in row_ins] + list(par_ins)
    if not with_grads:
        out_shape = tuple(jax.ShapeDtypeStruct((rows, w), dt) for (w, dt) in row_out_dtypes)
        out_specs = tuple(pl.BlockSpec((tm, w), lambda i: (i, 0)) for (w, _) in row_out_dtypes)
        sem = ("parallel",)
    else:
        out_shape = tuple(jax.ShapeDtypeStruct((rows, w), dt) for (w, dt) in row_out_dtypes)
        out_shape += tuple(jax.ShapeDtypeStruct(p.shape, F32) for p in par_ins)
        out_specs = tuple(pl.BlockSpec((tm, w), lambda i: (i, 0)) for (w, _) in row_out_dtypes)
        out_specs += tuple(pl.BlockSpec(p.shape, (lambda i, n=p.ndim: (0,) * n)) for p in par_ins)
        sem = ("arbitrary",)
    return pl.pallas_call(body, out_shape=out_shape, grid=(rows // tm,), in_specs=in_specs, out_specs=out_specs,
                          compiler_params=_params(sem), name=name)(*args)


def _glu(v, g):
    return (v * jax.nn.sigmoid(g),)


def _neg_expm1(z):
    return -jnp.tanh(0.5 * z) * (jnp.exp(z) + 1.0)


def _lru_gate(xc, w_r, w_i, b_r, b_i, lam):
    r = jax.nn.sigmoid(bdot(xc, w_r) + b_r)
    i = jax.nn.sigmoid(bdot(xc, w_i) + b_i)
    log_a = -LRU_C * r * jax.nn.softplus(-lam)
    a = jnp.exp(log_a)
    mult = jnp.sqrt(_neg_expm1(2.0 * log_a))
    return a, mult * (i * xc)


def _layernorm(x, g, b):
    mu = jnp.mean(x, axis=-1, keepdims=True)
    var = jnp.mean(jnp.square(x - mu), axis=-1, keepdims=True)
    return (x - mu) * lax.rsqrt(var + EPS) * g + b


def _mix_post(y_ssm, u, h1, hseq, lru_g, dgp, s5_d, w_glu, b_glu, ln_g, ln_b, w_pw, b_pw, pool_bd, pool_scale):
    y = y_ssm + s5_d * u
    gl = jax.nn.gelu(y, approximate=True)
    out_s5 = gl * jax.nn.sigmoid(bdot(gl, w_glu) + b_glu)
    out_cv = bdot(jax.nn.silu(_layernorm(h1, ln_g, ln_b)), w_pw) + b_pw
    out_lru = hseq * jax.nn.gelu(lru_g, approximate=True)
    out_pool = bdot(dgp, pool_bd) * pool_scale
    return (jnp.concatenate([out_s5, out_cv, out_lru, out_pool], axis=-1),)


def _gelu_terms(x):
    sq = x * x
    t = jnp.tanh(x * (GELU_K0 + (GELU_K0 * GELU_K1) * sq))
    return sq, t, 0.5 + 0.5 * t


def _halo_rows(taps):
    return -(-(taps - 1) // SUBLANES) * SUBLANES


def _row_windows(ext, offsets, n, shifted_ref=None):
    if shifted_ref is None:
        return {off: ext[off:off + n] for off in offsets}
    room = ext.shape[0] - SUBLANES
    slots, out = {}, {}
    for off in offsets:
        r = off % SUBLANES
        if r == 0:
            out[off] = ext[off:off + n]
            continue
        if r not in slots:
            slots[r] = len(slots)
            shifted_ref[slots[r]] = ext[r:r + room]
        out[off] = shifted_ref[slots[r], off - r:off - r + n, :]
    return out


def _shift_scratch(taps, tm, c):
    return [pltpu.VMEM((SUBLANES - 1, _halo_rows(taps) + tm - SUBLANES, c), F32)] if taps > SUBLANES else []


def dwconv_fwd(name, x, cblk, c, w, b, taps, tm, out_dtype=F32):
    nb = w.shape[0]
    rows = x.shape[1]
    halo = _halo_rows(taps)
    per = tm // halo

    def body(x_ref, h_ref, w_ref, b_ref, o_ref, *shifted):
        i = pl.program_id(1)
        prev = jnp.where(i > 0, h_ref[...], 0.0)
        ext = jnp.concatenate([prev, x_ref[...]], axis=0)
        win = _row_windows(ext, [halo - (taps - 1) + k for k in range(taps)], tm, *shifted)
        acc = jnp.broadcast_to(b_ref[...], (tm, c))
        for k in range(taps):
            acc = acc + w_ref[k:k + 1, :] * win[halo - (taps - 1) + k]
        o_ref[...] = acc.astype(o_ref.dtype)

    return pl.pallas_call(
        body, out_shape=jax.ShapeDtypeStruct((nb, rows, c), out_dtype), grid=(nb, rows // tm),
        in_specs=[pl.BlockSpec((None, tm, c), lambda n, i: (n, i, cblk)),
                  pl.BlockSpec((None, halo, c), lambda n, i: (n, jnp.maximum(i * per - 1, 0), cblk)),
                  pl.BlockSpec((None, taps, c), lambda n, i: (n, 0, 0)),
                  pl.BlockSpec((None, 1, c), lambda n, i: (n, 0, 0))],
        out_specs=pl.BlockSpec((None, tm, c), lambda n, i: (n, i, 0)), scratch_shapes=_shift_scratch(taps, tm, c),
        compiler_params=_params(("parallel", "parallel")), name=name)(x, x, w, b)


def dwconv_bwd(name, dy, x, cblk, c, w, taps, tm, dx_dtype=F32):
    nb = w.shape[0]
    rows = x.shape[1]
    halo = _halo_rows(taps)
    per = tm // halo
    n_tiles = rows // tm
    last_halo = rows // halo - 1

    def body(dy_ref, dn_ref, x_ref, xp_ref, w_ref, dx_ref, dw_ref, db_ref, *shifted):
        i = pl.program_id(1)
        dyv = dy_ref[...]
        nxt = jnp.where(i < n_tiles - 1, dn_ref[...], 0.0)
        dext = jnp.concatenate([dyv, nxt], axis=0)
        prev = jnp.where(i > 0, xp_ref[...], 0.0)
        xext = jnp.concatenate([prev, x_ref[...]], axis=0)
        acc = jnp.zeros((tm, c), F32)

        @pl.when(i == 0)
        def _():
            dw_ref[...] = jnp.zeros_like(dw_ref)
            db_ref[...] = jnp.zeros_like(db_ref)

        dwin = _row_windows(dext, list(range(taps)), tm, *shifted[:1])
        xwin = _row_windows(xext, [halo - (taps - 1) + k for k in range(taps)], tm, *shifted[1:])
        for k in range(taps):
            acc = acc + w_ref[k:k + 1, :] * dwin[taps - 1 - k]
            dw_ref[k:k + 1, :] += jnp.sum(dyv * xwin[halo - (taps - 1) + k], axis=0, keepdims=True)
        dx_ref[...] = acc.astype(dx_ref.dtype)
        db_ref[...] += jnp.sum(dyv, axis=0, keepdims=True)

    return pl.pallas_call(
        body, out_shape=(jax.ShapeDtypeStruct((nb, rows, c), dx_dtype), jax.ShapeDtypeStruct((nb, taps, c), F32),
                         jax.ShapeDtypeStruct((nb, 1, c), F32)),
        grid=(nb, n_tiles),
        in_specs=[pl.BlockSpec((None, tm, c), lambda n, i: (n, i, 0)),
                  pl.BlockSpec((None, halo, c), lambda n, i: (n, jnp.minimum((i + 1) * per, last_halo), 0)),
                  pl.BlockSpec((None, tm, c), lambda n, i: (n, i, cblk)),
                  pl.BlockSpec((None, halo, c), lambda n, i: (n, jnp.maximum(i * per - 1, 0), cblk)),
                  pl.BlockSpec((None, taps, c), lambda n, i: (n, 0, 0))],
        out_specs=(pl.BlockSpec((None, tm, c), lambda n, i: (n, i, 0)), pl.BlockSpec((None, taps, c), lambda n, i: (n, 0, 0)),
                   pl.BlockSpec((None, 1, c), lambda n, i: (n, 0, 0))),
        scratch_shapes=2 * _shift_scratch(taps, tm, c),
        compiler_params=_params(("parallel", "arbitrary")), name=name)(dy, dy, x, x, w)


def ffn_gate_fwd(name, up, w, b, tm):
    _, rows, c = up.shape
    halo = _halo_rows(FFN_TAPS)
    per = tm // halo

    def body(g_ref, gp_ref, v_ref, w_ref, b_ref, o_ref):
        i = pl.program_id(1)
        ext = jnp.concatenate([jnp.where(i > 0, gp_ref[...], 0.0), g_ref[...]], axis=0)
        gc = jnp.broadcast_to(b_ref[...], (tm, c))
        for k in range(FFN_TAPS):
            off = halo - (FFN_TAPS - 1) + k
            gc = gc + w_ref[k:k + 1, :] * ext[off:off + tm]
        o_ref[...] = (gc * _gelu_terms(gc)[2] * v_ref[...]).astype(BF16)

    return pl.pallas_call(
        body, out_shape=jax.ShapeDtypeStruct((2, rows, c), BF16), grid=(2, rows // tm),
        in_specs=[pl.BlockSpec((None, tm, c), lambda h, i: (h, i, 0)),
                  pl.BlockSpec((None, halo, c), lambda h, i: (h, jnp.maximum(i * per - 1, 0), 0)),
                  pl.BlockSpec((None, tm, c), lambda h, i: (h + 2, i, 0)),
                  pl.BlockSpec((None, FFN_TAPS, c), lambda h, i: (h, 0, 0)), pl.BlockSpec((None, 1, c), lambda h, i: (h, 0, 0))],
        out_specs=pl.BlockSpec((None, tm, c), lambda h, i: (h, i, 0)),
        compiler_params=_params(("parallel", "parallel")), name=name)(up, up, up, w, b)


def ffn_gate_bwd(name, up, dact, w, b, tm):
    _, rows, c = up.shape
    halo = _halo_rows(FFN_TAPS)
    per = tm // halo
    n_tiles = rows // tm
    last_halo = rows // halo - 1
    n_ext = tm + halo

    def body(g_ref, gp_ref, gn_ref, v_ref, vn_ref, d_ref, dn_ref, w_ref, b_ref, dup_ref, dw_ref, db_ref):
        i = pl.program_id(1)
        gext = jnp.concatenate([jnp.where(i > 0, gp_ref[...], 0.0), g_ref[...], gn_ref[...]], axis=0)
        shifted = [gext[halo - (FFN_TAPS - 1) + k:halo - (FFN_TAPS - 1) + k + n_ext] for k in range(FFN_TAPS)]
        gc = jnp.broadcast_to(b_ref[...], (n_ext, c))
        for k in range(FFN_TAPS):
            gc = gc + w_ref[k:k + 1, :] * shifted[k]
        vext = jnp.concatenate([v_ref[...], vn_ref[...]], axis=0)
        dext = jnp.concatenate([d_ref[...], dn_ref[...]], axis=0)
        sq, t, half = _gelu_terms(gc)
        dval = dext * (gc * half)
        dgc = (dext * vext) * (half + (0.5 * gc) * (1.0 - t * t) * (GELU_K0 + (3.0 * GELU_K0 * GELU_K1) * sq))
        r = lax.broadcasted_iota(jnp.int32, (n_ext, c), 0)
        dgc = jnp.where((r < tm) | (i < n_tiles - 1), dgc, 0.0)
        dgate = jnp.zeros((tm, c), F32)
        for k in range(FFN_TAPS):
            dgate = dgate + w_ref[k:k + 1, :] * dgc[FFN_TAPS - 1 - k:FFN_TAPS - 1 - k + tm]
        dup_ref[0] = dgate.astype(BF16)
        dup_ref[1] = dval[:tm].astype(BF16)

        @pl.when(i == 0)
        def _():
            dw_ref[...] = jnp.zeros_like(dw_ref)
            db_ref[...] = jnp.zeros_like(db_ref)

        dgc_t = dgc[:tm]
        for k in range(FFN_TAPS):
            dw_ref[k:k + 1, :] += jnp.sum(dgc_t * shifted[k][:tm], axis=0, keepdims=True)
        db_ref[...] += jnp.sum(dgc_t, axis=0, keepdims=True)

    def tile(shift):
        return pl.BlockSpec((None, tm, c), lambda h, i: (h + shift, i, 0))

    def after(shift):
        return pl.BlockSpec((None, halo, c), lambda h, i: (h + shift, jnp.minimum((i + 1) * per, last_halo), 0))

    return pl.pallas_call(
        body, out_shape=(jax.ShapeDtypeStruct((2, 2, rows, c), BF16), jax.ShapeDtypeStruct((2, FFN_TAPS, c), F32),
                         jax.ShapeDtypeStruct((2, 1, c), F32)),
        grid=(2, n_tiles),
        in_specs=[tile(0), pl.BlockSpec((None, halo, c), lambda h, i: (h, jnp.maximum(i * per - 1, 0), 0)), after(0),
                  tile(2), after(2), tile(0), after(0),
                  pl.BlockSpec((None, FFN_TAPS, c), lambda h, i: (h, 0, 0)), pl.BlockSpec((None, 1, c), lambda h, i: (h, 0, 0))],
        out_specs=(pl.BlockSpec((2, None, tm, c), lambda h, i: (0, h, i, 0)), pl.BlockSpec((None, FFN_TAPS, c), lambda h, i: (h, 0, 0)),
                   pl.BlockSpec((None, 1, c), lambda h, i: (h, 0, 0))),
        compiler_params=_params(("parallel", "arbitrary")), name=name)(up, up, up, up, up, dact, dact, w, b)


POOL_HALO = 16


def pool_fwd(name, proj, cblk, tm):
    rows = proj.shape[0]
    c = 128 * len(POOL_WINDOWS)
    per = tm // POOL_HALO

    def body(x_ref, h_ref, o_ref):
        i = pl.program_id(0)
        xv = x_ref[...]
        ext = jnp.concatenate([jnp.where(i > 0, h_ref[...], 0.0), xv], axis=0)
        t1 = (lax.broadcasted_iota(jnp.int32, (tm, 128), 0) + i * tm + 1).astype(F32)
        outs = []
        for gi, win in enumerate(POOL_WINDOWS):
            seg = ext[:, gi * 128:(gi + 1) * 128]
            s = seg[POOL_HALO:POOL_HALO + tm]
            for j in range(1, win):
                s = s + seg[POOL_HALO - j:POOL_HALO - j + tm]
            outs.append(s / jnp.minimum(t1, float(win)) - xv[:, gi * 128:(gi + 1) * 128])
        o_ref[...] = jnp.concatenate(outs, axis=-1)

    return pl.pallas_call(
        body, out_shape=jax.ShapeDtypeStruct((rows, c), F32), grid=(rows // tm,),
        in_specs=[pl.BlockSpec((tm, c), lambda i: (i, cblk)),
                  pl.BlockSpec((POOL_HALO, c), lambda i: (jnp.maximum(i * per - 1, 0), cblk))],
        out_specs=pl.BlockSpec((tm, c), lambda i: (i, 0)), compiler_params=_params(("parallel",)), name=name)(proj, proj)


def pool_bwd(name, dd, tm):
    rows, c = dd.shape
    per = tm // POOL_HALO
    n_tiles = rows // tm
    last_halo = rows // POOL_HALO - 1

    def body(d_ref, n_ref, o_ref):
        i = pl.program_id(0)
        dv = d_ref[...]
        nxt = jnp.where(i < n_tiles - 1, n_ref[...], 0.0)
        t1 = (lax.broadcasted_iota(jnp.int32, (tm, 128), 0) + i * tm + 1).astype(F32)
        t1n = (lax.broadcasted_iota(jnp.int32, (POOL_HALO, 128), 0) + (i + 1) * tm + 1).astype(F32)
        outs = []
        for gi, win in enumerate(POOL_WINDOWS):
            sl = slice(gi * 128, (gi + 1) * 128)
            q = jnp.concatenate([dv[:, sl] / jnp.minimum(t1, float(win)), nxt[:, sl] / jnp.minimum(t1n, float(win))], axis=0)
            s = q[0:tm]
            for j in range(1, win):
                s = s + q[j:j + tm]
            outs.append(s - dv[:, sl])
        o_ref[...] = jnp.concatenate(outs, axis=-1)

    return pl.pallas_call(
        body, out_shape=jax.ShapeDtypeStruct((rows, c), F32), grid=(n_tiles,),
        in_specs=[pl.BlockSpec((tm, c), lambda i: (i, 0)),
                  pl.BlockSpec((POOL_HALO, c), lambda i: (jnp.minimum((i + 1) * per, last_halo), 0))],
        out_specs=pl.BlockSpec((tm, c), lambda i: (i, 0)), compiler_params=_params(("parallel",)), name=name)(dd, dd)


BLOCK_STEPS = 3


def _shift_down(v, s, fill):
    r = lax.broadcasted_iota(jnp.int32, v.shape, 0)
    return jnp.where(r >= s, pltpu.roll(v, s, 0), fill)


def _shift_up(v, s, fill):
    n = v.shape[0]
    r = lax.broadcasted_iota(jnp.int32, v.shape, 0)
    return jnp.where(r < n - s, pltpu.roll(v, n - s, 0), fill)


def _shift_in_blocks(v, s, fill, reverse):
    n = v.shape[0]
    q = lax.broadcasted_iota(jnp.int32, v.shape, 0) & (SUBLANES - 1)
    if reverse:
        return jnp.where(q < SUBLANES - s, pltpu.roll(v, n - s, 0), fill)
    return jnp.where(q >= s, pltpu.roll(v, s, 0), fill)


def _cscan_blocks(vr, vi, powers, reverse):
    for k, (qr, qi) in enumerate(powers):
        s = 1 << k
        sr, si = _shift_in_blocks(vr, s, 0.0, reverse), _shift_in_blocks(vi, s, 0.0, reverse)
        if reverse:
            vr, vi = vr + qr * sr + qi * si, vi + qr * si - qi * sr
        else:
            vr, vi = vr + qr * sr - qi * si, vi + qr * si + qi * sr
    return vr, vi


def _cscan_table(pr, pi, powers, reverse):
    r = lax.broadcasted_iota(jnp.int32, (SUBLANES, 128), 0)
    at = (SUBLANES - 1) if reverse else 0
    return _cscan_blocks(jnp.where(r == at, pr, 0.0), jnp.where(r == at, -pi if reverse else pi, 0.0), powers, reverse)


def _cscan_chunk(vr, vi, powers, table, carry, reverse):
    vr, vi = _cscan_blocks(vr, vi, powers, reverse)
    tr, ti = table
    cr, ci = carry
    nb = vr.shape[0] // SUBLANES
    outr, outi = [None] * nb, [None] * nb
    edge = 0 if reverse else SUBLANES - 1
    for j in (reversed(range(nb)) if reverse else range(nb)):
        rows = slice(j * SUBLANES, (j + 1) * SUBLANES)
        zr = vr[rows] + tr * cr - ti * ci
        zi = vi[rows] + tr * ci + ti * cr
        outr[j], outi[j] = zr, zi
        cr, ci = zr[edge:edge + 1], zi[edge:edge + 1]
    return jnp.concatenate(outr, axis=0), jnp.concatenate(outi, axis=0), (cr, ci)


def _powers(pr, pi, n):
    out = [(pr, pi)]
    for _ in range(n - 1):
        pr, pi = pr * pr - pi * pi, 2.0 * pr * pi
        out.append((pr, pi))
    return out


def s5_scan_fwd(name, bu, a):
    _, rows, n = bu.shape
    t = min(SCAN_CHUNK, rows)

    def body(bu_ref, a_ref, z_ref):
        pr, pi = a_ref[0], a_ref[1]
        powers = _powers(pr, pi, BLOCK_STEPS)
        table = _cscan_table(pr, pi, powers, False)

        def chunk(ci, carry):
            base = pl.multiple_of(ci * t, t)
            zr, zi, carry = _cscan_chunk(bu_ref[0, pl.ds(base, t), :], bu_ref[1, pl.ds(base, t), :], powers, table, carry, False)
            z_ref[0, pl.ds(base, t), :] = zr
            z_ref[1, pl.ds(base, t), :] = zi
            return carry

        zero = jnp.zeros((1, 128), F32)
        lax.fori_loop(0, rows // t, chunk, (zero, zero))

    return pl.pallas_call(
        body, out_shape=jax.ShapeDtypeStruct((2, rows, n), F32), grid=(n // 128,),
        in_specs=[pl.BlockSpec((2, rows, 128), lambda j: (0, 0, j)), pl.BlockSpec((2, 1, 128), lambda j: (0, 0, j))],
        out_specs=pl.BlockSpec((2, rows, 128), lambda j: (0, 0, j)), compiler_params=_params(("parallel",)), name=name)(bu, a)


def s5_scan_bwd(name, dz, z, a):
    _, rows, n = dz.shape
    t = min(SCAN_CHUNK, rows)
    n_chunks = rows // t

    def body(dz_ref, z_ref, a_ref, lam_ref, da_ref):
        pr, pi = a_ref[0], a_ref[1]
        powers = _powers(pr, pi, BLOCK_STEPS)
        table = _cscan_table(pr, pi, powers, True)

        def chunk(k, carry):
            ci = n_chunks - 1 - k
            base = pl.multiple_of(ci * t, t)
            cr, cim, dar, dai = carry
            lr, li, (cr, cim) = _cscan_chunk(dz_ref[0, pl.ds(base, t), :], dz_ref[1, pl.ds(base, t), :], powers, table, (cr, cim), True)
            lam_ref[0, pl.ds(base, t), :] = lr
            lam_ref[1, pl.ds(base, t), :] = li
            pbase = pl.multiple_of(jnp.maximum(base - SUBLANES, 0), SUBLANES)
            keep = (ci > 0).astype(F32)
            pzr = z_ref[0, pl.ds(pbase, SUBLANES), :][SUBLANES - 1:SUBLANES, :] * keep
            pzi = z_ref[1, pl.ds(pbase, SUBLANES), :][SUBLANES - 1:SUBLANES, :] * keep
            zpr = _shift_down(z_ref[0, pl.ds(base, t), :], 1, pzr)
            zpi = _shift_down(z_ref[1, pl.ds(base, t), :], 1, pzi)
            dar = dar + jnp.sum(lr * zpr + li * zpi, axis=0, keepdims=True)
            dai = dai + jnp.sum(li * zpr - lr * zpi, axis=0, keepdims=True)
            return cr, cim, dar, dai

        zero = jnp.zeros((1, 128), F32)
        _, _, dar, dai = lax.fori_loop(0, n_chunks, chunk, (zero, zero, zero, zero))
        da_ref[0] = dar
        da_ref[1] = dai

    seq = pl.BlockSpec((2, rows, 128), lambda j: (0, 0, j))
    vec = pl.BlockSpec((2, 1, 128), lambda j: (0, 0, j))
    return pl.pallas_call(
        body, out_shape=(jax.ShapeDtypeStruct((2, rows, n), F32), jax.ShapeDtypeStruct((2, 1, n), F32)), grid=(n // 128,),
        in_specs=[seq, seq, vec], out_specs=(seq, vec), compiler_params=_params(("parallel",)), name=name)(dz, z, a)


def _rscan_chunk(a, b, carry, reverse):
    n = a.shape[0]
    shift = _shift_up if reverse else _shift_down
    for k in range(n.bit_length() - 1):
        s = 1 << k
        b = b + a * shift(b, s, 0.0)
        a = a * shift(a, s, 1.0)
    h = b + a * carry
    edge = 0 if reverse else n - 1
    return h, h[edge:edge + 1]


def lru_scan_fwd(name, a, b):
    rows, n = a.shape
    t = min(SCAN_CHUNK, rows)

    def body(a_ref, b_ref, h_ref):
        def chunk(ci, carry):
            base = pl.multiple_of(ci * t, t)
            h, carry = _rscan_chunk(a_ref[pl.ds(base, t), :], b_ref[pl.ds(base, t), :], carry, False)
            h_ref[pl.ds(base, t), :] = h
            return carry

        lax.fori_loop(0, rows // t, chunk, jnp.zeros((1, 128), F32))

    seq = pl.BlockSpec((rows, 128), lambda j: (0, j))
    return pl.pallas_call(body, out_shape=jax.ShapeDtypeStruct((rows, n), F32), grid=(n // 128,), in_specs=[seq, seq],
                          out_specs=seq, compiler_params=_params(("parallel",)), name=name)(a, b)


def lru_scan_bwd(name, dh, a, h):
    rows, n = a.shape
    t = min(SCAN_CHUNK, rows)
    n_chunks = rows // t

    def body(dh_ref, a_ref, h_ref, da_ref, db_ref):
        def chunk(k, carry):
            ci = n_chunks - 1 - k
            base = pl.multiple_of(ci * t, t)
            nbase = pl.multiple_of(jnp.minimum(base + t, rows - SUBLANES), SUBLANES)
            a_next = a_ref[pl.ds(nbase, SUBLANES), :][0:1, :]
            an = _shift_up(a_ref[pl.ds(base, t), :], 1, a_next)
            mu, carry = _rscan_chunk(an, dh_ref[pl.ds(base, t), :], carry, True)
            pbase = pl.multiple_of(jnp.maximum(base - SUBLANES, 0), SUBLANES)
            hp_row = h_ref[pl.ds(pbase, SUBLANES), :][SUBLANES - 1:SUBLANES, :] * (ci > 0).astype(F32)
            hp = _shift_down(h_ref[pl.ds(base, t), :], 1, hp_row)
            da_ref[pl.ds(base, t), :] = mu * hp
            db_ref[pl.ds(base, t), :] = mu
            return carry

        lax.fori_loop(0, n_chunks, chunk, jnp.zeros((1, 128), F32))

    seq = pl.BlockSpec((rows, 128), lambda j: (0, j))
    return pl.pallas_call(
        body, out_shape=(jax.ShapeDtypeStruct((rows, n), F32), jax.ShapeDtypeStruct((rows, n), F32)), grid=(n // 128,),
        in_specs=[seq, seq, seq], out_specs=(seq, seq), compiler_params=_params(("parallel",)), name=name)(dh, a, h)


def _s5_param(lr, li, ls, bre, bim):
    st = jnp.exp(ls)
    er = jnp.exp(lr * st)
    th = li * st
    ar, ai = er * jnp.cos(th), er * jnp.sin(th)
    nr, ni = ar - 1.0, ai
    den = lr * lr + li * li
    cr, ci = (nr * lr + ni * li) / den, (ni * lr - nr * li) / den
    return ar, ai, cr * bre - ci * bim, cr * bim + ci * bre


def s5_param_fwd(name, lr, li, ls, bre, bim):
    gh, n = bre.shape

    def body(lr_ref, li_ref, ls_ref, bre_ref, bim_ref, a_ref, bb_ref):
        ar, ai, br, bi = _s5_param(lr_ref[...], li_ref[...], ls_ref[...], bre_ref[...], bim_ref[...])
        a_ref[0] = ar
        a_ref[1] = ai
        bb_ref[0] = br.astype(BF16)
        bb_ref[1] = bi.astype(BF16)

    return pl.pallas_call(body, out_shape=(jax.ShapeDtypeStruct((2, 1, n), F32), jax.ShapeDtypeStruct((2, gh, n), BF16)),
                          compiler_params=_params(), name=name)(lr, li, ls, bre, bim)


def s5_param_bwd(name, lr, li, ls, bre, bim, da, dbb, gsum):
    gh, n = bre.shape

    def body(lr_ref, li_ref, ls_ref, bre_ref, bim_ref, da_ref, dbb_ref, gs_ref, dlr_ref, dli_ref, dls_ref, dbre_ref, dbim_ref):
        _, vjp = jax.vjp(_s5_param, lr_ref[...], li_ref[...], ls_ref[...], bre_ref[...], bim_ref[...])
        dlr, dli, dls, dbre, dbim = vjp((da_ref[0], da_ref[1], dbb_ref[0], dbb_ref[1]))
        dlr_ref[...] = dlr
        dli_ref[...] = dli
        dls_ref[...] = jnp.dot(jnp.broadcast_to(dls, (SUBLANES, n)), gs_ref[...], preferred_element_type=F32,
                               precision=lax.Precision.HIGHEST)
        dbre_ref[...] = dbre
        dbim_ref[...] = dbim

    vec = jax.ShapeDtypeStruct((1, n), F32)
    mat = jax.ShapeDtypeStruct((gh, n), F32)
    return pl.pallas_call(body, out_shape=(vec, vec, jax.ShapeDtypeStruct((SUBLANES, 128), F32), mat, mat),
                          compiler_params=_params(), name=name)(lr, li, ls, bre, bim, da, dbb, gsum)


def sum_lead(name, x, tr):
    n, rows, cols = x.shape

    def body(x_ref, o_ref):
        acc = x_ref[0]
        for j in range(1, n):
            acc = acc + x_ref[j]
        o_ref[...] = acc

    return pl.pallas_call(
        body, out_shape=jax.ShapeDtypeStruct((rows, cols), x.dtype), grid=(rows // tr,),
        in_specs=[pl.BlockSpec((n, tr, cols), lambda i: (0, i, 0))], out_specs=pl.BlockSpec((tr, cols), lambda i: (i, 0)),
        compiler_params=_params(("parallel",)), name=name)(x)


def _adamw(w, g, m, v):
    m = ADAM_B1 * m + (1.0 - ADAM_B1) * g
    v = ADAM_B2 * v + (1.0 - ADAM_B2) * jnp.square(g)
    m_hat = m / (1.0 - ADAM_B1 ** ADAM_STEP)
    v_hat = v / (1.0 - ADAM_B2 ** ADAM_STEP)
    delta = -ADAM_LR * (m_hat / (jnp.sqrt(v_hat) + ADAM_EPS) + ADAM_WD * w)
    return delta, m, v


def adamw_sharded(name, w, m, v, g0, g1, split_cols, tile):
    _, r, c = w.shape
    if split_cols:
        nt = c // tile
        per = (c // 2) // tile
        block = (None, r, tile)
        wspec = pl.BlockSpec(block, lambda l, t: (l, 0, t))
        gidx = lambda t: (t // per, 0, t % per)
    else:
        nt = r // tile
        per = (r // 2) // tile
        block = (None, tile, c)
        wspec = pl.BlockSpec(block, lambda l, t: (l, t, 0))
        gidx = lambda t: (t // per, t % per, 0)

    def gspec(layer):
        return pl.BlockSpec(block, lambda l, t: gidx(jnp.where(l == layer, t, (nt - 1) * (1 - layer))))

    def body(w_ref, m_ref, v_ref, g0_ref, g1_ref, g_ref, d_ref, nm_ref, nv_ref):
        g = jnp.where(pl.program_id(0) == 0, g0_ref[...], g1_ref[...])
        d, nm, nv = _adamw(w_ref[...], g, m_ref[...], v_ref[...])
        g_ref[...] = g
        d_ref[...] = d
        nm_ref[...] = nm
        nv_ref[...] = nv

    sds = jax.ShapeDtypeStruct(w.shape, F32)
    return pl.pallas_call(body, out_shape=(sds,) * 4, grid=(2, nt), in_specs=[wspec, wspec, wspec, gspec(0), gspec(1)],
                          out_specs=(wspec,) * 4, compiler_params=_params(("arbitrary", "arbitrary")), name=name)(w, m, v, g0, g1)


def adamw_flat(name, w, g, m, v):
    rows, cols = w.shape
    tr = _row_tile(rows, 1024)

    def body(w_ref, g_ref, m_ref, v_ref, d_ref, nm_ref, nv_ref):
        d, nm, nv = _adamw(w_ref[...], g_ref[...], m_ref[...], v_ref[...])
        d_ref[...] = d
        nm_ref[...] = nm
        nv_ref[...] = nv

    blk = pl.BlockSpec((tr, cols), lambda i: (i, 0))
    sds = jax.ShapeDtypeStruct((rows, cols), F32)
    return pl.pallas_call(body, out_shape=(sds,) * 3, grid=(rows // tr,), in_specs=[blk] * 4, out_specs=(blk,) * 3,
                          compiler_params=_params(("parallel",)), name=name)(w, g, m, v)


def _flips(axes):
    out = []
    for fx in ((0, 1) if "x" in axes else (0,)):
        for fy in ((0, 1) if "y" in axes else (0,)):
            for fc in ((0, 1) if "c" in axes else (0,)):
                if fx or fy or fc:
                    out.append((fx, fy, fc))
    return out


def _slot(pos, axes):
    s = 0
    for name, p in zip(("x", "y", "c"), pos):
        if name in axes:
            s = 2 * s + p
    return s


_HBM = pl.BlockSpec(memory_space=pltpu.HBM)
_SEM = pl.BlockSpec(memory_space=pltpu.SEMAPHORE)
_EFFECT = pltpu.SideEffectType.DATAFLOW_SIDE_EFFECTING


def place_own(name, arrs, axes):
    n = len(_flips(axes)) + 1
    na = len(arrs)

    def body(*refs):
        ins, outs, sems = refs[:na], refs[na:2 * na], refs[2 * na]
        my = _slot((lax.axis_index("x"), lax.axis_index("y"), lax.axis_index("c")), axes)
        copies = [pltpu.make_async_copy(ins[a], outs[a].at[my], sems.at[a]) for a in range(na)]
        for cp in copies:
            cp.start()
        for cp in copies:
            cp.wait()

    out_shape = tuple(jax.ShapeDtypeStruct((n,) + a.shape, a.dtype) for a in arrs)
    anyspec = pl.BlockSpec(memory_space=pl.ANY)
    return pl.pallas_call(body, out_shape=out_shape, in_specs=[anyspec] * na, out_specs=(anyspec,) * na,
                          scratch_shapes=[pltpu.SemaphoreType.DMA((na,))], name=name)(*arrs)


def _peers(axes):
    me = (lax.axis_index("x"), lax.axis_index("y"), lax.axis_index("c"))
    return me, [tuple((1 - p) if f else p for p, f in zip(me, fl)) for fl in _flips(axes)]


def place_tile(name, arr, layer, my, slots=4, dtype=BF16, after=None):
    _, r, cols = arr.shape
    tr = _tile_rows(r, cols)

    def body(my_ref, x_ref, *rest):
        rest[-1][...] = x_ref[...].astype(dtype)

    in_specs = [pl.BlockSpec((None, tr, cols), lambda i, my: (layer, i, 0))]
    args = [arr]
    if after is not None:
        in_specs.append(pl.BlockSpec(after.shape, lambda i, my: (0, 0)))
        args.append(after)
    grid_spec = pltpu.PrefetchScalarGridSpec(num_scalar_prefetch=1, grid=(r // tr,), in_specs=in_specs,
                                             out_specs=pl.BlockSpec((None, tr, cols), lambda i, my: (my[0], i, 0)))
    return pl.pallas_call(body, out_shape=jax.ShapeDtypeStruct((slots, r, cols), dtype), grid_spec=grid_spec,
                          compiler_params=_params(("parallel",)), name=name)(my, *args)


def exchange_start(name, groups, axes, scatter):
    flat = [(p if scatter else (p,)) for grp in groups for p in grp]
    per = 2 if scatter else 1
    na, ng, npeer = len(flat), len(groups), len(_flips(axes))

    def body(*refs):
        ops = refs[:per * na]
        zones = ops[(per - 1) * na:]
        sems, token = refs[per * na:per * na + 2 * ng], refs[-1]
        me, peers = _peers(axes)
        my = _slot(me, axes)
        ai = 0
        for g, grp in enumerate(groups):
            for k in range(len(grp)):
                for j, peer in enumerate(peers):
                    src = ops[ai].at[_slot(peer, axes)] if scatter else zones[ai].at[my]
                    dst = zones[ai].at[j] if scatter else zones[ai].at[my]
                    pltpu.make_async_remote_copy(
                        src_ref=src, dst_ref=dst, send_sem=sems[2 * g].at[k * npeer + j],
                        recv_sem=sems[2 * g + 1].at[k * npeer + j], device_id=peer, device_id_type=pl.DeviceIdType.MESH).start()
                ai += 1
        token[...] = jnp.zeros_like(token)

    out_shape, out_specs = [], []
    for grp in groups:
        out_shape += [pltpu.SemaphoreType.DMA((npeer * len(grp),))] * 2
        out_specs += [_SEM, _SEM]
    for idx in range(per):
        out_shape += [pltpu.HBM(p[idx].shape, p[idx].dtype) for p in flat]
        out_specs += [_HBM] * na
    out_shape.append(jax.ShapeDtypeStruct((SUBLANES, 128), F32))
    out_specs.append(pl.BlockSpec(memory_space=pltpu.VMEM))
    args = [pltpu.with_memory_space_constraint(p[idx], pltpu.HBM) for idx in range(per) for p in flat]
    res = pl.pallas_call(body, out_shape=tuple(out_shape), in_specs=[_HBM] * (per * na), out_specs=tuple(out_specs),
                         input_output_aliases={i: 2 * ng + i for i in range(per * na)},
                         compiler_params=pltpu.CompilerParams(has_side_effects=_EFFECT), name=name)(*args)
    thru = res[2 * ng:2 * ng + per * na]
    out, ai = [], 0
    for g, grp in enumerate(groups):
        srcs = list(thru[ai:ai + len(grp)]) if scatter else []
        zones = list(thru[(per - 1) * na + ai:(per - 1) * na + ai + len(grp)])
        out.append(((res[2 * g], res[2 * g + 1]), srcs, zones))
        ai += len(grp)
    return out, res[-1]


def exchange_wait(name, group, after, axes, scatter):
    (send_sems, recv_sems), srcs, zones = group
    n, ns = len(zones), len(srcs)
    npeer = len(_flips(axes))

    def body(*refs):
        z_refs = refs[ns:ns + n]
        ssem, rsem = refs[ns + n], refs[ns + n + 1]
        _, peers = _peers(axes)
        for k in range(n):
            for j, peer in enumerate(peers):
                part = z_refs[k].at[j if scatter else _slot(peer, axes)]
                copy = pltpu.make_async_remote_copy(
                    src_ref=part, dst_ref=part, send_sem=ssem.at[k * npeer + j], recv_sem=rsem.at[k * npeer + j],
                    device_id=peer, device_id_type=pl.DeviceIdType.MESH)
                copy.wait_send()
                copy.wait_recv()

    ops = list(srcs) + list(zones)
    out_shape = tuple(pltpu.HBM(a.shape, a.dtype) for a in ops)
    res = pl.pallas_call(body, out_shape=out_shape, in_specs=[_HBM] * len(ops) + [_SEM, _SEM, pl.BlockSpec(memory_space=pl.ANY)],
                         out_specs=(_HBM,) * len(ops), input_output_aliases={i: i for i in range(len(ops))},
                         compiler_params=pltpu.CompilerParams(has_side_effects=_EFFECT), name=name)(*ops, send_sems, recv_sems, after)
    return list(res[:ns]), list(res[ns:])


def _pair_exchange(name, ins, in_specs, n_steps, tile, fn_send, fn_out, out_shape, out_spec, prefetch=None, wire=F32):
    n_in = len(ins)

    def body(*refs):
        if prefetch is not None:
            refs = refs[1:]
        in_refs, o_ref = refs[:n_in], refs[n_in]
        send_buf, recv_buf, send_sems, recv_sems, credit = refs[n_in + 1:]
        i = pl.program_id(0)
        slot = lax.rem(i, 2)
        c = lax.axis_index("c")
        sibling = (lax.axis_index("x"), lax.axis_index("y"), 1 - c)
        vals = [r[...] for r in in_refs]
        send_buf[slot] = fn_send(*vals, c).astype(wire)

        @pl.when(i >= 2)
        def _():
            pl.semaphore_wait(credit, 1)

        copy = pltpu.make_async_remote_copy(
            src_ref=send_buf.at[slot], dst_ref=recv_buf.at[slot], send_sem=send_sems.at[slot], recv_sem=recv_sems.at[slot],
            device_id=sibling, device_id_type=pl.DeviceIdType.MESH)
        copy.start()
        copy.wait_recv()
        o_ref[...] = fn_out(*vals, recv_buf[slot], c).astype(o_ref.dtype)
        copy.wait_send()

        @pl.when(i < n_steps - 2)
        def _():
            pl.semaphore_signal(credit, inc=1, device_id=sibling, device_id_type=pl.DeviceIdType.MESH)

    scratch = [pltpu.VMEM((2,) + tile, wire), pltpu.VMEM((2,) + tile, wire), pltpu.SemaphoreType.DMA((2,)),
               pltpu.SemaphoreType.DMA((2,)), pltpu.SemaphoreType.REGULAR]
    if prefetch is None:
        return pl.pallas_call(body, out_shape=out_shape, grid=(n_steps,), in_specs=in_specs, out_specs=out_spec,
                              scratch_shapes=scratch, compiler_params=_params(("arbitrary",)), name=name)(*ins)
    grid_spec = pltpu.PrefetchScalarGridSpec(num_scalar_prefetch=1, grid=(n_steps,), in_specs=in_specs, out_specs=out_spec,
                                             scratch_shapes=scratch)
    return pl.pallas_call(body, out_shape=out_shape, grid_spec=grid_spec, compiler_params=_params(("arbitrary",)),
                          name=name)(prefetch, *ins)


def _tile_rows(rows, cols, f32_bytes=3 << 19):
    return _row_tile(rows, max(2 * SUBLANES, f32_bytes // (4 * cols)), 2 * SUBLANES)


def pair_sum(name, x):
    rows, cols = x.shape
    tr = _tile_rows(rows, cols)
    return _pair_exchange(name, [x], [pl.BlockSpec((tr, cols), lambda i: (i, 0))], rows // tr, (tr, cols),
                          lambda v, c: v, lambda v, got, c: v + got, jax.ShapeDtypeStruct((rows, cols), F32),
                          pl.BlockSpec((tr, cols), lambda i: (i, 0)))


def reduce_cores(name, g):
    _, m, cols = g.shape
    tr = _tile_rows(m, cols, 6 << 20)

    def fn_send(g0, g1, c):
        return jnp.where(c == 0, g1, g0)

    def fn_out(g0, g1, got, c):
        return jnp.where(c == 0, g0, g1) + got.astype(F32)

    return _pair_exchange(
        name, [g, g], [pl.BlockSpec((None, tr, cols), lambda i: (0, i, 0)), pl.BlockSpec((None, tr, cols), lambda i: (1, i, 0))],
        m // tr, (tr, cols), fn_send, fn_out, jax.ShapeDtypeStruct((m, cols), BF16), pl.BlockSpec((tr, cols), lambda i: (i, 0)),
        wire=BF16)


def sum_and_share(name, own, parts, my):
    n, r, cols = parts.shape
    tr = _tile_rows(r, cols, 3 << 20)

    def total(o, p):
        acc = o.astype(F32)
        for j in range(n):
            acc = acc + p[j].astype(F32)
        return acc

    def fn_send(o, p, c):
        return total(o, p)

    def fn_out(o, p, got, c):
        mine = total(o, p)
        return jnp.stack([jnp.where(c == 0, mine, got), jnp.where(c == 0, got, mine)])

    return _pair_exchange(
        name, [own, parts], [pl.BlockSpec((None, tr, cols), lambda i, my_ref: (my_ref[0], i, 0)), pl.BlockSpec((n, tr, cols), lambda i, my_ref: (0, i, 0))],
        r // tr, (tr, cols), fn_send, fn_out, jax.ShapeDtypeStruct((2, r, cols), F32),
        pl.BlockSpec((2, tr, cols), lambda i, my_ref: (0, i, 0)), prefetch=my)


def _block_diag(blocks):
    g, r, c = blocks.shape
    eye = jnp.eye(g, dtype=blocks.dtype)
    return (blocks[:, :, None, :] * eye[:, None, :, None]).reshape(g * r, g * c)


def _diag_blocks(mat, g):
    r, c = mat.shape[0] // g, mat.shape[1] // g
    eye = jnp.eye(g, dtype=mat.dtype)
    return (mat.reshape(g, r, g, c) * eye[:, None, :, None]).sum(axis=2)


def _halves(gfull, shards):
    rows, cols = gfull.shape
    return gfull.reshape(shards, 2, rows // shards // 2, cols).transpose(1, 0, 2, 3)


def _step(inp):
    x = inp['x'][0]
    target = inp['loss_target'][0]
    rows, d = x.shape
    depth = inp['w_in'].shape[0]
    mix_w = d // 4
    n_state = S5_GROUPS * S5_STATE
    ffn_half = inp['ffn_w_up'].shape[2]
    tm = min(512, rows)
    tc = min(256, rows)
    xy = ("x", "y")

    my_chip = (2 * lax.axis_index("x") + lax.axis_index("y")).astype(jnp.int32).reshape(1)
    small_keys = [(nme, None) for nme in SMALL_SHARDED]
    group_keys = []
    for l in range(depth):
        group_keys += [[('w_in', l)] + (small_keys if l == 0 else []),
                       [('w_out', l), ('s5_w_glu', l), ('cv_w_pw', l)], [('ffn_w_up', l)], [('ffn_w_down', l)]]

    def zone_of(key, after=None):
        nme, l = key
        src = jnp.swapaxes(inp[nme], 1, 2) if nme == 'ffn_w_up' else inp[nme]
        return place_tile(f"place_{nme}{l}", src, l, my_chip, after=after)

    zones = {('w_in', 0): zone_of(('w_in', 0))}
    zones.update(zip(small_keys, place_own("place_small", [inp[nme] for nme in SMALL_SHARDED], xy)))
    first_group, first_token = exchange_start("gather_start_first", [[zones[key] for key in group_keys[0]]], xy, False)
    for grp in group_keys[1:]:
        zones.update({key: zone_of(key, first_token) for key in grp})
    rest_groups, gather_token = exchange_start("gather_start", [[zones[key] for key in grp] for grp in group_keys[1:]], xy, False)
    gather_groups = first_group + rest_groups

    def gathered(gi, after):
        return dict(zip(group_keys[gi], exchange_wait(f"gather_wait{gi}", gather_groups[gi], after, xy, False)[1]))

    def full_small(g):
        return g.transpose(1, 2, 0, 3).reshape(g.shape[1], g.shape[2], 4 * g.shape[3])

    gsum = jnp.repeat(jnp.eye(128, dtype=F32)[:S5_GROUPS], S5_STATE, axis=0)

    saved = []
    grads = {nme: [None] * depth for nme in WEIGHTS}
    xcur = x
    for l in range(depth):
        vec = lambda a: a[l].reshape(1, -1)
        gain = vec(inp['norm_mix_g']) + (gather_token[0, 0] if l == 0 else 0.0)
        h = rms_fwd(f"rms_mix{l}", xcur, gain, tm)
        got = gathered(4 * l, h)
        w_in = got[('w_in', l)]
        if l == 0:
            cv_w_dw, lru_w_conv, ffn_w_dw = (full_small(got[(nme, None)]) for nme in ('cv_w_dw', 'lru_w_conv', 'ffn_w_dw'))
        ncol = w_in.shape[2]

        lam_re, lam_im = vec(inp['s5_lam_re']), vec(inp['s5_lam_im'])
        log_step = jnp.broadcast_to(inp['s5_log_step'][l][:, None], (S5_GROUPS, S5_STATE)).reshape(1, n_state)
        b_re = _block_diag(inp['s5_b_re'][l].transpose(0, 2, 1))
        b_im = _block_diag(inp['s5_b_im'][l].transpose(0, 2, 1))
        c_cat = jnp.stack([_block_diag(inp['s5_c_re'][l].transpose(0, 2, 1)),
                           -_block_diag(inp['s5_c_im'][l].transpose(0, 2, 1))]).astype(BF16)
        a_bar, b_bar = s5_param_fwd(f"s5_param_fwd{l}", lam_re, lam_im, log_step, b_re, b_im)
        w_r = _block_diag(inp['lru_w_r'][l]).astype(BF16)
        w_i = _block_diag(inp['lru_w_i'][l]).astype(BF16)
        pool_bd = _block_diag(inp['pool_w'][l]).astype(BF16)
        gate_pars = [w_r, w_i, vec(inp['lru_b_r']), vec(inp['lru_b_i']), vec(inp['lru_lam'])]

        proj = _mm(f"proj{l}", h, w_in, jax.ShapeDtypeStruct((rows, 4 * ncol), F32), (4, rows // tm),
                   pl.BlockSpec((tm, d), lambda j, i: (i, 0)), pl.BlockSpec((None, d, ncol), lambda j, i: (j, 0, 0)),
                   pl.BlockSpec((tm, ncol), lambda j, i: (i, j)), NN)
        proj3 = proj.reshape(1, rows, 4 * ncol)
        ts = min(2048, rows)
        cw, sw = mix_w // 4, n_state // 4
        bu = _mm(f"s5_bu{l}", proj, b_bar, jax.ShapeDtypeStruct((2, rows, n_state), F32), (rows // ts, 2, 4),
                 pl.BlockSpec((ts, cw), lambda i, c, s: (i, s)), pl.BlockSpec((None, cw, sw), lambda i, c, s: (c, s, s)),
                 pl.BlockSpec((None, ts, sw), lambda i, c, s: (c, i, s)), NN)
        z = s5_scan_fwd(f"s5_scan{l}", bu, a_bar)
        y_ssm = _mm(f"s5_read{l}", z, c_cat, jax.ShapeDtypeStruct((rows, mix_w), F32), (rows // ts, 4, 2),
                    pl.BlockSpec((None, ts, sw), lambda i, s, c: (c, i, s)), pl.BlockSpec((None, sw, cw), lambda i, s, c: (c, s, s)),
                    pl.BlockSpec((ts, cw), lambda i, s, c: (i, s)), NN, k_axis=2)
        (h0,) = _rowwise(f"cv_glu{l}", _glu, [(proj, 1, mix_w), (proj, 2, mix_w)], [], 1, [(mix_w, F32)], tm)
        h1 = dwconv_fwd(f"cv_conv{l}", h0.reshape(1, rows, mix_w), 0, mix_w, cv_w_dw[l][None], vec(inp['cv_b_dw'])[None],
                        CV_TAPS, tc)[0]
        xc = dwconv_fwd(f"lru_conv{l}", proj3, 3, mix_w, lru_w_conv[l][None], vec(inp['lru_b_conv'])[None], LRU_TAPS, tc)[0]
        a_t, b_t = _rowwise(f"lru_gate{l}", _lru_gate, [(xc, 0, mix_w)], gate_pars, 2, [(mix_w, F32), (mix_w, F32)], tm)
        hseq = lru_scan_fwd(f"lru_scan{l}", a_t, b_t)
        dgp = pool_fwd(f"pool{l}", proj, 5, tc)
        got = gathered(4 * l + 1, proj)
        w_out = got[('w_out', l)].reshape(d, d)
        w_glu, w_pw = got[('s5_w_glu', l)].reshape(mix_w, mix_w), got[('cv_w_pw', l)].reshape(mix_w, mix_w)
        post_pars = [vec(inp['s5_d']), w_glu, vec(inp['s5_b_glu']), vec(inp['cv_ln_g']), vec(inp['cv_ln_b']), w_pw,
                     vec(inp['cv_b_pw']), pool_bd, vec(inp['pool_scale'])]
        post_rows = [(y_ssm, 0, mix_w), (proj, 0, mix_w), (h1, 0, mix_w), (hseq, 0, mix_w), (proj, 4, mix_w), (dgp, 0, mix_w)]
        (mixed,) = _rowwise(f"mix_post{l}", _mix_post, post_rows, post_pars, 1, [(d, BF16)], tm)
        td = min(1024, rows)
        x1 = _mm(f"out_proj{l}", mixed, w_out, jax.ShapeDtypeStruct((rows, d), F32), (2, rows // td),
                 pl.BlockSpec((td, d), lambda j, i: (i, 0)), pl.BlockSpec((d, d // 2), lambda j, i: (0, j)),
                 pl.BlockSpec((td, d // 2), lambda j, i: (i, j)), NN,
                 add=xcur, add_spec=pl.BlockSpec((td, d // 2), lambda j, i: (i, j)))

        h2 = rms_fwd(f"rms_ffn{l}", x1, vec(inp['norm_ffn_g']), tm)
        tu = min(512, rows)
        w_up = gathered(4 * l + 2, x1)[('ffn_w_up', l)]
        up = _mm(f"ffn_up{l}", h2, w_up, jax.ShapeDtypeStruct((4, rows, ffn_half), F32), (4, rows // tu),
                 pl.BlockSpec((tu, d), lambda k, i: (i, 0)), pl.BlockSpec((None, ffn_half, d), lambda k, i: (k, 0, 0)),
                 pl.BlockSpec((None, tu, ffn_half), lambda k, i: (k, i, 0)), NT)
        w_dw = ffn_w_dw[l].reshape(FFN_TAPS, 2, ffn_half).transpose(1, 0, 2)
        b_dw = inp['ffn_b_dw'][l].reshape(2, 1, ffn_half)
        act = ffn_gate_fwd(f"ffn_gate{l}", up, w_dw, b_dw, tc)
        w_down = gathered(4 * l + 3, up)[('ffn_w_down', l)].reshape(2, ffn_half, d)
        x2 = _mm(f"ffn_down{l}", act, w_down, jax.ShapeDtypeStruct((rows, d), F32), (rows // td, 4),
                 pl.BlockSpec((2, td, ffn_half), lambda i, j: (0, i, 0)), pl.BlockSpec((2, ffn_half, d // 4), lambda i, j: (0, 0, j)),
                 pl.BlockSpec((td, d // 4), lambda i, j: (i, j)), NN, inner=("lead", 2),
                 add=x1, add_spec=pl.BlockSpec((td, d // 4), lambda i, j: (i, j)))
        saved.append(dict(x=xcur, h=h, proj=proj, z=z, y_ssm=y_ssm, h0=h0, h1=h1, xc=xc, a_t=a_t, hseq=hseq, dgp=dgp,
                          mixed=mixed, x1=x1, h2=h2, up=up, act=act, w_in=w_in, w_out=w_out, w_up=w_up, w_down=w_down,
                          a_bar=a_bar, b_bar=b_bar, c_cat=c_cat, post_pars=post_pars, gate_pars=gate_pars, w_dw=w_dw, b_dw=b_dw,
                          s5=(lam_re, lam_im, log_step, b_re, b_im), cv_w=cv_w_dw[l][None], lru_w=lru_w_conv[l][None]))
        xcur = x2

    loss_row, dx, dx_op, dg_final = final_loss("final_loss", xcur, inp['norm_final_g'].reshape(1, d), target, tm)
    grads['norm_final_g'] = dg_final.reshape(d)

    big_g = {nme: [None] * depth for nme in BIG}
    reduce_groups = []

    def start_reduce(tag, keys):
        pieces = []
        for nme, lyr in keys:
            g = big_g[nme][lyr]
            if nme == 'ffn_w_down':
                g = g.reshape(2, 4, ffn_half // 2, d // 2)
            pieces.append(reduce_cores(f"reduce_cores_{nme}{lyr}", g.reshape(2, -1, g.shape[-1])).reshape(g.shape[1:]))
        landing = [lax.empty((3,) + p.shape[1:], p.dtype) for p in pieces]
        groups, token = exchange_start(f"reduce_start_{tag}", [list(zip(pieces, landing))], xy, True)
        reduce_groups.append((tag, keys, groups[0]))
        return token

    for l in reversed(range(depth)):
        s = saved[l]
        ncol = s['w_in'].shape[2]
        tu = min(512, rows)
        dact = _mm(f"d_act{l}", dx_op, s['w_down'], jax.ShapeDtypeStruct((2, rows, ffn_half), F32), (2, rows // tu),
                   pl.BlockSpec((tu, d), lambda k, i: (i, 0)), pl.BlockSpec((None, ffn_half, d), lambda k, i: (k, 0, 0)),
                   pl.BlockSpec((None, tu, ffn_half), lambda k, i: (k, i, 0)), NT)
        tn = d // 4
        tk = min(1024, rows)
        tkb = min(2048, rows)
        big_g['ffn_w_down'][l] = _mm(
            f"dw_down{l}", s['act'], dx_op, jax.ShapeDtypeStruct((2, 2, ffn_half, d // 2), F32), (2, 4, rows // tkb),
            pl.BlockSpec((None, tkb, ffn_half), lambda hh, n, k: (hh, k, 0)), pl.BlockSpec((tkb, tn), lambda hh, n, k: (k, n)),
            pl.BlockSpec((None, None, ffn_half, tn), lambda hh, n, k: (n // 2, hh, 0, n % 2)), TN, k_axis=2)
        dup, dw_dw, db_dw = ffn_gate_bwd(f"ffn_gate_bwd{l}", s['up'], dact, s['w_dw'], s['b_dw'], tc)
        grads['ffn_w_dw'][l] = dw_dw.transpose(1, 0, 2).reshape(FFN_TAPS, 2 * ffn_half)
        grads['ffn_b_dw'][l] = db_dw.reshape(2 * ffn_half)
        dup = dup.reshape(4, rows, ffn_half)
        tm2 = min(1024, rows)
        dh2 = _mm(f"d_h2{l}", dup, s['w_up'], jax.ShapeDtypeStruct((rows, d), F32), (rows // tm2, 2, 4),
                  pl.BlockSpec((None, tm2, ffn_half), lambda i, j, k: (k, i, 0)), pl.BlockSpec((None, ffn_half, d // 2), lambda i, j, k: (k, 0, j)),
                  pl.BlockSpec((tm2, d // 2), lambda i, j, k: (i, j)), NN, k_axis=2)
        tmm = d // 4
        big_g['ffn_w_up'][l] = _mm(
            f"dw_up{l}", dup, s['h2'], jax.ShapeDtypeStruct((2, 4, ffn_half, d // 2), F32), (4, 4, rows // tkb),
            pl.BlockSpec((None, tkb, ffn_half), lambda k4, n, k: (k4, k, 0)), pl.BlockSpec((tkb, tn), lambda k4, n, k: (k, n)),
            pl.BlockSpec((None, None, ffn_half, tn), lambda k4, n, k: (n // 2, k4, 0, n % 2)), TN, k_axis=2)
        token = start_reduce(f"ffn{l}", [('ffn_w_down', l), ('ffn_w_up', l)])
        dx1, dx1_op, dg = rms_bwd(f"rms_ffn_bwd{l}", s['x1'], inp['norm_ffn_g'][l].reshape(1, d) + token[0, 0], dh2, dx, tm)
        grads['norm_ffn_g'][l] = dg.reshape(d)
        dmixed = _mm(f"d_mixed{l}", dx1_op, s['w_out'], jax.ShapeDtypeStruct((rows, d), F32), (rows // tm2, 4),
                     pl.BlockSpec((tm2, d), lambda i, j: (i, 0)), pl.BlockSpec((d // 4, d), lambda i, j: (j, 0)),
                     pl.BlockSpec((tm2, d // 4), lambda i, j: (i, j)), NT)
        tq = mix_w // 2
        big_g['w_out'][l] = _mm(
            f"dw_out{l}", s['mixed'], dx1_op, jax.ShapeDtypeStruct((2, 4, tq, d), F32), (4, rows // tkb),
            pl.BlockSpec((tkb, 2 * tq), lambda t, k: (k, t)), pl.BlockSpec((tkb, d), lambda t, k: (k, 0)),
            pl.BlockSpec((2, None, tq, d), lambda t, k: (0, t, 0, 0)), TN, k_axis=1)
        post_rows = [(s['y_ssm'], 0, mix_w), (s['proj'], 0, mix_w), (s['h1'], 0, mix_w), (s['hseq'], 0, mix_w),
                     (s['proj'], 4, mix_w), (s['dgp'], 0, mix_w), (dmixed, 0, d)]
        res = _rowwise(f"mix_post_bwd{l}", _mix_post, post_rows, s['post_pars'], 1, [(mix_w, F32)] * 6, tm, with_grads=True)
        dy_ssm, du_dir, dh1, dhseq, dlru_g, ddgp = res[:6]
        dd, dwglu, dbglu, dlng, dlnb, dwpw, dbpw, dpoolbd, dscale = res[6:]
        grads['s5_d'][l], grads['s5_b_glu'][l] = dd.reshape(mix_w), dbglu.reshape(mix_w)
        grads['cv_ln_g'][l], grads['cv_ln_b'][l], grads['cv_b_pw'][l] = dlng.reshape(mix_w), dlnb.reshape(mix_w), dbpw.reshape(mix_w)
        grads['pool_w'][l] = _diag_blocks(dpoolbd, len(POOL_WINDOWS))
        grads['pool_scale'][l] = dscale.reshape(mix_w)
        big_g['s5_w_glu'][l] = _halves(dwglu, 4)
        big_g['cv_w_pw'][l] = _halves(dwpw, 4)
        ts = min(2048, rows)
        cw, sw = mix_w // 4, n_state // 4
        slab = jnp.arange(mix_w)[:, None] // cw == jnp.arange(n_state)[None, :] // sw
        dz = _mm(f"s5_dz{l}", dy_ssm, s['c_cat'], jax.ShapeDtypeStruct((2, rows, n_state), F32), (rows // ts, 2, 4),
                 pl.BlockSpec((ts, cw), lambda i, c, q: (i, q)), pl.BlockSpec((None, sw, cw), lambda i, c, q: (c, q, q)),
                 pl.BlockSpec((None, ts, sw), lambda i, c, q: (c, i, q)), NT)
        dccat = _mm(f"s5_dc{l}", s['z'], dy_ssm, jax.ShapeDtypeStruct((2, n_state, mix_w), F32), (2, 4, rows // tk),
                    pl.BlockSpec((None, tk, sw), lambda c, q, k: (c, k, q)), pl.BlockSpec((tk, cw), lambda c, q, k: (k, q)),
                    pl.BlockSpec((None, sw, cw), lambda c, q, k: (c, q, q)), TN, k_axis=2)
        dccat = jnp.where(slab.T, dccat, 0.0)
        grads['s5_c_re'][l] = _diag_blocks(dccat[0], S5_GROUPS).transpose(0, 2, 1)
        grads['s5_c_im'][l] = -_diag_blocks(dccat[1], S5_GROUPS).transpose(0, 2, 1)
        lam, da_bar = s5_scan_bwd(f"s5_scan_bwd{l}", dz, s['z'], s['a_bar'])
        du = _mm(f"s5_du{l}", lam, s['b_bar'], jax.ShapeDtypeStruct((rows, mix_w), F32), (rows // ts, 4, 2),
                 pl.BlockSpec((None, ts, sw), lambda i, q, c: (c, i, q)), pl.BlockSpec((None, cw, sw), lambda i, q, c: (c, q, q)),
                 pl.BlockSpec((ts, cw), lambda i, q, c: (i, q)), NT, k_axis=2,
                 add=du_dir, add_spec=pl.BlockSpec((ts, cw), lambda i, q, c: (i, q)))
        dbbar = _mm(f"s5_db{l}", s['proj'], lam, jax.ShapeDtypeStruct((2, mix_w, n_state), F32), (2, 4, rows // tk),
                    pl.BlockSpec((tk, cw), lambda c, q, k: (k, q)), pl.BlockSpec((None, tk, sw), lambda c, q, k: (c, k, q)),
                    pl.BlockSpec((None, cw, sw), lambda c, q, k: (c, q, q)), TN, k_axis=2)
        dbbar = jnp.where(slab, dbbar, 0.0)
        dlr, dli, dls, dbre, dbim = s5_param_bwd(f"s5_param_bwd{l}", *s['s5'], da_bar, dbbar, gsum)
        grads['s5_lam_re'][l] = dlr.reshape(S5_GROUPS, S5_STATE)
        grads['s5_lam_im'][l] = dli.reshape(S5_GROUPS, S5_STATE)
        grads['s5_log_step'][l] = dls[0, :S5_GROUPS]
        grads['s5_b_re'][l] = _diag_blocks(dbre, S5_GROUPS).transpose(0, 2, 1)
        grads['s5_b_im'][l] = _diag_blocks(dbim, S5_GROUPS).transpose(0, 2, 1)
        dh0, dw_cv, db_cv = dwconv_bwd(f"cv_conv_bwd{l}", dh1.reshape(1, rows, mix_w), s['h0'].reshape(1, rows, mix_w), 0, mix_w,
                                       s['cv_w'], CV_TAPS, tc)
        grads['cv_w_dw'][l], grads['cv_b_dw'][l] = dw_cv[0], db_cv.reshape(mix_w)
        dv, dgg = _rowwise(f"cv_glu_bwd{l}", _glu, [(s['proj'], 1, mix_w), (s['proj'], 2, mix_w), (dh0[0], 0, mix_w)], [], 1,
                           [(mix_w, F32)] * 2, tm, with_grads=True)
        da_t, db_t = lru_scan_bwd(f"lru_scan_bwd{l}", dhseq, s['a_t'], s['hseq'])
        res = _rowwise(f"lru_gate_bwd{l}", _lru_gate, [(s['xc'], 0, mix_w), (da_t, 0, mix_w), (db_t, 0, mix_w)], s['gate_pars'], 2,
                       [(mix_w, F32)], tm, with_grads=True)
        dxc, dwr, dwi, dbr, dbi, dlam = res
        grads['lru_w_r'][l], grads['lru_w_i'][l] = _diag_blocks(dwr, LRU_HEADS), _diag_blocks(dwi, LRU_HEADS)
        grads['lru_b_r'][l], grads['lru_b_i'][l], grads['lru_lam'][l] = dbr.reshape(mix_w), dbi.reshape(mix_w), dlam.reshape(mix_w)
        dlx, dw_lc, db_lc = dwconv_bwd(f"lru_conv_bwd{l}", dxc.reshape(1, rows, mix_w), s['proj'].reshape(1, rows, 4 * ncol), 3, mix_w,
                                       s['lru_w'], LRU_TAPS, tc)
        grads['lru_w_conv'][l], grads['lru_b_conv'][l] = dw_lc[0], db_lc.reshape(mix_w)
        dpx = pool_bwd(f"pool_bwd{l}", ddgp, tc)
        dproj = jnp.concatenate([du, dv, dgg, dlx[0], dlru_g, dpx], axis=-1)
        dh = _mm(f"d_h{l}", dproj, s['w_in'], jax.ShapeDtypeStruct((rows, d), F32), (rows // tm2, 4),
                 pl.BlockSpec((tm2, 4 * ncol), lambda i, j: (i, 0)), pl.BlockSpec((4, d // 4, ncol), lambda i, j: (0, j, 0)),
                 pl.BlockSpec((tm2, d // 4), lambda i, j: (i, j)), NT, inner=("cols", 4))
        tk2 = min(2048, rows)
        big_g['w_in'][l] = _mm(
            f"dw_in{l}", s['h'], dproj, jax.ShapeDtypeStruct((2, 4, d // 2, ncol), F32), (4, 2, rows // tk2),
            pl.BlockSpec((tk2, d // 2), lambda k4, m, k: (k, m)), pl.BlockSpec((tk2, ncol), lambda k4, m, k: (k, k4)),
            pl.BlockSpec((None, None, d // 2, ncol), lambda k4, m, k: (m, k4, 0, 0)), TN, k_axis=2)
        token = start_reduce(f"mix{l}", [('w_out', l), ('s5_w_glu', l), ('cv_w_pw', l), ('w_in', l)])
        dx, dx_op, dg = rms_bwd(f"rms_mix_bwd{l}", s['x'], inp['norm_mix_g'][l].reshape(1, d) + token[0, 0], dh, dx1, tm)
        grads['norm_mix_g'][l] = dg.reshape(d)

    small = [nme for nme in WEIGHTS if nme not in BIG]
    full_g = {nme: (grads[nme] if nme == 'norm_final_g' else jnp.stack(grads[nme])) for nme in small}
    flat = jnp.concatenate([full_g[nme].reshape(-1) for nme in small])
    packed = jnp.pad(flat, (0, (-flat.shape[0]) % (128 * 64))).reshape(-1, 128)
    chip_sum = pair_sum("small_pair_sum", packed)
    small_zone = place_tile("place_small_grads", chip_sum[None], 0, my_chip, dtype=F32)
    (small_group,), small_token = exchange_start("small_start", [[small_zone]], xy, False)

    t_full = {}
    for tag, keys, group in reduce_groups:
        pieces, parts = exchange_wait(f"reduce_wait_{tag}", group, small_token, xy, True)
        for key, own, got in zip(keys, pieces, parts):
            t_full[key] = sum_and_share(f"share_cores_{key[0]}{key[1]}", own, got, my_chip)

    outs, done_big = {}, []
    tiles = {'w_in': 256, 'w_out': 128, 'ffn_w_up': 128, 'ffn_w_down': 256, 's5_w_glu': 64, 'cv_w_pw': 64}
    for nme in BIG:
        g0, g1 = t_full[(nme, 0)], t_full[(nme, 1)]
        if nme == 'ffn_w_up':
            res = adamw_sharded(f"adamw_{nme}", *(jnp.swapaxes(inp[p + nme], 1, 2) for p in ('', 'm_', 'v_')), g0, g1, True, tiles[nme])
            outs[nme] = tuple(jnp.swapaxes(r, 1, 2) for r in res)
        else:
            res = adamw_sharded(f"adamw_{nme}", inp[nme], inp['m_' + nme], inp['v_' + nme], g0, g1, nme == 'ffn_w_down', tiles[nme])
            outs[nme] = res
        done_big.append(res[1][:1, :1, :1].reshape(1))

    after_big = sum(done_big)
    (g4,) = exchange_wait("small_wait", small_group, after_big, xy, False)[1]
    gsum_small = sum_lead("sum_small", g4, _row_tile(g4.shape[1], 1024)).reshape(-1)
    red, off = {}, 0
    for nme in small:
        g = gsum_small[off:off + full_g[nme].size].reshape(full_g[nme].shape)
        off += full_g[nme].size
        if nme in SMALL_SHARDED:
            width = inp[nme].shape[2]
            g = lax.dynamic_slice_in_dim(g, my_chip[0] * width, width, axis=2)
        red[nme] = g

    def pack(tree):
        f = jnp.concatenate([tree[nme].reshape(-1) for nme in small])
        return jnp.pad(f, (0, (-f.shape[0]) % (128 * 64))).reshape(-1, 128)

    pd, pm, pv = adamw_flat("adamw_small", pack({n_: inp[n_] for n_ in small}), pack(red), pack({n_: inp['m_' + n_] for n_ in small}),
                            pack({n_: inp['v_' + n_] for n_ in small}))
    off = 0
    for nme in small:
        size, shape = inp[nme].size, inp[nme].shape
        outs[nme] = (red[nme],) + tuple(p.reshape(-1)[off:off + size].reshape(shape) for p in (pd, pm, pv))
        off += size

    loss = lax.psum(loss_row[0, 0], ("x", "y", "c"))
    result = [loss, dx[None]]
    for part in range(4):
        result += [outs[nme][part] for nme in WEIGHTS]
    return tuple(result)


def kernel(x, norm_mix_g, w_in, s5_lam_re, s5_lam_im, s5_log_step, s5_b_re, s5_b_im, s5_c_re, s5_c_im, s5_d, s5_w_glu, s5_b_glu, cv_w_dw, cv_b_dw, cv_ln_g, cv_ln_b, cv_w_pw, cv_b_pw, lru_w_conv, lru_b_conv, lru_w_r, lru_b_r, lru_w_i, lru_b_i, lru_lam, pool_w, pool_scale, w_out, norm_ffn_g, ffn_w_up, ffn_w_dw, ffn_b_dw, ffn_w_down, norm_final_g, loss_target, m_norm_mix_g, m_w_in, m_s5_lam_re, m_s5_lam_im, m_s5_log_step, m_s5_b_re, m_s5_b_im, m_s5_c_re, m_s5_c_im, m_s5_d, m_s5_w_glu, m_s5_b_glu, m_cv_w_dw, m_cv_b_dw, m_cv_ln_g, m_cv_ln_b, m_cv_w_pw, m_cv_b_pw, m_lru_w_conv, m_lru_b_conv, m_lru_w_r, m_lru_b_r, m_lru_w_i, m_lru_b_i, m_lru_lam, m_pool_w, m_pool_scale, m_w_out, m_norm_ffn_g, m_ffn_w_up, m_ffn_w_dw, m_ffn_b_dw, m_ffn_w_down, m_norm_final_g, v_norm_mix_g, v_w_in, v_s5_lam_re, v_s5_lam_im, v_s5_log_step, v_s5_b_re, v_s5_b_im, v_s5_c_re, v_s5_c_im, v_s5_d, v_s5_w_glu, v_s5_b_glu, v_cv_w_dw, v_cv_b_dw, v_cv_ln_g, v_cv_ln_b, v_cv_w_pw, v_cv_b_pw, v_lru_w_conv, v_lru_b_conv, v_lru_w_r, v_lru_b_r, v_lru_w_i, v_lru_b_i, v_lru_lam, v_pool_w, v_pool_scale, v_w_out, v_norm_ffn_g, v_ffn_w_up, v_ffn_w_dw, v_ffn_b_dw, v_ffn_w_down, v_norm_final_g):
    inp = dict(locals())
    return _step(inp)
```

```python
import functools

import jax
import jax.numpy as jnp
from jax import lax
from jax.experimental import pallas as pl
from jax.experimental.pallas import tpu as pltpu

F32 = jnp.float32
BF16 = jnp.bfloat16

VMEM_LIMIT_BYTES = 56 * 1024 * 1024
SUBLANES = 8

EPS = 1e-6
S5_GROUPS, S5_STATE, S5_GROUP_CH = 32, 64, 16
LRU_HEADS, LRU_C = 8, 8.0
POOL_WINDOWS = (2, 4, 8, 16)
CV_TAPS, LRU_TAPS, FFN_TAPS = 31, 4, 3
SCAN_CHUNK = 64
GELU_K0, GELU_K1 = 0.7978845608028654, 0.044715

ADAM_LR, ADAM_B1, ADAM_B2, ADAM_EPS, ADAM_WD, ADAM_STEP = 0.001, 0.9, 0.999, 1e-08, 0.01, 10

NN = ((1,), (0,))
NT = ((1,), (1,))
TN = ((0,), (0,))

WEIGHTS = ['norm_mix_g', 'w_in', 's5_lam_re', 's5_lam_im', 's5_log_step', 's5_b_re', 's5_b_im', 's5_c_re', 's5_c_im',
           's5_d', 's5_w_glu', 's5_b_glu', 'cv_w_dw', 'cv_b_dw', 'cv_ln_g', 'cv_ln_b', 'cv_w_pw', 'cv_b_pw',
           'lru_w_conv', 'lru_b_conv', 'lru_w_r', 'lru_b_r', 'lru_w_i', 'lru_b_i', 'lru_lam', 'pool_w', 'pool_scale',
           'w_out', 'norm_ffn_g', 'ffn_w_up', 'ffn_w_dw', 'ffn_b_dw', 'ffn_w_down', 'norm_final_g']
BIG = ('w_in', 'w_out', 'ffn_w_up', 'ffn_w_down', 's5_w_glu', 'cv_w_pw')
SMALL_SHARDED = {'cv_w_dw': 2, 'lru_w_conv': 2, 'ffn_w_dw': 2}


def _params(sem=None):
    if sem is None:
        return pltpu.CompilerParams(vmem_limit_bytes=VMEM_LIMIT_BYTES)
    return pltpu.CompilerParams(dimension_semantics=sem, vmem_limit_bytes=VMEM_LIMIT_BYTES)


def _row_tile(rows, cap, mult=SUBLANES):
    best = mult
    for t in range(mult, min(rows, cap) + 1, mult):
        if rows % t == 0:
            best = t
    return best


def _bdot(a, b, dims=NN):
    return lax.dot_general(a.astype(BF16), b.astype(BF16), (dims, ((), ())), preferred_element_type=F32)


@jax.custom_vjp
def bdot(a, b):
    return _bdot(a, b)


def _bdot_fwd(a, b):
    return _bdot(a, b), (a, b)


def _bdot_bwd(res, g):
    a, b = res
    return _bdot(g, b, NT).astype(a.dtype), _bdot(a, g, TN).astype(b.dtype)


bdot.defvjp(_bdot_fwd, _bdot_bwd)


def _mm(name, a, b, out_sds, grid, a_spec, b_spec, o_spec, dims, k_axis=None, add=None, add_spec=None, inner=None):
    nk = grid[k_axis] if k_axis is not None else 1
    has_add = add is not None
    acc_shape = tuple(d for d in o_spec.block_shape if d is not None)
    acc_in_out = out_sds.dtype == F32

    def product(a_ref, b_ref):
        if inner is None:
            return _bdot(a_ref[...], b_ref[...], dims)
        kind, n = inner
        width = a_ref.shape[-1] // n
        acc = None
        for j in range(n):
            a_j = a_ref[j] if kind == "lead" else a_ref[:, j * width:(j + 1) * width]
            p = _bdot(a_j, b_ref[j], dims)
            acc = p if acc is None else acc + p
        return acc

    def body(*refs):
        a_ref, b_ref = refs[0], refs[1]
        add_ref = refs[2] if has_add else None
        o_ref = refs[3] if has_add else refs[2]
        prod = product(a_ref, b_ref).reshape(acc_shape)
        if k_axis is None:
            if has_add:
                prod = prod + add_ref[...]
            o_ref[...] = prod.astype(o_ref.dtype)
        else:
            acc_ref = o_ref if acc_in_out else refs[-1]
            k = pl.program_id(k_axis)

            @pl.when(k == 0)
            def _():
                acc_ref[...] = prod

            @pl.when(k > 0)
            def _():
                acc_ref[...] += prod

            if has_add or not acc_in_out:
                @pl.when(k == nk - 1)
                def _():
                    r = acc_ref[...]
                    if has_add:
                        r = r + add_ref[...]
                    o_ref[...] = r.astype(o_ref.dtype)

    sem = tuple("arbitrary" if d == k_axis else "parallel" for d in range(len(grid)))
    in_specs = [a_spec, b_spec] + ([add_spec] if has_add else [])
    args = (a, b) + ((add,) if has_add else ())
    scratch = [pltpu.VMEM(acc_shape, F32)] if (k_axis is not None and not acc_in_out) else []
    return pl.pallas_call(body, out_shape=out_sds, grid=grid, in_specs=in_specs, out_specs=o_spec,
                          scratch_shapes=scratch, compiler_params=_params(sem), name=name)(*args)


def _rms(x, g):
    return x * lax.rsqrt(jnp.mean(x * x, axis=-1, keepdims=True) + EPS) * g


def rms_fwd(name, x, g, tm):
    rows, d = x.shape

    def body(x_ref, g_ref, o_ref):
        o_ref[...] = _rms(x_ref[...], g_ref[...]).astype(BF16)

    return pl.pallas_call(
        body, out_shape=jax.ShapeDtypeStruct((rows, d), BF16), grid=(rows // tm,),
        in_specs=[pl.BlockSpec((tm, d), lambda i: (i, 0)), pl.BlockSpec((1, d), lambda i: (0, 0))],
        out_specs=pl.BlockSpec((tm, d), lambda i: (i, 0)), compiler_params=_params(("parallel",)), name=name)(x, g)


def rms_bwd(name, x, g, dh, dres, tm):
    rows, d = x.shape

    def tile(g_ref, dg_ref, x_ref, dh_ref, dres_ref, dx_ref, dxb_ref):
        xv, dy = x_ref[...], dh_ref[...]
        r = lax.rsqrt(jnp.mean(xv * xv, axis=-1, keepdims=True) + EPS)
        dyg = dy * g_ref[...]
        s = jnp.mean(dyg * xv, axis=-1, keepdims=True)
        dx = r * dyg - xv * (r * r * r * s) + dres_ref[...]
        dx_ref[...] = dx
        dxb_ref[...] = dx.astype(BF16)
        dg_ref[...] += jnp.sum(dy * xv * r, axis=0, keepdims=True)

    def body(x_hbm, g_ref, dh_hbm, dres_hbm, dx_hbm, dxb_hbm, dg_ref):
        tr = tm // 2
        row = pl.BlockSpec((tr, d), lambda i: (i, 0))
        row_in = pl.BlockSpec((tr, d), lambda i: (i, 0), pipeline_mode=pl.Buffered(3))
        dg_ref[...] = jnp.zeros_like(dg_ref)
        pltpu.emit_pipeline(functools.partial(tile, g_ref, dg_ref), grid=(rows // tr,), in_specs=[row_in] * 3,
                            out_specs=[row, row])(x_hbm, dh_hbm, dres_hbm, dx_hbm, dxb_hbm)

    whole = pl.BlockSpec(memory_space=pltpu.VMEM)
    return pl.pallas_call(
        body, out_shape=(jax.ShapeDtypeStruct((rows, d), F32), jax.ShapeDtypeStruct((rows, d), BF16), jax.ShapeDtypeStruct((1, d), F32)),
        in_specs=[_HBM, whole, _HBM, _HBM], out_specs=(_HBM, _HBM, whole),
        compiler_params=_params(None), name=name)(x, g, dh, dres)


def final_loss(name, x, g, target, tm):
    rows, d = x.shape

    def body(x_ref, g_ref, t_ref, l_ref, dx_ref, dxb_ref, dg_ref):
        def f(xv, gv):
            e = _rms(xv, gv) - t_ref[...]
            return 0.5 * jnp.sum(jnp.mean(e * e, axis=-1))

        loss, (dx, dg) = jax.value_and_grad(f, argnums=(0, 1))(x_ref[...], g_ref[...])
        dx_ref[...] = dx
        dxb_ref[...] = dx.astype(BF16)

        @pl.when(pl.program_id(0) == 0)
        def _():
            l_ref[...] = jnp.zeros_like(l_ref)
            dg_ref[...] = jnp.zeros_like(dg_ref)

        l_ref[...] += jnp.full(l_ref.shape, loss, F32)
        dg_ref[...] += dg

    row = pl.BlockSpec((tm, d), lambda i: (i, 0))
    vec = pl.BlockSpec((1, d), lambda i: (0, 0))
    lspec = pl.BlockSpec((1, 128), lambda i: (0, 0))
    return pl.pallas_call(
        body, out_shape=(jax.ShapeDtypeStruct((1, 128), F32), jax.ShapeDtypeStruct((rows, d), F32), jax.ShapeDtypeStruct((rows, d), BF16),
                         jax.ShapeDtypeStruct((1, d), F32)),
        grid=(rows // tm,), in_specs=[row, vec, row], out_specs=(lspec, row, row, vec),
        compiler_params=_params(("arbitrary",)), name=name)(x, g, target)


def _rowwise(name, fn, row_ins, par_ins, n_row_out, row_out_dtypes, tm, with_grads=False):
    rows = row_ins[0][0].shape[0]
    n_prim = len(row_ins) - (n_row_out if with_grads else 0)
    n_par = len(par_ins)

    def body(*refs):
        ins = [r[...] for r in refs[:len(row_ins) + n_par]]
        outs = refs[len(row_ins) + n_par:]
        prim, cts, pars = ins[:n_prim], ins[n_prim:len(row_ins)], ins[len(row_ins):]
        if not with_grads:
            res = fn(*prim, *pars)
            for o_ref, r in zip(outs, res):
                o_ref[...] = r.astype(o_ref.dtype)
            return
        _, vjp = jax.vjp(fn, *prim, *[p.astype(F32) for p in pars])
        grads = vjp(tuple(cts))
        for o_ref, gr in zip(outs[:n_prim], grads[:n_prim]):
            o_ref[...] = gr.astype(o_ref.dtype)

        @pl.when(pl.program_id(0) == 0)
        def _():
            for o_ref in outs[n_prim:]:
                o_ref[...] = jnp.zeros_like(o_ref)

        for o_ref, gr in zip(outs[n_prim:], grads[n_prim:]):
            o_ref[...] += gr.astype(F32)

    in_specs = [pl.BlockSpec((tm, w), (lambda i, c=c: (i, c))) for (_, c, w) in row_ins]
    in_specs += [pl.BlockSpec(p.shape, (lambda i, n=p.ndim: (0,) * n)) for p in par_ins]
    args = [a for (a, _, _) in row_ins] + list(par_ins)
    if not with_grads:
        out_shape = tuple(jax.ShapeDtypeStruct((rows, w), dt) for (w, dt) in row_out_dtypes)
        out_specs = tuple(pl.BlockSpec((tm, w), lambda i: (i, 0)) for (w, _) in row_out_dtypes)
        sem = ("parallel",)
    else:
        out_shape = tuple(jax.ShapeDtypeStruct((rows, w), dt) for (w, dt) in row_out_dtypes)
        out_shape += tuple(jax.ShapeDtypeStruct(p.shape, F32) for p in par_ins)
        out_specs = tuple(pl.BlockSpec((tm, w), lambda i: (i, 0)) for (w, _) in row_out_dtypes)
        out_specs += tuple(pl.BlockSpec(p.shape, (lambda i, n=p.ndim: (0,) * n)) for p in par_ins)
        sem = ("arbitrary",)
    return pl.pallas_call(body, out_shape=out_shape, grid=(rows // tm,), in_specs=in_specs, out_specs=out_specs,
                          compiler_params=_params(sem), name=name)(*args)


def _glu(v, g):
    return (v * jax.nn.sigmoid(g),)


def _neg_expm1(z):
    return -jnp.tanh(0.5 * z) * (jnp.exp(z) + 1.0)


def _lru_gate(xc, w_r, w_i, b_r, b_i, lam):
    r = jax.nn.sigmoid(bdot(xc, w_r) + b_r)
    i = jax.nn.sigmoid(bdot(xc, w_i) + b_i)
    log_a = -LRU_C * r * jax.nn.softplus(-lam)
    a = jnp.exp(log_a)
    mult = jnp.sqrt(_neg_expm1(2.0 * log_a))
    return a, mult * (i * xc)


def _layernorm(x, g, b):
    mu = jnp.mean(x, axis=-1, keepdims=True)
    var = jnp.mean(jnp.square(x - mu), axis=-1, keepdims=True)
    return (x - mu) * lax.rsqrt(var + EPS) * g + b


def _mix_post(y_ssm, u, h1, hseq, lru_g, dgp, s5_d, w_glu, b_glu, ln_g, ln_b, w_pw, b_pw, pool_bd, pool_scale):
    y = y_ssm + s5_d * u
    gl = jax.nn.gelu(y, approximate=True)
    out_s5 = gl * jax.nn.sigmoid(bdot(gl, w_glu) + b_glu)
    out_cv = bdot(jax.nn.silu(_layernorm(h1, ln_g, ln_b)), w_pw) + b_pw
    out_lru = hseq * jax.nn.gelu(lru_g, approximate=True)
    out_pool = bdot(dgp, pool_bd) * pool_scale
    return (jnp.concatenate([out_s5, out_cv, out_lru, out_pool], axis=-1),)


def _gelu_terms(x):
    sq = x * x
    t = jnp.tanh(x * (GELU_K0 + (GELU_K0 * GELU_K1) * sq))
    return sq, t, 0.5 + 0.5 * t


def _halo_rows(taps):
    return -(-(taps - 1) // SUBLANES) * SUBLANES


def _row_windows(ext, offsets, n, shifted_ref=None):
    if shifted_ref is None:
        return {off: ext[off:off + n] for off in offsets}
    room = ext.shape[0] - SUBLANES
    slots, out = {}, {}
    for off in offsets:
        r = off % SUBLANES
        if r == 0:
            out[off] = ext[off:off + n]
            continue
        if r not in slots:
            slots[r] = len(slots)
            shifted_ref[slots[r]] = ext[r:r + room]
        out[off] = shifted_ref[slots[r], off - r:off - r + n, :]
    return out


def _shift_scratch(taps, tm, c):
    return [pltpu.VMEM((SUBLANES - 1, _halo_rows(taps) + tm - SUBLANES, c), F32)] if taps > SUBLANES else []


def dwconv_fwd(name, x, cblk, c, w, b, taps, tm, out_dtype=F32):
    nb = w.shape[0]
    rows = x.shape[1]
    halo = _halo_rows(taps)
    per = tm // halo

    def body(x_ref, h_ref, w_ref, b_ref, o_ref, *shifted):
        i = pl.program_id(1)
        prev = jnp.where(i > 0, h_ref[...], 0.0)
        ext = jnp.concatenate([prev, x_ref[...]], axis=0)
        win = _row_windows(ext, [halo - (taps - 1) + k for k in range(taps)], tm, *shifted)
        acc = jnp.broadcast_to(b_ref[...], (tm, c))
        for k in range(taps):
            acc = acc + w_ref[k:k + 1, :] * win[halo - (taps - 1) + k]
        o_ref[...] = acc.astype(o_ref.dtype)

    return pl.pallas_call(
        body, out_shape=jax.ShapeDtypeStruct((nb, rows, c), out_dtype), grid=(nb, rows // tm),
        in_specs=[pl.BlockSpec((None, tm, c), lambda n, i: (n, i, cblk)),
                  pl.BlockSpec((None, halo, c), lambda n, i: (n, jnp.maximum(i * per - 1, 0), cblk)),
                  pl.BlockSpec((None, taps, c), lambda n, i: (n, 0, 0)),
                  pl.BlockSpec((None, 1, c), lambda n, i: (n, 0, 0))],
        out_specs=pl.BlockSpec((None, tm, c), lambda n, i: (n, i, 0)), scratch_shapes=_shift_scratch(taps, tm, c),
        compiler_params=_params(("parallel", "parallel")), name=name)(x, x, w, b)


def dwconv_bwd(name, dy, x, cblk, c, w, taps, tm, dx_dtype=F32):
    nb = w.shape[0]
    rows = x.shape[1]
    halo = _halo_rows(taps)
    per = tm // halo
    n_tiles = rows // tm
    last_halo = rows // halo - 1

    def body(dy_ref, dn_ref, x_ref, xp_ref, w_ref, dx_ref, dw_ref, db_ref, *shifted):
        i = pl.program_id(1)
        dyv = dy_ref[...]
        nxt = jnp.where(i < n_tiles - 1, dn_ref[...], 0.0)
        dext = jnp.concatenate([dyv, nxt], axis=0)
        prev = jnp.where(i > 0, xp_ref[...], 0.0)
        xext = jnp.concatenate([prev, x_ref[...]], axis=0)
        acc = jnp.zeros((tm, c), F32)

        @pl.when(i == 0)
        def _():
            dw_ref[...] = jnp.zeros_like(dw_ref)
            db_ref[...] = jnp.zeros_like(db_ref)

        dwin = _row_windows(dext, list(range(taps)), tm, *shifted[:1])
        xwin = _row_windows(xext, [halo - (taps - 1) + k for k in range(taps)], tm, *shifted[1:])
        for k in range(taps):
            acc = acc + w_ref[k:k + 1, :] * dwin[taps - 1 - k]
            dw_ref[k:k + 1, :] += jnp.sum(dyv * xwin[halo - (taps - 1) + k], axis=0, keepdims=True)
        dx_ref[...] = acc.astype(dx_ref.dtype)
        db_ref[...] += jnp.sum(dyv, axis=0, keepdims=True)

    return pl.pallas_call(
        body, out_shape=(jax.ShapeDtypeStruct((nb, rows, c), dx_dtype), jax.ShapeDtypeStruct((nb, taps, c), F32),
                         jax.ShapeDtypeStruct((nb, 1, c), F32)),
        grid=(nb, n_tiles),
        in_specs=[pl.BlockSpec((None, tm, c), lambda n, i: (n, i, 0)),
                  pl.BlockSpec((None, halo, c), lambda n, i: (n, jnp.minimum((i + 1) * per, last_halo), 0)),
                  pl.BlockSpec((None, tm, c), lambda n, i: (n, i, cblk)),
                  pl.BlockSpec((None, halo, c), lambda n, i: (n, jnp.maximum(i * per - 1, 0), cblk)),
                  pl.BlockSpec((None, taps, c), lambda n, i: (n, 0, 0))],
        out_specs=(pl.BlockSpec((None, tm, c), lambda n, i: (n, i, 0)), pl.BlockSpec((None, taps, c), lambda n, i: (n, 0, 0)),
                   pl.BlockSpec((None, 1, c), lambda n, i: (n, 0, 0))),
        scratch_shapes=2 * _shift_scratch(taps, tm, c),
        compiler_params=_params(("parallel", "arbitrary")), name=name)(dy, dy, x, x, w)


def ffn_gate_fwd(name, up, w, b, tm):
    _, rows, c = up.shape
    halo = _halo_rows(FFN_TAPS)
    per = tm // halo

    def body(g_ref, gp_ref, v_ref, w_ref, b_ref, o_ref):
        i = pl.program_id(1)
        ext = jnp.concatenate([jnp.where(i > 0, gp_ref[...], 0.0), g_ref[...]], axis=0)
        gc = jnp.broadcast_to(b_ref[...], (tm, c))
        for k in range(FFN_TAPS):
            off = halo - (FFN_TAPS - 1) + k
            gc = gc + w_ref[k:k + 1, :] * ext[off:off + tm]
        o_ref[...] = (gc * _gelu_terms(gc)[2] * v_ref[...]).astype(BF16)

    return pl.pallas_call(
        body, out_shape=jax.ShapeDtypeStruct((2, rows, c), BF16), grid=(2, rows // tm),
        in_specs=[pl.BlockSpec((None, tm, c), lambda h, i: (h, i, 0)),
                  pl.BlockSpec((None, halo, c), lambda h, i: (h, jnp.maximum(i * per - 1, 0), 0)),
                  pl.BlockSpec((None, tm, c), lambda h, i: (h + 2, i, 0)),
                  pl.BlockSpec((None, FFN_TAPS, c), lambda h, i: (h, 0, 0)), pl.BlockSpec((None, 1, c), lambda h, i: (h, 0, 0))],
        out_specs=pl.BlockSpec((None, tm, c), lambda h, i: (h, i, 0)),
        compiler_params=_params(("parallel", "parallel")), name=name)(up, up, up, w, b)


def ffn_gate_bwd(name, up, dact, w, b, tm):
    _, rows, c = up.shape
    halo = _halo_rows(FFN_TAPS)
    per = tm // halo
    n_tiles = rows // tm
    last_halo = rows // halo - 1
    n_ext = tm + halo

    def body(g_ref, gp_ref, gn_ref, v_ref, vn_ref, d_ref, dn_ref, w_ref, b_ref, dup_ref, dw_ref, db_ref):
        i = pl.program_id(1)
        gext = jnp.concatenate([jnp.where(i > 0, gp_ref[...], 0.0), g_ref[...], gn_ref[...]], axis=0)
        shifted = [gext[halo - (FFN_TAPS - 1) + k:halo - (FFN_TAPS - 1) + k + n_ext] for k in range(FFN_TAPS)]
        gc = jnp.broadcast_to(b_ref[...], (n_ext, c))
        for k in range(FFN_TAPS):
            gc = gc + w_ref[k:k + 1, :] * shifted[k]
        vext = jnp.concatenate([v_ref[...], vn_ref[...]], axis=0)
        dext = jnp.concatenate([d_ref[...], dn_ref[...]], axis=0)
        sq, t, half = _gelu_terms(gc)
        dval = dext * (gc * half)
        dgc = (dext * vext) * (half + (0.5 * gc) * (1.0 - t * t) * (GELU_K0 + (3.0 * GELU_K0 * GELU_K1) * sq))
        r = lax.broadcasted_iota(jnp.int32, (n_ext, c), 0)
        dgc = jnp.where((r < tm) | (i < n_tiles - 1), dgc, 0.0)
        dgate = jnp.zeros((tm, c), F32)
        for k in range(FFN_TAPS):
            dgate = dgate + w_ref[k:k + 1, :] * dgc[FFN_TAPS - 1 - k:FFN_TAPS - 1 - k + tm]
        dup_ref[0] = dgate.astype(BF16)
        dup_ref[1] = dval[:tm].astype(BF16)

        @pl.when(i == 0)
        def _():
            dw_ref[...] = jnp.zeros_like(dw_ref)
            db_ref[...] = jnp.zeros_like(db_ref)

        dgc_t = dgc[:tm]
        for k in range(FFN_TAPS):
            dw_ref[k:k + 1, :] += jnp.sum(dgc_t * shifted[k][:tm], axis=0, keepdims=True)
        db_ref[...] += jnp.sum(dgc_t, axis=0, keepdims=True)

    def tile(shift):
        return pl.BlockSpec((None, tm, c), lambda h, i: (h + shift, i, 0))

    def after(shift):
        return pl.BlockSpec((None, halo, c), lambda h, i: (h + shift, jnp.minimum((i + 1) * per, last_halo), 0))

    return pl.pallas_call(
        body, out_shape=(jax.ShapeDtypeStruct((2, 2, rows, c), BF16), jax.ShapeDtypeStruct((2, FFN_TAPS, c), F32),
                         jax.ShapeDtypeStruct((2, 1, c), F32)),
        grid=(2, n_tiles),
        in_specs=[tile(0), pl.BlockSpec((None, halo, c), lambda h, i: (h, jnp.maximum(i * per - 1, 0), 0)), after(0),
                  tile(2), after(2), tile(0), after(0),
                  pl.BlockSpec((None, FFN_TAPS, c), lambda h, i: (h, 0, 0)), pl.BlockSpec((None, 1, c), lambda h, i: (h, 0, 0))],
        out_specs=(pl.BlockSpec((2, None, tm, c), lambda h, i: (0, h, i, 0)), pl.BlockSpec((None, FFN_TAPS, c), lambda h, i: (h, 0, 0)),
                   pl.BlockSpec((None, 1, c), lambda h, i: (h, 0, 0))),
        compiler_params=_params(("parallel", "arbitrary")), name=name)(up, up, up, up, up, dact, dact, w, b)


POOL_HALO = 16


def pool_fwd(name, proj, cblk, tm):
    rows = proj.shape[0]
    c = 128 * len(POOL_WINDOWS)
    per = tm // POOL_HALO

    def body(x_ref, h_ref, o_ref):
        i = pl.program_id(0)
        xv = x_ref[...]
        ext = jnp.concatenate([jnp.where(i > 0, h_ref[...], 0.0), xv], axis=0)
        t1 = (lax.broadcasted_iota(jnp.int32, (tm, 128), 0) + i * tm + 1).astype(F32)
        outs = []
        for gi, win in enumerate(POOL_WINDOWS):
            seg = ext[:, gi * 128:(gi + 1) * 128]
            s = seg[POOL_HALO:POOL_HALO + tm]
            for j in range(1, win):
                s = s + seg[POOL_HALO - j:POOL_HALO - j + tm]
            outs.append(s / jnp.minimum(t1, float(win)) - xv[:, gi * 128:(gi + 1) * 128])
        o_ref[...] = jnp.concatenate(outs, axis=-1)

    return pl.pallas_call(
        body, out_shape=jax.ShapeDtypeStruct((rows, c), F32), grid=(rows // tm,),
        in_specs=[pl.BlockSpec((tm, c), lambda i: (i, cblk)),
                  pl.BlockSpec((POOL_HALO, c), lambda i: (jnp.maximum(i * per - 1, 0), cblk))],
        out_specs=pl.BlockSpec((tm, c), lambda i: (i, 0)), compiler_params=_params(("parallel",)), name=name)(proj, proj)


def pool_bwd(name, dd, tm):
    rows, c = dd.shape
    per = tm // POOL_HALO
    n_tiles = rows // tm
    last_halo = rows // POOL_HALO - 1

    def body(d_ref, n_ref, o_ref):
        i = pl.program_id(0)
        dv = d_ref[...]
        nxt = jnp.where(i < n_tiles - 1, n_ref[...], 0.0)
        t1 = (lax.broadcasted_iota(jnp.int32, (tm, 128), 0) + i * tm + 1).astype(F32)
        t1n = (lax.broadcasted_iota(jnp.int32, (POOL_HALO, 128), 0) + (i + 1) * tm + 1).astype(F32)
        outs = []
        for gi, win in enumerate(POOL_WINDOWS):
            sl = slice(gi * 128, (gi + 1) * 128)
            q = jnp.concatenate([dv[:, sl] / jnp.minimum(t1, float(win)), nxt[:, sl] / jnp.minimum(t1n, float(win))], axis=0)
            s = q[0:tm]
            for j in range(1, win):
                s = s + q[j:j + tm]
            outs.append(s - dv[:, sl])
        o_ref[...] = jnp.concatenate(outs, axis=-1)

    return pl.pallas_call(
        body, out_shape=jax.ShapeDtypeStruct((rows, c), F32), grid=(n_tiles,),
        in_specs=[pl.BlockSpec((tm, c), lambda i: (i, 0)),
                  pl.BlockSpec((POOL_HALO, c), lambda i: (jnp.minimum((i + 1) * per, last_halo), 0))],
        out_specs=pl.BlockSpec((tm, c), lambda i: (i, 0)), compiler_params=_params(("parallel",)), name=name)(dd, dd)


BLOCK_STEPS = 3


def _shift_down(v, s, fill):
    r = lax.broadcasted_iota(jnp.int32, v.shape, 0)
    return jnp.where(r >= s, pltpu.roll(v, s, 0), fill)


def _shift_up(v, s, fill):
    n = v.shape[0]
    r = lax.broadcasted_iota(jnp.int32, v.shape, 0)
    return jnp.where(r < n - s, pltpu.roll(v, n - s, 0), fill)


def _shift_in_blocks(v, s, fill, reverse):
    n = v.shape[0]
    q = lax.broadcasted_iota(jnp.int32, v.shape, 0) & (SUBLANES - 1)
    if reverse:
        return jnp.where(q < SUBLANES - s, pltpu.roll(v, n - s, 0), fill)
    return jnp.where(q >= s, pltpu.roll(v, s, 0), fill)


def _cscan_blocks(vr, vi, powers, reverse):
    for k, (qr, qi) in enumerate(powers):
        s = 1 << k
        sr, si = _shift_in_blocks(vr, s, 0.0, reverse), _shift_in_blocks(vi, s, 0.0, reverse)
        if reverse:
            vr, vi = vr + qr * sr + qi * si, vi + qr * si - qi * sr
        else:
            vr, vi = vr + qr * sr - qi * si, vi + qr * si + qi * sr
    return vr, vi


def _cscan_table(pr, pi, powers, reverse):
    r = lax.broadcasted_iota(jnp.int32, (SUBLANES, 128), 0)
    at = (SUBLANES - 1) if reverse else 0
    return _cscan_blocks(jnp.where(r == at, pr, 0.0), jnp.where(r == at, -pi if reverse else pi, 0.0), powers, reverse)


def _cscan_chunk(vr, vi, powers, table, carry, reverse):
    vr, vi = _cscan_blocks(vr, vi, powers, reverse)
    tr, ti = table
    cr, ci = carry
    nb = vr.shape[0] // SUBLANES
    outr, outi = [None] * nb, [None] * nb
    edge = 0 if reverse else SUBLANES - 1
    for j in (reversed(range(nb)) if reverse else range(nb)):
        rows = slice(j * SUBLANES, (j + 1) * SUBLANES)
        zr = vr[rows] + tr * cr - ti * ci
        zi = vi[rows] + tr * ci + ti * cr
        outr[j], outi[j] = zr, zi
        cr, ci = zr[edge:edge + 1], zi[edge:edge + 1]
    return jnp.concatenate(outr, axis=0), jnp.concatenate(outi, axis=0), (cr, ci)


def _powers(pr, pi, n):
    out = [(pr, pi)]
    for _ in range(n - 1):
        pr, pi = pr * pr - pi * pi, 2.0 * pr * pi
        out.append((pr, pi))
    return out


def s5_scan_fwd(name, bu, a):
    _, rows, n = bu.shape
    t = min(SCAN_CHUNK, rows)

    def body(bu_ref, a_ref, z_ref):
        pr, pi = a_ref[0], a_ref[1]
        powers = _powers(pr, pi, BLOCK_STEPS)
        table = _cscan_table(pr, pi, powers, False)

        def chunk(ci, carry):
            base = pl.multiple_of(ci * t, t)
            zr, zi, carry = _cscan_chunk(bu_ref[0, pl.ds(base, t), :], bu_ref[1, pl.ds(base, t), :], powers, table, carry, False)
            z_ref[0, pl.ds(base, t), :] = zr
            z_ref[1, pl.ds(base, t), :] = zi
            return carry

        zero = jnp.zeros((1, 128), F32)
        lax.fori_loop(0, rows // t, chunk, (zero, zero))

    return pl.pallas_call(
        body, out_shape=jax.ShapeDtypeStruct((2, rows, n), F32), grid=(n // 128,),
        in_specs=[pl.BlockSpec((2, rows, 128), lambda j: (0, 0, j)), pl.BlockSpec((2, 1, 128), lambda j: (0, 0, j))],
        out_specs=pl.BlockSpec((2, rows, 128), lambda j: (0, 0, j)), compiler_params=_params(("parallel",)), name=name)(bu, a)


def s5_scan_bwd(name, dz, z, a):
    _, rows, n = dz.shape
    t = min(SCAN_CHUNK, rows)
    n_chunks = rows // t

    def body(dz_ref, z_ref, a_ref, lam_ref, da_ref):
        pr, pi = a_ref[0], a_ref[1]
        powers = _powers(pr, pi, BLOCK_STEPS)
        table = _cscan_table(pr, pi, powers, True)

        def chunk(k, carry):
            ci = n_chunks - 1 - k
            base = pl.multiple_of(ci * t, t)
            cr, cim, dar, dai = carry
            lr, li, (cr, cim) = _cscan_chunk(dz_ref[0, pl.ds(base, t), :], dz_ref[1, pl.ds(base, t), :], powers, table, (cr, cim), True)
            lam_ref[0, pl.ds(base, t), :] = lr
            lam_ref[1, pl.ds(base, t), :] = li
            pbase = pl.multiple_of(jnp.maximum(base - SUBLANES, 0), SUBLANES)
            keep = (ci > 0).astype(F32)
            pzr = z_ref[0, pl.ds(pbase, SUBLANES), :][SUBLANES - 1:SUBLANES, :] * keep
            pzi = z_ref[1, pl.ds(pbase, SUBLANES), :][SUBLANES - 1:SUBLANES, :] * keep
            zpr = _shift_down(z_ref[0, pl.ds(base, t), :], 1, pzr)
            zpi = _shift_down(z_ref[1, pl.ds(base, t), :], 1, pzi)
            dar = dar + jnp.sum(lr * zpr + li * zpi, axis=0, keepdims=True)
            dai = dai + jnp.sum(li * zpr - lr * zpi, axis=0, keepdims=True)
            return cr, cim, dar, dai

        zero = jnp.zeros((1, 128), F32)
        _, _, dar, dai = lax.fori_loop(0, n_chunks, chunk, (zero, zero, zero, zero))
        da_ref[0] = dar
        da_ref[1] = dai

    seq = pl.BlockSpec((2, rows, 128), lambda j: (0, 0, j))
    vec = pl.BlockSpec((2, 1, 128), lambda j: (0, 0, j))
    return pl.pallas_call(
        body, out_shape=(jax.ShapeDtypeStruct((2, rows, n), F32), jax.ShapeDtypeStruct((2, 1, n), F32)), grid=(n // 128,),
        in_specs=[seq, seq, vec], out_specs=(seq, vec), compiler_params=_params(("parallel",)), name=name)(dz, z, a)


def _rscan_chunk(a, b, carry, reverse):
    n = a.shape[0]
    shift = _shift_up if reverse else _shift_down
    for k in range(n.bit_length() - 1):
        s = 1 << k
        b = b + a * shift(b, s, 0.0)
        a = a * shift(a, s, 1.0)
    h = b + a * carry
    edge = 0 if reverse else n - 1
    return h, h[edge:edge + 1]


def lru_scan_fwd(name, a, b):
    rows, n = a.shape
    t = min(SCAN_CHUNK, rows)

    def body(a_ref, b_ref, h_ref):
        def chunk(ci, carry):
            base = pl.multiple_of(ci * t, t)
            h, carry = _rscan_chunk(a_ref[pl.ds(base, t), :], b_ref[pl.ds(base, t), :], carry, False)
            h_ref[pl.ds(base, t), :] = h
            return carry

        lax.fori_loop(0, rows // t, chunk, jnp.zeros((1, 128), F32))

    seq = pl.BlockSpec((rows, 128), lambda j: (0, j))
    return pl.pallas_call(body, out_shape=jax.ShapeDtypeStruct((rows, n), F32), grid=(n // 128,), in_specs=[seq, seq],
                          out_specs=seq, compiler_params=_params(("parallel",)), name=name)(a, b)


def lru_scan_bwd(name, dh, a, h):
    rows, n = a.shape
    t = min(SCAN_CHUNK, rows)
    n_chunks = rows // t

    def body(dh_ref, a_ref, h_ref, da_ref, db_ref):
        def chunk(k, carry):
            ci = n_chunks - 1 - k
            base = pl.multiple_of(ci * t, t)
            nbase = pl.multiple_of(jnp.minimum(base + t, rows - SUBLANES), SUBLANES)
            a_next = a_ref[pl.ds(nbase, SUBLANES), :][0:1, :]
            an = _shift_up(a_ref[pl.ds(base, t), :], 1, a_next)
            mu, carry = _rscan_chunk(an, dh_ref[pl.ds(base, t), :], carry, True)
            pbase = pl.multiple_of(jnp.maximum(base - SUBLANES, 0), SUBLANES)
            hp_row = h_ref[pl.ds(pbase, SUBLANES), :][SUBLANES - 1:SUBLANES, :] * (ci > 0).astype(F32)
            hp = _shift_down(h_ref[pl.ds(base, t), :], 1, hp_row)
            da_ref[pl.ds(base, t), :] = mu * hp
            db_ref[pl.ds(base, t), :] = mu
            return carry

        lax.fori_loop(0, n_chunks, chunk, jnp.zeros((1, 128), F32))

    seq = pl.BlockSpec((rows, 128), lambda j: (0, j))
    return pl.pallas_call(
        body, out_shape=(jax.ShapeDtypeStruct((rows, n), F32), jax.ShapeDtypeStruct((rows, n), F32)), grid=(n // 128,),
        in_specs=[seq, seq, seq], out_specs=(seq, seq), compiler_params=_params(("parallel",)), name=name)(dh, a, h)


def _s5_param(lr, li, ls, bre, bim):
    st = jnp.exp(ls)
    er = jnp.exp(lr * st)
    th = li * st
    ar, ai = er * jnp.cos(th), er * jnp.sin(th)
    nr, ni = ar - 1.0, ai
    den = lr * lr + li * li
    cr, ci = (nr * lr + ni * li) / den, (ni * lr - nr * li) / den
    return ar, ai, cr * bre - ci * bim, cr * bim + ci * bre


def s5_param_fwd(name, lr, li, ls, bre, bim):
    gh, n = bre.shape

    def body(lr_ref, li_ref, ls_ref, bre_ref, bim_ref, a_ref, bb_ref):
        ar, ai, br, bi = _s5_param(lr_ref[...], li_ref[...], ls_ref[...], bre_ref[...], bim_ref[...])
        a_ref[0] = ar
        a_ref[1] = ai
        bb_ref[0] = br.astype(BF16)
        bb_ref[1] = bi.astype(BF16)

    return pl.pallas_call(body, out_shape=(jax.ShapeDtypeStruct((2, 1, n), F32), jax.ShapeDtypeStruct((2, gh, n), BF16)),
                          compiler_params=_params(), name=name)(lr, li, ls, bre, bim)


def s5_param_bwd(name, lr, li, ls, bre, bim, da, dbb, gsum):
    gh, n = bre.shape

    def body(lr_ref, li_ref, ls_ref, bre_ref, bim_ref, da_ref, dbb_ref, gs_ref, dlr_ref, dli_ref, dls_ref, dbre_ref, dbim_ref):
        _, vjp = jax.vjp(_s5_param, lr_ref[...], li_ref[...], ls_ref[...], bre_ref[...], bim_ref[...])
        dlr, dli, dls, dbre, dbim = vjp((da_ref[0], da_ref[1], dbb_ref[0], dbb_ref[1]))
        dlr_ref[...] = dlr
        dli_ref[...] = dli
        dls_ref[...] = jnp.dot(jnp.broadcast_to(dls, (SUBLANES, n)), gs_ref[...], preferred_element_type=F32,
                               precision=lax.Precision.HIGHEST)
        dbre_ref[...] = dbre
        dbim_ref[...] = dbim

    vec = jax.ShapeDtypeStruct((1, n), F32)
    mat = jax.ShapeDtypeStruct((gh, n), F32)
    return pl.pallas_call(body, out_shape=(vec, vec, jax.ShapeDtypeStruct((SUBLANES, 128), F32), mat, mat),
                          compiler_params=_params(), name=name)(lr, li, ls, bre, bim, da, dbb, gsum)


def sum_lead(name, x, tr):
    n, rows, cols = x.shape

    def body(x_ref, o_ref):
        acc = x_ref[0]
        for j in range(1, n):
            acc = acc + x_ref[j]
        o_ref[...] = acc

    return pl.pallas_call(
        body, out_shape=jax.ShapeDtypeStruct((rows, cols), x.dtype), grid=(rows // tr,),
        in_specs=[pl.BlockSpec((n, tr, cols), lambda i: (0, i, 0))], out_specs=pl.BlockSpec((tr, cols), lambda i: (i, 0)),
        compiler_params=_params(("parallel",)), name=name)(x)


def _adamw(w, g, m, v):
    m = ADAM_B1 * m + (1.0 - ADAM_B1) * g
    v = ADAM_B2 * v + (1.0 - ADAM_B2) * jnp.square(g)
    m_hat = m / (1.0 - ADAM_B1 ** ADAM_STEP)
    v_hat = v / (1.0 - ADAM_B2 ** ADAM_STEP)
    delta = -ADAM_LR * (m_hat / (jnp.sqrt(v_hat) + ADAM_EPS) + ADAM_WD * w)
    return delta, m, v


def adamw_sharded(name, w, m, v, g0, g1, split_cols, tile):
    _, r, c = w.shape
    if split_cols:
        nt = c // tile
        per = (c // 2) // tile
        block = (None, r, tile)
        wspec = pl.BlockSpec(block, lambda l, t: (l, 0, t))
        gidx = lambda t: (t // per, 0, t % per)
    else:
        nt = r // tile
        per = (r // 2) // tile
        block = (None, tile, c)
        wspec = pl.BlockSpec(block, lambda l, t: (l, t, 0))
        gidx = lambda t: (t // per, t % per, 0)

    def gspec(layer):
        return pl.BlockSpec(block, lambda l, t: gidx(jnp.where(l == layer, t, (nt - 1) * (1 - layer))))

    def body(w_ref, m_ref, v_ref, g0_ref, g1_ref, g_ref, d_ref, nm_ref, nv_ref):
        g = jnp.where(pl.program_id(0) == 0, g0_ref[...], g1_ref[...])
        d, nm, nv = _adamw(w_ref[...], g, m_ref[...], v_ref[...])
        g_ref[...] = g
        d_ref[...] = d
        nm_ref[...] = nm
        nv_ref[...] = nv

    sds = jax.ShapeDtypeStruct(w.shape, F32)
    return pl.pallas_call(body, out_shape=(sds,) * 4, grid=(2, nt), in_specs=[wspec, wspec, wspec, gspec(0), gspec(1)],
                          out_specs=(wspec,) * 4, compiler_params=_params(("arbitrary", "arbitrary")), name=name)(w, m, v, g0, g1)


def adamw_flat(name, w, g, m, v):
    rows, cols = w.shape
    tr = _row_tile(rows, 1024)

    def body(w_ref, g_ref, m_ref, v_ref, d_ref, nm_ref, nv_ref):
        d, nm, nv = _adamw(w_ref[...], g_ref[...], m_ref[...], v_ref[...])
        d_ref[...] = d
        nm_ref[...] = nm
        nv_ref[...] = nv

    blk = pl.BlockSpec((tr, cols), lambda i: (i, 0))
    sds = jax.ShapeDtypeStruct((rows, cols), F32)
    return pl.pallas_call(body, out_shape=(sds,) * 3, grid=(rows // tr,), in_specs=[blk] * 4, out_specs=(blk,) * 3,
                          compiler_params=_params(("parallel",)), name=name)(w, g, m, v)


def _flips(axes):
    out = []
    for fx in ((0, 1) if "x" in axes else (0,)):
        for fy in ((0, 1) if "y" in axes else (0,)):
            for fc in ((0, 1) if "c" in axes else (0,)):
                if fx or fy or fc:
                    out.append((fx, fy, fc))
    return out


def _slot(pos, axes):
    s = 0
    for name, p in zip(("x", "y", "c"), pos):
        if name in axes:
            s = 2 * s + p
    return s


_HBM = pl.BlockSpec(memory_space=pltpu.HBM)
_SEM = pl.BlockSpec(memory_space=pltpu.SEMAPHORE)
_EFFECT = pltpu.SideEffectType.DATAFLOW_SIDE_EFFECTING


def place_own(name, arrs, axes):
    n = len(_flips(axes)) + 1
    na = len(arrs)

    def body(*refs):
        ins, outs, sems = refs[:na], refs[na:2 * na], refs[2 * na]
        my = _slot((lax.axis_index("x"), lax.axis_index("y"), lax.axis_index("c")), axes)
        copies = [pltpu.make_async_copy(ins[a], outs[a].at[my], sems.at[a]) for a in range(na)]
        for cp in copies:
            cp.start()
        for cp in copies:
            cp.wait()

    out_shape = tuple(jax.ShapeDtypeStruct((n,) + a.shape, a.dtype) for a in arrs)
    anyspec = pl.BlockSpec(memory_space=pl.ANY)
    return pl.pallas_call(body, out_shape=out_shape, in_specs=[anyspec] * na, out_specs=(anyspec,) * na,
                          scratch_shapes=[pltpu.SemaphoreType.DMA((na,))], name=name)(*arrs)


def _peers(axes):
    me = (lax.axis_index("x"), lax.axis_index("y"), lax.axis_index("c"))
    return me, [tuple((1 - p) if f else p for p, f in zip(me, fl)) for fl in _flips(axes)]


def place_tile(name, arr, layer, my, slots=4, dtype=BF16, after=None):
    _, r, cols = arr.shape
    tr = _tile_rows(r, cols)

    def body(my_ref, x_ref, *rest):
        rest[-1][...] = x_ref[...].astype(dtype)

    in_specs = [pl.BlockSpec((None, tr, cols), lambda i, my: (layer, i, 0))]
    args = [arr]
    if after is not None:
        in_specs.append(pl.BlockSpec(after.shape, lambda i, my: (0, 0)))
        args.append(after)
    grid_spec = pltpu.PrefetchScalarGridSpec(num_scalar_prefetch=1, grid=(r // tr,), in_specs=in_specs,
                                             out_specs=pl.BlockSpec((None, tr, cols), lambda i, my: (my[0], i, 0)))
    return pl.pallas_call(body, out_shape=jax.ShapeDtypeStruct((slots, r, cols), dtype), grid_spec=grid_spec,
                          compiler_params=_params(("parallel",)), name=name)(my, *args)


def exchange_start(name, groups, axes, scatter):
    flat = [(p if scatter else (p,)) for grp in groups for p in grp]
    per = 2 if scatter else 1
    na, ng, npeer = len(flat), len(groups), len(_flips(axes))

    def body(*refs):
        ops = refs[:per * na]
        zones = ops[(per - 1) * na:]
        sems, token = refs[per * na:per * na + 2 * ng], refs[-1]
        me, peers = _peers(axes)
        my = _slot(me, axes)
        ai = 0
        for g, grp in enumerate(groups):
            for k in range(len(grp)):
                for j, peer in enumerate(peers):
                    src = ops[ai].at[_slot(peer, axes)] if scatter else zones[ai].at[my]
                    dst = zones[ai].at[j] if scatter else zones[ai].at[my]
                    pltpu.make_async_remote_copy(
                        src_ref=src, dst_ref=dst, send_sem=sems[2 * g].at[k * npeer + j],
                        recv_sem=sems[2 * g + 1].at[k * npeer + j], device_id=peer, device_id_type=pl.DeviceIdType.MESH).start()
                ai += 1
        token[...] = jnp.zeros_like(token)

    out_shape, out_specs = [], []
    for grp in groups:
        out_shape += [pltpu.SemaphoreType.DMA((npeer * len(grp),))] * 2
        out_specs += [_SEM, _SEM]
    for idx in range(per):
        out_shape += [pltpu.HBM(p[idx].shape, p[idx].dtype) for p in flat]
        out_specs += [_HBM] * na
    out_shape.append(jax.ShapeDtypeStruct((SUBLANES, 128), F32))
    out_specs.append(pl.BlockSpec(memory_space=pltpu.VMEM))
    args = [pltpu.with_memory_space_constraint(p[idx], pltpu.HBM) for idx in range(per) for p in flat]
    res = pl.pallas_call(body, out_shape=tuple(out_shape), in_specs=[_HBM] * (per * na), out_specs=tuple(out_specs),
                         input_output_aliases={i: 2 * ng + i for i in range(per * na)},
                         compiler_params=pltpu.CompilerParams(has_side_effects=_EFFECT), name=name)(*args)
    thru = res[2 * ng:2 * ng + per * na]
    out, ai = [], 0
    for g, grp in enumerate(groups):
        srcs = list(thru[ai:ai + len(grp)]) if scatter else []
        zones = list(thru[(per - 1) * na + ai:(per - 1) * na + ai + len(grp)])
        out.append(((res[2 * g], res[2 * g + 1]), srcs, zones))
        ai += len(grp)
    return out, res[-1]


def exchange_wait(name, group, after, axes, scatter):
    (send_sems, recv_sems), srcs, zones = group
    n, ns = len(zones), len(srcs)
    npeer = len(_flips(axes))

    def body(*refs):
        z_refs = refs[ns:ns + n]
        ssem, rsem = refs[ns + n], refs[ns + n + 1]
        _, peers = _peers(axes)
        for k in range(n):
            for j, peer in enumerate(peers):
                part = z_refs[k].at[j if scatter else _slot(peer, axes)]
                copy = pltpu.make_async_remote_copy(
                    src_ref=part, dst_ref=part, send_sem=ssem.at[k * npeer + j], recv_sem=rsem.at[k * npeer + j],
                    device_id=peer, device_id_type=pl.DeviceIdType.MESH)
                copy.wait_send()
                copy.wait_recv()

    ops = list(srcs) + list(zones)
    out_shape = tuple(pltpu.HBM(a.shape, a.dtype) for a in ops)
    res = pl.pallas_call(body, out_shape=out_shape, in_specs=[_HBM] * len(ops) + [_SEM, _SEM, pl.BlockSpec(memory_space=pl.ANY)],
                         out_specs=(_HBM,) * len(ops), input_output_aliases={i: i for i in range(len(ops))},
                         compiler_params=pltpu.CompilerParams(has_side_effects=_EFFECT), name=name)(*ops, send_sems, recv_sems, after)
    return list(res[:ns]), list(res[ns:])


def _pair_exchange(name, ins, in_specs, n_steps, tile, fn_send, fn_out, out_shape, out_spec, prefetch=None, wire=F32):
    n_in = len(ins)

    def body(*refs):
        if prefetch is not None:
            refs = refs[1:]
        in_refs, o_ref = refs[:n_in], refs[n_in]
        send_buf, recv_buf, send_sems, recv_sems, credit = refs[n_in + 1:]
        i = pl.program_id(0)
        slot = lax.rem(i, 2)
        c = lax.axis_index("c")
        sibling = (lax.axis_index("x"), lax.axis_index("y"), 1 - c)
        vals = [r[...] for r in in_refs]
        send_buf[slot] = fn_send(*vals, c).astype(wire)

        @pl.when(i >= 2)
        def _():
            pl.semaphore_wait(credit, 1)

        copy = pltpu.make_async_remote_copy(
            src_ref=send_buf.at[slot], dst_ref=recv_buf.at[slot], send_sem=send_sems.at[slot], recv_sem=recv_sems.at[slot],
            device_id=sibling, device_id_type=pl.DeviceIdType.MESH)
        copy.start()
        copy.wait_recv()
        o_ref[...] = fn_out(*vals, recv_buf[slot], c).astype(o_ref.dtype)
        copy.wait_send()

        @pl.when(i < n_steps - 2)
        def _():
            pl.semaphore_signal(credit, inc=1, device_id=sibling, device_id_type=pl.DeviceIdType.MESH)

    scratch = [pltpu.VMEM((2,) + tile, wire), pltpu.VMEM((2,) + tile, wire), pltpu.SemaphoreType.DMA((2,)),
               pltpu.SemaphoreType.DMA((2,)), pltpu.SemaphoreType.REGULAR]
    if prefetch is None:
        return pl.pallas_call(body, out_shape=out_shape, grid=(n_steps,), in_specs=in_specs, out_specs=out_spec,
                              scratch_shapes=scratch, compiler_params=_params(("arbitrary",)), name=name)(*ins)
    grid_spec = pltpu.PrefetchScalarGridSpec(num_scalar_prefetch=1, grid=(n_steps,), in_specs=in_specs, out_specs=out_spec,
                                             scratch_shapes=scratch)
    return pl.pallas_call(body, out_shape=out_shape, grid_spec=grid_spec, compiler_params=_params(("arbitrary",)),
                          name=name)(prefetch, *ins)


def _tile_rows(rows, cols, f32_bytes=3 << 19):
    return _row_tile(rows, max(2 * SUBLANES, f32_bytes // (4 * cols)), 2 * SUBLANES)


def pair_sum(name, x):
    rows, cols = x.shape
    tr = _tile_rows(rows, cols)
    return _pair_exchange(name, [x], [pl.BlockSpec((tr, cols), lambda i: (i, 0))], rows // tr, (tr, cols),
                          lambda v, c: v, lambda v, got, c: v + got, jax.ShapeDtypeStruct((rows, cols), F32),
                          pl.BlockSpec((tr, cols), lambda i: (i, 0)))


def reduce_cores(name, g):
    _, m, cols = g.shape
    tr = _tile_rows(m, cols, 6 << 20)

    def fn_send(g0, g1, c):
        return jnp.where(c == 0, g1, g0)

    def fn_out(g0, g1, got, c):
        return jnp.where(c == 0, g0, g1) + got.astype(F32)

    return _pair_exchange(
        name, [g, g], [pl.BlockSpec((None, tr, cols), lambda i: (0, i, 0)), pl.BlockSpec((None, tr, cols), lambda i: (1, i, 0))],
        m // tr, (tr, cols), fn_send, fn_out, jax.ShapeDtypeStruct((m, cols), BF16), pl.BlockSpec((tr, cols), lambda i: (i, 0)),
        wire=BF16)


def sum_and_share(name, own, parts, my):
    n, r, cols = parts.shape
    tr = _tile_rows(r, cols, 3 << 20)

    def total(o, p):
        acc = o.astype(F32)
        for j in range(n):
            acc = acc + p[j].astype(F32)
        return acc

    def fn_send(o, p, c):
        return total(o, p)

    def fn_out(o, p, got, c):
        mine = total(o, p)
        return jnp.stack([jnp.where(c == 0, mine, got), jnp.where(c == 0, got, mine)])

    return _pair_exchange(
        name, [own, parts], [pl.BlockSpec((None, tr, cols), lambda i, my_ref: (my_ref[0], i, 0)), pl.BlockSpec((n, tr, cols), lambda i, my_ref: (0, i, 0))],
        r // tr, (tr, cols), fn_send, fn_out, jax.ShapeDtypeStruct((2, r, cols), F32),
        pl.BlockSpec((2, tr, cols), lambda i, my_ref: (0, i, 0)), prefetch=my)


def _block_diag(blocks):
    g, r, c = blocks.shape
    eye = jnp.eye(g, dtype=blocks.dtype)
    return (blocks[:, :, None, :] * eye[:, None, :, None]).reshape(g * r, g * c)


def _diag_blocks(mat, g):
    r, c = mat.shape[0] // g, mat.shape[1] // g
    eye = jnp.eye(g, dtype=mat.dtype)
    return (mat.reshape(g, r, g, c) * eye[:, None, :, None]).sum(axis=2)


def _halves(gfull, shards):
    rows, cols = gfull.shape
    return gfull.reshape(shards, 2, rows // shards // 2, cols).transpose(1, 0, 2, 3)


def _step(inp):
    x = inp['x'][0]
    target = inp['loss_target'][0]
    rows, d = x.shape
    depth = inp['w_in'].shape[0]
    mix_w = d // 4
    n_state = S5_GROUPS * S5_STATE
    ffn_half = inp['ffn_w_up'].shape[2]
    tm = min(512, rows)
    tc = min(256, rows)
    xy = ("x", "y")

    my_chip = (2 * lax.axis_index("x") + lax.axis_index("y")).astype(jnp.int32).reshape(1)
    small_keys = [(nme, None) for nme in SMALL_SHARDED]
    group_keys = []
    for l in range(depth):
        group_keys += [[('w_in', l)] + (small_keys if l == 0 else []),
                       [('w_out', l), ('s5_w_glu', l), ('cv_w_pw', l)], [('ffn_w_up', l)], [('ffn_w_down', l)]]

    def zone_of(key, after=None):
        nme, l = key
        src = jnp.swapaxes(inp[nme], 1, 2) if nme == 'ffn_w_up' else inp[nme]
        return place_tile(f"place_{nme}{l}", src, l, my_chip, after=after)

    zones = {('w_in', 0): zone_of(('w_in', 0))}
    zones.update(zip(small_keys, place_own("place_small", [inp[nme] for nme in SMALL_SHARDED], xy)))
    first_group, first_token = exchange_start("gather_start_first", [[zones[key] for key in group_keys[0]]], xy, False)
    for grp in group_keys[1:]:
        zones.update({key: zone_of(key, first_token) for key in grp})
    rest_groups, gather_token = exchange_start("gather_start", [[zones[key] for key in grp] for grp in group_keys[1:]], xy, False)
    gather_groups = first_group + rest_groups

    def gathered(gi, after):
        return dict(zip(group_keys[gi], exchange_wait(f"gather_wait{gi}", gather_groups[gi], after, xy, False)[1]))

    def full_small(g):
        return g.transpose(1, 2, 0, 3).reshape(g.shape[1], g.shape[2], 4 * g.shape[3])

    gsum = jnp.repeat(jnp.eye(128, dtype=F32)[:S5_GROUPS], S5_STATE, axis=0)

    saved = []
    grads = {nme: [None] * depth for nme in WEIGHTS}
    xcur = x
    for l in range(depth):
        vec = lambda a: a[l].reshape(1, -1)
        gain = vec(inp['norm_mix_g']) + (gather_token[0, 0] if l == 0 else 0.0)
        h = rms_fwd(f"rms_mix{l}", xcur, gain, tm)
        got = gathered(4 * l, h)
        w_in = got[('w_in', l)]
        if l == 0:
            cv_w_dw, lru_w_conv, ffn_w_dw = (full_small(got[(nme, None)]) for nme in ('cv_w_dw', 'lru_w_conv', 'ffn_w_dw'))
        ncol = w_in.shape[2]

        lam_re, lam_im = vec(inp['s5_lam_re']), vec(inp['s5_lam_im'])
        log_step = jnp.broadcast_to(inp['s5_log_step'][l][:, None], (S5_GROUPS, S5_STATE)).reshape(1, n_state)
        b_re = _block_diag(inp['s5_b_re'][l].transpose(0, 2, 1))
        b_im = _block_diag(inp['s5_b_im'][l].transpose(0, 2, 1))
        c_cat = jnp.stack([_block_diag(inp['s5_c_re'][l].transpose(0, 2, 1)),
                           -_block_diag(inp['s5_c_im'][l].transpose(0, 2, 1))]).astype(BF16)
        a_bar, b_bar = s5_param_fwd(f"s5_param_fwd{l}", lam_re, lam_im, log_step, b_re, b_im)
        w_r = _block_diag(inp['lru_w_r'][l]).astype(BF16)
        w_i = _block_diag(inp['lru_w_i'][l]).astype(BF16)
        pool_bd = _block_diag(inp['pool_w'][l]).astype(BF16)
        gate_pars = [w_r, w_i, vec(inp['lru_b_r']), vec(inp['lru_b_i']), vec(inp['lru_lam'])]

        proj = _mm(f"proj{l}", h, w_in, jax.ShapeDtypeStruct((rows, 4 * ncol), F32), (4, rows // tm),
                   pl.BlockSpec((tm, d), lambda j, i: (i, 0)), pl.BlockSpec((None, d, ncol), lambda j, i: (j, 0, 0)),
                   pl.BlockSpec((tm, ncol), lambda j, i: (i, j)), NN)
        proj3 = proj.reshape(1, rows, 4 * ncol)
        ts = min(2048, rows)
        cw, sw = mix_w // 4, n_state // 4
        bu = _mm(f"s5_bu{l}", proj, b_bar, jax.ShapeDtypeStruct((2, rows, n_state), F32), (rows // ts, 2, 4),
                 pl.BlockSpec((ts, cw), lambda i, c, s: (i, s)), pl.BlockSpec((None, cw, sw), lambda i, c, s: (c, s, s)),
                 pl.BlockSpec((None, ts, sw), lambda i, c, s: (c, i, s)), NN)
        z = s5_scan_fwd(f"s5_scan{l}", bu, a_bar)
        y_ssm = _mm(f"s5_read{l}", z, c_cat, jax.ShapeDtypeStruct((rows, mix_w), F32), (rows // ts, 4, 2),
                    pl.BlockSpec((None, ts, sw), lambda i, s, c: (c, i, s)), pl.BlockSpec((None, sw, cw), lambda i, s, c: (c, s, s)),
                    pl.BlockSpec((ts, cw), lambda i, s, c: (i, s)), NN, k_axis=2)
        (h0,) = _rowwise(f"cv_glu{l}", _glu, [(proj, 1, mix_w), (proj, 2, mix_w)], [], 1, [(mix_w, F32)], tm)
        h1 = dwconv_fwd(f"cv_conv{l}", h0.reshape(1, rows, mix_w), 0, mix_w, cv_w_dw[l][None], vec(inp['cv_b_dw'])[None],
                        CV_TAPS, tc)[0]
        xc = dwconv_fwd(f"lru_conv{l}", proj3, 3, mix_w, lru_w_conv[l][None], vec(inp['lru_b_conv'])[None], LRU_TAPS, tc)[0]
        a_t, b_t = _rowwise(f"lru_gate{l}", _lru_gate, [(xc, 0, mix_w)], gate_pars, 2, [(mix_w, F32), (mix_w, F32)], tm)
        hseq = lru_scan_fwd(f"lru_scan{l}", a_t, b_t)
        dgp = pool_fwd(f"pool{l}", proj, 5, tc)
        got = gathered(4 * l + 1, proj)
        w_out = got[('w_out', l)].reshape(d, d)
        w_glu, w_pw = got[('s5_w_glu', l)].reshape(mix_w, mix_w), got[('cv_w_pw', l)].reshape(mix_w, mix_w)
        post_pars = [vec(inp['s5_d']), w_glu, vec(inp['s5_b_glu']), vec(inp['cv_ln_g']), vec(inp['cv_ln_b']), w_pw,
                     vec(inp['cv_b_pw']), pool_bd, vec(inp['pool_scale'])]
        post_rows = [(y_ssm, 0, mix_w), (proj, 0, mix_w), (h1, 0, mix_w), (hseq, 0, mix_w), (proj, 4, mix_w), (dgp, 0, mix_w)]
        (mixed,) = _rowwise(f"mix_post{l}", _mix_post, post_rows, post_pars, 1, [(d, BF16)], tm)
        td = min(1024, rows)
        x1 = _mm(f"out_proj{l}", mixed, w_out, jax.ShapeDtypeStruct((rows, d), F32), (2, rows // td),
                 pl.BlockSpec((td, d), lambda j, i: (i, 0)), pl.BlockSpec((d, d // 2), lambda j, i: (0, j)),
                 pl.BlockSpec((td, d // 2), lambda j, i: (i, j)), NN,
                 add=xcur, add_spec=pl.BlockSpec((td, d // 2), lambda j, i: (i, j)))

        h2 = rms_fwd(f"rms_ffn{l}", x1, vec(inp['norm_ffn_g']), tm)
        tu = min(512, rows)
        w_up = gathered(4 * l + 2, x1)[('ffn_w_up', l)]
        up = _mm(f"ffn_up{l}", h2, w_up, jax.ShapeDtypeStruct((4, rows, ffn_half), F32), (4, rows // tu),
                 pl.BlockSpec((tu, d), lambda k, i: (i, 0)), pl.BlockSpec((None, ffn_half, d), lambda k, i: (k, 0, 0)),
                 pl.BlockSpec((None, tu, ffn_half), lambda k, i: (k, i, 0)), NT)
        w_dw = ffn_w_dw[l].reshape(FFN_TAPS, 2, ffn_half).transpose(1, 0, 2)
        b_dw = inp['ffn_b_dw'][l].reshape(2, 1, ffn_half)
        act = ffn_gate_fwd(f"ffn_gate{l}", up, w_dw, b_dw, tc)
        w_down = gathered(4 * l + 3, up)[('ffn_w_down', l)].reshape(2, ffn_half, d)
        x2 = _mm(f"ffn_down{l}", act, w_down, jax.ShapeDtypeStruct((rows, d), F32), (rows // td, 4),
                 pl.BlockSpec((2, td, ffn_half), lambda i, j: (0, i, 0)), pl.BlockSpec((2, ffn_half, d // 4), lambda i, j: (0, 0, j)),
                 pl.BlockSpec((td, d // 4), lambda i, j: (i, j)), NN, inner=("lead", 2),
                 add=x1, add_spec=pl.BlockSpec((td, d // 4), lambda i, j: (i, j)))
        saved.append(dict(x=xcur, h=h, proj=proj, z=z, y_ssm=y_ssm, h0=h0, h1=h1, xc=xc, a_t=a_t, hseq=hseq, dgp=dgp,
                          mixed=mixed, x1=x1, h2=h2, up=up, act=act, w_in=w_in, w_out=w_out, w_up=w_up, w_down=w_down,
                          a_bar=a_bar, b_bar=b_bar, c_cat=c_cat, post_pars=post_pars, gate_pars=gate_pars, w_dw=w_dw, b_dw=b_dw,
                          s5=(lam_re, lam_im, log_step, b_re, b_im), cv_w=cv_w_dw[l][None], lru_w=lru_w_conv[l][None]))
        xcur = x2

    loss_row, dx, dx_op, dg_final = final_loss("final_loss", xcur, inp['norm_final_g'].reshape(1, d), target, tm)
    grads['norm_final_g'] = dg_final.reshape(d)

    big_g = {nme: [None] * depth for nme in BIG}
    reduce_groups = []

    def start_reduce(tag, keys):
        pieces = []
        for nme, lyr in keys:
            g = big_g[nme][lyr]
            if nme == 'ffn_w_down':
                g = g.reshape(2, 4, ffn_half // 2, d // 2)
            pieces.append(reduce_cores(f"reduce_cores_{nme}{lyr}", g.reshape(2, -1, g.shape[-1])).reshape(g.shape[1:]))
        landing = [lax.empty((3,) + p.shape[1:], p.dtype) for p in pieces]
        groups, token = exchange_start(f"reduce_start_{tag}", [list(zip(pieces, landing))], xy, True)
        reduce_groups.append((tag, keys, groups[0]))
        return token

    for l in reversed(range(depth)):
        s = saved[l]
        ncol = s['w_in'].shape[2]
        tu = min(512, rows)
        dact = _mm(f"d_act{l}", dx_op, s['w_down'], jax.ShapeDtypeStruct((2, rows, ffn_half), F32), (2, rows // tu),
                   pl.BlockSpec((tu, d), lambda k, i: (i, 0)), pl.BlockSpec((None, ffn_half, d), lambda k, i: (k, 0, 0)),
                   pl.BlockSpec((None, tu, ffn_half), lambda k, i: (k, i, 0)), NT)
        tn = d // 4
        tk = min(1024, rows)
        tkb = min(2048, rows)
        big_g['ffn_w_down'][l] = _mm(
            f"dw_down{l}", s['act'], dx_op, jax.ShapeDtypeStruct((2, 2, ffn_half, d // 2), F32), (2, 4, rows // tkb),
            pl.BlockSpec((None, tkb, ffn_half), lambda hh, n, k: (hh, k, 0)), pl.BlockSpec((tkb, tn), lambda hh, n, k: (k, n)),
            pl.BlockSpec((None, None, ffn_half, tn), lambda hh, n, k: (n // 2, hh, 0, n % 2)), TN, k_axis=2)
        dup, dw_dw, db_dw = ffn_gate_bwd(f"ffn_gate_bwd{l}", s['up'], dact, s['w_dw'], s['b_dw'], tc)
        grads['ffn_w_dw'][l] = dw_dw.transpose(1, 0, 2).reshape(FFN_TAPS, 2 * ffn_half)
        grads['ffn_b_dw'][l] = db_dw.reshape(2 * ffn_half)
        dup = dup.reshape(4, rows, ffn_half)
        tm2 = min(1024, rows)
        dh2 = _mm(f"d_h2{l}", dup, s['w_up'], jax.ShapeDtypeStruct((rows, d), F32), (rows // tm2, 2, 4),
                  pl.BlockSpec((None, tm2, ffn_half), lambda i, j, k: (k, i, 0)), pl.BlockSpec((None, ffn_half, d // 2), lambda i, j, k: (k, 0, j)),
                  pl.BlockSpec((tm2, d // 2), lambda i, j, k: (i, j)), NN, k_axis=2)
        tmm = d // 4
        big_g['ffn_w_up'][l] = _mm(
            f"dw_up{l}", dup, s['h2'], jax.ShapeDtypeStruct((2, 4, ffn_half, d // 2), F32), (4, 4, rows // tkb),
            pl.BlockSpec((None, tkb, ffn_half), lambda k4, n, k: (k4, k, 0)), pl.BlockSpec((tkb, tn), lambda k4, n, k: (k, n)),
            pl.BlockSpec((None, None, ffn_half, tn), lambda k4, n, k: (n // 2, k4, 0, n % 2)), TN, k_axis=2)
        token = start_reduce(f"ffn{l}", [('ffn_w_down', l), ('ffn_w_up', l)])
        dx1, dx1_op, dg = rms_bwd(f"rms_ffn_bwd{l}", s['x1'], inp['norm_ffn_g'][l].reshape(1, d) + token[0, 0], dh2, dx, tm)
        grads['norm_ffn_g'][l] = dg.reshape(d)
        dmixed = _mm(f"d_mixed{l}", dx1_op, s['w_out'], jax.ShapeDtypeStruct((rows, d), F32), (rows // tm2, 4),
                     pl.BlockSpec((tm2, d), lambda i, j: (i, 0)), pl.BlockSpec((d // 4, d), lambda i, j: (j, 0)),
                     pl.BlockSpec((tm2, d // 4), lambda i, j: (i, j)), NT)
        tq = mix_w // 2
        big_g['w_out'][l] = _mm(
            f"dw_out{l}", s['mixed'], dx1_op, jax.ShapeDtypeStruct((2, 4, tq, d), F32), (4, rows // tkb),
            pl.BlockSpec((tkb, 2 * tq), lambda t, k: (k, t)), pl.BlockSpec((tkb, d), lambda t, k: (k, 0)),
            pl.BlockSpec((2, None, tq, d), lambda t, k: (0, t, 0, 0)), TN, k_axis=1)
        post_rows = [(s['y_ssm'], 0, mix_w), (s['proj'], 0, mix_w), (s['h1'], 0, mix_w), (s['hseq'], 0, mix_w),
                     (s['proj'], 4, mix_w), (s['dgp'], 0, mix_w), (dmixed, 0, d)]
        res = _rowwise(f"mix_post_bwd{l}", _mix_post, post_rows, s['post_pars'], 1, [(mix_w, F32)] * 6, tm, with_grads=True)
        dy_ssm, du_dir, dh1, dhseq, dlru_g, ddgp = res[:6]
        dd, dwglu, dbglu, dlng, dlnb, dwpw, dbpw, dpoolbd, dscale = res[6:]
        grads['s5_d'][l], grads['s5_b_glu'][l] = dd.reshape(mix_w), dbglu.reshape(mix_w)
        grads['cv_ln_g'][l], grads['cv_ln_b'][l], grads['cv_b_pw'][l] = dlng.reshape(mix_w), dlnb.reshape(mix_w), dbpw.reshape(mix_w)
        grads['pool_w'][l] = _diag_blocks(dpoolbd, len(POOL_WINDOWS))
        grads['pool_scale'][l] = dscale.reshape(mix_w)
        big_g['s5_w_glu'][l] = _halves(dwglu, 4)
        big_g['cv_w_pw'][l] = _halves(dwpw, 4)
        ts = min(2048, rows)
        cw, sw = mix_w // 4, n_state // 4
        slab = jnp.arange(mix_w)[:, None] // cw == jnp.arange(n_state)[None, :] // sw
        dz = _mm(f"s5_dz{l}", dy_ssm, s['c_cat'], jax.ShapeDtypeStruct((2, rows, n_state), F32), (rows // ts, 2, 4),
                 pl.BlockSpec((ts, cw), lambda i, c, q: (i, q)), pl.BlockSpec((None, sw, cw), lambda i, c, q: (c, q, q)),
                 pl.BlockSpec((None, ts, sw), lambda i, c, q: (c, i, q)), NT)
        dccat = _mm(f"s5_dc{l}", s['z'], dy_ssm, jax.ShapeDtypeStruct((2, n_state, mix_w), F32), (2, 4, rows // tk),
                    pl.BlockSpec((None, tk, sw), lambda c, q, k: (c, k, q)), pl.BlockSpec((tk, cw), lambda c, q, k: (k, q)),
                    pl.BlockSpec((None, sw, cw), lambda c, q, k: (c, q, q)), TN, k_axis=2)
        dccat = jnp.where(slab.T, dccat, 0.0)
        grads['s5_c_re'][l] = _diag_blocks(dccat[0], S5_GROUPS).transpose(0, 2, 1)
        grads['s5_c_im'][l] = -_diag_blocks(dccat[1], S5_GROUPS).transpose(0, 2, 1)
        lam, da_bar = s5_scan_bwd(f"s5_scan_bwd{l}", dz, s['z'], s['a_bar'])
        du = _mm(f"s5_du{l}", lam, s['b_bar'], jax.ShapeDtypeStruct((rows, mix_w), F32), (rows // ts, 4, 2),
                 pl.BlockSpec((None, ts, sw), lambda i, q, c: (c, i, q)), pl.BlockSpec((None, cw, sw), lambda i, q, c: (c, q, q)),
                 pl.BlockSpec((ts, cw), lambda i, q, c: (i, q)), NT, k_axis=2,
                 add=du_dir, add_spec=pl.BlockSpec((ts, cw), lambda i, q, c: (i, q)))
        dbbar = _mm(f"s5_db{l}", s['proj'], lam, jax.ShapeDtypeStruct((2, mix_w, n_state), F32), (2, 4, rows // tk),
                    pl.BlockSpec((tk, cw), lambda c, q, k: (k, q)), pl.BlockSpec((None, tk, sw), lambda c, q, k: (c, k, q)),
                    pl.BlockSpec((None, cw, sw), lambda c, q, k: (c, q, q)), TN, k_axis=2)
        dbbar = jnp.where(slab, dbbar, 0.0)
        dlr, dli, dls, dbre, dbim = s5_param_bwd(f"s5_param_bwd{l}", *s['s5'], da_bar, dbbar, gsum)
        grads['s5_lam_re'][l] = dlr.reshape(S5_GROUPS, S5_STATE)
        grads['s5_lam_im'][l] = dli.reshape(S5_GROUPS, S5_STATE)
        grads['s5_log_step'][l] = dls[0, :S5_GROUPS]
        grads['s5_b_re'][l] = _diag_blocks(dbre, S5_GROUPS).transpose(0, 2, 1)
        grads['s5_b_im'][l] = _diag_blocks(dbim, S5_GROUPS).transpose(0, 2, 1)
        dh0, dw_cv, db_cv = dwconv_bwd(f"cv_conv_bwd{l}", dh1.reshape(1, rows, mix_w), s['h0'].reshape(1, rows, mix_w), 0, mix_w,
                                       s['cv_w'], CV_TAPS, tc)
        grads['cv_w_dw'][l], grads['cv_b_dw'][l] = dw_cv[0], db_cv.reshape(mix_w)
        dv, dgg = _rowwise(f"cv_glu_bwd{l}", _glu, [(s['proj'], 1, mix_w), (s['proj'], 2, mix_w), (dh0[0], 0, mix_w)], [], 1,
                           [(mix_w, F32)] * 2, tm, with_grads=True)
        da_t, db_t = lru_scan_bwd(f"lru_scan_bwd{l}", dhseq, s['a_t'], s['hseq'])
        res = _rowwise(f"lru_gate_bwd{l}", _lru_gate, [(s['xc'], 0, mix_w), (da_t, 0, mix_w), (db_t, 0, mix_w)], s['gate_pars'], 2,
                       [(mix_w, F32)], tm, with_grads=True)
        dxc, dwr, dwi, dbr, dbi, dlam = res
        grads['lru_w_r'][l], grads['lru_w_i'][l] = _diag_blocks(dwr, LRU_HEADS), _diag_blocks(dwi, LRU_HEADS)
        grads['lru_b_r'][l], grads['lru_b_i'][l], grads['lru_lam'][l] = dbr.reshape(mix_w), dbi.reshape(mix_w), dlam.reshape(mix_w)
        dlx, dw_lc, db_lc = dwconv_bwd(f"lru_conv_bwd{l}", dxc.reshape(1, rows, mix_w), s['proj'].reshape(1, rows, 4 * ncol), 3, mix_w,
                                       s['lru_w'], LRU_TAPS, tc)
        grads['lru_w_conv'][l], grads['lru_b_conv'][l] = dw_lc[0], db_lc.reshape(mix_w)
        dpx = pool_bwd(f"pool_bwd{l}", ddgp, tc)
        dproj = jnp.concatenate([du, dv, dgg, dlx[0], dlru_g, dpx], axis=-1)
        dh = _mm(f"d_h{l}", dproj, s['w_in'], jax.ShapeDtypeStruct((rows, d), F32), (rows // tm2, 4),
                 pl.BlockSpec((tm2, 4 * ncol), lambda i, j: (i, 0)), pl.BlockSpec((4, d // 4, ncol), lambda i, j: (0, j, 0)),
                 pl.BlockSpec((tm2, d // 4), lambda i, j: (i, j)), NT, inner=("cols", 4))
        tk2 = min(2048, rows)
        big_g['w_in'][l] = _mm(
            f"dw_in{l}", s['h'], dproj, jax.ShapeDtypeStruct((2, 4, d // 2, ncol), F32), (4, 2, rows // tk2),
            pl.BlockSpec((tk2, d // 2), lambda k4, m, k: (k, m)), pl.BlockSpec((tk2, ncol), lambda k4, m, k: (k, k4)),
            pl.BlockSpec((None, None, d // 2, ncol), lambda k4, m, k: (m, k4, 0, 0)), TN, k_axis=2)
        token = start_reduce(f"mix{l}", [('w_out', l), ('s5_w_glu', l), ('cv_w_pw', l), ('w_in', l)])
        dx, dx_op, dg = rms_bwd(f"rms_mix_bwd{l}", s['x'], inp['norm_mix_g'][l].reshape(1, d) + token[0, 0], dh, dx1, tm)
        grads['norm_mix_g'][l] = dg.reshape(d)

    small = [nme for nme in WEIGHTS if nme not in BIG]
    full_g = {nme: (grads[nme] if nme == 'norm_final_g' else jnp.stack(grads[nme])) for nme in small}
    flat = jnp.concatenate([full_g[nme].reshape(-1) for nme in small])
    packed = jnp.pad(flat, (0, (-flat.shape[0]) % (128 * 64))).reshape(-1, 128)
    chip_sum = pair_sum("small_pair_sum", packed)
    small_zone = place_tile("place_small_grads", chip_sum[None], 0, my_chip, dtype=F32)
    (small_group,), small_token = exchange_start("small_start", [[small_zone]], xy, False)

    t_full = {}
    for tag, keys, group in reduce_groups:
        pieces, parts = exchange_wait(f"reduce_wait_{tag}", group, small_token, xy, True)
        for key, own, got in zip(keys, pieces, parts):
            t_full[key] = sum_and_share(f"share_cores_{key[0]}{key[1]}", own, got, my_chip)

    outs, done_big = {}, []
    tiles = {'w_in': 256, 'w_out': 128, 'ffn_w_up': 128, 'ffn_w_down': 256, 's5_w_glu': 64, 'cv_w_pw': 64}
    for nme in BIG:
        g0, g1 = t_full[(nme, 0)], t_full[(nme, 1)]
        if nme == 'ffn_w_up':
            res = adamw_sharded(f"adamw_{nme}", *(jnp.swapaxes(inp[p + nme], 1, 2) for p in ('', 'm_', 'v_')), g0, g1, True, tiles[nme])
            outs[nme] = tuple(jnp.swapaxes(r, 1, 2) for r in res)
        else:
            res = adamw_sharded(f"adamw_{nme}", inp[nme], inp['m_' + nme], inp['v_' + nme], g0, g1, nme == 'ffn_w_down', tiles[nme])
            outs[nme] = res
        done_big.append(res[1][:1, :1, :1].reshape(1))

    after_big = sum(done_big)
    (g4,) = exchange_wait("small_wait", small_group, after_big, xy, False)[1]
    gsum_small = sum_lead("sum_small", g4, _row_tile(g4.shape[1], 1024)).reshape(-1)
    red, off = {}, 0
    for nme in small:
        g = gsum_small[off:off + full_g[nme].size].reshape(full_g[nme].shape)
        off += full_g[nme].size
        if nme in SMALL_SHARDED:
            width = inp[nme].shape[2]
            g = lax.dynamic_slice_in_dim(g, my_chip[0] * width, width, axis=2)
        red[nme] = g

    def pack(tree):
        f = jnp.concatenate([tree[nme].reshape(-1) for nme in small])
        return jnp.pad(f, (0, (-f.shape[0]) % (128 * 64))).reshape(-1, 128)

    pd, pm, pv = adamw_flat("adamw_small", pack({n_: inp[n_] for n_ in small}), pack(red), pack({n_: inp['m_' + n_] for n_ in small}),
                            pack({n_: inp['v_' + n_] for n_ in small}))
    off = 0
    for nme in small:
        size, shape = inp[nme].size, inp[nme].shape
        outs[nme] = (red[nme],) + tuple(p.reshape(-1)[off:off + size].reshape(shape) for p in (pd, pm, pv))
        off += size

    loss = lax.psum(loss_row[0, 0], ("x", "y", "c"))
    result = [loss, dx[None]]
    for part in range(4):
        result += [outs[nme][part] for nme in WEIGHTS]
    return tuple(result)


def kernel(x, norm_mix_g, w_in, s5_lam_re, s5_lam_im, s5_log_step, s5_b_re, s5_b_im, s5_c_re, s5_c_im, s5_d, s5_w_glu, s5_b_glu, cv_w_dw, cv_b_dw, cv_ln_g, cv_ln_b, cv_w_pw, cv_b_pw, lru_w_conv, lru_b_conv, lru_w_r, lru_b_r, lru_w_i, lru_b_i, lru_lam, pool_w, pool_scale, w_out, norm_ffn_g, ffn_w_up, ffn_w_dw, ffn_b_dw, ffn_w_down, norm_final_g, loss_target, m_norm_mix_g, m_w_in, m_s5_lam_re, m_s5_lam_im, m_s5_log_step, m_s5_b_re, m_s5_b_im, m_s5_c_re, m_s5_c_im, m_s5_d, m_s5_w_glu, m_s5_b_glu, m_cv_w_dw, m_cv_b_dw, m_cv_ln_g, m_cv_ln_b, m_cv_w_pw, m_cv_b_pw, m_lru_w_conv, m_lru_b_conv, m_lru_w_r, m_lru_b_r, m_lru_w_i, m_lru_b_i, m_lru_lam, m_pool_w, m_pool_scale, m_w_out, m_norm_ffn_g, m_ffn_w_up, m_ffn_w_dw, m_ffn_b_dw, m_ffn_w_down, m_norm_final_g, v_norm_mix_g, v_w_in, v_s5_lam_re, v_s5_lam_im, v_s5_log_step, v_s5_b_re, v_s5_b_im, v_s5_c_re, v_s5_c_im, v_s5_d, v_s5_w_glu, v_s5_b_glu, v_cv_w_dw, v_cv_b_dw, v_cv_ln_g, v_cv_ln_b, v_cv_w_pw, v_cv_b_pw, v_lru_w_conv, v_lru_b_conv, v_lru_w_r, v_lru_b_r, v_lru_w_i, v_lru_b_i, v_lru_lam, v_pool_w, v_pool_scale, v_w_out, v_norm_ffn_g, v_ffn_w_up, v_ffn_w_dw, v_ffn_b_dw, v_ffn_w_down, v_norm_final_g):
    inp = dict(locals())
    return _step(inp)
```
